```python
import functools
import jax, jax.numpy as jnp
from jax import lax
import numpy as np

D_MODEL = 1024
BATCH = 4
SEQ = 4096
DEPTH = 2
DEC_BATCH = 128
DEC_SEQ = 1
PAST_LEN = 2048
PAGE_SIZE = 128

MIX_W = D_MODEL
GROUP_W = MIX_W // 4
POOL_W = GROUP_W
POOL_WINDOWS = (2, 4, 8, 16)
POOL_GROUP = POOL_W // len(POOL_WINDOWS)
POOL_KEEP = max(POOL_WINDOWS) - 1
RG_W = GROUP_W
RG_HEADS = 4
RG_BLOCK = RG_W // RG_HEADS
RG_CONV = 4
RG_C = 8.0
HEAD_DIM = 64
N_HEADS = GROUP_W // HEAD_DIM
N_KV = 2
GQA = N_HEADS // N_KV
CMP_BLOCK = 32
CMP_STRIDE = 16
SEL_BLOCK = 64
SEL_TOPK = 16
WINDOW = 512
Q_BLOCK = 128
SC_W = GROUP_W
SC_CONV = 3
N_GROUPS = 4
EXP_PER_GROUP = 8
N_EXPERTS = N_GROUPS * EXP_PER_GROUP
TOP_E = 2
D_EXPERT = 512
MOE_BLOCK = 128
EPS = 1e-6
SPLIT_SIZES = (POOL_W, RG_W, RG_W, N_HEADS * HEAD_DIM, 6 * N_KV * HEAD_DIM, 3 * N_HEADS, 3 * SC_W)
N_IN = sum(SPLIT_SIZES)

kernel_name = 'hymba_pool_rglru_nsa_shortconv_hmoe_step'


def rmsnorm(x, g):
    xf = x.astype(jnp.float32)
    y = xf * lax.rsqrt(jnp.mean(xf * xf, axis=-1, keepdims=True) + EPS)
    return (y * g.astype(jnp.float32)).astype(x.dtype)


def split_cols(a, sizes):
    outs, o = [], 0
    for s in sizes:
        outs.append(a[..., o:o + s])
        o += s
    return outs


def causal_dwconv(u, prev, w, b):
    L = u.shape[1]
    ext = jnp.concatenate([prev.astype(u.dtype), u], axis=1)
    y = lax.conv_general_dilated(ext, w[:, None, :].astype(u.dtype), window_strides=(1,), padding='VALID',
                                 dimension_numbers=('NWC', 'WIO', 'NWC'), feature_group_count=u.shape[-1])
    return y + b.astype(u.dtype), ext[:, L:]


def pool_mixer(u, prev, pos0, w, scale):
    B_, L, C = u.shape
    ext = jnp.concatenate([prev.astype(u.dtype), u], axis=1)
    ef = ext.astype(jnp.float32)
    cs = jnp.concatenate([jnp.zeros((B_, 1, C), jnp.float32), jnp.cumsum(ef, axis=1)], axis=1)
    pos = pos0 + jnp.arange(L)
    means = []
    for g, win in enumerate(POOL_WINDOWS):
        sl = slice(g * POOL_GROUP, (g + 1) * POOL_GROUP)
        tot = cs[:, POOL_KEEP + 1:POOL_KEEP + 1 + L, sl] - cs[:, POOL_KEEP + 1 - win:POOL_KEEP + 1 - win + L, sl]
        cnt = jnp.minimum(win, pos + 1).astype(jnp.float32)
        means.append(tot / cnt[None, :, None])
    d = (jnp.concatenate(means, axis=-1) - ef[:, POOL_KEEP:]).astype(u.dtype)
    y = jnp.einsum('blgc,gcd->blgd', d.reshape(B_, L, len(POOL_WINDOWS), POOL_GROUP), w).reshape(B_, L, C)
    return y * scale, ext[:, L:]


def rglru_mixer(xb, gb, conv_prev, h0, conv_w, conv_b, w_a, b_a, w_x, b_x, lam):
    B_, L, C = xb.shape
    xc, conv_new = causal_dwconv(xb, conv_prev, conv_w, conv_b)
    xh = xc.reshape(B_, L, RG_HEADS, RG_BLOCK)
    r = jax.nn.sigmoid(jnp.einsum('blhi,hij->blhj', xh, w_a).reshape(B_, L, C) + b_a)
    ig = jax.nn.sigmoid(jnp.einsum('blhi,hij->blhj', xh, w_x).reshape(B_, L, C) + b_x)
    log_a = -RG_C * r.astype(jnp.float32) * jax.nn.softplus(-lam.astype(jnp.float32))
    a = jnp.exp(log_a)
    bt = jnp.sqrt(-jnp.expm1(2.0 * log_a)) * (ig * xc).astype(jnp.float32)
    bt = bt.at[:, 0].add(a[:, 0] * h0.astype(jnp.float32))
    _, h = lax.associative_scan(lambda e1, e2: (e1[0] * e2[0], e2[0] * e1[1] + e2[1]), (a, bt), axis=1)
    y = h.astype(xb.dtype) * jax.nn.gelu(gb)
    return y, conv_new, h[:, -1].astype(xb.dtype)


def masked_softmax(s, mask):
    s = jnp.where(mask, s.astype(jnp.float32), -jnp.inf)
    m = jnp.max(s, axis=-1, keepdims=True)
    e = jnp.exp(s - jnp.where(jnp.isfinite(m), m, 0.0))
    d = jnp.sum(e, axis=-1, keepdims=True)
    return e / jnp.where(d > 0, d, 1.0)


def nsa_compress(k_raw, v_raw, phi, phi_b, g_kc):
    B_, T = k_raw.shape[:2]
    R = CMP_BLOCK // CMP_STRIDE
    nch = T // CMP_STRIDE
    ncmp = nch - (R - 1)

    def comp(a, w, bias):
        ch = a[:, :nch * CMP_STRIDE].reshape(B_, nch, CMP_STRIDE, N_KV, HEAD_DIM)
        ch = ch.transpose(0, 1, 3, 2, 4).reshape(B_, nch, N_KV, CMP_STRIDE * HEAD_DIM)
        wr = w.reshape(R, CMP_STRIDE * HEAD_DIM, HEAD_DIM)
        out = jnp.einsum('bckf,fd->bckd', ch[:, 0:ncmp], wr[0])
        for r in range(1, R):
            out = out + jnp.einsum('bckf,fd->bckd', ch[:, r:r + ncmp], wr[r])
        return out + bias

    kc = rmsnorm(comp(k_raw, phi[0], phi_b[0]), g_kc)
    vc = comp(v_raw, phi[1], phi_b[1])
    cmp_end = jnp.arange(ncmp) * CMP_STRIDE + (CMP_BLOCK - 1)
    return kc, vc, cmp_end


def sel_blocks(a):
    B_, T = a.shape[:2]
    n_sel = -(-T // SEL_BLOCK)
    a = jnp.pad(a, ((0, 0), (0, n_sel * SEL_BLOCK - T), (0, 0), (0, 0)))
    return a.reshape(B_, n_sel, SEL_BLOCK, N_KV, HEAD_DIM).transpose(0, 3, 1, 2, 4)


def nsa_attend(q, q_pos, gates, kc, vc, cmp_end, ks_blk, vs_blk, kw, vw, w_pos):
    dt = q.dtype
    B_, Q = q.shape[:2]
    t = q_pos[:, None]
    s = jnp.einsum('bqkgd,bckd->bqkgc', q, kc)
    p_cmp = masked_softmax(s, (cmp_end[None, :] <= t)[None, :, None, None, :])
    o_cmp = jnp.einsum('bqkgc,bckd->bqkgd', p_cmp.astype(dt), vc)
    n_sel = ks_blk.shape[2]
    ci = jnp.arange(kc.shape[1])[:, None] * CMP_STRIDE
    sj = jnp.arange(n_sel)[None, :] * SEL_BLOCK
    overlap = ((ci < sj + SEL_BLOCK) & (ci + CMP_BLOCK > sj)).astype(jnp.float32)
    imp = jnp.einsum('bqkgc,cs->bqks', p_cmp, overlap)
    blk = jnp.arange(n_sel)[None, :]
    cur = t // SEL_BLOCK
    valid = blk <= cur
    forced = (blk == 0) | (blk == cur) | (blk == cur - 1)
    score = jnp.where(valid[None, :, None, :], imp, -jnp.inf)
    score = jnp.where((forced & valid)[None, :, None, :], jnp.inf, score)
    top_v, top_i = lax.top_k(score, min(SEL_TOPK, n_sel))
    kk = top_i.shape[-1]
    bi = jnp.arange(B_)[:, None, None, None]
    hi = jnp.arange(N_KV)[None, None, :, None]
    ks = ks_blk[bi, hi, top_i].reshape(B_, Q, N_KV, kk * SEL_BLOCK, HEAD_DIM)
    vs = vs_blk[bi, hi, top_i].reshape(B_, Q, N_KV, kk * SEL_BLOCK, HEAD_DIM)
    spos = (top_i[..., None] * SEL_BLOCK + jnp.arange(SEL_BLOCK)).reshape(B_, Q, N_KV, kk * SEL_BLOCK)
    smask = (spos <= q_pos[None, :, None, None]) & jnp.repeat(top_v > -jnp.inf, SEL_BLOCK, axis=-1)
    s = jnp.einsum('bqkgd,bqknd->bqkgn', q, ks)
    o_sel = jnp.einsum('bqkgn,bqknd->bqkgd', masked_softmax(s, smask[:, :, :, None, :]).astype(dt), vs)
    wd = t - w_pos[None, :]
    wmask = (w_pos[None, :] >= 0) & (wd >= 0) & (wd <= WINDOW)
    s = jnp.einsum('bqkgd,bnkd->bqkgn', q, kw)
    o_win = jnp.einsum('bqkgn,bnkd->bqkgd', masked_softmax(s, wmask[None, :, None, None, :]).astype(dt), vw)
    return gates[..., 0:1] * o_cmp + gates[..., 1:2] * o_sel + gates[..., 2:3] * o_win


def nsa_prompt(q, gates, kc_raw, vc_raw, ksel, vsel, kwin, vwin, phi, phi_b, g_kc):
    B_, S = q.shape[:2]
    kc, vc, cmp_end = nsa_compress(kc_raw, vc_raw, phi, phi_b, g_kc)
    ks_blk, vs_blk = sel_blocks(ksel), sel_blocks(vsel)
    zpad = jnp.zeros((B_, WINDOW, N_KV, HEAD_DIM), kwin.dtype)
    kw_pad = jnp.concatenate([zpad, kwin], axis=1)
    vw_pad = jnp.concatenate([zpad, vwin], axis=1)
    nq = S // Q_BLOCK

    def body(args):
        qc, gc, i = args
        start = i * Q_BLOCK
        kw = lax.dynamic_slice_in_dim(kw_pad, start, WINDOW + Q_BLOCK, axis=1)
        vw = lax.dynamic_slice_in_dim(vw_pad, start, WINDOW + Q_BLOCK, axis=1)
        return nsa_attend(qc, start + jnp.arange(Q_BLOCK), gc, kc, vc, cmp_end, ks_blk, vs_blk,
                          kw, vw, start - WINDOW + jnp.arange(WINDOW + Q_BLOCK))

    qb = q.reshape(B_, nq, Q_BLOCK, N_KV, GQA, HEAD_DIM).swapaxes(0, 1)
    gb = gates.reshape(B_, nq, Q_BLOCK, N_KV, GQA, 3).swapaxes(0, 1)
    o = lax.map(body, (qb, gb, jnp.arange(nq)))
    o = o.swapaxes(0, 1).reshape(B_, S, N_HEADS * HEAD_DIM)
    rows = jnp.stack([kc_raw, vc_raw, ksel, vsel], axis=2)
    win_new = jnp.stack([kwin, vwin], axis=2)[:, S - min(WINDOW, S):]
    return o, rows, win_new


def nsa_sample(pool, page_table, win_buf, q, gates, kc_raw, vc_raw, ksel, vsel, kwin, vwin, phi, phi_b, g_kc):
    B_, L = q.shape[:2]
    past = pool[page_table]
    past = past.reshape(B_, past.shape[1] * past.shape[2], 4, N_KV, HEAD_DIM)
    P = past.shape[1]
    rows = jnp.stack([kc_raw, vc_raw, ksel, vsel], axis=2)
    full = jnp.concatenate([past.astype(rows.dtype), rows], axis=1)
    kc, vc, cmp_end = nsa_compress(full[:, :, 0], full[:, :, 1], phi, phi_b, g_kc)
    ks_blk, vs_blk = sel_blocks(full[:, :, 2]), sel_blocks(full[:, :, 3])
    Lw = win_buf.shape[1]
    new_w = jnp.stack([kwin, vwin], axis=2)
    wfull = jnp.concatenate([win_buf.astype(new_w.dtype), new_w], axis=1)
    o = nsa_attend(q, P + jnp.arange(L), gates, kc, vc, cmp_end, ks_blk, vs_blk,
                   wfull[:, :, 0], wfull[:, :, 1], P - Lw + jnp.arange(Lw + L))
    return o.reshape(B_, L, N_HEADS * HEAD_DIM), rows, wfull[:, L:]


def expert_dispatch(xt, eidx, gate, w_gu, w_down):
    T, D = xt.shape
    M = T * TOP_E
    fe = eidx.reshape(M)
    ftok = jnp.arange(M, dtype=jnp.int32) // TOP_E
    fgate = gate.reshape(M)
    order = jnp.argsort(fe)
    se, stok, sgate = fe[order], ftok[order], fgate[order]
    counts = jnp.bincount(fe, length=N_EXPERTS)
    padded = (counts + MOE_BLOCK - 1) // MOE_BLOCK * MOE_BLOCK
    pad_end = jnp.cumsum(padded)
    pad_start = pad_end - padded
    start = jnp.cumsum(counts) - counts
    dest = pad_start[se] + jnp.arange(M) - start[se]
    n_blk = -(-M // MOE_BLOCK) + N_EXPERTS
    P = n_blk * MOE_BLOCK
    buf_tok = jnp.zeros((P,), jnp.int32).at[dest].set(stok)
    buf_gate = jnp.zeros((P,), fgate.dtype).at[dest].set(sgate)
    blk_exp = jnp.minimum(jnp.searchsorted(pad_end, jnp.arange(n_blk) * MOE_BLOCK, side='right'), N_EXPERTS - 1)
    xb = xt[buf_tok].reshape(n_blk, MOE_BLOCK, D)

    def run(args):
        xi, e = args
        a, b = jnp.split(xi @ w_gu[e], 2, axis=-1)
        return (jax.nn.silu(a) * b) @ w_down[e]

    yb = lax.map(run, (xb, blk_exp)).reshape(P, D)
    return jax.ops.segment_sum(yb * buf_gate[:, None].astype(yb.dtype), buf_tok, num_segments=T)


def moe_ffn(x, wg_r, bg_r, we_r, be_r, w_gu, w_down):
    B_, L, D = x.shape
    xt = x.reshape(B_ * L, D)
    T = xt.shape[0]
    lg = (xt @ wg_r + bg_r).astype(jnp.float32)
    pg = jax.nn.softmax(lg, axis=-1)
    gsel = jnp.argmax(lg, axis=-1)
    p_group = jnp.take_along_axis(pg, gsel[:, None], axis=-1)
    le = (xt @ we_r + be_r).astype(jnp.float32).reshape(T, N_GROUPS, EXP_PER_GROUP)
    le_g = jnp.take_along_axis(le, gsel[:, None, None], axis=1)[:, 0]
    tv, ti = lax.top_k(le_g, TOP_E)
    gate = p_group * jax.nn.softmax(tv, axis=-1)
    eidx = gsel[:, None] * EXP_PER_GROUP + ti
    return expert_dispatch(xt, eidx, gate, w_gu, w_down).reshape(B_, L, D)


def layer_forward(x, pos0, lw, pool_prev, rgc_prev, rgh0, sc_prev, nsa_fn):
    B_, L, _ = x.shape
    h = rmsnorm(x, lw['norm_mix_g'])
    proj = jnp.einsum('bld,dn->bln', h, lw['w_in'])
    pu, rx, rgate, q, kv, ng, sc = split_cols(proj, SPLIT_SIZES)
    y_pool, pool_new = pool_mixer(pu, pool_prev, pos0, lw['pool_w'], lw['pool_scale'])
    y_rg, rgc_new, rgh_new = rglru_mixer(rx, rgate, rgc_prev, rgh0, lw['rg_conv_w'], lw['rg_conv_b'],
                                         lw['rg_w_a'], lw['rg_b_a'], lw['rg_w_x'], lw['rg_b_x'], lw['rg_lambda'])
    qk_g = lw['nsa_qk_g']
    q = rmsnorm(q.reshape(B_, L, N_KV, GQA, HEAD_DIM), qk_g[0]) * (HEAD_DIM ** -0.5)
    kv = kv.reshape(B_, L, 6, N_KV, HEAD_DIM)
    gates = jax.nn.sigmoid(ng.reshape(B_, L, N_KV, GQA, 3))
    y_nsa, nsa_rows, win_new = nsa_fn(q, gates, kv[:, :, 0], kv[:, :, 1], rmsnorm(kv[:, :, 2], qk_g[2]), kv[:, :, 3],
                                      rmsnorm(kv[:, :, 4], qk_g[3]), kv[:, :, 5], lw['nsa_phi'], lw['nsa_phi_b'], qk_g[1])
    z, bg, cg = split_cols(sc, (SC_W, SC_W, SC_W))
    v, sc_new = causal_dwconv(cg * z, sc_prev, lw['sc_conv_w'], lw['sc_conv_b'])
    y_sc = bg * v
    og = lw['mix_out_g'].reshape(4, GROUP_W)
    y = jnp.concatenate([rmsnorm(yi, og[i]) for i, yi in enumerate((y_pool, y_rg, y_nsa, y_sc))], axis=-1)
    x = x + jnp.einsum('blm,md->bld', y, lw['w_out'])
    x = x + moe_ffn(rmsnorm(x, lw['norm_ffn_g']), lw['router_group_w'], lw['router_group_b'],
                    lw['router_expert_w'], lw['router_expert_b'], lw['exp_w_gu'], lw['exp_w_down'])
    return x, (nsa_rows, win_new, pool_new, rgc_new, rgh_new, sc_new)


def setup_inputs(seed: int = 0) -> dict:
    key = jax.random.key(seed)
    keys = jax.random.split(key, 48)
    ctr = iter(range(48))

    def nrm(shape, scale):
        return jax.random.normal(keys[next(ctr)], shape, jnp.float32) * scale

    n_pages = PAST_LEN // PAGE_SIZE
    n_phys = (DEC_BATCH * n_pages * 5 + 3) // 4
    win_keep = min(WINDOW, PAST_LEN)
    u = jax.random.uniform(keys[next(ctr)], (DEPTH, RG_W), jnp.float32, 0.9, 0.999)
    a0 = u ** (1.0 / RG_C)
    page_table = jax.random.permutation(keys[next(ctr)], n_phys)[:DEC_BATCH * n_pages].reshape(DEC_BATCH, n_pages).astype(jnp.int32)
    return {
        'x_prompt': nrm((BATCH, SEQ, D_MODEL), 1.0),
        'x_sample': nrm((DEC_BATCH, DEC_SEQ, D_MODEL), 1.0),
        'cache_nsa': nrm((DEPTH, n_phys, PAGE_SIZE, 4, N_KV, HEAD_DIM), 1.0),
        'state_win_kv': nrm((DEPTH, DEC_BATCH, win_keep, 2, N_KV, HEAD_DIM), 1.0),
        'state_pool': nrm((DEPTH, DEC_BATCH, POOL_KEEP, POOL_W), 1.0),
        'state_rg_conv': nrm((DEPTH, DEC_BATCH, RG_CONV - 1, RG_W), 1.0),
        'state_rg_h': nrm((DEPTH, DEC_BATCH, RG_W), 0.5),
        'state_sc_conv': nrm((DEPTH, DEC_BATCH, SC_CONV - 1, SC_W), 1.0),
        'page_table': page_table,
        'norm_mix_g': 1.0 + nrm((DEPTH, D_MODEL), 0.02),
        'w_in': nrm((DEPTH, D_MODEL, N_IN), D_MODEL ** -0.5),
        'pool_w': nrm((DEPTH, len(POOL_WINDOWS), POOL_GROUP, POOL_GROUP), POOL_GROUP ** -0.5),
        'pool_scale': 1.0 + nrm((DEPTH, POOL_W), 0.02),
        'rg_conv_w': nrm((DEPTH, RG_CONV, RG_W), RG_CONV ** -0.5),
        'rg_conv_b': nrm((DEPTH, RG_W), 0.02),
        'rg_w_a': nrm((DEPTH, RG_HEADS, RG_BLOCK, RG_BLOCK), RG_BLOCK ** -0.5),
        'rg_b_a': nrm((DEPTH, RG_W), 0.02),
        'rg_w_x': nrm((DEPTH, RG_HEADS, RG_BLOCK, RG_BLOCK), RG_BLOCK ** -0.5),
        'rg_b_x': nrm((DEPTH, RG_W), 0.02),
        'rg_lambda': jnp.log(a0) - jnp.log1p(-a0),
        'nsa_phi': nrm((DEPTH, 2, CMP_BLOCK * HEAD_DIM, HEAD_DIM), (CMP_BLOCK * HEAD_DIM) ** -0.5),
        'nsa_phi_b': nrm((DEPTH, 2, HEAD_DIM), 0.02),
        'nsa_qk_g': 1.0 + nrm((DEPTH, 4, HEAD_DIM), 0.02),
        'sc_conv_w': nrm((DEPTH, SC_CONV, SC_W), SC_CONV ** -0.5),
        'sc_conv_b': nrm((DEPTH, SC_W), 0.02),
        'mix_out_g': 1.0 + nrm((DEPTH, MIX_W), 0.02),
        'w_out': nrm((DEPTH, MIX_W, D_MODEL), MIX_W ** -0.5),
        'norm_ffn_g': 1.0 + nrm((DEPTH, D_MODEL), 0.02),
        'router_group_w': nrm((DEPTH, D_MODEL, N_GROUPS), D_MODEL ** -0.5),
        'router_group_b': nrm((DEPTH, N_GROUPS), 0.01),
        'router_expert_w': nrm((DEPTH, D_MODEL, N_EXPERTS), D_MODEL ** -0.5),
        'router_expert_b': nrm((DEPTH, N_EXPERTS), 0.01),
        'exp_w_gu': nrm((DEPTH, N_EXPERTS, D_MODEL, 2 * D_EXPERT), D_MODEL ** -0.5),
        'exp_w_down': nrm((DEPTH, N_EXPERTS, D_EXPERT, D_MODEL), D_EXPERT ** -0.5),
    }


def reference(x_prompt, x_sample, cache_nsa, state_win_kv, state_pool, state_rg_conv, state_rg_h, state_sc_conv,
              page_table, norm_mix_g, w_in, pool_w, pool_scale, rg_conv_w, rg_conv_b, rg_w_a, rg_b_a, rg_w_x, rg_b_x,
              rg_lambda, nsa_phi, nsa_phi_b, nsa_qk_g, sc_conv_w, sc_conv_b, mix_out_g, w_out, norm_ffn_g,
              router_group_w, router_group_b, router_expert_w, router_expert_b, exp_w_gu, exp_w_down):
    past_len = page_table.shape[1] * cache_nsa.shape[2]
    xp, xs = x_prompt, x_sample
    Bp = xp.shape[0]
    st_p, st_s = [], []
    for l in range(DEPTH):
        lw = dict(norm_mix_g=norm_mix_g[l], w_in=w_in[l], pool_w=pool_w[l], pool_scale=pool_scale[l],
                  rg_conv_w=rg_conv_w[l], rg_conv_b=rg_conv_b[l], rg_w_a=rg_w_a[l], rg_b_a=rg_b_a[l],
                  rg_w_x=rg_w_x[l], rg_b_x=rg_b_x[l], rg_lambda=rg_lambda[l], nsa_phi=nsa_phi[l],
                  nsa_phi_b=nsa_phi_b[l], nsa_qk_g=nsa_qk_g[l], sc_conv_w=sc_conv_w[l], sc_conv_b=sc_conv_b[l],
                  mix_out_g=mix_out_g[l], w_out=w_out[l], norm_ffn_g=norm_ffn_g[l],
                  router_group_w=router_group_w[l], router_group_b=router_group_b[l],
                  router_expert_w=router_expert_w[l], router_expert_b=router_expert_b[l],
                  exp_w_gu=exp_w_gu[l], exp_w_down=exp_w_down[l])
        xp, sp = layer_forward(xp, 0, lw,
                               jnp.zeros((Bp, POOL_KEEP, POOL_W), xp.dtype),
                               jnp.zeros((Bp, RG_CONV - 1, RG_W), xp.dtype),
                               jnp.zeros((Bp, RG_W), xp.dtype),
                               jnp.zeros((Bp, SC_CONV - 1, SC_W), xp.dtype),
                               nsa_prompt)
        xs, ss = layer_forward(xs, past_len, lw, state_pool[l], state_rg_conv[l], state_rg_h[l], state_sc_conv[l],
                               functools.partial(nsa_sample, cache_nsa[l], page_table, state_win_kv[l]))
        st_p.append(sp)
        st_s.append(ss)

    def stk(lst, i):
        return jnp.stack([s[i] for s in lst])

    return (xp, xs, stk(st_p, 0), stk(st_s, 0), stk(st_p, 1), stk(st_s, 1), stk(st_p, 2), stk(st_s, 2),
            stk(st_p, 3), stk(st_s, 3), stk(st_p, 4), stk(st_s, 4), stk(st_p, 5), stk(st_s, 5))
```

```python
import functools
import jax, jax.numpy as jnp
from jax import lax
import numpy as np
from jax.experimental import pallas as pl
from jax.experimental.pallas import tpu as pltpu

D_MODEL = 1024
BATCH = 4
SEQ = 4096
DEPTH = 2
DEC_BATCH = 128
DEC_SEQ = 1
PAST_LEN = 2048
PAGE_SIZE = 128

MIX_W = D_MODEL
GROUP_W = MIX_W // 4
POOL_W = GROUP_W
POOL_WINDOWS = (2, 4, 8, 16)
POOL_GROUP = POOL_W // len(POOL_WINDOWS)
POOL_KEEP = max(POOL_WINDOWS) - 1
RG_W = GROUP_W
RG_HEADS = 4
RG_BLOCK = RG_W // RG_HEADS
RG_CONV = 4
RG_C = 8.0
HEAD_DIM = 64
N_HEADS = GROUP_W // HEAD_DIM
N_KV = 2
GQA = N_HEADS // N_KV
CMP_BLOCK = 32
CMP_STRIDE = 16
SEL_BLOCK = 64
SEL_TOPK = 16
WINDOW = 512
Q_BLOCK = 128
SC_W = GROUP_W
SC_CONV = 3
N_GROUPS = 4
EXP_PER_GROUP = 8
N_EXPERTS = N_GROUPS * EXP_PER_GROUP
TOP_E = 2
D_EXPERT = 512
MOE_BLOCK = 128
EPS = 1e-6
SPLIT_SIZES = (POOL_W, RG_W, RG_W, N_HEADS * HEAD_DIM, 6 * N_KV * HEAD_DIM, 3 * N_HEADS, 3 * SC_W)
N_IN = sum(SPLIT_SIZES)

LANE = 128
N_IN_PAD = -(-N_IN // LANE) * LANE
ROW_TILE = 512


def _norm_matmul_body(x_ref, g_ref, w_ref, o_ref):
    xf = x_ref[...]
    h = xf * lax.rsqrt(jnp.mean(xf * xf, axis=-1, keepdims=True) + EPS) * g_ref[...]
    o_ref[...] = jnp.dot(h.astype(jnp.bfloat16), w_ref[...], preferred_element_type=jnp.float32)


def norm_matmul(x2d, g, w_bf16):
    T, D = x2d.shape
    N = w_bf16.shape[1]
    tm = min(ROW_TILE, T)
    return pl.pallas_call(
        _norm_matmul_body,
        grid=(T // tm,),
        in_specs=[pl.BlockSpec((tm, D), lambda i: (i, 0)),
                  pl.BlockSpec((1, D), lambda i: (0, 0)),
                  pl.BlockSpec((D, N), lambda i: (0, 0))],
        out_specs=pl.BlockSpec((tm, N), lambda i: (i, 0)),
        out_shape=jax.ShapeDtypeStruct((T, N), jnp.float32),
        compiler_params=pltpu.CompilerParams(dimension_semantics=("arbitrary",),
                                             vmem_limit_bytes=48 * 1024 * 1024),
        name="norm_in_proj",
    )(x2d, g.reshape(1, D), w_bf16)


def rmsnorm(x, g):
    xf = x.astype(jnp.float32)
    y = xf * lax.rsqrt(jnp.mean(xf * xf, axis=-1, keepdims=True) + EPS)
    return (y * g.astype(jnp.float32)).astype(x.dtype)


def split_cols(a, sizes):
    outs, o = [], 0
    for s in sizes:
        outs.append(a[..., o:o + s])
        o += s
    return outs


def causal_dwconv(u, prev, w, b):
    L = u.shape[1]
    ext = jnp.concatenate([prev.astype(u.dtype), u], axis=1)
    y = lax.conv_general_dilated(ext, w[:, None, :].astype(u.dtype), window_strides=(1,), padding='VALID',
                                 dimension_numbers=('NWC', 'WIO', 'NWC'), feature_group_count=u.shape[-1])
    return y + b.astype(u.dtype), ext[:, L:]


def pool_mixer(u, prev, pos0, w, scale):
    B_, L, C = u.shape
    ext = jnp.concatenate([prev.astype(u.dtype), u], axis=1)
    ef = ext.astype(jnp.float32)
    cs = jnp.concatenate([jnp.zeros((B_, 1, C), jnp.float32), jnp.cumsum(ef, axis=1)], axis=1)
    pos = pos0 + jnp.arange(L)
    means = []
    for g, win in enumerate(POOL_WINDOWS):
        sl = slice(g * POOL_GROUP, (g + 1) * POOL_GROUP)
        tot = cs[:, POOL_KEEP + 1:POOL_KEEP + 1 + L, sl] - cs[:, POOL_KEEP + 1 - win:POOL_KEEP + 1 - win + L, sl]
        cnt = jnp.minimum(win, pos + 1).astype(jnp.float32)
        means.append(tot / cnt[None, :, None])
    d = (jnp.concatenate(means, axis=-1) - ef[:, POOL_KEEP:]).astype(u.dtype)
    y = jnp.einsum('blgc,gcd->blgd', d.reshape(B_, L, len(POOL_WINDOWS), POOL_GROUP), w).reshape(B_, L, C)
    return y * scale, ext[:, L:]


def rglru_mixer(xb, gb, conv_prev, h0, conv_w, conv_b, w_a, b_a, w_x, b_x, lam):
    B_, L, C = xb.shape
    xc, conv_new = causal_dwconv(xb, conv_prev, conv_w, conv_b)
    xh = xc.reshape(B_, L, RG_HEADS, RG_BLOCK)
    r = jax.nn.sigmoid(jnp.einsum('blhi,hij->blhj', xh, w_a).reshape(B_, L, C) + b_a)
    ig = jax.nn.sigmoid(jnp.einsum('blhi,hij->blhj', xh, w_x).reshape(B_, L, C) + b_x)
    log_a = -RG_C * r.astype(jnp.float32) * jax.nn.softplus(-lam.astype(jnp.float32))
    a = jnp.exp(log_a)
    bt = jnp.sqrt(-jnp.expm1(2.0 * log_a)) * (ig * xc).astype(jnp.float32)
    bt = bt.at[:, 0].add(a[:, 0] * h0.astype(jnp.float32))
    _, h = lax.associative_scan(lambda e1, e2: (e1[0] * e2[0], e2[0] * e1[1] + e2[1]), (a, bt), axis=1)
    y = h.astype(xb.dtype) * jax.nn.gelu(gb)
    return y, conv_new, h[:, -1].astype(xb.dtype)


def masked_softmax(s, mask):
    s = jnp.where(mask, s.astype(jnp.float32), -jnp.inf)
    m = jnp.max(s, axis=-1, keepdims=True)
    e = jnp.exp(s - jnp.where(jnp.isfinite(m), m, 0.0))
    d = jnp.sum(e, axis=-1, keepdims=True)
    return e / jnp.where(d > 0, d, 1.0)


def nsa_compress(k_raw, v_raw, phi, phi_b, g_kc):
    B_, T = k_raw.shape[:2]
    R = CMP_BLOCK // CMP_STRIDE
    nch = T // CMP_STRIDE
    ncmp = nch - (R - 1)

    def comp(a, w, bias):
        ch = a[:, :nch * CMP_STRIDE].reshape(B_, nch, CMP_STRIDE, N_KV, HEAD_DIM)
        ch = ch.transpose(0, 1, 3, 2, 4).reshape(B_, nch, N_KV, CMP_STRIDE * HEAD_DIM)
        wr = w.reshape(R, CMP_STRIDE * HEAD_DIM, HEAD_DIM)
        out = jnp.einsum('bckf,fd->bckd', ch[:, 0:ncmp], wr[0])
        for r in range(1, R):
            out = out + jnp.einsum('bckf,fd->bckd', ch[:, r:r + ncmp], wr[r])
        return out + bias

    kc = rmsnorm(comp(k_raw, phi[0], phi_b[0]), g_kc)
    vc = comp(v_raw, phi[1], phi_b[1])
    cmp_end = jnp.arange(ncmp) * CMP_STRIDE + (CMP_BLOCK - 1)
    return kc, vc, cmp_end


def sel_blocks(a):
    B_, T = a.shape[:2]
    n_sel = -(-T // SEL_BLOCK)
    a = jnp.pad(a, ((0, 0), (0, n_sel * SEL_BLOCK - T), (0, 0), (0, 0)))
    return a.reshape(B_, n_sel, SEL_BLOCK, N_KV, HEAD_DIM).transpose(0, 3, 1, 2, 4)


def nsa_attend(q, q_pos, gates, kc, vc, cmp_end, ks_blk, vs_blk, kw, vw, w_pos):
    dt = q.dtype
    B_, Q = q.shape[:2]
    t = q_pos[:, None]
    s = jnp.einsum('bqkgd,bckd->bqkgc', q, kc)
    p_cmp = masked_softmax(s, (cmp_end[None, :] <= t)[None, :, None, None, :])
    o_cmp = jnp.einsum('bqkgc,bckd->bqkgd', p_cmp.astype(dt), vc)
    n_sel = ks_blk.shape[2]
    ci = jnp.arange(kc.shape[1])[:, None] * CMP_STRIDE
    sj = jnp.arange(n_sel)[None, :] * SEL_BLOCK
    overlap = ((ci < sj + SEL_BLOCK) & (ci + CMP_BLOCK > sj)).astype(jnp.float32)
    imp = jnp.einsum('bqkgc,cs->bqks', p_cmp, overlap)
    blk = jnp.arange(n_sel)[None, :]
    cur = t // SEL_BLOCK
    valid = blk <= cur
    forced = (blk == 0) | (blk == cur) | (blk == cur - 1)
    score = jnp.where(valid[None, :, None, :], imp, -jnp.inf)
    score = jnp.where((forced & valid)[None, :, None, :], jnp.inf, score)
    top_v, top_i = lax.top_k(score, min(SEL_TOPK, n_sel))
    kk = top_i.shape[-1]
    bi = jnp.arange(B_)[:, None, None, None]
    hi = jnp.arange(N_KV)[None, None, :, None]
    ks = ks_blk[bi, hi, top_i].reshape(B_, Q, N_KV, kk * SEL_BLOCK, HEAD_DIM)
    vs = vs_blk[bi, hi, top_i].reshape(B_, Q, N_KV, kk * SEL_BLOCK, HEAD_DIM)
    spos = (top_i[..., None] * SEL_BLOCK + jnp.arange(SEL_BLOCK)).reshape(B_, Q, N_KV, kk * SEL_BLOCK)
    smask = (spos <= q_pos[None, :, None, None]) & jnp.repeat(top_v > -jnp.inf, SEL_BLOCK, axis=-1)
    s = jnp.einsum('bqkgd,bqknd->bqkgn', q, ks)
    o_sel = jnp.einsum('bqkgn,bqknd->bqkgd', masked_softmax(s, smask[:, :, :, None, :]).astype(dt), vs)
    wd = t - w_pos[None, :]
    wmask = (w_pos[None, :] >= 0) & (wd >= 0) & (wd <= WINDOW)
    s = jnp.einsum('bqkgd,bnkd->bqkgn', q, kw)
    o_win = jnp.einsum('bqkgn,bnkd->bqkgd', masked_softmax(s, wmask[None, :, None, None, :]).astype(dt), vw)
    return gates[..., 0:1] * o_cmp + gates[..., 1:2] * o_sel + gates[..., 2:3] * o_win


def nsa_prompt(q, gates, kc_raw, vc_raw, ksel, vsel, kwin, vwin, phi, phi_b, g_kc):
    B_, S = q.shape[:2]
    kc, vc, cmp_end = nsa_compress(kc_raw, vc_raw, phi, phi_b, g_kc)
    ks_blk, vs_blk = sel_blocks(ksel), sel_blocks(vsel)
    zpad = jnp.zeros((B_, WINDOW, N_KV, HEAD_DIM), kwin.dtype)
    kw_pad = jnp.concatenate([zpad, kwin], axis=1)
    vw_pad = jnp.concatenate([zpad, vwin], axis=1)
    nq = S // Q_BLOCK

    def body(args):
        qc, gc, i = args
        start = i * Q_BLOCK
        kw = lax.dynamic_slice_in_dim(kw_pad, start, WINDOW + Q_BLOCK, axis=1)
        vw = lax.dynamic_slice_in_dim(vw_pad, start, WINDOW + Q_BLOCK, axis=1)
        return nsa_attend(qc, start + jnp.arange(Q_BLOCK), gc, kc, vc, cmp_end, ks_blk, vs_blk,
                          kw, vw, start - WINDOW + jnp.arange(WINDOW + Q_BLOCK))

    qb = q.reshape(B_, nq, Q_BLOCK, N_KV, GQA, HEAD_DIM).swapaxes(0, 1)
    gb = gates.reshape(B_, nq, Q_BLOCK, N_KV, GQA, 3).swapaxes(0, 1)
    o = lax.map(body, (qb, gb, jnp.arange(nq)))
    o = o.swapaxes(0, 1).reshape(B_, S, N_HEADS * HEAD_DIM)
    rows = jnp.stack([kc_raw, vc_raw, ksel, vsel], axis=2)
    win_new = jnp.stack([kwin, vwin], axis=2)[:, S - min(WINDOW, S):]
    return o, rows, win_new


def nsa_sample(pool, page_table, win_buf, q, gates, kc_raw, vc_raw, ksel, vsel, kwin, vwin, phi, phi_b, g_kc):
    B_, L = q.shape[:2]
    past = pool[page_table]
    past = past.reshape(B_, past.shape[1] * past.shape[2], 4, N_KV, HEAD_DIM)
    P = past.shape[1]
    rows = jnp.stack([kc_raw, vc_raw, ksel, vsel], axis=2)
    full = jnp.concatenate([past.astype(rows.dtype), rows], axis=1)
    kc, vc, cmp_end = nsa_compress(full[:, :, 0], full[:, :, 1], phi, phi_b, g_kc)
    ks_blk, vs_blk = sel_blocks(full[:, :, 2]), sel_blocks(full[:, :, 3])
    Lw = win_buf.shape[1]
    new_w = jnp.stack([kwin, vwin], axis=2)
    wfull = jnp.concatenate([win_buf.astype(new_w.dtype), new_w], axis=1)
    o = nsa_attend(q, P + jnp.arange(L), gates, kc, vc, cmp_end, ks_blk, vs_blk,
                   wfull[:, :, 0], wfull[:, :, 1], P - Lw + jnp.arange(Lw + L))
    return o.reshape(B_, L, N_HEADS * HEAD_DIM), rows, wfull[:, L:]


def expert_dispatch(xt, eidx, gate, w_gu, w_down):
    T, D = xt.shape
    M = T * TOP_E
    fe = eidx.reshape(M)
    ftok = jnp.arange(M, dtype=jnp.int32) // TOP_E
    fgate = gate.reshape(M)
    order = jnp.argsort(fe)
    se, stok, sgate = fe[order], ftok[order], fgate[order]
    counts = jnp.bincount(fe, length=N_EXPERTS)
    padded = (counts + MOE_BLOCK - 1) // MOE_BLOCK * MOE_BLOCK
    pad_end = jnp.cumsum(padded)
    pad_start = pad_end - padded
    start = jnp.cumsum(counts) - counts
    dest = pad_start[se] + jnp.arange(M) - start[se]
    n_blk = -(-M // MOE_BLOCK) + N_EXPERTS
    P = n_blk * MOE_BLOCK
    buf_tok = jnp.zeros((P,), jnp.int32).at[dest].set(stok)
    buf_gate = jnp.zeros((P,), fgate.dtype).at[dest].set(sgate)
    blk_exp = jnp.minimum(jnp.searchsorted(pad_end, jnp.arange(n_blk) * MOE_BLOCK, side='right'), N_EXPERTS - 1)
    xb = xt[buf_tok].reshape(n_blk, MOE_BLOCK, D)

    def run(args):
        xi, e = args
        a, b = jnp.split(xi @ w_gu[e], 2, axis=-1)
        return (jax.nn.silu(a) * b) @ w_down[e]

    yb = lax.map(run, (xb, blk_exp)).reshape(P, D)
    return jax.ops.segment_sum(yb * buf_gate[:, None].astype(yb.dtype), buf_tok, num_segments=T)


def moe_ffn(x, wg_r, bg_r, we_r, be_r, w_gu, w_down):
    B_, L, D = x.shape
    xt = x.reshape(B_ * L, D)
    T = xt.shape[0]
    lg = (xt @ wg_r + bg_r).astype(jnp.float32)
    pg = jax.nn.softmax(lg, axis=-1)
    gsel = jnp.argmax(lg, axis=-1)
    p_group = jnp.take_along_axis(pg, gsel[:, None], axis=-1)
    le = (xt @ we_r + be_r).astype(jnp.float32).reshape(T, N_GROUPS, EXP_PER_GROUP)
    le_g = jnp.take_along_axis(le, gsel[:, None, None], axis=1)[:, 0]
    tv, ti = lax.top_k(le_g, TOP_E)
    gate = p_group * jax.nn.softmax(tv, axis=-1)
    eidx = gsel[:, None] * EXP_PER_GROUP + ti
    return expert_dispatch(xt, eidx, gate, w_gu, w_down).reshape(B_, L, D)


def layer_forward(x, pos0, lw, pool_prev, rgc_prev, rgh0, sc_prev, nsa_fn):
    B_, L, _ = x.shape
    w_pad = jnp.pad(lw['w_in'], ((0, 0), (0, N_IN_PAD - N_IN))).astype(jnp.bfloat16)
    proj = norm_matmul(x.reshape(B_ * L, D_MODEL), lw['norm_mix_g'], w_pad)[:, :N_IN].reshape(B_, L, N_IN)
    pu, rx, rgate, q, kv, ng, sc = split_cols(proj, SPLIT_SIZES)
    y_pool, pool_new = pool_mixer(pu, pool_prev, pos0, lw['pool_w'], lw['pool_scale'])
    y_rg, rgc_new, rgh_new = rglru_mixer(rx, rgate, rgc_prev, rgh0, lw['rg_conv_w'], lw['rg_conv_b'],
                                         lw['rg_w_a'], lw['rg_b_a'], lw['rg_w_x'], lw['rg_b_x'], lw['rg_lambda'])
    qk_g = lw['nsa_qk_g']
    q = rmsnorm(q.reshape(B_, L, N_KV, GQA, HEAD_DIM), qk_g[0]) * (HEAD_DIM ** -0.5)
    kv = kv.reshape(B_, L, 6, N_KV, HEAD_DIM)
    gates = jax.nn.sigmoid(ng.reshape(B_, L, N_KV, GQA, 3))
    y_nsa, nsa_rows, win_new = nsa_fn(q, gates, kv[:, :, 0], kv[:, :, 1], rmsnorm(kv[:, :, 2], qk_g[2]), kv[:, :, 3],
                                      rmsnorm(kv[:, :, 4], qk_g[3]), kv[:, :, 5], lw['nsa_phi'], lw['nsa_phi_b'], qk_g[1])
    z, bg, cg = split_cols(sc, (SC_W, SC_W, SC_W))
    v, sc_new = causal_dwconv(cg * z, sc_prev, lw['sc_conv_w'], lw['sc_conv_b'])
    y_sc = bg * v
    og = lw['mix_out_g'].reshape(4, GROUP_W)
    y = jnp.concatenate([rmsnorm(yi, og[i]) for i, yi in enumerate((y_pool, y_rg, y_nsa, y_sc))], axis=-1)
    x = x + jnp.einsum('blm,md->bld', y, lw['w_out'])
    x = x + moe_ffn(rmsnorm(x, lw['norm_ffn_g']), lw['router_group_w'], lw['router_group_b'],
                    lw['router_expert_w'], lw['router_expert_b'], lw['exp_w_gu'], lw['exp_w_down'])
    return x, (nsa_rows, win_new, pool_new, rgc_new, rgh_new, sc_new)


def kernel(x_prompt, x_sample, cache_nsa, state_win_kv, state_pool, state_rg_conv, state_rg_h, state_sc_conv,
           page_table, norm_mix_g, w_in, pool_w, pool_scale, rg_conv_w, rg_conv_b, rg_w_a, rg_b_a, rg_w_x, rg_b_x,
           rg_lambda, nsa_phi, nsa_phi_b, nsa_qk_g, sc_conv_w, sc_conv_b, mix_out_g, w_out, norm_ffn_g,
           router_group_w, router_group_b, router_expert_w, router_expert_b, exp_w_gu, exp_w_down):
    past_len = page_table.shape[1] * cache_nsa.shape[2]
    xp, xs = x_prompt, x_sample
    Bp = xp.shape[0]
    st_p, st_s = [], []
    for l in range(DEPTH):
        lw = dict(norm_mix_g=norm_mix_g[l], w_in=w_in[l], pool_w=pool_w[l], pool_scale=pool_scale[l],
                  rg_conv_w=rg_conv_w[l], rg_conv_b=rg_conv_b[l], rg_w_a=rg_w_a[l], rg_b_a=rg_b_a[l],
                  rg_w_x=rg_w_x[l], rg_b_x=rg_b_x[l], rg_lambda=rg_lambda[l], nsa_phi=nsa_phi[l],
                  nsa_phi_b=nsa_phi_b[l], nsa_qk_g=nsa_qk_g[l], sc_conv_w=sc_conv_w[l], sc_conv_b=sc_conv_b[l],
                  mix_out_g=mix_out_g[l], w_out=w_out[l], norm_ffn_g=norm_ffn_g[l],
                  router_group_w=router_group_w[l], router_group_b=router_group_b[l],
                  router_expert_w=router_expert_w[l], router_expert_b=router_expert_b[l],
                  exp_w_gu=exp_w_gu[l], exp_w_down=exp_w_down[l])
        xp, sp = layer_forward(xp, 0, lw,
                               jnp.zeros((Bp, POOL_KEEP, POOL_W), xp.dtype),
                               jnp.zeros((Bp, RG_CONV - 1, RG_W), xp.dtype),
                               jnp.zeros((Bp, RG_W), xp.dtype),
                               jnp.zeros((Bp, SC_CONV - 1, SC_W), xp.dtype),
                               nsa_prompt)
        xs, ss = layer_forward(xs, past_len, lw, state_pool[l], state_rg_conv[l], state_rg_h[l], state_sc_conv[l],
                               functools.partial(nsa_sample, cache_nsa[l], page_table, state_win_kv[l]))
        st_p.append(sp)
        st_s.append(ss)

    def stk(lst, i):
        return jnp.stack([s[i] for s in lst])

    return (xp, xs, stk(st_p, 0), stk(st_s, 0), stk(st_p, 1), stk(st_s, 1), stk(st_p, 2), stk(st_s, 2),
            stk(st_p, 3), stk(st_s, 3), stk(st_p, 4), stk(st_s, 4), stk(st_p, 5), stk(st_s, 5))
```

```python
import functools
import jax, jax.numpy as jnp
from jax import lax
import numpy as np
from jax.experimental import pallas as pl
from jax.experimental.pallas import tpu as pltpu

D_MODEL = 1024
BATCH = 4
SEQ = 4096
DEPTH = 2
DEC_BATCH = 128
DEC_SEQ = 1
PAST_LEN = 2048
PAGE_SIZE = 128

MIX_W = D_MODEL
GROUP_W = MIX_W // 4
POOL_W = GROUP_W
POOL_WINDOWS = (2, 4, 8, 16)
POOL_GROUP = POOL_W // len(POOL_WINDOWS)
POOL_KEEP = max(POOL_WINDOWS) - 1
RG_W = GROUP_W
RG_HEADS = 4
RG_BLOCK = RG_W // RG_HEADS
RG_CONV = 4
RG_C = 8.0
HEAD_DIM = 64
N_HEADS = GROUP_W // HEAD_DIM
N_KV = 2
GQA = N_HEADS // N_KV
CMP_BLOCK = 32
CMP_STRIDE = 16
SEL_BLOCK = 64
SEL_TOPK = 16
WINDOW = 512
Q_BLOCK = 128
SC_W = GROUP_W
SC_CONV = 3
N_GROUPS = 4
EXP_PER_GROUP = 8
N_EXPERTS = N_GROUPS * EXP_PER_GROUP
TOP_E = 2
D_EXPERT = 512
MOE_BLOCK = 128
EPS = 1e-6
SPLIT_SIZES = (POOL_W, RG_W, RG_W, N_HEADS * HEAD_DIM, 6 * N_KV * HEAD_DIM, 3 * N_HEADS, 3 * SC_W)
N_IN = sum(SPLIT_SIZES)

LANE = 128
ROW_TILE = 512
VMEM_LIMIT = 48 * 1024 * 1024
MXU_DTYPE = jnp.bfloat16
F32 = jnp.float32
NEG = -1e30

KV_W = 6 * N_KV * HEAD_DIM
COL_Q = 0
COL_KV = COL_Q + N_HEADS * HEAD_DIM
COL_POOL = COL_KV + KV_W
COL_RX = COL_POOL + POOL_W
COL_RGATE = COL_RX + RG_W
COL_SC = COL_RGATE + RG_W
COL_NG = COL_SC + 3 * SC_W
N_IN_PAD = COL_NG + LANE
SEL_TILE = 256
N_SEL_PROMPT = SEQ // SEL_BLOCK


def _cparams(n_axes=1):
    return pltpu.CompilerParams(dimension_semantics=("arbitrary",) * n_axes, vmem_limit_bytes=VMEM_LIMIT)


def _mm(a, b):
    return jnp.dot(a.astype(MXU_DTYPE), b.astype(MXU_DTYPE), preferred_element_type=F32)


def _mm_nt(a, b):
    return lax.dot_general(a.astype(MXU_DTYPE), b.astype(MXU_DTYPE), (((1,), (1,)), ((), ())),
                           preferred_element_type=F32)


def permute_w_in(w):
    pu, rx, rgate, q, kv, ng, sc = split_cols(w, SPLIT_SIZES)
    pad = jnp.zeros((w.shape[0], LANE - ng.shape[1]), w.dtype)
    return jnp.concatenate([q, kv, pu, rx, rgate, sc, ng, pad], axis=1)


def _norm_matmul_body(x_ref, g_ref, w_ref, o_ref):
    xf = x_ref[...]
    h = xf * lax.rsqrt(jnp.mean(xf * xf, axis=-1, keepdims=True) + EPS) * g_ref[...]
    o_ref[...] = _mm(h, w_ref[...])


def norm_matmul(x2d, g, w):
    T, D = x2d.shape
    N = w.shape[1]
    tm = min(ROW_TILE, T)
    return pl.pallas_call(
        _norm_matmul_body,
        grid=(T // tm,),
        in_specs=[pl.BlockSpec((tm, D), lambda i: (i, 0)),
                  pl.BlockSpec((1, D), lambda i: (0, 0)),
                  pl.BlockSpec((D, N), lambda i: (0, 0))],
        out_specs=pl.BlockSpec((tm, N), lambda i: (i, 0)),
        out_shape=jax.ShapeDtypeStruct((T, N), F32),
        compiler_params=_cparams(),
        name="norm_in_proj",
    )(x2d, g.reshape(1, D), w)


def _seg_rmsnorm(x, g):
    x2 = x * x
    left = lax.broadcasted_iota(jnp.int32, x.shape, 1) < HEAD_DIM
    s_l = jnp.sum(jnp.where(left, x2, 0.0), axis=-1, keepdims=True)
    s_r = jnp.sum(jnp.where(left, 0.0, x2), axis=-1, keepdims=True)
    ms = jnp.where(left, s_l, s_r) * (1.0 / HEAD_DIM)
    return x * lax.rsqrt(ms + EPS) * g


def _nsa_prep_body(qkv_ref, ng_ref, g_ref, perm_ref, qa_ref, kvb_ref, rawb_ref, rows_ref, win_ref, gates_ref):
    g = g_ref[...]
    for hb in range(N_KV):
        qn = _seg_rmsnorm(qkv_ref[:, COL_Q + hb * LANE:COL_Q + (hb + 1) * LANE], g[0:1]) * (HEAD_DIM ** -0.5)
        qa_ref[:, hb * 2 * LANE:(hb + 1) * 2 * LANE] = _mm(qn, perm_ref[hb]).astype(qa_ref.dtype)
    comp = [qkv_ref[:, COL_KV + c * LANE:COL_KV + (c + 1) * LANE] for c in range(6)]
    comp[2] = _seg_rmsnorm(comp[2], g[2:3])
    comp[4] = _seg_rmsnorm(comp[4], g[3:4])
    for c in range(6):
        kvb_ref[:, c * LANE:(c + 1) * LANE] = comp[c].astype(kvb_ref.dtype)
    for c in range(2):
        rawb_ref[:, c * LANE:(c + 1) * LANE] = comp[c].astype(rawb_ref.dtype)
    for c in range(4):
        rows_ref[:, c * LANE:(c + 1) * LANE] = comp[c]
    win_ref[:, 0:LANE] = comp[4]
    win_ref[:, LANE:2 * LANE] = comp[5]
    gates_ref[...] = jax.nn.sigmoid(ng_ref[...])


def _q_place_matrices():
    p = np.zeros((N_KV, LANE, 2 * LANE), np.float32)
    for hb in range(N_KV):
        for gq in range(GQA):
            for d in range(HEAD_DIM):
                p[hb, gq * HEAD_DIM + d, gq * LANE + hb * HEAD_DIM + d] = 1.0
    return jnp.asarray(p, MXU_DTYPE)


def nsa_prep(proj, qk_g):
    T = proj.shape[0]
    tm = min(ROW_TILE, T)
    qkv_w = COL_POOL
    g4 = jnp.tile(qk_g, (1, 2))
    return pl.pallas_call(
        _nsa_prep_body,
        grid=(T // tm,),
        in_specs=[pl.BlockSpec((tm, qkv_w), lambda i: (i, 0)),
                  pl.BlockSpec((tm, LANE), lambda i: (i, COL_NG // LANE)),
                  pl.BlockSpec((4, LANE), lambda i: (0, 0)),
                  pl.BlockSpec((N_KV, LANE, 2 * LANE), lambda i: (0, 0, 0))],
        out_specs=[pl.BlockSpec((tm, 4 * LANE), lambda i: (i, 0)),
                   pl.BlockSpec((tm, 6 * LANE), lambda i: (i, 0)),
                   pl.BlockSpec((tm, 2 * LANE), lambda i: (i, 0)),
                   pl.BlockSpec((tm, 4 * LANE), lambda i: (i, 0)),
                   pl.BlockSpec((tm, 2 * LANE), lambda i: (i, 0)),
                   pl.BlockSpec((tm, LANE), lambda i: (i, 0))],
        out_shape=[jax.ShapeDtypeStruct((T, 4 * LANE), MXU_DTYPE),
                   jax.ShapeDtypeStruct((T, 6 * LANE), MXU_DTYPE),
                   jax.ShapeDtypeStruct((T, 2 * LANE), MXU_DTYPE),
                   jax.ShapeDtypeStruct((T, 4 * LANE), F32),
                   jax.ShapeDtypeStruct((T, 2 * LANE), F32),
                   jax.ShapeDtypeStruct((T, LANE), F32)],
        compiler_params=_cparams(),
        name="nsa_prep",
    )(proj, proj, g4, _q_place_matrices())


def compress_weights(phi):
    R = CMP_BLOCK // CMP_STRIDE
    wr = phi.reshape(2, R, CMP_STRIDE, HEAD_DIM, HEAD_DIM)
    eye = jnp.eye(2, dtype=phi.dtype)
    w = jnp.einsum('crjde,cx,hy->rjchdxye', wr, eye, eye)
    return w.reshape(R, CMP_STRIDE * 2 * LANE, 2 * LANE).astype(MXU_DTYPE)


def _compress_body(x_ref, w_ref, b_ref, g_ref, kc_ref, vc_ref):
    x = x_ref[0]
    nch = x.shape[0]
    a = _mm(x, w_ref[0])
    bm = _mm(x, w_ref[1])
    out = a + pltpu.roll(bm, nch - 1, 0) + b_ref[...]
    kc_ref[0] = _seg_rmsnorm(out[:, 0:LANE], g_ref[...]).astype(kc_ref.dtype)
    vc_ref[0] = out[:, LANE:2 * LANE].astype(vc_ref.dtype)


def nsa_compress_pallas(rawb3, wc, phi_b, g_kc):
    B_, nch, K = rawb3.shape
    bias = jnp.concatenate([jnp.tile(phi_b[0], 2), jnp.tile(phi_b[1], 2)]).reshape(1, 2 * LANE)
    return pl.pallas_call(
        _compress_body,
        grid=(B_,),
        in_specs=[pl.BlockSpec((1, nch, K), lambda b: (b, 0, 0)),
                  pl.BlockSpec(wc.shape, lambda b: (0, 0, 0)),
                  pl.BlockSpec((1, 2 * LANE), lambda b: (0, 0)),
                  pl.BlockSpec((1, LANE), lambda b: (0, 0))],
        out_specs=[pl.BlockSpec((1, nch, LANE), lambda b: (b, 0, 0)),
                   pl.BlockSpec((1, nch, LANE), lambda b: (b, 0, 0))],
        out_shape=[jax.ShapeDtypeStruct((B_, nch, LANE), MXU_DTYPE),
                   jax.ShapeDtypeStruct((B_, nch, LANE), MXU_DTYPE)],
        compiler_params=_cparams(),
        name="nsa_compress",
    )(rawb3, wc, bias, jnp.tile(g_kc, 2).reshape(1, LANE))


def _online_update(carry, s, mask, v):
    m, l, acc = carry
    s = jnp.where(mask, s, NEG)
    m_new = jnp.maximum(m, jnp.max(s, axis=-1, keepdims=True))
    alpha = jnp.exp(m - m_new)
    p = jnp.exp(s - m_new)
    l = alpha * l + jnp.sum(p, axis=-1, keepdims=True)
    acc = alpha * acc + _mm(p, v)
    return m_new, l, acc


def _select_blocks(imp, start):
    n_sel = N_SEL_PROMPT
    sc_t = imp.T[0:n_sel]
    blk = lax.broadcasted_iota(jnp.int32, sc_t.shape, 0)
    cur = (start + lax.broadcasted_iota(jnp.int32, sc_t.shape, 1)) // SEL_BLOCK
    valid = blk <= cur
    forced = (blk == 0) | (blk == cur) | (blk == cur - 1)
    score = jnp.where(valid, sc_t, -jnp.inf)
    score = jnp.where(forced & valid, jnp.inf, score)
    cnt = jnp.zeros(sc_t.shape, F32)
    for i in range(n_sel):
        ri = score[i:i + 1, :]
        beats = (ri > score) | ((ri == score) & (blk > i))
        cnt = cnt + jnp.where(beats, 1.0, 0.0)
    sel_t = jnp.where((cnt < SEL_TOPK) & (score > -jnp.inf), 1.0, 0.0)
    sel_t = jnp.concatenate([sel_t, jnp.zeros((LANE - n_sel, sc_t.shape[1]), F32)], axis=0)
    return sel_t.T


def _nsa_attn_body(qa_ref, gates_ref, kc_ref, vc_ref, kv_ref, ov_ref, e_ref, o_ref):
    i = pl.program_id(1)
    start = i * Q_BLOCK
    Q = Q_BLOCK
    R = GQA * Q
    t_row = start + lax.broadcasted_iota(jnp.int32, (R, 1), 0) % Q
    gates = gates_ref[...]
    lane_q = lax.broadcasted_iota(jnp.int32, (Q, LANE), 1)
    for h in range(N_KV):
        qs = jnp.concatenate([qa_ref[:, (h * GQA + gq) * LANE:(h * GQA + gq + 1) * LANE] for gq in range(GQA)],
                             axis=0)
        kc = kc_ref[0]
        ncmp = kc.shape[0]
        s = _mm_nt(qs, kc)
        cmp_end = lax.broadcasted_iota(jnp.int32, (R, ncmp), 1) * CMP_STRIDE + (CMP_BLOCK - 1)
        s = jnp.where(cmp_end <= t_row, s, -jnp.inf)
        m = jnp.max(s, axis=-1, keepdims=True)
        e = jnp.exp(s - jnp.where(m > -jnp.inf, m, 0.0))
        d = jnp.sum(e, axis=-1, keepdims=True)
        p_cmp = e / jnp.where(d > 0, d, 1.0)
        o_cmp = _mm(p_cmp, vc_ref[0])
        imp = _mm(p_cmp[0:Q], ov_ref[...]) + _mm(p_cmp[Q:R], ov_ref[...])
        sel = _select_blocks(imp, start).astype(MXU_DTYPE)

        def sel_step(j, carry):
            off = pl.multiple_of(j * SEL_TILE, SEL_TILE)
            k = kv_ref[pl.ds(off, SEL_TILE), 2 * LANE:3 * LANE]
            v = kv_ref[pl.ds(off, SEL_TILE), 3 * LANE:4 * LANE]
            sj = _mm_nt(qs, k)
            msel = _mm(sel, e_ref[j])
            msel = jnp.concatenate([msel] * GQA, axis=0)
            kpos = off + lax.broadcasted_iota(jnp.int32, (R, SEL_TILE), 1)
            return _online_update(carry, sj, (msel > 0.5) & (kpos <= t_row), v)

        init = (jnp.full((R, 1), NEG, F32), jnp.zeros((R, 1), F32), jnp.zeros((R, LANE), F32))
        n_tiles = (start + Q + SEL_TILE - 1) // SEL_TILE
        _, l_s, acc_s = lax.fori_loop(0, n_tiles, sel_step, init)
        o_sel = acc_s / l_s

        carry = init
        for kk in range(WINDOW // Q + 1):
            tile = i - kk
            off = pl.multiple_of(jnp.maximum(tile, 0) * Q, Q)
            k = kv_ref[pl.ds(off, Q), 4 * LANE:5 * LANE]
            v = kv_ref[pl.ds(off, Q), 5 * LANE:6 * LANE]
            sj = _mm_nt(qs, k)
            wd = t_row - (tile * Q + lax.broadcasted_iota(jnp.int32, (R, Q), 1))
            carry = _online_update(carry, sj, (tile >= 0) & (wd >= 0) & (wd <= WINDOW), v)
        o_win = carry[2] / carry[1]

        outs = []
        for gq in range(GQA):
            c0 = (h * GQA + gq) * 3
            rs = slice(gq * Q, (gq + 1) * Q)
            og = (gates[:, c0:c0 + 1] * o_cmp[rs] + gates[:, c0 + 1:c0 + 2] * o_sel[rs]
                  + gates[:, c0 + 2:c0 + 3] * o_win[rs])
            outs.append(og if gq == h else pltpu.roll(og, HEAD_DIM, 1))
        o_ref[:, h * LANE:(h + 1) * LANE] = jnp.where(lane_q < HEAD_DIM, outs[0], outs[1])


def _sel_constants(S):
    ncmp_rows = S // CMP_STRIDE
    ci = np.arange(ncmp_rows)[:, None] * CMP_STRIDE
    sj = np.arange(LANE)[None, :] * SEL_BLOCK
    ov = ((ci < sj + SEL_BLOCK) & (ci + CMP_BLOCK > sj) & (np.arange(LANE)[None, :] < S // SEL_BLOCK))
    n_t = S // SEL_TILE
    key_blk = (np.arange(n_t)[:, None, None] * SEL_TILE + np.arange(SEL_TILE)[None, None, :]) // SEL_BLOCK
    e = (np.arange(LANE)[None, :, None] == key_blk)
    return jnp.asarray(ov, MXU_DTYPE), jnp.asarray(e, MXU_DTYPE)


def nsa_attn_prompt(qa, gates, kc, vc, kvb, B_, S):
    nq = S // Q_BLOCK
    nch = S // CMP_STRIDE
    ov, e3 = _sel_constants(S)
    return pl.pallas_call(
        _nsa_attn_body,
        grid=(B_, nq),
        in_specs=[pl.BlockSpec((Q_BLOCK, 4 * LANE), lambda b, i: (b * nq + i, 0)),
                  pl.BlockSpec((Q_BLOCK, LANE), lambda b, i: (b * nq + i, 0)),
                  pl.BlockSpec((1, nch, LANE), lambda b, i: (b, 0, 0)),
                  pl.BlockSpec((1, nch, LANE), lambda b, i: (b, 0, 0)),
                  pl.BlockSpec((S, 6 * LANE), lambda b, i: (b, 0)),
                  pl.BlockSpec(ov.shape, lambda b, i: (0, 0)),
                  pl.BlockSpec(e3.shape, lambda b, i: (0, 0, 0))],
        out_specs=pl.BlockSpec((Q_BLOCK, 2 * LANE), lambda b, i: (b * nq + i, 0)),
        out_shape=jax.ShapeDtypeStruct((B_ * S, N_HEADS * HEAD_DIM), F32),
        compiler_params=_cparams(2),
        name="nsa_attn_prompt",
    )(qa, gates, kc, vc, kvb, ov, e3)


def nsa_prompt_pallas(proj, B_, S, phi, phi_b, qk_g):
    qa, kvb, rawb, rows, win, gates = nsa_prep(proj, qk_g)
    nch = S // CMP_STRIDE
    kc, vc = nsa_compress_pallas(rawb.reshape(B_, nch, CMP_STRIDE * 2 * LANE), compress_weights(phi), phi_b, qk_g[1])
    o = nsa_attn_prompt(qa, gates, kc, vc, kvb, B_, S)
    rows = rows.reshape(B_, S, 4, N_KV, HEAD_DIM)
    win_new = win.reshape(B_, S, 2, N_KV, HEAD_DIM)[:, S - min(WINDOW, S):]
    return o.reshape(B_, S, N_HEADS * HEAD_DIM), rows, win_new


def rmsnorm(x, g):
    xf = x.astype(jnp.float32)
    y = xf * lax.rsqrt(jnp.mean(xf * xf, axis=-1, keepdims=True) + EPS)
    return (y * g.astype(jnp.float32)).astype(x.dtype)


def split_cols(a, sizes):
    outs, o = [], 0
    for s in sizes:
        outs.append(a[..., o:o + s])
        o += s
    return outs


def causal_dwconv(u, prev, w, b):
    L = u.shape[1]
    ext = jnp.concatenate([prev.astype(u.dtype), u], axis=1)
    y = lax.conv_general_dilated(ext, w[:, None, :].astype(u.dtype), window_strides=(1,), padding='VALID',
                                 dimension_numbers=('NWC', 'WIO', 'NWC'), feature_group_count=u.shape[-1])
    return y + b.astype(u.dtype), ext[:, L:]


def pool_mixer(u, prev, pos0, w, scale):
    B_, L, C = u.shape
    ext = jnp.concatenate([prev.astype(u.dtype), u], axis=1)
    ef = ext.astype(jnp.float32)
    cs = jnp.concatenate([jnp.zeros((B_, 1, C), jnp.float32), jnp.cumsum(ef, axis=1)], axis=1)
    pos = pos0 + jnp.arange(L)
    means = []
    for g, win in enumerate(POOL_WINDOWS):
        sl = slice(g * POOL_GROUP, (g + 1) * POOL_GROUP)
        tot = cs[:, POOL_KEEP + 1:POOL_KEEP + 1 + L, sl] - cs[:, POOL_KEEP + 1 - win:POOL_KEEP + 1 - win + L, sl]
        cnt = jnp.minimum(win, pos + 1).astype(jnp.float32)
        means.append(tot / cnt[None, :, None])
    d = (jnp.concatenate(means, axis=-1) - ef[:, POOL_KEEP:]).astype(u.dtype)
    y = jnp.einsum('blgc,gcd->blgd', d.reshape(B_, L, len(POOL_WINDOWS), POOL_GROUP), w).reshape(B_, L, C)
    return y * scale, ext[:, L:]


def rglru_mixer(xb, gb, conv_prev, h0, conv_w, conv_b, w_a, b_a, w_x, b_x, lam):
    B_, L, C = xb.shape
    xc, conv_new = causal_dwconv(xb, conv_prev, conv_w, conv_b)
    xh = xc.reshape(B_, L, RG_HEADS, RG_BLOCK)
    r = jax.nn.sigmoid(jnp.einsum('blhi,hij->blhj', xh, w_a).reshape(B_, L, C) + b_a)
    ig = jax.nn.sigmoid(jnp.einsum('blhi,hij->blhj', xh, w_x).reshape(B_, L, C) + b_x)
    log_a = -RG_C * r.astype(jnp.float32) * jax.nn.softplus(-lam.astype(jnp.float32))
    a = jnp.exp(log_a)
    bt = jnp.sqrt(-jnp.expm1(2.0 * log_a)) * (ig * xc).astype(jnp.float32)
    bt = bt.at[:, 0].add(a[:, 0] * h0.astype(jnp.float32))
    _, h = lax.associative_scan(lambda e1, e2: (e1[0] * e2[0], e2[0] * e1[1] + e2[1]), (a, bt), axis=1)
    y = h.astype(xb.dtype) * jax.nn.gelu(gb)
    return y, conv_new, h[:, -1].astype(xb.dtype)


def masked_softmax(s, mask):
    s = jnp.where(mask, s.astype(jnp.float32), -jnp.inf)
    m = jnp.max(s, axis=-1, keepdims=True)
    e = jnp.exp(s - jnp.where(jnp.isfinite(m), m, 0.0))
    d = jnp.sum(e, axis=-1, keepdims=True)
    return e / jnp.where(d > 0, d, 1.0)


def nsa_compress(k_raw, v_raw, phi, phi_b, g_kc):
    B_, T = k_raw.shape[:2]
    R = CMP_BLOCK // CMP_STRIDE
    nch = T // CMP_STRIDE
    ncmp = nch - (R - 1)

    def comp(a, w, bias):
        ch = a[:, :nch * CMP_STRIDE].reshape(B_, nch, CMP_STRIDE, N_KV, HEAD_DIM)
        ch = ch.transpose(0, 1, 3, 2, 4).reshape(B_, nch, N_KV, CMP_STRIDE * HEAD_DIM)
        wr = w.reshape(R, CMP_STRIDE * HEAD_DIM, HEAD_DIM)
        out = jnp.einsum('bckf,fd->bckd', ch[:, 0:ncmp], wr[0])
        for r in range(1, R):
            out = out + jnp.einsum('bckf,fd->bckd', ch[:, r:r + ncmp], wr[r])
        return out + bias

    kc = rmsnorm(comp(k_raw, phi[0], phi_b[0]), g_kc)
    vc = comp(v_raw, phi[1], phi_b[1])
    cmp_end = jnp.arange(ncmp) * CMP_STRIDE + (CMP_BLOCK - 1)
    return kc, vc, cmp_end


def sel_blocks(a):
    B_, T = a.shape[:2]
    n_sel = -(-T // SEL_BLOCK)
    a = jnp.pad(a, ((0, 0), (0, n_sel * SEL_BLOCK - T), (0, 0), (0, 0)))
    return a.reshape(B_, n_sel, SEL_BLOCK, N_KV, HEAD_DIM).transpose(0, 3, 1, 2, 4)


def nsa_attend(q, q_pos, gates, kc, vc, cmp_end, ks_blk, vs_blk, kw, vw, w_pos):
    dt = q.dtype
    B_, Q = q.shape[:2]
    t = q_pos[:, None]
    s = jnp.einsum('bqkgd,bckd->bqkgc', q, kc)
    p_cmp = masked_softmax(s, (cmp_end[None, :] <= t)[None, :, None, None, :])
    o_cmp = jnp.einsum('bqkgc,bckd->bqkgd', p_cmp.astype(dt), vc)
    n_sel = ks_blk.shape[2]
    ci = jnp.arange(kc.shape[1])[:, None] * CMP_STRIDE
    sj = jnp.arange(n_sel)[None, :] * SEL_BLOCK
    overlap = ((ci < sj + SEL_BLOCK) & (ci + CMP_BLOCK > sj)).astype(jnp.float32)
    imp = jnp.einsum('bqkgc,cs->bqks', p_cmp, overlap)
    blk = jnp.arange(n_sel)[None, :]
    cur = t // SEL_BLOCK
    valid = blk <= cur
    forced = (blk == 0) | (blk == cur) | (blk == cur - 1)
    score = jnp.where(valid[None, :, None, :], imp, -jnp.inf)
    score = jnp.where((forced & valid)[None, :, None, :], jnp.inf, score)
    top_v, top_i = lax.top_k(score, min(SEL_TOPK, n_sel))
    kk = top_i.shape[-1]
    bi = jnp.arange(B_)[:, None, None, None]
    hi = jnp.arange(N_KV)[None, None, :, None]
    ks = ks_blk[bi, hi, top_i].reshape(B_, Q, N_KV, kk * SEL_BLOCK, HEAD_DIM)
    vs = vs_blk[bi, hi, top_i].reshape(B_, Q, N_KV, kk * SEL_BLOCK, HEAD_DIM)
    spos = (top_i[..., None] * SEL_BLOCK + jnp.arange(SEL_BLOCK)).reshape(B_, Q, N_KV, kk * SEL_BLOCK)
    smask = (spos <= q_pos[None, :, None, None]) & jnp.repeat(top_v > -jnp.inf, SEL_BLOCK, axis=-1)
    s = jnp.einsum('bqkgd,bqknd->bqkgn', q, ks)
    o_sel = jnp.einsum('bqkgn,bqknd->bqkgd', masked_softmax(s, smask[:, :, :, None, :]).astype(dt), vs)
    wd = t - w_pos[None, :]
    wmask = (w_pos[None, :] >= 0) & (wd >= 0) & (wd <= WINDOW)
    s = jnp.einsum('bqkgd,bnkd->bqkgn', q, kw)
    o_win = jnp.einsum('bqkgn,bnkd->bqkgd', masked_softmax(s, wmask[None, :, None, None, :]).astype(dt), vw)
    return gates[..., 0:1] * o_cmp + gates[..., 1:2] * o_sel + gates[..., 2:3] * o_win


def nsa_prompt(q, gates, kc_raw, vc_raw, ksel, vsel, kwin, vwin, phi, phi_b, g_kc):
    B_, S = q.shape[:2]
    kc, vc, cmp_end = nsa_compress(kc_raw, vc_raw, phi, phi_b, g_kc)
    ks_blk, vs_blk = sel_blocks(ksel), sel_blocks(vsel)
    zpad = jnp.zeros((B_, WINDOW, N_KV, HEAD_DIM), kwin.dtype)
    kw_pad = jnp.concatenate([zpad, kwin], axis=1)
    vw_pad = jnp.concatenate([zpad, vwin], axis=1)
    nq = S // Q_BLOCK

    def body(args):
        qc, gc, i = args
        start = i * Q_BLOCK
        kw = lax.dynamic_slice_in_dim(kw_pad, start, WINDOW + Q_BLOCK, axis=1)
        vw = lax.dynamic_slice_in_dim(vw_pad, start, WINDOW + Q_BLOCK, axis=1)
        return nsa_attend(qc, start + jnp.arange(Q_BLOCK), gc, kc, vc, cmp_end, ks_blk, vs_blk,
                          kw, vw, start - WINDOW + jnp.arange(WINDOW + Q_BLOCK))

    qb = q.reshape(B_, nq, Q_BLOCK, N_KV, GQA, HEAD_DIM).swapaxes(0, 1)
    gb = gates.reshape(B_, nq, Q_BLOCK, N_KV, GQA, 3).swapaxes(0, 1)
    o = lax.map(body, (qb, gb, jnp.arange(nq)))
    o = o.swapaxes(0, 1).reshape(B_, S, N_HEADS * HEAD_DIM)
    rows = jnp.stack([kc_raw, vc_raw, ksel, vsel], axis=2)
    win_new = jnp.stack([kwin, vwin], axis=2)[:, S - min(WINDOW, S):]
    return o, rows, win_new


def nsa_sample(pool, page_table, win_buf, q, gates, kc_raw, vc_raw, ksel, vsel, kwin, vwin, phi, phi_b, g_kc):
    B_, L = q.shape[:2]
    past = pool[page_table]
    past = past.reshape(B_, past.shape[1] * past.shape[2], 4, N_KV, HEAD_DIM)
    P = past.shape[1]
    rows = jnp.stack([kc_raw, vc_raw, ksel, vsel], axis=2)
    full = jnp.concatenate([past.astype(rows.dtype), rows], axis=1)
    kc, vc, cmp_end = nsa_compress(full[:, :, 0], full[:, :, 1], phi, phi_b, g_kc)
    ks_blk, vs_blk = sel_blocks(full[:, :, 2]), sel_blocks(full[:, :, 3])
    Lw = win_buf.shape[1]
    new_w = jnp.stack([kwin, vwin], axis=2)
    wfull = jnp.concatenate([win_buf.astype(new_w.dtype), new_w], axis=1)
    o = nsa_attend(q, P + jnp.arange(L), gates, kc, vc, cmp_end, ks_blk, vs_blk,
                   wfull[:, :, 0], wfull[:, :, 1], P - Lw + jnp.arange(Lw + L))
    return o.reshape(B_, L, N_HEADS * HEAD_DIM), rows, wfull[:, L:]


def expert_dispatch(xt, eidx, gate, w_gu, w_down):
    T, D = xt.shape
    M = T * TOP_E
    fe = eidx.reshape(M)
    ftok = jnp.arange(M, dtype=jnp.int32) // TOP_E
    fgate = gate.reshape(M)
    order = jnp.argsort(fe)
    se, stok, sgate = fe[order], ftok[order], fgate[order]
    counts = jnp.bincount(fe, length=N_EXPERTS)
    padded = (counts + MOE_BLOCK - 1) // MOE_BLOCK * MOE_BLOCK
    pad_end = jnp.cumsum(padded)
    pad_start = pad_end - padded
    start = jnp.cumsum(counts) - counts
    dest = pad_start[se] + jnp.arange(M) - start[se]
    n_blk = -(-M // MOE_BLOCK) + N_EXPERTS
    P = n_blk * MOE_BLOCK
    buf_tok = jnp.zeros((P,), jnp.int32).at[dest].set(stok)
    buf_gate = jnp.zeros((P,), fgate.dtype).at[dest].set(sgate)
    blk_exp = jnp.minimum(jnp.searchsorted(pad_end, jnp.arange(n_blk) * MOE_BLOCK, side='right'), N_EXPERTS - 1)
    xb = xt[buf_tok].reshape(n_blk, MOE_BLOCK, D)

    def run(args):
        xi, e = args
        a, b = jnp.split(xi @ w_gu[e], 2, axis=-1)
        return (jax.nn.silu(a) * b) @ w_down[e]

    yb = lax.map(run, (xb, blk_exp)).reshape(P, D)
    return jax.ops.segment_sum(yb * buf_gate[:, None].astype(yb.dtype), buf_tok, num_segments=T)


def moe_ffn(x, wg_r, bg_r, we_r, be_r, w_gu, w_down):
    B_, L, D = x.shape
    xt = x.reshape(B_ * L, D)
    T = xt.shape[0]
    lg = (xt @ wg_r + bg_r).astype(jnp.float32)
    pg = jax.nn.softmax(lg, axis=-1)
    gsel = jnp.argmax(lg, axis=-1)
    p_group = jnp.take_along_axis(pg, gsel[:, None], axis=-1)
    le = (xt @ we_r + be_r).astype(jnp.float32).reshape(T, N_GROUPS, EXP_PER_GROUP)
    le_g = jnp.take_along_axis(le, gsel[:, None, None], axis=1)[:, 0]
    tv, ti = lax.top_k(le_g, TOP_E)
    gate = p_group * jax.nn.softmax(tv, axis=-1)
    eidx = gsel[:, None] * EXP_PER_GROUP + ti
    return expert_dispatch(xt, eidx, gate, w_gu, w_down).reshape(B_, L, D)


def layer_forward(x, pos0, lw, pool_prev, rgc_prev, rgh0, sc_prev, nsa_fn):
    B_, L, _ = x.shape
    w_perm = permute_w_in(lw['w_in']).astype(MXU_DTYPE)
    proj2d = norm_matmul(x.reshape(B_ * L, D_MODEL), lw['norm_mix_g'], w_perm)
    proj = proj2d.reshape(B_, L, N_IN_PAD)
    q, kv, pu, rx, rgate, sc, ng = split_cols(proj, (COL_KV, KV_W, POOL_W, RG_W, RG_W, 3 * SC_W, 3 * N_HEADS))
    y_pool, pool_new = pool_mixer(pu, pool_prev, pos0, lw['pool_w'], lw['pool_scale'])
    y_rg, rgc_new, rgh_new = rglru_mixer(rx, rgate, rgc_prev, rgh0, lw['rg_conv_w'], lw['rg_conv_b'],
                                         lw['rg_w_a'], lw['rg_b_a'], lw['rg_w_x'], lw['rg_b_x'], lw['rg_lambda'])
    qk_g = lw['nsa_qk_g']
    if nsa_fn is None:
        y_nsa, nsa_rows, win_new = nsa_prompt_pallas(proj2d, B_, L, lw['nsa_phi'], lw['nsa_phi_b'], qk_g)
    else:
        q = rmsnorm(q.reshape(B_, L, N_KV, GQA, HEAD_DIM), qk_g[0]) * (HEAD_DIM ** -0.5)
        kv = kv.reshape(B_, L, 6, N_KV, HEAD_DIM)
        gates = jax.nn.sigmoid(ng.reshape(B_, L, N_KV, GQA, 3))
        y_nsa, nsa_rows, win_new = nsa_fn(q, gates, kv[:, :, 0], kv[:, :, 1], rmsnorm(kv[:, :, 2], qk_g[2]),
                                          kv[:, :, 3], rmsnorm(kv[:, :, 4], qk_g[3]), kv[:, :, 5],
                                          lw['nsa_phi'], lw['nsa_phi_b'], qk_g[1])
    z, bg, cg = split_cols(sc, (SC_W, SC_W, SC_W))
    v, sc_new = causal_dwconv(cg * z, sc_prev, lw['sc_conv_w'], lw['sc_conv_b'])
    y_sc = bg * v
    og = lw['mix_out_g'].reshape(4, GROUP_W)
    y = jnp.concatenate([rmsnorm(yi, og[i]) for i, yi in enumerate((y_pool, y_rg, y_nsa, y_sc))], axis=-1)
    x = x + jnp.einsum('blm,md->bld', y, lw['w_out'])
    x = x + moe_ffn(rmsnorm(x, lw['norm_ffn_g']), lw['router_group_w'], lw['router_group_b'],
                    lw['router_expert_w'], lw['router_expert_b'], lw['exp_w_gu'], lw['exp_w_down'])
    return x, (nsa_rows, win_new, pool_new, rgc_new, rgh_new, sc_new)


def kernel(x_prompt, x_sample, cache_nsa, state_win_kv, state_pool, state_rg_conv, state_rg_h, state_sc_conv,
           page_table, norm_mix_g, w_in, pool_w, pool_scale, rg_conv_w, rg_conv_b, rg_w_a, rg_b_a, rg_w_x, rg_b_x,
           rg_lambda, nsa_phi, nsa_phi_b, nsa_qk_g, sc_conv_w, sc_conv_b, mix_out_g, w_out, norm_ffn_g,
           router_group_w, router_group_b, router_expert_w, router_expert_b, exp_w_gu, exp_w_down):
    past_len = page_table.shape[1] * cache_nsa.shape[2]
    xp, xs = x_prompt, x_sample
    Bp = xp.shape[0]
    st_p, st_s = [], []
    for l in range(DEPTH):
        lw = dict(norm_mix_g=norm_mix_g[l], w_in=w_in[l], pool_w=pool_w[l], pool_scale=pool_scale[l],
                  rg_conv_w=rg_conv_w[l], rg_conv_b=rg_conv_b[l], rg_w_a=rg_w_a[l], rg_b_a=rg_b_a[l],
                  rg_w_x=rg_w_x[l], rg_b_x=rg_b_x[l], rg_lambda=rg_lambda[l], nsa_phi=nsa_phi[l],
                  nsa_phi_b=nsa_phi_b[l], nsa_qk_g=nsa_qk_g[l], sc_conv_w=sc_conv_w[l], sc_conv_b=sc_conv_b[l],
                  mix_out_g=mix_out_g[l], w_out=w_out[l], norm_ffn_g=norm_ffn_g[l],
                  router_group_w=router_group_w[l], router_group_b=router_group_b[l],
                  router_expert_w=router_expert_w[l], router_expert_b=router_expert_b[l],
                  exp_w_gu=exp_w_gu[l], exp_w_down=exp_w_down[l])
        xp, sp = layer_forward(xp, 0, lw,
                               jnp.zeros((Bp, POOL_KEEP, POOL_W), xp.dtype),
                               jnp.zeros((Bp, RG_CONV - 1, RG_W), xp.dtype),
                               jnp.zeros((Bp, RG_W), xp.dtype),
                               jnp.zeros((Bp, SC_CONV - 1, SC_W), xp.dtype),
                               None)
        xs, ss = layer_forward(xs, past_len, lw, state_pool[l], state_rg_conv[l], state_rg_h[l], state_sc_conv[l],
                               functools.partial(nsa_sample, cache_nsa[l], page_table, state_win_kv[l]))
        st_p.append(sp)
        st_s.append(ss)

    def stk(lst, i):
        return jnp.stack([s[i] for s in lst])

    return (xp, xs, stk(st_p, 0), stk(st_s, 0), stk(st_p, 1), stk(st_s, 1), stk(st_p, 2), stk(st_s, 2),
            stk(st_p, 3), stk(st_s, 3), stk(st_p, 4), stk(st_s, 4), stk(st_p, 5), stk(st_s, 5))
```

```python
import functools
import jax, jax.numpy as jnp
from jax import lax
import numpy as np
from jax.experimental import pallas as pl
from jax.experimental.pallas import tpu as pltpu

D_MODEL = 1024
BATCH = 4
SEQ = 4096
DEPTH = 2
DEC_BATCH = 128
DEC_SEQ = 1
PAST_LEN = 2048
PAGE_SIZE = 128

MIX_W = D_MODEL
GROUP_W = MIX_W // 4
POOL_W = GROUP_W
POOL_WINDOWS = (2, 4, 8, 16)
POOL_GROUP = POOL_W // len(POOL_WINDOWS)
POOL_KEEP = max(POOL_WINDOWS) - 1
RG_W = GROUP_W
RG_HEADS = 4
RG_BLOCK = RG_W // RG_HEADS
RG_CONV = 4
RG_C = 8.0
HEAD_DIM = 64
N_HEADS = GROUP_W // HEAD_DIM
N_KV = 2
GQA = N_HEADS // N_KV
CMP_BLOCK = 32
CMP_STRIDE = 16
SEL_BLOCK = 64
SEL_TOPK = 16
WINDOW = 512
Q_BLOCK = 128
SC_W = GROUP_W
SC_CONV = 3
N_GROUPS = 4
EXP_PER_GROUP = 8
N_EXPERTS = N_GROUPS * EXP_PER_GROUP
TOP_E = 2
D_EXPERT = 512
MOE_BLOCK = 128
EPS = 1e-6
SPLIT_SIZES = (POOL_W, RG_W, RG_W, N_HEADS * HEAD_DIM, 6 * N_KV * HEAD_DIM, 3 * N_HEADS, 3 * SC_W)
N_IN = sum(SPLIT_SIZES)

LANE = 128
ROW_TILE = 512
VMEM_LIMIT = 48 * 1024 * 1024
MXU_DTYPE = jnp.bfloat16
F32 = jnp.float32
NEG = -1e30

KV_W = 6 * N_KV * HEAD_DIM
COL_Q = 0
COL_KV = COL_Q + N_HEADS * HEAD_DIM
COL_POOL = COL_KV + KV_W
COL_RX = COL_POOL + POOL_W
COL_RGATE = COL_RX + RG_W
COL_SC = COL_RGATE + RG_W
COL_NG = COL_SC + 3 * SC_W
N_IN_PAD = COL_NG + LANE
SEL_TILE = 256
N_SEL_PROMPT = SEQ // SEL_BLOCK


def _cparams(n_axes=1):
    return pltpu.CompilerParams(dimension_semantics=("arbitrary",) * n_axes, vmem_limit_bytes=VMEM_LIMIT)


def _mm(a, b):
    return jnp.dot(a.astype(MXU_DTYPE), b.astype(MXU_DTYPE), preferred_element_type=F32)


def _mm_nt(a, b):
    return lax.dot_general(a.astype(MXU_DTYPE), b.astype(MXU_DTYPE), (((1,), (1,)), ((), ())),
                           preferred_element_type=F32)


def permute_w_in(w):
    pu, rx, rgate, q, kv, ng, sc = split_cols(w, SPLIT_SIZES)
    pad = jnp.zeros((w.shape[0], LANE - ng.shape[1]), w.dtype)
    return jnp.concatenate([q, kv, pu, rx, rgate, sc, ng, pad], axis=1)


def _norm_matmul_body(x_ref, g_ref, w_ref, o_ref):
    xf = x_ref[...]
    h = xf * lax.rsqrt(jnp.mean(xf * xf, axis=-1, keepdims=True) + EPS) * g_ref[...]
    o_ref[...] = _mm(h, w_ref[...])


def norm_matmul(x2d, g, w):
    T, D = x2d.shape
    N = w.shape[1]
    tm = min(ROW_TILE, T)
    return pl.pallas_call(
        _norm_matmul_body,
        grid=(T // tm,),
        in_specs=[pl.BlockSpec((tm, D), lambda i: (i, 0)),
                  pl.BlockSpec((1, D), lambda i: (0, 0)),
                  pl.BlockSpec((D, N), lambda i: (0, 0))],
        out_specs=pl.BlockSpec((tm, N), lambda i: (i, 0)),
        out_shape=jax.ShapeDtypeStruct((T, N), F32),
        compiler_params=_cparams(),
        name="norm_in_proj",
    )(x2d, g.reshape(1, D), w)


def _seg_rmsnorm(x, g):
    x2 = x * x
    left = lax.broadcasted_iota(jnp.int32, x.shape, 1) < HEAD_DIM
    s_l = jnp.sum(jnp.where(left, x2, 0.0), axis=-1, keepdims=True)
    s_r = jnp.sum(jnp.where(left, 0.0, x2), axis=-1, keepdims=True)
    ms = jnp.where(left, s_l, s_r) * (1.0 / HEAD_DIM)
    return x * lax.rsqrt(ms + EPS) * g


def _nsa_prep_body(qkv_ref, ng_ref, g_ref, perm_ref, qa_ref, kvb_ref, rawb_ref, rows_ref, win_ref, gates_ref):
    g = g_ref[...]
    for hb in range(N_KV):
        qn = _seg_rmsnorm(qkv_ref[:, COL_Q + hb * LANE:COL_Q + (hb + 1) * LANE], g[0:1]) * (HEAD_DIM ** -0.5)
        qa_ref[:, hb * 2 * LANE:(hb + 1) * 2 * LANE] = _mm(qn, perm_ref[hb]).astype(qa_ref.dtype)
    comp = [qkv_ref[:, COL_KV + c * LANE:COL_KV + (c + 1) * LANE] for c in range(6)]
    comp[2] = _seg_rmsnorm(comp[2], g[2:3])
    comp[4] = _seg_rmsnorm(comp[4], g[3:4])
    for c in range(6):
        kvb_ref[:, c * LANE:(c + 1) * LANE] = comp[c].astype(kvb_ref.dtype)
    for c in range(2):
        rawb_ref[:, c * LANE:(c + 1) * LANE] = comp[c].astype(rawb_ref.dtype)
    for c in range(4):
        rows_ref[:, c * LANE:(c + 1) * LANE] = comp[c]
    win_ref[:, 0:LANE] = comp[4]
    win_ref[:, LANE:2 * LANE] = comp[5]
    gates_ref[...] = jax.nn.sigmoid(ng_ref[...])


def _q_place_matrices():
    p = np.zeros((N_KV, LANE, 2 * LANE), np.float32)
    for hb in range(N_KV):
        for gq in range(GQA):
            for d in range(HEAD_DIM):
                p[hb, gq * HEAD_DIM + d, gq * LANE + hb * HEAD_DIM + d] = 1.0
    return jnp.asarray(p, MXU_DTYPE)


def nsa_prep(proj, qk_g):
    T = proj.shape[0]
    tm = min(ROW_TILE, T)
    qkv_w = COL_POOL
    g4 = jnp.tile(qk_g, (1, 2))
    return pl.pallas_call(
        _nsa_prep_body,
        grid=(T // tm,),
        in_specs=[pl.BlockSpec((tm, qkv_w), lambda i: (i, 0)),
                  pl.BlockSpec((tm, LANE), lambda i: (i, COL_NG // LANE)),
                  pl.BlockSpec((4, LANE), lambda i: (0, 0)),
                  pl.BlockSpec((N_KV, LANE, 2 * LANE), lambda i: (0, 0, 0))],
        out_specs=[pl.BlockSpec((tm, 4 * LANE), lambda i: (i, 0)),
                   pl.BlockSpec((tm, 6 * LANE), lambda i: (i, 0)),
                   pl.BlockSpec((tm, 2 * LANE), lambda i: (i, 0)),
                   pl.BlockSpec((tm, 4 * LANE), lambda i: (i, 0)),
                   pl.BlockSpec((tm, 2 * LANE), lambda i: (i, 0)),
                   pl.BlockSpec((tm, LANE), lambda i: (i, 0))],
        out_shape=[jax.ShapeDtypeStruct((T, 4 * LANE), MXU_DTYPE),
                   jax.ShapeDtypeStruct((T, 6 * LANE), MXU_DTYPE),
                   jax.ShapeDtypeStruct((T, 2 * LANE), MXU_DTYPE),
                   jax.ShapeDtypeStruct((T, 4 * LANE), F32),
                   jax.ShapeDtypeStruct((T, 2 * LANE), F32),
                   jax.ShapeDtypeStruct((T, LANE), F32)],
        compiler_params=_cparams(),
        name="nsa_prep",
    )(proj, proj, g4, _q_place_matrices())


def compress_weights(phi):
    R = CMP_BLOCK // CMP_STRIDE
    wr = phi.reshape(2, R, CMP_STRIDE, HEAD_DIM, HEAD_DIM)
    eye = jnp.eye(2, dtype=phi.dtype)
    w = jnp.einsum('crjde,cx,hy->rjchdxye', wr, eye, eye)
    return w.reshape(R, CMP_STRIDE * 2 * LANE, 2 * LANE).astype(MXU_DTYPE)


def _compress_body(x_ref, w_ref, b_ref, g_ref, kc_ref, vc_ref):
    x = x_ref[0]
    nch = x.shape[0]
    a = _mm(x, w_ref[0])
    bm = _mm(x, w_ref[1])
    out = a + pltpu.roll(bm, nch - 1, 0) + b_ref[...]
    kc_ref[0] = _seg_rmsnorm(out[:, 0:LANE], g_ref[...]).astype(kc_ref.dtype)
    vc_ref[0] = out[:, LANE:2 * LANE].astype(vc_ref.dtype)


def nsa_compress_pallas(rawb3, wc, phi_b, g_kc):
    B_, nch, K = rawb3.shape
    bias = jnp.concatenate([jnp.tile(phi_b[0], 2), jnp.tile(phi_b[1], 2)]).reshape(1, 2 * LANE)
    return pl.pallas_call(
        _compress_body,
        grid=(B_,),
        in_specs=[pl.BlockSpec((1, nch, K), lambda b: (b, 0, 0)),
                  pl.BlockSpec(wc.shape, lambda b: (0, 0, 0)),
                  pl.BlockSpec((1, 2 * LANE), lambda b: (0, 0)),
                  pl.BlockSpec((1, LANE), lambda b: (0, 0))],
        out_specs=[pl.BlockSpec((1, nch, LANE), lambda b: (b, 0, 0)),
                   pl.BlockSpec((1, nch, LANE), lambda b: (b, 0, 0))],
        out_shape=[jax.ShapeDtypeStruct((B_, nch, LANE), MXU_DTYPE),
                   jax.ShapeDtypeStruct((B_, nch, LANE), MXU_DTYPE)],
        compiler_params=_cparams(),
        name="nsa_compress",
    )(rawb3, wc, bias, jnp.tile(g_kc, 2).reshape(1, LANE))


def _online_update(carry, s, mask, v):
    m, l, acc = carry
    s = jnp.where(mask, s, NEG)
    m_new = jnp.maximum(m, jnp.max(s, axis=-1, keepdims=True))
    alpha = jnp.exp(m - m_new)
    p = jnp.exp(s - m_new)
    l = alpha * l + jnp.sum(p, axis=-1, keepdims=True)
    acc = alpha * acc + _mm(p, v)
    return m_new, l, acc


def _select_blocks(imp, start):
    n_sel = N_SEL_PROMPT
    sc_t = imp.T[0:n_sel]
    blk = lax.broadcasted_iota(jnp.int32, sc_t.shape, 0)
    cur = (start + lax.broadcasted_iota(jnp.int32, sc_t.shape, 1)) // SEL_BLOCK
    valid = blk <= cur
    forced = (blk == 0) | (blk == cur) | (blk == cur - 1)
    score = jnp.where(valid, sc_t, -jnp.inf)
    score = jnp.where(forced & valid, jnp.inf, score)
    cnt = jnp.zeros(sc_t.shape, F32)
    for i in range(n_sel):
        ri = score[i:i + 1, :]
        beats = (ri > score) | ((ri == score) & (blk > i))
        cnt = cnt + jnp.where(beats, 1.0, 0.0)
    sel_t = jnp.where((cnt < SEL_TOPK) & (score > -jnp.inf), 1.0, 0.0)
    sel_t = jnp.concatenate([sel_t, jnp.zeros((LANE - n_sel, sc_t.shape[1]), F32)], axis=0)
    return sel_t.T


def _nsa_attn_body(qa_ref, gates_ref, kc_ref, vc_ref, kv_ref, ov_ref, e_ref, o_ref):
    i = pl.program_id(1)
    start = i * Q_BLOCK
    Q = Q_BLOCK
    R = GQA * Q
    t_row = start + lax.broadcasted_iota(jnp.int32, (R, 1), 0) % Q
    gates = gates_ref[...]
    lane_q = lax.broadcasted_iota(jnp.int32, (Q, LANE), 1)
    for h in range(N_KV):
        qs = jnp.concatenate([qa_ref[:, (h * GQA + gq) * LANE:(h * GQA + gq + 1) * LANE] for gq in range(GQA)],
                             axis=0)
        kc = kc_ref[0]
        ncmp = kc.shape[0]
        s = _mm_nt(qs, kc)
        cmp_end = lax.broadcasted_iota(jnp.int32, (R, ncmp), 1) * CMP_STRIDE + (CMP_BLOCK - 1)
        s = jnp.where(cmp_end <= t_row, s, -jnp.inf)
        m = jnp.max(s, axis=-1, keepdims=True)
        e = jnp.exp(s - jnp.where(m > -jnp.inf, m, 0.0))
        d = jnp.sum(e, axis=-1, keepdims=True)
        p_cmp = e / jnp.where(d > 0, d, 1.0)
        o_cmp = _mm(p_cmp, vc_ref[0])
        imp = _mm(p_cmp[0:Q], ov_ref[...]) + _mm(p_cmp[Q:R], ov_ref[...])
        sel = _select_blocks(imp, start).astype(MXU_DTYPE)

        def sel_step(j, carry):
            off = pl.multiple_of(j * SEL_TILE, SEL_TILE)
            k = kv_ref[pl.ds(off, SEL_TILE), 2 * LANE:3 * LANE]
            v = kv_ref[pl.ds(off, SEL_TILE), 3 * LANE:4 * LANE]
            sj = _mm_nt(qs, k)
            msel = _mm(sel, e_ref[j])
            msel = jnp.concatenate([msel] * GQA, axis=0)
            kpos = off + lax.broadcasted_iota(jnp.int32, (R, SEL_TILE), 1)
            return _online_update(carry, sj, (msel > 0.5) & (kpos <= t_row), v)

        init = (jnp.full((R, 1), NEG, F32), jnp.zeros((R, 1), F32), jnp.zeros((R, LANE), F32))
        n_tiles = (start + Q + SEL_TILE - 1) // SEL_TILE
        _, l_s, acc_s = lax.fori_loop(0, n_tiles, sel_step, init)
        o_sel = acc_s / l_s

        carry = init
        for kk in range(WINDOW // Q + 1):
            tile = i - kk
            off = pl.multiple_of(jnp.maximum(tile, 0) * Q, Q)
            k = kv_ref[pl.ds(off, Q), 4 * LANE:5 * LANE]
            v = kv_ref[pl.ds(off, Q), 5 * LANE:6 * LANE]
            sj = _mm_nt(qs, k)
            wd = t_row - (tile * Q + lax.broadcasted_iota(jnp.int32, (R, Q), 1))
            carry = _online_update(carry, sj, (tile >= 0) & (wd >= 0) & (wd <= WINDOW), v)
        o_win = carry[2] / carry[1]

        outs = []
        for gq in range(GQA):
            c0 = (h * GQA + gq) * 3
            rs = slice(gq * Q, (gq + 1) * Q)
            og = (gates[:, c0:c0 + 1] * o_cmp[rs] + gates[:, c0 + 1:c0 + 2] * o_sel[rs]
                  + gates[:, c0 + 2:c0 + 3] * o_win[rs])
            outs.append(og if gq == h else pltpu.roll(og, HEAD_DIM, 1))
        o_ref[:, h * LANE:(h + 1) * LANE] = jnp.where(lane_q < HEAD_DIM, outs[0], outs[1])


def _sel_constants(S):
    ncmp_rows = S // CMP_STRIDE
    ci = np.arange(ncmp_rows)[:, None] * CMP_STRIDE
    sj = np.arange(LANE)[None, :] * SEL_BLOCK
    ov = ((ci < sj + SEL_BLOCK) & (ci + CMP_BLOCK > sj) & (np.arange(LANE)[None, :] < S // SEL_BLOCK))
    n_t = S // SEL_TILE
    key_blk = (np.arange(n_t)[:, None, None] * SEL_TILE + np.arange(SEL_TILE)[None, None, :]) // SEL_BLOCK
    e = (np.arange(LANE)[None, :, None] == key_blk)
    return jnp.asarray(ov, MXU_DTYPE), jnp.asarray(e, MXU_DTYPE)


def nsa_attn_prompt(qa, gates, kc, vc, kvb, B_, S):
    nq = S // Q_BLOCK
    nch = S // CMP_STRIDE
    ov, e3 = _sel_constants(S)
    return pl.pallas_call(
        _nsa_attn_body,
        grid=(B_, nq),
        in_specs=[pl.BlockSpec((Q_BLOCK, 4 * LANE), lambda b, i: (b * nq + i, 0)),
                  pl.BlockSpec((Q_BLOCK, LANE), lambda b, i: (b * nq + i, 0)),
                  pl.BlockSpec((1, nch, LANE), lambda b, i: (b, 0, 0)),
                  pl.BlockSpec((1, nch, LANE), lambda b, i: (b, 0, 0)),
                  pl.BlockSpec((S, 6 * LANE), lambda b, i: (b, 0)),
                  pl.BlockSpec(ov.shape, lambda b, i: (0, 0)),
                  pl.BlockSpec(e3.shape, lambda b, i: (0, 0, 0))],
        out_specs=pl.BlockSpec((Q_BLOCK, 2 * LANE), lambda b, i: (b * nq + i, 0)),
        out_shape=jax.ShapeDtypeStruct((B_ * S, N_HEADS * HEAD_DIM), F32),
        compiler_params=_cparams(2),
        name="nsa_attn_prompt",
    )(qa, gates, kc, vc, kvb, ov, e3)


def nsa_prompt_pallas(proj, B_, S, phi, phi_b, qk_g):
    qa, kvb, rawb, rows, win, gates = nsa_prep(proj, qk_g)
    nch = S // CMP_STRIDE
    kc, vc = nsa_compress_pallas(rawb.reshape(B_, nch, CMP_STRIDE * 2 * LANE), compress_weights(phi), phi_b, qk_g[1])
    o = nsa_attn_prompt(qa, gates, kc, vc, kvb, B_, S)
    rows = rows.reshape(B_, S, 4, N_KV, HEAD_DIM)
    win_new = win.reshape(B_, S, 2, N_KV, HEAD_DIM)[:, S - min(WINDOW, S):]
    return o.reshape(B_, S, N_HEADS * HEAD_DIM), rows, win_new


N_PAGES = PAST_LEN // PAGE_SIZE
N_CHUNK_S = PAST_LEN // CMP_STRIDE
N_SEL_S = -(-(PAST_LEN + DEC_SEQ) // SEL_BLOCK)
CUR_S = PAST_LEN // SEL_BLOCK
QROWS = 8


def compress_weights_paged(phi):
    R = CMP_BLOCK // CMP_STRIDE
    wr = phi.reshape(2, R, CMP_STRIDE, HEAD_DIM, HEAD_DIM)
    eye = jnp.eye(2, dtype=phi.dtype)
    w = jnp.einsum('crjde,cx,hy->jchdrxye', wr, eye, eye)
    return w.reshape(CMP_STRIDE, 2 * LANE, R * 2 * LANE).astype(MXU_DTYPE)


def _softmax_with_extra(s, s_new):
    m = jnp.maximum(jnp.max(s, axis=-1, keepdims=True), s_new)
    e = jnp.exp(s - m)
    e_new = jnp.exp(s_new - m)
    return e, e_new, jnp.sum(e, axis=-1, keepdims=True) + e_new


def _cache_compress_body(x_ref, w_ref, o_ref):
    part = _mm(x_ref[...], w_ref[0])

    @pl.when(pl.program_id(1) == 0)
    def _():
        o_ref[...] = part

    @pl.when(pl.program_id(1) > 0)
    def _():
        o_ref[...] += part


CHUNKS_PER_PAGE = PAGE_SIZE // CMP_STRIDE
SWEEP_ROWS = 2048


def cache_compress(cache_nsa, nsa_phi):
    n_rows = cache_nsa.shape[1] * CHUNKS_PER_PAGE
    assert n_rows % SWEEP_ROWS == 0
    tiles = n_rows // SWEEP_ROWS
    cv = cache_nsa.reshape(DEPTH * n_rows, CMP_STRIDE * 4 * LANE)
    wc = jnp.concatenate([compress_weights_paged(nsa_phi[l]) for l in range(DEPTH)], axis=0)
    return pl.pallas_call(
        _cache_compress_body,
        grid=(DEPTH * tiles, CMP_STRIDE),
        in_specs=[pl.BlockSpec((SWEEP_ROWS, 2 * LANE), lambda i, j: (i, 2 * j)),
                  pl.BlockSpec((1, 2 * LANE, 4 * LANE), lambda i, j: ((i // tiles) * CMP_STRIDE + j, 0, 0))],
        out_specs=pl.BlockSpec((SWEEP_ROWS, 4 * LANE), lambda i, j: (i, 0)),
        out_shape=jax.ShapeDtypeStruct((DEPTH * n_rows, 4 * LANE), F32),
        compiler_params=_cparams(2),
        name="cache_compress",
    )(cv, wc)


def _nsa_sample_body(pt_ref, qa_ref, newb_ref, wnew_ref, gates_ref, *rest):
    pages = rest[:N_PAGES]
    abs_ = rest[N_PAGES:2 * N_PAGES]
    win_ref, bias_ref, gkc_ref, ov_ref, e_ref, y_ref, wout_ref = rest[2 * N_PAGES:]
    qs = qa_ref[0]
    newb = newb_ref[0].astype(F32)
    lane = lax.broadcasted_iota(jnp.int32, (QROWS, LANE), 1)
    row = lax.broadcasted_iota(jnp.int32, (QROWS, LANE), 0)

    ab = jnp.concatenate([a[...] for a in abs_], axis=0)
    out = ab[:, 0:2 * LANE] + pltpu.roll(ab[:, 2 * LANE:4 * LANE], N_CHUNK_S - 1, 0) + bias_ref[...]
    kc = _seg_rmsnorm(out[:, 0:LANE], gkc_ref[...])
    vc = out[:, LANE:2 * LANE]

    s = _mm_nt(qs, kc)
    s = jnp.where(lane < N_CHUNK_S - 1, s, -jnp.inf)
    e = jnp.exp(s - jnp.max(s, axis=-1, keepdims=True))
    p_cmp = e / jnp.sum(e, axis=-1, keepdims=True)
    o_cmp = _mm(p_cmp, vc)
    imp = _mm(p_cmp, ov_ref[...])
    imp = imp + jnp.where(row % GQA == 0, pltpu.roll(imp, QROWS - 1, 0), pltpu.roll(imp, 1, 0))

    valid = lane <= CUR_S
    forced = (lane == 0) | (lane == CUR_S) | (lane == CUR_S - 1)
    score = jnp.where(valid, imp, -jnp.inf)
    score = jnp.where(forced & valid, jnp.inf, score)
    cnt = jnp.zeros((QROWS, LANE), F32)
    for i in range(N_SEL_S):
        ci = score[:, i:i + 1]
        cnt = cnt + jnp.where((ci > score) | ((ci == score) & (lane > i)), 1.0, 0.0)
    sel = jnp.where((cnt < SEL_TOPK) & (score > -jnp.inf), 1.0, 0.0)

    msel = _mm(sel, e_ref[...])
    s = jnp.concatenate([_mm_nt(qs, pg[0, :, 0:LANE]) for pg in pages], axis=1)
    s = jnp.where(msel > 0.5, s, NEG)
    qf = qs.astype(F32)
    s_new = jnp.sum(qf * newb[:, 2 * LANE:3 * LANE], axis=-1, keepdims=True)
    s_new = jnp.where(sel[:, CUR_S:CUR_S + 1] > 0.5, s_new, NEG)
    e, e_new, d = _softmax_with_extra(s, s_new)
    acc_o = e_new.astype(MXU_DTYPE).astype(F32) * newb[:, 3 * LANE:4 * LANE]
    for p, pg in enumerate(pages):
        acc_o = acc_o + _mm(e[:, p * PAGE_SIZE:(p + 1) * PAGE_SIZE], pg[0, :, LANE:2 * LANE])
    o_sel = acc_o / d

    win = win_ref[0]
    s = _mm_nt(qs, win[:, 0:LANE])
    s_new = jnp.sum(qf * newb[:, 4 * LANE:5 * LANE], axis=-1, keepdims=True)
    e, e_new, d = _softmax_with_extra(s, s_new)
    o_win = (_mm(e, win[:, LANE:2 * LANE]) + e_new.astype(MXU_DTYPE).astype(F32) * newb[:, 5 * LANE:6 * LANE]) / d

    g = gates_ref[0]
    o = g[:, 0:1] * o_cmp + g[:, 1:2] * o_sel + g[:, 2:3] * o_win
    o_sw = pltpu.roll(o, HEAD_DIM, 1)
    lane1 = lax.broadcasted_iota(jnp.int32, (1, LANE), 1)
    ys = []
    for h in range(N_KV):
        a = (o if h == 0 else o_sw)[GQA * h:GQA * h + 1]
        b = (o if h == 1 else o_sw)[GQA * h + 1:GQA * h + 2]
        ys.append(jnp.where(lane1 < HEAD_DIM, a, b))
    y_ref[0] = jnp.concatenate(ys, axis=1)

    lw = win.shape[0]
    shifted = pltpu.roll(win, lw - 1, 0)
    wout_ref[0] = jnp.where(lax.broadcasted_iota(jnp.int32, win.shape, 0) == lw - 1, wnew_ref[0], shifted)


def _sample_constants():
    ci = np.arange(LANE)[:, None] * CMP_STRIDE
    sj = np.arange(LANE)[None, :] * SEL_BLOCK
    ov = ((ci < sj + SEL_BLOCK) & (ci + CMP_BLOCK > sj) & (np.arange(LANE)[:, None] < N_CHUNK_S - 1)
          & (np.arange(LANE)[None, :] < N_SEL_S))
    e = (np.arange(LANE)[:, None] == (np.arange(PAST_LEN)[None, :] // SEL_BLOCK))
    return jnp.asarray(ov, MXU_DTYPE), jnp.asarray(e, MXU_DTYPE)


def nsa_sample_pallas(proj, layer, cache3, cache_ab, page_table, win3, phi_b, qk_g):
    B_ = proj.shape[0]
    n_phys = cache3.shape[0] // DEPTH
    lw = win3.shape[1]
    assert page_table.shape == (B_, N_PAGES) and lw <= WINDOW and lw <= PAST_LEN and CUR_S == N_SEL_S - 1
    qa, kvb, _, rows, wnew, gates = nsa_prep(proj, qk_g)
    qa8 = jnp.pad(qa.astype(F32).reshape(B_, N_HEADS, LANE), ((0, 0), (0, QROWS - N_HEADS), (0, 0)))
    gates8 = jnp.pad(gates[:, :3 * N_HEADS].reshape(B_, N_HEADS, 3), ((0, 0), (0, QROWS - N_HEADS), (0, LANE - 3)))
    ov, e = _sample_constants()
    bias = jnp.concatenate([jnp.tile(phi_b[0], 2), jnp.tile(phi_b[1], 2)]).reshape(1, 2 * LANE)

    def page_spec(p):
        return pl.BlockSpec((1, PAGE_SIZE, 2 * LANE), lambda b, pt: (layer * n_phys + pt[b, p], 0, 1))

    def ab_spec(p):
        return pl.BlockSpec((CHUNKS_PER_PAGE, 4 * LANE), lambda b, pt: (layer * n_phys + pt[b, p], 0))

    def per_b(shape):
        return pl.BlockSpec((1,) + shape, lambda b, pt: (b, 0, 0))

    def const(a):
        return pl.BlockSpec(a.shape, lambda b, pt: (0,) * a.ndim)

    gkc = jnp.tile(qk_g[1], 2).reshape(1, LANE)
    y, wout = pl.pallas_call(
        _nsa_sample_body,
        grid_spec=pltpu.PrefetchScalarGridSpec(
            num_scalar_prefetch=1,
            grid=(B_,),
            in_specs=[per_b((QROWS, LANE)), per_b((1, 6 * LANE)), per_b((1, 2 * LANE)), per_b((QROWS, LANE))]
                     + [page_spec(p) for p in range(N_PAGES)] + [ab_spec(p) for p in range(N_PAGES)]
                     + [pl.BlockSpec((1, lw, 2 * LANE), lambda b, pt: (layer * B_ + b, 0, 0)),
                        const(bias), const(gkc), const(ov), const(e)],
            out_specs=[per_b((1, 2 * LANE)), per_b((lw, 2 * LANE))]),
        out_shape=[jax.ShapeDtypeStruct((B_, 1, 2 * LANE), F32),
                   jax.ShapeDtypeStruct((B_, lw, 2 * LANE), F32)],
        compiler_params=_cparams(),
        name="nsa_sample",
    )(page_table, qa8, kvb.reshape(B_, 1, 6 * LANE), wnew.reshape(B_, 1, 2 * LANE), gates8,
      *([cache3] * N_PAGES), *([cache_ab] * N_PAGES), win3, bias, gkc, ov, e)
    return (y.reshape(B_, 1, N_HEADS * HEAD_DIM), rows.reshape(B_, 1, 4, N_KV, HEAD_DIM),
            wout.reshape(B_, lw, 2, N_KV, HEAD_DIM))


ROUTE_W = LANE
GROUP_LANE0 = N_EXPERTS
MOE_TILE_PROMPT = 256
MOE_TILE_SAMPLE = 32
COMBINE_TILE = 256


def _rms(x, g):
    return x * lax.rsqrt(jnp.mean(x * x, axis=-1, keepdims=True) + EPS) * g


def _mix_out_router_body(y_ref, x_ref, og_ref, wo_ref, gf_ref, wr_ref, br_ref, x2_ref, xn_ref, route_ref):
    og = og_ref[...]
    yn = jnp.concatenate([_rms(y_ref[:, i * GROUP_W:(i + 1) * GROUP_W], og[:, i * GROUP_W:(i + 1) * GROUP_W])
                          for i in range(4)], axis=1)
    x2 = x_ref[...] + _mm(yn, wo_ref[...])
    x2_ref[...] = x2
    xn = _rms(x2, gf_ref[...])
    xn_ref[...] = xn
    logits = _mm(xn, wr_ref[...]) + br_ref[...]
    lane = lax.broadcasted_iota(jnp.int32, logits.shape, 1)
    is_grp = (lane >= GROUP_LANE0) & (lane < GROUP_LANE0 + N_GROUPS)
    grp = jnp.where(is_grp, logits, -jnp.inf)
    gmax = jnp.max(grp, axis=-1, keepdims=True)
    gsel = jnp.min(jnp.where(grp == gmax, lane - GROUP_LANE0, N_GROUPS), axis=-1, keepdims=True)
    p_group = 1.0 / jnp.sum(jnp.where(is_grp, jnp.exp(logits - gmax), 0.0), axis=-1, keepdims=True)
    le = jnp.where((lane < N_EXPERTS) & (lane // EXP_PER_GROUP == gsel), logits, -jnp.inf)
    m1 = jnp.max(le, axis=-1, keepdims=True)
    i1 = jnp.min(jnp.where(le == m1, lane, LANE), axis=-1, keepdims=True)
    le2 = jnp.where(lane == i1, -jnp.inf, le)
    m2 = jnp.max(le2, axis=-1, keepdims=True)
    i2 = jnp.min(jnp.where(le2 == m2, lane, LANE), axis=-1, keepdims=True)
    e2 = jnp.exp(m2 - m1)
    g1 = p_group * (1.0 / (1.0 + e2))
    g2 = p_group * (e2 / (1.0 + e2))
    route_ref[...] = jnp.where(lane == 0, i1.astype(F32), jnp.where(lane == 1, i2.astype(F32),
                               jnp.where(lane == 2, g1, jnp.where(lane == 3, g2, 0.0))))


def mix_out_router(ymix, x2d, lw):
    T, D = x2d.shape
    tm = min(256, T)
    wr = jnp.concatenate([lw['router_expert_w'], lw['router_group_w'],
                          jnp.zeros((D, ROUTE_W - N_EXPERTS - N_GROUPS), F32)], axis=1).astype(MXU_DTYPE)
    br = jnp.concatenate([lw['router_expert_b'], lw['router_group_b'],
                          jnp.zeros((ROUTE_W - N_EXPERTS - N_GROUPS,), F32)]).reshape(1, ROUTE_W)
    row = lambda i: (i, 0)
    fixed = lambda i: (0, 0)
    return pl.pallas_call(
        _mix_out_router_body,
        grid=(T // tm,),
        in_specs=[pl.BlockSpec((tm, MIX_W), row), pl.BlockSpec((tm, D), row), pl.BlockSpec((1, MIX_W), fixed),
                  pl.BlockSpec((MIX_W, D), fixed), pl.BlockSpec((1, D), fixed), pl.BlockSpec((D, ROUTE_W), fixed),
                  pl.BlockSpec((1, ROUTE_W), fixed)],
        out_specs=[pl.BlockSpec((tm, D), row), pl.BlockSpec((tm, D), row), pl.BlockSpec((tm, ROUTE_W), row)],
        out_shape=[jax.ShapeDtypeStruct((T, D), F32), jax.ShapeDtypeStruct((T, D), F32),
                   jax.ShapeDtypeStruct((T, ROUTE_W), F32)],
        compiler_params=_cparams(),
        name="mix_out_router",
    )(ymix, x2d, lw['mix_out_g'].reshape(1, MIX_W), lw['w_out'].astype(MXU_DTYPE), lw['norm_ffn_g'].reshape(1, D),
      wr, br)


def moe_schedule(eidx, tile):
    T = eidx.shape[0]
    M = T * TOP_E
    fe = eidx.reshape(M)
    onehot = (fe[:, None] == jnp.arange(N_EXPERTS, dtype=jnp.int32)[None, :]).astype(jnp.int32)
    csum = jnp.cumsum(onehot, axis=0)
    rank = jnp.take_along_axis(csum, fe[:, None], axis=1)[:, 0] - 1
    counts = csum[-1]
    padded = (counts + tile - 1) // tile * tile
    pad_end = jnp.cumsum(padded)
    dest = (pad_end - padded)[fe] + rank
    n_blk = -(-M // tile) + N_EXPERTS
    tok = jnp.arange(M, dtype=jnp.int32) // TOP_E
    buf_tok = jnp.zeros((n_blk * tile,), jnp.int32).at[dest].set(tok)
    blk_exp = jnp.minimum(jnp.searchsorted(pad_end, jnp.arange(n_blk, dtype=jnp.int32) * tile, side='right'),
                          N_EXPERTS - 1).astype(jnp.int32)
    n_used = (pad_end[-1:] // tile).astype(jnp.int32)
    return buf_tok, blk_exp, n_used, dest.astype(jnp.int32)


def _moe_ffn_body(tile, tok_ref, bexp_ref, nused_ref, x_hbm, wgu_ref, wdn_ref, y_ref, xg, sem, wgu_bf, wdn_bf):
    j = pl.program_id(0)
    n = nused_ref[0]

    def gather(blk, slot):
        def body(r, c):
            t = tok_ref[blk * tile + r]
            pltpu.make_async_copy(x_hbm.at[pl.ds(t, 1)], xg.at[slot, pl.ds(r, 1)], sem.at[slot]).start()
            return c
        lax.fori_loop(0, tile, body, 0, unroll=8)

    @pl.when((j == 0) & (n > 0))
    def _():
        gather(0, 0)

    @pl.when(j < n)
    def _():
        slot = j % 2

        @pl.when(j + 1 < n)
        def _():
            gather(j + 1, 1 - slot)

        @pl.when((j == 0) | (bexp_ref[j] != bexp_ref[jnp.maximum(j - 1, 0)]))
        def _():
            wgu_bf[...] = wgu_ref[0].astype(wgu_bf.dtype)
            wdn_bf[...] = wdn_ref[0].astype(wdn_bf.dtype)

        pltpu.make_async_copy(x_hbm.at[pl.ds(0, tile)], xg.at[slot], sem.at[slot]).wait()
        h = _mm(xg[slot], wgu_bf[...])
        a, b = h[:, :D_EXPERT], h[:, D_EXPERT:]
        y_ref[...] = _mm(a * jax.nn.sigmoid(a) * b, wdn_bf[...])

    @pl.when(j >= n)
    def _():
        y_ref[...] = jnp.zeros_like(y_ref)


def moe_ffn_pallas(xn, buf_tok, blk_exp, n_used, w_gu, w_down, tile):
    T, D = xn.shape
    n_blk = blk_exp.shape[0]
    return pl.pallas_call(
        functools.partial(_moe_ffn_body, tile),
        grid_spec=pltpu.PrefetchScalarGridSpec(
            num_scalar_prefetch=3,
            grid=(n_blk,),
            in_specs=[pl.BlockSpec(memory_space=pl.ANY),
                      pl.BlockSpec((1, D, 2 * D_EXPERT), lambda j, tok, bexp, nu: (bexp[j], 0, 0)),
                      pl.BlockSpec((1, D_EXPERT, D), lambda j, tok, bexp, nu: (bexp[j], 0, 0))],
            out_specs=pl.BlockSpec((tile, D), lambda j, tok, bexp, nu: (j, 0)),
            scratch_shapes=[pltpu.VMEM((2, tile, D), F32), pltpu.SemaphoreType.DMA((2,)),
                            pltpu.VMEM((D, 2 * D_EXPERT), MXU_DTYPE), pltpu.VMEM((D_EXPERT, D), MXU_DTYPE)]),
        out_shape=jax.ShapeDtypeStruct((n_blk * tile, D), F32),
        compiler_params=_cparams(),
        name="moe_ffn",
    )(buf_tok, blk_exp, n_used, xn, w_gu, w_down)


def _moe_combine_body(tm, slots_ref, y_hbm, x2_ref, route_ref, o_ref, yb, sem):
    i = pl.program_id(0)
    nt = pl.num_programs(0)

    def gather(tile_i, buf):
        def body(r, c):
            for k in range(TOP_E):
                s = slots_ref[(tile_i * tm + r) * TOP_E + k]
                pltpu.make_async_copy(y_hbm.at[pl.ds(s, 1)], yb.at[buf, k, pl.ds(r, 1)], sem.at[buf]).start()
            return c
        lax.fori_loop(0, tm, body, 0, unroll=8)

    @pl.when(i == 0)
    def _():
        gather(0, 0)

    buf = i % 2

    @pl.when(i + 1 < nt)
    def _():
        gather(i + 1, 1 - buf)

    for k in range(TOP_E):
        pltpu.make_async_copy(y_hbm.at[pl.ds(0, tm)], yb.at[buf, k], sem.at[buf]).wait()
    r = route_ref[...]
    o_ref[...] = x2_ref[...] + (r[:, 2:3] * yb[buf, 0] + r[:, 3:4] * yb[buf, 1])


def moe_combine_pallas(y, slots, x2, route):
    T, D = x2.shape
    tm = min(COMBINE_TILE, T)
    return pl.pallas_call(
        functools.partial(_moe_combine_body, tm),
        grid_spec=pltpu.PrefetchScalarGridSpec(
            num_scalar_prefetch=1,
            grid=(T // tm,),
            in_specs=[pl.BlockSpec(memory_space=pl.ANY),
                      pl.BlockSpec((tm, D), lambda i, s: (i, 0)),
                      pl.BlockSpec((tm, ROUTE_W), lambda i, s: (i, 0))],
            out_specs=pl.BlockSpec((tm, D), lambda i, s: (i, 0)),
            scratch_shapes=[pltpu.VMEM((2, TOP_E, tm, D), F32), pltpu.SemaphoreType.DMA((2,))]),
        out_shape=jax.ShapeDtypeStruct((T, D), F32),
        compiler_params=_cparams(),
        name="moe_combine",
    )(slots, y, x2, route)


def mix_out_moe(ymix, x2d, lw, tile):
    x2, xn, route = mix_out_router(ymix, x2d, lw)
    eidx = route[:, 0:TOP_E].astype(jnp.int32)
    buf_tok, blk_exp, n_used, slots = moe_schedule(eidx, tile)
    y = moe_ffn_pallas(xn, buf_tok, blk_exp, n_used, lw['exp_w_gu'], lw['exp_w_down'], tile)
    return moe_combine_pallas(y, slots, x2, route)


def rmsnorm(x, g):
    xf = x.astype(jnp.float32)
    y = xf * lax.rsqrt(jnp.mean(xf * xf, axis=-1, keepdims=True) + EPS)
    return (y * g.astype(jnp.float32)).astype(x.dtype)


def split_cols(a, sizes):
    outs, o = [], 0
    for s in sizes:
        outs.append(a[..., o:o + s])
        o += s
    return outs


def causal_dwconv(u, prev, w, b):
    L = u.shape[1]
    ext = jnp.concatenate([prev.astype(u.dtype), u], axis=1)
    y = lax.conv_general_dilated(ext, w[:, None, :].astype(u.dtype), window_strides=(1,), padding='VALID',
                                 dimension_numbers=('NWC', 'WIO', 'NWC'), feature_group_count=u.shape[-1])
    return y + b.astype(u.dtype), ext[:, L:]


def pool_mixer(u, prev, pos0, w, scale):
    B_, L, C = u.shape
    ext = jnp.concatenate([prev.astype(u.dtype), u], axis=1)
    ef = ext.astype(jnp.float32)
    cs = jnp.concatenate([jnp.zeros((B_, 1, C), jnp.float32), jnp.cumsum(ef, axis=1)], axis=1)
    pos = pos0 + jnp.arange(L)
    means = []
    for g, win in enumerate(POOL_WINDOWS):
        sl = slice(g * POOL_GROUP, (g + 1) * POOL_GROUP)
        tot = cs[:, POOL_KEEP + 1:POOL_KEEP + 1 + L, sl] - cs[:, POOL_KEEP + 1 - win:POOL_KEEP + 1 - win + L, sl]
        cnt = jnp.minimum(win, pos + 1).astype(jnp.float32)
        means.append(tot / cnt[None, :, None])
    d = (jnp.concatenate(means, axis=-1) - ef[:, POOL_KEEP:]).astype(u.dtype)
    y = jnp.einsum('blgc,gcd->blgd', d.reshape(B_, L, len(POOL_WINDOWS), POOL_GROUP), w).reshape(B_, L, C)
    return y * scale, ext[:, L:]


def rglru_mixer(xb, gb, conv_prev, h0, conv_w, conv_b, w_a, b_a, w_x, b_x, lam):
    B_, L, C = xb.shape
    xc, conv_new = causal_dwconv(xb, conv_prev, conv_w, conv_b)
    xh = xc.reshape(B_, L, RG_HEADS, RG_BLOCK)
    r = jax.nn.sigmoid(jnp.einsum('blhi,hij->blhj', xh, w_a).reshape(B_, L, C) + b_a)
    ig = jax.nn.sigmoid(jnp.einsum('blhi,hij->blhj', xh, w_x).reshape(B_, L, C) + b_x)
    log_a = -RG_C * r.astype(jnp.float32) * jax.nn.softplus(-lam.astype(jnp.float32))
    a = jnp.exp(log_a)
    bt = jnp.sqrt(-jnp.expm1(2.0 * log_a)) * (ig * xc).astype(jnp.float32)
    bt = bt.at[:, 0].add(a[:, 0] * h0.astype(jnp.float32))
    _, h = lax.associative_scan(lambda e1, e2: (e1[0] * e2[0], e2[0] * e1[1] + e2[1]), (a, bt), axis=1)
    y = h.astype(xb.dtype) * jax.nn.gelu(gb)
    return y, conv_new, h[:, -1].astype(xb.dtype)


def masked_softmax(s, mask):
    s = jnp.where(mask, s.astype(jnp.float32), -jnp.inf)
    m = jnp.max(s, axis=-1, keepdims=True)
    e = jnp.exp(s - jnp.where(jnp.isfinite(m), m, 0.0))
    d = jnp.sum(e, axis=-1, keepdims=True)
    return e / jnp.where(d > 0, d, 1.0)


def nsa_compress(k_raw, v_raw, phi, phi_b, g_kc):
    B_, T = k_raw.shape[:2]
    R = CMP_BLOCK // CMP_STRIDE
    nch = T // CMP_STRIDE
    ncmp = nch - (R - 1)

    def comp(a, w, bias):
        ch = a[:, :nch * CMP_STRIDE].reshape(B_, nch, CMP_STRIDE, N_KV, HEAD_DIM)
        ch = ch.transpose(0, 1, 3, 2, 4).reshape(B_, nch, N_KV, CMP_STRIDE * HEAD_DIM)
        wr = w.reshape(R, CMP_STRIDE * HEAD_DIM, HEAD_DIM)
        out = jnp.einsum('bckf,fd->bckd', ch[:, 0:ncmp], wr[0])
        for r in range(1, R):
            out = out + jnp.einsum('bckf,fd->bckd', ch[:, r:r + ncmp], wr[r])
        return out + bias

    kc = rmsnorm(comp(k_raw, phi[0], phi_b[0]), g_kc)
    vc = comp(v_raw, phi[1], phi_b[1])
    cmp_end = jnp.arange(ncmp) * CMP_STRIDE + (CMP_BLOCK - 1)
    return kc, vc, cmp_end


def sel_blocks(a):
    B_, T = a.shape[:2]
    n_sel = -(-T // SEL_BLOCK)
    a = jnp.pad(a, ((0, 0), (0, n_sel * SEL_BLOCK - T), (0, 0), (0, 0)))
    return a.reshape(B_, n_sel, SEL_BLOCK, N_KV, HEAD_DIM).transpose(0, 3, 1, 2, 4)


def nsa_attend(q, q_pos, gates, kc, vc, cmp_end, ks_blk, vs_blk, kw, vw, w_pos):
    dt = q.dtype
    B_, Q = q.shape[:2]
    t = q_pos[:, None]
    s = jnp.einsum('bqkgd,bckd->bqkgc', q, kc)
    p_cmp = masked_softmax(s, (cmp_end[None, :] <= t)[None, :, None, None, :])
    o_cmp = jnp.einsum('bqkgc,bckd->bqkgd', p_cmp.astype(dt), vc)
    n_sel = ks_blk.shape[2]
    ci = jnp.arange(kc.shape[1])[:, None] * CMP_STRIDE
    sj = jnp.arange(n_sel)[None, :] * SEL_BLOCK
    overlap = ((ci < sj + SEL_BLOCK) & (ci + CMP_BLOCK > sj)).astype(jnp.float32)
    imp = jnp.einsum('bqkgc,cs->bqks', p_cmp, overlap)
    blk = jnp.arange(n_sel)[None, :]
    cur = t // SEL_BLOCK
    valid = blk <= cur
    forced = (blk == 0) | (blk == cur) | (blk == cur - 1)
    score = jnp.where(valid[None, :, None, :], imp, -jnp.inf)
    score = jnp.where((forced & valid)[None, :, None, :], jnp.inf, score)
    top_v, top_i = lax.top_k(score, min(SEL_TOPK, n_sel))
    kk = top_i.shape[-1]
    bi = jnp.arange(B_)[:, None, None, None]
    hi = jnp.arange(N_KV)[None, None, :, None]
    ks = ks_blk[bi, hi, top_i].reshape(B_, Q, N_KV, kk * SEL_BLOCK, HEAD_DIM)
    vs = vs_blk[bi, hi, top_i].reshape(B_, Q, N_KV, kk * SEL_BLOCK, HEAD_DIM)
    spos = (top_i[..., None] * SEL_BLOCK + jnp.arange(SEL_BLOCK)).reshape(B_, Q, N_KV, kk * SEL_BLOCK)
    smask = (spos <= q_pos[None, :, None, None]) & jnp.repeat(top_v > -jnp.inf, SEL_BLOCK, axis=-1)
    s = jnp.einsum('bqkgd,bqknd->bqkgn', q, ks)
    o_sel = jnp.einsum('bqkgn,bqknd->bqkgd', masked_softmax(s, smask[:, :, :, None, :]).astype(dt), vs)
    wd = t - w_pos[None, :]
    wmask = (w_pos[None, :] >= 0) & (wd >= 0) & (wd <= WINDOW)
    s = jnp.einsum('bqkgd,bnkd->bqkgn', q, kw)
    o_win = jnp.einsum('bqkgn,bnkd->bqkgd', masked_softmax(s, wmask[None, :, None, None, :]).astype(dt), vw)
    return gates[..., 0:1] * o_cmp + gates[..., 1:2] * o_sel + gates[..., 2:3] * o_win


def nsa_prompt(q, gates, kc_raw, vc_raw, ksel, vsel, kwin, vwin, phi, phi_b, g_kc):
    B_, S = q.shape[:2]
    kc, vc, cmp_end = nsa_compress(kc_raw, vc_raw, phi, phi_b, g_kc)
    ks_blk, vs_blk = sel_blocks(ksel), sel_blocks(vsel)
    zpad = jnp.zeros((B_, WINDOW, N_KV, HEAD_DIM), kwin.dtype)
    kw_pad = jnp.concatenate([zpad, kwin], axis=1)
    vw_pad = jnp.concatenate([zpad, vwin], axis=1)
    nq = S // Q_BLOCK

    def body(args):
        qc, gc, i = args
        start = i * Q_BLOCK
        kw = lax.dynamic_slice_in_dim(kw_pad, start, WINDOW + Q_BLOCK, axis=1)
        vw = lax.dynamic_slice_in_dim(vw_pad, start, WINDOW + Q_BLOCK, axis=1)
        return nsa_attend(qc, start + jnp.arange(Q_BLOCK), gc, kc, vc, cmp_end, ks_blk, vs_blk,
                          kw, vw, start - WINDOW + jnp.arange(WINDOW + Q_BLOCK))

    qb = q.reshape(B_, nq, Q_BLOCK, N_KV, GQA, HEAD_DIM).swapaxes(0, 1)
    gb = gates.reshape(B_, nq, Q_BLOCK, N_KV, GQA, 3).swapaxes(0, 1)
    o = lax.map(body, (qb, gb, jnp.arange(nq)))
    o = o.swapaxes(0, 1).reshape(B_, S, N_HEADS * HEAD_DIM)
    rows = jnp.stack([kc_raw, vc_raw, ksel, vsel], axis=2)
    win_new = jnp.stack([kwin, vwin], axis=2)[:, S - min(WINDOW, S):]
    return o, rows, win_new


def nsa_sample(pool, page_table, win_buf, q, gates, kc_raw, vc_raw, ksel, vsel, kwin, vwin, phi, phi_b, g_kc):
    B_, L = q.shape[:2]
    past = pool[page_table]
    past = past.reshape(B_, past.shape[1] * past.shape[2], 4, N_KV, HEAD_DIM)
    P = past.shape[1]
    rows = jnp.stack([kc_raw, vc_raw, ksel, vsel], axis=2)
    full = jnp.concatenate([past.astype(rows.dtype), rows], axis=1)
    kc, vc, cmp_end = nsa_compress(full[:, :, 0], full[:, :, 1], phi, phi_b, g_kc)
    ks_blk, vs_blk = sel_blocks(full[:, :, 2]), sel_blocks(full[:, :, 3])
    Lw = win_buf.shape[1]
    new_w = jnp.stack([kwin, vwin], axis=2)
    wfull = jnp.concatenate([win_buf.astype(new_w.dtype), new_w], axis=1)
    o = nsa_attend(q, P + jnp.arange(L), gates, kc, vc, cmp_end, ks_blk, vs_blk,
                   wfull[:, :, 0], wfull[:, :, 1], P - Lw + jnp.arange(Lw + L))
    return o.reshape(B_, L, N_HEADS * HEAD_DIM), rows, wfull[:, L:]


def expert_dispatch(xt, eidx, gate, w_gu, w_down):
    T, D = xt.shape
    M = T * TOP_E
    fe = eidx.reshape(M)
    ftok = jnp.arange(M, dtype=jnp.int32) // TOP_E
    fgate = gate.reshape(M)
    order = jnp.argsort(fe)
    se, stok, sgate = fe[order], ftok[order], fgate[order]
    counts = jnp.bincount(fe, length=N_EXPERTS)
    padded = (counts + MOE_BLOCK - 1) // MOE_BLOCK * MOE_BLOCK
    pad_end = jnp.cumsum(padded)
    pad_start = pad_end - padded
    start = jnp.cumsum(counts) - counts
    dest = pad_start[se] + jnp.arange(M) - start[se]
    n_blk = -(-M // MOE_BLOCK) + N_EXPERTS
    P = n_blk * MOE_BLOCK
    buf_tok = jnp.zeros((P,), jnp.int32).at[dest].set(stok)
    buf_gate = jnp.zeros((P,), fgate.dtype).at[dest].set(sgate)
    blk_exp = jnp.minimum(jnp.searchsorted(pad_end, jnp.arange(n_blk) * MOE_BLOCK, side='right'), N_EXPERTS - 1)
    xb = xt[buf_tok].reshape(n_blk, MOE_BLOCK, D)

    def run(args):
        xi, e = args
        a, b = jnp.split(xi @ w_gu[e], 2, axis=-1)
        return (jax.nn.silu(a) * b) @ w_down[e]

    yb = lax.map(run, (xb, blk_exp)).reshape(P, D)
    return jax.ops.segment_sum(yb * buf_gate[:, None].astype(yb.dtype), buf_tok, num_segments=T)


def moe_ffn(x, wg_r, bg_r, we_r, be_r, w_gu, w_down):
    B_, L, D = x.shape
    xt = x.reshape(B_ * L, D)
    T = xt.shape[0]
    lg = (xt @ wg_r + bg_r).astype(jnp.float32)
    pg = jax.nn.softmax(lg, axis=-1)
    gsel = jnp.argmax(lg, axis=-1)
    p_group = jnp.take_along_axis(pg, gsel[:, None], axis=-1)
    le = (xt @ we_r + be_r).astype(jnp.float32).reshape(T, N_GROUPS, EXP_PER_GROUP)
    le_g = jnp.take_along_axis(le, gsel[:, None, None], axis=1)[:, 0]
    tv, ti = lax.top_k(le_g, TOP_E)
    gate = p_group * jax.nn.softmax(tv, axis=-1)
    eidx = gsel[:, None] * EXP_PER_GROUP + ti
    return expert_dispatch(xt, eidx, gate, w_gu, w_down).reshape(B_, L, D)


def layer_forward(x, pos0, lw, pool_prev, rgc_prev, rgh0, sc_prev, nsa_fn):
    B_, L, _ = x.shape
    w_perm = permute_w_in(lw['w_in']).astype(MXU_DTYPE)
    proj2d = norm_matmul(x.reshape(B_ * L, D_MODEL), lw['norm_mix_g'], w_perm)
    proj = proj2d.reshape(B_, L, N_IN_PAD)
    q, kv, pu, rx, rgate, sc, ng = split_cols(proj, (COL_KV, KV_W, POOL_W, RG_W, RG_W, 3 * SC_W, 3 * N_HEADS))
    y_pool, pool_new = pool_mixer(pu, pool_prev, pos0, lw['pool_w'], lw['pool_scale'])
    y_rg, rgc_new, rgh_new = rglru_mixer(rx, rgate, rgc_prev, rgh0, lw['rg_conv_w'], lw['rg_conv_b'],
                                         lw['rg_w_a'], lw['rg_b_a'], lw['rg_w_x'], lw['rg_b_x'], lw['rg_lambda'])
    qk_g = lw['nsa_qk_g']
    y_nsa, nsa_rows, win_new = nsa_fn(proj2d, lw['nsa_phi'], lw['nsa_phi_b'], qk_g)
    z, bg, cg = split_cols(sc, (SC_W, SC_W, SC_W))
    v, sc_new = causal_dwconv(cg * z, sc_prev, lw['sc_conv_w'], lw['sc_conv_b'])
    y_sc = bg * v
    ymix = jnp.concatenate([y_pool, y_rg, y_nsa, y_sc], axis=-1).reshape(B_ * L, MIX_W)
    x = mix_out_moe(ymix, x.reshape(B_ * L, D_MODEL), lw, MOE_TILE_PROMPT if L > 1 else MOE_TILE_SAMPLE)
    return x.reshape(B_, L, D_MODEL), (nsa_rows, win_new, pool_new, rgc_new, rgh_new, sc_new)


def kernel(x_prompt, x_sample, cache_nsa, state_win_kv, state_pool, state_rg_conv, state_rg_h, state_sc_conv,
           page_table, norm_mix_g, w_in, pool_w, pool_scale, rg_conv_w, rg_conv_b, rg_w_a, rg_b_a, rg_w_x, rg_b_x,
           rg_lambda, nsa_phi, nsa_phi_b, nsa_qk_g, sc_conv_w, sc_conv_b, mix_out_g, w_out, norm_ffn_g,
           router_group_w, router_group_b, router_expert_w, router_expert_b, exp_w_gu, exp_w_down):
    past_len = page_table.shape[1] * cache_nsa.shape[2]
    xp, xs = x_prompt, x_sample
    cache3 = cache_nsa.reshape(DEPTH * cache_nsa.shape[1], PAGE_SIZE, 4 * N_KV * HEAD_DIM)
    win3 = state_win_kv.reshape(DEPTH * state_win_kv.shape[1], state_win_kv.shape[2], 2 * N_KV * HEAD_DIM)
    cache_ab = cache_compress(cache_nsa, nsa_phi)
    Bp = xp.shape[0]
    st_p, st_s = [], []
    for l in range(DEPTH):
        lw = dict(norm_mix_g=norm_mix_g[l], w_in=w_in[l], pool_w=pool_w[l], pool_scale=pool_scale[l],
                  rg_conv_w=rg_conv_w[l], rg_conv_b=rg_conv_b[l], rg_w_a=rg_w_a[l], rg_b_a=rg_b_a[l],
                  rg_w_x=rg_w_x[l], rg_b_x=rg_b_x[l], rg_lambda=rg_lambda[l], nsa_phi=nsa_phi[l],
                  nsa_phi_b=nsa_phi_b[l], nsa_qk_g=nsa_qk_g[l], sc_conv_w=sc_conv_w[l], sc_conv_b=sc_conv_b[l],
                  mix_out_g=mix_out_g[l], w_out=w_out[l], norm_ffn_g=norm_ffn_g[l],
                  router_group_w=router_group_w[l], router_group_b=router_group_b[l],
                  router_expert_w=router_expert_w[l], router_expert_b=router_expert_b[l],
                  exp_w_gu=exp_w_gu[l], exp_w_down=exp_w_down[l])
        xp, sp = layer_forward(xp, 0, lw,
                               jnp.zeros((Bp, POOL_KEEP, POOL_W), xp.dtype),
                               jnp.zeros((Bp, RG_CONV - 1, RG_W), xp.dtype),
                               jnp.zeros((Bp, RG_W), xp.dtype),
                               jnp.zeros((Bp, SC_CONV - 1, SC_W), xp.dtype),
                               lambda p, phi, phi_b, g: nsa_prompt_pallas(p, Bp, xp.shape[1], phi, phi_b, g))
        xs, ss = layer_forward(xs, past_len, lw, state_pool[l], state_rg_conv[l], state_rg_h[l], state_sc_conv[l],
                               lambda p, phi, phi_b, g: nsa_sample_pallas(p, l, cache3, cache_ab, page_table, win3,
                                                                          phi_b, g))
        st_p.append(sp)
        st_s.append(ss)

    def stk(lst, i):
        return jnp.stack([s[i] for s in lst])

    return (xp, xs, stk(st_p, 0), stk(st_s, 0), stk(st_p, 1), stk(st_s, 1), stk(st_p, 2), stk(st_s, 2),
            stk(st_p, 3), stk(st_s, 3), stk(st_p, 4), stk(st_s, 4), stk(st_p, 5), stk(st_s, 5))
```

```python
import functools
import jax, jax.numpy as jnp
from jax import lax
import numpy as np
from jax.experimental import pallas as pl
from jax.experimental.pallas import tpu as pltpu

D_MODEL = 1024
BATCH = 4
SEQ = 4096
DEPTH = 2
DEC_BATCH = 128
DEC_SEQ = 1
PAST_LEN = 2048
PAGE_SIZE = 128

MIX_W = D_MODEL
GROUP_W = MIX_W // 4
POOL_W = GROUP_W
POOL_WINDOWS = (2, 4, 8, 16)
POOL_GROUP = POOL_W // len(POOL_WINDOWS)
POOL_KEEP = max(POOL_WINDOWS) - 1
RG_W = GROUP_W
RG_HEADS = 4
RG_BLOCK = RG_W // RG_HEADS
RG_CONV = 4
RG_C = 8.0
HEAD_DIM = 64
N_HEADS = GROUP_W // HEAD_DIM
N_KV = 2
GQA = N_HEADS // N_KV
CMP_BLOCK = 32
CMP_STRIDE = 16
SEL_BLOCK = 64
SEL_TOPK = 16
WINDOW = 512
Q_BLOCK = 128
SC_W = GROUP_W
SC_CONV = 3
N_GROUPS = 4
EXP_PER_GROUP = 8
N_EXPERTS = N_GROUPS * EXP_PER_GROUP
TOP_E = 2
D_EXPERT = 512
MOE_BLOCK = 128
EPS = 1e-6
SPLIT_SIZES = (POOL_W, RG_W, RG_W, N_HEADS * HEAD_DIM, 6 * N_KV * HEAD_DIM, 3 * N_HEADS, 3 * SC_W)
N_IN = sum(SPLIT_SIZES)

LANE = 128
ROW_TILE = 512
VMEM_LIMIT = 48 * 1024 * 1024
MXU_DTYPE = jnp.bfloat16
F32 = jnp.float32
NEG = -1e30

KV_W = 6 * N_KV * HEAD_DIM
COL_Q = 0
COL_KV = COL_Q + N_HEADS * HEAD_DIM
COL_POOL = COL_KV + KV_W
COL_RX = COL_POOL + POOL_W
COL_RGATE = COL_RX + RG_W
COL_SC = COL_RGATE + RG_W
COL_NG = COL_SC + 3 * SC_W
N_IN_PAD = COL_NG + LANE
SEL_TILE = 256
N_SEL_PROMPT = SEQ // SEL_BLOCK


def _cparams(n_axes=1):
    return pltpu.CompilerParams(dimension_semantics=("arbitrary",) * n_axes, vmem_limit_bytes=VMEM_LIMIT)


def _mm(a, b):
    return jnp.dot(a.astype(MXU_DTYPE), b.astype(MXU_DTYPE), preferred_element_type=F32)


def _mm_nt(a, b):
    return lax.dot_general(a.astype(MXU_DTYPE), b.astype(MXU_DTYPE), (((1,), (1,)), ((), ())),
                           preferred_element_type=F32)


def permute_w_in(w):
    pu, rx, rgate, q, kv, ng, sc = split_cols(w, SPLIT_SIZES)
    pad = jnp.zeros((w.shape[0], LANE - ng.shape[1]), w.dtype)
    return jnp.concatenate([q, kv, pu, rx, rgate, sc, ng, pad], axis=1)


def _norm_matmul_body(x_ref, g_ref, w_ref, o_ref):
    xf = x_ref[...]
    h = xf * lax.rsqrt(jnp.mean(xf * xf, axis=-1, keepdims=True) + EPS) * g_ref[...]
    o_ref[...] = _mm(h, w_ref[...])


def norm_matmul(x2d, g, w):
    T, D = x2d.shape
    N = w.shape[1]
    tm = min(ROW_TILE, T)
    return pl.pallas_call(
        _norm_matmul_body,
        grid=(T // tm,),
        in_specs=[pl.BlockSpec((tm, D), lambda i: (i, 0)),
                  pl.BlockSpec((1, D), lambda i: (0, 0)),
                  pl.BlockSpec((D, N), lambda i: (0, 0))],
        out_specs=pl.BlockSpec((tm, N), lambda i: (i, 0)),
        out_shape=jax.ShapeDtypeStruct((T, N), F32),
        compiler_params=_cparams(),
        name="norm_in_proj",
    )(x2d, g.reshape(1, D), w)


def _seg_rmsnorm(x, g):
    x2 = x * x
    left = lax.broadcasted_iota(jnp.int32, x.shape, 1) < HEAD_DIM
    s_l = jnp.sum(jnp.where(left, x2, 0.0), axis=-1, keepdims=True)
    s_r = jnp.sum(jnp.where(left, 0.0, x2), axis=-1, keepdims=True)
    ms = jnp.where(left, s_l, s_r) * (1.0 / HEAD_DIM)
    return x * lax.rsqrt(ms + EPS) * g


def _nsa_prep_body(qkv_ref, ng_ref, g_ref, perm_ref, qa_ref, kvb_ref, rawb_ref, rows_t_ref, win_t_ref, win_ref,
                   gates_ref):
    g = g_ref[...]
    for hb in range(N_KV):
        qn = _seg_rmsnorm(qkv_ref[:, COL_Q + hb * LANE:COL_Q + (hb + 1) * LANE], g[0:1]) * (HEAD_DIM ** -0.5)
        qa_ref[:, hb * 2 * LANE:(hb + 1) * 2 * LANE] = _mm(qn, perm_ref[hb]).astype(qa_ref.dtype)
    comp = [qkv_ref[:, COL_KV + c * LANE:COL_KV + (c + 1) * LANE] for c in range(6)]
    comp[2] = _seg_rmsnorm(comp[2], g[2:3])
    comp[4] = _seg_rmsnorm(comp[4], g[3:4])
    for c in range(6):
        kvb_ref[:, c * LANE:(c + 1) * LANE] = comp[c].astype(kvb_ref.dtype)
    for c in range(2):
        rawb_ref[:, c * LANE:(c + 1) * LANE] = comp[c].astype(rawb_ref.dtype)
    for c in range(4):
        rows_t_ref[0, c * LANE:(c + 1) * LANE, :] = comp[c].T
    for c in range(2):
        win_t_ref[0, c * LANE:(c + 1) * LANE, :] = comp[4 + c].T
        win_ref[:, c * LANE:(c + 1) * LANE] = comp[4 + c]
    gates_ref[...] = jax.nn.sigmoid(ng_ref[...])


def _q_place_matrices():
    p = np.zeros((N_KV, LANE, 2 * LANE), np.float32)
    for hb in range(N_KV):
        for gq in range(GQA):
            for d in range(HEAD_DIM):
                p[hb, gq * HEAD_DIM + d, gq * LANE + hb * HEAD_DIM + d] = 1.0
    return jnp.asarray(p, MXU_DTYPE)


def nsa_prep(proj, qk_g, B_, S):
    T = proj.shape[0]
    tm = min(ROW_TILE, S)
    tpb = S // tm
    qkv_w = COL_POOL
    g4 = jnp.tile(qk_g, (1, 2))
    return pl.pallas_call(
        _nsa_prep_body,
        grid=(T // tm,),
        in_specs=[pl.BlockSpec((tm, qkv_w), lambda i: (i, 0)),
                  pl.BlockSpec((tm, LANE), lambda i: (i, COL_NG // LANE)),
                  pl.BlockSpec((4, LANE), lambda i: (0, 0)),
                  pl.BlockSpec((N_KV, LANE, 2 * LANE), lambda i: (0, 0, 0))],
        out_specs=[pl.BlockSpec((tm, 4 * LANE), lambda i: (i, 0)),
                   pl.BlockSpec((tm, 6 * LANE), lambda i: (i, 0)),
                   pl.BlockSpec((tm, 2 * LANE), lambda i: (i, 0)),
                   pl.BlockSpec((1, 4 * LANE, tm), lambda i: (i // tpb, 0, i % tpb)),
                   pl.BlockSpec((1, 2 * LANE, tm), lambda i: (i // tpb, 0, i % tpb)),
                   pl.BlockSpec((tm, 2 * LANE), lambda i: (i, 0)),
                   pl.BlockSpec((tm, LANE), lambda i: (i, 0))],
        out_shape=[jax.ShapeDtypeStruct((T, 4 * LANE), MXU_DTYPE),
                   jax.ShapeDtypeStruct((T, 6 * LANE), MXU_DTYPE),
                   jax.ShapeDtypeStruct((T, 2 * LANE), MXU_DTYPE),
                   jax.ShapeDtypeStruct((B_, 4 * LANE, S), F32),
                   jax.ShapeDtypeStruct((B_, 2 * LANE, S), F32),
                   jax.ShapeDtypeStruct((T, 2 * LANE), F32),
                   jax.ShapeDtypeStruct((T, LANE), F32)],
        compiler_params=_cparams(),
        name="nsa_prep",
    )(proj, proj, g4, _q_place_matrices())


def compress_weights(phi):
    R = CMP_BLOCK // CMP_STRIDE
    wr = phi.reshape(2, R, CMP_STRIDE, HEAD_DIM, HEAD_DIM)
    eye = jnp.eye(2, dtype=phi.dtype)
    w = jnp.einsum('crjde,cx,hy->rjchdxye', wr, eye, eye)
    return w.reshape(R, CMP_STRIDE * 2 * LANE, 2 * LANE).astype(MXU_DTYPE)


def _compress_body(x_ref, w_ref, b_ref, g_ref, kc_ref, vc_ref):
    x = x_ref[0]
    nch = x.shape[0]
    a = _mm(x, w_ref[0])
    bm = _mm(x, w_ref[1])
    out = a + pltpu.roll(bm, nch - 1, 0) + b_ref[...]
    kc_ref[0] = _seg_rmsnorm(out[:, 0:LANE], g_ref[...]).astype(kc_ref.dtype)
    vc_ref[0] = out[:, LANE:2 * LANE].astype(vc_ref.dtype)


def nsa_compress_pallas(rawb3, wc, phi_b, g_kc):
    B_, nch, K = rawb3.shape
    bias = jnp.concatenate([jnp.tile(phi_b[0], 2), jnp.tile(phi_b[1], 2)]).reshape(1, 2 * LANE)
    return pl.pallas_call(
        _compress_body,
        grid=(B_,),
        in_specs=[pl.BlockSpec((1, nch, K), lambda b: (b, 0, 0)),
                  pl.BlockSpec(wc.shape, lambda b: (0, 0, 0)),
                  pl.BlockSpec((1, 2 * LANE), lambda b: (0, 0)),
                  pl.BlockSpec((1, LANE), lambda b: (0, 0))],
        out_specs=[pl.BlockSpec((1, nch, LANE), lambda b: (b, 0, 0)),
                   pl.BlockSpec((1, nch, LANE), lambda b: (b, 0, 0))],
        out_shape=[jax.ShapeDtypeStruct((B_, nch, LANE), MXU_DTYPE),
                   jax.ShapeDtypeStruct((B_, nch, LANE), MXU_DTYPE)],
        compiler_params=_cparams(),
        name="nsa_compress",
    )(rawb3, wc, bias, jnp.tile(g_kc, 2).reshape(1, LANE))


def _online_update(carry, s, mask, v):
    m, l, acc = carry
    s = jnp.where(mask, s, NEG)
    m_new = jnp.maximum(m, jnp.max(s, axis=-1, keepdims=True))
    alpha = jnp.exp(m - m_new)
    p = jnp.exp(s - m_new)
    l = alpha * l + jnp.sum(p, axis=-1, keepdims=True)
    acc = alpha * acc + _mm(p, v)
    return m_new, l, acc


def _select_blocks(imp, start):
    n_sel = N_SEL_PROMPT
    sc_t = imp.T[0:n_sel]
    blk = lax.broadcasted_iota(jnp.int32, sc_t.shape, 0)
    cur = (start + lax.broadcasted_iota(jnp.int32, sc_t.shape, 1)) // SEL_BLOCK
    valid = blk <= cur
    forced = (blk == 0) | (blk == cur) | (blk == cur - 1)
    score = jnp.where(valid, sc_t, -jnp.inf)
    score = jnp.where(forced & valid, jnp.inf, score)
    cnt = jnp.zeros(sc_t.shape, F32)
    for i in range(n_sel):
        ri = score[i:i + 1, :]
        beats = (ri > score) | ((ri == score) & (blk > i))
        cnt = cnt + jnp.where(beats, 1.0, 0.0)
    sel_t = jnp.where((cnt < SEL_TOPK) & (score > -jnp.inf), 1.0, 0.0)
    sel_t = jnp.concatenate([sel_t, jnp.zeros((LANE - n_sel, sc_t.shape[1]), F32)], axis=0)
    return sel_t.T


def _nsa_attn_body(qa_ref, gates_ref, kc_ref, vc_ref, kv_ref, ov_ref, e_ref, o_ref):
    i = pl.program_id(1)
    start = i * Q_BLOCK
    Q = Q_BLOCK
    R = GQA * Q
    t_row = start + lax.broadcasted_iota(jnp.int32, (R, 1), 0) % Q
    gates = gates_ref[...]
    lane_q = lax.broadcasted_iota(jnp.int32, (Q, LANE), 1)
    for h in range(N_KV):
        qs = jnp.concatenate([qa_ref[:, (h * GQA + gq) * LANE:(h * GQA + gq + 1) * LANE] for gq in range(GQA)],
                             axis=0)
        kc = kc_ref[0]
        ncmp = kc.shape[0]
        s = _mm_nt(qs, kc)
        cmp_end = lax.broadcasted_iota(jnp.int32, (R, ncmp), 1) * CMP_STRIDE + (CMP_BLOCK - 1)
        s = jnp.where(cmp_end <= t_row, s, -jnp.inf)
        m = jnp.max(s, axis=-1, keepdims=True)
        e = jnp.exp(s - jnp.where(m > -jnp.inf, m, 0.0))
        d = jnp.sum(e, axis=-1, keepdims=True)
        p_cmp = e / jnp.where(d > 0, d, 1.0)
        o_cmp = _mm(p_cmp, vc_ref[0])
        imp = _mm(p_cmp[0:Q], ov_ref[...]) + _mm(p_cmp[Q:R], ov_ref[...])
        sel = _select_blocks(imp, start).astype(MXU_DTYPE)

        def sel_step(j, carry):
            off = pl.multiple_of(j * SEL_TILE, SEL_TILE)
            k = kv_ref[pl.ds(off, SEL_TILE), 2 * LANE:3 * LANE]
            v = kv_ref[pl.ds(off, SEL_TILE), 3 * LANE:4 * LANE]
            sj = _mm_nt(qs, k)
            msel = _mm(sel, e_ref[j])
            msel = jnp.concatenate([msel] * GQA, axis=0)
            kpos = off + lax.broadcasted_iota(jnp.int32, (R, SEL_TILE), 1)
            return _online_update(carry, sj, (msel > 0.5) & (kpos <= t_row), v)

        init = (jnp.full((R, 1), NEG, F32), jnp.zeros((R, 1), F32), jnp.zeros((R, LANE), F32))
        n_tiles = (start + Q + SEL_TILE - 1) // SEL_TILE
        _, l_s, acc_s = lax.fori_loop(0, n_tiles, sel_step, init)
        o_sel = acc_s / l_s

        carry = init
        for kk in range(WINDOW // Q + 1):
            tile = i - kk
            off = pl.multiple_of(jnp.maximum(tile, 0) * Q, Q)
            k = kv_ref[pl.ds(off, Q), 4 * LANE:5 * LANE]
            v = kv_ref[pl.ds(off, Q), 5 * LANE:6 * LANE]
            sj = _mm_nt(qs, k)
            wd = t_row - (tile * Q + lax.broadcasted_iota(jnp.int32, (R, Q), 1))
            carry = _online_update(carry, sj, (tile >= 0) & (wd >= 0) & (wd <= WINDOW), v)
        o_win = carry[2] / carry[1]

        outs = []
        for gq in range(GQA):
            c0 = (h * GQA + gq) * 3
            rs = slice(gq * Q, (gq + 1) * Q)
            og = (gates[:, c0:c0 + 1] * o_cmp[rs] + gates[:, c0 + 1:c0 + 2] * o_sel[rs]
                  + gates[:, c0 + 2:c0 + 3] * o_win[rs])
            outs.append(og if gq == h else pltpu.roll(og, HEAD_DIM, 1))
        o_ref[:, h * LANE:(h + 1) * LANE] = jnp.where(lane_q < HEAD_DIM, outs[0], outs[1])


def _sel_constants(S):
    ncmp_rows = S // CMP_STRIDE
    ci = np.arange(ncmp_rows)[:, None] * CMP_STRIDE
    sj = np.arange(LANE)[None, :] * SEL_BLOCK
    ov = ((ci < sj + SEL_BLOCK) & (ci + CMP_BLOCK > sj) & (np.arange(LANE)[None, :] < S // SEL_BLOCK))
    n_t = S // SEL_TILE
    key_blk = (np.arange(n_t)[:, None, None] * SEL_TILE + np.arange(SEL_TILE)[None, None, :]) // SEL_BLOCK
    e = (np.arange(LANE)[None, :, None] == key_blk)
    return jnp.asarray(ov, MXU_DTYPE), jnp.asarray(e, MXU_DTYPE)


def nsa_attn_prompt(qa, gates, kc, vc, kvb, B_, S):
    nq = S // Q_BLOCK
    nch = S // CMP_STRIDE
    ov, e3 = _sel_constants(S)
    return pl.pallas_call(
        _nsa_attn_body,
        grid=(B_, nq),
        in_specs=[pl.BlockSpec((Q_BLOCK, 4 * LANE), lambda b, i: (b * nq + i, 0)),
                  pl.BlockSpec((Q_BLOCK, LANE), lambda b, i: (b * nq + i, 0)),
                  pl.BlockSpec((1, nch, LANE), lambda b, i: (b, 0, 0)),
                  pl.BlockSpec((1, nch, LANE), lambda b, i: (b, 0, 0)),
                  pl.BlockSpec((S, 6 * LANE), lambda b, i: (b, 0)),
                  pl.BlockSpec(ov.shape, lambda b, i: (0, 0)),
                  pl.BlockSpec(e3.shape, lambda b, i: (0, 0, 0))],
        out_specs=pl.BlockSpec((Q_BLOCK, 2 * LANE), lambda b, i: (b * nq + i, 0)),
        out_shape=jax.ShapeDtypeStruct((B_ * S, N_HEADS * HEAD_DIM), F32),
        compiler_params=_cparams(2),
        name="nsa_attn_prompt",
    )(qa, gates, kc, vc, kvb, ov, e3)


def nsa_prompt_pallas(proj, B_, S, phi, phi_b, qk_g):
    qa, kvb, rawb, rows_t, win_t, _, gates = nsa_prep(proj, qk_g, B_, S)
    nch = S // CMP_STRIDE
    kc, vc = nsa_compress_pallas(rawb.reshape(B_, nch, CMP_STRIDE * 2 * LANE), compress_weights(phi), phi_b, qk_g[1])
    o = nsa_attn_prompt(qa, gates, kc, vc, kvb, B_, S)
    rows = rows_t.reshape(B_, 4, N_KV, HEAD_DIM, S).transpose(0, 4, 1, 2, 3)
    wk = min(WINDOW, S)
    win_new = win_t[:, :, S - wk:].reshape(B_, 2, N_KV, HEAD_DIM, wk).transpose(0, 4, 1, 2, 3)
    return o.reshape(B_, S, N_HEADS * HEAD_DIM), rows, win_new


N_PAGES = PAST_LEN // PAGE_SIZE
N_CHUNK_S = PAST_LEN // CMP_STRIDE
N_SEL_S = -(-(PAST_LEN + DEC_SEQ) // SEL_BLOCK)
CUR_S = PAST_LEN // SEL_BLOCK
QROWS = 8


def compress_weights_paged(phi):
    R = CMP_BLOCK // CMP_STRIDE
    wr = phi.reshape(2, R, CMP_STRIDE, HEAD_DIM, HEAD_DIM)
    eye = jnp.eye(2, dtype=phi.dtype)
    w = jnp.einsum('crjde,cx,hy->jchdrxye', wr, eye, eye)
    return w.reshape(CMP_STRIDE, 2 * LANE, R * 2 * LANE).astype(MXU_DTYPE)


def _softmax_with_extra(s, s_new):
    m = jnp.maximum(jnp.max(s, axis=-1, keepdims=True), s_new)
    e = jnp.exp(s - m)
    e_new = jnp.exp(s_new - m)
    return e, e_new, jnp.sum(e, axis=-1, keepdims=True) + e_new


CHUNKS_PER_PAGE = PAGE_SIZE // CMP_STRIDE
SWEEP_PAGES = 64


def feature_major_pages(cache_nsa):
    d, n = cache_nsa.shape[:2]
    return cache_nsa.transpose(0, 1, 3, 4, 5, 2).reshape(d * n, 4, N_KV * HEAD_DIM, PAGE_SIZE)


def _cache_compress_body(c_ref, w_ref, o_ref, sk, sv):
    n_pages = c_ref.shape[0]

    def to_row_major(p, carry):
        r0 = pl.multiple_of(p * PAGE_SIZE, PAGE_SIZE)
        sk[pl.ds(r0, PAGE_SIZE), :] = c_ref[p, 0].T
        sv[pl.ds(r0, PAGE_SIZE), :] = c_ref[p, 1].T
        return carry

    lax.fori_loop(0, n_pages, to_row_major, 0)
    n = n_pages * CHUNKS_PER_PAGE
    acc = jnp.zeros((n, 4 * LANE), F32)
    for j in range(CMP_STRIDE):
        xj = jnp.concatenate([sk[pl.ds(j, n, stride=CMP_STRIDE), :], sv[pl.ds(j, n, stride=CMP_STRIDE), :]], axis=1)
        acc = acc + _mm(xj, w_ref[0, j])
    o_ref[...] = acc


def cache_compress(cache_fm, nsa_phi):
    n_total = cache_fm.shape[0]
    assert (n_total // DEPTH) % SWEEP_PAGES == 0
    tiles = n_total // DEPTH // SWEEP_PAGES
    wc = jnp.stack([compress_weights_paged(nsa_phi[l]) for l in range(DEPTH)])
    rows = SWEEP_PAGES * PAGE_SIZE
    return pl.pallas_call(
        _cache_compress_body,
        grid=(DEPTH * tiles,),
        in_specs=[pl.BlockSpec((SWEEP_PAGES, 2, LANE, PAGE_SIZE), lambda i: (i, 0, 0, 0)),
                  pl.BlockSpec((1,) + wc.shape[1:], lambda i: (i // tiles, 0, 0, 0))],
        out_specs=pl.BlockSpec((SWEEP_PAGES * CHUNKS_PER_PAGE, 4 * LANE), lambda i: (i, 0)),
        out_shape=jax.ShapeDtypeStruct((n_total * CHUNKS_PER_PAGE, 4 * LANE), F32),
        scratch_shapes=[pltpu.VMEM((rows, LANE), F32), pltpu.VMEM((rows, LANE), F32)],
        compiler_params=_cparams(),
        name="cache_compress",
    )(cache_fm, wc)


def _nsa_sample_body(pt_ref, qa_ref, newb_ref, wnew_ref, gates_ref, *rest):
    pages = rest[:N_PAGES]
    abs_ = rest[N_PAGES:2 * N_PAGES]
    win_ref, bias_ref, gkc_ref, ov_ref, e_ref, y_ref, wout_ref = rest[2 * N_PAGES:]
    qs = qa_ref[0]
    newb = newb_ref[0].astype(F32)
    lane = lax.broadcasted_iota(jnp.int32, (QROWS, LANE), 1)
    row = lax.broadcasted_iota(jnp.int32, (QROWS, LANE), 0)

    ab = jnp.concatenate([a[...] for a in abs_], axis=0)
    out = ab[:, 0:2 * LANE] + pltpu.roll(ab[:, 2 * LANE:4 * LANE], N_CHUNK_S - 1, 0) + bias_ref[...]
    kc = _seg_rmsnorm(out[:, 0:LANE], gkc_ref[...])
    vc = out[:, LANE:2 * LANE]

    s = _mm_nt(qs, kc)
    s = jnp.where(lane < N_CHUNK_S - 1, s, -jnp.inf)
    e = jnp.exp(s - jnp.max(s, axis=-1, keepdims=True))
    p_cmp = e / jnp.sum(e, axis=-1, keepdims=True)
    o_cmp = _mm(p_cmp, vc)
    imp = _mm(p_cmp, ov_ref[...])
    imp = imp + jnp.where(row % GQA == 0, pltpu.roll(imp, QROWS - 1, 0), pltpu.roll(imp, 1, 0))

    valid = lane <= CUR_S
    forced = (lane == 0) | (lane == CUR_S) | (lane == CUR_S - 1)
    score = jnp.where(valid, imp, -jnp.inf)
    score = jnp.where(forced & valid, jnp.inf, score)
    cnt = jnp.zeros((QROWS, LANE), F32)
    for i in range(N_SEL_S):
        ci = score[:, i:i + 1]
        cnt = cnt + jnp.where((ci > score) | ((ci == score) & (lane > i)), 1.0, 0.0)
    sel = jnp.where((cnt < SEL_TOPK) & (score > -jnp.inf), 1.0, 0.0)

    msel = _mm(sel, e_ref[...])
    s = jnp.concatenate([_mm(qs, pg[0, 0]) for pg in pages], axis=1)
    s = jnp.where(msel > 0.5, s, NEG)
    qf = qs.astype(F32)
    s_new = jnp.sum(qf * newb[:, 2 * LANE:3 * LANE], axis=-1, keepdims=True)
    s_new = jnp.where(sel[:, CUR_S:CUR_S + 1] > 0.5, s_new, NEG)
    e, e_new, d = _softmax_with_extra(s, s_new)
    acc_o = e_new.astype(MXU_DTYPE).astype(F32) * newb[:, 3 * LANE:4 * LANE]
    for p, pg in enumerate(pages):
        acc_o = acc_o + _mm_nt(e[:, p * PAGE_SIZE:(p + 1) * PAGE_SIZE], pg[0, 1])
    o_sel = acc_o / d

    s = _mm(qs, win_ref[0, 0])
    s_new = jnp.sum(qf * newb[:, 4 * LANE:5 * LANE], axis=-1, keepdims=True)
    e, e_new, d = _softmax_with_extra(s, s_new)
    o_win = (_mm_nt(e, win_ref[0, 1]) + e_new.astype(MXU_DTYPE).astype(F32) * newb[:, 5 * LANE:6 * LANE]) / d

    g = gates_ref[0]
    o = g[:, 0:1] * o_cmp + g[:, 1:2] * o_sel + g[:, 2:3] * o_win
    o_sw = pltpu.roll(o, HEAD_DIM, 1)
    lane1 = lax.broadcasted_iota(jnp.int32, (1, LANE), 1)
    ys = []
    for h in range(N_KV):
        a = (o if h == 0 else o_sw)[GQA * h:GQA * h + 1]
        b = (o if h == 1 else o_sw)[GQA * h + 1:GQA * h + 2]
        ys.append(jnp.where(lane1 < HEAD_DIM, a, b))
    y_ref[0] = jnp.concatenate(ys, axis=1)

    lw = win_ref.shape[3]
    last = lax.broadcasted_iota(jnp.int32, (LANE, lw), 1) == lw - 1
    for c in range(2):
        col = jnp.broadcast_to(wnew_ref[0][:, c * LANE:(c + 1) * LANE], (QROWS, LANE)).T[:, 0:1]
        wout_ref[0, c] = jnp.where(last, col, pltpu.roll(win_ref[0, c], lw - 1, 1))


def _sample_constants():
    ci = np.arange(LANE)[:, None] * CMP_STRIDE
    sj = np.arange(LANE)[None, :] * SEL_BLOCK
    ov = ((ci < sj + SEL_BLOCK) & (ci + CMP_BLOCK > sj) & (np.arange(LANE)[:, None] < N_CHUNK_S - 1)
          & (np.arange(LANE)[None, :] < N_SEL_S))
    e = (np.arange(LANE)[:, None] == (np.arange(PAST_LEN)[None, :] // SEL_BLOCK))
    return jnp.asarray(ov, MXU_DTYPE), jnp.asarray(e, MXU_DTYPE)


def nsa_sample_pallas(proj, layer, cache_fm, cache_ab, page_table, win_fm, phi_b, qk_g):
    B_ = proj.shape[0]
    n_phys = cache_fm.shape[0] // DEPTH
    lw = win_fm.shape[3]
    assert page_table.shape == (B_, N_PAGES) and lw <= WINDOW and lw <= PAST_LEN and CUR_S == N_SEL_S - 1
    qa, kvb, _, rows_t, _, wnew, gates = nsa_prep(proj, qk_g, 1, B_)
    qa8 = jnp.pad(qa.astype(F32).reshape(B_, N_HEADS, LANE), ((0, 0), (0, QROWS - N_HEADS), (0, 0)))
    gates8 = jnp.pad(gates[:, :3 * N_HEADS].reshape(B_, N_HEADS, 3), ((0, 0), (0, QROWS - N_HEADS), (0, LANE - 3)))
    ov, e = _sample_constants()
    bias = jnp.concatenate([jnp.tile(phi_b[0], 2), jnp.tile(phi_b[1], 2)]).reshape(1, 2 * LANE)

    def page_spec(p):
        return pl.BlockSpec((1, 2, LANE, PAGE_SIZE), lambda b, pt: (layer * n_phys + pt[b, p], 1, 0, 0))

    def ab_spec(p):
        return pl.BlockSpec((CHUNKS_PER_PAGE, 4 * LANE), lambda b, pt: (layer * n_phys + pt[b, p], 0))

    def per_b(shape):
        return pl.BlockSpec((1,) + shape, lambda b, pt: (b, 0, 0))

    def const(a):
        return pl.BlockSpec(a.shape, lambda b, pt: (0,) * a.ndim)

    gkc = jnp.tile(qk_g[1], 2).reshape(1, LANE)
    y, wout = pl.pallas_call(
        _nsa_sample_body,
        grid_spec=pltpu.PrefetchScalarGridSpec(
            num_scalar_prefetch=1,
            grid=(B_,),
            in_specs=[per_b((QROWS, LANE)), per_b((1, 6 * LANE)), per_b((1, 2 * LANE)), per_b((QROWS, LANE))]
                     + [page_spec(p) for p in range(N_PAGES)] + [ab_spec(p) for p in range(N_PAGES)]
                     + [pl.BlockSpec((1, 2, LANE, lw), lambda b, pt: (layer * B_ + b, 0, 0, 0)),
                        const(bias), const(gkc), const(ov), const(e)],
            out_specs=[per_b((1, 2 * LANE)), pl.BlockSpec((1, 2, LANE, lw), lambda b, pt: (b, 0, 0, 0))]),
        out_shape=[jax.ShapeDtypeStruct((B_, 1, 2 * LANE), F32),
                   jax.ShapeDtypeStruct((B_, 2, LANE, lw), F32)],
        compiler_params=_cparams(),
        name="nsa_sample",
    )(page_table, qa8, kvb.reshape(B_, 1, 6 * LANE), wnew.reshape(B_, 1, 2 * LANE), gates8,
      *([cache_fm] * N_PAGES), *([cache_ab] * N_PAGES), win_fm, bias, gkc, ov, e)
    rows = rows_t.reshape(4, N_KV, HEAD_DIM, B_).transpose(3, 0, 1, 2)[:, None]
    return (y.reshape(B_, 1, N_HEADS * HEAD_DIM), rows,
            wout.reshape(B_, 2, N_KV, HEAD_DIM, lw).transpose(0, 4, 1, 2, 3))


ROUTE_W = LANE
GROUP_LANE0 = N_EXPERTS
MOE_TILE_PROMPT = 256
MOE_TILE_SAMPLE = 32
COMBINE_TILE = 256


def _rms(x, g):
    return x * lax.rsqrt(jnp.mean(x * x, axis=-1, keepdims=True) + EPS) * g


def _mix_out_router_body(y_ref, x_ref, og_ref, wo_ref, gf_ref, wr_ref, br_ref, x2_ref, xn_ref, route_ref):
    og = og_ref[...]
    yn = jnp.concatenate([_rms(y_ref[:, i * GROUP_W:(i + 1) * GROUP_W], og[:, i * GROUP_W:(i + 1) * GROUP_W])
                          for i in range(4)], axis=1)
    x2 = x_ref[...] + _mm(yn, wo_ref[...])
    x2_ref[...] = x2
    xn = _rms(x2, gf_ref[...])
    xn_ref[...] = xn
    logits = _mm(xn, wr_ref[...]) + br_ref[...]
    lane = lax.broadcasted_iota(jnp.int32, logits.shape, 1)
    is_grp = (lane >= GROUP_LANE0) & (lane < GROUP_LANE0 + N_GROUPS)
    grp = jnp.where(is_grp, logits, -jnp.inf)
    gmax = jnp.max(grp, axis=-1, keepdims=True)
    gsel = jnp.min(jnp.where(grp == gmax, lane - GROUP_LANE0, N_GROUPS), axis=-1, keepdims=True)
    p_group = 1.0 / jnp.sum(jnp.where(is_grp, jnp.exp(logits - gmax), 0.0), axis=-1, keepdims=True)
    le = jnp.where((lane < N_EXPERTS) & (lane // EXP_PER_GROUP == gsel), logits, -jnp.inf)
    m1 = jnp.max(le, axis=-1, keepdims=True)
    i1 = jnp.min(jnp.where(le == m1, lane, LANE), axis=-1, keepdims=True)
    le2 = jnp.where(lane == i1, -jnp.inf, le)
    m2 = jnp.max(le2, axis=-1, keepdims=True)
    i2 = jnp.min(jnp.where(le2 == m2, lane, LANE), axis=-1, keepdims=True)
    e2 = jnp.exp(m2 - m1)
    g1 = p_group * (1.0 / (1.0 + e2))
    g2 = p_group * (e2 / (1.0 + e2))
    route_ref[...] = jnp.where(lane == 0, i1.astype(F32), jnp.where(lane == 1, i2.astype(F32),
                               jnp.where(lane == 2, g1, jnp.where(lane == 3, g2, 0.0))))


def mix_out_router(ymix, x2d, lw):
    T, D = x2d.shape
    tm = min(256, T)
    wr = jnp.concatenate([lw['router_expert_w'], lw['router_group_w'],
                          jnp.zeros((D, ROUTE_W - N_EXPERTS - N_GROUPS), F32)], axis=1).astype(MXU_DTYPE)
    br = jnp.concatenate([lw['router_expert_b'], lw['router_group_b'],
                          jnp.zeros((ROUTE_W - N_EXPERTS - N_GROUPS,), F32)]).reshape(1, ROUTE_W)
    row = lambda i: (i, 0)
    fixed = lambda i: (0, 0)
    return pl.pallas_call(
        _mix_out_router_body,
        grid=(T // tm,),
        in_specs=[pl.BlockSpec((tm, MIX_W), row), pl.BlockSpec((tm, D), row), pl.BlockSpec((1, MIX_W), fixed),
                  pl.BlockSpec((MIX_W, D), fixed), pl.BlockSpec((1, D), fixed), pl.BlockSpec((D, ROUTE_W), fixed),
                  pl.BlockSpec((1, ROUTE_W), fixed)],
        out_specs=[pl.BlockSpec((tm, D), row), pl.BlockSpec((tm, D), row), pl.BlockSpec((tm, ROUTE_W), row)],
        out_shape=[jax.ShapeDtypeStruct((T, D), F32), jax.ShapeDtypeStruct((T, D), F32),
                   jax.ShapeDtypeStruct((T, ROUTE_W), F32)],
        compiler_params=_cparams(),
        name="mix_out_router",
    )(ymix, x2d, lw['mix_out_g'].reshape(1, MIX_W), lw['w_out'].astype(MXU_DTYPE), lw['norm_ffn_g'].reshape(1, D),
      wr, br)


def moe_schedule(eidx, tile):
    T = eidx.shape[0]
    M = T * TOP_E
    fe = eidx.reshape(M)
    onehot = (fe[:, None] == jnp.arange(N_EXPERTS, dtype=jnp.int32)[None, :]).astype(jnp.int32)
    csum = jnp.cumsum(onehot, axis=0)
    rank = jnp.take_along_axis(csum, fe[:, None], axis=1)[:, 0] - 1
    counts = csum[-1]
    padded = (counts + tile - 1) // tile * tile
    pad_end = jnp.cumsum(padded)
    dest = (pad_end - padded)[fe] + rank
    n_blk = -(-M // tile) + N_EXPERTS
    tok = jnp.arange(M, dtype=jnp.int32) // TOP_E
    buf_tok = jnp.zeros((n_blk * tile,), jnp.int32).at[dest].set(tok)
    blk_exp = jnp.minimum(jnp.searchsorted(pad_end, jnp.arange(n_blk, dtype=jnp.int32) * tile, side='right'),
                          N_EXPERTS - 1).astype(jnp.int32)
    n_used = (pad_end[-1:] // tile).astype(jnp.int32)
    return buf_tok, blk_exp, n_used, dest.astype(jnp.int32)


def _moe_ffn_body(tile, tok_ref, bexp_ref, nused_ref, x_hbm, wgu_ref, wdn_ref, y_ref, xg, sem, wgu_bf, wdn_bf):
    j = pl.program_id(0)
    n = nused_ref[0]

    def gather(blk, slot):
        def body(r, c):
            t = tok_ref[blk * tile + r]
            pltpu.make_async_copy(x_hbm.at[pl.ds(t, 1)], xg.at[slot, pl.ds(r, 1)], sem.at[slot]).start()
            return c
        lax.fori_loop(0, tile, body, 0, unroll=8)

    @pl.when((j == 0) & (n > 0))
    def _():
        gather(0, 0)

    @pl.when(j < n)
    def _():
        slot = j % 2

        @pl.when(j + 1 < n)
        def _():
            gather(j + 1, 1 - slot)

        @pl.when((j == 0) | (bexp_ref[j] != bexp_ref[jnp.maximum(j - 1, 0)]))
        def _():
            wgu_bf[...] = wgu_ref[0].astype(wgu_bf.dtype)
            wdn_bf[...] = wdn_ref[0].astype(wdn_bf.dtype)

        pltpu.make_async_copy(x_hbm.at[pl.ds(0, tile)], xg.at[slot], sem.at[slot]).wait()
        h = _mm(xg[slot], wgu_bf[...])
        a, b = h[:, :D_EXPERT], h[:, D_EXPERT:]
        y_ref[...] = _mm(a * jax.nn.sigmoid(a) * b, wdn_bf[...])

    @pl.when(j >= n)
    def _():
        y_ref[...] = jnp.zeros_like(y_ref)


def moe_ffn_pallas(xn, buf_tok, blk_exp, n_used, w_gu, w_down, tile):
    T, D = xn.shape
    n_blk = blk_exp.shape[0]
    return pl.pallas_call(
        functools.partial(_moe_ffn_body, tile),
        grid_spec=pltpu.PrefetchScalarGridSpec(
            num_scalar_prefetch=3,
            grid=(n_blk,),
            in_specs=[pl.BlockSpec(memory_space=pl.ANY),
                      pl.BlockSpec((1, D, 2 * D_EXPERT), lambda j, tok, bexp, nu: (bexp[j], 0, 0)),
                      pl.BlockSpec((1, D_EXPERT, D), lambda j, tok, bexp, nu: (bexp[j], 0, 0))],
            out_specs=pl.BlockSpec((tile, D), lambda j, tok, bexp, nu: (j, 0)),
            scratch_shapes=[pltpu.VMEM((2, tile, D), F32), pltpu.SemaphoreType.DMA((2,)),
                            pltpu.VMEM((D, 2 * D_EXPERT), MXU_DTYPE), pltpu.VMEM((D_EXPERT, D), MXU_DTYPE)]),
        out_shape=jax.ShapeDtypeStruct((n_blk * tile, D), F32),
        compiler_params=_cparams(),
        name="moe_ffn",
    )(buf_tok, blk_exp, n_used, xn, w_gu, w_down)


def _moe_combine_body(tm, slots_ref, y_hbm, x2_ref, route_ref, o_ref, yb, sem):
    i = pl.program_id(0)
    nt = pl.num_programs(0)

    def gather(tile_i, buf):
        def body(r, c):
            for k in range(TOP_E):
                s = slots_ref[(tile_i * tm + r) * TOP_E + k]
                pltpu.make_async_copy(y_hbm.at[pl.ds(s, 1)], yb.at[buf, k, pl.ds(r, 1)], sem.at[buf]).start()
            return c
        lax.fori_loop(0, tm, body, 0, unroll=8)

    @pl.when(i == 0)
    def _():
        gather(0, 0)

    buf = i % 2

    @pl.when(i + 1 < nt)
    def _():
        gather(i + 1, 1 - buf)

    for k in range(TOP_E):
        pltpu.make_async_copy(y_hbm.at[pl.ds(0, tm)], yb.at[buf, k], sem.at[buf]).wait()
    r = route_ref[...]
    o_ref[...] = x2_ref[...] + (r[:, 2:3] * yb[buf, 0] + r[:, 3:4] * yb[buf, 1])


def moe_combine_pallas(y, slots, x2, route):
    T, D = x2.shape
    tm = min(COMBINE_TILE, T)
    return pl.pallas_call(
        functools.partial(_moe_combine_body, tm),
        grid_spec=pltpu.PrefetchScalarGridSpec(
            num_scalar_prefetch=1,
            grid=(T // tm,),
            in_specs=[pl.BlockSpec(memory_space=pl.ANY),
                      pl.BlockSpec((tm, D), lambda i, s: (i, 0)),
                      pl.BlockSpec((tm, ROUTE_W), lambda i, s: (i, 0))],
            out_specs=pl.BlockSpec((tm, D), lambda i, s: (i, 0)),
            scratch_shapes=[pltpu.VMEM((2, TOP_E, tm, D), F32), pltpu.SemaphoreType.DMA((2,))]),
        out_shape=jax.ShapeDtypeStruct((T, D), F32),
        compiler_params=_cparams(),
        name="moe_combine",
    )(slots, y, x2, route)


def mix_out_moe(ymix, x2d, lw, tile):
    x2, xn, route = mix_out_router(ymix, x2d, lw)
    eidx = route[:, 0:TOP_E].astype(jnp.int32)
    buf_tok, blk_exp, n_used, slots = moe_schedule(eidx, tile)
    y = moe_ffn_pallas(xn, buf_tok, blk_exp, n_used, lw['exp_w_gu'], lw['exp_w_down'], tile)
    return moe_combine_pallas(y, slots, x2, route)


def rmsnorm(x, g):
    xf = x.astype(jnp.float32)
    y = xf * lax.rsqrt(jnp.mean(xf * xf, axis=-1, keepdims=True) + EPS)
    return (y * g.astype(jnp.float32)).astype(x.dtype)


def split_cols(a, sizes):
    outs, o = [], 0
    for s in sizes:
        outs.append(a[..., o:o + s])
        o += s
    return outs


def causal_dwconv(u, prev, w, b):
    L = u.shape[1]
    ext = jnp.concatenate([prev.astype(u.dtype), u], axis=1)
    y = lax.conv_general_dilated(ext, w[:, None, :].astype(u.dtype), window_strides=(1,), padding='VALID',
                                 dimension_numbers=('NWC', 'WIO', 'NWC'), feature_group_count=u.shape[-1])
    return y + b.astype(u.dtype), ext[:, L:]


def pool_mixer(u, prev, pos0, w, scale):
    B_, L, C = u.shape
    ext = jnp.concatenate([prev.astype(u.dtype), u], axis=1)
    ef = ext.astype(jnp.float32)
    cs = jnp.concatenate([jnp.zeros((B_, 1, C), jnp.float32), jnp.cumsum(ef, axis=1)], axis=1)
    pos = pos0 + jnp.arange(L)
    means = []
    for g, win in enumerate(POOL_WINDOWS):
        sl = slice(g * POOL_GROUP, (g + 1) * POOL_GROUP)
        tot = cs[:, POOL_KEEP + 1:POOL_KEEP + 1 + L, sl] - cs[:, POOL_KEEP + 1 - win:POOL_KEEP + 1 - win + L, sl]
        cnt = jnp.minimum(win, pos + 1).astype(jnp.float32)
        means.append(tot / cnt[None, :, None])
    d = (jnp.concatenate(means, axis=-1) - ef[:, POOL_KEEP:]).astype(u.dtype)
    y = jnp.einsum('blgc,gcd->blgd', d.reshape(B_, L, len(POOL_WINDOWS), POOL_GROUP), w).reshape(B_, L, C)
    return y * scale, ext[:, L:]


def rglru_mixer(xb, gb, conv_prev, h0, conv_w, conv_b, w_a, b_a, w_x, b_x, lam):
    B_, L, C = xb.shape
    xc, conv_new = causal_dwconv(xb, conv_prev, conv_w, conv_b)
    xh = xc.reshape(B_, L, RG_HEADS, RG_BLOCK)
    r = jax.nn.sigmoid(jnp.einsum('blhi,hij->blhj', xh, w_a).reshape(B_, L, C) + b_a)
    ig = jax.nn.sigmoid(jnp.einsum('blhi,hij->blhj', xh, w_x).reshape(B_, L, C) + b_x)
    log_a = -RG_C * r.astype(jnp.float32) * jax.nn.softplus(-lam.astype(jnp.float32))
    a = jnp.exp(log_a)
    bt = jnp.sqrt(-jnp.expm1(2.0 * log_a)) * (ig * xc).astype(jnp.float32)
    bt = bt.at[:, 0].add(a[:, 0] * h0.astype(jnp.float32))
    _, h = lax.associative_scan(lambda e1, e2: (e1[0] * e2[0], e2[0] * e1[1] + e2[1]), (a, bt), axis=1)
    y = h.astype(xb.dtype) * jax.nn.gelu(gb)
    return y, conv_new, h[:, -1].astype(xb.dtype)


def masked_softmax(s, mask):
    s = jnp.where(mask, s.astype(jnp.float32), -jnp.inf)
    m = jnp.max(s, axis=-1, keepdims=True)
    e = jnp.exp(s - jnp.where(jnp.isfinite(m), m, 0.0))
    d = jnp.sum(e, axis=-1, keepdims=True)
    return e / jnp.where(d > 0, d, 1.0)


def nsa_compress(k_raw, v_raw, phi, phi_b, g_kc):
    B_, T = k_raw.shape[:2]
    R = CMP_BLOCK // CMP_STRIDE
    nch = T // CMP_STRIDE
    ncmp = nch - (R - 1)

    def comp(a, w, bias):
        ch = a[:, :nch * CMP_STRIDE].reshape(B_, nch, CMP_STRIDE, N_KV, HEAD_DIM)
        ch = ch.transpose(0, 1, 3, 2, 4).reshape(B_, nch, N_KV, CMP_STRIDE * HEAD_DIM)
        wr = w.reshape(R, CMP_STRIDE * HEAD_DIM, HEAD_DIM)
        out = jnp.einsum('bckf,fd->bckd', ch[:, 0:ncmp], wr[0])
        for r in range(1, R):
            out = out + jnp.einsum('bckf,fd->bckd', ch[:, r:r + ncmp], wr[r])
        return out + bias

    kc = rmsnorm(comp(k_raw, phi[0], phi_b[0]), g_kc)
    vc = comp(v_raw, phi[1], phi_b[1])
    cmp_end = jnp.arange(ncmp) * CMP_STRIDE + (CMP_BLOCK - 1)
    return kc, vc, cmp_end


def sel_blocks(a):
    B_, T = a.shape[:2]
    n_sel = -(-T // SEL_BLOCK)
    a = jnp.pad(a, ((0, 0), (0, n_sel * SEL_BLOCK - T), (0, 0), (0, 0)))
    return a.reshape(B_, n_sel, SEL_BLOCK, N_KV, HEAD_DIM).transpose(0, 3, 1, 2, 4)


def nsa_attend(q, q_pos, gates, kc, vc, cmp_end, ks_blk, vs_blk, kw, vw, w_pos):
    dt = q.dtype
    B_, Q = q.shape[:2]
    t = q_pos[:, None]
    s = jnp.einsum('bqkgd,bckd->bqkgc', q, kc)
    p_cmp = masked_softmax(s, (cmp_end[None, :] <= t)[None, :, None, None, :])
    o_cmp = jnp.einsum('bqkgc,bckd->bqkgd', p_cmp.astype(dt), vc)
    n_sel = ks_blk.shape[2]
    ci = jnp.arange(kc.shape[1])[:, None] * CMP_STRIDE
    sj = jnp.arange(n_sel)[None, :] * SEL_BLOCK
    overlap = ((ci < sj + SEL_BLOCK) & (ci + CMP_BLOCK > sj)).astype(jnp.float32)
    imp = jnp.einsum('bqkgc,cs->bqks', p_cmp, overlap)
    blk = jnp.arange(n_sel)[None, :]
    cur = t // SEL_BLOCK
    valid = blk <= cur
    forced = (blk == 0) | (blk == cur) | (blk == cur - 1)
    score = jnp.where(valid[None, :, None, :], imp, -jnp.inf)
    score = jnp.where((forced & valid)[None, :, None, :], jnp.inf, score)
    top_v, top_i = lax.top_k(score, min(SEL_TOPK, n_sel))
    kk = top_i.shape[-1]
    bi = jnp.arange(B_)[:, None, None, None]
    hi = jnp.arange(N_KV)[None, None, :, None]
    ks = ks_blk[bi, hi, top_i].reshape(B_, Q, N_KV, kk * SEL_BLOCK, HEAD_DIM)
    vs = vs_blk[bi, hi, top_i].reshape(B_, Q, N_KV, kk * SEL_BLOCK, HEAD_DIM)
    spos = (top_i[..., None] * SEL_BLOCK + jnp.arange(SEL_BLOCK)).reshape(B_, Q, N_KV, kk * SEL_BLOCK)
    smask = (spos <= q_pos[None, :, None, None]) & jnp.repeat(top_v > -jnp.inf, SEL_BLOCK, axis=-1)
    s = jnp.einsum('bqkgd,bqknd->bqkgn', q, ks)
    o_sel = jnp.einsum('bqkgn,bqknd->bqkgd', masked_softmax(s, smask[:, :, :, None, :]).astype(dt), vs)
    wd = t - w_pos[None, :]
    wmask = (w_pos[None, :] >= 0) & (wd >= 0) & (wd <= WINDOW)
    s = jnp.einsum('bqkgd,bnkd->bqkgn', q, kw)
    o_win = jnp.einsum('bqkgn,bnkd->bqkgd', masked_softmax(s, wmask[None, :, None, None, :]).astype(dt), vw)
    return gates[..., 0:1] * o_cmp + gates[..., 1:2] * o_sel + gates[..., 2:3] * o_win


def nsa_prompt(q, gates, kc_raw, vc_raw, ksel, vsel, kwin, vwin, phi, phi_b, g_kc):
    B_, S = q.shape[:2]
    kc, vc, cmp_end = nsa_compress(kc_raw, vc_raw, phi, phi_b, g_kc)
    ks_blk, vs_blk = sel_blocks(ksel), sel_blocks(vsel)
    zpad = jnp.zeros((B_, WINDOW, N_KV, HEAD_DIM), kwin.dtype)
    kw_pad = jnp.concatenate([zpad, kwin], axis=1)
    vw_pad = jnp.concatenate([zpad, vwin], axis=1)
    nq = S // Q_BLOCK

    def body(args):
        qc, gc, i = args
        start = i * Q_BLOCK
        kw = lax.dynamic_slice_in_dim(kw_pad, start, WINDOW + Q_BLOCK, axis=1)
        vw = lax.dynamic_slice_in_dim(vw_pad, start, WINDOW + Q_BLOCK, axis=1)
        return nsa_attend(qc, start + jnp.arange(Q_BLOCK), gc, kc, vc, cmp_end, ks_blk, vs_blk,
                          kw, vw, start - WINDOW + jnp.arange(WINDOW + Q_BLOCK))

    qb = q.reshape(B_, nq, Q_BLOCK, N_KV, GQA, HEAD_DIM).swapaxes(0, 1)
    gb = gates.reshape(B_, nq, Q_BLOCK, N_KV, GQA, 3).swapaxes(0, 1)
    o = lax.map(body, (qb, gb, jnp.arange(nq)))
    o = o.swapaxes(0, 1).reshape(B_, S, N_HEADS * HEAD_DIM)
    rows = jnp.stack([kc_raw, vc_raw, ksel, vsel], axis=2)
    win_new = jnp.stack([kwin, vwin], axis=2)[:, S - min(WINDOW, S):]
    return o, rows, win_new


def nsa_sample(pool, page_table, win_buf, q, gates, kc_raw, vc_raw, ksel, vsel, kwin, vwin, phi, phi_b, g_kc):
    B_, L = q.shape[:2]
    past = pool[page_table]
    past = past.reshape(B_, past.shape[1] * past.shape[2], 4, N_KV, HEAD_DIM)
    P = past.shape[1]
    rows = jnp.stack([kc_raw, vc_raw, ksel, vsel], axis=2)
    full = jnp.concatenate([past.astype(rows.dtype), rows], axis=1)
    kc, vc, cmp_end = nsa_compress(full[:, :, 0], full[:, :, 1], phi, phi_b, g_kc)
    ks_blk, vs_blk = sel_blocks(full[:, :, 2]), sel_blocks(full[:, :, 3])
    Lw = win_buf.shape[1]
    new_w = jnp.stack([kwin, vwin], axis=2)
    wfull = jnp.concatenate([win_buf.astype(new_w.dtype), new_w], axis=1)
    o = nsa_attend(q, P + jnp.arange(L), gates, kc, vc, cmp_end, ks_blk, vs_blk,
                   wfull[:, :, 0], wfull[:, :, 1], P - Lw + jnp.arange(Lw + L))
    return o.reshape(B_, L, N_HEADS * HEAD_DIM), rows, wfull[:, L:]


def expert_dispatch(xt, eidx, gate, w_gu, w_down):
    T, D = xt.shape
    M = T * TOP_E
    fe = eidx.reshape(M)
    ftok = jnp.arange(M, dtype=jnp.int32) // TOP_E
    fgate = gate.reshape(M)
    order = jnp.argsort(fe)
    se, stok, sgate = fe[order], ftok[order], fgate[order]
    counts = jnp.bincount(fe, length=N_EXPERTS)
    padded = (counts + MOE_BLOCK - 1) // MOE_BLOCK * MOE_BLOCK
    pad_end = jnp.cumsum(padded)
    pad_start = pad_end - padded
    start = jnp.cumsum(counts) - counts
    dest = pad_start[se] + jnp.arange(M) - start[se]
    n_blk = -(-M // MOE_BLOCK) + N_EXPERTS
    P = n_blk * MOE_BLOCK
    buf_tok = jnp.zeros((P,), jnp.int32).at[dest].set(stok)
    buf_gate = jnp.zeros((P,), fgate.dtype).at[dest].set(sgate)
    blk_exp = jnp.minimum(jnp.searchsorted(pad_end, jnp.arange(n_blk) * MOE_BLOCK, side='right'), N_EXPERTS - 1)
    xb = xt[buf_tok].reshape(n_blk, MOE_BLOCK, D)

    def run(args):
        xi, e = args
        a, b = jnp.split(xi @ w_gu[e], 2, axis=-1)
        return (jax.nn.silu(a) * b) @ w_down[e]

    yb = lax.map(run, (xb, blk_exp)).reshape(P, D)
    return jax.ops.segment_sum(yb * buf_gate[:, None].astype(yb.dtype), buf_tok, num_segments=T)


def moe_ffn(x, wg_r, bg_r, we_r, be_r, w_gu, w_down):
    B_, L, D = x.shape
    xt = x.reshape(B_ * L, D)
    T = xt.shape[0]
    lg = (xt @ wg_r + bg_r).astype(jnp.float32)
    pg = jax.nn.softmax(lg, axis=-1)
    gsel = jnp.argmax(lg, axis=-1)
    p_group = jnp.take_along_axis(pg, gsel[:, None], axis=-1)
    le = (xt @ we_r + be_r).astype(jnp.float32).reshape(T, N_GROUPS, EXP_PER_GROUP)
    le_g = jnp.take_along_axis(le, gsel[:, None, None], axis=1)[:, 0]
    tv, ti = lax.top_k(le_g, TOP_E)
    gate = p_group * jax.nn.softmax(tv, axis=-1)
    eidx = gsel[:, None] * EXP_PER_GROUP + ti
    return expert_dispatch(xt, eidx, gate, w_gu, w_down).reshape(B_, L, D)


def layer_forward(x, pos0, lw, pool_prev, rgc_prev, rgh0, sc_prev, nsa_fn):
    B_, L, _ = x.shape
    w_perm = permute_w_in(lw['w_in']).astype(MXU_DTYPE)
    proj2d = norm_matmul(x.reshape(B_ * L, D_MODEL), lw['norm_mix_g'], w_perm)
    proj = proj2d.reshape(B_, L, N_IN_PAD)
    q, kv, pu, rx, rgate, sc, ng = split_cols(proj, (COL_KV, KV_W, POOL_W, RG_W, RG_W, 3 * SC_W, 3 * N_HEADS))
    y_pool, pool_new = pool_mixer(pu, pool_prev, pos0, lw['pool_w'], lw['pool_scale'])
    y_rg, rgc_new, rgh_new = rglru_mixer(rx, rgate, rgc_prev, rgh0, lw['rg_conv_w'], lw['rg_conv_b'],
                                         lw['rg_w_a'], lw['rg_b_a'], lw['rg_w_x'], lw['rg_b_x'], lw['rg_lambda'])
    qk_g = lw['nsa_qk_g']
    y_nsa, nsa_rows, win_new = nsa_fn(proj2d, lw['nsa_phi'], lw['nsa_phi_b'], qk_g)
    z, bg, cg = split_cols(sc, (SC_W, SC_W, SC_W))
    v, sc_new = causal_dwconv(cg * z, sc_prev, lw['sc_conv_w'], lw['sc_conv_b'])
    y_sc = bg * v
    ymix = jnp.concatenate([y_pool, y_rg, y_nsa, y_sc], axis=-1).reshape(B_ * L, MIX_W)
    x = mix_out_moe(ymix, x.reshape(B_ * L, D_MODEL), lw, MOE_TILE_PROMPT if L > 1 else MOE_TILE_SAMPLE)
    return x.reshape(B_, L, D_MODEL), (nsa_rows, win_new, pool_new, rgc_new, rgh_new, sc_new)


def kernel(x_prompt, x_sample, cache_nsa, state_win_kv, state_pool, state_rg_conv, state_rg_h, state_sc_conv,
           page_table, norm_mix_g, w_in, pool_w, pool_scale, rg_conv_w, rg_conv_b, rg_w_a, rg_b_a, rg_w_x, rg_b_x,
           rg_lambda, nsa_phi, nsa_phi_b, nsa_qk_g, sc_conv_w, sc_conv_b, mix_out_g, w_out, norm_ffn_g,
           router_group_w, router_group_b, router_expert_w, router_expert_b, exp_w_gu, exp_w_down):
    past_len = page_table.shape[1] * cache_nsa.shape[2]
    xp, xs = x_prompt, x_sample
    cache3 = feature_major_pages(cache_nsa)
    win3 = state_win_kv.transpose(0, 1, 3, 4, 5, 2).reshape(DEPTH * state_win_kv.shape[1], 2, N_KV * HEAD_DIM,
                                                             state_win_kv.shape[2])
    cache_ab = cache_compress(cache3, nsa_phi)
    Bp = xp.shape[0]
    st_p, st_s = [], []
    for l in range(DEPTH):
        lw = dict(norm_mix_g=norm_mix_g[l], w_in=w_in[l], pool_w=pool_w[l], pool_scale=pool_scale[l],
                  rg_conv_w=rg_conv_w[l], rg_conv_b=rg_conv_b[l], rg_w_a=rg_w_a[l], rg_b_a=rg_b_a[l],
                  rg_w_x=rg_w_x[l], rg_b_x=rg_b_x[l], rg_lambda=rg_lambda[l], nsa_phi=nsa_phi[l],
                  nsa_phi_b=nsa_phi_b[l], nsa_qk_g=nsa_qk_g[l], sc_conv_w=sc_conv_w[l], sc_conv_b=sc_conv_b[l],
                  mix_out_g=mix_out_g[l], w_out=w_out[l], norm_ffn_g=norm_ffn_g[l],
                  router_group_w=router_group_w[l], router_group_b=router_group_b[l],
                  router_expert_w=router_expert_w[l], router_expert_b=router_expert_b[l],
                  exp_w_gu=exp_w_gu[l], exp_w_down=exp_w_down[l])
        xp, sp = layer_forward(xp, 0, lw,
                               jnp.zeros((Bp, POOL_KEEP, POOL_W), xp.dtype),
                               jnp.zeros((Bp, RG_CONV - 1, RG_W), xp.dtype),
                               jnp.zeros((Bp, RG_W), xp.dtype),
                               jnp.zeros((Bp, SC_CONV - 1, SC_W), xp.dtype),
                               lambda p, phi, phi_b, g: nsa_prompt_pallas(p, Bp, xp.shape[1], phi, phi_b, g))
        xs, ss = layer_forward(xs, past_len, lw, state_pool[l], state_rg_conv[l], state_rg_h[l], state_sc_conv[l],
                               lambda p, phi, phi_b, g: nsa_sample_pallas(p, l, cache3, cache_ab, page_table, win3,
                                                                          phi_b, g))
        st_p.append(sp)
        st_s.append(ss)

    def stk(lst, i):
        return jnp.stack([s[i] for s in lst])

    return (xp, xs, stk(st_p, 0), stk(st_s, 0), stk(st_p, 1), stk(st_s, 1), stk(st_p, 2), stk(st_s, 2),
            stk(st_p, 3), stk(st_s, 3), stk(st_p, 4), stk(st_s, 4), stk(st_p, 5), stk(st_s, 5))
```

```python
import functools
import jax, jax.numpy as jnp
from jax import lax
import numpy as np
from jax.experimental import pallas as pl
from jax.experimental.pallas import tpu as pltpu

D_MODEL = 1024
BATCH = 4
SEQ = 4096
DEPTH = 2
DEC_BATCH = 128
DEC_SEQ = 1
PAST_LEN = 2048
PAGE_SIZE = 128

MIX_W = D_MODEL
GROUP_W = MIX_W // 4
POOL_W = GROUP_W
POOL_WINDOWS = (2, 4, 8, 16)
POOL_GROUP = POOL_W // len(POOL_WINDOWS)
POOL_KEEP = max(POOL_WINDOWS) - 1
RG_W = GROUP_W
RG_HEADS = 4
RG_BLOCK = RG_W // RG_HEADS
RG_CONV = 4
RG_C = 8.0
HEAD_DIM = 64
N_HEADS = GROUP_W // HEAD_DIM
N_KV = 2
GQA = N_HEADS // N_KV
CMP_BLOCK = 32
CMP_STRIDE = 16
SEL_BLOCK = 64
SEL_TOPK = 16
WINDOW = 512
Q_BLOCK = 128
SC_W = GROUP_W
SC_CONV = 3
N_GROUPS = 4
EXP_PER_GROUP = 8
N_EXPERTS = N_GROUPS * EXP_PER_GROUP
TOP_E = 2
D_EXPERT = 512
MOE_BLOCK = 128
EPS = 1e-6
SPLIT_SIZES = (POOL_W, RG_W, RG_W, N_HEADS * HEAD_DIM, 6 * N_KV * HEAD_DIM, 3 * N_HEADS, 3 * SC_W)
N_IN = sum(SPLIT_SIZES)

LANE = 128
ROW_TILE = 512
VMEM_LIMIT = 48 * 1024 * 1024
MXU_DTYPE = jnp.bfloat16
F32 = jnp.float32
NEG = -1e30

KV_W = 6 * N_KV * HEAD_DIM
COL_Q = 0
COL_KV = COL_Q + N_HEADS * HEAD_DIM
COL_POOL = COL_KV + KV_W
COL_RX = COL_POOL + POOL_W
COL_RGATE = COL_RX + RG_W
COL_SC = COL_RGATE + RG_W
COL_NG = COL_SC + 3 * SC_W
N_IN_PAD = COL_NG + LANE
SEL_TILE = 256
N_SEL_PROMPT = SEQ // SEL_BLOCK


def _cparams(n_axes=1):
    return pltpu.CompilerParams(dimension_semantics=("arbitrary",) * n_axes, vmem_limit_bytes=VMEM_LIMIT)


def _mm(a, b):
    return jnp.dot(a.astype(MXU_DTYPE), b.astype(MXU_DTYPE), preferred_element_type=F32)


def _mm_nt(a, b):
    return lax.dot_general(a.astype(MXU_DTYPE), b.astype(MXU_DTYPE), (((1,), (1,)), ((), ())),
                           preferred_element_type=F32)


def permute_w_in(w):
    pu, rx, rgate, q, kv, ng, sc = split_cols(w, SPLIT_SIZES)
    pad = jnp.zeros((w.shape[0], LANE - ng.shape[1]), w.dtype)
    return jnp.concatenate([q, kv, pu, rx, rgate, sc, ng, pad], axis=1)


def _norm_matmul_body(x_ref, g_ref, w_ref, o_ref):
    xf = x_ref[...]
    h = xf * lax.rsqrt(jnp.mean(xf * xf, axis=-1, keepdims=True) + EPS) * g_ref[...]
    o_ref[...] = _mm(h, w_ref[...])


def norm_matmul(x2d, g, w):
    T, D = x2d.shape
    N = w.shape[1]
    tm = min(ROW_TILE, T)
    return pl.pallas_call(
        _norm_matmul_body,
        grid=(T // tm,),
        in_specs=[pl.BlockSpec((tm, D), lambda i: (i, 0)),
                  pl.BlockSpec((1, D), lambda i: (0, 0)),
                  pl.BlockSpec((D, N), lambda i: (0, 0))],
        out_specs=pl.BlockSpec((tm, N), lambda i: (i, 0)),
        out_shape=jax.ShapeDtypeStruct((T, N), F32),
        compiler_params=_cparams(),
        name="norm_in_proj",
    )(x2d, g.reshape(1, D), w)


def _seg_rmsnorm(x, g):
    x2 = x * x
    left = lax.broadcasted_iota(jnp.int32, x.shape, 1) < HEAD_DIM
    s_l = jnp.sum(jnp.where(left, x2, 0.0), axis=-1, keepdims=True)
    s_r = jnp.sum(jnp.where(left, 0.0, x2), axis=-1, keepdims=True)
    ms = jnp.where(left, s_l, s_r) * (1.0 / HEAD_DIM)
    return x * lax.rsqrt(ms + EPS) * g


def _nsa_prep_body(qkv_ref, ng_ref, g_ref, perm_ref, qa_ref, kvb_ref, rawb_ref, rows_t_ref, win_t_ref, win_ref,
                   gates_ref):
    g = g_ref[...]
    for hb in range(N_KV):
        qn = _seg_rmsnorm(qkv_ref[:, COL_Q + hb * LANE:COL_Q + (hb + 1) * LANE], g[0:1]) * (HEAD_DIM ** -0.5)
        qa_ref[:, hb * 2 * LANE:(hb + 1) * 2 * LANE] = _mm(qn, perm_ref[hb]).astype(qa_ref.dtype)
    comp = [qkv_ref[:, COL_KV + c * LANE:COL_KV + (c + 1) * LANE] for c in range(6)]
    comp[2] = _seg_rmsnorm(comp[2], g[2:3])
    comp[4] = _seg_rmsnorm(comp[4], g[3:4])
    for c in range(6):
        kvb_ref[:, c * LANE:(c + 1) * LANE] = comp[c].astype(kvb_ref.dtype)
    for c in range(2):
        rawb_ref[:, c * LANE:(c + 1) * LANE] = comp[c].astype(rawb_ref.dtype)
    for c in range(4):
        rows_t_ref[0, c * LANE:(c + 1) * LANE, :] = comp[c].T
    for c in range(2):
        win_t_ref[0, c * LANE:(c + 1) * LANE, :] = comp[4 + c].T
        win_ref[:, c * LANE:(c + 1) * LANE] = comp[4 + c]
    gates_ref[...] = jax.nn.sigmoid(ng_ref[...])


def _q_place_matrices():
    p = np.zeros((N_KV, LANE, 2 * LANE), np.float32)
    for hb in range(N_KV):
        for gq in range(GQA):
            for d in range(HEAD_DIM):
                p[hb, gq * HEAD_DIM + d, gq * LANE + hb * HEAD_DIM + d] = 1.0
    return jnp.asarray(p, MXU_DTYPE)


def nsa_prep(proj, qk_g, B_, S):
    T = proj.shape[0]
    tm = min(ROW_TILE, S)
    tpb = S // tm
    qkv_w = COL_POOL
    g4 = jnp.tile(qk_g, (1, 2))
    return pl.pallas_call(
        _nsa_prep_body,
        grid=(T // tm,),
        in_specs=[pl.BlockSpec((tm, qkv_w), lambda i: (i, 0)),
                  pl.BlockSpec((tm, LANE), lambda i: (i, COL_NG // LANE)),
                  pl.BlockSpec((4, LANE), lambda i: (0, 0)),
                  pl.BlockSpec((N_KV, LANE, 2 * LANE), lambda i: (0, 0, 0))],
        out_specs=[pl.BlockSpec((tm, 4 * LANE), lambda i: (i, 0)),
                   pl.BlockSpec((tm, 6 * LANE), lambda i: (i, 0)),
                   pl.BlockSpec((tm, 2 * LANE), lambda i: (i, 0)),
                   pl.BlockSpec((1, 4 * LANE, tm), lambda i: (i // tpb, 0, i % tpb)),
                   pl.BlockSpec((1, 2 * LANE, tm), lambda i: (i // tpb, 0, i % tpb)),
                   pl.BlockSpec((tm, 2 * LANE), lambda i: (i, 0)),
                   pl.BlockSpec((tm, LANE), lambda i: (i, 0))],
        out_shape=[jax.ShapeDtypeStruct((T, 4 * LANE), MXU_DTYPE),
                   jax.ShapeDtypeStruct((T, 6 * LANE), MXU_DTYPE),
                   jax.ShapeDtypeStruct((T, 2 * LANE), MXU_DTYPE),
                   jax.ShapeDtypeStruct((B_, 4 * LANE, S), F32),
                   jax.ShapeDtypeStruct((B_, 2 * LANE, S), F32),
                   jax.ShapeDtypeStruct((T, 2 * LANE), F32),
                   jax.ShapeDtypeStruct((T, LANE), F32)],
        compiler_params=_cparams(),
        name="nsa_prep",
    )(proj, proj, g4, _q_place_matrices())


def compress_weights(phi):
    R = CMP_BLOCK // CMP_STRIDE
    wr = phi.reshape(2, R, CMP_STRIDE, HEAD_DIM, HEAD_DIM)
    eye = jnp.eye(2, dtype=phi.dtype)
    w = jnp.einsum('crjde,cx,hy->rjchdxye', wr, eye, eye)
    return w.reshape(R, CMP_STRIDE * 2 * LANE, 2 * LANE).astype(MXU_DTYPE)


def _compress_body(x_ref, w_ref, b_ref, g_ref, kc_ref, vc_ref):
    x = x_ref[0]
    nch = x.shape[0]
    a = _mm(x, w_ref[0])
    bm = _mm(x, w_ref[1])
    out = a + pltpu.roll(bm, nch - 1, 0) + b_ref[...]
    kc_ref[0] = _seg_rmsnorm(out[:, 0:LANE], g_ref[...]).astype(kc_ref.dtype)
    vc_ref[0] = out[:, LANE:2 * LANE].astype(vc_ref.dtype)


def nsa_compress_pallas(rawb3, wc, phi_b, g_kc):
    B_, nch, K = rawb3.shape
    bias = jnp.concatenate([jnp.tile(phi_b[0], 2), jnp.tile(phi_b[1], 2)]).reshape(1, 2 * LANE)
    return pl.pallas_call(
        _compress_body,
        grid=(B_,),
        in_specs=[pl.BlockSpec((1, nch, K), lambda b: (b, 0, 0)),
                  pl.BlockSpec(wc.shape, lambda b: (0, 0, 0)),
                  pl.BlockSpec((1, 2 * LANE), lambda b: (0, 0)),
                  pl.BlockSpec((1, LANE), lambda b: (0, 0))],
        out_specs=[pl.BlockSpec((1, nch, LANE), lambda b: (b, 0, 0)),
                   pl.BlockSpec((1, nch, LANE), lambda b: (b, 0, 0))],
        out_shape=[jax.ShapeDtypeStruct((B_, nch, LANE), MXU_DTYPE),
                   jax.ShapeDtypeStruct((B_, nch, LANE), MXU_DTYPE)],
        compiler_params=_cparams(),
        name="nsa_compress",
    )(rawb3, wc, bias, jnp.tile(g_kc, 2).reshape(1, LANE))


def _online_update(carry, s, mask, v):
    m, l, acc = carry
    s = jnp.where(mask, s, NEG)
    m_new = jnp.maximum(m, jnp.max(s, axis=-1, keepdims=True))
    alpha = jnp.exp(m - m_new)
    p = jnp.exp(s - m_new)
    l = alpha * l + jnp.sum(p, axis=-1, keepdims=True)
    acc = alpha * acc + _mm(p, v)
    return m_new, l, acc


def _select_blocks(imp, start):
    n_sel = N_SEL_PROMPT
    sc_t = imp.T[0:n_sel]
    blk = lax.broadcasted_iota(jnp.int32, sc_t.shape, 0)
    cur = (start + lax.broadcasted_iota(jnp.int32, sc_t.shape, 1)) // SEL_BLOCK
    valid = blk <= cur
    forced = (blk == 0) | (blk == cur) | (blk == cur - 1)
    score = jnp.where(valid, sc_t, -jnp.inf)
    score = jnp.where(forced & valid, jnp.inf, score)
    cnt = jnp.zeros(sc_t.shape, F32)
    for i in range(n_sel):
        ri = score[i:i + 1, :]
        beats = (ri > score) | ((ri == score) & (blk > i))
        cnt = cnt + jnp.where(beats, 1.0, 0.0)
    sel_t = jnp.where((cnt < SEL_TOPK) & (score > -jnp.inf), 1.0, 0.0)
    sel_t = jnp.concatenate([sel_t, jnp.zeros((LANE - n_sel, sc_t.shape[1]), F32)], axis=0)
    return sel_t.T


def _nsa_attn_body(qa_ref, gates_ref, kc_ref, vc_ref, kv_ref, ov_ref, e_ref, o_ref):
    i = pl.program_id(1)
    start = i * Q_BLOCK
    Q = Q_BLOCK
    R = GQA * Q
    t_row = start + lax.broadcasted_iota(jnp.int32, (R, 1), 0) % Q
    gates = gates_ref[...]
    lane_q = lax.broadcasted_iota(jnp.int32, (Q, LANE), 1)
    for h in range(N_KV):
        qs = jnp.concatenate([qa_ref[:, (h * GQA + gq) * LANE:(h * GQA + gq + 1) * LANE] for gq in range(GQA)],
                             axis=0)
        kc = kc_ref[0]
        ncmp = kc.shape[0]
        s = _mm_nt(qs, kc)
        cmp_end = lax.broadcasted_iota(jnp.int32, (R, ncmp), 1) * CMP_STRIDE + (CMP_BLOCK - 1)
        s = jnp.where(cmp_end <= t_row, s, -jnp.inf)
        m = jnp.max(s, axis=-1, keepdims=True)
        e = jnp.exp(s - jnp.where(m > -jnp.inf, m, 0.0))
        d = jnp.sum(e, axis=-1, keepdims=True)
        p_cmp = e / jnp.where(d > 0, d, 1.0)
        o_cmp = _mm(p_cmp, vc_ref[0])
        imp = _mm(p_cmp[0:Q], ov_ref[...]) + _mm(p_cmp[Q:R], ov_ref[...])
        sel = _select_blocks(imp, start).astype(MXU_DTYPE)

        def sel_step(j, carry):
            off = pl.multiple_of(j * SEL_TILE, SEL_TILE)
            k = kv_ref[pl.ds(off, SEL_TILE), 2 * LANE:3 * LANE]
            v = kv_ref[pl.ds(off, SEL_TILE), 3 * LANE:4 * LANE]
            sj = _mm_nt(qs, k)
            msel = _mm(sel, e_ref[j])
            msel = jnp.concatenate([msel] * GQA, axis=0)
            kpos = off + lax.broadcasted_iota(jnp.int32, (R, SEL_TILE), 1)
            return _online_update(carry, sj, (msel > 0.5) & (kpos <= t_row), v)

        init = (jnp.full((R, 1), NEG, F32), jnp.zeros((R, 1), F32), jnp.zeros((R, LANE), F32))
        n_tiles = (start + Q + SEL_TILE - 1) // SEL_TILE
        _, l_s, acc_s = lax.fori_loop(0, n_tiles, sel_step, init)
        o_sel = acc_s / l_s

        carry = init
        for kk in range(WINDOW // Q + 1):
            tile = i - kk
            off = pl.multiple_of(jnp.maximum(tile, 0) * Q, Q)
            k = kv_ref[pl.ds(off, Q), 4 * LANE:5 * LANE]
            v = kv_ref[pl.ds(off, Q), 5 * LANE:6 * LANE]
            sj = _mm_nt(qs, k)
            wd = t_row - (tile * Q + lax.broadcasted_iota(jnp.int32, (R, Q), 1))
            carry = _online_update(carry, sj, (tile >= 0) & (wd >= 0) & (wd <= WINDOW), v)
        o_win = carry[2] / carry[1]

        outs = []
        for gq in range(GQA):
            c0 = (h * GQA + gq) * 3
            rs = slice(gq * Q, (gq + 1) * Q)
            og = (gates[:, c0:c0 + 1] * o_cmp[rs] + gates[:, c0 + 1:c0 + 2] * o_sel[rs]
                  + gates[:, c0 + 2:c0 + 3] * o_win[rs])
            outs.append(og if gq == h else pltpu.roll(og, HEAD_DIM, 1))
        o_ref[:, h * LANE:(h + 1) * LANE] = jnp.where(lane_q < HEAD_DIM, outs[0], outs[1])


def _sel_constants(S):
    ncmp_rows = S // CMP_STRIDE
    ci = np.arange(ncmp_rows)[:, None] * CMP_STRIDE
    sj = np.arange(LANE)[None, :] * SEL_BLOCK
    ov = ((ci < sj + SEL_BLOCK) & (ci + CMP_BLOCK > sj) & (np.arange(LANE)[None, :] < S // SEL_BLOCK))
    n_t = S // SEL_TILE
    key_blk = (np.arange(n_t)[:, None, None] * SEL_TILE + np.arange(SEL_TILE)[None, None, :]) // SEL_BLOCK
    e = (np.arange(LANE)[None, :, None] == key_blk)
    return jnp.asarray(ov, MXU_DTYPE), jnp.asarray(e, MXU_DTYPE)


def nsa_attn_prompt(qa, gates, kc, vc, kvb, B_, S):
    nq = S // Q_BLOCK
    nch = S // CMP_STRIDE
    ov, e3 = _sel_constants(S)
    return pl.pallas_call(
        _nsa_attn_body,
        grid=(B_, nq),
        in_specs=[pl.BlockSpec((Q_BLOCK, 4 * LANE), lambda b, i: (b * nq + i, 0)),
                  pl.BlockSpec((Q_BLOCK, LANE), lambda b, i: (b * nq + i, 0)),
                  pl.BlockSpec((1, nch, LANE), lambda b, i: (b, 0, 0)),
                  pl.BlockSpec((1, nch, LANE), lambda b, i: (b, 0, 0)),
                  pl.BlockSpec((S, 6 * LANE), lambda b, i: (b, 0)),
                  pl.BlockSpec(ov.shape, lambda b, i: (0, 0)),
                  pl.BlockSpec(e3.shape, lambda b, i: (0, 0, 0))],
        out_specs=pl.BlockSpec((Q_BLOCK, 2 * LANE), lambda b, i: (b * nq + i, 0)),
        out_shape=jax.ShapeDtypeStruct((B_ * S, N_HEADS * HEAD_DIM), F32),
        compiler_params=_cparams(2),
        name="nsa_attn_prompt",
    )(qa, gates, kc, vc, kvb, ov, e3)


def nsa_prompt_pallas(proj, B_, S, phi, phi_b, qk_g):
    qa, kvb, rawb, rows_t, win_t, _, gates = nsa_prep(proj, qk_g, B_, S)
    nch = S // CMP_STRIDE
    kc, vc = nsa_compress_pallas(rawb.reshape(B_, nch, CMP_STRIDE * 2 * LANE), compress_weights(phi), phi_b, qk_g[1])
    o = nsa_attn_prompt(qa, gates, kc, vc, kvb, B_, S)
    rows = rows_t.reshape(B_, 4, N_KV, HEAD_DIM, S).transpose(0, 4, 1, 2, 3)
    wk = min(WINDOW, S)
    win_new = win_t[:, :, S - wk:].reshape(B_, 2, N_KV, HEAD_DIM, wk).transpose(0, 4, 1, 2, 3)
    return o.reshape(B_, S, N_HEADS * HEAD_DIM), rows, win_new


N_PAGES = PAST_LEN // PAGE_SIZE
N_CHUNK_S = PAST_LEN // CMP_STRIDE
N_SEL_S = -(-(PAST_LEN + DEC_SEQ) // SEL_BLOCK)
CUR_S = PAST_LEN // SEL_BLOCK
QROWS = 8


def compress_weights_paged(phi):
    R = CMP_BLOCK // CMP_STRIDE
    wr = phi.reshape(2, R, CMP_STRIDE, HEAD_DIM, HEAD_DIM)
    eye = jnp.eye(2, dtype=phi.dtype)
    w = jnp.einsum('crjde,cx,hy->jchdrxye', wr, eye, eye)
    return w.reshape(CMP_STRIDE, 2 * LANE, R * 2 * LANE).astype(MXU_DTYPE)


def _softmax_with_extra(s, s_new):
    m = jnp.maximum(jnp.max(s, axis=-1, keepdims=True), s_new)
    e = jnp.exp(s - m)
    e_new = jnp.exp(s_new - m)
    return e, e_new, jnp.sum(e, axis=-1, keepdims=True) + e_new


CHUNKS_PER_PAGE = PAGE_SIZE // CMP_STRIDE
SWEEP_PAGES = 64


def feature_major_pages(cache_nsa):
    d, n = cache_nsa.shape[:2]
    return cache_nsa.transpose(0, 1, 3, 4, 5, 2).reshape(d * n, 4, N_KV * HEAD_DIM, PAGE_SIZE)


def _cache_compress_body(c_ref, w_ref, o_ref, sk, sv):
    n_pages = c_ref.shape[0]

    def to_row_major(p, carry):
        r0 = pl.multiple_of(p * PAGE_SIZE, PAGE_SIZE)
        sk[pl.ds(r0, PAGE_SIZE), :] = c_ref[p, 0].T
        sv[pl.ds(r0, PAGE_SIZE), :] = c_ref[p, 1].T
        return carry

    lax.fori_loop(0, n_pages, to_row_major, 0)
    n = n_pages * CHUNKS_PER_PAGE
    acc = jnp.zeros((n, 4 * LANE), F32)
    for j in range(CMP_STRIDE):
        xj = jnp.concatenate([sk[pl.ds(j, n, stride=CMP_STRIDE), :], sv[pl.ds(j, n, stride=CMP_STRIDE), :]], axis=1)
        acc = acc + _mm(xj, w_ref[0, j])
    o_ref[...] = acc


def cache_compress(cache_fm, nsa_phi):
    n_total = cache_fm.shape[0]
    assert (n_total // DEPTH) % SWEEP_PAGES == 0
    tiles = n_total // DEPTH // SWEEP_PAGES
    wc = jnp.stack([compress_weights_paged(nsa_phi[l]) for l in range(DEPTH)])
    rows = SWEEP_PAGES * PAGE_SIZE
    return pl.pallas_call(
        _cache_compress_body,
        grid=(DEPTH * tiles,),
        in_specs=[pl.BlockSpec((SWEEP_PAGES, 2, LANE, PAGE_SIZE), lambda i: (i, 0, 0, 0)),
                  pl.BlockSpec((1,) + wc.shape[1:], lambda i: (i // tiles, 0, 0, 0))],
        out_specs=pl.BlockSpec((SWEEP_PAGES * CHUNKS_PER_PAGE, 4 * LANE), lambda i: (i, 0)),
        out_shape=jax.ShapeDtypeStruct((n_total * CHUNKS_PER_PAGE, 4 * LANE), F32),
        scratch_shapes=[pltpu.VMEM((rows, LANE), F32), pltpu.VMEM((rows, LANE), F32)],
        compiler_params=_cparams(),
        name="cache_compress",
    )(cache_fm, wc)


def _nsa_sample_body(pt_ref, qa_ref, newb_ref, wnew_ref, gates_ref, *rest):
    pages = rest[:N_PAGES]
    abs_ = rest[N_PAGES:2 * N_PAGES]
    win_ref, bias_ref, gkc_ref, ov_ref, e_ref, y_ref, wout_ref = rest[2 * N_PAGES:]
    qs = qa_ref[0]
    newb = newb_ref[0].astype(F32)
    lane = lax.broadcasted_iota(jnp.int32, (QROWS, LANE), 1)
    row = lax.broadcasted_iota(jnp.int32, (QROWS, LANE), 0)

    ab = jnp.concatenate([a[...] for a in abs_], axis=0)
    out = ab[:, 0:2 * LANE] + pltpu.roll(ab[:, 2 * LANE:4 * LANE], N_CHUNK_S - 1, 0) + bias_ref[...]
    kc = _seg_rmsnorm(out[:, 0:LANE], gkc_ref[...])
    vc = out[:, LANE:2 * LANE]

    s = _mm_nt(qs, kc)
    s = jnp.where(lane < N_CHUNK_S - 1, s, -jnp.inf)
    e = jnp.exp(s - jnp.max(s, axis=-1, keepdims=True))
    p_cmp = e / jnp.sum(e, axis=-1, keepdims=True)
    o_cmp = _mm(p_cmp, vc)
    imp = _mm(p_cmp, ov_ref[...])
    imp = imp + jnp.where(row % GQA == 0, pltpu.roll(imp, QROWS - 1, 0), pltpu.roll(imp, 1, 0))

    valid = lane <= CUR_S
    forced = (lane == 0) | (lane == CUR_S) | (lane == CUR_S - 1)
    score = jnp.where(valid, imp, -jnp.inf)
    score = jnp.where(forced & valid, jnp.inf, score)
    cnt = jnp.zeros((QROWS, LANE), F32)
    for i in range(N_SEL_S):
        ci = score[:, i:i + 1]
        cnt = cnt + jnp.where((ci > score) | ((ci == score) & (lane > i)), 1.0, 0.0)
    sel = jnp.where((cnt < SEL_TOPK) & (score > -jnp.inf), 1.0, 0.0)

    msel = _mm(sel, e_ref[...])
    s = jnp.concatenate([_mm(qs, pg[0, 0]) for pg in pages], axis=1)
    s = jnp.where(msel > 0.5, s, NEG)
    qf = qs.astype(F32)
    s_new = jnp.sum(qf * newb[:, 2 * LANE:3 * LANE], axis=-1, keepdims=True)
    s_new = jnp.where(sel[:, CUR_S:CUR_S + 1] > 0.5, s_new, NEG)
    e, e_new, d = _softmax_with_extra(s, s_new)
    acc_o = e_new.astype(MXU_DTYPE).astype(F32) * newb[:, 3 * LANE:4 * LANE]
    for p, pg in enumerate(pages):
        acc_o = acc_o + _mm_nt(e[:, p * PAGE_SIZE:(p + 1) * PAGE_SIZE], pg[0, 1])
    o_sel = acc_o / d

    s = _mm(qs, win_ref[0, 0])
    s_new = jnp.sum(qf * newb[:, 4 * LANE:5 * LANE], axis=-1, keepdims=True)
    e, e_new, d = _softmax_with_extra(s, s_new)
    o_win = (_mm_nt(e, win_ref[0, 1]) + e_new.astype(MXU_DTYPE).astype(F32) * newb[:, 5 * LANE:6 * LANE]) / d

    g = gates_ref[0]
    o = g[:, 0:1] * o_cmp + g[:, 1:2] * o_sel + g[:, 2:3] * o_win
    o_sw = pltpu.roll(o, HEAD_DIM, 1)
    lane1 = lax.broadcasted_iota(jnp.int32, (1, LANE), 1)
    ys = []
    for h in range(N_KV):
        a = (o if h == 0 else o_sw)[GQA * h:GQA * h + 1]
        b = (o if h == 1 else o_sw)[GQA * h + 1:GQA * h + 2]
        ys.append(jnp.where(lane1 < HEAD_DIM, a, b))
    y_ref[0] = jnp.concatenate(ys, axis=1)

    lw = win_ref.shape[3]
    last = lax.broadcasted_iota(jnp.int32, (LANE, lw), 1) == lw - 1
    for c in range(2):
        col = jnp.broadcast_to(wnew_ref[0][:, c * LANE:(c + 1) * LANE], (QROWS, LANE)).T[:, 0:1]
        wout_ref[0, c] = jnp.where(last, col, pltpu.roll(win_ref[0, c], lw - 1, 1))


def _sample_constants():
    ci = np.arange(LANE)[:, None] * CMP_STRIDE
    sj = np.arange(LANE)[None, :] * SEL_BLOCK
    ov = ((ci < sj + SEL_BLOCK) & (ci + CMP_BLOCK > sj) & (np.arange(LANE)[:, None] < N_CHUNK_S - 1)
          & (np.arange(LANE)[None, :] < N_SEL_S))
    e = (np.arange(LANE)[:, None] == (np.arange(PAST_LEN)[None, :] // SEL_BLOCK))
    return jnp.asarray(ov, MXU_DTYPE), jnp.asarray(e, MXU_DTYPE)


def nsa_sample_pallas(proj, layer, cache_fm, cache_ab, page_table, win_fm, phi_b, qk_g):
    B_ = proj.shape[0]
    n_phys = cache_fm.shape[0] // DEPTH
    lw = win_fm.shape[3]
    assert page_table.shape == (B_, N_PAGES) and lw <= WINDOW and lw <= PAST_LEN and CUR_S == N_SEL_S - 1
    qa, kvb, _, rows_t, _, wnew, gates = nsa_prep(proj, qk_g, 1, B_)
    qa8 = jnp.pad(qa.astype(F32).reshape(B_, N_HEADS, LANE), ((0, 0), (0, QROWS - N_HEADS), (0, 0)))
    gates8 = jnp.pad(gates[:, :3 * N_HEADS].reshape(B_, N_HEADS, 3), ((0, 0), (0, QROWS - N_HEADS), (0, LANE - 3)))
    ov, e = _sample_constants()
    bias = jnp.concatenate([jnp.tile(phi_b[0], 2), jnp.tile(phi_b[1], 2)]).reshape(1, 2 * LANE)

    def page_spec(p):
        return pl.BlockSpec((1, 2, LANE, PAGE_SIZE), lambda b, pt: (layer * n_phys + pt[b, p], 1, 0, 0))

    def ab_spec(p):
        return pl.BlockSpec((CHUNKS_PER_PAGE, 4 * LANE), lambda b, pt: (layer * n_phys + pt[b, p], 0))

    def per_b(shape):
        return pl.BlockSpec((1,) + shape, lambda b, pt: (b, 0, 0))

    def const(a):
        return pl.BlockSpec(a.shape, lambda b, pt: (0,) * a.ndim)

    gkc = jnp.tile(qk_g[1], 2).reshape(1, LANE)
    y, wout = pl.pallas_call(
        _nsa_sample_body,
        grid_spec=pltpu.PrefetchScalarGridSpec(
            num_scalar_prefetch=1,
            grid=(B_,),
            in_specs=[per_b((QROWS, LANE)), per_b((1, 6 * LANE)), per_b((1, 2 * LANE)), per_b((QROWS, LANE))]
                     + [page_spec(p) for p in range(N_PAGES)] + [ab_spec(p) for p in range(N_PAGES)]
                     + [pl.BlockSpec((1, 2, LANE, lw), lambda b, pt: (layer * B_ + b, 0, 0, 0)),
                        const(bias), const(gkc), const(ov), const(e)],
            out_specs=[per_b((1, 2 * LANE)), pl.BlockSpec((1, 2, LANE, lw), lambda b, pt: (b, 0, 0, 0))]),
        out_shape=[jax.ShapeDtypeStruct((B_, 1, 2 * LANE), F32),
                   jax.ShapeDtypeStruct((B_, 2, LANE, lw), F32)],
        compiler_params=_cparams(),
        name="nsa_sample",
    )(page_table, qa8, kvb.reshape(B_, 1, 6 * LANE), wnew.reshape(B_, 1, 2 * LANE), gates8,
      *([cache_fm] * N_PAGES), *([cache_ab] * N_PAGES), win_fm, bias, gkc, ov, e)
    rows = rows_t.reshape(4, N_KV, HEAD_DIM, B_).transpose(3, 0, 1, 2)[:, None]
    return (y.reshape(B_, 1, N_HEADS * HEAD_DIM), rows,
            wout.reshape(B_, 2, N_KV, HEAD_DIM, lw).transpose(0, 4, 1, 2, 3))


MIX_CHUNK = 512
HALO = 16
YM_W = POOL_W + RG_W + SC_W


def _expm1(x):
    p = jnp.full_like(x, 1.0 / 3628800.0)
    for c in (1.0 / 362880.0, 1.0 / 40320.0, 1.0 / 5040.0, 1.0 / 720.0, 1.0 / 120.0, 1.0 / 24.0, 1.0 / 6.0, 0.5, 1.0):
        p = p * x + c
    return jnp.where(jnp.abs(x) < 0.25, p * x, jnp.exp(x) - 1.0)


def _softplus(x):
    return jnp.maximum(x, 0.0) + jnp.log1p(jnp.exp(-jnp.abs(x)))


def _gelu_tanh(x):
    return 0.5 * x * (1.0 + jnp.tanh(np.sqrt(2.0 / np.pi).astype(np.float32) * (x + 0.044715 * (x * x * x))))


def _rg_coeffs(xc, wa, ba, wx, bx, lam):
    r = jax.nn.sigmoid(_mm(xc, wa) + ba)
    ig = jax.nn.sigmoid(_mm(xc, wx) + bx)
    log_a = (-RG_C * r) * _softplus(-lam)
    return jnp.exp(log_a), jnp.sqrt(-_expm1(2.0 * log_a)) * (ig * xc)


def _pool_select(s2, s4, s8, s16):
    lane = lax.broadcasted_iota(jnp.int32, s2.shape, 1)
    return jnp.where(lane < POOL_GROUP, s2, jnp.where(lane < 2 * POOL_GROUP, s4,
                                                      jnp.where(lane < 3 * POOL_GROUP, s8, s16)))


def _pool_count(pos, shape):
    lane = lax.broadcasted_iota(jnp.int32, shape, 1)
    win = jnp.left_shift(2, lane // POOL_GROUP)
    return jnp.minimum(win, pos + 1).astype(F32)


def _mixers_prompt_body(pu_ref, rx_ref, rg_ref, z_ref, bg_ref, cg_ref, pw_ref, ps_ref, cw_ref, cb_ref, wa_ref, ba_ref,
                        wx_ref, bx_ref, lam_ref, scw_ref, scb_ref, ym_ref, tails_ref, hlast_ref, halo, hcar):
    c = pl.program_id(1)
    tc = pu_ref.shape[0]

    @pl.when(c == 0)
    def _():
        halo[...] = jnp.zeros_like(halo)
        hcar[...] = jnp.zeros_like(hcar)

    pu, rx = pu_ref[...], rx_ref[...]
    u = cg_ref[...] * z_ref[...]
    ext = [jnp.concatenate([halo[i], v], axis=0) for i, v in enumerate((pu, rx, u))]

    def back(e, k):
        return pltpu.roll(e, k, 0)

    s2 = ext[0] + back(ext[0], 1)
    s4 = s2 + back(s2, 2)
    s8 = s4 + back(s4, 4)
    s16 = s8 + back(s8, 8)
    tot = _pool_select(s2, s4, s8, s16)[HALO:]
    pos = c * tc + lax.broadcasted_iota(jnp.int32, (tc, POOL_W), 0)
    d = tot / _pool_count(pos, (tc, POOL_W)) - pu
    ym_ref[:, 0:POOL_W] = _mm(d, pw_ref[...]) * ps_ref[...]

    cw = cw_ref[...]
    xc = cb_ref[...] + cw[RG_CONV - 1:RG_CONV] * rx
    for k in range(1, RG_CONV):
        xc = xc + cw[RG_CONV - 1 - k:RG_CONV - k] * back(ext[1], k)[HALO:]
    a, b = _rg_coeffs(xc, wa_ref[...], ba_ref[...], wx_ref[...], bx_ref[...], lam_ref[...])
    row = lax.broadcasted_iota(jnp.int32, (tc, RG_W), 0)
    k = 1
    while k < tc:
        a_prev = jnp.where(row < k, 1.0, pltpu.roll(a, k, 0))
        b_prev = jnp.where(row < k, 0.0, pltpu.roll(b, k, 0))
        b = a * b_prev + b
        a = a * a_prev
        k *= 2
    h = a * hcar[0:1] + b
    hcar[...] = jnp.broadcast_to(h[tc - 1:tc], hcar.shape)
    hlast_ref[0] = jnp.broadcast_to(h[tc - 1:tc], hcar.shape)
    ym_ref[:, POOL_W:POOL_W + RG_W] = h * _gelu_tanh(rg_ref[...])

    scw = scw_ref[...]
    v = scb_ref[...] + scw[SC_CONV - 1:SC_CONV] * u
    for k in range(1, SC_CONV):
        v = v + scw[SC_CONV - 1 - k:SC_CONV - k] * back(ext[2], k)[HALO:]
    ym_ref[:, POOL_W + RG_W:YM_W] = bg_ref[...] * v

    for i, val in enumerate((pu, rx, u)):
        halo[i] = val[tc - HALO:]
        tails_ref[0, i] = val[tc - HALO:]


def _block_diag(w):
    g, n, _ = w.shape
    return jnp.einsum('gij,gh->gihj', w, jnp.eye(g, dtype=w.dtype)).reshape(g * n, g * n)


def _mixer_params(lw):
    row = lambda a: a.reshape(1, -1)
    return [_block_diag(lw['pool_w']).astype(MXU_DTYPE), row(lw['pool_scale']), lw['rg_conv_w'], row(lw['rg_conv_b']),
            _block_diag(lw['rg_w_a']).astype(MXU_DTYPE), row(lw['rg_b_a']),
            _block_diag(lw['rg_w_x']).astype(MXU_DTYPE), row(lw['rg_b_x']), row(lw['rg_lambda']),
            lw['sc_conv_w'], row(lw['sc_conv_b'])]


def _proj_col_specs(rows, index):
    cols = (COL_POOL, COL_RX, COL_RGATE, COL_SC, COL_SC + SC_W, COL_SC + 2 * SC_W)
    return [pl.BlockSpec((rows, GROUP_W), functools.partial(index, col // GROUP_W)) for col in cols]


def mixers_prompt(proj, lw, B_, S):
    tc = min(MIX_CHUNK, S)
    nc = S // tc
    params = _mixer_params(lw)
    fixed = lambda a: pl.BlockSpec(a.shape, lambda b, c: (0,) * a.ndim)
    return pl.pallas_call(
        _mixers_prompt_body,
        grid=(B_, nc),
        in_specs=_proj_col_specs(tc, lambda col, b, c: (b * nc + c, col)) + [fixed(a) for a in params],
        out_specs=[pl.BlockSpec((tc, YM_W), lambda b, c: (b * nc + c, 0)),
                   pl.BlockSpec((1, 3, HALO, GROUP_W), lambda b, c: (b, 0, 0, 0)),
                   pl.BlockSpec((1, 8, RG_W), lambda b, c: (b, 0, 0))],
        out_shape=[jax.ShapeDtypeStruct((B_ * S, YM_W), F32),
                   jax.ShapeDtypeStruct((B_, 3, HALO, GROUP_W), F32),
                   jax.ShapeDtypeStruct((B_, 8, RG_W), F32)],
        scratch_shapes=[pltpu.VMEM((3, HALO, GROUP_W), F32), pltpu.VMEM((8, RG_W), F32)],
        compiler_params=_cparams(2),
        name="mixers_prompt",
    )(*([proj] * 6), *params)


def _mixers_sample_body(pos0, pu_ref, rx_ref, rg_ref, z_ref, bg_ref, cg_ref, pp_ref, rp_ref, h0_ref, sp_ref, pw_ref,
                        ps_ref, cw_ref, cb_ref, wa_ref, ba_ref, wx_ref, bx_ref, lam_ref, scw_ref, scb_ref,
                        ym_ref, pn_ref, rn_ref, hn_ref, sn_ref):
    pu, rx = pu_ref[...], rx_ref[...]
    u = cg_ref[...] * z_ref[...]
    run, sums = pu, {}
    for k in range(1, POOL_KEEP + 1):
        run = run + pp_ref[POOL_KEEP - k]
        sums[k + 1] = run
    tot = _pool_select(*(sums[w] for w in POOL_WINDOWS))
    d = tot / _pool_count(pos0, pu.shape) - pu
    ym_ref[:, 0:POOL_W] = _mm(d, pw_ref[...]) * ps_ref[...]
    for k in range(POOL_KEEP - 1):
        pn_ref[k] = pp_ref[k + 1]
    pn_ref[POOL_KEEP - 1] = pu

    cw = cw_ref[...]
    xc = cb_ref[...] + cw[RG_CONV - 1:RG_CONV] * rx
    for k in range(RG_CONV - 1):
        xc = xc + cw[k:k + 1] * rp_ref[k]
    a, b = _rg_coeffs(xc, wa_ref[...], ba_ref[...], wx_ref[...], bx_ref[...], lam_ref[...])
    h = b + a * h0_ref[...]
    hn_ref[...] = h
    ym_ref[:, POOL_W:POOL_W + RG_W] = h * _gelu_tanh(rg_ref[...])
    for k in range(RG_CONV - 2):
        rn_ref[k] = rp_ref[k + 1]
    rn_ref[RG_CONV - 2] = rx

    scw = scw_ref[...]
    v = scb_ref[...] + scw[SC_CONV - 1:SC_CONV] * u
    for k in range(SC_CONV - 1):
        v = v + scw[k:k + 1] * sp_ref[k]
    ym_ref[:, POOL_W + RG_W:YM_W] = bg_ref[...] * v
    for k in range(SC_CONV - 2):
        sn_ref[k] = sp_ref[k + 1]
    sn_ref[SC_CONV - 2] = u


def mixers_sample(proj, lw, pos0, pool_prev, rgc_prev, h0, sc_prev):
    B_ = proj.shape[0]
    params = _mixer_params(lw)
    states = [pool_prev.transpose(1, 0, 2), rgc_prev.transpose(1, 0, 2), h0, sc_prev.transpose(1, 0, 2)]
    full = lambda a: pl.BlockSpec(a.shape, lambda i: (0,) * a.ndim)
    ym, pn, rn, hn, sn = pl.pallas_call(
        functools.partial(_mixers_sample_body, pos0),
        grid=(1,),
        in_specs=_proj_col_specs(B_, lambda col, i: (0, col)) + [full(a) for a in states] + [full(a) for a in params],
        out_specs=[pl.BlockSpec((B_, YM_W), lambda i: (0, 0))] + [full(a) for a in states],
        out_shape=[jax.ShapeDtypeStruct((B_, YM_W), F32)] + [jax.ShapeDtypeStruct(a.shape, F32) for a in states],
        compiler_params=_cparams(),
        name="mixers_sample",
    )(*([proj] * 6), *states, *params)
    return ym, pn.transpose(1, 0, 2), rn.transpose(1, 0, 2), hn, sn.transpose(1, 0, 2)


ROUTE_W = LANE
GROUP_LANE0 = N_EXPERTS
MOE_TILE_PROMPT = 256
MOE_TILE_SAMPLE = 32
COMBINE_TILE = 256


def _rms(x, g):
    return x * lax.rsqrt(jnp.mean(x * x, axis=-1, keepdims=True) + EPS) * g


def _mix_out_router_body(ym_ref, yn_ref, x_ref, og_ref, wo_ref, gf_ref, wr_ref, br_ref, x2_ref, xn_ref, route_ref):
    og = og_ref[...]
    groups = (ym_ref[:, 0:POOL_W], ym_ref[:, POOL_W:POOL_W + RG_W], yn_ref[...], ym_ref[:, POOL_W + RG_W:YM_W])
    yn = jnp.concatenate([_rms(y, og[:, i * GROUP_W:(i + 1) * GROUP_W]) for i, y in enumerate(groups)], axis=1)
    x2 = x_ref[...] + _mm(yn, wo_ref[...])
    x2_ref[...] = x2
    xn = _rms(x2, gf_ref[...])
    xn_ref[...] = xn
    logits = _mm(xn, wr_ref[...]) + br_ref[...]
    lane = lax.broadcasted_iota(jnp.int32, logits.shape, 1)
    is_grp = (lane >= GROUP_LANE0) & (lane < GROUP_LANE0 + N_GROUPS)
    grp = jnp.where(is_grp, logits, -jnp.inf)
    gmax = jnp.max(grp, axis=-1, keepdims=True)
    gsel = jnp.min(jnp.where(grp == gmax, lane - GROUP_LANE0, N_GROUPS), axis=-1, keepdims=True)
    p_group = 1.0 / jnp.sum(jnp.where(is_grp, jnp.exp(logits - gmax), 0.0), axis=-1, keepdims=True)
    le = jnp.where((lane < N_EXPERTS) & (lane // EXP_PER_GROUP == gsel), logits, -jnp.inf)
    m1 = jnp.max(le, axis=-1, keepdims=True)
    i1 = jnp.min(jnp.where(le == m1, lane, LANE), axis=-1, keepdims=True)
    le2 = jnp.where(lane == i1, -jnp.inf, le)
    m2 = jnp.max(le2, axis=-1, keepdims=True)
    i2 = jnp.min(jnp.where(le2 == m2, lane, LANE), axis=-1, keepdims=True)
    e2 = jnp.exp(m2 - m1)
    g1 = p_group * (1.0 / (1.0 + e2))
    g2 = p_group * (e2 / (1.0 + e2))
    route_ref[...] = jnp.where(lane == 0, i1.astype(F32), jnp.where(lane == 1, i2.astype(F32),
                               jnp.where(lane == 2, g1, jnp.where(lane == 3, g2, 0.0))))


def mix_out_router(ym, y_nsa, x2d, lw):
    T, D = x2d.shape
    tm = min(256, T)
    wr = jnp.concatenate([lw['router_expert_w'], lw['router_group_w'],
                          jnp.zeros((D, ROUTE_W - N_EXPERTS - N_GROUPS), F32)], axis=1).astype(MXU_DTYPE)
    br = jnp.concatenate([lw['router_expert_b'], lw['router_group_b'],
                          jnp.zeros((ROUTE_W - N_EXPERTS - N_GROUPS,), F32)]).reshape(1, ROUTE_W)
    row = lambda i: (i, 0)
    fixed = lambda i: (0, 0)
    return pl.pallas_call(
        _mix_out_router_body,
        grid=(T // tm,),
        in_specs=[pl.BlockSpec((tm, YM_W), row), pl.BlockSpec((tm, GROUP_W), row), pl.BlockSpec((tm, D), row),
                  pl.BlockSpec((1, MIX_W), fixed),
                  pl.BlockSpec((MIX_W, D), fixed), pl.BlockSpec((1, D), fixed), pl.BlockSpec((D, ROUTE_W), fixed),
                  pl.BlockSpec((1, ROUTE_W), fixed)],
        out_specs=[pl.BlockSpec((tm, D), row), pl.BlockSpec((tm, D), row), pl.BlockSpec((tm, ROUTE_W), row)],
        out_shape=[jax.ShapeDtypeStruct((T, D), F32), jax.ShapeDtypeStruct((T, D), F32),
                   jax.ShapeDtypeStruct((T, ROUTE_W), F32)],
        compiler_params=_cparams(),
        name="mix_out_router",
    )(ym, y_nsa, x2d, lw['mix_out_g'].reshape(1, MIX_W), lw['w_out'].astype(MXU_DTYPE),
      lw['norm_ffn_g'].reshape(1, D), wr, br)


def moe_schedule(eidx, tile):
    T = eidx.shape[0]
    M = T * TOP_E
    fe = eidx.reshape(M)
    onehot = (fe[:, None] == jnp.arange(N_EXPERTS, dtype=jnp.int32)[None, :]).astype(jnp.int32)
    csum = jnp.cumsum(onehot, axis=0)
    rank = jnp.take_along_axis(csum, fe[:, None], axis=1)[:, 0] - 1
    counts = csum[-1]
    padded = (counts + tile - 1) // tile * tile
    pad_end = jnp.cumsum(padded)
    dest = (pad_end - padded)[fe] + rank
    n_blk = -(-M // tile) + N_EXPERTS
    tok = jnp.arange(M, dtype=jnp.int32) // TOP_E
    buf_tok = jnp.zeros((n_blk * tile,), jnp.int32).at[dest].set(tok)
    blk_exp = jnp.minimum(jnp.searchsorted(pad_end, jnp.arange(n_blk, dtype=jnp.int32) * tile, side='right'),
                          N_EXPERTS - 1).astype(jnp.int32)
    n_used = (pad_end[-1:] // tile).astype(jnp.int32)
    return buf_tok, blk_exp, n_used, dest.astype(jnp.int32)


def _moe_ffn_body(tile, tok_ref, bexp_ref, nused_ref, x_hbm, wgu_ref, wdn_ref, y_ref, xg, sem, wgu_bf, wdn_bf):
    j = pl.program_id(0)
    n = nused_ref[0]

    def gather(blk, slot):
        def body(r, c):
            t = tok_ref[blk * tile + r]
            pltpu.make_async_copy(x_hbm.at[pl.ds(t, 1)], xg.at[slot, pl.ds(r, 1)], sem.at[slot]).start()
            return c
        lax.fori_loop(0, tile, body, 0, unroll=8)

    @pl.when((j == 0) & (n > 0))
    def _():
        gather(0, 0)

    @pl.when(j < n)
    def _():
        slot = j % 2

        @pl.when(j + 1 < n)
        def _():
            gather(j + 1, 1 - slot)

        @pl.when((j == 0) | (bexp_ref[j] != bexp_ref[jnp.maximum(j - 1, 0)]))
        def _():
            wgu_bf[...] = wgu_ref[0].astype(wgu_bf.dtype)
            wdn_bf[...] = wdn_ref[0].astype(wdn_bf.dtype)

        pltpu.make_async_copy(x_hbm.at[pl.ds(0, tile)], xg.at[slot], sem.at[slot]).wait()
        h = _mm(xg[slot], wgu_bf[...])
        a, b = h[:, :D_EXPERT], h[:, D_EXPERT:]
        y_ref[...] = _mm(a * jax.nn.sigmoid(a) * b, wdn_bf[...])

    @pl.when(j >= n)
    def _():
        y_ref[...] = jnp.zeros_like(y_ref)


def moe_ffn_pallas(xn, buf_tok, blk_exp, n_used, w_gu, w_down, tile):
    T, D = xn.shape
    n_blk = blk_exp.shape[0]
    return pl.pallas_call(
        functools.partial(_moe_ffn_body, tile),
        grid_spec=pltpu.PrefetchScalarGridSpec(
            num_scalar_prefetch=3,
            grid=(n_blk,),
            in_specs=[pl.BlockSpec(memory_space=pl.ANY),
                      pl.BlockSpec((1, D, 2 * D_EXPERT), lambda j, tok, bexp, nu: (bexp[j], 0, 0)),
                      pl.BlockSpec((1, D_EXPERT, D), lambda j, tok, bexp, nu: (bexp[j], 0, 0))],
            out_specs=pl.BlockSpec((tile, D), lambda j, tok, bexp, nu: (j, 0)),
            scratch_shapes=[pltpu.VMEM((2, tile, D), F32), pltpu.SemaphoreType.DMA((2,)),
                            pltpu.VMEM((D, 2 * D_EXPERT), MXU_DTYPE), pltpu.VMEM((D_EXPERT, D), MXU_DTYPE)]),
        out_shape=jax.ShapeDtypeStruct((n_blk * tile, D), F32),
        compiler_params=_cparams(),
        name="moe_ffn",
    )(buf_tok, blk_exp, n_used, xn, w_gu, w_down)


def _moe_combine_body(tm, slots_ref, y_hbm, x2_ref, route_ref, o_ref, yb, sem):
    i = pl.program_id(0)
    nt = pl.num_programs(0)

    def gather(tile_i, buf):
        def body(r, c):
            for k in range(TOP_E):
                s = slots_ref[(tile_i * tm + r) * TOP_E + k]
                pltpu.make_async_copy(y_hbm.at[pl.ds(s, 1)], yb.at[buf, k, pl.ds(r, 1)], sem.at[buf]).start()
            return c
        lax.fori_loop(0, tm, body, 0, unroll=8)

    @pl.when(i == 0)
    def _():
        gather(0, 0)

    buf = i % 2

    @pl.when(i + 1 < nt)
    def _():
        gather(i + 1, 1 - buf)

    for k in range(TOP_E):
        pltpu.make_async_copy(y_hbm.at[pl.ds(0, tm)], yb.at[buf, k], sem.at[buf]).wait()
    r = route_ref[...]
    o_ref[...] = x2_ref[...] + (r[:, 2:3] * yb[buf, 0] + r[:, 3:4] * yb[buf, 1])


def moe_combine_pallas(y, slots, x2, route):
    T, D = x2.shape
    tm = min(COMBINE_TILE, T)
    return pl.pallas_call(
        functools.partial(_moe_combine_body, tm),
        grid_spec=pltpu.PrefetchScalarGridSpec(
            num_scalar_prefetch=1,
            grid=(T // tm,),
            in_specs=[pl.BlockSpec(memory_space=pl.ANY),
                      pl.BlockSpec((tm, D), lambda i, s: (i, 0)),
                      pl.BlockSpec((tm, ROUTE_W), lambda i, s: (i, 0))],
            out_specs=pl.BlockSpec((tm, D), lambda i, s: (i, 0)),
            scratch_shapes=[pltpu.VMEM((2, TOP_E, tm, D), F32), pltpu.SemaphoreType.DMA((2,))]),
        out_shape=jax.ShapeDtypeStruct((T, D), F32),
        compiler_params=_cparams(),
        name="moe_combine",
    )(slots, y, x2, route)


def mix_out_moe(ym, y_nsa, x2d, lw, tile):
    x2, xn, route = mix_out_router(ym, y_nsa, x2d, lw)
    eidx = route[:, 0:TOP_E].astype(jnp.int32)
    buf_tok, blk_exp, n_used, slots = moe_schedule(eidx, tile)
    y = moe_ffn_pallas(xn, buf_tok, blk_exp, n_used, lw['exp_w_gu'], lw['exp_w_down'], tile)
    return moe_combine_pallas(y, slots, x2, route)


def rmsnorm(x, g):
    xf = x.astype(jnp.float32)
    y = xf * lax.rsqrt(jnp.mean(xf * xf, axis=-1, keepdims=True) + EPS)
    return (y * g.astype(jnp.float32)).astype(x.dtype)


def split_cols(a, sizes):
    outs, o = [], 0
    for s in sizes:
        outs.append(a[..., o:o + s])
        o += s
    return outs


def causal_dwconv(u, prev, w, b):
    L = u.shape[1]
    ext = jnp.concatenate([prev.astype(u.dtype), u], axis=1)
    y = lax.conv_general_dilated(ext, w[:, None, :].astype(u.dtype), window_strides=(1,), padding='VALID',
                                 dimension_numbers=('NWC', 'WIO', 'NWC'), feature_group_count=u.shape[-1])
    return y + b.astype(u.dtype), ext[:, L:]


def pool_mixer(u, prev, pos0, w, scale):
    B_, L, C = u.shape
    ext = jnp.concatenate([prev.astype(u.dtype), u], axis=1)
    ef = ext.astype(jnp.float32)
    cs = jnp.concatenate([jnp.zeros((B_, 1, C), jnp.float32), jnp.cumsum(ef, axis=1)], axis=1)
    pos = pos0 + jnp.arange(L)
    means = []
    for g, win in enumerate(POOL_WINDOWS):
        sl = slice(g * POOL_GROUP, (g + 1) * POOL_GROUP)
        tot = cs[:, POOL_KEEP + 1:POOL_KEEP + 1 + L, sl] - cs[:, POOL_KEEP + 1 - win:POOL_KEEP + 1 - win + L, sl]
        cnt = jnp.minimum(win, pos + 1).astype(jnp.float32)
        means.append(tot / cnt[None, :, None])
    d = (jnp.concatenate(means, axis=-1) - ef[:, POOL_KEEP:]).astype(u.dtype)
    y = jnp.einsum('blgc,gcd->blgd', d.reshape(B_, L, len(POOL_WINDOWS), POOL_GROUP), w).reshape(B_, L, C)
    return y * scale, ext[:, L:]


def rglru_mixer(xb, gb, conv_prev, h0, conv_w, conv_b, w_a, b_a, w_x, b_x, lam):
    B_, L, C = xb.shape
    xc, conv_new = causal_dwconv(xb, conv_prev, conv_w, conv_b)
    xh = xc.reshape(B_, L, RG_HEADS, RG_BLOCK)
    r = jax.nn.sigmoid(jnp.einsum('blhi,hij->blhj', xh, w_a).reshape(B_, L, C) + b_a)
    ig = jax.nn.sigmoid(jnp.einsum('blhi,hij->blhj', xh, w_x).reshape(B_, L, C) + b_x)
    log_a = -RG_C * r.astype(jnp.float32) * jax.nn.softplus(-lam.astype(jnp.float32))
    a = jnp.exp(log_a)
    bt = jnp.sqrt(-jnp.expm1(2.0 * log_a)) * (ig * xc).astype(jnp.float32)
    bt = bt.at[:, 0].add(a[:, 0] * h0.astype(jnp.float32))
    _, h = lax.associative_scan(lambda e1, e2: (e1[0] * e2[0], e2[0] * e1[1] + e2[1]), (a, bt), axis=1)
    y = h.astype(xb.dtype) * jax.nn.gelu(gb)
    return y, conv_new, h[:, -1].astype(xb.dtype)


def masked_softmax(s, mask):
    s = jnp.where(mask, s.astype(jnp.float32), -jnp.inf)
    m = jnp.max(s, axis=-1, keepdims=True)
    e = jnp.exp(s - jnp.where(jnp.isfinite(m), m, 0.0))
    d = jnp.sum(e, axis=-1, keepdims=True)
    return e / jnp.where(d > 0, d, 1.0)


def nsa_compress(k_raw, v_raw, phi, phi_b, g_kc):
    B_, T = k_raw.shape[:2]
    R = CMP_BLOCK // CMP_STRIDE
    nch = T // CMP_STRIDE
    ncmp = nch - (R - 1)

    def comp(a, w, bias):
        ch = a[:, :nch * CMP_STRIDE].reshape(B_, nch, CMP_STRIDE, N_KV, HEAD_DIM)
        ch = ch.transpose(0, 1, 3, 2, 4).reshape(B_, nch, N_KV, CMP_STRIDE * HEAD_DIM)
        wr = w.reshape(R, CMP_STRIDE * HEAD_DIM, HEAD_DIM)
        out = jnp.einsum('bckf,fd->bckd', ch[:, 0:ncmp], wr[0])
        for r in range(1, R):
            out = out + jnp.einsum('bckf,fd->bckd', ch[:, r:r + ncmp], wr[r])
        return out + bias

    kc = rmsnorm(comp(k_raw, phi[0], phi_b[0]), g_kc)
    vc = comp(v_raw, phi[1], phi_b[1])
    cmp_end = jnp.arange(ncmp) * CMP_STRIDE + (CMP_BLOCK - 1)
    return kc, vc, cmp_end


def sel_blocks(a):
    B_, T = a.shape[:2]
    n_sel = -(-T // SEL_BLOCK)
    a = jnp.pad(a, ((0, 0), (0, n_sel * SEL_BLOCK - T), (0, 0), (0, 0)))
    return a.reshape(B_, n_sel, SEL_BLOCK, N_KV, HEAD_DIM).transpose(0, 3, 1, 2, 4)


def nsa_attend(q, q_pos, gates, kc, vc, cmp_end, ks_blk, vs_blk, kw, vw, w_pos):
    dt = q.dtype
    B_, Q = q.shape[:2]
    t = q_pos[:, None]
    s = jnp.einsum('bqkgd,bckd->bqkgc', q, kc)
    p_cmp = masked_softmax(s, (cmp_end[None, :] <= t)[None, :, None, None, :])
    o_cmp = jnp.einsum('bqkgc,bckd->bqkgd', p_cmp.astype(dt), vc)
    n_sel = ks_blk.shape[2]
    ci = jnp.arange(kc.shape[1])[:, None] * CMP_STRIDE
    sj = jnp.arange(n_sel)[None, :] * SEL_BLOCK
    overlap = ((ci < sj + SEL_BLOCK) & (ci + CMP_BLOCK > sj)).astype(jnp.float32)
    imp = jnp.einsum('bqkgc,cs->bqks', p_cmp, overlap)
    blk = jnp.arange(n_sel)[None, :]
    cur = t // SEL_BLOCK
    valid = blk <= cur
    forced = (blk == 0) | (blk == cur) | (blk == cur - 1)
    score = jnp.where(valid[None, :, None, :], imp, -jnp.inf)
    score = jnp.where((forced & valid)[None, :, None, :], jnp.inf, score)
    top_v, top_i = lax.top_k(score, min(SEL_TOPK, n_sel))
    kk = top_i.shape[-1]
    bi = jnp.arange(B_)[:, None, None, None]
    hi = jnp.arange(N_KV)[None, None, :, None]
    ks = ks_blk[bi, hi, top_i].reshape(B_, Q, N_KV, kk * SEL_BLOCK, HEAD_DIM)
    vs = vs_blk[bi, hi, top_i].reshape(B_, Q, N_KV, kk * SEL_BLOCK, HEAD_DIM)
    spos = (top_i[..., None] * SEL_BLOCK + jnp.arange(SEL_BLOCK)).reshape(B_, Q, N_KV, kk * SEL_BLOCK)
    smask = (spos <= q_pos[None, :, None, None]) & jnp.repeat(top_v > -jnp.inf, SEL_BLOCK, axis=-1)
    s = jnp.einsum('bqkgd,bqknd->bqkgn', q, ks)
    o_sel = jnp.einsum('bqkgn,bqknd->bqkgd', masked_softmax(s, smask[:, :, :, None, :]).astype(dt), vs)
    wd = t - w_pos[None, :]
    wmask = (w_pos[None, :] >= 0) & (wd >= 0) & (wd <= WINDOW)
    s = jnp.einsum('bqkgd,bnkd->bqkgn', q, kw)
    o_win = jnp.einsum('bqkgn,bnkd->bqkgd', masked_softmax(s, wmask[None, :, None, None, :]).astype(dt), vw)
    return gates[..., 0:1] * o_cmp + gates[..., 1:2] * o_sel + gates[..., 2:3] * o_win


def nsa_prompt(q, gates, kc_raw, vc_raw, ksel, vsel, kwin, vwin, phi, phi_b, g_kc):
    B_, S = q.shape[:2]
    kc, vc, cmp_end = nsa_compress(kc_raw, vc_raw, phi, phi_b, g_kc)
    ks_blk, vs_blk = sel_blocks(ksel), sel_blocks(vsel)
    zpad = jnp.zeros((B_, WINDOW, N_KV, HEAD_DIM), kwin.dtype)
    kw_pad = jnp.concatenate([zpad, kwin], axis=1)
    vw_pad = jnp.concatenate([zpad, vwin], axis=1)
    nq = S // Q_BLOCK

    def body(args):
        qc, gc, i = args
        start = i * Q_BLOCK
        kw = lax.dynamic_slice_in_dim(kw_pad, start, WINDOW + Q_BLOCK, axis=1)
        vw = lax.dynamic_slice_in_dim(vw_pad, start, WINDOW + Q_BLOCK, axis=1)
        return nsa_attend(qc, start + jnp.arange(Q_BLOCK), gc, kc, vc, cmp_end, ks_blk, vs_blk,
                          kw, vw, start - WINDOW + jnp.arange(WINDOW + Q_BLOCK))

    qb = q.reshape(B_, nq, Q_BLOCK, N_KV, GQA, HEAD_DIM).swapaxes(0, 1)
    gb = gates.reshape(B_, nq, Q_BLOCK, N_KV, GQA, 3).swapaxes(0, 1)
    o = lax.map(body, (qb, gb, jnp.arange(nq)))
    o = o.swapaxes(0, 1).reshape(B_, S, N_HEADS * HEAD_DIM)
    rows = jnp.stack([kc_raw, vc_raw, ksel, vsel], axis=2)
    win_new = jnp.stack([kwin, vwin], axis=2)[:, S - min(WINDOW, S):]
    return o, rows, win_new


def nsa_sample(pool, page_table, win_buf, q, gates, kc_raw, vc_raw, ksel, vsel, kwin, vwin, phi, phi_b, g_kc):
    B_, L = q.shape[:2]
    past = pool[page_table]
    past = past.reshape(B_, past.shape[1] * past.shape[2], 4, N_KV, HEAD_DIM)
    P = past.shape[1]
    rows = jnp.stack([kc_raw, vc_raw, ksel, vsel], axis=2)
    full = jnp.concatenate([past.astype(rows.dtype), rows], axis=1)
    kc, vc, cmp_end = nsa_compress(full[:, :, 0], full[:, :, 1], phi, phi_b, g_kc)
    ks_blk, vs_blk = sel_blocks(full[:, :, 2]), sel_blocks(full[:, :, 3])
    Lw = win_buf.shape[1]
    new_w = jnp.stack([kwin, vwin], axis=2)
    wfull = jnp.concatenate([win_buf.astype(new_w.dtype), new_w], axis=1)
    o = nsa_attend(q, P + jnp.arange(L), gates, kc, vc, cmp_end, ks_blk, vs_blk,
                   wfull[:, :, 0], wfull[:, :, 1], P - Lw + jnp.arange(Lw + L))
    return o.reshape(B_, L, N_HEADS * HEAD_DIM), rows, wfull[:, L:]


def expert_dispatch(xt, eidx, gate, w_gu, w_down):
    T, D = xt.shape
    M = T * TOP_E
    fe = eidx.reshape(M)
    ftok = jnp.arange(M, dtype=jnp.int32) // TOP_E
    fgate = gate.reshape(M)
    order = jnp.argsort(fe)
    se, stok, sgate = fe[order], ftok[order], fgate[order]
    counts = jnp.bincount(fe, length=N_EXPERTS)
    padded = (counts + MOE_BLOCK - 1) // MOE_BLOCK * MOE_BLOCK
    pad_end = jnp.cumsum(padded)
    pad_start = pad_end - padded
    start = jnp.cumsum(counts) - counts
    dest = pad_start[se] + jnp.arange(M) - start[se]
    n_blk = -(-M // MOE_BLOCK) + N_EXPERTS
    P = n_blk * MOE_BLOCK
    buf_tok = jnp.zeros((P,), jnp.int32).at[dest].set(stok)
    buf_gate = jnp.zeros((P,), fgate.dtype).at[dest].set(sgate)
    blk_exp = jnp.minimum(jnp.searchsorted(pad_end, jnp.arange(n_blk) * MOE_BLOCK, side='right'), N_EXPERTS - 1)
    xb = xt[buf_tok].reshape(n_blk, MOE_BLOCK, D)

    def run(args):
        xi, e = args
        a, b = jnp.split(xi @ w_gu[e], 2, axis=-1)
        return (jax.nn.silu(a) * b) @ w_down[e]

    yb = lax.map(run, (xb, blk_exp)).reshape(P, D)
    return jax.ops.segment_sum(yb * buf_gate[:, None].astype(yb.dtype), buf_tok, num_segments=T)


def moe_ffn(x, wg_r, bg_r, we_r, be_r, w_gu, w_down):
    B_, L, D = x.shape
    xt = x.reshape(B_ * L, D)
    T = xt.shape[0]
    lg = (xt @ wg_r + bg_r).astype(jnp.float32)
    pg = jax.nn.softmax(lg, axis=-1)
    gsel = jnp.argmax(lg, axis=-1)
    p_group = jnp.take_along_axis(pg, gsel[:, None], axis=-1)
    le = (xt @ we_r + be_r).astype(jnp.float32).reshape(T, N_GROUPS, EXP_PER_GROUP)
    le_g = jnp.take_along_axis(le, gsel[:, None, None], axis=1)[:, 0]
    tv, ti = lax.top_k(le_g, TOP_E)
    gate = p_group * jax.nn.softmax(tv, axis=-1)
    eidx = gsel[:, None] * EXP_PER_GROUP + ti
    return expert_dispatch(xt, eidx, gate, w_gu, w_down).reshape(B_, L, D)


def layer_forward(x, pos0, lw, pool_prev, rgc_prev, rgh0, sc_prev, nsa_fn):
    B_, L, _ = x.shape
    w_perm = permute_w_in(lw['w_in']).astype(MXU_DTYPE)
    proj2d = norm_matmul(x.reshape(B_ * L, D_MODEL), lw['norm_mix_g'], w_perm)
    if pool_prev is None:
        ym, tails, hlast = mixers_prompt(proj2d, lw, B_, L)
        pool_new = tails[:, 0, HALO - POOL_KEEP:]
        rgc_new = tails[:, 1, HALO - (RG_CONV - 1):]
        sc_new = tails[:, 2, HALO - (SC_CONV - 1):]
        rgh_new = hlast[:, 0]
    else:
        ym, pool_new, rgc_new, rgh_new, sc_new = mixers_sample(proj2d, lw, pos0, pool_prev, rgc_prev, rgh0, sc_prev)
    y_nsa, nsa_rows, win_new = nsa_fn(proj2d, lw['nsa_phi'], lw['nsa_phi_b'], lw['nsa_qk_g'])
    x = mix_out_moe(ym, y_nsa.reshape(B_ * L, GROUP_W), x.reshape(B_ * L, D_MODEL), lw,
                    MOE_TILE_PROMPT if L > 1 else MOE_TILE_SAMPLE)
    return x.reshape(B_, L, D_MODEL), (nsa_rows, win_new, pool_new, rgc_new, rgh_new, sc_new)


def kernel(x_prompt, x_sample, cache_nsa, state_win_kv, state_pool, state_rg_conv, state_rg_h, state_sc_conv,
           page_table, norm_mix_g, w_in, pool_w, pool_scale, rg_conv_w, rg_conv_b, rg_w_a, rg_b_a, rg_w_x, rg_b_x,
           rg_lambda, nsa_phi, nsa_phi_b, nsa_qk_g, sc_conv_w, sc_conv_b, mix_out_g, w_out, norm_ffn_g,
           router_group_w, router_group_b, router_expert_w, router_expert_b, exp_w_gu, exp_w_down):
    past_len = page_table.shape[1] * cache_nsa.shape[2]
    xp, xs = x_prompt, x_sample
    cache3 = feature_major_pages(cache_nsa)
    win3 = state_win_kv.transpose(0, 1, 3, 4, 5, 2).reshape(DEPTH * state_win_kv.shape[1], 2, N_KV * HEAD_DIM,
                                                             state_win_kv.shape[2])
    cache_ab = cache_compress(cache3, nsa_phi)
    Bp = xp.shape[0]
    st_p, st_s = [], []
    for l in range(DEPTH):
        lw = dict(norm_mix_g=norm_mix_g[l], w_in=w_in[l], pool_w=pool_w[l], pool_scale=pool_scale[l],
                  rg_conv_w=rg_conv_w[l], rg_conv_b=rg_conv_b[l], rg_w_a=rg_w_a[l], rg_b_a=rg_b_a[l],
                  rg_w_x=rg_w_x[l], rg_b_x=rg_b_x[l], rg_lambda=rg_lambda[l], nsa_phi=nsa_phi[l],
                  nsa_phi_b=nsa_phi_b[l], nsa_qk_g=nsa_qk_g[l], sc_conv_w=sc_conv_w[l], sc_conv_b=sc_conv_b[l],
                  mix_out_g=mix_out_g[l], w_out=w_out[l], norm_ffn_g=norm_ffn_g[l],
                  router_group_w=router_group_w[l], router_group_b=router_group_b[l],
                  router_expert_w=router_expert_w[l], router_expert_b=router_expert_b[l],
                  exp_w_gu=exp_w_gu[l], exp_w_down=exp_w_down[l])
        xp, sp = layer_forward(xp, 0, lw, None, None, None, None,
                               lambda p, phi, phi_b, g: nsa_prompt_pallas(p, Bp, xp.shape[1], phi, phi_b, g))
        xs, ss = layer_forward(xs, past_len, lw, state_pool[l], state_rg_conv[l], state_rg_h[l], state_sc_conv[l],
                               lambda p, phi, phi_b, g: nsa_sample_pallas(p, l, cache3, cache_ab, page_table, win3,
                                                                          phi_b, g))
        st_p.append(sp)
        st_s.append(ss)

    def stk(lst, i):
        return jnp.stack([s[i] for s in lst])

    return (xp, xs, stk(st_p, 0), stk(st_s, 0), stk(st_p, 1), stk(st_s, 1), stk(st_p, 2), stk(st_s, 2),
            stk(st_p, 3), stk(st_s, 3), stk(st_p, 4), stk(st_s, 4), stk(st_p, 5), stk(st_s, 5))
```

```python
import functools
import jax, jax.numpy as jnp
from jax import lax
import numpy as np
from jax.experimental import pallas as pl
from jax.experimental.pallas import tpu as pltpu

D_MODEL = 1024
BATCH = 4
SEQ = 4096
DEPTH = 2
DEC_BATCH = 128
DEC_SEQ = 1
PAST_LEN = 2048
PAGE_SIZE = 128

MIX_W = D_MODEL
GROUP_W = MIX_W // 4
POOL_W = GROUP_W
POOL_WINDOWS = (2, 4, 8, 16)
POOL_GROUP = POOL_W // len(POOL_WINDOWS)
POOL_KEEP = max(POOL_WINDOWS) - 1
RG_W = GROUP_W
RG_HEADS = 4
RG_BLOCK = RG_W // RG_HEADS
RG_CONV = 4
RG_C = 8.0
HEAD_DIM = 64
N_HEADS = GROUP_W // HEAD_DIM
N_KV = 2
GQA = N_HEADS // N_KV
CMP_BLOCK = 32
CMP_STRIDE = 16
SEL_BLOCK = 64
SEL_TOPK = 16
WINDOW = 512
Q_BLOCK = 128
SC_W = GROUP_W
SC_CONV = 3
N_GROUPS = 4
EXP_PER_GROUP = 8
N_EXPERTS = N_GROUPS * EXP_PER_GROUP
TOP_E = 2
D_EXPERT = 512
MOE_BLOCK = 128
EPS = 1e-6
SPLIT_SIZES = (POOL_W, RG_W, RG_W, N_HEADS * HEAD_DIM, 6 * N_KV * HEAD_DIM, 3 * N_HEADS, 3 * SC_W)
N_IN = sum(SPLIT_SIZES)

LANE = 128
ROW_TILE = 512
VMEM_LIMIT = 48 * 1024 * 1024
MXU_DTYPE = jnp.bfloat16
F32 = jnp.float32
NEG = -1e30

KV_W = 6 * N_KV * HEAD_DIM
COL_Q = 0
COL_KV = COL_Q + N_HEADS * HEAD_DIM
COL_POOL = COL_KV + KV_W
COL_RX = COL_POOL + POOL_W
COL_RGATE = COL_RX + RG_W
COL_SC = COL_RGATE + RG_W
COL_NG = COL_SC + 3 * SC_W
N_IN_PAD = COL_NG + LANE
SEL_TILE = 256
N_SEL_PROMPT = SEQ // SEL_BLOCK


def _cparams(n_axes=1):
    return pltpu.CompilerParams(dimension_semantics=("arbitrary",) * n_axes, vmem_limit_bytes=VMEM_LIMIT)


def _mm(a, b):
    return jnp.dot(a.astype(MXU_DTYPE), b.astype(MXU_DTYPE), preferred_element_type=F32)


def _mm_nt(a, b):
    return lax.dot_general(a.astype(MXU_DTYPE), b.astype(MXU_DTYPE), (((1,), (1,)), ((), ())),
                           preferred_element_type=F32)


def permute_w_in(w):
    pu, rx, rgate, q, kv, ng, sc = split_cols(w, SPLIT_SIZES)
    pad = jnp.zeros((w.shape[0], LANE - ng.shape[1]), w.dtype)
    return jnp.concatenate([q, kv, pu, rx, rgate, sc, ng, pad], axis=1)


def _norm_matmul_body(x_ref, g_ref, w_ref, o_ref):
    xf = x_ref[...]
    h = xf * lax.rsqrt(jnp.mean(xf * xf, axis=-1, keepdims=True) + EPS) * g_ref[...]
    o_ref[...] = _mm(h, w_ref[...])


def norm_matmul(x2d, g, w):
    T, D = x2d.shape
    N = w.shape[1]
    tm = min(ROW_TILE, T)
    return pl.pallas_call(
        _norm_matmul_body,
        grid=(T // tm,),
        in_specs=[pl.BlockSpec((tm, D), lambda i: (i, 0)),
                  pl.BlockSpec((1, D), lambda i: (0, 0)),
                  pl.BlockSpec((D, N), lambda i: (0, 0))],
        out_specs=pl.BlockSpec((tm, N), lambda i: (i, 0)),
        out_shape=jax.ShapeDtypeStruct((T, N), F32),
        compiler_params=_cparams(),
        name="norm_in_proj",
    )(x2d, g.reshape(1, D), w)


def _seg_rmsnorm(x, g):
    x2 = x * x
    left = lax.broadcasted_iota(jnp.int32, x.shape, 1) < HEAD_DIM
    s_l = jnp.sum(jnp.where(left, x2, 0.0), axis=-1, keepdims=True)
    s_r = jnp.sum(jnp.where(left, 0.0, x2), axis=-1, keepdims=True)
    ms = jnp.where(left, s_l, s_r) * (1.0 / HEAD_DIM)
    return x * lax.rsqrt(ms + EPS) * g


def _nsa_prep_body(qkv_ref, ng_ref, g_ref, perm_ref, qa_ref, kvb_ref, rawb_ref, rows_t_ref, win_t_ref, win_ref,
                   gates_ref):
    g = g_ref[...]
    for hb in range(N_KV):
        qn = _seg_rmsnorm(qkv_ref[:, COL_Q + hb * LANE:COL_Q + (hb + 1) * LANE], g[0:1]) * (HEAD_DIM ** -0.5)
        qa_ref[:, hb * 2 * LANE:(hb + 1) * 2 * LANE] = _mm(qn, perm_ref[hb]).astype(qa_ref.dtype)
    comp = [qkv_ref[:, COL_KV + c * LANE:COL_KV + (c + 1) * LANE] for c in range(6)]
    comp[2] = _seg_rmsnorm(comp[2], g[2:3])
    comp[4] = _seg_rmsnorm(comp[4], g[3:4])
    for c in range(6):
        kvb_ref[:, c * LANE:(c + 1) * LANE] = comp[c].astype(kvb_ref.dtype)
    for c in range(2):
        rawb_ref[:, c * LANE:(c + 1) * LANE] = comp[c].astype(rawb_ref.dtype)
    for c in range(4):
        rows_t_ref[0, c * LANE:(c + 1) * LANE, :] = comp[c].T
    for c in range(2):
        win_t_ref[0, c * LANE:(c + 1) * LANE, :] = comp[4 + c].T
        win_ref[:, c * LANE:(c + 1) * LANE] = comp[4 + c]
    gates_ref[...] = jax.nn.sigmoid(ng_ref[...])


def _q_place_matrices():
    p = np.zeros((N_KV, LANE, 2 * LANE), np.float32)
    for hb in range(N_KV):
        for gq in range(GQA):
            for d in range(HEAD_DIM):
                p[hb, gq * HEAD_DIM + d, gq * LANE + hb * HEAD_DIM + d] = 1.0
    return jnp.asarray(p, MXU_DTYPE)


def nsa_prep(proj, qk_g, B_, S):
    T = proj.shape[0]
    tm = min(ROW_TILE, S)
    tpb = S // tm
    qkv_w = COL_POOL
    g4 = jnp.tile(qk_g, (1, 2))
    return pl.pallas_call(
        _nsa_prep_body,
        grid=(T // tm,),
        in_specs=[pl.BlockSpec((tm, qkv_w), lambda i: (i, 0)),
                  pl.BlockSpec((tm, LANE), lambda i: (i, COL_NG // LANE)),
                  pl.BlockSpec((4, LANE), lambda i: (0, 0)),
                  pl.BlockSpec((N_KV, LANE, 2 * LANE), lambda i: (0, 0, 0))],
        out_specs=[pl.BlockSpec((tm, 4 * LANE), lambda i: (i, 0)),
                   pl.BlockSpec((tm, 6 * LANE), lambda i: (i, 0)),
                   pl.BlockSpec((tm, 2 * LANE), lambda i: (i, 0)),
                   pl.BlockSpec((1, 4 * LANE, tm), lambda i: (i // tpb, 0, i % tpb)),
                   pl.BlockSpec((1, 2 * LANE, tm), lambda i: (i // tpb, 0, i % tpb)),
                   pl.BlockSpec((tm, 2 * LANE), lambda i: (i, 0)),
                   pl.BlockSpec((tm, LANE), lambda i: (i, 0))],
        out_shape=[jax.ShapeDtypeStruct((T, 4 * LANE), MXU_DTYPE),
                   jax.ShapeDtypeStruct((T, 6 * LANE), MXU_DTYPE),
                   jax.ShapeDtypeStruct((T, 2 * LANE), MXU_DTYPE),
                   jax.ShapeDtypeStruct((B_, 4 * LANE, S), F32),
                   jax.ShapeDtypeStruct((B_, 2 * LANE, S), F32),
                   jax.ShapeDtypeStruct((T, 2 * LANE), F32),
                   jax.ShapeDtypeStruct((T, LANE), F32)],
        compiler_params=_cparams(),
        name="nsa_prep",
    )(proj, proj, g4, _q_place_matrices())


def compress_weights(phi):
    R = CMP_BLOCK // CMP_STRIDE
    wr = phi.reshape(2, R, CMP_STRIDE, HEAD_DIM, HEAD_DIM)
    eye = jnp.eye(2, dtype=phi.dtype)
    w = jnp.einsum('crjde,cx,hy->rjchdxye', wr, eye, eye)
    return w.reshape(R, CMP_STRIDE * 2 * LANE, 2 * LANE).astype(MXU_DTYPE)


def _compress_body(x_ref, w_ref, b_ref, g_ref, kc_ref, vc_ref):
    x = x_ref[0]
    nch = x.shape[0]
    a = _mm(x, w_ref[0])
    bm = _mm(x, w_ref[1])
    out = a + pltpu.roll(bm, nch - 1, 0) + b_ref[...]
    kc_ref[0] = _seg_rmsnorm(out[:, 0:LANE], g_ref[...]).astype(kc_ref.dtype)
    vc_ref[0] = out[:, LANE:2 * LANE].astype(vc_ref.dtype)


def nsa_compress_pallas(rawb3, wc, phi_b, g_kc):
    B_, nch, K = rawb3.shape
    bias = jnp.concatenate([jnp.tile(phi_b[0], 2), jnp.tile(phi_b[1], 2)]).reshape(1, 2 * LANE)
    return pl.pallas_call(
        _compress_body,
        grid=(B_,),
        in_specs=[pl.BlockSpec((1, nch, K), lambda b: (b, 0, 0)),
                  pl.BlockSpec(wc.shape, lambda b: (0, 0, 0)),
                  pl.BlockSpec((1, 2 * LANE), lambda b: (0, 0)),
                  pl.BlockSpec((1, LANE), lambda b: (0, 0))],
        out_specs=[pl.BlockSpec((1, nch, LANE), lambda b: (b, 0, 0)),
                   pl.BlockSpec((1, nch, LANE), lambda b: (b, 0, 0))],
        out_shape=[jax.ShapeDtypeStruct((B_, nch, LANE), MXU_DTYPE),
                   jax.ShapeDtypeStruct((B_, nch, LANE), MXU_DTYPE)],
        compiler_params=_cparams(),
        name="nsa_compress",
    )(rawb3, wc, bias, jnp.tile(g_kc, 2).reshape(1, LANE))


def _online_update(carry, s, v):
    m, l, acc = carry
    m_new = jnp.maximum(m, jnp.max(s, axis=-1, keepdims=True))
    alpha = jnp.exp(m - m_new)
    p = jnp.exp(s - m_new)
    l = alpha * l + jnp.sum(p, axis=-1, keepdims=True)
    acc = alpha * acc + _mm(p, v)
    return m_new, l, acc


def _select_blocks(imp, start):
    n_sel = N_SEL_PROMPT
    sc_t = imp.T[0:n_sel]
    blk = lax.broadcasted_iota(jnp.int32, sc_t.shape, 0)
    cur = (start + lax.broadcasted_iota(jnp.int32, sc_t.shape, 1)) // SEL_BLOCK
    valid = blk <= cur
    forced = (blk == 0) | (blk == cur) | (blk == cur - 1)
    score = jnp.where(valid, sc_t, -jnp.inf)
    score = jnp.where(forced & valid, jnp.inf, score)
    cnt = jnp.zeros(sc_t.shape, F32)
    for i in range(n_sel):
        ri = score[i:i + 1, :]
        beats = (ri > score) | ((ri == score) & (blk > i))
        cnt = cnt + jnp.where(beats, 1.0, 0.0)
    sel_t = jnp.where((cnt < SEL_TOPK) & (score > -jnp.inf), 1.0, 0.0)
    sel_t = jnp.concatenate([sel_t, jnp.zeros((LANE - n_sel, sc_t.shape[1]), F32)], axis=0)
    return sel_t.T


def _nsa_attn_body(qa_ref, gates_ref, kc_ref, vc_ref, kv_ref, ov_ref, e_ref, o_ref):
    i = pl.program_id(1)
    start = i * Q_BLOCK
    Q = Q_BLOCK
    R = GQA * Q
    t_row = start + lax.broadcasted_iota(jnp.int32, (R, 1), 0) % Q
    gates = gates_ref[...]
    lane_q = lax.broadcasted_iota(jnp.int32, (Q, LANE), 1)
    heads = range(N_KV)
    qs = [jnp.concatenate([qa_ref[:, (h * GQA + gq) * LANE:(h * GQA + gq + 1) * LANE] for gq in range(GQA)], axis=0)
          for h in heads]

    o_cmps, sel_bias = [], []
    kc = kc_ref[0]
    ncmp = kc.shape[0]
    cmp_end = lax.broadcasted_iota(jnp.int32, (R, ncmp), 1) * CMP_STRIDE + (CMP_BLOCK - 1)
    for h in heads:
        s = jnp.where(cmp_end <= t_row, _mm_nt(qs[h], kc), -jnp.inf)
        m = jnp.max(s, axis=-1, keepdims=True)
        e = jnp.exp(s - jnp.where(m > -jnp.inf, m, 0.0))
        d = jnp.sum(e, axis=-1, keepdims=True)
        p_cmp = e / jnp.where(d > 0, d, 1.0)
        o_cmps.append(_mm(p_cmp, vc_ref[0]))
        imp = _mm(p_cmp[0:Q], ov_ref[...]) + _mm(p_cmp[Q:R], ov_ref[...])
        sel = _select_blocks(imp, start)
        sel_bias.append(jnp.concatenate([jnp.where(sel > 0.5, 0.0, NEG)] * GQA, axis=0).astype(MXU_DTYPE))

    def sel_scores(j):
        off = pl.multiple_of(j * SEL_TILE, SEL_TILE)
        k = kv_ref[pl.ds(off, SEL_TILE), 2 * LANE:3 * LANE]
        v = kv_ref[pl.ds(off, SEL_TILE), 3 * LANE:4 * LANE]
        return off, v, [_mm_nt(qs[h], k) + _mm(sel_bias[h], e_ref[j]) for h in heads]

    def sel_step(j, carry):
        _, v, ss = sel_scores(j)
        return tuple(_online_update(carry[h], ss[h], v) for h in heads)

    init = (jnp.full((R, 1), NEG, F32), jnp.zeros((R, 1), F32), jnp.zeros((R, LANE), F32))
    n_tiles = (start + Q + SEL_TILE - 1) // SEL_TILE
    carry = lax.fori_loop(0, n_tiles - 1, sel_step, (init,) * N_KV)
    off, v, ss = sel_scores(n_tiles - 1)
    causal = off + lax.broadcasted_iota(jnp.int32, (R, SEL_TILE), 1) <= t_row
    o_sels = []
    for h in heads:
        _, l_s, acc_s = _online_update(carry[h], jnp.where(causal, ss[h], NEG), v)
        o_sels.append(acc_s / l_s)

    n_w = WINDOW // Q + 1
    offs = [pl.multiple_of(jnp.maximum(i - kk, 0) * Q, Q) for kk in range(n_w)]
    kw = jnp.concatenate([kv_ref[pl.ds(o, Q), 4 * LANE:5 * LANE] for o in offs], axis=0)
    vw = jnp.concatenate([kv_ref[pl.ds(o, Q), 5 * LANE:6 * LANE] for o in offs], axis=0)
    lane_w = lax.broadcasted_iota(jnp.int32, (1, n_w * Q), 1)
    w_pos = (i - lane_w // Q) * Q + lane_w % Q
    wd = t_row - w_pos
    wmask = (w_pos >= 0) & (wd >= 0) & (wd <= WINDOW)
    o_wins = []
    for h in heads:
        s = jnp.where(wmask, _mm_nt(qs[h], kw), NEG)
        p = jnp.exp(s - jnp.max(s, axis=-1, keepdims=True))
        o_wins.append(_mm(p, vw) / jnp.sum(p, axis=-1, keepdims=True))

    for h in heads:
        o_cmp, o_sel, o_win = o_cmps[h], o_sels[h], o_wins[h]
        outs = []
        for gq in range(GQA):
            c0 = (h * GQA + gq) * 3
            rs = slice(gq * Q, (gq + 1) * Q)
            og = (gates[:, c0:c0 + 1] * o_cmp[rs] + gates[:, c0 + 1:c0 + 2] * o_sel[rs]
                  + gates[:, c0 + 2:c0 + 3] * o_win[rs])
            outs.append(og if gq == h else pltpu.roll(og, HEAD_DIM, 1))
        o_ref[:, h * LANE:(h + 1) * LANE] = jnp.where(lane_q < HEAD_DIM, outs[0], outs[1])


def _sel_constants(S):
    ncmp_rows = S // CMP_STRIDE
    ci = np.arange(ncmp_rows)[:, None] * CMP_STRIDE
    sj = np.arange(LANE)[None, :] * SEL_BLOCK
    ov = ((ci < sj + SEL_BLOCK) & (ci + CMP_BLOCK > sj) & (np.arange(LANE)[None, :] < S // SEL_BLOCK))
    n_t = S // SEL_TILE
    key_blk = (np.arange(n_t)[:, None, None] * SEL_TILE + np.arange(SEL_TILE)[None, None, :]) // SEL_BLOCK
    e = (np.arange(LANE)[None, :, None] == key_blk)
    return jnp.asarray(ov, MXU_DTYPE), jnp.asarray(e, MXU_DTYPE)


def nsa_attn_prompt(qa, gates, kc, vc, kvb, B_, S):
    nq = S // Q_BLOCK
    nch = S // CMP_STRIDE
    ov, e3 = _sel_constants(S)
    return pl.pallas_call(
        _nsa_attn_body,
        grid=(B_, nq),
        in_specs=[pl.BlockSpec((Q_BLOCK, 4 * LANE), lambda b, i: (b * nq + i, 0)),
                  pl.BlockSpec((Q_BLOCK, LANE), lambda b, i: (b * nq + i, 0)),
                  pl.BlockSpec((1, nch, LANE), lambda b, i: (b, 0, 0)),
                  pl.BlockSpec((1, nch, LANE), lambda b, i: (b, 0, 0)),
                  pl.BlockSpec((S, 6 * LANE), lambda b, i: (b, 0)),
                  pl.BlockSpec(ov.shape, lambda b, i: (0, 0)),
                  pl.BlockSpec(e3.shape, lambda b, i: (0, 0, 0))],
        out_specs=pl.BlockSpec((Q_BLOCK, 2 * LANE), lambda b, i: (b * nq + i, 0)),
        out_shape=jax.ShapeDtypeStruct((B_ * S, N_HEADS * HEAD_DIM), F32),
        compiler_params=_cparams(2),
        name="nsa_attn_prompt",
    )(qa, gates, kc, vc, kvb, ov, e3)


def nsa_prompt_pallas(proj, B_, S, phi, phi_b, qk_g):
    qa, kvb, rawb, rows_t, win_t, _, gates = nsa_prep(proj, qk_g, B_, S)
    nch = S // CMP_STRIDE
    kc, vc = nsa_compress_pallas(rawb.reshape(B_, nch, CMP_STRIDE * 2 * LANE), compress_weights(phi), phi_b, qk_g[1])
    o = nsa_attn_prompt(qa, gates, kc, vc, kvb, B_, S)
    rows = rows_t.reshape(B_, 4, N_KV, HEAD_DIM, S).transpose(0, 4, 1, 2, 3)
    wk = min(WINDOW, S)
    win_new = win_t[:, :, S - wk:].reshape(B_, 2, N_KV, HEAD_DIM, wk).transpose(0, 4, 1, 2, 3)
    return o.reshape(B_, S, N_HEADS * HEAD_DIM), rows, win_new


N_PAGES = PAST_LEN // PAGE_SIZE
N_CHUNK_S = PAST_LEN // CMP_STRIDE
N_SEL_S = -(-(PAST_LEN + DEC_SEQ) // SEL_BLOCK)
CUR_S = PAST_LEN // SEL_BLOCK
QROWS = 8


def compress_weights_paged(phi):
    R = CMP_BLOCK // CMP_STRIDE
    wr = phi.reshape(2, R, CMP_STRIDE, HEAD_DIM, HEAD_DIM)
    w = jnp.einsum('crjde,hy->cjhdrye', wr, jnp.eye(2, dtype=phi.dtype))
    return w.reshape(2, CMP_STRIDE * LANE, R * LANE).astype(MXU_DTYPE)


def _softmax_with_extra(s, s_new):
    m = jnp.maximum(jnp.max(s, axis=-1, keepdims=True), s_new)
    e = jnp.exp(s - m)
    e_new = jnp.exp(s_new - m)
    return e, e_new, jnp.sum(e, axis=-1, keepdims=True) + e_new


CHUNKS_PER_PAGE = PAGE_SIZE // CMP_STRIDE
SWEEP_PAGES = 64


def feature_major_pages(cache_nsa):
    d, n = cache_nsa.shape[:2]
    return cache_nsa.transpose(0, 1, 3, 4, 5, 2).reshape(d * n, 4, N_KV * HEAD_DIM, PAGE_SIZE)


def _cache_compress_body(c_ref, w_ref, o_ref, sk, sv):
    n_pages = c_ref.shape[0]

    def to_row_major(p, carry):
        r0 = pl.multiple_of(p * PAGE_SIZE, PAGE_SIZE)
        sk[pl.ds(r0, PAGE_SIZE), :] = c_ref[p, 0].T
        sv[pl.ds(r0, PAGE_SIZE), :] = c_ref[p, 1].T
        return carry

    lax.fori_loop(0, n_pages, to_row_major, 0, unroll=4)
    n = n_pages * CHUNKS_PER_PAGE
    for c, src in enumerate((sk, sv)):
        x = jnp.concatenate([src[pl.ds(j, n, stride=CMP_STRIDE), :] for j in range(CMP_STRIDE)], axis=1)
        ab = _mm(x, w_ref[0, c])
        o_ref[:, c * LANE:(c + 1) * LANE] = ab[:, 0:LANE]
        o_ref[:, (2 + c) * LANE:(3 + c) * LANE] = ab[:, LANE:2 * LANE]


def cache_compress(cache_fm, nsa_phi):
    n_total = cache_fm.shape[0]
    assert (n_total // DEPTH) % SWEEP_PAGES == 0
    tiles = n_total // DEPTH // SWEEP_PAGES
    wc = jnp.stack([compress_weights_paged(nsa_phi[l]) for l in range(DEPTH)])
    rows = SWEEP_PAGES * PAGE_SIZE
    return pl.pallas_call(
        _cache_compress_body,
        grid=(DEPTH * tiles,),
        in_specs=[pl.BlockSpec((SWEEP_PAGES, 2, LANE, PAGE_SIZE), lambda i: (i, 0, 0, 0)),
                  pl.BlockSpec((1,) + wc.shape[1:], lambda i: (i // tiles, 0, 0, 0))],
        out_specs=pl.BlockSpec((SWEEP_PAGES * CHUNKS_PER_PAGE, 4 * LANE), lambda i: (i, 0)),
        out_shape=jax.ShapeDtypeStruct((n_total * CHUNKS_PER_PAGE, 4 * LANE), F32),
        scratch_shapes=[pltpu.VMEM((rows, LANE), F32), pltpu.VMEM((rows, LANE), F32)],
        compiler_params=_cparams(),
        name="cache_compress",
    )(cache_fm, wc)


def _nsa_sample_body(pt_ref, qa_ref, newb_ref, wnew_ref, gates_ref, *rest):
    pages = rest[:N_PAGES]
    abs_ = rest[N_PAGES:2 * N_PAGES]
    win_ref, bias_ref, gkc_ref, ov_ref, e_ref, y_ref, wout_ref = rest[2 * N_PAGES:]
    qs = qa_ref[0]
    newb = newb_ref[0].astype(F32)
    lane = lax.broadcasted_iota(jnp.int32, (QROWS, LANE), 1)
    row = lax.broadcasted_iota(jnp.int32, (QROWS, LANE), 0)

    ab = jnp.concatenate([a[...] for a in abs_], axis=0)
    out = ab[:, 0:2 * LANE] + pltpu.roll(ab[:, 2 * LANE:4 * LANE], N_CHUNK_S - 1, 0) + bias_ref[...]
    kc = _seg_rmsnorm(out[:, 0:LANE], gkc_ref[...])
    vc = out[:, LANE:2 * LANE]

    s = _mm_nt(qs, kc)
    s = jnp.where(lane < N_CHUNK_S - 1, s, -jnp.inf)
    e = jnp.exp(s - jnp.max(s, axis=-1, keepdims=True))
    p_cmp = e / jnp.sum(e, axis=-1, keepdims=True)
    o_cmp = _mm(p_cmp, vc)
    imp = _mm(p_cmp, ov_ref[...])
    imp = imp + jnp.where(row % GQA == 0, pltpu.roll(imp, QROWS - 1, 0), pltpu.roll(imp, 1, 0))

    valid = lane <= CUR_S
    forced = (lane == 0) | (lane == CUR_S) | (lane == CUR_S - 1)
    score = jnp.where(valid, imp, -jnp.inf)
    score = jnp.where(forced & valid, jnp.inf, score)
    cnt = jnp.zeros((QROWS, LANE), F32)
    for i in range(N_SEL_S):
        ci = score[:, i:i + 1]
        cnt = cnt + jnp.where((ci > score) | ((ci == score) & (lane > i)), 1.0, 0.0)
    sel = jnp.where((cnt < SEL_TOPK) & (score > -jnp.inf), 1.0, 0.0)

    msel = _mm(sel, e_ref[...])
    s = jnp.concatenate([_mm(qs, pg[0, 0]) for pg in pages], axis=1)
    s = jnp.where(msel > 0.5, s, NEG)
    qf = qs.astype(F32)
    s_new = jnp.sum(qf * newb[:, 2 * LANE:3 * LANE], axis=-1, keepdims=True)
    s_new = jnp.where(sel[:, CUR_S:CUR_S + 1] > 0.5, s_new, NEG)
    e, e_new, d = _softmax_with_extra(s, s_new)
    acc_o = e_new.astype(MXU_DTYPE).astype(F32) * newb[:, 3 * LANE:4 * LANE]
    for p, pg in enumerate(pages):
        acc_o = acc_o + _mm_nt(e[:, p * PAGE_SIZE:(p + 1) * PAGE_SIZE], pg[0, 1])
    o_sel = acc_o / d

    s = _mm(qs, win_ref[0, 0])
    s_new = jnp.sum(qf * newb[:, 4 * LANE:5 * LANE], axis=-1, keepdims=True)
    e, e_new, d = _softmax_with_extra(s, s_new)
    o_win = (_mm_nt(e, win_ref[0, 1]) + e_new.astype(MXU_DTYPE).astype(F32) * newb[:, 5 * LANE:6 * LANE]) / d

    g = gates_ref[0]
    o = g[:, 0:1] * o_cmp + g[:, 1:2] * o_sel + g[:, 2:3] * o_win
    o_sw = pltpu.roll(o, HEAD_DIM, 1)
    lane1 = lax.broadcasted_iota(jnp.int32, (1, LANE), 1)
    ys = []
    for h in range(N_KV):
        a = (o if h == 0 else o_sw)[GQA * h:GQA * h + 1]
        b = (o if h == 1 else o_sw)[GQA * h + 1:GQA * h + 2]
        ys.append(jnp.where(lane1 < HEAD_DIM, a, b))
    y_ref[0] = jnp.concatenate(ys, axis=1)

    lw = win_ref.shape[3]
    last = lax.broadcasted_iota(jnp.int32, (LANE, lw), 1) == lw - 1
    for c in range(2):
        col = jnp.broadcast_to(wnew_ref[0][:, c * LANE:(c + 1) * LANE], (QROWS, LANE)).T[:, 0:1]
        wout_ref[0, c] = jnp.where(last, col, pltpu.roll(win_ref[0, c], lw - 1, 1))


def _sample_constants():
    ci = np.arange(LANE)[:, None] * CMP_STRIDE
    sj = np.arange(LANE)[None, :] * SEL_BLOCK
    ov = ((ci < sj + SEL_BLOCK) & (ci + CMP_BLOCK > sj) & (np.arange(LANE)[:, None] < N_CHUNK_S - 1)
          & (np.arange(LANE)[None, :] < N_SEL_S))
    e = (np.arange(LANE)[:, None] == (np.arange(PAST_LEN)[None, :] // SEL_BLOCK))
    return jnp.asarray(ov, MXU_DTYPE), jnp.asarray(e, MXU_DTYPE)


def nsa_sample_pallas(proj, layer, cache_fm, cache_ab, page_table, win_fm, phi_b, qk_g):
    B_ = proj.shape[0]
    n_phys = cache_fm.shape[0] // DEPTH
    lw = win_fm.shape[3]
    assert page_table.shape == (B_, N_PAGES) and lw <= WINDOW and lw <= PAST_LEN and CUR_S == N_SEL_S - 1
    qa, kvb, _, rows_t, _, wnew, gates = nsa_prep(proj, qk_g, 1, B_)
    qa8 = jnp.pad(qa.astype(F32).reshape(B_, N_HEADS, LANE), ((0, 0), (0, QROWS - N_HEADS), (0, 0)))
    gates8 = jnp.pad(gates[:, :3 * N_HEADS].reshape(B_, N_HEADS, 3), ((0, 0), (0, QROWS - N_HEADS), (0, LANE - 3)))
    ov, e = _sample_constants()
    bias = jnp.concatenate([jnp.tile(phi_b[0], 2), jnp.tile(phi_b[1], 2)]).reshape(1, 2 * LANE)

    def page_spec(p):
        return pl.BlockSpec((1, 2, LANE, PAGE_SIZE), lambda b, pt: (layer * n_phys + pt[b, p], 1, 0, 0))

    def ab_spec(p):
        return pl.BlockSpec((CHUNKS_PER_PAGE, 4 * LANE), lambda b, pt: (layer * n_phys + pt[b, p], 0))

    def per_b(shape):
        return pl.BlockSpec((1,) + shape, lambda b, pt: (b, 0, 0))

    def const(a):
        return pl.BlockSpec(a.shape, lambda b, pt: (0,) * a.ndim)

    gkc = jnp.tile(qk_g[1], 2).reshape(1, LANE)
    y, wout = pl.pallas_call(
        _nsa_sample_body,
        grid_spec=pltpu.PrefetchScalarGridSpec(
            num_scalar_prefetch=1,
            grid=(B_,),
            in_specs=[per_b((QROWS, LANE)), per_b((1, 6 * LANE)), per_b((1, 2 * LANE)), per_b((QROWS, LANE))]
                     + [page_spec(p) for p in range(N_PAGES)] + [ab_spec(p) for p in range(N_PAGES)]
                     + [pl.BlockSpec((1, 2, LANE, lw), lambda b, pt: (layer * B_ + b, 0, 0, 0)),
                        const(bias), const(gkc), const(ov), const(e)],
            out_specs=[per_b((1, 2 * LANE)), pl.BlockSpec((1, 2, LANE, lw), lambda b, pt: (b, 0, 0, 0))]),
        out_shape=[jax.ShapeDtypeStruct((B_, 1, 2 * LANE), F32),
                   jax.ShapeDtypeStruct((B_, 2, LANE, lw), F32)],
        compiler_params=_cparams(),
        name="nsa_sample",
    )(page_table, qa8, kvb.reshape(B_, 1, 6 * LANE), wnew.reshape(B_, 1, 2 * LANE), gates8,
      *([cache_fm] * N_PAGES), *([cache_ab] * N_PAGES), win_fm, bias, gkc, ov, e)
    rows = rows_t.reshape(4, N_KV, HEAD_DIM, B_).transpose(3, 0, 1, 2)[:, None]
    return (y.reshape(B_, 1, N_HEADS * HEAD_DIM), rows,
            wout.reshape(B_, 2, N_KV, HEAD_DIM, lw).transpose(0, 4, 1, 2, 3))


MIX_CHUNK = 512
HALO = 16
YM_W = POOL_W + RG_W + SC_W


def _expm1(x):
    p = jnp.full_like(x, 1.0 / 3628800.0)
    for c in (1.0 / 362880.0, 1.0 / 40320.0, 1.0 / 5040.0, 1.0 / 720.0, 1.0 / 120.0, 1.0 / 24.0, 1.0 / 6.0, 0.5, 1.0):
        p = p * x + c
    return jnp.where(jnp.abs(x) < 0.25, p * x, jnp.exp(x) - 1.0)


def _softplus(x):
    return jnp.maximum(x, 0.0) + jnp.log1p(jnp.exp(-jnp.abs(x)))


def _gelu_tanh(x):
    return 0.5 * x * (1.0 + jnp.tanh(np.sqrt(2.0 / np.pi).astype(np.float32) * (x + 0.044715 * (x * x * x))))


def _rg_coeffs(xc, wa, ba, wx, bx, lam):
    r = jax.nn.sigmoid(_mm(xc, wa) + ba)
    ig = jax.nn.sigmoid(_mm(xc, wx) + bx)
    log_a = (-RG_C * r) * _softplus(-lam)
    return jnp.exp(log_a), jnp.sqrt(-_expm1(2.0 * log_a)) * (ig * xc)


def _pool_select(s2, s4, s8, s16):
    lane = lax.broadcasted_iota(jnp.int32, s2.shape, 1)
    return jnp.where(lane < POOL_GROUP, s2, jnp.where(lane < 2 * POOL_GROUP, s4,
                                                      jnp.where(lane < 3 * POOL_GROUP, s8, s16)))


def _pool_count(pos, shape):
    lane = lax.broadcasted_iota(jnp.int32, shape, 1)
    win = jnp.left_shift(2, lane // POOL_GROUP)
    return jnp.minimum(win, pos + 1).astype(F32)


def _mixers_prompt_body(pu_ref, rx_ref, rg_ref, z_ref, bg_ref, cg_ref, pw_ref, ps_ref, cw_ref, cb_ref, wa_ref, ba_ref,
                        wx_ref, bx_ref, lam_ref, scw_ref, scb_ref, ym_ref, tails_ref, hlast_ref, halo, hcar):
    c = pl.program_id(1)
    tc = pu_ref.shape[0]

    @pl.when(c == 0)
    def _():
        halo[...] = jnp.zeros_like(halo)
        hcar[...] = jnp.zeros_like(hcar)

    pu, rx = pu_ref[...], rx_ref[...]
    u = cg_ref[...] * z_ref[...]
    ext = [jnp.concatenate([halo[i], v], axis=0) for i, v in enumerate((pu, rx, u))]

    def back(e, k):
        return pltpu.roll(e, k, 0)

    s2 = ext[0] + back(ext[0], 1)
    s4 = s2 + back(s2, 2)
    s8 = s4 + back(s4, 4)
    s16 = s8 + back(s8, 8)
    tot = _pool_select(s2, s4, s8, s16)[HALO:]
    pos = c * tc + lax.broadcasted_iota(jnp.int32, (tc, POOL_W), 0)
    d = tot / _pool_count(pos, (tc, POOL_W)) - pu
    ym_ref[:, 0:POOL_W] = _mm(d, pw_ref[...]) * ps_ref[...]

    cw = cw_ref[...]
    xc = cb_ref[...] + cw[RG_CONV - 1:RG_CONV] * rx
    for k in range(1, RG_CONV):
        xc = xc + cw[RG_CONV - 1 - k:RG_CONV - k] * back(ext[1], k)[HALO:]
    a, b = _rg_coeffs(xc, wa_ref[...], ba_ref[...], wx_ref[...], bx_ref[...], lam_ref[...])
    row = lax.broadcasted_iota(jnp.int32, (tc, RG_W), 0)
    k = 1
    while k < tc:
        a_prev = jnp.where(row < k, 1.0, pltpu.roll(a, k, 0))
        b_prev = jnp.where(row < k, 0.0, pltpu.roll(b, k, 0))
        b = a * b_prev + b
        a = a * a_prev
        k *= 2
    h = a * hcar[0:1] + b
    hcar[...] = jnp.broadcast_to(h[tc - 1:tc], hcar.shape)
    hlast_ref[0] = jnp.broadcast_to(h[tc - 1:tc], hcar.shape)
    ym_ref[:, POOL_W:POOL_W + RG_W] = h * _gelu_tanh(rg_ref[...])

    scw = scw_ref[...]
    v = scb_ref[...] + scw[SC_CONV - 1:SC_CONV] * u
    for k in range(1, SC_CONV):
        v = v + scw[SC_CONV - 1 - k:SC_CONV - k] * back(ext[2], k)[HALO:]
    ym_ref[:, POOL_W + RG_W:YM_W] = bg_ref[...] * v

    for i, val in enumerate((pu, rx, u)):
        halo[i] = val[tc - HALO:]
        tails_ref[0, i] = val[tc - HALO:]


def _block_diag(w):
    g, n, _ = w.shape
    return jnp.einsum('gij,gh->gihj', w, jnp.eye(g, dtype=w.dtype)).reshape(g * n, g * n)


def _mixer_params(lw):
    row = lambda a: a.reshape(1, -1)
    return [_block_diag(lw['pool_w']).astype(MXU_DTYPE), row(lw['pool_scale']), lw['rg_conv_w'], row(lw['rg_conv_b']),
            _block_diag(lw['rg_w_a']).astype(MXU_DTYPE), row(lw['rg_b_a']),
            _block_diag(lw['rg_w_x']).astype(MXU_DTYPE), row(lw['rg_b_x']), row(lw['rg_lambda']),
            lw['sc_conv_w'], row(lw['sc_conv_b'])]


def _proj_col_specs(rows, index):
    cols = (COL_POOL, COL_RX, COL_RGATE, COL_SC, COL_SC + SC_W, COL_SC + 2 * SC_W)
    return [pl.BlockSpec((rows, GROUP_W), functools.partial(index, col // GROUP_W)) for col in cols]


def mixers_prompt(proj, lw, B_, S):
    tc = min(MIX_CHUNK, S)
    nc = S // tc
    params = _mixer_params(lw)
    fixed = lambda a: pl.BlockSpec(a.shape, lambda b, c: (0,) * a.ndim)
    return pl.pallas_call(
        _mixers_prompt_body,
        grid=(B_, nc),
        in_specs=_proj_col_specs(tc, lambda col, b, c: (b * nc + c, col)) + [fixed(a) for a in params],
        out_specs=[pl.BlockSpec((tc, YM_W), lambda b, c: (b * nc + c, 0)),
                   pl.BlockSpec((1, 3, HALO, GROUP_W), lambda b, c: (b, 0, 0, 0)),
                   pl.BlockSpec((1, 8, RG_W), lambda b, c: (b, 0, 0))],
        out_shape=[jax.ShapeDtypeStruct((B_ * S, YM_W), F32),
                   jax.ShapeDtypeStruct((B_, 3, HALO, GROUP_W), F32),
                   jax.ShapeDtypeStruct((B_, 8, RG_W), F32)],
        scratch_shapes=[pltpu.VMEM((3, HALO, GROUP_W), F32), pltpu.VMEM((8, RG_W), F32)],
        compiler_params=_cparams(2),
        name="mixers_prompt",
    )(*([proj] * 6), *params)


def _mixers_sample_body(pos0, pu_ref, rx_ref, rg_ref, z_ref, bg_ref, cg_ref, pp_ref, rp_ref, h0_ref, sp_ref, pw_ref,
                        ps_ref, cw_ref, cb_ref, wa_ref, ba_ref, wx_ref, bx_ref, lam_ref, scw_ref, scb_ref,
                        ym_ref, pn_ref, rn_ref, hn_ref, sn_ref):
    pu, rx = pu_ref[...], rx_ref[...]
    u = cg_ref[...] * z_ref[...]
    run, sums = pu, {}
    for k in range(1, POOL_KEEP + 1):
        run = run + pp_ref[POOL_KEEP - k]
        sums[k + 1] = run
    tot = _pool_select(*(sums[w] for w in POOL_WINDOWS))
    d = tot / _pool_count(pos0, pu.shape) - pu
    ym_ref[:, 0:POOL_W] = _mm(d, pw_ref[...]) * ps_ref[...]
    for k in range(POOL_KEEP - 1):
        pn_ref[k] = pp_ref[k + 1]
    pn_ref[POOL_KEEP - 1] = pu

    cw = cw_ref[...]
    xc = cb_ref[...] + cw[RG_CONV - 1:RG_CONV] * rx
    for k in range(RG_CONV - 1):
        xc = xc + cw[k:k + 1] * rp_ref[k]
    a, b = _rg_coeffs(xc, wa_ref[...], ba_ref[...], wx_ref[...], bx_ref[...], lam_ref[...])
    h = b + a * h0_ref[...]
    hn_ref[...] = h
    ym_ref[:, POOL_W:POOL_W + RG_W] = h * _gelu_tanh(rg_ref[...])
    for k in range(RG_CONV - 2):
        rn_ref[k] = rp_ref[k + 1]
    rn_ref[RG_CONV - 2] = rx

    scw = scw_ref[...]
    v = scb_ref[...] + scw[SC_CONV - 1:SC_CONV] * u
    for k in range(SC_CONV - 1):
        v = v + scw[k:k + 1] * sp_ref[k]
    ym_ref[:, POOL_W + RG_W:YM_W] = bg_ref[...] * v
    for k in range(SC_CONV - 2):
        sn_ref[k] = sp_ref[k + 1]
    sn_ref[SC_CONV - 2] = u


def mixers_sample(proj, lw, pos0, pool_prev, rgc_prev, h0, sc_prev):
    B_ = proj.shape[0]
    params = _mixer_params(lw)
    states = [pool_prev.transpose(1, 0, 2), rgc_prev.transpose(1, 0, 2), h0, sc_prev.transpose(1, 0, 2)]
    full = lambda a: pl.BlockSpec(a.shape, lambda i: (0,) * a.ndim)
    ym, pn, rn, hn, sn = pl.pallas_call(
        functools.partial(_mixers_sample_body, pos0),
        grid=(1,),
        in_specs=_proj_col_specs(B_, lambda col, i: (0, col)) + [full(a) for a in states] + [full(a) for a in params],
        out_specs=[pl.BlockSpec((B_, YM_W), lambda i: (0, 0))] + [full(a) for a in states],
        out_shape=[jax.ShapeDtypeStruct((B_, YM_W), F32)] + [jax.ShapeDtypeStruct(a.shape, F32) for a in states],
        compiler_params=_cparams(),
        name="mixers_sample",
    )(*([proj] * 6), *states, *params)
    return ym, pn.transpose(1, 0, 2), rn.transpose(1, 0, 2), hn, sn.transpose(1, 0, 2)


ROUTE_W = LANE
GROUP_LANE0 = N_EXPERTS
MOE_TILE_PROMPT = 256
MOE_TILE_SAMPLE = 32
COMBINE_TILE = 256


def _rms(x, g):
    return x * lax.rsqrt(jnp.mean(x * x, axis=-1, keepdims=True) + EPS) * g


def _mix_out_router_body(ym_ref, yn_ref, x_ref, og_ref, wo_ref, gf_ref, wr_ref, br_ref, tri_ref, x2_ref, xn_ref,
                         route_ref, cnt_ref, cnt_sc):
    og = og_ref[...]
    groups = (ym_ref[:, 0:POOL_W], ym_ref[:, POOL_W:POOL_W + RG_W], yn_ref[...], ym_ref[:, POOL_W + RG_W:YM_W])
    yn = jnp.concatenate([_rms(y, og[:, i * GROUP_W:(i + 1) * GROUP_W]) for i, y in enumerate(groups)], axis=1)
    x2 = x_ref[...] + _mm(yn, wo_ref[...])
    x2_ref[...] = x2
    xn = _rms(x2, gf_ref[...])
    xn_ref[...] = xn
    logits = _mm(xn, wr_ref[...]) + br_ref[...]
    lane = lax.broadcasted_iota(jnp.int32, logits.shape, 1)
    is_grp = (lane >= GROUP_LANE0) & (lane < GROUP_LANE0 + N_GROUPS)
    grp = jnp.where(is_grp, logits, -jnp.inf)
    gmax = jnp.max(grp, axis=-1, keepdims=True)
    gsel = jnp.min(jnp.where(grp == gmax, lane - GROUP_LANE0, N_GROUPS), axis=-1, keepdims=True)
    p_group = 1.0 / jnp.sum(jnp.where(is_grp, jnp.exp(logits - gmax), 0.0), axis=-1, keepdims=True)
    le = jnp.where((lane < N_EXPERTS) & (lane // EXP_PER_GROUP == gsel), logits, -jnp.inf)
    m1 = jnp.max(le, axis=-1, keepdims=True)
    i1 = jnp.min(jnp.where(le == m1, lane, LANE), axis=-1, keepdims=True)
    le2 = jnp.where(lane == i1, -jnp.inf, le)
    m2 = jnp.max(le2, axis=-1, keepdims=True)
    i2 = jnp.min(jnp.where(le2 == m2, lane, LANE), axis=-1, keepdims=True)
    e2 = jnp.exp(m2 - m1)
    g1 = p_group * (1.0 / (1.0 + e2))
    g2 = p_group * (e2 / (1.0 + e2))
    @pl.when(pl.program_id(0) == 0)
    def _():
        cnt_sc[...] = jnp.zeros_like(cnt_sc)

    oh = jnp.where((lane == i1) | (lane == i2), 1.0, 0.0)
    before = cnt_sc[0:1] + _mm(tri_ref[...], oh)
    r1 = jnp.sum(jnp.where(lane == i1, before, 0.0), axis=-1, keepdims=True)
    r2 = jnp.sum(jnp.where(lane == i2, before, 0.0), axis=-1, keepdims=True)
    total = cnt_sc[0:1] + jnp.sum(oh, axis=0, keepdims=True)
    cnt_sc[...] = jnp.broadcast_to(total, cnt_sc.shape)
    cnt_ref[...] = jnp.broadcast_to(total, cnt_ref.shape)
    vals = (i1.astype(F32), i2.astype(F32), g1, g2, r1, r2)
    route = jnp.zeros(logits.shape, F32)
    for k, v in enumerate(vals):
        route = jnp.where(lane == k, v, route)
    route_ref[...] = route


def mix_out_router(ym, y_nsa, x2d, lw):
    T, D = x2d.shape
    tm = min(256, T)
    wr = jnp.concatenate([lw['router_expert_w'], lw['router_group_w'],
                          jnp.zeros((D, ROUTE_W - N_EXPERTS - N_GROUPS), F32)], axis=1).astype(MXU_DTYPE)
    br = jnp.concatenate([lw['router_expert_b'], lw['router_group_b'],
                          jnp.zeros((ROUTE_W - N_EXPERTS - N_GROUPS,), F32)]).reshape(1, ROUTE_W)
    row = lambda i: (i, 0)
    fixed = lambda i: (0, 0)
    tri = jnp.asarray(np.tril(np.ones((tm, tm), np.float32), -1), MXU_DTYPE)
    return pl.pallas_call(
        _mix_out_router_body,
        grid=(T // tm,),
        in_specs=[pl.BlockSpec((tm, YM_W), row), pl.BlockSpec((tm, GROUP_W), row), pl.BlockSpec((tm, D), row),
                  pl.BlockSpec((1, MIX_W), fixed),
                  pl.BlockSpec((MIX_W, D), fixed), pl.BlockSpec((1, D), fixed), pl.BlockSpec((D, ROUTE_W), fixed),
                  pl.BlockSpec((1, ROUTE_W), fixed), pl.BlockSpec((tm, tm), fixed)],
        out_specs=[pl.BlockSpec((tm, D), row), pl.BlockSpec((tm, D), row), pl.BlockSpec((tm, ROUTE_W), row),
                   pl.BlockSpec((8, ROUTE_W), fixed)],
        out_shape=[jax.ShapeDtypeStruct((T, D), F32), jax.ShapeDtypeStruct((T, D), F32),
                   jax.ShapeDtypeStruct((T, ROUTE_W), F32), jax.ShapeDtypeStruct((8, ROUTE_W), F32)],
        scratch_shapes=[pltpu.VMEM((8, ROUTE_W), F32)],
        compiler_params=_cparams(),
        name="mix_out_router",
    )(ym, y_nsa, x2d, lw['mix_out_g'].reshape(1, MIX_W), lw['w_out'].astype(MXU_DTYPE),
      lw['norm_ffn_g'].reshape(1, D), wr, br, tri)


def _slot_scatter_body(n_slots, dest_ref, tok_ref):
    def clear(p, c):
        tok_ref[p] = 0
        return c

    lax.fori_loop(0, tok_ref.shape[0], clear, 0, unroll=8)

    def put(s, c):
        tok_ref[dest_ref[s]] = s // TOP_E
        return c

    lax.fori_loop(0, n_slots, put, 0, unroll=8)


def moe_schedule(route, counts, tile):
    T = route.shape[0]
    M = T * TOP_E
    fe = route[:, 0:TOP_E].astype(jnp.int32).reshape(M)
    rank = route[:, 4:4 + TOP_E].astype(jnp.int32).reshape(M)
    counts = counts.astype(jnp.int32)
    padded = (counts + tile - 1) // tile * tile
    pad_end = jnp.cumsum(padded)
    dest = ((pad_end - padded)[fe] + rank).astype(jnp.int32)
    n_blk = -(-M // tile) + N_EXPERTS
    buf_tok = pl.pallas_call(
        functools.partial(_slot_scatter_body, M),
        in_specs=[pl.BlockSpec(memory_space=pltpu.SMEM)],
        out_specs=pl.BlockSpec(memory_space=pltpu.SMEM),
        out_shape=jax.ShapeDtypeStruct((n_blk * tile,), jnp.int32),
        name="moe_slot_scatter",
    )(dest)
    blk_exp = jnp.minimum(jnp.sum(pad_end[None, :] <= (jnp.arange(n_blk, dtype=jnp.int32) * tile)[:, None], axis=1),
                          N_EXPERTS - 1).astype(jnp.int32)
    n_used = (pad_end[-1:] // tile).astype(jnp.int32)
    return buf_tok, blk_exp, n_used, dest


def _moe_ffn_body(tile, tok_ref, bexp_ref, nused_ref, x_hbm, wgu_ref, wdn_ref, y_ref, xg, sem, wgu_bf, wdn_bf):
    j = pl.program_id(0)
    n = nused_ref[0]

    def gather(blk, slot):
        def body(r, c):
            t = tok_ref[blk * tile + r]
            pltpu.make_async_copy(x_hbm.at[pl.ds(t, 1)], xg.at[slot, pl.ds(r, 1)], sem.at[slot]).start()
            return c
        lax.fori_loop(0, tile, body, 0, unroll=8)

    @pl.when((j == 0) & (n > 0))
    def _():
        gather(0, 0)

    @pl.when(j < n)
    def _():
        slot = j % 2

        @pl.when(j + 1 < n)
        def _():
            gather(j + 1, 1 - slot)

        @pl.when((j == 0) | (bexp_ref[j] != bexp_ref[jnp.maximum(j - 1, 0)]))
        def _():
            wgu_bf[...] = wgu_ref[0].astype(wgu_bf.dtype)
            wdn_bf[...] = wdn_ref[0].astype(wdn_bf.dtype)

        pltpu.make_async_copy(x_hbm.at[pl.ds(0, tile)], xg.at[slot], sem.at[slot]).wait()
        h = _mm(xg[slot], wgu_bf[...])
        a, b = h[:, :D_EXPERT], h[:, D_EXPERT:]
        y_ref[...] = _mm(a * jax.nn.sigmoid(a) * b, wdn_bf[...])

    @pl.when(j >= n)
    def _():
        y_ref[...] = jnp.zeros_like(y_ref)


def moe_ffn_pallas(xn, buf_tok, blk_exp, n_used, w_gu, w_down, tile):
    T, D = xn.shape
    n_blk = blk_exp.shape[0]
    return pl.pallas_call(
        functools.partial(_moe_ffn_body, tile),
        grid_spec=pltpu.PrefetchScalarGridSpec(
            num_scalar_prefetch=3,
            grid=(n_blk,),
            in_specs=[pl.BlockSpec(memory_space=pl.ANY),
                      pl.BlockSpec((1, D, 2 * D_EXPERT), lambda j, tok, bexp, nu: (bexp[j], 0, 0)),
                      pl.BlockSpec((1, D_EXPERT, D), lambda j, tok, bexp, nu: (bexp[j], 0, 0))],
            out_specs=pl.BlockSpec((tile, D), lambda j, tok, bexp, nu: (j, 0)),
            scratch_shapes=[pltpu.VMEM((2, tile, D), F32), pltpu.SemaphoreType.DMA((2,)),
                            pltpu.VMEM((D, 2 * D_EXPERT), MXU_DTYPE), pltpu.VMEM((D_EXPERT, D), MXU_DTYPE)]),
        out_shape=jax.ShapeDtypeStruct((n_blk * tile, D), F32),
        compiler_params=_cparams(),
        name="moe_ffn",
    )(buf_tok, blk_exp, n_used, xn, w_gu, w_down)


def _moe_combine_body(tm, slots_ref, y_hbm, x2_ref, route_ref, o_ref, yb, sem):
    i = pl.program_id(0)
    nt = pl.num_programs(0)

    def gather(tile_i, buf):
        def body(r, c):
            for k in range(TOP_E):
                s = slots_ref[(tile_i * tm + r) * TOP_E + k]
                pltpu.make_async_copy(y_hbm.at[pl.ds(s, 1)], yb.at[buf, k, pl.ds(r, 1)], sem.at[buf]).start()
            return c
        lax.fori_loop(0, tm, body, 0, unroll=8)

    @pl.when(i == 0)
    def _():
        gather(0, 0)

    buf = i % 2

    @pl.when(i + 1 < nt)
    def _():
        gather(i + 1, 1 - buf)

    for k in range(TOP_E):
        pltpu.make_async_copy(y_hbm.at[pl.ds(0, tm)], yb.at[buf, k], sem.at[buf]).wait()
    r = route_ref[...]
    o_ref[...] = x2_ref[...] + (r[:, 2:3] * yb[buf, 0] + r[:, 3:4] * yb[buf, 1])


def moe_combine_pallas(y, slots, x2, route):
    T, D = x2.shape
    tm = min(COMBINE_TILE, T)
    return pl.pallas_call(
        functools.partial(_moe_combine_body, tm),
        grid_spec=pltpu.PrefetchScalarGridSpec(
            num_scalar_prefetch=1,
            grid=(T // tm,),
            in_specs=[pl.BlockSpec(memory_space=pl.ANY),
                      pl.BlockSpec((tm, D), lambda i, s: (i, 0)),
                      pl.BlockSpec((tm, ROUTE_W), lambda i, s: (i, 0))],
            out_specs=pl.BlockSpec((tm, D), lambda i, s: (i, 0)),
            scratch_shapes=[pltpu.VMEM((2, TOP_E, tm, D), F32), pltpu.SemaphoreType.DMA((2,))]),
        out_shape=jax.ShapeDtypeStruct((T, D), F32),
        compiler_params=_cparams(),
        name="moe_combine",
    )(slots, y, x2, route)


def mix_out_moe(ym, y_nsa, x2d, lw, tile):
    x2, xn, route, counts = mix_out_router(ym, y_nsa, x2d, lw)
    buf_tok, blk_exp, n_used, slots = moe_schedule(route, counts[0, :N_EXPERTS], tile)
    y = moe_ffn_pallas(xn, buf_tok, blk_exp, n_used, lw['exp_w_gu'], lw['exp_w_down'], tile)
    return moe_combine_pallas(y, slots, x2, route)


def rmsnorm(x, g):
    xf = x.astype(jnp.float32)
    y = xf * lax.rsqrt(jnp.mean(xf * xf, axis=-1, keepdims=True) + EPS)
    return (y * g.astype(jnp.float32)).astype(x.dtype)


def split_cols(a, sizes):
    outs, o = [], 0
    for s in sizes:
        outs.append(a[..., o:o + s])
        o += s
    return outs


def causal_dwconv(u, prev, w, b):
    L = u.shape[1]
    ext = jnp.concatenate([prev.astype(u.dtype), u], axis=1)
    y = lax.conv_general_dilated(ext, w[:, None, :].astype(u.dtype), window_strides=(1,), padding='VALID',
                                 dimension_numbers=('NWC', 'WIO', 'NWC'), feature_group_count=u.shape[-1])
    return y + b.astype(u.dtype), ext[:, L:]


def pool_mixer(u, prev, pos0, w, scale):
    B_, L, C = u.shape
    ext = jnp.concatenate([prev.astype(u.dtype), u], axis=1)
    ef = ext.astype(jnp.float32)
    cs = jnp.concatenate([jnp.zeros((B_, 1, C), jnp.float32), jnp.cumsum(ef, axis=1)], axis=1)
    pos = pos0 + jnp.arange(L)
    means = []
    for g, win in enumerate(POOL_WINDOWS):
        sl = slice(g * POOL_GROUP, (g + 1) * POOL_GROUP)
        tot = cs[:, POOL_KEEP + 1:POOL_KEEP + 1 + L, sl] - cs[:, POOL_KEEP + 1 - win:POOL_KEEP + 1 - win + L, sl]
        cnt = jnp.minimum(win, pos + 1).astype(jnp.float32)
        means.append(tot / cnt[None, :, None])
    d = (jnp.concatenate(means, axis=-1) - ef[:, POOL_KEEP:]).astype(u.dtype)
    y = jnp.einsum('blgc,gcd->blgd', d.reshape(B_, L, len(POOL_WINDOWS), POOL_GROUP), w).reshape(B_, L, C)
    return y * scale, ext[:, L:]


def rglru_mixer(xb, gb, conv_prev, h0, conv_w, conv_b, w_a, b_a, w_x, b_x, lam):
    B_, L, C = xb.shape
    xc, conv_new = causal_dwconv(xb, conv_prev, conv_w, conv_b)
    xh = xc.reshape(B_, L, RG_HEADS, RG_BLOCK)
    r = jax.nn.sigmoid(jnp.einsum('blhi,hij->blhj', xh, w_a).reshape(B_, L, C) + b_a)
    ig = jax.nn.sigmoid(jnp.einsum('blhi,hij->blhj', xh, w_x).reshape(B_, L, C) + b_x)
    log_a = -RG_C * r.astype(jnp.float32) * jax.nn.softplus(-lam.astype(jnp.float32))
    a = jnp.exp(log_a)
    bt = jnp.sqrt(-jnp.expm1(2.0 * log_a)) * (ig * xc).astype(jnp.float32)
    bt = bt.at[:, 0].add(a[:, 0] * h0.astype(jnp.float32))
    _, h = lax.associative_scan(lambda e1, e2: (e1[0] * e2[0], e2[0] * e1[1] + e2[1]), (a, bt), axis=1)
    y = h.astype(xb.dtype) * jax.nn.gelu(gb)
    return y, conv_new, h[:, -1].astype(xb.dtype)


def masked_softmax(s, mask):
    s = jnp.where(mask, s.astype(jnp.float32), -jnp.inf)
    m = jnp.max(s, axis=-1, keepdims=True)
    e = jnp.exp(s - jnp.where(jnp.isfinite(m), m, 0.0))
    d = jnp.sum(e, axis=-1, keepdims=True)
    return e / jnp.where(d > 0, d, 1.0)


def nsa_compress(k_raw, v_raw, phi, phi_b, g_kc):
    B_, T = k_raw.shape[:2]
    R = CMP_BLOCK // CMP_STRIDE
    nch = T // CMP_STRIDE
    ncmp = nch - (R - 1)

    def comp(a, w, bias):
        ch = a[:, :nch * CMP_STRIDE].reshape(B_, nch, CMP_STRIDE, N_KV, HEAD_DIM)
        ch = ch.transpose(0, 1, 3, 2, 4).reshape(B_, nch, N_KV, CMP_STRIDE * HEAD_DIM)
        wr = w.reshape(R, CMP_STRIDE * HEAD_DIM, HEAD_DIM)
        out = jnp.einsum('bckf,fd->bckd', ch[:, 0:ncmp], wr[0])
        for r in range(1, R):
            out = out + jnp.einsum('bckf,fd->bckd', ch[:, r:r + ncmp], wr[r])
        return out + bias

    kc = rmsnorm(comp(k_raw, phi[0], phi_b[0]), g_kc)
    vc = comp(v_raw, phi[1], phi_b[1])
    cmp_end = jnp.arange(ncmp) * CMP_STRIDE + (CMP_BLOCK - 1)
    return kc, vc, cmp_end


def sel_blocks(a):
    B_, T = a.shape[:2]
    n_sel = -(-T // SEL_BLOCK)
    a = jnp.pad(a, ((0, 0), (0, n_sel * SEL_BLOCK - T), (0, 0), (0, 0)))
    return a.reshape(B_, n_sel, SEL_BLOCK, N_KV, HEAD_DIM).transpose(0, 3, 1, 2, 4)


def nsa_attend(q, q_pos, gates, kc, vc, cmp_end, ks_blk, vs_blk, kw, vw, w_pos):
    dt = q.dtype
    B_, Q = q.shape[:2]
    t = q_pos[:, None]
    s = jnp.einsum('bqkgd,bckd->bqkgc', q, kc)
    p_cmp = masked_softmax(s, (cmp_end[None, :] <= t)[None, :, None, None, :])
    o_cmp = jnp.einsum('bqkgc,bckd->bqkgd', p_cmp.astype(dt), vc)
    n_sel = ks_blk.shape[2]
    ci = jnp.arange(kc.shape[1])[:, None] * CMP_STRIDE
    sj = jnp.arange(n_sel)[None, :] * SEL_BLOCK
    overlap = ((ci < sj + SEL_BLOCK) & (ci + CMP_BLOCK > sj)).astype(jnp.float32)
    imp = jnp.einsum('bqkgc,cs->bqks', p_cmp, overlap)
    blk = jnp.arange(n_sel)[None, :]
    cur = t // SEL_BLOCK
    valid = blk <= cur
    forced = (blk == 0) | (blk == cur) | (blk == cur - 1)
    score = jnp.where(valid[None, :, None, :], imp, -jnp.inf)
    score = jnp.where((forced & valid)[None, :, None, :], jnp.inf, score)
    top_v, top_i = lax.top_k(score, min(SEL_TOPK, n_sel))
    kk = top_i.shape[-1]
    bi = jnp.arange(B_)[:, None, None, None]
    hi = jnp.arange(N_KV)[None, None, :, None]
    ks = ks_blk[bi, hi, top_i].reshape(B_, Q, N_KV, kk * SEL_BLOCK, HEAD_DIM)
    vs = vs_blk[bi, hi, top_i].reshape(B_, Q, N_KV, kk * SEL_BLOCK, HEAD_DIM)
    spos = (top_i[..., None] * SEL_BLOCK + jnp.arange(SEL_BLOCK)).reshape(B_, Q, N_KV, kk * SEL_BLOCK)
    smask = (spos <= q_pos[None, :, None, None]) & jnp.repeat(top_v > -jnp.inf, SEL_BLOCK, axis=-1)
    s = jnp.einsum('bqkgd,bqknd->bqkgn', q, ks)
    o_sel = jnp.einsum('bqkgn,bqknd->bqkgd', masked_softmax(s, smask[:, :, :, None, :]).astype(dt), vs)
    wd = t - w_pos[None, :]
    wmask = (w_pos[None, :] >= 0) & (wd >= 0) & (wd <= WINDOW)
    s = jnp.einsum('bqkgd,bnkd->bqkgn', q, kw)
    o_win = jnp.einsum('bqkgn,bnkd->bqkgd', masked_softmax(s, wmask[None, :, None, None, :]).astype(dt), vw)
    return gates[..., 0:1] * o_cmp + gates[..., 1:2] * o_sel + gates[..., 2:3] * o_win


def nsa_prompt(q, gates, kc_raw, vc_raw, ksel, vsel, kwin, vwin, phi, phi_b, g_kc):
    B_, S = q.shape[:2]
    kc, vc, cmp_end = nsa_compress(kc_raw, vc_raw, phi, phi_b, g_kc)
    ks_blk, vs_blk = sel_blocks(ksel), sel_blocks(vsel)
    zpad = jnp.zeros((B_, WINDOW, N_KV, HEAD_DIM), kwin.dtype)
    kw_pad = jnp.concatenate([zpad, kwin], axis=1)
    vw_pad = jnp.concatenate([zpad, vwin], axis=1)
    nq = S // Q_BLOCK

    def body(args):
        qc, gc, i = args
        start = i * Q_BLOCK
        kw = lax.dynamic_slice_in_dim(kw_pad, start, WINDOW + Q_BLOCK, axis=1)
        vw = lax.dynamic_slice_in_dim(vw_pad, start, WINDOW + Q_BLOCK, axis=1)
        return nsa_attend(qc, start + jnp.arange(Q_BLOCK), gc, kc, vc, cmp_end, ks_blk, vs_blk,
                          kw, vw, start - WINDOW + jnp.arange(WINDOW + Q_BLOCK))

    qb = q.reshape(B_, nq, Q_BLOCK, N_KV, GQA, HEAD_DIM).swapaxes(0, 1)
    gb = gates.reshape(B_, nq, Q_BLOCK, N_KV, GQA, 3).swapaxes(0, 1)
    o = lax.map(body, (qb, gb, jnp.arange(nq)))
    o = o.swapaxes(0, 1).reshape(B_, S, N_HEADS * HEAD_DIM)
    rows = jnp.stack([kc_raw, vc_raw, ksel, vsel], axis=2)
    win_new = jnp.stack([kwin, vwin], axis=2)[:, S - min(WINDOW, S):]
    return o, rows, win_new


def nsa_sample(pool, page_table, win_buf, q, gates, kc_raw, vc_raw, ksel, vsel, kwin, vwin, phi, phi_b, g_kc):
    B_, L = q.shape[:2]
    past = pool[page_table]
    past = past.reshape(B_, past.shape[1] * past.shape[2], 4, N_KV, HEAD_DIM)
    P = past.shape[1]
    rows = jnp.stack([kc_raw, vc_raw, ksel, vsel], axis=2)
    full = jnp.concatenate([past.astype(rows.dtype), rows], axis=1)
    kc, vc, cmp_end = nsa_compress(full[:, :, 0], full[:, :, 1], phi, phi_b, g_kc)
    ks_blk, vs_blk = sel_blocks(full[:, :, 2]), sel_blocks(full[:, :, 3])
    Lw = win_buf.shape[1]
    new_w = jnp.stack([kwin, vwin], axis=2)
    wfull = jnp.concatenate([win_buf.astype(new_w.dtype), new_w], axis=1)
    o = nsa_attend(q, P + jnp.arange(L), gates, kc, vc, cmp_end, ks_blk, vs_blk,
                   wfull[:, :, 0], wfull[:, :, 1], P - Lw + jnp.arange(Lw + L))
    return o.reshape(B_, L, N_HEADS * HEAD_DIM), rows, wfull[:, L:]


def expert_dispatch(xt, eidx, gate, w_gu, w_down):
    T, D = xt.shape
    M = T * TOP_E
    fe = eidx.reshape(M)
    ftok = jnp.arange(M, dtype=jnp.int32) // TOP_E
    fgate = gate.reshape(M)
    order = jnp.argsort(fe)
    se, stok, sgate = fe[order], ftok[order], fgate[order]
    counts = jnp.bincount(fe, length=N_EXPERTS)
    padded = (counts + MOE_BLOCK - 1) // MOE_BLOCK * MOE_BLOCK
    pad_end = jnp.cumsum(padded)
    pad_start = pad_end - padded
    start = jnp.cumsum(counts) - counts
    dest = pad_start[se] + jnp.arange(M) - start[se]
    n_blk = -(-M // MOE_BLOCK) + N_EXPERTS
    P = n_blk * MOE_BLOCK
    buf_tok = jnp.zeros((P,), jnp.int32).at[dest].set(stok)
    buf_gate = jnp.zeros((P,), fgate.dtype).at[dest].set(sgate)
    blk_exp = jnp.minimum(jnp.searchsorted(pad_end, jnp.arange(n_blk) * MOE_BLOCK, side='right'), N_EXPERTS - 1)
    xb = xt[buf_tok].reshape(n_blk, MOE_BLOCK, D)

    def run(args):
        xi, e = args
        a, b = jnp.split(xi @ w_gu[e], 2, axis=-1)
        return (jax.nn.silu(a) * b) @ w_down[e]

    yb = lax.map(run, (xb, blk_exp)).reshape(P, D)
    return jax.ops.segment_sum(yb * buf_gate[:, None].astype(yb.dtype), buf_tok, num_segments=T)


def moe_ffn(x, wg_r, bg_r, we_r, be_r, w_gu, w_down):
    B_, L, D = x.shape
    xt = x.reshape(B_ * L, D)
    T = xt.shape[0]
    lg = (xt @ wg_r + bg_r).astype(jnp.float32)
    pg = jax.nn.softmax(lg, axis=-1)
    gsel = jnp.argmax(lg, axis=-1)
    p_group = jnp.take_along_axis(pg, gsel[:, None], axis=-1)
    le = (xt @ we_r + be_r).astype(jnp.float32).reshape(T, N_GROUPS, EXP_PER_GROUP)
    le_g = jnp.take_along_axis(le, gsel[:, None, None], axis=1)[:, 0]
    tv, ti = lax.top_k(le_g, TOP_E)
    gate = p_group * jax.nn.softmax(tv, axis=-1)
    eidx = gsel[:, None] * EXP_PER_GROUP + ti
    return expert_dispatch(xt, eidx, gate, w_gu, w_down).reshape(B_, L, D)


def layer_forward(x, pos0, lw, pool_prev, rgc_prev, rgh0, sc_prev, nsa_fn):
    B_, L, _ = x.shape
    w_perm = permute_w_in(lw['w_in']).astype(MXU_DTYPE)
    proj2d = norm_matmul(x.reshape(B_ * L, D_MODEL), lw['norm_mix_g'], w_perm)
    if pool_prev is None:
        ym, tails, hlast = mixers_prompt(proj2d, lw, B_, L)
        pool_new = tails[:, 0, HALO - POOL_KEEP:]
        rgc_new = tails[:, 1, HALO - (RG_CONV - 1):]
        sc_new = tails[:, 2, HALO - (SC_CONV - 1):]
        rgh_new = hlast[:, 0]
    else:
        ym, pool_new, rgc_new, rgh_new, sc_new = mixers_sample(proj2d, lw, pos0, pool_prev, rgc_prev, rgh0, sc_prev)
    y_nsa, nsa_rows, win_new = nsa_fn(proj2d, lw['nsa_phi'], lw['nsa_phi_b'], lw['nsa_qk_g'])
    x = mix_out_moe(ym, y_nsa.reshape(B_ * L, GROUP_W), x.reshape(B_ * L, D_MODEL), lw,
                    MOE_TILE_PROMPT if L > 1 else MOE_TILE_SAMPLE)
    return x.reshape(B_, L, D_MODEL), (nsa_rows, win_new, pool_new, rgc_new, rgh_new, sc_new)


def kernel(x_prompt, x_sample, cache_nsa, state_win_kv, state_pool, state_rg_conv, state_rg_h, state_sc_conv,
           page_table, norm_mix_g, w_in, pool_w, pool_scale, rg_conv_w, rg_conv_b, rg_w_a, rg_b_a, rg_w_x, rg_b_x,
           rg_lambda, nsa_phi, nsa_phi_b, nsa_qk_g, sc_conv_w, sc_conv_b, mix_out_g, w_out, norm_ffn_g,
           router_group_w, router_group_b, router_expert_w, router_expert_b, exp_w_gu, exp_w_down):
    past_len = page_table.shape[1] * cache_nsa.shape[2]
    xp, xs = x_prompt, x_sample
    cache3 = feature_major_pages(cache_nsa)
    win3 = state_win_kv.transpose(0, 1, 3, 4, 5, 2).reshape(DEPTH * state_win_kv.shape[1], 2, N_KV * HEAD_DIM,
                                                             state_win_kv.shape[2])
    cache_ab = cache_compress(cache3, nsa_phi)
    Bp = xp.shape[0]
    st_p, st_s = [], []
    for l in range(DEPTH):
        lw = dict(norm_mix_g=norm_mix_g[l], w_in=w_in[l], pool_w=pool_w[l], pool_scale=pool_scale[l],
                  rg_conv_w=rg_conv_w[l], rg_conv_b=rg_conv_b[l], rg_w_a=rg_w_a[l], rg_b_a=rg_b_a[l],
                  rg_w_x=rg_w_x[l], rg_b_x=rg_b_x[l], rg_lambda=rg_lambda[l], nsa_phi=nsa_phi[l],
                  nsa_phi_b=nsa_phi_b[l], nsa_qk_g=nsa_qk_g[l], sc_conv_w=sc_conv_w[l], sc_conv_b=sc_conv_b[l],
                  mix_out_g=mix_out_g[l], w_out=w_out[l], norm_ffn_g=norm_ffn_g[l],
                  router_group_w=router_group_w[l], router_group_b=router_group_b[l],
                  router_expert_w=router_expert_w[l], router_expert_b=router_expert_b[l],
                  exp_w_gu=exp_w_gu[l], exp_w_down=exp_w_down[l])
        xp, sp = layer_forward(xp, 0, lw, None, None, None, None,
                               lambda p, phi, phi_b, g: nsa_prompt_pallas(p, Bp, xp.shape[1], phi, phi_b, g))
        xs, ss = layer_forward(xs, past_len, lw, state_pool[l], state_rg_conv[l], state_rg_h[l], state_sc_conv[l],
                               lambda p, phi, phi_b, g: nsa_sample_pallas(p, l, cache3, cache_ab, page_table, win3,
                                                                          phi_b, g))
        st_p.append(sp)
        st_s.append(ss)

    def stk(lst, i):
        return jnp.stack([s[i] for s in lst])

    return (xp, xs, stk(st_p, 0), stk(st_s, 0), stk(st_p, 1), stk(st_s, 1), stk(st_p, 2), stk(st_s, 2),
            stk(st_p, 3), stk(st_s, 3), stk(st_p, 4), stk(st_s, 4), stk(st_p, 5), stk(st_s, 5))
```

```python
import functools
import jax, jax.numpy as jnp
from jax import lax
import numpy as np
from jax.experimental import pallas as pl
from jax.experimental.pallas import tpu as pltpu

D_MODEL = 1024
BATCH = 4
SEQ = 4096
DEPTH = 2
DEC_BATCH = 128
DEC_SEQ = 1
PAST_LEN = 2048
PAGE_SIZE = 128

MIX_W = D_MODEL
GROUP_W = MIX_W // 4
POOL_W = GROUP_W
POOL_WINDOWS = (2, 4, 8, 16)
POOL_GROUP = POOL_W // len(POOL_WINDOWS)
POOL_KEEP = max(POOL_WINDOWS) - 1
RG_W = GROUP_W
RG_HEADS = 4
RG_BLOCK = RG_W // RG_HEADS
RG_CONV = 4
RG_C = 8.0
HEAD_DIM = 64
N_HEADS = GROUP_W // HEAD_DIM
N_KV = 2
GQA = N_HEADS // N_KV
CMP_BLOCK = 32
CMP_STRIDE = 16
SEL_BLOCK = 64
SEL_TOPK = 16
WINDOW = 512
Q_BLOCK = 128
SC_W = GROUP_W
SC_CONV = 3
N_GROUPS = 4
EXP_PER_GROUP = 8
N_EXPERTS = N_GROUPS * EXP_PER_GROUP
TOP_E = 2
D_EXPERT = 512
MOE_BLOCK = 128
EPS = 1e-6
SPLIT_SIZES = (POOL_W, RG_W, RG_W, N_HEADS * HEAD_DIM, 6 * N_KV * HEAD_DIM, 3 * N_HEADS, 3 * SC_W)
N_IN = sum(SPLIT_SIZES)

LANE = 128
ROW_TILE = 512
VMEM_LIMIT = 48 * 1024 * 1024
MXU_DTYPE = jnp.bfloat16
F32 = jnp.float32
NEG = -1e30

KV_W = 6 * N_KV * HEAD_DIM
COL_Q = 0
COL_KV = COL_Q + N_HEADS * HEAD_DIM
COL_POOL = COL_KV + KV_W
COL_RX = COL_POOL + POOL_W
COL_RGATE = COL_RX + RG_W
COL_SC = COL_RGATE + RG_W
COL_NG = COL_SC + 3 * SC_W
N_IN_PAD = COL_NG + LANE
SEL_TILE = 512
N_SEL_PROMPT = SEQ // SEL_BLOCK


def _cparams(n_axes=1):
    return pltpu.CompilerParams(dimension_semantics=("arbitrary",) * n_axes, vmem_limit_bytes=VMEM_LIMIT)


def _mm(a, b):
    return jnp.dot(a.astype(MXU_DTYPE), b.astype(MXU_DTYPE), preferred_element_type=F32)


def _mm_nt(a, b):
    return lax.dot_general(a.astype(MXU_DTYPE), b.astype(MXU_DTYPE), (((1,), (1,)), ((), ())),
                           preferred_element_type=F32)


def permute_w_in(w):
    pu, rx, rgate, q, kv, ng, sc = split_cols(w, SPLIT_SIZES)
    pad = jnp.zeros((w.shape[0], LANE - ng.shape[1]), w.dtype)
    return jnp.concatenate([q, kv, pu, rx, rgate, sc, ng, pad], axis=1)


def _norm_matmul_body(x_ref, g_ref, w_ref, o_ref):
    xf = x_ref[...]
    h = xf * lax.rsqrt(jnp.mean(xf * xf, axis=-1, keepdims=True) + EPS) * g_ref[...]
    o_ref[...] = _mm(h, w_ref[...])


def norm_matmul(x2d, g, w):
    T, D = x2d.shape
    N = w.shape[1]
    tm = min(ROW_TILE, T)
    return pl.pallas_call(
        _norm_matmul_body,
        grid=(T // tm,),
        in_specs=[pl.BlockSpec((tm, D), lambda i: (i, 0)),
                  pl.BlockSpec((1, D), lambda i: (0, 0)),
                  pl.BlockSpec((D, N), lambda i: (0, 0))],
        out_specs=pl.BlockSpec((tm, N), lambda i: (i, 0)),
        out_shape=jax.ShapeDtypeStruct((T, N), F32),
        compiler_params=_cparams(),
        name="norm_in_proj",
    )(x2d, g.reshape(1, D), w)


def _seg_rmsnorm(x, g):
    x2 = x * x
    left = lax.broadcasted_iota(jnp.int32, x.shape, 1) < HEAD_DIM
    s_l = jnp.sum(jnp.where(left, x2, 0.0), axis=-1, keepdims=True)
    s_r = jnp.sum(jnp.where(left, 0.0, x2), axis=-1, keepdims=True)
    ms = jnp.where(left, s_l, s_r) * (1.0 / HEAD_DIM)
    return x * lax.rsqrt(ms + EPS) * g


def _nsa_prep_body(qkv_ref, ng_ref, g_ref, perm_ref, qa_ref, kvb_ref, rawb_ref, rows_t_ref, win_t_ref, win_ref,
                   gates_ref):
    g = g_ref[...]
    for hb in range(N_KV):
        qn = _seg_rmsnorm(qkv_ref[:, COL_Q + hb * LANE:COL_Q + (hb + 1) * LANE], g[0:1]) * (HEAD_DIM ** -0.5)
        qa_ref[:, hb * 2 * LANE:(hb + 1) * 2 * LANE] = _mm(qn, perm_ref[hb]).astype(qa_ref.dtype)
    comp = [qkv_ref[:, COL_KV + c * LANE:COL_KV + (c + 1) * LANE] for c in range(6)]
    comp[2] = _seg_rmsnorm(comp[2], g[2:3])
    comp[4] = _seg_rmsnorm(comp[4], g[3:4])
    for c in range(6):
        kvb_ref[:, c * LANE:(c + 1) * LANE] = comp[c].astype(kvb_ref.dtype)
    for c in range(2):
        rawb_ref[:, c * LANE:(c + 1) * LANE] = comp[c].astype(rawb_ref.dtype)
    for c in range(4):
        rows_t_ref[0, c * LANE:(c + 1) * LANE, :] = comp[c].T
    for c in range(2):
        win_t_ref[0, c * LANE:(c + 1) * LANE, :] = comp[4 + c].T
        win_ref[:, c * LANE:(c + 1) * LANE] = comp[4 + c]
    gates_ref[...] = jax.nn.sigmoid(ng_ref[...])


def _q_place_matrices():
    p = np.zeros((N_KV, LANE, 2 * LANE), np.float32)
    for hb in range(N_KV):
        for gq in range(GQA):
            for d in range(HEAD_DIM):
                p[hb, gq * HEAD_DIM + d, gq * LANE + hb * HEAD_DIM + d] = 1.0
    return jnp.asarray(p, MXU_DTYPE)


def nsa_prep(proj, qk_g, B_, S):
    T = proj.shape[0]
    tm = min(ROW_TILE, S)
    tpb = S // tm
    qkv_w = COL_POOL
    g4 = jnp.tile(qk_g, (1, 2))
    return pl.pallas_call(
        _nsa_prep_body,
        grid=(T // tm,),
        in_specs=[pl.BlockSpec((tm, qkv_w), lambda i: (i, 0)),
                  pl.BlockSpec((tm, LANE), lambda i: (i, COL_NG // LANE)),
                  pl.BlockSpec((4, LANE), lambda i: (0, 0)),
                  pl.BlockSpec((N_KV, LANE, 2 * LANE), lambda i: (0, 0, 0))],
        out_specs=[pl.BlockSpec((tm, 4 * LANE), lambda i: (i, 0)),
                   pl.BlockSpec((tm, 6 * LANE), lambda i: (i, 0)),
                   pl.BlockSpec((tm, 2 * LANE), lambda i: (i, 0)),
                   pl.BlockSpec((1, 4 * LANE, tm), lambda i: (i // tpb, 0, i % tpb)),
                   pl.BlockSpec((1, 2 * LANE, tm), lambda i: (i // tpb, 0, i % tpb)),
                   pl.BlockSpec((tm, 2 * LANE), lambda i: (i, 0)),
                   pl.BlockSpec((tm, LANE), lambda i: (i, 0))],
        out_shape=[jax.ShapeDtypeStruct((T, 4 * LANE), MXU_DTYPE),
                   jax.ShapeDtypeStruct((T, 6 * LANE), MXU_DTYPE),
                   jax.ShapeDtypeStruct((T, 2 * LANE), MXU_DTYPE),
                   jax.ShapeDtypeStruct((B_, 4 * LANE, S), F32),
                   jax.ShapeDtypeStruct((B_, 2 * LANE, S), F32),
                   jax.ShapeDtypeStruct((T, 2 * LANE), F32),
                   jax.ShapeDtypeStruct((T, LANE), F32)],
        compiler_params=_cparams(),
        name="nsa_prep",
    )(proj, proj, g4, _q_place_matrices())


def compress_weights(phi):
    R = CMP_BLOCK // CMP_STRIDE
    wr = phi.reshape(2, R, CMP_STRIDE, HEAD_DIM, HEAD_DIM)
    eye = jnp.eye(2, dtype=phi.dtype)
    w = jnp.einsum('crjde,cx,hy->rjchdxye', wr, eye, eye)
    return w.reshape(R, CMP_STRIDE * 2 * LANE, 2 * LANE).astype(MXU_DTYPE)


def _compress_body(x_ref, w_ref, b_ref, g_ref, kc_ref, vc_ref):
    x = x_ref[0]
    nch = x.shape[0]
    a = _mm(x, w_ref[0])
    bm = _mm(x, w_ref[1])
    out = a + pltpu.roll(bm, nch - 1, 0) + b_ref[...]
    kc_ref[0] = _seg_rmsnorm(out[:, 0:LANE], g_ref[...]).astype(kc_ref.dtype)
    vc_ref[0] = out[:, LANE:2 * LANE].astype(vc_ref.dtype)


def nsa_compress_pallas(rawb3, wc, phi_b, g_kc):
    B_, nch, K = rawb3.shape
    bias = jnp.concatenate([jnp.tile(phi_b[0], 2), jnp.tile(phi_b[1], 2)]).reshape(1, 2 * LANE)
    return pl.pallas_call(
        _compress_body,
        grid=(B_,),
        in_specs=[pl.BlockSpec((1, nch, K), lambda b: (b, 0, 0)),
                  pl.BlockSpec(wc.shape, lambda b: (0, 0, 0)),
                  pl.BlockSpec((1, 2 * LANE), lambda b: (0, 0)),
                  pl.BlockSpec((1, LANE), lambda b: (0, 0))],
        out_specs=[pl.BlockSpec((1, nch, LANE), lambda b: (b, 0, 0)),
                   pl.BlockSpec((1, nch, LANE), lambda b: (b, 0, 0))],
        out_shape=[jax.ShapeDtypeStruct((B_, nch, LANE), MXU_DTYPE),
                   jax.ShapeDtypeStruct((B_, nch, LANE), MXU_DTYPE)],
        compiler_params=_cparams(),
        name="nsa_compress",
    )(rawb3, wc, bias, jnp.tile(g_kc, 2).reshape(1, LANE))


def _online_update(carry, s, v):
    m, l, acc = carry
    m_new = jnp.maximum(m, jnp.max(s, axis=-1, keepdims=True))
    alpha = jnp.exp(m - m_new)
    p = jnp.exp(s - m_new)
    l = alpha * l + jnp.sum(p, axis=-1, keepdims=True)
    acc = alpha * acc + _mm(p, v)
    return m_new, l, acc


def _select_blocks(imp, start):
    n_sel = N_SEL_PROMPT
    sc_t = imp.T[0:n_sel]
    blk = lax.broadcasted_iota(jnp.int32, sc_t.shape, 0)
    cur = (start + lax.broadcasted_iota(jnp.int32, sc_t.shape, 1)) // SEL_BLOCK
    valid = blk <= cur
    forced = (blk == 0) | (blk == cur) | (blk == cur - 1)
    score = jnp.where(valid, sc_t, -jnp.inf)
    score = jnp.where(forced & valid, jnp.inf, score)
    cnt = jnp.zeros(sc_t.shape, F32)
    for i in range(n_sel):
        ri = score[i:i + 1, :]
        beats = (ri > score) | ((ri == score) & (blk > i))
        cnt = cnt + jnp.where(beats, 1.0, 0.0)
    sel_t = jnp.where((cnt < SEL_TOPK) & (score > -jnp.inf), 1.0, 0.0)
    sel_t = jnp.concatenate([sel_t, jnp.zeros((LANE - n_sel, sc_t.shape[1]), F32)], axis=0)
    return sel_t.T


def _nsa_attn_body(qa_ref, gates_ref, kc_ref, vc_ref, kv_ref, ov_ref, e_ref, o_ref):
    i = pl.program_id(1)
    start = i * Q_BLOCK
    Q = Q_BLOCK
    R = GQA * Q
    t_row = start + lax.broadcasted_iota(jnp.int32, (R, 1), 0) % Q
    gates = gates_ref[...]
    lane_q = lax.broadcasted_iota(jnp.int32, (Q, LANE), 1)
    heads = range(N_KV)
    qs = [jnp.concatenate([qa_ref[:, (h * GQA + gq) * LANE:(h * GQA + gq + 1) * LANE] for gq in range(GQA)], axis=0)
          for h in heads]

    o_cmps, sel_bias = [], []
    kc = kc_ref[0]
    ncmp = kc.shape[0]
    cmp_end = lax.broadcasted_iota(jnp.int32, (R, ncmp), 1) * CMP_STRIDE + (CMP_BLOCK - 1)
    for h in heads:
        s = jnp.where(cmp_end <= t_row, _mm_nt(qs[h], kc), -jnp.inf)
        m = jnp.max(s, axis=-1, keepdims=True)
        e = jnp.exp(s - jnp.where(m > -jnp.inf, m, 0.0))
        d = jnp.sum(e, axis=-1, keepdims=True)
        p_cmp = e / jnp.where(d > 0, d, 1.0)
        o_cmps.append(_mm(p_cmp, vc_ref[0]))
        imp = _mm(p_cmp[0:Q], ov_ref[...]) + _mm(p_cmp[Q:R], ov_ref[...])
        sel = _select_blocks(imp, start)
        sel_bias.append(jnp.concatenate([jnp.where(sel > 0.5, 0.0, NEG)] * GQA, axis=0).astype(MXU_DTYPE))

    def sel_scores(j):
        off = pl.multiple_of(j * SEL_TILE, SEL_TILE)
        k = kv_ref[pl.ds(off, SEL_TILE), 2 * LANE:3 * LANE]
        v = kv_ref[pl.ds(off, SEL_TILE), 3 * LANE:4 * LANE]
        return off, v, [_mm_nt(qs[h], k) + _mm(sel_bias[h], e_ref[j]) for h in heads]

    def sel_step(j, carry):
        _, v, ss = sel_scores(j)
        return tuple(_online_update(carry[h], ss[h], v) for h in heads)

    init = (jnp.full((R, 1), NEG, F32), jnp.zeros((R, 1), F32), jnp.zeros((R, LANE), F32))
    n_tiles = (start + Q + SEL_TILE - 1) // SEL_TILE
    carry = lax.fori_loop(0, n_tiles - 1, sel_step, (init,) * N_KV)
    off, v, ss = sel_scores(n_tiles - 1)
    causal = off + lax.broadcasted_iota(jnp.int32, (R, SEL_TILE), 1) <= t_row
    o_sels = []
    for h in heads:
        _, l_s, acc_s = _online_update(carry[h], jnp.where(causal, ss[h], NEG), v)
        o_sels.append(acc_s / l_s)

    n_w = WINDOW // Q + 1
    offs = [pl.multiple_of(jnp.maximum(i - kk, 0) * Q, Q) for kk in range(n_w)]
    kw = jnp.concatenate([kv_ref[pl.ds(o, Q), 4 * LANE:5 * LANE] for o in offs], axis=0)
    vw = jnp.concatenate([kv_ref[pl.ds(o, Q), 5 * LANE:6 * LANE] for o in offs], axis=0)
    lane_w = lax.broadcasted_iota(jnp.int32, (1, n_w * Q), 1)
    w_pos = (i - lane_w // Q) * Q + lane_w % Q
    wd = t_row - w_pos
    wmask = (w_pos >= 0) & (wd >= 0) & (wd <= WINDOW)
    o_wins = []
    for h in heads:
        s = jnp.where(wmask, _mm_nt(qs[h], kw), NEG)
        p = jnp.exp(s - jnp.max(s, axis=-1, keepdims=True))
        o_wins.append(_mm(p, vw) / jnp.sum(p, axis=-1, keepdims=True))

    for h in heads:
        o_cmp, o_sel, o_win = o_cmps[h], o_sels[h], o_wins[h]
        outs = []
        for gq in range(GQA):
            c0 = (h * GQA + gq) * 3
            rs = slice(gq * Q, (gq + 1) * Q)
            og = (gates[:, c0:c0 + 1] * o_cmp[rs] + gates[:, c0 + 1:c0 + 2] * o_sel[rs]
                  + gates[:, c0 + 2:c0 + 3] * o_win[rs])
            outs.append(og if gq == h else pltpu.roll(og, HEAD_DIM, 1))
        o_ref[:, h * LANE:(h + 1) * LANE] = jnp.where(lane_q < HEAD_DIM, outs[0], outs[1])


def _sel_constants(S):
    ncmp_rows = S // CMP_STRIDE
    ci = np.arange(ncmp_rows)[:, None] * CMP_STRIDE
    sj = np.arange(LANE)[None, :] * SEL_BLOCK
    ov = ((ci < sj + SEL_BLOCK) & (ci + CMP_BLOCK > sj) & (np.arange(LANE)[None, :] < S // SEL_BLOCK))
    n_t = S // SEL_TILE
    key_blk = (np.arange(n_t)[:, None, None] * SEL_TILE + np.arange(SEL_TILE)[None, None, :]) // SEL_BLOCK
    e = (np.arange(LANE)[None, :, None] == key_blk)
    return jnp.asarray(ov, MXU_DTYPE), jnp.asarray(e, MXU_DTYPE)


def nsa_attn_prompt(qa, gates, kc, vc, kvb, B_, S):
    nq = S // Q_BLOCK
    nch = S // CMP_STRIDE
    ov, e3 = _sel_constants(S)
    return pl.pallas_call(
        _nsa_attn_body,
        grid=(B_, nq),
        in_specs=[pl.BlockSpec((Q_BLOCK, 4 * LANE), lambda b, i: (b * nq + i, 0)),
                  pl.BlockSpec((Q_BLOCK, LANE), lambda b, i: (b * nq + i, 0)),
                  pl.BlockSpec((1, nch, LANE), lambda b, i: (b, 0, 0)),
                  pl.BlockSpec((1, nch, LANE), lambda b, i: (b, 0, 0)),
                  pl.BlockSpec((S, 6 * LANE), lambda b, i: (b, 0)),
                  pl.BlockSpec(ov.shape, lambda b, i: (0, 0)),
                  pl.BlockSpec(e3.shape, lambda b, i: (0, 0, 0))],
        out_specs=pl.BlockSpec((Q_BLOCK, 2 * LANE), lambda b, i: (b * nq + i, 0)),
        out_shape=jax.ShapeDtypeStruct((B_ * S, N_HEADS * HEAD_DIM), F32),
        compiler_params=_cparams(2),
        name="nsa_attn_prompt",
    )(qa, gates, kc, vc, kvb, ov, e3)


def nsa_prompt_pallas(proj, B_, S, phi, phi_b, qk_g):
    qa, kvb, rawb, rows_t, win_t, _, gates = nsa_prep(proj, qk_g, B_, S)
    nch = S // CMP_STRIDE
    kc, vc = nsa_compress_pallas(rawb.reshape(B_, nch, CMP_STRIDE * 2 * LANE), compress_weights(phi), phi_b, qk_g[1])
    o = nsa_attn_prompt(qa, gates, kc, vc, kvb, B_, S)
    rows = rows_t.reshape(B_, 4, N_KV, HEAD_DIM, S).transpose(0, 4, 1, 2, 3)
    wk = min(WINDOW, S)
    win_new = win_t[:, :, S - wk:].reshape(B_, 2, N_KV, HEAD_DIM, wk).transpose(0, 4, 1, 2, 3)
    return o.reshape(B_, S, N_HEADS * HEAD_DIM), rows, win_new


N_PAGES = PAST_LEN // PAGE_SIZE
N_CHUNK_S = PAST_LEN // CMP_STRIDE
N_SEL_S = -(-(PAST_LEN + DEC_SEQ) // SEL_BLOCK)
CUR_S = PAST_LEN // SEL_BLOCK
QROWS = 8


def compress_weights_paged(phi):
    R = CMP_BLOCK // CMP_STRIDE
    wr = phi.reshape(2, R, CMP_STRIDE, HEAD_DIM, HEAD_DIM)
    w = jnp.einsum('crjde,hy->cjhdrye', wr, jnp.eye(2, dtype=phi.dtype))
    return w.reshape(2, CMP_STRIDE * LANE, R * LANE).astype(MXU_DTYPE)


def _softmax_with_extra(s, s_new):
    m = jnp.maximum(jnp.max(s, axis=-1, keepdims=True), s_new)
    e = jnp.exp(s - m)
    e_new = jnp.exp(s_new - m)
    return e, e_new, jnp.sum(e, axis=-1, keepdims=True) + e_new


CHUNKS_PER_PAGE = PAGE_SIZE // CMP_STRIDE
SWEEP_PAGES = 64


def feature_major_pages(cache_nsa):
    d, n = cache_nsa.shape[:2]
    return cache_nsa.transpose(0, 1, 3, 4, 5, 2).reshape(d * n, 4, N_KV * HEAD_DIM, PAGE_SIZE)


def _cache_compress_body(c_ref, w_ref, o_ref, sk, sv):
    n_pages = c_ref.shape[0]

    def to_row_major(p, carry):
        r0 = pl.multiple_of(p * PAGE_SIZE, PAGE_SIZE)
        sk[pl.ds(r0, PAGE_SIZE), :] = c_ref[p, 0].T
        sv[pl.ds(r0, PAGE_SIZE), :] = c_ref[p, 1].T
        return carry

    lax.fori_loop(0, n_pages, to_row_major, 0, unroll=4)
    n = n_pages * CHUNKS_PER_PAGE
    for c, src in enumerate((sk, sv)):
        x = jnp.concatenate([src[pl.ds(j, n, stride=CMP_STRIDE), :] for j in range(CMP_STRIDE)], axis=1)
        ab = _mm(x, w_ref[0, c])
        o_ref[:, c * LANE:(c + 1) * LANE] = ab[:, 0:LANE]
        o_ref[:, (2 + c) * LANE:(3 + c) * LANE] = ab[:, LANE:2 * LANE]


def cache_compress(cache_fm, nsa_phi):
    n_total = cache_fm.shape[0]
    assert (n_total // DEPTH) % SWEEP_PAGES == 0
    tiles = n_total // DEPTH // SWEEP_PAGES
    wc = jnp.stack([compress_weights_paged(nsa_phi[l]) for l in range(DEPTH)])
    rows = SWEEP_PAGES * PAGE_SIZE
    return pl.pallas_call(
        _cache_compress_body,
        grid=(DEPTH * tiles,),
        in_specs=[pl.BlockSpec((SWEEP_PAGES, 2, LANE, PAGE_SIZE), lambda i: (i, 0, 0, 0)),
                  pl.BlockSpec((1,) + wc.shape[1:], lambda i: (i // tiles, 0, 0, 0))],
        out_specs=pl.BlockSpec((SWEEP_PAGES * CHUNKS_PER_PAGE, 4 * LANE), lambda i: (i, 0)),
        out_shape=jax.ShapeDtypeStruct((n_total * CHUNKS_PER_PAGE, 4 * LANE), F32),
        scratch_shapes=[pltpu.VMEM((rows, LANE), F32), pltpu.VMEM((rows, LANE), F32)],
        compiler_params=_cparams(),
        name="cache_compress",
    )(cache_fm, wc)


SAMPLE_GROUP = 2


def _nsa_sample_body(pt_ref, qa_ref, newb_ref, wnew_ref, gates_ref, *rest):
    n_pg = SAMPLE_GROUP * N_PAGES
    pages, abs_ = rest[:n_pg], rest[n_pg:2 * n_pg]
    y_ref, wout_ref = rest[-2:]
    outs = [_nsa_sample_one(u, qa_ref, newb_ref, wnew_ref, gates_ref, pages[u * N_PAGES:(u + 1) * N_PAGES],
                            abs_[u * N_PAGES:(u + 1) * N_PAGES], *rest[2 * n_pg:-2]) for u in range(SAMPLE_GROUP)]
    y_ref[...] = jnp.stack([o[0] for o in outs])
    wout_ref[...] = jnp.stack([o[1] for o in outs])


def _nsa_sample_one(u, qa_ref, newb_ref, wnew_ref, gates_ref, pages, abs_, win_ref, bias_ref, gkc_ref, ov_ref, e_ref):
    qs = qa_ref[u]
    newb = newb_ref[u].astype(F32)
    lane = lax.broadcasted_iota(jnp.int32, (QROWS, LANE), 1)
    row = lax.broadcasted_iota(jnp.int32, (QROWS, LANE), 0)

    ab = jnp.concatenate([a[...] for a in abs_], axis=0)
    out = ab[:, 0:2 * LANE] + pltpu.roll(ab[:, 2 * LANE:4 * LANE], N_CHUNK_S - 1, 0) + bias_ref[...]
    kc = _seg_rmsnorm(out[:, 0:LANE], gkc_ref[...])
    vc = out[:, LANE:2 * LANE]

    s = _mm_nt(qs, kc)
    s = jnp.where(lane < N_CHUNK_S - 1, s, -jnp.inf)
    e = jnp.exp(s - jnp.max(s, axis=-1, keepdims=True))
    p_cmp = e / jnp.sum(e, axis=-1, keepdims=True)
    o_cmp = _mm(p_cmp, vc)
    imp = _mm(p_cmp, ov_ref[...])
    imp = imp + jnp.where(row % GQA == 0, pltpu.roll(imp, QROWS - 1, 0), pltpu.roll(imp, 1, 0))

    valid = lane <= CUR_S
    forced = (lane == 0) | (lane == CUR_S) | (lane == CUR_S - 1)
    score = jnp.where(valid, imp, -jnp.inf)
    score = jnp.where(forced & valid, jnp.inf, score)
    cnt = jnp.zeros((QROWS, LANE), F32)
    for i in range(N_SEL_S):
        ci = score[:, i:i + 1]
        cnt = cnt + jnp.where((ci > score) | ((ci == score) & (lane > i)), 1.0, 0.0)
    sel = jnp.where((cnt < SEL_TOPK) & (score > -jnp.inf), 1.0, 0.0)

    msel = _mm(sel, e_ref[...])
    s = jnp.concatenate([_mm(qs, pg[0, 0]) for pg in pages], axis=1)
    s = jnp.where(msel > 0.5, s, NEG)
    qf = qs.astype(F32)
    s_new = jnp.sum(qf * newb[:, 2 * LANE:3 * LANE], axis=-1, keepdims=True)
    s_new = jnp.where(sel[:, CUR_S:CUR_S + 1] > 0.5, s_new, NEG)
    e, e_new, d = _softmax_with_extra(s, s_new)
    acc_o = e_new.astype(MXU_DTYPE).astype(F32) * newb[:, 3 * LANE:4 * LANE]
    for p, pg in enumerate(pages):
        acc_o = acc_o + _mm_nt(e[:, p * PAGE_SIZE:(p + 1) * PAGE_SIZE], pg[0, 1])
    o_sel = acc_o / d

    s = _mm(qs, win_ref[u, 0])
    s_new = jnp.sum(qf * newb[:, 4 * LANE:5 * LANE], axis=-1, keepdims=True)
    e, e_new, d = _softmax_with_extra(s, s_new)
    o_win = (_mm_nt(e, win_ref[u, 1]) + e_new.astype(MXU_DTYPE).astype(F32) * newb[:, 5 * LANE:6 * LANE]) / d

    g = gates_ref[u]
    o = g[:, 0:1] * o_cmp + g[:, 1:2] * o_sel + g[:, 2:3] * o_win
    o_sw = pltpu.roll(o, HEAD_DIM, 1)
    lane1 = lax.broadcasted_iota(jnp.int32, (1, LANE), 1)
    ys = []
    for h in range(N_KV):
        a = (o if h == 0 else o_sw)[GQA * h:GQA * h + 1]
        b = (o if h == 1 else o_sw)[GQA * h + 1:GQA * h + 2]
        ys.append(jnp.where(lane1 < HEAD_DIM, a, b))
    lw = win_ref.shape[3]
    last = lax.broadcasted_iota(jnp.int32, (LANE, lw), 1) == lw - 1
    wouts = []
    for c in range(2):
        col = jnp.broadcast_to(wnew_ref[u][:, c * LANE:(c + 1) * LANE], (QROWS, LANE)).T[:, 0:1]
        wouts.append(jnp.where(last, col, pltpu.roll(win_ref[u, c], lw - 1, 1)))
    return jnp.concatenate(ys, axis=1), jnp.stack(wouts)


def _sample_constants():
    ci = np.arange(LANE)[:, None] * CMP_STRIDE
    sj = np.arange(LANE)[None, :] * SEL_BLOCK
    ov = ((ci < sj + SEL_BLOCK) & (ci + CMP_BLOCK > sj) & (np.arange(LANE)[:, None] < N_CHUNK_S - 1)
          & (np.arange(LANE)[None, :] < N_SEL_S))
    e = (np.arange(LANE)[:, None] == (np.arange(PAST_LEN)[None, :] // SEL_BLOCK))
    return jnp.asarray(ov, MXU_DTYPE), jnp.asarray(e, MXU_DTYPE)


def nsa_sample_pallas(proj, layer, cache_fm, cache_ab, page_table, win_fm, phi_b, qk_g):
    B_ = proj.shape[0]
    n_phys = cache_fm.shape[0] // DEPTH
    lw = win_fm.shape[3]
    assert page_table.shape == (B_, N_PAGES) and lw <= WINDOW and lw <= PAST_LEN and CUR_S == N_SEL_S - 1
    qa, kvb, _, rows_t, _, wnew, gates = nsa_prep(proj, qk_g, 1, B_)
    qa8 = jnp.pad(qa.astype(F32).reshape(B_, N_HEADS, LANE), ((0, 0), (0, QROWS - N_HEADS), (0, 0)))
    gates8 = jnp.pad(gates[:, :3 * N_HEADS].reshape(B_, N_HEADS, 3), ((0, 0), (0, QROWS - N_HEADS), (0, LANE - 3)))
    ov, e = _sample_constants()
    bias = jnp.concatenate([jnp.tile(phi_b[0], 2), jnp.tile(phi_b[1], 2)]).reshape(1, 2 * LANE)

    G = SAMPLE_GROUP
    assert B_ % G == 0
    seq_page = [(u, p) for u in range(G) for p in range(N_PAGES)]

    def page_spec(u, p):
        return pl.BlockSpec((1, 2, LANE, PAGE_SIZE), lambda b, pt: (layer * n_phys + pt[G * b + u, p], 1, 0, 0))

    def ab_spec(u, p):
        return pl.BlockSpec((CHUNKS_PER_PAGE, 4 * LANE), lambda b, pt: (layer * n_phys + pt[G * b + u, p], 0))

    def per_b(shape):
        return pl.BlockSpec((G,) + shape, lambda b, pt: (b, 0, 0))

    def const(a):
        return pl.BlockSpec(a.shape, lambda b, pt: (0,) * a.ndim)

    gkc = jnp.tile(qk_g[1], 2).reshape(1, LANE)
    y, wout = pl.pallas_call(
        _nsa_sample_body,
        grid_spec=pltpu.PrefetchScalarGridSpec(
            num_scalar_prefetch=1,
            grid=(B_ // G,),
            in_specs=[per_b((QROWS, LANE)), per_b((1, 6 * LANE)), per_b((1, 2 * LANE)), per_b((QROWS, LANE))]
                     + [page_spec(u, p) for u, p in seq_page] + [ab_spec(u, p) for u, p in seq_page]
                     + [pl.BlockSpec((G, 2, LANE, lw), lambda b, pt: (layer * (B_ // G) + b, 0, 0, 0)),
                        const(bias), const(gkc), const(ov), const(e)],
            out_specs=[per_b((1, 2 * LANE)), pl.BlockSpec((G, 2, LANE, lw), lambda b, pt: (b, 0, 0, 0))]),
        out_shape=[jax.ShapeDtypeStruct((B_, 1, 2 * LANE), F32),
                   jax.ShapeDtypeStruct((B_, 2, LANE, lw), F32)],
        compiler_params=_cparams(),
        name="nsa_sample",
    )(page_table, qa8, kvb.reshape(B_, 1, 6 * LANE), wnew.reshape(B_, 1, 2 * LANE), gates8,
      *([cache_fm] * (G * N_PAGES)), *([cache_ab] * (G * N_PAGES)), win_fm, bias, gkc, ov, e)
    rows = rows_t.reshape(4, N_KV, HEAD_DIM, B_).transpose(3, 0, 1, 2)[:, None]
    return (y.reshape(B_, 1, N_HEADS * HEAD_DIM), rows,
            wout.reshape(B_, 2, N_KV, HEAD_DIM, lw).transpose(0, 4, 1, 2, 3))


MIX_CHUNK = 512
HALO = 16
YM_W = POOL_W + RG_W + SC_W


def _expm1(x):
    p = jnp.full_like(x, 1.0 / 3628800.0)
    for c in (1.0 / 362880.0, 1.0 / 40320.0, 1.0 / 5040.0, 1.0 / 720.0, 1.0 / 120.0, 1.0 / 24.0, 1.0 / 6.0, 0.5, 1.0):
        p = p * x + c
    return jnp.where(jnp.abs(x) < 0.25, p * x, jnp.exp(x) - 1.0)


def _softplus(x):
    return jnp.maximum(x, 0.0) + jnp.log1p(jnp.exp(-jnp.abs(x)))


def _gelu_tanh(x):
    return 0.5 * x * (1.0 + jnp.tanh(np.sqrt(2.0 / np.pi).astype(np.float32) * (x + 0.044715 * (x * x * x))))


def _rg_coeffs(xc, wa, ba, wx, bx, lam):
    r = jax.nn.sigmoid(_mm(xc, wa) + ba)
    ig = jax.nn.sigmoid(_mm(xc, wx) + bx)
    log_a = (-RG_C * r) * _softplus(-lam)
    return jnp.exp(log_a), jnp.sqrt(-_expm1(2.0 * log_a)) * (ig * xc)


def _pool_select(s2, s4, s8, s16):
    lane = lax.broadcasted_iota(jnp.int32, s2.shape, 1)
    return jnp.where(lane < POOL_GROUP, s2, jnp.where(lane < 2 * POOL_GROUP, s4,
                                                      jnp.where(lane < 3 * POOL_GROUP, s8, s16)))


def _pool_count(pos, shape):
    lane = lax.broadcasted_iota(jnp.int32, shape, 1)
    win = jnp.left_shift(2, lane // POOL_GROUP)
    return jnp.minimum(win, pos + 1).astype(F32)


def _mixers_prompt_body(pu_ref, rx_ref, rg_ref, z_ref, bg_ref, cg_ref, pw_ref, ps_ref, cw_ref, cb_ref, wa_ref, ba_ref,
                        wx_ref, bx_ref, lam_ref, scw_ref, scb_ref, ym_ref, tails_ref, hlast_ref, halo, hcar):
    c = pl.program_id(1)
    tc = pu_ref.shape[0]

    @pl.when(c == 0)
    def _():
        halo[...] = jnp.zeros_like(halo)
        hcar[...] = jnp.zeros_like(hcar)

    pu, rx = pu_ref[...], rx_ref[...]
    u = cg_ref[...] * z_ref[...]
    ext = [jnp.concatenate([halo[i], v], axis=0) for i, v in enumerate((pu, rx, u))]

    def back(e, k):
        return pltpu.roll(e, k, 0)

    s2 = ext[0] + back(ext[0], 1)
    s4 = s2 + back(s2, 2)
    s8 = s4 + back(s4, 4)
    s16 = s8 + back(s8, 8)
    tot = _pool_select(s2, s4, s8, s16)[HALO:]
    pos = c * tc + lax.broadcasted_iota(jnp.int32, (tc, POOL_W), 0)
    d = tot / _pool_count(pos, (tc, POOL_W)) - pu
    ym_ref[:, 0:POOL_W] = _mm(d, pw_ref[...]) * ps_ref[...]

    cw = cw_ref[...]
    xc = cb_ref[...] + cw[RG_CONV - 1:RG_CONV] * rx
    for k in range(1, RG_CONV):
        xc = xc + cw[RG_CONV - 1 - k:RG_CONV - k] * back(ext[1], k)[HALO:]
    a, b = _rg_coeffs(xc, wa_ref[...], ba_ref[...], wx_ref[...], bx_ref[...], lam_ref[...])
    row = lax.broadcasted_iota(jnp.int32, (tc, RG_W), 0)
    k = 1
    while k < tc:
        a_prev = jnp.where(row < k, 1.0, pltpu.roll(a, k, 0))
        b_prev = jnp.where(row < k, 0.0, pltpu.roll(b, k, 0))
        b = a * b_prev + b
        a = a * a_prev
        k *= 2
    h = a * hcar[0:1] + b
    hcar[...] = jnp.broadcast_to(h[tc - 1:tc], hcar.shape)
    hlast_ref[0] = jnp.broadcast_to(h[tc - 1:tc], hcar.shape)
    ym_ref[:, POOL_W:POOL_W + RG_W] = h * _gelu_tanh(rg_ref[...])

    scw = scw_ref[...]
    v = scb_ref[...] + scw[SC_CONV - 1:SC_CONV] * u
    for k in range(1, SC_CONV):
        v = v + scw[SC_CONV - 1 - k:SC_CONV - k] * back(ext[2], k)[HALO:]
    ym_ref[:, POOL_W + RG_W:YM_W] = bg_ref[...] * v

    for i, val in enumerate((pu, rx, u)):
        halo[i] = val[tc - HALO:]
        tails_ref[0, i] = val[tc - HALO:]


def _block_diag(w):
    g, n, _ = w.shape
    return jnp.einsum('gij,gh->gihj', w, jnp.eye(g, dtype=w.dtype)).reshape(g * n, g * n)


def _mixer_params(lw):
    row = lambda a: a.reshape(1, -1)
    return [_block_diag(lw['pool_w']).astype(MXU_DTYPE), row(lw['pool_scale']), lw['rg_conv_w'], row(lw['rg_conv_b']),
            _block_diag(lw['rg_w_a']).astype(MXU_DTYPE), row(lw['rg_b_a']),
            _block_diag(lw['rg_w_x']).astype(MXU_DTYPE), row(lw['rg_b_x']), row(lw['rg_lambda']),
            lw['sc_conv_w'], row(lw['sc_conv_b'])]


def _proj_col_specs(rows, index):
    cols = (COL_POOL, COL_RX, COL_RGATE, COL_SC, COL_SC + SC_W, COL_SC + 2 * SC_W)
    return [pl.BlockSpec((rows, GROUP_W), functools.partial(index, col // GROUP_W)) for col in cols]


def mixers_prompt(proj, lw, B_, S):
    tc = min(MIX_CHUNK, S)
    nc = S // tc
    params = _mixer_params(lw)
    fixed = lambda a: pl.BlockSpec(a.shape, lambda b, c: (0,) * a.ndim)
    return pl.pallas_call(
        _mixers_prompt_body,
        grid=(B_, nc),
        in_specs=_proj_col_specs(tc, lambda col, b, c: (b * nc + c, col)) + [fixed(a) for a in params],
        out_specs=[pl.BlockSpec((tc, YM_W), lambda b, c: (b * nc + c, 0)),
                   pl.BlockSpec((1, 3, HALO, GROUP_W), lambda b, c: (b, 0, 0, 0)),
                   pl.BlockSpec((1, 8, RG_W), lambda b, c: (b, 0, 0))],
        out_shape=[jax.ShapeDtypeStruct((B_ * S, YM_W), F32),
                   jax.ShapeDtypeStruct((B_, 3, HALO, GROUP_W), F32),
                   jax.ShapeDtypeStruct((B_, 8, RG_W), F32)],
        scratch_shapes=[pltpu.VMEM((3, HALO, GROUP_W), F32), pltpu.VMEM((8, RG_W), F32)],
        compiler_params=_cparams(2),
        name="mixers_prompt",
    )(*([proj] * 6), *params)


def _mixers_sample_body(pos0, pu_ref, rx_ref, rg_ref, z_ref, bg_ref, cg_ref, pp_ref, rp_ref, h0_ref, sp_ref, pw_ref,
                        ps_ref, cw_ref, cb_ref, wa_ref, ba_ref, wx_ref, bx_ref, lam_ref, scw_ref, scb_ref,
                        ym_ref, pn_ref, rn_ref, hn_ref, sn_ref):
    pu, rx = pu_ref[...], rx_ref[...]
    u = cg_ref[...] * z_ref[...]
    run, sums = pu, {}
    for k in range(1, POOL_KEEP + 1):
        run = run + pp_ref[POOL_KEEP - k]
        sums[k + 1] = run
    tot = _pool_select(*(sums[w] for w in POOL_WINDOWS))
    d = tot / _pool_count(pos0, pu.shape) - pu
    ym_ref[:, 0:POOL_W] = _mm(d, pw_ref[...]) * ps_ref[...]
    for k in range(POOL_KEEP - 1):
        pn_ref[k] = pp_ref[k + 1]
    pn_ref[POOL_KEEP - 1] = pu

    cw = cw_ref[...]
    xc = cb_ref[...] + cw[RG_CONV - 1:RG_CONV] * rx
    for k in range(RG_CONV - 1):
        xc = xc + cw[k:k + 1] * rp_ref[k]
    a, b = _rg_coeffs(xc, wa_ref[...], ba_ref[...], wx_ref[...], bx_ref[...], lam_ref[...])
    h = b + a * h0_ref[...]
    hn_ref[...] = h
    ym_ref[:, POOL_W:POOL_W + RG_W] = h * _gelu_tanh(rg_ref[...])
    for k in range(RG_CONV - 2):
        rn_ref[k] = rp_ref[k + 1]
    rn_ref[RG_CONV - 2] = rx

    scw = scw_ref[...]
    v = scb_ref[...] + scw[SC_CONV - 1:SC_CONV] * u
    for k in range(SC_CONV - 1):
        v = v + scw[k:k + 1] * sp_ref[k]
    ym_ref[:, POOL_W + RG_W:YM_W] = bg_ref[...] * v
    for k in range(SC_CONV - 2):
        sn_ref[k] = sp_ref[k + 1]
    sn_ref[SC_CONV - 2] = u


def mixers_sample(proj, lw, pos0, pool_prev, rgc_prev, h0, sc_prev):
    B_ = proj.shape[0]
    params = _mixer_params(lw)
    states = [pool_prev.transpose(1, 0, 2), rgc_prev.transpose(1, 0, 2), h0, sc_prev.transpose(1, 0, 2)]
    full = lambda a: pl.BlockSpec(a.shape, lambda i: (0,) * a.ndim)
    ym, pn, rn, hn, sn = pl.pallas_call(
        functools.partial(_mixers_sample_body, pos0),
        grid=(1,),
        in_specs=_proj_col_specs(B_, lambda col, i: (0, col)) + [full(a) for a in states] + [full(a) for a in params],
        out_specs=[pl.BlockSpec((B_, YM_W), lambda i: (0, 0))] + [full(a) for a in states],
        out_shape=[jax.ShapeDtypeStruct((B_, YM_W), F32)] + [jax.ShapeDtypeStruct(a.shape, F32) for a in states],
        compiler_params=_cparams(),
        name="mixers_sample",
    )(*([proj] * 6), *states, *params)
    return ym, pn.transpose(1, 0, 2), rn.transpose(1, 0, 2), hn, sn.transpose(1, 0, 2)


ROUTE_W = LANE
GROUP_LANE0 = N_EXPERTS
MOE_TILE_PROMPT = 256
MOE_TILE_SAMPLE = 32
COMBINE_TILE = 256
FETCH_GROUPS = 8


def _rms(x, g):
    return x * lax.rsqrt(jnp.mean(x * x, axis=-1, keepdims=True) + EPS) * g


def _mix_out_router_body(ym_ref, yn_ref, x_ref, og_ref, wo_ref, gf_ref, wr_ref, br_ref, tri_ref, x2_ref, xn_ref,
                         route_ref, cnt_ref, cnt_sc):
    og = og_ref[...]
    groups = (ym_ref[:, 0:POOL_W], ym_ref[:, POOL_W:POOL_W + RG_W], yn_ref[...], ym_ref[:, POOL_W + RG_W:YM_W])
    yn = jnp.concatenate([_rms(y, og[:, i * GROUP_W:(i + 1) * GROUP_W]) for i, y in enumerate(groups)], axis=1)
    x2 = x_ref[...] + _mm(yn, wo_ref[...])
    x2_ref[...] = x2
    xn = _rms(x2, gf_ref[...])
    xn_ref[...] = xn
    logits = _mm(xn, wr_ref[...]) + br_ref[...]
    lane = lax.broadcasted_iota(jnp.int32, logits.shape, 1)
    is_grp = (lane >= GROUP_LANE0) & (lane < GROUP_LANE0 + N_GROUPS)
    grp = jnp.where(is_grp, logits, -jnp.inf)
    gmax = jnp.max(grp, axis=-1, keepdims=True)
    gsel = jnp.min(jnp.where(grp == gmax, lane - GROUP_LANE0, N_GROUPS), axis=-1, keepdims=True)
    p_group = 1.0 / jnp.sum(jnp.where(is_grp, jnp.exp(logits - gmax), 0.0), axis=-1, keepdims=True)
    le = jnp.where((lane < N_EXPERTS) & (lane // EXP_PER_GROUP == gsel), logits, -jnp.inf)
    m1 = jnp.max(le, axis=-1, keepdims=True)
    i1 = jnp.min(jnp.where(le == m1, lane, LANE), axis=-1, keepdims=True)
    le2 = jnp.where(lane == i1, -jnp.inf, le)
    m2 = jnp.max(le2, axis=-1, keepdims=True)
    i2 = jnp.min(jnp.where(le2 == m2, lane, LANE), axis=-1, keepdims=True)
    e2 = jnp.exp(m2 - m1)
    g1 = p_group * (1.0 / (1.0 + e2))
    g2 = p_group * (e2 / (1.0 + e2))
    @pl.when(pl.program_id(0) == 0)
    def _():
        cnt_sc[...] = jnp.zeros_like(cnt_sc)

    oh = jnp.where((lane == i1) | (lane == i2), 1.0, 0.0)
    before = cnt_sc[0:1] + _mm(tri_ref[...], oh)
    r1 = jnp.sum(jnp.where(lane == i1, before, 0.0), axis=-1, keepdims=True)
    r2 = jnp.sum(jnp.where(lane == i2, before, 0.0), axis=-1, keepdims=True)
    total = cnt_sc[0:1] + jnp.sum(oh, axis=0, keepdims=True)
    cnt_sc[...] = jnp.broadcast_to(total, cnt_sc.shape)
    cnt_ref[...] = jnp.broadcast_to(total, cnt_ref.shape)
    vals = (i1.astype(F32), i2.astype(F32), g1, g2, r1, r2)
    route = jnp.zeros(logits.shape, F32)
    for k, v in enumerate(vals):
        route = jnp.where(lane == k, v, route)
    route_ref[...] = route


def mix_out_router(ym, y_nsa, x2d, lw):
    T, D = x2d.shape
    tm = min(256, T)
    wr = jnp.concatenate([lw['router_expert_w'], lw['router_group_w'],
                          jnp.zeros((D, ROUTE_W - N_EXPERTS - N_GROUPS), F32)], axis=1).astype(MXU_DTYPE)
    br = jnp.concatenate([lw['router_expert_b'], lw['router_group_b'],
                          jnp.zeros((ROUTE_W - N_EXPERTS - N_GROUPS,), F32)]).reshape(1, ROUTE_W)
    row = lambda i: (i, 0)
    fixed = lambda i: (0, 0)
    tri = jnp.asarray(np.tril(np.ones((tm, tm), np.float32), -1), MXU_DTYPE)
    return pl.pallas_call(
        _mix_out_router_body,
        grid=(T // tm,),
        in_specs=[pl.BlockSpec((tm, YM_W), row), pl.BlockSpec((tm, GROUP_W), row), pl.BlockSpec((tm, D), row),
                  pl.BlockSpec((1, MIX_W), fixed),
                  pl.BlockSpec((MIX_W, D), fixed), pl.BlockSpec((1, D), fixed), pl.BlockSpec((D, ROUTE_W), fixed),
                  pl.BlockSpec((1, ROUTE_W), fixed), pl.BlockSpec((tm, tm), fixed)],
        out_specs=[pl.BlockSpec((tm, D), row), pl.BlockSpec((tm, D), row), pl.BlockSpec((tm, ROUTE_W), row),
                   pl.BlockSpec((8, ROUTE_W), fixed)],
        out_shape=[jax.ShapeDtypeStruct((T, D), F32), jax.ShapeDtypeStruct((T, D), F32),
                   jax.ShapeDtypeStruct((T, ROUTE_W), F32), jax.ShapeDtypeStruct((8, ROUTE_W), F32)],
        scratch_shapes=[pltpu.VMEM((8, ROUTE_W), F32)],
        compiler_params=_cparams(),
        name="mix_out_router",
    )(ym, y_nsa, x2d, lw['mix_out_g'].reshape(1, MIX_W), lw['w_out'].astype(MXU_DTYPE),
      lw['norm_ffn_g'].reshape(1, D), wr, br, tri)


def moe_schedule(route, counts, tile):
    T = route.shape[0]
    M = T * TOP_E
    fe = route[:, 0:TOP_E].astype(jnp.int32).reshape(M)
    rank = route[:, 4:4 + TOP_E].astype(jnp.int32).reshape(M)
    counts = counts.astype(jnp.int32)
    padded = (counts + tile - 1) // tile * tile
    pad_end = jnp.cumsum(padded)
    dest = ((pad_end - padded)[fe] + rank).astype(jnp.int32)
    n_blk = -(-M // tile) + N_EXPERTS
    tok = jnp.arange(M, dtype=jnp.int32) // TOP_E
    buf_tok = jnp.zeros((n_blk * tile,), jnp.int32).at[dest].set(tok)
    blk_exp = jnp.minimum(jnp.sum(pad_end[None, :] <= (jnp.arange(n_blk, dtype=jnp.int32) * tile)[:, None], axis=1),
                          N_EXPERTS - 1).astype(jnp.int32)
    n_used = (pad_end[-1:] // tile).astype(jnp.int32)
    return buf_tok, blk_exp, n_used, dest


def _moe_ffn_body(tile, tok_ref, bexp_ref, nused_ref, x_hbm, wgu_ref, wdn_ref, y_ref, xg, sem, wgu_bf, wdn_bf):
    j = pl.program_id(0)
    n = nused_ref[0]

    def gather(blk, slot):
        def body(r, c):
            t = tok_ref[blk * tile + r]
            pltpu.make_async_copy(x_hbm.at[pl.ds(t, 1)], xg.at[slot, pl.ds(r, 1)], sem.at[slot]).start()
            return c
        lax.fori_loop(0, tile, body, 0, unroll=8)

    @pl.when((j == 0) & (n > 0))
    def _():
        gather(0, 0)

    def wait_block(slot):
        pltpu.make_async_copy(x_hbm.at[pl.ds(0, tile)], xg.at[slot], sem.at[slot]).wait()

    @pl.when(j < n)
    def _():
        slot = j % 2
        nxt = jnp.minimum(j + 1, n - 1)

        def fetch_group(g):
            per = tile // FETCH_GROUPS
            for r in range(g * per, (g + 1) * per):
                t = tok_ref[nxt * tile + r]
                pltpu.make_async_copy(x_hbm.at[pl.ds(t, 1)], xg.at[1 - slot, pl.ds(r, 1)], sem.at[1 - slot]).start()

        @pl.when((j == 0) | (bexp_ref[j] != bexp_ref[jnp.maximum(j - 1, 0)]))
        def _():
            wgu_bf[...] = wgu_ref[0].astype(wgu_bf.dtype)
            wdn_bf[...] = wdn_ref[0].astype(wdn_bf.dtype)

        wait_block(slot)
        x = xg[slot]
        half = FETCH_GROUPS // 2
        cg, cd = 2 * D_EXPERT // half, y_ref.shape[1] // half
        hs = []
        for c in range(half):
            fetch_group(c)
            hs.append(_mm(x, wgu_bf[:, c * cg:(c + 1) * cg]))
        h = jnp.concatenate(hs, axis=1)
        a, b = h[:, :D_EXPERT], h[:, D_EXPERT:]
        act = a * jax.nn.sigmoid(a) * b
        for c in range(half):
            fetch_group(half + c)
            y_ref[:, c * cd:(c + 1) * cd] = _mm(act, wdn_bf[:, c * cd:(c + 1) * cd])

        @pl.when(j + 1 >= n)
        def _():
            wait_block(1 - slot)

    @pl.when(j >= n)
    def _():
        y_ref[...] = jnp.zeros_like(y_ref)


def moe_ffn_pallas(xn, buf_tok, blk_exp, n_used, w_gu, w_down, tile):
    T, D = xn.shape
    n_blk = blk_exp.shape[0]
    return pl.pallas_call(
        functools.partial(_moe_ffn_body, tile),
        grid_spec=pltpu.PrefetchScalarGridSpec(
            num_scalar_prefetch=3,
            grid=(n_blk,),
            in_specs=[pl.BlockSpec(memory_space=pl.ANY),
                      pl.BlockSpec((1, D, 2 * D_EXPERT), lambda j, tok, bexp, nu: (bexp[j], 0, 0)),
                      pl.BlockSpec((1, D_EXPERT, D), lambda j, tok, bexp, nu: (bexp[j], 0, 0))],
            out_specs=pl.BlockSpec((tile, D), lambda j, tok, bexp, nu: (j, 0)),
            scratch_shapes=[pltpu.VMEM((2, tile, D), F32), pltpu.SemaphoreType.DMA((2,)),
                            pltpu.VMEM((D, 2 * D_EXPERT), MXU_DTYPE), pltpu.VMEM((D_EXPERT, D), MXU_DTYPE)]),
        out_shape=jax.ShapeDtypeStruct((n_blk * tile, D), F32),
        compiler_params=_cparams(),
        name="moe_ffn",
    )(buf_tok, blk_exp, n_used, xn, w_gu, w_down)


def _moe_combine_body(tm, slots_ref, y_hbm, x2_ref, route_ref, o_ref, yb, sem):
    i = pl.program_id(0)
    nt = pl.num_programs(0)

    def gather(tile_i, buf):
        def body(r, c):
            for k in range(TOP_E):
                s = slots_ref[(tile_i * tm + r) * TOP_E + k]
                pltpu.make_async_copy(y_hbm.at[pl.ds(s, 1)], yb.at[buf, k, pl.ds(r, 1)], sem.at[buf]).start()
            return c
        lax.fori_loop(0, tm, body, 0, unroll=8)

    @pl.when(i == 0)
    def _():
        gather(0, 0)

    buf = i % 2

    @pl.when(i + 1 < nt)
    def _():
        gather(i + 1, 1 - buf)

    for k in range(TOP_E):
        pltpu.make_async_copy(y_hbm.at[pl.ds(0, tm)], yb.at[buf, k], sem.at[buf]).wait()
    r = route_ref[...]
    o_ref[...] = x2_ref[...] + (r[:, 2:3] * yb[buf, 0] + r[:, 3:4] * yb[buf, 1])


def moe_combine_pallas(y, slots, x2, route):
    T, D = x2.shape
    tm = min(COMBINE_TILE, T)
    return pl.pallas_call(
        functools.partial(_moe_combine_body, tm),
        grid_spec=pltpu.PrefetchScalarGridSpec(
            num_scalar_prefetch=1,
            grid=(T // tm,),
            in_specs=[pl.BlockSpec(memory_space=pl.ANY),
                      pl.BlockSpec((tm, D), lambda i, s: (i, 0)),
                      pl.BlockSpec((tm, ROUTE_W), lambda i, s: (i, 0))],
            out_specs=pl.BlockSpec((tm, D), lambda i, s: (i, 0)),
            scratch_shapes=[pltpu.VMEM((2, TOP_E, tm, D), F32), pltpu.SemaphoreType.DMA((2,))]),
        out_shape=jax.ShapeDtypeStruct((T, D), F32),
        compiler_params=_cparams(),
        name="moe_combine",
    )(slots, y, x2, route)


def mix_out_moe(ym, y_nsa, x2d, lw, tile):
    x2, xn, route, counts = mix_out_router(ym, y_nsa, x2d, lw)
    buf_tok, blk_exp, n_used, slots = moe_schedule(route, counts[0, :N_EXPERTS], tile)
    y = moe_ffn_pallas(xn, buf_tok, blk_exp + lw['expert_base'], n_used, lw['exp_w_gu'], lw['exp_w_down'], tile)
    return moe_combine_pallas(y, slots, x2, route)


def rmsnorm(x, g):
    xf = x.astype(jnp.float32)
    y = xf * lax.rsqrt(jnp.mean(xf * xf, axis=-1, keepdims=True) + EPS)
    return (y * g.astype(jnp.float32)).astype(x.dtype)


def split_cols(a, sizes):
    outs, o = [], 0
    for s in sizes:
        outs.append(a[..., o:o + s])
        o += s
    return outs


def causal_dwconv(u, prev, w, b):
    L = u.shape[1]
    ext = jnp.concatenate([prev.astype(u.dtype), u], axis=1)
    y = lax.conv_general_dilated(ext, w[:, None, :].astype(u.dtype), window_strides=(1,), padding='VALID',
                                 dimension_numbers=('NWC', 'WIO', 'NWC'), feature_group_count=u.shape[-1])
    return y + b.astype(u.dtype), ext[:, L:]


def pool_mixer(u, prev, pos0, w, scale):
    B_, L, C = u.shape
    ext = jnp.concatenate([prev.astype(u.dtype), u], axis=1)
    ef = ext.astype(jnp.float32)
    cs = jnp.concatenate([jnp.zeros((B_, 1, C), jnp.float32), jnp.cumsum(ef, axis=1)], axis=1)
    pos = pos0 + jnp.arange(L)
    means = []
    for g, win in enumerate(POOL_WINDOWS):
        sl = slice(g * POOL_GROUP, (g + 1) * POOL_GROUP)
        tot = cs[:, POOL_KEEP + 1:POOL_KEEP + 1 + L, sl] - cs[:, POOL_KEEP + 1 - win:POOL_KEEP + 1 - win + L, sl]
        cnt = jnp.minimum(win, pos + 1).astype(jnp.float32)
        means.append(tot / cnt[None, :, None])
    d = (jnp.concatenate(means, axis=-1) - ef[:, POOL_KEEP:]).astype(u.dtype)
    y = jnp.einsum('blgc,gcd->blgd', d.reshape(B_, L, len(POOL_WINDOWS), POOL_GROUP), w).reshape(B_, L, C)
    return y * scale, ext[:, L:]


def rglru_mixer(xb, gb, conv_prev, h0, conv_w, conv_b, w_a, b_a, w_x, b_x, lam):
    B_, L, C = xb.shape
    xc, conv_new = causal_dwconv(xb, conv_prev, conv_w, conv_b)
    xh = xc.reshape(B_, L, RG_HEADS, RG_BLOCK)
    r = jax.nn.sigmoid(jnp.einsum('blhi,hij->blhj', xh, w_a).reshape(B_, L, C) + b_a)
    ig = jax.nn.sigmoid(jnp.einsum('blhi,hij->blhj', xh, w_x).reshape(B_, L, C) + b_x)
    log_a = -RG_C * r.astype(jnp.float32) * jax.nn.softplus(-lam.astype(jnp.float32))
    a = jnp.exp(log_a)
    bt = jnp.sqrt(-jnp.expm1(2.0 * log_a)) * (ig * xc).astype(jnp.float32)
    bt = bt.at[:, 0].add(a[:, 0] * h0.astype(jnp.float32))
    _, h = lax.associative_scan(lambda e1, e2: (e1[0] * e2[0], e2[0] * e1[1] + e2[1]), (a, bt), axis=1)
    y = h.astype(xb.dtype) * jax.nn.gelu(gb)
    return y, conv_new, h[:, -1].astype(xb.dtype)


def masked_softmax(s, mask):
    s = jnp.where(mask, s.astype(jnp.float32), -jnp.inf)
    m = jnp.max(s, axis=-1, keepdims=True)
    e = jnp.exp(s - jnp.where(jnp.isfinite(m), m, 0.0))
    d = jnp.sum(e, axis=-1, keepdims=True)
    return e / jnp.where(d > 0, d, 1.0)


def nsa_compress(k_raw, v_raw, phi, phi_b, g_kc):
    B_, T = k_raw.shape[:2]
    R = CMP_BLOCK // CMP_STRIDE
    nch = T // CMP_STRIDE
    ncmp = nch - (R - 1)

    def comp(a, w, bias):
        ch = a[:, :nch * CMP_STRIDE].reshape(B_, nch, CMP_STRIDE, N_KV, HEAD_DIM)
        ch = ch.transpose(0, 1, 3, 2, 4).reshape(B_, nch, N_KV, CMP_STRIDE * HEAD_DIM)
        wr = w.reshape(R, CMP_STRIDE * HEAD_DIM, HEAD_DIM)
        out = jnp.einsum('bckf,fd->bckd', ch[:, 0:ncmp], wr[0])
        for r in range(1, R):
            out = out + jnp.einsum('bckf,fd->bckd', ch[:, r:r + ncmp], wr[r])
        return out + bias

    kc = rmsnorm(comp(k_raw, phi[0], phi_b[0]), g_kc)
    vc = comp(v_raw, phi[1], phi_b[1])
    cmp_end = jnp.arange(ncmp) * CMP_STRIDE + (CMP_BLOCK - 1)
    return kc, vc, cmp_end


def sel_blocks(a):
    B_, T = a.shape[:2]
    n_sel = -(-T // SEL_BLOCK)
    a = jnp.pad(a, ((0, 0), (0, n_sel * SEL_BLOCK - T), (0, 0), (0, 0)))
    return a.reshape(B_, n_sel, SEL_BLOCK, N_KV, HEAD_DIM).transpose(0, 3, 1, 2, 4)


def nsa_attend(q, q_pos, gates, kc, vc, cmp_end, ks_blk, vs_blk, kw, vw, w_pos):
    dt = q.dtype
    B_, Q = q.shape[:2]
    t = q_pos[:, None]
    s = jnp.einsum('bqkgd,bckd->bqkgc', q, kc)
    p_cmp = masked_softmax(s, (cmp_end[None, :] <= t)[None, :, None, None, :])
    o_cmp = jnp.einsum('bqkgc,bckd->bqkgd', p_cmp.astype(dt), vc)
    n_sel = ks_blk.shape[2]
    ci = jnp.arange(kc.shape[1])[:, None] * CMP_STRIDE
    sj = jnp.arange(n_sel)[None, :] * SEL_BLOCK
    overlap = ((ci < sj + SEL_BLOCK) & (ci + CMP_BLOCK > sj)).astype(jnp.float32)
    imp = jnp.einsum('bqkgc,cs->bqks', p_cmp, overlap)
    blk = jnp.arange(n_sel)[None, :]
    cur = t // SEL_BLOCK
    valid = blk <= cur
    forced = (blk == 0) | (blk == cur) | (blk == cur - 1)
    score = jnp.where(valid[None, :, None, :], imp, -jnp.inf)
    score = jnp.where((forced & valid)[None, :, None, :], jnp.inf, score)
    top_v, top_i = lax.top_k(score, min(SEL_TOPK, n_sel))
    kk = top_i.shape[-1]
    bi = jnp.arange(B_)[:, None, None, None]
    hi = jnp.arange(N_KV)[None, None, :, None]
    ks = ks_blk[bi, hi, top_i].reshape(B_, Q, N_KV, kk * SEL_BLOCK, HEAD_DIM)
    vs = vs_blk[bi, hi, top_i].reshape(B_, Q, N_KV, kk * SEL_BLOCK, HEAD_DIM)
    spos = (top_i[..., None] * SEL_BLOCK + jnp.arange(SEL_BLOCK)).reshape(B_, Q, N_KV, kk * SEL_BLOCK)
    smask = (spos <= q_pos[None, :, None, None]) & jnp.repeat(top_v > -jnp.inf, SEL_BLOCK, axis=-1)
    s = jnp.einsum('bqkgd,bqknd->bqkgn', q, ks)
    o_sel = jnp.einsum('bqkgn,bqknd->bqkgd', masked_softmax(s, smask[:, :, :, None, :]).astype(dt), vs)
    wd = t - w_pos[None, :]
    wmask = (w_pos[None, :] >= 0) & (wd >= 0) & (wd <= WINDOW)
    s = jnp.einsum('bqkgd,bnkd->bqkgn', q, kw)
    o_win = jnp.einsum('bqkgn,bnkd->bqkgd', masked_softmax(s, wmask[None, :, None, None, :]).astype(dt), vw)
    return gates[..., 0:1] * o_cmp + gates[..., 1:2] * o_sel + gates[..., 2:3] * o_win


def nsa_prompt(q, gates, kc_raw, vc_raw, ksel, vsel, kwin, vwin, phi, phi_b, g_kc):
    B_, S = q.shape[:2]
    kc, vc, cmp_end = nsa_compress(kc_raw, vc_raw, phi, phi_b, g_kc)
    ks_blk, vs_blk = sel_blocks(ksel), sel_blocks(vsel)
    zpad = jnp.zeros((B_, WINDOW, N_KV, HEAD_DIM), kwin.dtype)
    kw_pad = jnp.concatenate([zpad, kwin], axis=1)
    vw_pad = jnp.concatenate([zpad, vwin], axis=1)
    nq = S // Q_BLOCK

    def body(args):
        qc, gc, i = args
        start = i * Q_BLOCK
        kw = lax.dynamic_slice_in_dim(kw_pad, start, WINDOW + Q_BLOCK, axis=1)
        vw = lax.dynamic_slice_in_dim(vw_pad, start, WINDOW + Q_BLOCK, axis=1)
        return nsa_attend(qc, start + jnp.arange(Q_BLOCK), gc, kc, vc, cmp_end, ks_blk, vs_blk,
                          kw, vw, start - WINDOW + jnp.arange(WINDOW + Q_BLOCK))

    qb = q.reshape(B_, nq, Q_BLOCK, N_KV, GQA, HEAD_DIM).swapaxes(0, 1)
    gb = gates.reshape(B_, nq, Q_BLOCK, N_KV, GQA, 3).swapaxes(0, 1)
    o = lax.map(body, (qb, gb, jnp.arange(nq)))
    o = o.swapaxes(0, 1).reshape(B_, S, N_HEADS * HEAD_DIM)
    rows = jnp.stack([kc_raw, vc_raw, ksel, vsel], axis=2)
    win_new = jnp.stack([kwin, vwin], axis=2)[:, S - min(WINDOW, S):]
    return o, rows, win_new


def nsa_sample(pool, page_table, win_buf, q, gates, kc_raw, vc_raw, ksel, vsel, kwin, vwin, phi, phi_b, g_kc):
    B_, L = q.shape[:2]
    past = pool[page_table]
    past = past.reshape(B_, past.shape[1] * past.shape[2], 4, N_KV, HEAD_DIM)
    P = past.shape[1]
    rows = jnp.stack([kc_raw, vc_raw, ksel, vsel], axis=2)
    full = jnp.concatenate([past.astype(rows.dtype), rows], axis=1)
    kc, vc, cmp_end = nsa_compress(full[:, :, 0], full[:, :, 1], phi, phi_b, g_kc)
    ks_blk, vs_blk = sel_blocks(full[:, :, 2]), sel_blocks(full[:, :, 3])
    Lw = win_buf.shape[1]
    new_w = jnp.stack([kwin, vwin], axis=2)
    wfull = jnp.concatenate([win_buf.astype(new_w.dtype), new_w], axis=1)
    o = nsa_attend(q, P + jnp.arange(L), gates, kc, vc, cmp_end, ks_blk, vs_blk,
                   wfull[:, :, 0], wfull[:, :, 1], P - Lw + jnp.arange(Lw + L))
    return o.reshape(B_, L, N_HEADS * HEAD_DIM), rows, wfull[:, L:]


def expert_dispatch(xt, eidx, gate, w_gu, w_down):
    T, D = xt.shape
    M = T * TOP_E
    fe = eidx.reshape(M)
    ftok = jnp.arange(M, dtype=jnp.int32) // TOP_E
    fgate = gate.reshape(M)
    order = jnp.argsort(fe)
    se, stok, sgate = fe[order], ftok[order], fgate[order]
    counts = jnp.bincount(fe, length=N_EXPERTS)
    padded = (counts + MOE_BLOCK - 1) // MOE_BLOCK * MOE_BLOCK
    pad_end = jnp.cumsum(padded)
    pad_start = pad_end - padded
    start = jnp.cumsum(counts) - counts
    dest = pad_start[se] + jnp.arange(M) - start[se]
    n_blk = -(-M // MOE_BLOCK) + N_EXPERTS
    P = n_blk * MOE_BLOCK
    buf_tok = jnp.zeros((P,), jnp.int32).at[dest].set(stok)
    buf_gate = jnp.zeros((P,), fgate.dtype).at[dest].set(sgate)
    blk_exp = jnp.minimum(jnp.searchsorted(pad_end, jnp.arange(n_blk) * MOE_BLOCK, side='right'), N_EXPERTS - 1)
    xb = xt[buf_tok].reshape(n_blk, MOE_BLOCK, D)

    def run(args):
        xi, e = args
        a, b = jnp.split(xi @ w_gu[e], 2, axis=-1)
        return (jax.nn.silu(a) * b) @ w_down[e]

    yb = lax.map(run, (xb, blk_exp)).reshape(P, D)
    return jax.ops.segment_sum(yb * buf_gate[:, None].astype(yb.dtype), buf_tok, num_segments=T)


def moe_ffn(x, wg_r, bg_r, we_r, be_r, w_gu, w_down):
    B_, L, D = x.shape
    xt = x.reshape(B_ * L, D)
    T = xt.shape[0]
    lg = (xt @ wg_r + bg_r).astype(jnp.float32)
    pg = jax.nn.softmax(lg, axis=-1)
    gsel = jnp.argmax(lg, axis=-1)
    p_group = jnp.take_along_axis(pg, gsel[:, None], axis=-1)
    le = (xt @ we_r + be_r).astype(jnp.float32).reshape(T, N_GROUPS, EXP_PER_GROUP)
    le_g = jnp.take_along_axis(le, gsel[:, None, None], axis=1)[:, 0]
    tv, ti = lax.top_k(le_g, TOP_E)
    gate = p_group * jax.nn.softmax(tv, axis=-1)
    eidx = gsel[:, None] * EXP_PER_GROUP + ti
    return expert_dispatch(xt, eidx, gate, w_gu, w_down).reshape(B_, L, D)


def layer_forward(x, pos0, lw, pool_prev, rgc_prev, rgh0, sc_prev, nsa_fn):
    B_, L, _ = x.shape
    w_perm = permute_w_in(lw['w_in']).astype(MXU_DTYPE)
    proj2d = norm_matmul(x.reshape(B_ * L, D_MODEL), lw['norm_mix_g'], w_perm)
    if pool_prev is None:
        ym, tails, hlast = mixers_prompt(proj2d, lw, B_, L)
        pool_new = tails[:, 0, HALO - POOL_KEEP:]
        rgc_new = tails[:, 1, HALO - (RG_CONV - 1):]
        sc_new = tails[:, 2, HALO - (SC_CONV - 1):]
        rgh_new = hlast[:, 0]
    else:
        ym, pool_new, rgc_new, rgh_new, sc_new = mixers_sample(proj2d, lw, pos0, pool_prev, rgc_prev, rgh0, sc_prev)
    y_nsa, nsa_rows, win_new = nsa_fn(proj2d, lw['nsa_phi'], lw['nsa_phi_b'], lw['nsa_qk_g'])
    x = mix_out_moe(ym, y_nsa.reshape(B_ * L, GROUP_W), x.reshape(B_ * L, D_MODEL), lw,
                    MOE_TILE_PROMPT if L > 1 else MOE_TILE_SAMPLE)
    return x.reshape(B_, L, D_MODEL), (nsa_rows, win_new, pool_new, rgc_new, rgh_new, sc_new)


def kernel(x_prompt, x_sample, cache_nsa, state_win_kv, state_pool, state_rg_conv, state_rg_h, state_sc_conv,
           page_table, norm_mix_g, w_in, pool_w, pool_scale, rg_conv_w, rg_conv_b, rg_w_a, rg_b_a, rg_w_x, rg_b_x,
           rg_lambda, nsa_phi, nsa_phi_b, nsa_qk_g, sc_conv_w, sc_conv_b, mix_out_g, w_out, norm_ffn_g,
           router_group_w, router_group_b, router_expert_w, router_expert_b, exp_w_gu, exp_w_down):
    past_len = page_table.shape[1] * cache_nsa.shape[2]
    xp, xs = x_prompt, x_sample
    cache3 = feature_major_pages(cache_nsa)
    win3 = state_win_kv.transpose(0, 1, 3, 4, 5, 2).reshape(DEPTH * state_win_kv.shape[1], 2, N_KV * HEAD_DIM,
                                                             state_win_kv.shape[2])
    cache_ab = cache_compress(cache3, nsa_phi)
    Bp = xp.shape[0]
    st_p, st_s = [], []
    for l in range(DEPTH):
        lw = dict(norm_mix_g=norm_mix_g[l], w_in=w_in[l], pool_w=pool_w[l], pool_scale=pool_scale[l],
                  rg_conv_w=rg_conv_w[l], rg_conv_b=rg_conv_b[l], rg_w_a=rg_w_a[l], rg_b_a=rg_b_a[l],
                  rg_w_x=rg_w_x[l], rg_b_x=rg_b_x[l], rg_lambda=rg_lambda[l], nsa_phi=nsa_phi[l],
                  nsa_phi_b=nsa_phi_b[l], nsa_qk_g=nsa_qk_g[l], sc_conv_w=sc_conv_w[l], sc_conv_b=sc_conv_b[l],
                  mix_out_g=mix_out_g[l], w_out=w_out[l], norm_ffn_g=norm_ffn_g[l],
                  router_group_w=router_group_w[l], router_group_b=router_group_b[l],
                  router_expert_w=router_expert_w[l], router_expert_b=router_expert_b[l],
                  exp_w_gu=exp_w_gu.reshape((DEPTH * N_EXPERTS,) + exp_w_gu.shape[2:]),
                  exp_w_down=exp_w_down.reshape((DEPTH * N_EXPERTS,) + exp_w_down.shape[2:]),
                  expert_base=l * N_EXPERTS)
        xp, sp = layer_forward(xp, 0, lw, None, None, None, None,
                               lambda p, phi, phi_b, g: nsa_prompt_pallas(p, Bp, xp.shape[1], phi, phi_b, g))
        xs, ss = layer_forward(xs, past_len, lw, state_pool[l], state_rg_conv[l], state_rg_h[l], state_sc_conv[l],
                               lambda p, phi, phi_b, g: nsa_sample_pallas(p, l, cache3, cache_ab, page_table, win3,
                                                                          phi_b, g))
        st_p.append(sp)
        st_s.append(ss)

    def stk(lst, i):
        return jnp.stack([s[i] for s in lst])

    return (xp, xs, stk(st_p, 0), stk(st_s, 0), stk(st_p, 1), stk(st_s, 1), stk(st_p, 2), stk(st_s, 2),
            stk(st_p, 3), stk(st_s, 3), stk(st_p, 4), stk(st_s, 4), stk(st_p, 5), stk(st_s, 5))
```

```python
import functools
import jax, jax.numpy as jnp
from jax import lax
import numpy as np
from jax.experimental import pallas as pl
from jax.experimental.pallas import tpu as pltpu

D_MODEL = 1024
BATCH = 4
SEQ = 4096
DEPTH = 2
DEC_BATCH = 128
DEC_SEQ = 1
PAST_LEN = 2048
PAGE_SIZE = 128

MIX_W = D_MODEL
GROUP_W = MIX_W // 4
POOL_W = GROUP_W
POOL_WINDOWS = (2, 4, 8, 16)
POOL_GROUP = POOL_W // len(POOL_WINDOWS)
POOL_KEEP = max(POOL_WINDOWS) - 1
RG_W = GROUP_W
RG_HEADS = 4
RG_BLOCK = RG_W // RG_HEADS
RG_CONV = 4
RG_C = 8.0
HEAD_DIM = 64
N_HEADS = GROUP_W // HEAD_DIM
N_KV = 2
GQA = N_HEADS // N_KV
CMP_BLOCK = 32
CMP_STRIDE = 16
SEL_BLOCK = 64
SEL_TOPK = 16
WINDOW = 512
Q_BLOCK = 128
SC_W = GROUP_W
SC_CONV = 3
N_GROUPS = 4
EXP_PER_GROUP = 8
N_EXPERTS = N_GROUPS * EXP_PER_GROUP
TOP_E = 2
D_EXPERT = 512
MOE_BLOCK = 128
EPS = 1e-6
SPLIT_SIZES = (POOL_W, RG_W, RG_W, N_HEADS * HEAD_DIM, 6 * N_KV * HEAD_DIM, 3 * N_HEADS, 3 * SC_W)
N_IN = sum(SPLIT_SIZES)

LANE = 128
ROW_TILE = 512
VMEM_LIMIT = 48 * 1024 * 1024
MXU_DTYPE = jnp.bfloat16
F32 = jnp.float32
NEG = -1e30

KV_W = 6 * N_KV * HEAD_DIM
COL_Q = 0
COL_KV = COL_Q + N_HEADS * HEAD_DIM
COL_POOL = COL_KV + KV_W
COL_RX = COL_POOL + POOL_W
COL_RGATE = COL_RX + RG_W
COL_SC = COL_RGATE + RG_W
COL_NG = COL_SC + 3 * SC_W
N_IN_PAD = COL_NG + LANE
SEL_TILE = 512
N_SEL_PROMPT = SEQ // SEL_BLOCK


def _cparams(n_axes=1):
    return pltpu.CompilerParams(dimension_semantics=("arbitrary",) * n_axes, vmem_limit_bytes=VMEM_LIMIT)


def _mm(a, b):
    return jnp.dot(a.astype(MXU_DTYPE), b.astype(MXU_DTYPE), preferred_element_type=F32)


def _mm_nt(a, b):
    return lax.dot_general(a.astype(MXU_DTYPE), b.astype(MXU_DTYPE), (((1,), (1,)), ((), ())),
                           preferred_element_type=F32)


def permute_w_in(w):
    pu, rx, rgate, q, kv, ng, sc = split_cols(w, SPLIT_SIZES)
    pad = jnp.zeros((w.shape[0], LANE - ng.shape[1]), w.dtype)
    return jnp.concatenate([q, kv, pu, rx, rgate, sc, ng, pad], axis=1)


def _norm_matmul_body(x_ref, g_ref, w_ref, o_ref):
    xf = x_ref[...]
    h = xf * lax.rsqrt(jnp.mean(xf * xf, axis=-1, keepdims=True) + EPS) * g_ref[...]
    o_ref[...] = _mm(h, w_ref[...])


def norm_matmul(x2d, g, w):
    T, D = x2d.shape
    N = w.shape[1]
    tm = min(ROW_TILE, T)
    return pl.pallas_call(
        _norm_matmul_body,
        grid=(T // tm,),
        in_specs=[pl.BlockSpec((tm, D), lambda i: (i, 0)),
                  pl.BlockSpec((1, D), lambda i: (0, 0)),
                  pl.BlockSpec((D, N), lambda i: (0, 0))],
        out_specs=pl.BlockSpec((tm, N), lambda i: (i, 0)),
        out_shape=jax.ShapeDtypeStruct((T, N), F32),
        compiler_params=_cparams(),
        name="norm_in_proj",
    )(x2d, g.reshape(1, D), w)


def _seg_rmsnorm(x, g):
    x2 = x * x
    left = lax.broadcasted_iota(jnp.int32, x.shape, 1) < HEAD_DIM
    s_l = jnp.sum(jnp.where(left, x2, 0.0), axis=-1, keepdims=True)
    s_r = jnp.sum(jnp.where(left, 0.0, x2), axis=-1, keepdims=True)
    ms = jnp.where(left, s_l, s_r) * (1.0 / HEAD_DIM)
    return x * lax.rsqrt(ms + EPS) * g


def _nsa_prep_body(qkv_ref, ng_ref, g_ref, perm_ref, qa_ref, kvb_ref, rawb_ref, rows_t_ref, win_t_ref, win_ref,
                   gates_ref):
    g = g_ref[...]
    for hb in range(N_KV):
        qn = _seg_rmsnorm(qkv_ref[:, COL_Q + hb * LANE:COL_Q + (hb + 1) * LANE], g[0:1]) * (HEAD_DIM ** -0.5)
        qa_ref[:, hb * 2 * LANE:(hb + 1) * 2 * LANE] = _mm(qn, perm_ref[hb]).astype(qa_ref.dtype)
    comp = [qkv_ref[:, COL_KV + c * LANE:COL_KV + (c + 1) * LANE] for c in range(6)]
    comp[2] = _seg_rmsnorm(comp[2], g[2:3])
    comp[4] = _seg_rmsnorm(comp[4], g[3:4])
    for c in range(6):
        kvb_ref[:, c * LANE:(c + 1) * LANE] = comp[c].astype(kvb_ref.dtype)
    for c in range(2):
        rawb_ref[:, c * LANE:(c + 1) * LANE] = comp[c].astype(rawb_ref.dtype)
    for c in range(4):
        rows_t_ref[0, c * LANE:(c + 1) * LANE, :] = comp[c].T
    for c in range(2):
        win_t_ref[0, c * LANE:(c + 1) * LANE, :] = comp[4 + c].T
        win_ref[:, c * LANE:(c + 1) * LANE] = comp[4 + c]
    gates_ref[...] = jax.nn.sigmoid(ng_ref[...])


def _q_place_matrices():
    p = np.zeros((N_KV, LANE, 2 * LANE), np.float32)
    for hb in range(N_KV):
        for gq in range(GQA):
            for d in range(HEAD_DIM):
                p[hb, gq * HEAD_DIM + d, gq * LANE + hb * HEAD_DIM + d] = 1.0
    return jnp.asarray(p, MXU_DTYPE)


def nsa_prep(proj, qk_g, B_, S):
    T = proj.shape[0]
    tm = min(ROW_TILE, S)
    tpb = S // tm
    qkv_w = COL_POOL
    g4 = jnp.tile(qk_g, (1, 2))
    return pl.pallas_call(
        _nsa_prep_body,
        grid=(T // tm,),
        in_specs=[pl.BlockSpec((tm, qkv_w), lambda i: (i, 0)),
                  pl.BlockSpec((tm, LANE), lambda i: (i, COL_NG // LANE)),
                  pl.BlockSpec((4, LANE), lambda i: (0, 0)),
                  pl.BlockSpec((N_KV, LANE, 2 * LANE), lambda i: (0, 0, 0))],
        out_specs=[pl.BlockSpec((tm, 4 * LANE), lambda i: (i, 0)),
                   pl.BlockSpec((tm, 6 * LANE), lambda i: (i, 0)),
                   pl.BlockSpec((tm, 2 * LANE), lambda i: (i, 0)),
                   pl.BlockSpec((1, 4 * LANE, tm), lambda i: (i // tpb, 0, i % tpb)),
                   pl.BlockSpec((1, 2 * LANE, tm), lambda i: (i // tpb, 0, i % tpb)),
                   pl.BlockSpec((tm, 2 * LANE), lambda i: (i, 0)),
                   pl.BlockSpec((tm, LANE), lambda i: (i, 0))],
        out_shape=[jax.ShapeDtypeStruct((T, 4 * LANE), MXU_DTYPE),
                   jax.ShapeDtypeStruct((T, 6 * LANE), MXU_DTYPE),
                   jax.ShapeDtypeStruct((T, 2 * LANE), MXU_DTYPE),
                   jax.ShapeDtypeStruct((B_, 4 * LANE, S), F32),
                   jax.ShapeDtypeStruct((B_, 2 * LANE, S), F32),
                   jax.ShapeDtypeStruct((T, 2 * LANE), F32),
                   jax.ShapeDtypeStruct((T, LANE), F32)],
        compiler_params=_cparams(),
        name="nsa_prep",
    )(proj, proj, g4, _q_place_matrices())


def compress_weights(phi):
    R = CMP_BLOCK // CMP_STRIDE
    wr = phi.reshape(2, R, CMP_STRIDE, HEAD_DIM, HEAD_DIM)
    eye = jnp.eye(2, dtype=phi.dtype)
    w = jnp.einsum('crjde,cx,hy->rjchdxye', wr, eye, eye)
    return w.reshape(R, CMP_STRIDE * 2 * LANE, 2 * LANE).astype(MXU_DTYPE)


def _compress_body(x_ref, w_ref, b_ref, g_ref, kc_ref, vc_ref):
    x = x_ref[0]
    nch = x.shape[0]
    a = _mm(x, w_ref[0])
    bm = _mm(x, w_ref[1])
    out = a + pltpu.roll(bm, nch - 1, 0) + b_ref[...]
    kc_ref[0] = _seg_rmsnorm(out[:, 0:LANE], g_ref[...]).astype(kc_ref.dtype)
    vc_ref[0] = out[:, LANE:2 * LANE].astype(vc_ref.dtype)


def nsa_compress_pallas(rawb3, wc, phi_b, g_kc):
    B_, nch, K = rawb3.shape
    bias = jnp.concatenate([jnp.tile(phi_b[0], 2), jnp.tile(phi_b[1], 2)]).reshape(1, 2 * LANE)
    return pl.pallas_call(
        _compress_body,
        grid=(B_,),
        in_specs=[pl.BlockSpec((1, nch, K), lambda b: (b, 0, 0)),
                  pl.BlockSpec(wc.shape, lambda b: (0, 0, 0)),
                  pl.BlockSpec((1, 2 * LANE), lambda b: (0, 0)),
                  pl.BlockSpec((1, LANE), lambda b: (0, 0))],
        out_specs=[pl.BlockSpec((1, nch, LANE), lambda b: (b, 0, 0)),
                   pl.BlockSpec((1, nch, LANE), lambda b: (b, 0, 0))],
        out_shape=[jax.ShapeDtypeStruct((B_, nch, LANE), MXU_DTYPE),
                   jax.ShapeDtypeStruct((B_, nch, LANE), MXU_DTYPE)],
        compiler_params=_cparams(),
        name="nsa_compress",
    )(rawb3, wc, bias, jnp.tile(g_kc, 2).reshape(1, LANE))


def _online_update(carry, s, v):
    m, l, acc = carry
    m_new = jnp.maximum(m, jnp.max(s, axis=-1, keepdims=True))
    alpha = jnp.exp(m - m_new)
    p = jnp.exp(s - m_new)
    l = alpha * l + jnp.sum(p, axis=-1, keepdims=True)
    acc = alpha * acc + _mm(p, v)
    return m_new, l, acc


def _select_blocks(imp, start):
    n_sel = N_SEL_PROMPT
    sc_t = imp.T[0:n_sel]
    blk = lax.broadcasted_iota(jnp.int32, sc_t.shape, 0)
    cur = (start + lax.broadcasted_iota(jnp.int32, sc_t.shape, 1)) // SEL_BLOCK
    valid = blk <= cur
    forced = (blk == 0) | (blk == cur) | (blk == cur - 1)
    score = jnp.where(valid, sc_t, -jnp.inf)
    score = jnp.where(forced & valid, jnp.inf, score)
    cnt = jnp.zeros(sc_t.shape, F32)
    for i in range(n_sel):
        ri = score[i:i + 1, :]
        beats = (ri > score) | ((ri == score) & (blk > i))
        cnt = cnt + jnp.where(beats, 1.0, 0.0)
    sel_t = jnp.where((cnt < SEL_TOPK) & (score > -jnp.inf), 1.0, 0.0)
    sel_t = jnp.concatenate([sel_t, jnp.zeros((LANE - n_sel, sc_t.shape[1]), F32)], axis=0)
    return sel_t.T


def _nsa_attn_body(qa_ref, gates_ref, kc_ref, vc_ref, kv_ref, ov_ref, e_ref, o_ref):
    i = pl.program_id(1)
    start = i * Q_BLOCK
    Q = Q_BLOCK
    R = GQA * Q
    t_row = start + lax.broadcasted_iota(jnp.int32, (R, 1), 0) % Q
    gates = gates_ref[...]
    lane_q = lax.broadcasted_iota(jnp.int32, (Q, LANE), 1)
    heads = range(N_KV)
    qs = [jnp.concatenate([qa_ref[:, (h * GQA + gq) * LANE:(h * GQA + gq + 1) * LANE] for gq in range(GQA)], axis=0)
          for h in heads]

    o_cmps, sel_bias = [], []
    kc = kc_ref[0]
    ncmp = kc.shape[0]
    cmp_end = lax.broadcasted_iota(jnp.int32, (R, ncmp), 1) * CMP_STRIDE + (CMP_BLOCK - 1)
    for h in heads:
        s = jnp.where(cmp_end <= t_row, _mm_nt(qs[h], kc), -jnp.inf)
        m = jnp.max(s, axis=-1, keepdims=True)
        e = jnp.exp(s - jnp.where(m > -jnp.inf, m, 0.0))
        d = jnp.sum(e, axis=-1, keepdims=True)
        p_cmp = e / jnp.where(d > 0, d, 1.0)
        o_cmps.append(_mm(p_cmp, vc_ref[0]))
        imp = _mm(p_cmp[0:Q], ov_ref[...]) + _mm(p_cmp[Q:R], ov_ref[...])
        sel = _select_blocks(imp, start)
        sel_bias.append(jnp.concatenate([jnp.where(sel > 0.5, 0.0, NEG)] * GQA, axis=0).astype(MXU_DTYPE))

    def sel_scores(j):
        off = pl.multiple_of(j * SEL_TILE, SEL_TILE)
        k = kv_ref[pl.ds(off, SEL_TILE), 2 * LANE:3 * LANE]
        v = kv_ref[pl.ds(off, SEL_TILE), 3 * LANE:4 * LANE]
        return off, v, [_mm_nt(qs[h], k) + _mm(sel_bias[h], e_ref[j]) for h in heads]

    def sel_step(j, carry):
        _, v, ss = sel_scores(j)
        return tuple(_online_update(carry[h], ss[h], v) for h in heads)

    init = (jnp.full((R, 1), NEG, F32), jnp.zeros((R, 1), F32), jnp.zeros((R, LANE), F32))
    n_tiles = (start + Q + SEL_TILE - 1) // SEL_TILE
    carry = lax.fori_loop(0, n_tiles - 1, sel_step, (init,) * N_KV)
    off, v, ss = sel_scores(n_tiles - 1)
    causal = off + lax.broadcasted_iota(jnp.int32, (R, SEL_TILE), 1) <= t_row
    o_sels = []
    for h in heads:
        _, l_s, acc_s = _online_update(carry[h], jnp.where(causal, ss[h], NEG), v)
        o_sels.append(acc_s / l_s)

    n_w = WINDOW // Q + 1
    offs = [pl.multiple_of(jnp.maximum(i - kk, 0) * Q, Q) for kk in range(n_w)]
    kw = jnp.concatenate([kv_ref[pl.ds(o, Q), 4 * LANE:5 * LANE] for o in offs], axis=0)
    vw = jnp.concatenate([kv_ref[pl.ds(o, Q), 5 * LANE:6 * LANE] for o in offs], axis=0)
    lane_w = lax.broadcasted_iota(jnp.int32, (1, n_w * Q), 1)
    w_pos = (i - lane_w // Q) * Q + lane_w % Q
    wd = t_row - w_pos
    wmask = (w_pos >= 0) & (wd >= 0) & (wd <= WINDOW)
    o_wins = []
    for h in heads:
        s = jnp.where(wmask, _mm_nt(qs[h], kw), NEG)
        p = jnp.exp(s - jnp.max(s, axis=-1, keepdims=True))
        o_wins.append(_mm(p, vw) / jnp.sum(p, axis=-1, keepdims=True))

    for h in heads:
        o_cmp, o_sel, o_win = o_cmps[h], o_sels[h], o_wins[h]
        outs = []
        for gq in range(GQA):
            c0 = (h * GQA + gq) * 3
            rs = slice(gq * Q, (gq + 1) * Q)
            og = (gates[:, c0:c0 + 1] * o_cmp[rs] + gates[:, c0 + 1:c0 + 2] * o_sel[rs]
                  + gates[:, c0 + 2:c0 + 3] * o_win[rs])
            outs.append(og if gq == h else pltpu.roll(og, HEAD_DIM, 1))
        o_ref[:, h * LANE:(h + 1) * LANE] = jnp.where(lane_q < HEAD_DIM, outs[0], outs[1])


def _sel_constants(S):
    ncmp_rows = S // CMP_STRIDE
    ci = np.arange(ncmp_rows)[:, None] * CMP_STRIDE
    sj = np.arange(LANE)[None, :] * SEL_BLOCK
    ov = ((ci < sj + SEL_BLOCK) & (ci + CMP_BLOCK > sj) & (np.arange(LANE)[None, :] < S // SEL_BLOCK))
    n_t = S // SEL_TILE
    key_blk = (np.arange(n_t)[:, None, None] * SEL_TILE + np.arange(SEL_TILE)[None, None, :]) // SEL_BLOCK
    e = (np.arange(LANE)[None, :, None] == key_blk)
    return jnp.asarray(ov, MXU_DTYPE), jnp.asarray(e, MXU_DTYPE)


def nsa_attn_prompt(qa, gates, kc, vc, kvb, B_, S):
    nq = S // Q_BLOCK
    nch = S // CMP_STRIDE
    ov, e3 = _sel_constants(S)
    return pl.pallas_call(
        _nsa_attn_body,
        grid=(B_, nq),
        in_specs=[pl.BlockSpec((Q_BLOCK, 4 * LANE), lambda b, i: (b * nq + i, 0)),
                  pl.BlockSpec((Q_BLOCK, LANE), lambda b, i: (b * nq + i, 0)),
                  pl.BlockSpec((1, nch, LANE), lambda b, i: (b, 0, 0)),
                  pl.BlockSpec((1, nch, LANE), lambda b, i: (b, 0, 0)),
                  pl.BlockSpec((S, 6 * LANE), lambda b, i: (b, 0)),
                  pl.BlockSpec(ov.shape, lambda b, i: (0, 0)),
                  pl.BlockSpec(e3.shape, lambda b, i: (0, 0, 0))],
        out_specs=pl.BlockSpec((Q_BLOCK, 2 * LANE), lambda b, i: (b * nq + i, 0)),
        out_shape=jax.ShapeDtypeStruct((B_ * S, N_HEADS * HEAD_DIM), F32),
        compiler_params=_cparams(2),
        name="nsa_attn_prompt",
    )(qa, gates, kc, vc, kvb, ov, e3)


def nsa_prompt_pallas(proj, B_, S, phi, phi_b, qk_g):
    qa, kvb, rawb, rows_t, win_t, _, gates = nsa_prep(proj, qk_g, B_, S)
    nch = S // CMP_STRIDE
    kc, vc = nsa_compress_pallas(rawb.reshape(B_, nch, CMP_STRIDE * 2 * LANE), compress_weights(phi), phi_b, qk_g[1])
    o = nsa_attn_prompt(qa, gates, kc, vc, kvb, B_, S)
    rows = rows_t.reshape(B_, 4, N_KV, HEAD_DIM, S).transpose(0, 4, 1, 2, 3)
    wk = min(WINDOW, S)
    win_new = win_t[:, :, S - wk:].reshape(B_, 2, N_KV, HEAD_DIM, wk).transpose(0, 4, 1, 2, 3)
    return o.reshape(B_, S, N_HEADS * HEAD_DIM), rows, win_new


N_PAGES = PAST_LEN // PAGE_SIZE
N_CHUNK_S = PAST_LEN // CMP_STRIDE
N_SEL_S = -(-(PAST_LEN + DEC_SEQ) // SEL_BLOCK)
CUR_S = PAST_LEN // SEL_BLOCK
QROWS = 8


def compress_weights_paged(phi):
    R = CMP_BLOCK // CMP_STRIDE
    wr = phi.reshape(2, R, CMP_STRIDE, HEAD_DIM, HEAD_DIM)
    w = jnp.einsum('crjde,hy->cjhdrye', wr, jnp.eye(2, dtype=phi.dtype))
    return w.reshape(2, CMP_STRIDE * LANE, R * LANE).astype(MXU_DTYPE)


def _softmax_with_extra(s, s_new):
    m = jnp.maximum(jnp.max(s, axis=-1, keepdims=True), s_new)
    e = jnp.exp(s - m)
    e_new = jnp.exp(s_new - m)
    return e, e_new, jnp.sum(e, axis=-1, keepdims=True) + e_new


CHUNKS_PER_PAGE = PAGE_SIZE // CMP_STRIDE
SWEEP_PAGES = 64


def feature_major_pages(cache_nsa):
    d, n = cache_nsa.shape[:2]
    return cache_nsa.transpose(0, 1, 3, 4, 5, 2).reshape(d * n, 4, N_KV * HEAD_DIM, PAGE_SIZE)


def _cache_compress_body(c_ref, w_ref, o_ref, sk, sv):
    n_pages = c_ref.shape[0]

    def to_row_major(p, carry):
        r0 = pl.multiple_of(p * PAGE_SIZE, PAGE_SIZE)
        sk[pl.ds(r0, PAGE_SIZE), :] = c_ref[p, 0].T
        sv[pl.ds(r0, PAGE_SIZE), :] = c_ref[p, 1].T
        return carry

    lax.fori_loop(0, n_pages, to_row_major, 0, unroll=4)
    n = n_pages * CHUNKS_PER_PAGE
    for c, src in enumerate((sk, sv)):
        x = jnp.concatenate([src[pl.ds(j, n, stride=CMP_STRIDE), :] for j in range(CMP_STRIDE)], axis=1)
        ab = _mm(x, w_ref[0, c])
        o_ref[:, c * LANE:(c + 1) * LANE] = ab[:, 0:LANE]
        o_ref[:, (2 + c) * LANE:(3 + c) * LANE] = ab[:, LANE:2 * LANE]


def cache_compress(cache_fm, nsa_phi):
    n_total = cache_fm.shape[0]
    assert (n_total // DEPTH) % SWEEP_PAGES == 0
    tiles = n_total // DEPTH // SWEEP_PAGES
    wc = jnp.stack([compress_weights_paged(nsa_phi[l]) for l in range(DEPTH)])
    rows = SWEEP_PAGES * PAGE_SIZE
    return pl.pallas_call(
        _cache_compress_body,
        grid=(DEPTH * tiles,),
        in_specs=[pl.BlockSpec((SWEEP_PAGES, 2, LANE, PAGE_SIZE), lambda i: (i, 0, 0, 0)),
                  pl.BlockSpec((1,) + wc.shape[1:], lambda i: (i // tiles, 0, 0, 0))],
        out_specs=pl.BlockSpec((SWEEP_PAGES * CHUNKS_PER_PAGE, 4 * LANE), lambda i: (i, 0)),
        out_shape=jax.ShapeDtypeStruct((n_total * CHUNKS_PER_PAGE, 4 * LANE), F32),
        scratch_shapes=[pltpu.VMEM((rows, LANE), F32), pltpu.VMEM((rows, LANE), F32)],
        compiler_params=_cparams(),
        name="cache_compress",
    )(cache_fm, wc)


SAMPLE_GROUP = 2


def _nsa_sample_body(pt_ref, qa_ref, newb_ref, wnew_ref, gates_ref, *rest):
    n_pg = SAMPLE_GROUP * N_PAGES
    pages, abs_ = rest[:n_pg], rest[n_pg:2 * n_pg]
    y_ref, wout_ref = rest[-2:]
    gens = [_nsa_sample_one(u, qa_ref, newb_ref, wnew_ref, gates_ref, pages[u * N_PAGES:(u + 1) * N_PAGES],
                            abs_[u * N_PAGES:(u + 1) * N_PAGES], *rest[2 * n_pg:-2]) for u in range(SAMPLE_GROUP)]
    outs = [None] * SAMPLE_GROUP
    while any(o is None for o in outs):
        for u, gen in enumerate(gens):
            try:
                next(gen)
            except StopIteration as stop:
                outs[u] = stop.value
    y_ref[...] = jnp.stack([o[0] for o in outs])
    wout_ref[...] = jnp.stack([o[1] for o in outs])


def _nsa_sample_one(u, qa_ref, newb_ref, wnew_ref, gates_ref, pages, abs_, win_ref, bias_ref, gkc_ref, ov_ref, e_ref):
    qs = qa_ref[u]
    newb = newb_ref[u].astype(F32)
    lane = lax.broadcasted_iota(jnp.int32, (QROWS, LANE), 1)
    row = lax.broadcasted_iota(jnp.int32, (QROWS, LANE), 0)

    ab = jnp.concatenate([a[...] for a in abs_], axis=0)
    out = ab[:, 0:2 * LANE] + pltpu.roll(ab[:, 2 * LANE:4 * LANE], N_CHUNK_S - 1, 0) + bias_ref[...]
    kc = _seg_rmsnorm(out[:, 0:LANE], gkc_ref[...])
    vc = out[:, LANE:2 * LANE]
    yield

    s = _mm_nt(qs, kc)
    yield
    s = jnp.where(lane < N_CHUNK_S - 1, s, -jnp.inf)
    e = jnp.exp(s - jnp.max(s, axis=-1, keepdims=True))
    p_cmp = e / jnp.sum(e, axis=-1, keepdims=True)
    yield
    o_cmp = _mm(p_cmp, vc)
    imp = _mm(p_cmp, ov_ref[...])
    yield
    imp = imp +jnp.where(row % GQA == 0, pltpu.roll(imp, QROWS - 1, 0), pltpu.roll(imp, 1, 0))

    valid = lane <= CUR_S
    forced = (lane == 0) | (lane == CUR_S) | (lane == CUR_S - 1)
    score = jnp.where(valid, imp, -jnp.inf)
    score = jnp.where(forced & valid, jnp.inf, score)
    cnt = jnp.zeros((QROWS, LANE), F32)
    for i in range(N_SEL_S):
        ci = score[:, i:i + 1]
        cnt = cnt + jnp.where((ci > score) | ((ci == score) & (lane > i)), 1.0, 0.0)
    sel = jnp.where((cnt < SEL_TOPK) & (score > -jnp.inf), 1.0, 0.0)
    yield

    msel = _mm(sel, e_ref[...])
    s = jnp.concatenate([_mm(qs, pg[0, 0]) for pg in pages], axis=1)
    yield
    s = jnp.where(msel > 0.5, s, NEG)
    qf = qs.astype(F32)
    s_new = jnp.sum(qf * newb[:, 2 * LANE:3 * LANE], axis=-1, keepdims=True)
    s_new = jnp.where(sel[:, CUR_S:CUR_S + 1] > 0.5, s_new, NEG)
    e, e_new, d = _softmax_with_extra(s, s_new)
    yield
    acc_o = e_new.astype(MXU_DTYPE).astype(F32) * newb[:, 3 * LANE:4 * LANE]
    for p, pg in enumerate(pages):
        acc_o = acc_o + _mm_nt(e[:, p * PAGE_SIZE:(p + 1) * PAGE_SIZE], pg[0, 1])
    o_sel = acc_o / d
    yield

    s = _mm(qs, win_ref[u, 0])
    yield
    s_new =jnp.sum(qf * newb[:, 4 * LANE:5 * LANE], axis=-1, keepdims=True)
    e, e_new, d = _softmax_with_extra(s, s_new)
    o_win = (_mm_nt(e, win_ref[u, 1]) + e_new.astype(MXU_DTYPE).astype(F32) * newb[:, 5 * LANE:6 * LANE]) / d

    g = gates_ref[u]
    o = g[:, 0:1] * o_cmp + g[:, 1:2] * o_sel + g[:, 2:3] * o_win
    o_sw = pltpu.roll(o, HEAD_DIM, 1)
    lane1 = lax.broadcasted_iota(jnp.int32, (1, LANE), 1)
    ys = []
    for h in range(N_KV):
        a = (o if h == 0 else o_sw)[GQA * h:GQA * h + 1]
        b = (o if h == 1 else o_sw)[GQA * h + 1:GQA * h + 2]
        ys.append(jnp.where(lane1 < HEAD_DIM, a, b))
    lw = win_ref.shape[3]
    last = lax.broadcasted_iota(jnp.int32, (LANE, lw), 1) == lw - 1
    wouts = []
    for c in range(2):
        col = jnp.broadcast_to(wnew_ref[u][:, c * LANE:(c + 1) * LANE], (QROWS, LANE)).T[:, 0:1]
        wouts.append(jnp.where(last, col, pltpu.roll(win_ref[u, c], lw - 1, 1)))
    return jnp.concatenate(ys, axis=1), jnp.stack(wouts)


def _sample_constants():
    ci = np.arange(LANE)[:, None] * CMP_STRIDE
    sj = np.arange(LANE)[None, :] * SEL_BLOCK
    ov = ((ci < sj + SEL_BLOCK) & (ci + CMP_BLOCK > sj) & (np.arange(LANE)[:, None] < N_CHUNK_S - 1)
          & (np.arange(LANE)[None, :] < N_SEL_S))
    e = (np.arange(LANE)[:, None] == (np.arange(PAST_LEN)[None, :] // SEL_BLOCK))
    return jnp.asarray(ov, MXU_DTYPE), jnp.asarray(e, MXU_DTYPE)


def nsa_sample_pallas(proj, layer, cache_fm, cache_ab, page_table, win_fm, phi_b, qk_g):
    B_ = proj.shape[0]
    n_phys = cache_fm.shape[0] // DEPTH
    lw = win_fm.shape[3]
    assert page_table.shape == (B_, N_PAGES) and lw <= WINDOW and lw <= PAST_LEN and CUR_S == N_SEL_S - 1
    qa, kvb, _, rows_t, _, wnew, gates = nsa_prep(proj, qk_g, 1, B_)
    qa8 = jnp.pad(qa.astype(F32).reshape(B_, N_HEADS, LANE), ((0, 0), (0, QROWS - N_HEADS), (0, 0)))
    gates8 = jnp.pad(gates[:, :3 * N_HEADS].reshape(B_, N_HEADS, 3), ((0, 0), (0, QROWS - N_HEADS), (0, LANE - 3)))
    ov, e = _sample_constants()
    bias = jnp.concatenate([jnp.tile(phi_b[0], 2), jnp.tile(phi_b[1], 2)]).reshape(1, 2 * LANE)

    G = SAMPLE_GROUP
    assert B_ % G == 0
    seq_page = [(u, p) for u in range(G) for p in range(N_PAGES)]

    def page_spec(u, p):
        return pl.BlockSpec((1, 2, LANE, PAGE_SIZE), lambda b, pt: (layer * n_phys + pt[G * b + u, p], 1, 0, 0))

    def ab_spec(u, p):
        return pl.BlockSpec((CHUNKS_PER_PAGE, 4 * LANE), lambda b, pt: (layer * n_phys + pt[G * b + u, p], 0))

    def per_b(shape):
        return pl.BlockSpec((G,) + shape, lambda b, pt: (b, 0, 0))

    def const(a):
        return pl.BlockSpec(a.shape, lambda b, pt: (0,) * a.ndim)

    gkc = jnp.tile(qk_g[1], 2).reshape(1, LANE)
    y, wout = pl.pallas_call(
        _nsa_sample_body,
        grid_spec=pltpu.PrefetchScalarGridSpec(
            num_scalar_prefetch=1,
            grid=(B_ // G,),
            in_specs=[per_b((QROWS, LANE)), per_b((1, 6 * LANE)), per_b((1, 2 * LANE)), per_b((QROWS, LANE))]
                     + [page_spec(u, p) for u, p in seq_page] + [ab_spec(u, p) for u, p in seq_page]
                     + [pl.BlockSpec((G, 2, LANE, lw), lambda b, pt: (layer * (B_ // G) + b, 0, 0, 0)),
                        const(bias), const(gkc), const(ov), const(e)],
            out_specs=[per_b((1, 2 * LANE)), pl.BlockSpec((G, 2, LANE, lw), lambda b, pt: (b, 0, 0, 0))]),
        out_shape=[jax.ShapeDtypeStruct((B_, 1, 2 * LANE), F32),
                   jax.ShapeDtypeStruct((B_, 2, LANE, lw), F32)],
        compiler_params=_cparams(),
        name="nsa_sample",
    )(page_table, qa8, kvb.reshape(B_, 1, 6 * LANE), wnew.reshape(B_, 1, 2 * LANE), gates8,
      *([cache_fm] * (G * N_PAGES)), *([cache_ab] * (G * N_PAGES)), win_fm, bias, gkc, ov, e)
    rows = rows_t.reshape(4, N_KV, HEAD_DIM, B_).transpose(3, 0, 1, 2)[:, None]
    return (y.reshape(B_, 1, N_HEADS * HEAD_DIM), rows,
            wout.reshape(B_, 2, N_KV, HEAD_DIM, lw).transpose(0, 4, 1, 2, 3))


MIX_CHUNK = 512
HALO = 16
YM_W = POOL_W + RG_W + SC_W


def _expm1(x):
    p = jnp.full_like(x, 1.0 / 3628800.0)
    for c in (1.0 / 362880.0, 1.0 / 40320.0, 1.0 / 5040.0, 1.0 / 720.0, 1.0 / 120.0, 1.0 / 24.0, 1.0 / 6.0, 0.5, 1.0):
        p = p * x + c
    return jnp.where(jnp.abs(x) < 0.25, p * x, jnp.exp(x) - 1.0)


def _softplus(x):
    return jnp.maximum(x, 0.0) + jnp.log1p(jnp.exp(-jnp.abs(x)))


def _gelu_tanh(x):
    return 0.5 * x * (1.0 + jnp.tanh(np.sqrt(2.0 / np.pi).astype(np.float32) * (x + 0.044715 * (x * x * x))))


def _rg_coeffs(xc, wa, ba, wx, bx, lam):
    r = jax.nn.sigmoid(_mm(xc, wa) + ba)
    ig = jax.nn.sigmoid(_mm(xc, wx) + bx)
    log_a = (-RG_C * r) * _softplus(-lam)
    return jnp.exp(log_a), jnp.sqrt(-_expm1(2.0 * log_a)) * (ig * xc)


def _pool_select(s2, s4, s8, s16):
    lane = lax.broadcasted_iota(jnp.int32, s2.shape, 1)
    return jnp.where(lane < POOL_GROUP, s2, jnp.where(lane < 2 * POOL_GROUP, s4,
                                                      jnp.where(lane < 3 * POOL_GROUP, s8, s16)))


def _pool_count(pos, shape):
    lane = lax.broadcasted_iota(jnp.int32, shape, 1)
    win = jnp.left_shift(2, lane // POOL_GROUP)
    return jnp.minimum(win, pos + 1).astype(F32)


def _mixers_prompt_body(pu_ref, rx_ref, rg_ref, z_ref, bg_ref, cg_ref, pw_ref, ps_ref, cw_ref, cb_ref, wa_ref, ba_ref,
                        wx_ref, bx_ref, lam_ref, scw_ref, scb_ref, ym_ref, tails_ref, hlast_ref, halo, hcar):
    c = pl.program_id(1)
    tc = pu_ref.shape[0]

    @pl.when(c == 0)
    def _():
        halo[...] = jnp.zeros_like(halo)
        hcar[...] = jnp.zeros_like(hcar)

    pu, rx = pu_ref[...], rx_ref[...]
    u = cg_ref[...] * z_ref[...]
    ext = [jnp.concatenate([halo[i], v], axis=0) for i, v in enumerate((pu, rx, u))]

    def back(e, k):
        return pltpu.roll(e, k, 0)

    s2 = ext[0] + back(ext[0], 1)
    s4 = s2 + back(s2, 2)
    s8 = s4 + back(s4, 4)
    s16 = s8 + back(s8, 8)
    tot = _pool_select(s2, s4, s8, s16)[HALO:]
    pos = c * tc + lax.broadcasted_iota(jnp.int32, (tc, POOL_W), 0)
    d = tot / _pool_count(pos, (tc, POOL_W)) - pu
    ym_ref[:, 0:POOL_W] = _mm(d, pw_ref[...]) * ps_ref[...]

    cw = cw_ref[...]
    xc = cb_ref[...] + cw[RG_CONV - 1:RG_CONV] * rx
    for k in range(1, RG_CONV):
        xc = xc + cw[RG_CONV - 1 - k:RG_CONV - k] * back(ext[1], k)[HALO:]
    a, b = _rg_coeffs(xc, wa_ref[...], ba_ref[...], wx_ref[...], bx_ref[...], lam_ref[...])
    row = lax.broadcasted_iota(jnp.int32, (tc, RG_W), 0)
    k = 1
    while k < tc:
        a_prev = jnp.where(row < k, 1.0, pltpu.roll(a, k, 0))
        b_prev = jnp.where(row < k, 0.0, pltpu.roll(b, k, 0))
        b = a * b_prev + b
        a = a * a_prev
        k *= 2
    h = a * hcar[0:1] + b
    hcar[...] = jnp.broadcast_to(h[tc - 1:tc], hcar.shape)
    hlast_ref[0] = jnp.broadcast_to(h[tc - 1:tc], hcar.shape)
    ym_ref[:, POOL_W:POOL_W + RG_W] = h * _gelu_tanh(rg_ref[...])

    scw = scw_ref[...]
    v = scb_ref[...] + scw[SC_CONV - 1:SC_CONV] * u
    for k in range(1, SC_CONV):
        v = v + scw[SC_CONV - 1 - k:SC_CONV - k] * back(ext[2], k)[HALO:]
    ym_ref[:, POOL_W + RG_W:YM_W] = bg_ref[...] * v

    for i, val in enumerate((pu, rx, u)):
        halo[i] = val[tc - HALO:]
        tails_ref[0, i] = val[tc - HALO:]


def _block_diag(w):
    g, n, _ = w.shape
    return jnp.einsum('gij,gh->gihj', w, jnp.eye(g, dtype=w.dtype)).reshape(g * n, g * n)


def _mixer_params(lw):
    row = lambda a: a.reshape(1, -1)
    return [_block_diag(lw['pool_w']).astype(MXU_DTYPE), row(lw['pool_scale']), lw['rg_conv_w'], row(lw['rg_conv_b']),
            _block_diag(lw['rg_w_a']).astype(MXU_DTYPE), row(lw['rg_b_a']),
            _block_diag(lw['rg_w_x']).astype(MXU_DTYPE), row(lw['rg_b_x']), row(lw['rg_lambda']),
            lw['sc_conv_w'], row(lw['sc_conv_b'])]


def _proj_col_specs(rows, index):
    cols = (COL_POOL, COL_RX, COL_RGATE, COL_SC, COL_SC + SC_W, COL_SC + 2 * SC_W)
    return [pl.BlockSpec((rows, GROUP_W), functools.partial(index, col // GROUP_W)) for col in cols]


def mixers_prompt(proj, lw, B_, S):
    tc = min(MIX_CHUNK, S)
    nc = S // tc
    params = _mixer_params(lw)
    fixed = lambda a: pl.BlockSpec(a.shape, lambda b, c: (0,) * a.ndim)
    return pl.pallas_call(
        _mixers_prompt_body,
        grid=(B_, nc),
        in_specs=_proj_col_specs(tc, lambda col, b, c: (b * nc + c, col)) + [fixed(a) for a in params],
        out_specs=[pl.BlockSpec((tc, YM_W), lambda b, c: (b * nc + c, 0)),
                   pl.BlockSpec((1, 3, HALO, GROUP_W), lambda b, c: (b, 0, 0, 0)),
                   pl.BlockSpec((1, 8, RG_W), lambda b, c: (b, 0, 0))],
        out_shape=[jax.ShapeDtypeStruct((B_ * S, YM_W), F32),
                   jax.ShapeDtypeStruct((B_, 3, HALO, GROUP_W), F32),
                   jax.ShapeDtypeStruct((B_, 8, RG_W), F32)],
        scratch_shapes=[pltpu.VMEM((3, HALO, GROUP_W), F32), pltpu.VMEM((8, RG_W), F32)],
        compiler_params=_cparams(2),
        name="mixers_prompt",
    )(*([proj] * 6), *params)


def _mixers_sample_body(pos0, pu_ref, rx_ref, rg_ref, z_ref, bg_ref, cg_ref, pp_ref, rp_ref, h0_ref, sp_ref, pw_ref,
                        ps_ref, cw_ref, cb_ref, wa_ref, ba_ref, wx_ref, bx_ref, lam_ref, scw_ref, scb_ref,
                        ym_ref, pn_ref, rn_ref, hn_ref, sn_ref):
    pu, rx = pu_ref[...], rx_ref[...]
    u = cg_ref[...] * z_ref[...]
    run, sums = pu, {}
    for k in range(1, POOL_KEEP + 1):
        run = run + pp_ref[POOL_KEEP - k]
        sums[k + 1] = run
    tot = _pool_select(*(sums[w] for w in POOL_WINDOWS))
    d = tot / _pool_count(pos0, pu.shape) - pu
    ym_ref[:, 0:POOL_W] = _mm(d, pw_ref[...]) * ps_ref[...]
    for k in range(POOL_KEEP - 1):
        pn_ref[k] = pp_ref[k + 1]
    pn_ref[POOL_KEEP - 1] = pu

    cw = cw_ref[...]
    xc = cb_ref[...] + cw[RG_CONV - 1:RG_CONV] * rx
    for k in range(RG_CONV - 1):
        xc = xc + cw[k:k + 1] * rp_ref[k]
    a, b = _rg_coeffs(xc, wa_ref[...], ba_ref[...], wx_ref[...], bx_ref[...], lam_ref[...])
    h = b + a * h0_ref[...]
    hn_ref[...] = h
    ym_ref[:, POOL_W:POOL_W + RG_W] = h * _gelu_tanh(rg_ref[...])
    for k in range(RG_CONV - 2):
        rn_ref[k] = rp_ref[k + 1]
    rn_ref[RG_CONV - 2] = rx

    scw = scw_ref[...]
    v = scb_ref[...] + scw[SC_CONV - 1:SC_CONV] * u
    for k in range(SC_CONV - 1):
        v = v + scw[k:k + 1] * sp_ref[k]
    ym_ref[:, POOL_W + RG_W:YM_W] = bg_ref[...] * v
    for k in range(SC_CONV - 2):
        sn_ref[k] = sp_ref[k + 1]
    sn_ref[SC_CONV - 2] = u


def mixers_sample(proj, lw, pos0, pool_prev, rgc_prev, h0, sc_prev):
    B_ = proj.shape[0]
    params = _mixer_params(lw)
    states = [pool_prev.transpose(1, 0, 2), rgc_prev.transpose(1, 0, 2), h0, sc_prev.transpose(1, 0, 2)]
    full = lambda a: pl.BlockSpec(a.shape, lambda i: (0,) * a.ndim)
    ym, pn, rn, hn, sn = pl.pallas_call(
        functools.partial(_mixers_sample_body, pos0),
        grid=(1,),
        in_specs=_proj_col_specs(B_, lambda col, i: (0, col)) + [full(a) for a in states] + [full(a) for a in params],
        out_specs=[pl.BlockSpec((B_, YM_W), lambda i: (0, 0))] + [full(a) for a in states],
        out_shape=[jax.ShapeDtypeStruct((B_, YM_W), F32)] + [jax.ShapeDtypeStruct(a.shape, F32) for a in states],
        compiler_params=_cparams(),
        name="mixers_sample",
    )(*([proj] * 6), *states, *params)
    return ym, pn.transpose(1, 0, 2), rn.transpose(1, 0, 2), hn, sn.transpose(1, 0, 2)


ROUTE_W = LANE
GROUP_LANE0 = N_EXPERTS
MOE_TILE_PROMPT = 512
MOE_TILE_SAMPLE = 32
COMBINE_TILE = 256
FETCH_GROUPS = 8


def _rms(x, g):
    return x * lax.rsqrt(jnp.mean(x * x, axis=-1, keepdims=True) + EPS) * g


def _mix_out_router_body(ym_ref, yn_ref, x_ref, og_ref, wo_ref, gf_ref, wr_ref, br_ref, tri_ref, x2_ref, xn_ref,
                         route_ref, cnt_ref, cnt_sc):
    og = og_ref[...]
    groups = (ym_ref[:, 0:POOL_W], ym_ref[:, POOL_W:POOL_W + RG_W], yn_ref[...], ym_ref[:, POOL_W + RG_W:YM_W])
    yn = jnp.concatenate([_rms(y, og[:, i * GROUP_W:(i + 1) * GROUP_W]) for i, y in enumerate(groups)], axis=1)
    x2 = x_ref[...] + _mm(yn, wo_ref[...])
    x2_ref[...] = x2
    xn = _rms(x2, gf_ref[...])
    xn_ref[...] = xn
    logits = _mm(xn, wr_ref[...]) + br_ref[...]
    lane = lax.broadcasted_iota(jnp.int32, logits.shape, 1)
    is_grp = (lane >= GROUP_LANE0) & (lane < GROUP_LANE0 + N_GROUPS)
    grp = jnp.where(is_grp, logits, -jnp.inf)
    gmax = jnp.max(grp, axis=-1, keepdims=True)
    gsel = jnp.min(jnp.where(grp == gmax, lane - GROUP_LANE0, N_GROUPS), axis=-1, keepdims=True)
    p_group = 1.0 / jnp.sum(jnp.where(is_grp, jnp.exp(logits - gmax), 0.0), axis=-1, keepdims=True)
    le = jnp.where((lane < N_EXPERTS) & (lane // EXP_PER_GROUP == gsel), logits, -jnp.inf)
    m1 = jnp.max(le, axis=-1, keepdims=True)
    i1 = jnp.min(jnp.where(le == m1, lane, LANE), axis=-1, keepdims=True)
    le2 = jnp.where(lane == i1, -jnp.inf, le)
    m2 = jnp.max(le2, axis=-1, keepdims=True)
    i2 = jnp.min(jnp.where(le2 == m2, lane, LANE), axis=-1, keepdims=True)
    e2 = jnp.exp(m2 - m1)
    g1 = p_group * (1.0 / (1.0 + e2))
    g2 = p_group * (e2 / (1.0 + e2))
    @pl.when(pl.program_id(0) == 0)
    def _():
        cnt_sc[...] = jnp.zeros_like(cnt_sc)

    oh = jnp.where((lane == i1) | (lane == i2), 1.0, 0.0)
    before = cnt_sc[0:1] + _mm(tri_ref[...], oh)
    r1 = jnp.sum(jnp.where(lane == i1, before, 0.0), axis=-1, keepdims=True)
    r2 = jnp.sum(jnp.where(lane == i2, before, 0.0), axis=-1, keepdims=True)
    total = cnt_sc[0:1] + jnp.sum(oh, axis=0, keepdims=True)
    cnt_sc[...] = jnp.broadcast_to(total, cnt_sc.shape)
    cnt_ref[...] = jnp.broadcast_to(total, cnt_ref.shape)
    vals = (i1.astype(F32), i2.astype(F32), g1, g2, r1, r2)
    route = jnp.zeros(logits.shape, F32)
    for k, v in enumerate(vals):
        route = jnp.where(lane == k, v, route)
    route_ref[...] = route


def mix_out_router(ym, y_nsa, x2d, lw):
    T, D = x2d.shape
    tm = min(256, T)
    wr = jnp.concatenate([lw['router_expert_w'], lw['router_group_w'],
                          jnp.zeros((D, ROUTE_W - N_EXPERTS - N_GROUPS), F32)], axis=1).astype(MXU_DTYPE)
    br = jnp.concatenate([lw['router_expert_b'], lw['router_group_b'],
                          jnp.zeros((ROUTE_W - N_EXPERTS - N_GROUPS,), F32)]).reshape(1, ROUTE_W)
    row = lambda i: (i, 0)
    fixed = lambda i: (0, 0)
    tri = jnp.asarray(np.tril(np.ones((tm, tm), np.float32), -1), MXU_DTYPE)
    return pl.pallas_call(
        _mix_out_router_body,
        grid=(T // tm,),
        in_specs=[pl.BlockSpec((tm, YM_W), row), pl.BlockSpec((tm, GROUP_W), row), pl.BlockSpec((tm, D), row),
                  pl.BlockSpec((1, MIX_W), fixed),
                  pl.BlockSpec((MIX_W, D), fixed), pl.BlockSpec((1, D), fixed), pl.BlockSpec((D, ROUTE_W), fixed),
                  pl.BlockSpec((1, ROUTE_W), fixed), pl.BlockSpec((tm, tm), fixed)],
        out_specs=[pl.BlockSpec((tm, D), row), pl.BlockSpec((tm, D), row), pl.BlockSpec((tm, ROUTE_W), row),
                   pl.BlockSpec((8, ROUTE_W), fixed)],
        out_shape=[jax.ShapeDtypeStruct((T, D), F32), jax.ShapeDtypeStruct((T, D), F32),
                   jax.ShapeDtypeStruct((T, ROUTE_W), F32), jax.ShapeDtypeStruct((8, ROUTE_W), F32)],
        scratch_shapes=[pltpu.VMEM((8, ROUTE_W), F32)],
        compiler_params=_cparams(),
        name="mix_out_router",
    )(ym, y_nsa, x2d, lw['mix_out_g'].reshape(1, MIX_W), lw['w_out'].astype(MXU_DTYPE),
      lw['norm_ffn_g'].reshape(1, D), wr, br, tri)


def moe_schedule(route, counts, tile):
    T = route.shape[0]
    M = T * TOP_E
    fe = route[:, 0:TOP_E].astype(jnp.int32).reshape(M)
    rank = route[:, 4:4 + TOP_E].astype(jnp.int32).reshape(M)
    counts = counts.astype(jnp.int32)
    padded = (counts + tile - 1) // tile * tile
    pad_end = jnp.cumsum(padded)
    dest = ((pad_end - padded)[fe] + rank).astype(jnp.int32)
    n_blk = -(-M // tile) + N_EXPERTS
    tok = jnp.arange(M, dtype=jnp.int32) // TOP_E
    buf_tok = jnp.zeros((n_blk * tile,), jnp.int32).at[dest].set(tok)
    blk_exp = jnp.minimum(jnp.sum(pad_end[None, :] <= (jnp.arange(n_blk, dtype=jnp.int32) * tile)[:, None], axis=1),
                          N_EXPERTS - 1).astype(jnp.int32)
    n_used = (pad_end[-1:] // tile).astype(jnp.int32)
    return buf_tok, blk_exp, n_used, dest


def _moe_ffn_body(tile, tok_ref, bexp_ref, nused_ref, x_hbm, wgu_ref, wdn_ref, y_ref, xg, sem, wgu_bf, wdn_bf):
    j = pl.program_id(0)
    n = nused_ref[0]

    def gather(blk, slot):
        def body(r, c):
            t = tok_ref[blk * tile + r]
            pltpu.make_async_copy(x_hbm.at[pl.ds(t, 1)], xg.at[slot, pl.ds(r, 1)], sem.at[slot]).start()
            return c
        lax.fori_loop(0, tile, body, 0, unroll=8)

    @pl.when((j == 0) & (n > 0))
    def _():
        gather(0, 0)

    def wait_block(slot):
        pltpu.make_async_copy(x_hbm.at[pl.ds(0, tile)], xg.at[slot], sem.at[slot]).wait()

    @pl.when(j < n)
    def _():
        slot = j % 2
        nxt = jnp.minimum(j + 1, n - 1)

        def fetch_group(g):
            per = tile // FETCH_GROUPS
            for r in range(g * per, (g + 1) * per):
                t = tok_ref[nxt * tile + r]
                pltpu.make_async_copy(x_hbm.at[pl.ds(t, 1)], xg.at[1 - slot, pl.ds(r, 1)], sem.at[1 - slot]).start()

        @pl.when((j == 0) | (bexp_ref[j] != bexp_ref[jnp.maximum(j - 1, 0)]))
        def _():
            wgu_bf[...] = wgu_ref[0].astype(wgu_bf.dtype)
            wdn_bf[...] = wdn_ref[0].astype(wdn_bf.dtype)

        wait_block(slot)
        x = xg[slot]
        half = FETCH_GROUPS // 2
        cg, cd = 2 * D_EXPERT // half, y_ref.shape[1] // half
        hs = []
        for c in range(half):
            fetch_group(c)
            hs.append(_mm(x, wgu_bf[:, c * cg:(c + 1) * cg]))
        h = jnp.concatenate(hs, axis=1)
        a, b = h[:, :D_EXPERT], h[:, D_EXPERT:]
        act = a * jax.nn.sigmoid(a) * b
        for c in range(half):
            fetch_group(half + c)
            y_ref[:, c * cd:(c + 1) * cd] = _mm(act, wdn_bf[:, c * cd:(c + 1) * cd])

        @pl.when(j + 1 >= n)
        def _():
            wait_block(1 - slot)

    @pl.when(j >= n)
    def _():
        y_ref[...] = jnp.zeros_like(y_ref)


def moe_ffn_pallas(xn, buf_tok, blk_exp, n_used, w_gu, w_down, tile):
    T, D = xn.shape
    n_blk = blk_exp.shape[0]
    return pl.pallas_call(
        functools.partial(_moe_ffn_body, tile),
        grid_spec=pltpu.PrefetchScalarGridSpec(
            num_scalar_prefetch=3,
            grid=(n_blk,),
            in_specs=[pl.BlockSpec(memory_space=pl.ANY),
                      pl.BlockSpec((1, D, 2 * D_EXPERT), lambda j, tok, bexp, nu: (bexp[j], 0, 0)),
                      pl.BlockSpec((1, D_EXPERT, D), lambda j, tok, bexp, nu: (bexp[j], 0, 0))],
            out_specs=pl.BlockSpec((tile, D), lambda j, tok, bexp, nu: (j, 0)),
            scratch_shapes=[pltpu.VMEM((2, tile, D), F32), pltpu.SemaphoreType.DMA((2,)),
                            pltpu.VMEM((D, 2 * D_EXPERT), MXU_DTYPE), pltpu.VMEM((D_EXPERT, D), MXU_DTYPE)]),
        out_shape=jax.ShapeDtypeStruct((n_blk * tile, D), F32),
        compiler_params=_cparams(),
        name="moe_ffn",
    )(buf_tok, blk_exp, n_used, xn, w_gu, w_down)


def _moe_combine_body(tm, slots_ref, y_hbm, x2_ref, route_ref, o_ref, yb, sem):
    i = pl.program_id(0)
    nt = pl.num_programs(0)

    def gather(tile_i, buf):
        def body(r, c):
            for k in range(TOP_E):
                s = slots_ref[(tile_i * tm + r) * TOP_E + k]
                pltpu.make_async_copy(y_hbm.at[pl.ds(s, 1)], yb.at[buf, k, pl.ds(r, 1)], sem.at[buf]).start()
            return c
        lax.fori_loop(0, tm, body, 0, unroll=8)

    @pl.when(i == 0)
    def _():
        gather(0, 0)

    buf = i % 2

    @pl.when(i + 1 < nt)
    def _():
        gather(i + 1, 1 - buf)

    for k in range(TOP_E):
        pltpu.make_async_copy(y_hbm.at[pl.ds(0, tm)], yb.at[buf, k], sem.at[buf]).wait()
    r = route_ref[...]
    o_ref[...] = x2_ref[...] + (r[:, 2:3] * yb[buf, 0] + r[:, 3:4] * yb[buf, 1])


def moe_combine_pallas(y, slots, x2, route):
    T, D = x2.shape
    tm = min(COMBINE_TILE, T)
    return pl.pallas_call(
        functools.partial(_moe_combine_body, tm),
        grid_spec=pltpu.PrefetchScalarGridSpec(
            num_scalar_prefetch=1,
            grid=(T // tm,),
            in_specs=[pl.BlockSpec(memory_space=pl.ANY),
                      pl.BlockSpec((tm, D), lambda i, s: (i, 0)),
                      pl.BlockSpec((tm, ROUTE_W), lambda i, s: (i, 0))],
            out_specs=pl.BlockSpec((tm, D), lambda i, s: (i, 0)),
            scratch_shapes=[pltpu.VMEM((2, TOP_E, tm, D), F32), pltpu.SemaphoreType.DMA((2,))]),
        out_shape=jax.ShapeDtypeStruct((T, D), F32),
        compiler_params=_cparams(),
        name="moe_combine",
    )(slots, y, x2, route)


def mix_out_moe(ym, y_nsa, x2d, lw, tile):
    x2, xn, route, counts = mix_out_router(ym, y_nsa, x2d, lw)
    buf_tok, blk_exp, n_used, slots = moe_schedule(route, counts[0, :N_EXPERTS], tile)
    y = moe_ffn_pallas(xn, buf_tok, blk_exp + lw['expert_base'], n_used, lw['exp_w_gu'], lw['exp_w_down'], tile)
    return moe_combine_pallas(y, slots, x2, route)


def rmsnorm(x, g):
    xf = x.astype(jnp.float32)
    y = xf * lax.rsqrt(jnp.mean(xf * xf, axis=-1, keepdims=True) + EPS)
    return (y * g.astype(jnp.float32)).astype(x.dtype)


def split_cols(a, sizes):
    outs, o = [], 0
    for s in sizes:
        outs.append(a[..., o:o + s])
        o += s
    return outs


def causal_dwconv(u, prev, w, b):
    L = u.shape[1]
    ext = jnp.concatenate([prev.astype(u.dtype), u], axis=1)
    y = lax.conv_general_dilated(ext, w[:, None, :].astype(u.dtype), window_strides=(1,), padding='VALID',
                                 dimension_numbers=('NWC', 'WIO', 'NWC'), feature_group_count=u.shape[-1])
    return y + b.astype(u.dtype), ext[:, L:]


def pool_mixer(u, prev, pos0, w, scale):
    B_, L, C = u.shape
    ext = jnp.concatenate([prev.astype(u.dtype), u], axis=1)
    ef = ext.astype(jnp.float32)
    cs = jnp.concatenate([jnp.zeros((B_, 1, C), jnp.float32), jnp.cumsum(ef, axis=1)], axis=1)
    pos = pos0 + jnp.arange(L)
    means = []
    for g, win in enumerate(POOL_WINDOWS):
        sl = slice(g * POOL_GROUP, (g + 1) * POOL_GROUP)
        tot = cs[:, POOL_KEEP + 1:POOL_KEEP + 1 + L, sl] - cs[:, POOL_KEEP + 1 - win:POOL_KEEP + 1 - win + L, sl]
        cnt = jnp.minimum(win, pos + 1).astype(jnp.float32)
        means.append(tot / cnt[None, :, None])
    d = (jnp.concatenate(means, axis=-1) - ef[:, POOL_KEEP:]).astype(u.dtype)
    y = jnp.einsum('blgc,gcd->blgd', d.reshape(B_, L, len(POOL_WINDOWS), POOL_GROUP), w).reshape(B_, L, C)
    return y * scale, ext[:, L:]


def rglru_mixer(xb, gb, conv_prev, h0, conv_w, conv_b, w_a, b_a, w_x, b_x, lam):
    B_, L, C = xb.shape
    xc, conv_new = causal_dwconv(xb, conv_prev, conv_w, conv_b)
    xh = xc.reshape(B_, L, RG_HEADS, RG_BLOCK)
    r = jax.nn.sigmoid(jnp.einsum('blhi,hij->blhj', xh, w_a).reshape(B_, L, C) + b_a)
    ig = jax.nn.sigmoid(jnp.einsum('blhi,hij->blhj', xh, w_x).reshape(B_, L, C) + b_x)
    log_a = -RG_C * r.astype(jnp.float32) * jax.nn.softplus(-lam.astype(jnp.float32))
    a = jnp.exp(log_a)
    bt = jnp.sqrt(-jnp.expm1(2.0 * log_a)) * (ig * xc).astype(jnp.float32)
    bt = bt.at[:, 0].add(a[:, 0] * h0.astype(jnp.float32))
    _, h = lax.associative_scan(lambda e1, e2: (e1[0] * e2[0], e2[0] * e1[1] + e2[1]), (a, bt), axis=1)
    y = h.astype(xb.dtype) * jax.nn.gelu(gb)
    return y, conv_new, h[:, -1].astype(xb.dtype)


def masked_softmax(s, mask):
    s = jnp.where(mask, s.astype(jnp.float32), -jnp.inf)
    m = jnp.max(s, axis=-1, keepdims=True)
    e = jnp.exp(s - jnp.where(jnp.isfinite(m), m, 0.0))
    d = jnp.sum(e, axis=-1, keepdims=True)
    return e / jnp.where(d > 0, d, 1.0)


def nsa_compress(k_raw, v_raw, phi, phi_b, g_kc):
    B_, T = k_raw.shape[:2]
    R = CMP_BLOCK // CMP_STRIDE
    nch = T // CMP_STRIDE
    ncmp = nch - (R - 1)

    def comp(a, w, bias):
        ch = a[:, :nch * CMP_STRIDE].reshape(B_, nch, CMP_STRIDE, N_KV, HEAD_DIM)
        ch = ch.transpose(0, 1, 3, 2, 4).reshape(B_, nch, N_KV, CMP_STRIDE * HEAD_DIM)
        wr = w.reshape(R, CMP_STRIDE * HEAD_DIM, HEAD_DIM)
        out = jnp.einsum('bckf,fd->bckd', ch[:, 0:ncmp], wr[0])
        for r in range(1, R):
            out = out + jnp.einsum('bckf,fd->bckd', ch[:, r:r + ncmp], wr[r])
        return out + bias

    kc = rmsnorm(comp(k_raw, phi[0], phi_b[0]), g_kc)
    vc = comp(v_raw, phi[1], phi_b[1])
    cmp_end = jnp.arange(ncmp) * CMP_STRIDE + (CMP_BLOCK - 1)
    return kc, vc, cmp_end


def sel_blocks(a):
    B_, T = a.shape[:2]
    n_sel = -(-T // SEL_BLOCK)
    a = jnp.pad(a, ((0, 0), (0, n_sel * SEL_BLOCK - T), (0, 0), (0, 0)))
    return a.reshape(B_, n_sel, SEL_BLOCK, N_KV, HEAD_DIM).transpose(0, 3, 1, 2, 4)


def nsa_attend(q, q_pos, gates, kc, vc, cmp_end, ks_blk, vs_blk, kw, vw, w_pos):
    dt = q.dtype
    B_, Q = q.shape[:2]
    t = q_pos[:, None]
    s = jnp.einsum('bqkgd,bckd->bqkgc', q, kc)
    p_cmp = masked_softmax(s, (cmp_end[None, :] <= t)[None, :, None, None, :])
    o_cmp = jnp.einsum('bqkgc,bckd->bqkgd', p_cmp.astype(dt), vc)
    n_sel = ks_blk.shape[2]
    ci = jnp.arange(kc.shape[1])[:, None] * CMP_STRIDE
    sj = jnp.arange(n_sel)[None, :] * SEL_BLOCK
    overlap = ((ci < sj + SEL_BLOCK) & (ci + CMP_BLOCK > sj)).astype(jnp.float32)
    imp = jnp.einsum('bqkgc,cs->bqks', p_cmp, overlap)
    blk = jnp.arange(n_sel)[None, :]
    cur = t // SEL_BLOCK
    valid = blk <= cur
    forced = (blk == 0) | (blk == cur) | (blk == cur - 1)
    score = jnp.where(valid[None, :, None, :], imp, -jnp.inf)
    score = jnp.where((forced & valid)[None, :, None, :], jnp.inf, score)
    top_v, top_i = lax.top_k(score, min(SEL_TOPK, n_sel))
    kk = top_i.shape[-1]
    bi = jnp.arange(B_)[:, None, None, None]
    hi = jnp.arange(N_KV)[None, None, :, None]
    ks = ks_blk[bi, hi, top_i].reshape(B_, Q, N_KV, kk * SEL_BLOCK, HEAD_DIM)
    vs = vs_blk[bi, hi, top_i].reshape(B_, Q, N_KV, kk * SEL_BLOCK, HEAD_DIM)
    spos = (top_i[..., None] * SEL_BLOCK + jnp.arange(SEL_BLOCK)).reshape(B_, Q, N_KV, kk * SEL_BLOCK)
    smask = (spos <= q_pos[None, :, None, None]) & jnp.repeat(top_v > -jnp.inf, SEL_BLOCK, axis=-1)
    s = jnp.einsum('bqkgd,bqknd->bqkgn', q, ks)
    o_sel = jnp.einsum('bqkgn,bqknd->bqkgd', masked_softmax(s, smask[:, :, :, None, :]).astype(dt), vs)
    wd = t - w_pos[None, :]
    wmask = (w_pos[None, :] >= 0) & (wd >= 0) & (wd <= WINDOW)
    s = jnp.einsum('bqkgd,bnkd->bqkgn', q, kw)
    o_win = jnp.einsum('bqkgn,bnkd->bqkgd', masked_softmax(s, wmask[None, :, None, None, :]).astype(dt), vw)
    return gates[..., 0:1] * o_cmp + gates[..., 1:2] * o_sel + gates[..., 2:3] * o_win


def nsa_prompt(q, gates, kc_raw, vc_raw, ksel, vsel, kwin, vwin, phi, phi_b, g_kc):
    B_, S = q.shape[:2]
    kc, vc, cmp_end = nsa_compress(kc_raw, vc_raw, phi, phi_b, g_kc)
    ks_blk, vs_blk = sel_blocks(ksel), sel_blocks(vsel)
    zpad = jnp.zeros((B_, WINDOW, N_KV, HEAD_DIM), kwin.dtype)
    kw_pad = jnp.concatenate([zpad, kwin], axis=1)
    vw_pad = jnp.concatenate([zpad, vwin], axis=1)
    nq = S // Q_BLOCK

    def body(args):
        qc, gc, i = args
        start = i * Q_BLOCK
        kw = lax.dynamic_slice_in_dim(kw_pad, start, WINDOW + Q_BLOCK, axis=1)
        vw = lax.dynamic_slice_in_dim(vw_pad, start, WINDOW + Q_BLOCK, axis=1)
        return nsa_attend(qc, start + jnp.arange(Q_BLOCK), gc, kc, vc, cmp_end, ks_blk, vs_blk,
                          kw, vw, start - WINDOW + jnp.arange(WINDOW + Q_BLOCK))

    qb = q.reshape(B_, nq, Q_BLOCK, N_KV, GQA, HEAD_DIM).swapaxes(0, 1)
    gb = gates.reshape(B_, nq, Q_BLOCK, N_KV, GQA, 3).swapaxes(0, 1)
    o = lax.map(body, (qb, gb, jnp.arange(nq)))
    o = o.swapaxes(0, 1).reshape(B_, S, N_HEADS * HEAD_DIM)
    rows = jnp.stack([kc_raw, vc_raw, ksel, vsel], axis=2)
    win_new = jnp.stack([kwin, vwin], axis=2)[:, S - min(WINDOW, S):]
    return o, rows, win_new


def nsa_sample(pool, page_table, win_buf, q, gates, kc_raw, vc_raw, ksel, vsel, kwin, vwin, phi, phi_b, g_kc):
    B_, L = q.shape[:2]
    past = pool[page_table]
    past = past.reshape(B_, past.shape[1] * past.shape[2], 4, N_KV, HEAD_DIM)
    P = past.shape[1]
    rows = jnp.stack([kc_raw, vc_raw, ksel, vsel], axis=2)
    full = jnp.concatenate([past.astype(rows.dtype), rows], axis=1)
    kc, vc, cmp_end = nsa_compress(full[:, :, 0], full[:, :, 1], phi, phi_b, g_kc)
    ks_blk, vs_blk = sel_blocks(full[:, :, 2]), sel_blocks(full[:, :, 3])
    Lw = win_buf.shape[1]
    new_w = jnp.stack([kwin, vwin], axis=2)
    wfull = jnp.concatenate([win_buf.astype(new_w.dtype), new_w], axis=1)
    o = nsa_attend(q, P + jnp.arange(L), gates, kc, vc, cmp_end, ks_blk, vs_blk,
                   wfull[:, :, 0], wfull[:, :, 1], P - Lw + jnp.arange(Lw + L))
    return o.reshape(B_, L, N_HEADS * HEAD_DIM), rows, wfull[:, L:]


def expert_dispatch(xt, eidx, gate, w_gu, w_down):
    T, D = xt.shape
    M = T * TOP_E
    fe = eidx.reshape(M)
    ftok = jnp.arange(M, dtype=jnp.int32) // TOP_E
    fgate = gate.reshape(M)
    order = jnp.argsort(fe)
    se, stok, sgate = fe[order], ftok[order], fgate[order]
    counts = jnp.bincount(fe, length=N_EXPERTS)
    padded = (counts + MOE_BLOCK - 1) // MOE_BLOCK * MOE_BLOCK
    pad_end = jnp.cumsum(padded)
    pad_start = pad_end - padded
    start = jnp.cumsum(counts) - counts
    dest = pad_start[se] + jnp.arange(M) - start[se]
    n_blk = -(-M // MOE_BLOCK) + N_EXPERTS
    P = n_blk * MOE_BLOCK
    buf_tok = jnp.zeros((P,), jnp.int32).at[dest].set(stok)
    buf_gate = jnp.zeros((P,), fgate.dtype).at[dest].set(sgate)
    blk_exp = jnp.minimum(jnp.searchsorted(pad_end, jnp.arange(n_blk) * MOE_BLOCK, side='right'), N_EXPERTS - 1)
    xb = xt[buf_tok].reshape(n_blk, MOE_BLOCK, D)

    def run(args):
        xi, e = args
        a, b = jnp.split(xi @ w_gu[e], 2, axis=-1)
        return (jax.nn.silu(a) * b) @ w_down[e]

    yb = lax.map(run, (xb, blk_exp)).reshape(P, D)
    return jax.ops.segment_sum(yb * buf_gate[:, None].astype(yb.dtype), buf_tok, num_segments=T)


def moe_ffn(x, wg_r, bg_r, we_r, be_r, w_gu, w_down):
    B_, L, D = x.shape
    xt = x.reshape(B_ * L, D)
    T = xt.shape[0]
    lg = (xt @ wg_r + bg_r).astype(jnp.float32)
    pg = jax.nn.softmax(lg, axis=-1)
    gsel = jnp.argmax(lg, axis=-1)
    p_group = jnp.take_along_axis(pg, gsel[:, None], axis=-1)
    le = (xt @ we_r + be_r).astype(jnp.float32).reshape(T, N_GROUPS, EXP_PER_GROUP)
    le_g = jnp.take_along_axis(le, gsel[:, None, None], axis=1)[:, 0]
    tv, ti = lax.top_k(le_g, TOP_E)
    gate = p_group * jax.nn.softmax(tv, axis=-1)
    eidx = gsel[:, None] * EXP_PER_GROUP + ti
    return expert_dispatch(xt, eidx, gate, w_gu, w_down).reshape(B_, L, D)


def layer_forward(x, pos0, lw, pool_prev, rgc_prev, rgh0, sc_prev, nsa_fn):
    B_, L, _ = x.shape
    w_perm = permute_w_in(lw['w_in']).astype(MXU_DTYPE)
    proj2d = norm_matmul(x.reshape(B_ * L, D_MODEL), lw['norm_mix_g'], w_perm)
    if pool_prev is None:
        ym, tails, hlast = mixers_prompt(proj2d, lw, B_, L)
        pool_new = tails[:, 0, HALO - POOL_KEEP:]
        rgc_new = tails[:, 1, HALO - (RG_CONV - 1):]
        sc_new = tails[:, 2, HALO - (SC_CONV - 1):]
        rgh_new = hlast[:, 0]
    else:
        ym, pool_new, rgc_new, rgh_new, sc_new = mixers_sample(proj2d, lw, pos0, pool_prev, rgc_prev, rgh0, sc_prev)
    y_nsa, nsa_rows, win_new = nsa_fn(proj2d, lw['nsa_phi'], lw['nsa_phi_b'], lw['nsa_qk_g'])
    x = mix_out_moe(ym, y_nsa.reshape(B_ * L, GROUP_W), x.reshape(B_ * L, D_MODEL), lw,
                    MOE_TILE_PROMPT if L > 1 else MOE_TILE_SAMPLE)
    return x.reshape(B_, L, D_MODEL), (nsa_rows, win_new, pool_new, rgc_new, rgh_new, sc_new)


def kernel(x_prompt, x_sample, cache_nsa, state_win_kv, state_pool, state_rg_conv, state_rg_h, state_sc_conv,
           page_table, norm_mix_g, w_in, pool_w, pool_scale, rg_conv_w, rg_conv_b, rg_w_a, rg_b_a, rg_w_x, rg_b_x,
           rg_lambda, nsa_phi, nsa_phi_b, nsa_qk_g, sc_conv_w, sc_conv_b, mix_out_g, w_out, norm_ffn_g,
           router_group_w, router_group_b, router_expert_w, router_expert_b, exp_w_gu, exp_w_down):
    past_len = page_table.shape[1] * cache_nsa.shape[2]
    xp, xs = x_prompt, x_sample
    cache3 = feature_major_pages(cache_nsa)
    win3 = state_win_kv.transpose(0, 1, 3, 4, 5, 2).reshape(DEPTH * state_win_kv.shape[1], 2, N_KV * HEAD_DIM,
                                                             state_win_kv.shape[2])
    cache_ab = cache_compress(cache3, nsa_phi)
    Bp = xp.shape[0]
    st_p, st_s = [], []
    for l in range(DEPTH):
        lw = dict(norm_mix_g=norm_mix_g[l], w_in=w_in[l], pool_w=pool_w[l], pool_scale=pool_scale[l],
                  rg_conv_w=rg_conv_w[l], rg_conv_b=rg_conv_b[l], rg_w_a=rg_w_a[l], rg_b_a=rg_b_a[l],
                  rg_w_x=rg_w_x[l], rg_b_x=rg_b_x[l], rg_lambda=rg_lambda[l], nsa_phi=nsa_phi[l],
                  nsa_phi_b=nsa_phi_b[l], nsa_qk_g=nsa_qk_g[l], sc_conv_w=sc_conv_w[l], sc_conv_b=sc_conv_b[l],
                  mix_out_g=mix_out_g[l], w_out=w_out[l], norm_ffn_g=norm_ffn_g[l],
                  router_group_w=router_group_w[l], router_group_b=router_group_b[l],
                  router_expert_w=router_expert_w[l], router_expert_b=router_expert_b[l],
                  exp_w_gu=exp_w_gu.reshape((DEPTH * N_EXPERTS,) + exp_w_gu.shape[2:]),
                  exp_w_down=exp_w_down.reshape((DEPTH * N_EXPERTS,) + exp_w_down.shape[2:]),
                  expert_base=l * N_EXPERTS)
        xp, sp = layer_forward(xp, 0, lw, None, None, None, None,
                               lambda p, phi, phi_b, g: nsa_prompt_pallas(p, Bp, xp.shape[1], phi, phi_b, g))
        xs, ss = layer_forward(xs, past_len, lw, state_pool[l], state_rg_conv[l], state_rg_h[l], state_sc_conv[l],
                               lambda p, phi, phi_b, g: nsa_sample_pallas(p, l, cache3, cache_ab, page_table, win3,
                                                                          phi_b, g))
        st_p.append(sp)
        st_s.append(ss)

    def stk(lst, i):
        return jnp.stack([s[i] for s in lst])

    return (xp, xs, stk(st_p, 0), stk(st_s, 0), stk(st_p, 1), stk(st_s, 1), stk(st_p, 2), stk(st_s, 2),
            stk(st_p, 3), stk(st_s, 3), stk(st_p, 4), stk(st_s, 4), stk(st_p, 5), stk(st_s, 5))
```

```python
import functools
import jax, jax.numpy as jnp
from jax import lax
import numpy as np
from jax.experimental import pallas as pl
from jax.experimental.pallas import tpu as pltpu

D_MODEL = 1024
BATCH = 4
SEQ = 4096
DEPTH = 2
DEC_BATCH = 128
DEC_SEQ = 1
PAST_LEN = 2048
PAGE_SIZE = 128

MIX_W = D_MODEL
GROUP_W = MIX_W // 4
POOL_W = GROUP_W
POOL_WINDOWS = (2, 4, 8, 16)
POOL_GROUP = POOL_W // len(POOL_WINDOWS)
POOL_KEEP = max(POOL_WINDOWS) - 1
RG_W = GROUP_W
RG_HEADS = 4
RG_BLOCK = RG_W // RG_HEADS
RG_CONV = 4
RG_C = 8.0
HEAD_DIM = 64
N_HEADS = GROUP_W // HEAD_DIM
N_KV = 2
GQA = N_HEADS // N_KV
CMP_BLOCK = 32
CMP_STRIDE = 16
SEL_BLOCK = 64
SEL_TOPK = 16
WINDOW = 512
Q_BLOCK = 128
SC_W = GROUP_W
SC_CONV = 3
N_GROUPS = 4
EXP_PER_GROUP = 8
N_EXPERTS = N_GROUPS * EXP_PER_GROUP
TOP_E = 2
D_EXPERT = 512
MOE_BLOCK = 128
EPS = 1e-6
SPLIT_SIZES = (POOL_W, RG_W, RG_W, N_HEADS * HEAD_DIM, 6 * N_KV * HEAD_DIM, 3 * N_HEADS, 3 * SC_W)
N_IN = sum(SPLIT_SIZES)

LANE = 128
ROW_TILE = 512
VMEM_LIMIT = 48 * 1024 * 1024
MXU_DTYPE = jnp.bfloat16
F32 = jnp.float32
NEG = -1e30

KV_W = 6 * N_KV * HEAD_DIM
COL_Q = 0
COL_KV = COL_Q + N_HEADS * HEAD_DIM
COL_POOL = COL_KV + KV_W
COL_RX = COL_POOL + POOL_W
COL_RGATE = COL_RX + RG_W
COL_SC = COL_RGATE + RG_W
COL_NG = COL_SC + 3 * SC_W
N_IN_PAD = COL_NG + LANE
SEL_TILE = 512
N_SEL_PROMPT = SEQ // SEL_BLOCK


def _cparams(n_axes=1):
    return pltpu.CompilerParams(dimension_semantics=("arbitrary",) * n_axes, vmem_limit_bytes=VMEM_LIMIT)


def _mm(a, b):
    return jnp.dot(a.astype(MXU_DTYPE), b.astype(MXU_DTYPE), preferred_element_type=F32)


def _mm_nt(a, b):
    return lax.dot_general(a.astype(MXU_DTYPE), b.astype(MXU_DTYPE), (((1,), (1,)), ((), ())),
                           preferred_element_type=F32)


def permute_w_in(w):
    pu, rx, rgate, q, kv, ng, sc = split_cols(w, SPLIT_SIZES)
    pad = jnp.zeros((w.shape[0], LANE - ng.shape[1]), w.dtype)
    return jnp.concatenate([q, kv, pu, rx, rgate, sc, ng, pad], axis=1)


def _norm_matmul_body(x_ref, g_ref, w_ref, o_ref):
    xf = x_ref[...]
    h = xf * lax.rsqrt(jnp.mean(xf * xf, axis=-1, keepdims=True) + EPS) * g_ref[...]
    o_ref[...] = _mm(h, w_ref[...])


def norm_matmul(x2d, g, w):
    T, D = x2d.shape
    N = w.shape[1]
    tm = min(ROW_TILE, T)
    return pl.pallas_call(
        _norm_matmul_body,
        grid=(T // tm,),
        in_specs=[pl.BlockSpec((tm, D), lambda i: (i, 0)),
                  pl.BlockSpec((1, D), lambda i: (0, 0)),
                  pl.BlockSpec((D, N), lambda i: (0, 0))],
        out_specs=pl.BlockSpec((tm, N), lambda i: (i, 0)),
        out_shape=jax.ShapeDtypeStruct((T, N), F32),
        compiler_params=_cparams(),
        name="norm_in_proj",
    )(x2d, g.reshape(1, D), w)


def _seg_rmsnorm(x, g):
    x2 = x * x
    left = lax.broadcasted_iota(jnp.int32, x.shape, 1) < HEAD_DIM
    s_l = jnp.sum(jnp.where(left, x2, 0.0), axis=-1, keepdims=True)
    s_r = jnp.sum(jnp.where(left, 0.0, x2), axis=-1, keepdims=True)
    ms = jnp.where(left, s_l, s_r) * (1.0 / HEAD_DIM)
    return x * lax.rsqrt(ms + EPS) * g


def _nsa_prep_body(qkv_ref, ng_ref, g_ref, perm_ref, qa_ref, kvb_ref, rawb_ref, rows_t_ref, win_t_ref, win_ref,
                   gates_ref):
    g = g_ref[...]
    for hb in range(N_KV):
        qn = _seg_rmsnorm(qkv_ref[:, COL_Q + hb * LANE:COL_Q + (hb + 1) * LANE], g[0:1]) * (HEAD_DIM ** -0.5)
        qa_ref[:, hb * 2 * LANE:(hb + 1) * 2 * LANE] = _mm(qn, perm_ref[hb]).astype(qa_ref.dtype)
    comp = [qkv_ref[:, COL_KV + c * LANE:COL_KV + (c + 1) * LANE] for c in range(6)]
    comp[2] = _seg_rmsnorm(comp[2], g[2:3])
    comp[4] = _seg_rmsnorm(comp[4], g[3:4])
    for c in range(6):
        kvb_ref[:, c * LANE:(c + 1) * LANE] = comp[c].astype(kvb_ref.dtype)
    for c in range(2):
        rawb_ref[:, c * LANE:(c + 1) * LANE] = comp[c].astype(rawb_ref.dtype)
    for c in range(4):
        rows_t_ref[0, c * LANE:(c + 1) * LANE, :] = comp[c].T
    for c in range(2):
        win_t_ref[0, c * LANE:(c + 1) * LANE, :] = comp[4 + c].T
        win_ref[:, c * LANE:(c + 1) * LANE] = comp[4 + c]
    gates_ref[...] = jax.nn.sigmoid(ng_ref[...])


def _q_place_matrices():
    p = np.zeros((N_KV, LANE, 2 * LANE), np.float32)
    for hb in range(N_KV):
        for gq in range(GQA):
            for d in range(HEAD_DIM):
                p[hb, gq * HEAD_DIM + d, gq * LANE + hb * HEAD_DIM + d] = 1.0
    return jnp.asarray(p, MXU_DTYPE)


def nsa_prep(proj, qk_g, B_, S):
    T = proj.shape[0]
    tm = min(ROW_TILE, S)
    tpb = S // tm
    qkv_w = COL_POOL
    g4 = jnp.tile(qk_g, (1, 2))
    return pl.pallas_call(
        _nsa_prep_body,
        grid=(T // tm,),
        in_specs=[pl.BlockSpec((tm, qkv_w), lambda i: (i, 0)),
                  pl.BlockSpec((tm, LANE), lambda i: (i, COL_NG // LANE)),
                  pl.BlockSpec((4, LANE), lambda i: (0, 0)),
                  pl.BlockSpec((N_KV, LANE, 2 * LANE), lambda i: (0, 0, 0))],
        out_specs=[pl.BlockSpec((tm, 4 * LANE), lambda i: (i, 0)),
                   pl.BlockSpec((tm, 6 * LANE), lambda i: (i, 0)),
                   pl.BlockSpec((tm, 2 * LANE), lambda i: (i, 0)),
                   pl.BlockSpec((1, 4 * LANE, tm), lambda i: (i // tpb, 0, i % tpb)),
                   pl.BlockSpec((1, 2 * LANE, tm), lambda i: (i // tpb, 0, i % tpb)),
                   pl.BlockSpec((tm, 2 * LANE), lambda i: (i, 0)),
                   pl.BlockSpec((tm, LANE), lambda i: (i, 0))],
        out_shape=[jax.ShapeDtypeStruct((T, 4 * LANE), MXU_DTYPE),
                   jax.ShapeDtypeStruct((T, 6 * LANE), MXU_DTYPE),
                   jax.ShapeDtypeStruct((T, 2 * LANE), MXU_DTYPE),
                   jax.ShapeDtypeStruct((B_, 4 * LANE, S), F32),
                   jax.ShapeDtypeStruct((B_, 2 * LANE, S), F32),
                   jax.ShapeDtypeStruct((T, 2 * LANE), F32),
                   jax.ShapeDtypeStruct((T, LANE), F32)],
        compiler_params=_cparams(),
        name="nsa_prep",
    )(proj, proj, g4, _q_place_matrices())


def compress_weights(phi):
    R = CMP_BLOCK // CMP_STRIDE
    wr = phi.reshape(2, R, CMP_STRIDE, HEAD_DIM, HEAD_DIM)
    eye = jnp.eye(2, dtype=phi.dtype)
    w = jnp.einsum('crjde,cx,hy->rjchdxye', wr, eye, eye)
    return w.reshape(R, CMP_STRIDE * 2 * LANE, 2 * LANE).astype(MXU_DTYPE)


def _compress_body(x_ref, w_ref, b_ref, g_ref, kc_ref, vc_ref):
    x = x_ref[0]
    nch = x.shape[0]
    a = _mm(x, w_ref[0])
    bm = _mm(x, w_ref[1])
    out = a + pltpu.roll(bm, nch - 1, 0) + b_ref[...]
    kc_ref[0] = _seg_rmsnorm(out[:, 0:LANE], g_ref[...]).astype(kc_ref.dtype)
    vc_ref[0] = out[:, LANE:2 * LANE].astype(vc_ref.dtype)


def nsa_compress_pallas(rawb3, wc, phi_b, g_kc):
    B_, nch, K = rawb3.shape
    bias = jnp.concatenate([jnp.tile(phi_b[0], 2), jnp.tile(phi_b[1], 2)]).reshape(1, 2 * LANE)
    return pl.pallas_call(
        _compress_body,
        grid=(B_,),
        in_specs=[pl.BlockSpec((1, nch, K), lambda b: (b, 0, 0)),
                  pl.BlockSpec(wc.shape, lambda b: (0, 0, 0)),
                  pl.BlockSpec((1, 2 * LANE), lambda b: (0, 0)),
                  pl.BlockSpec((1, LANE), lambda b: (0, 0))],
        out_specs=[pl.BlockSpec((1, nch, LANE), lambda b: (b, 0, 0)),
                   pl.BlockSpec((1, nch, LANE), lambda b: (b, 0, 0))],
        out_shape=[jax.ShapeDtypeStruct((B_, nch, LANE), MXU_DTYPE),
                   jax.ShapeDtypeStruct((B_, nch, LANE), MXU_DTYPE)],
        compiler_params=_cparams(),
        name="nsa_compress",
    )(rawb3, wc, bias, jnp.tile(g_kc, 2).reshape(1, LANE))


def _online_update(carry, s, v):
    m, l, acc = carry
    m_new = jnp.maximum(m, jnp.max(s, axis=-1, keepdims=True))
    alpha = jnp.exp(m - m_new)
    p = jnp.exp(s - m_new)
    l = alpha * l + jnp.sum(p, axis=-1, keepdims=True)
    acc = alpha * acc + _mm(p, v)
    return m_new, l, acc


def _select_blocks(imp, start):
    n_sel = N_SEL_PROMPT
    sc_t = imp.T[0:n_sel]
    blk = lax.broadcasted_iota(jnp.int32, sc_t.shape, 0)
    cur = (start + lax.broadcasted_iota(jnp.int32, sc_t.shape, 1)) // SEL_BLOCK
    valid = blk <= cur
    forced = (blk == 0) | (blk == cur) | (blk == cur - 1)
    score = jnp.where(valid, sc_t, -jnp.inf)
    score = jnp.where(forced & valid, jnp.inf, score)
    cnt = jnp.zeros(sc_t.shape, F32)
    for i in range(n_sel):
        ri = score[i:i + 1, :]
        beats = (ri > score) | ((ri == score) & (blk > i))
        cnt = cnt + jnp.where(beats, 1.0, 0.0)
    sel_t = jnp.where((cnt < SEL_TOPK) & (score > -jnp.inf), 1.0, 0.0)
    sel_t = jnp.concatenate([sel_t, jnp.zeros((LANE - n_sel, sc_t.shape[1]), F32)], axis=0)
    return sel_t.T


def _nsa_attn_body(qa_ref, gates_ref, kc_ref, vc_ref, kv_ref, ov_ref, e_ref, o_ref):
    i = pl.program_id(1)
    start = i * Q_BLOCK
    Q = Q_BLOCK
    R = GQA * Q
    t_row = start + lax.broadcasted_iota(jnp.int32, (R, 1), 0) % Q
    gates = gates_ref[...]
    lane_q = lax.broadcasted_iota(jnp.int32, (Q, LANE), 1)
    heads = range(N_KV)
    qs = [jnp.concatenate([qa_ref[:, (h * GQA + gq) * LANE:(h * GQA + gq + 1) * LANE] for gq in range(GQA)], axis=0)
          for h in heads]

    o_cmps, sel_bias = [], []
    kc = kc_ref[0]
    ncmp = kc.shape[0]
    cmp_end = lax.broadcasted_iota(jnp.int32, (R, ncmp), 1) * CMP_STRIDE + (CMP_BLOCK - 1)
    for h in heads:
        s = jnp.where(cmp_end <= t_row, _mm_nt(qs[h], kc), -jnp.inf)
        m = jnp.max(s, axis=-1, keepdims=True)
        e = jnp.exp(s - jnp.where(m > -jnp.inf, m, 0.0))
        d = jnp.sum(e, axis=-1, keepdims=True)
        p_cmp = e / jnp.where(d > 0, d, 1.0)
        o_cmps.append(_mm(p_cmp, vc_ref[0]))
        imp = _mm(p_cmp[0:Q], ov_ref[...]) + _mm(p_cmp[Q:R], ov_ref[...])
        sel = _select_blocks(imp, start)
        sel_bias.append(jnp.concatenate([jnp.where(sel > 0.5, 0.0, NEG)] * GQA, axis=0).astype(MXU_DTYPE))

    def sel_scores(j):
        off = pl.multiple_of(j * SEL_TILE, SEL_TILE)
        k = kv_ref[pl.ds(off, SEL_TILE), 2 * LANE:3 * LANE]
        v = kv_ref[pl.ds(off, SEL_TILE), 3 * LANE:4 * LANE]
        return off, v, [_mm_nt(qs[h], k) + _mm(sel_bias[h], e_ref[j]) for h in heads]

    def sel_step(j, carry):
        _, v, ss = sel_scores(j)
        return tuple(_online_update(carry[h], ss[h], v) for h in heads)

    init = (jnp.full((R, 1), NEG, F32), jnp.zeros((R, 1), F32), jnp.zeros((R, LANE), F32))
    n_tiles = (start + Q + SEL_TILE - 1) // SEL_TILE
    carry = lax.fori_loop(0, n_tiles - 1, sel_step, (init,) * N_KV)
    off, v, ss = sel_scores(n_tiles - 1)
    causal = off + lax.broadcasted_iota(jnp.int32, (R, SEL_TILE), 1) <= t_row
    o_sels = []
    for h in heads:
        _, l_s, acc_s = _online_update(carry[h], jnp.where(causal, ss[h], NEG), v)
        o_sels.append(acc_s / l_s)

    n_w = WINDOW // Q + 1
    offs = [pl.multiple_of(jnp.maximum(i - kk, 0) * Q, Q) for kk in range(n_w)]
    kw = jnp.concatenate([kv_ref[pl.ds(o, Q), 4 * LANE:5 * LANE] for o in offs], axis=0)
    vw = jnp.concatenate([kv_ref[pl.ds(o, Q), 5 * LANE:6 * LANE] for o in offs], axis=0)
    lane_w = lax.broadcasted_iota(jnp.int32, (1, n_w * Q), 1)
    w_pos = (i - lane_w // Q) * Q + lane_w % Q
    wd = t_row - w_pos
    wmask = (w_pos >= 0) & (wd >= 0) & (wd <= WINDOW)
    o_wins = []
    for h in heads:
        s = jnp.where(wmask, _mm_nt(qs[h], kw), NEG)
        p = jnp.exp(s - jnp.max(s, axis=-1, keepdims=True))
        o_wins.append(_mm(p, vw) / jnp.sum(p, axis=-1, keepdims=True))

    for h in heads:
        o_cmp, o_sel, o_win = o_cmps[h], o_sels[h], o_wins[h]
        outs = []
        for gq in range(GQA):
            c0 = (h * GQA + gq) * 3
            rs = slice(gq * Q, (gq + 1) * Q)
            og = (gates[:, c0:c0 + 1] * o_cmp[rs] + gates[:, c0 + 1:c0 + 2] * o_sel[rs]
                  + gates[:, c0 + 2:c0 + 3] * o_win[rs])
            outs.append(og if gq == h else pltpu.roll(og, HEAD_DIM, 1))
        o_ref[:, h * LANE:(h + 1) * LANE] = jnp.where(lane_q < HEAD_DIM, outs[0], outs[1])


def _sel_constants(S):
    ncmp_rows = S // CMP_STRIDE
    ci = np.arange(ncmp_rows)[:, None] * CMP_STRIDE
    sj = np.arange(LANE)[None, :] * SEL_BLOCK
    ov = ((ci < sj + SEL_BLOCK) & (ci + CMP_BLOCK > sj) & (np.arange(LANE)[None, :] < S // SEL_BLOCK))
    n_t = S // SEL_TILE
    key_blk = (np.arange(n_t)[:, None, None] * SEL_TILE + np.arange(SEL_TILE)[None, None, :]) // SEL_BLOCK
    e = (np.arange(LANE)[None, :, None] == key_blk)
    return jnp.asarray(ov, MXU_DTYPE), jnp.asarray(e, MXU_DTYPE)


def nsa_attn_prompt(qa, gates, kc, vc, kvb, B_, S):
    nq = S // Q_BLOCK
    nch = S // CMP_STRIDE
    ov, e3 = _sel_constants(S)
    return pl.pallas_call(
        _nsa_attn_body,
        grid=(B_, nq),
        in_specs=[pl.BlockSpec((Q_BLOCK, 4 * LANE), lambda b, i: (b * nq + i, 0)),
                  pl.BlockSpec((Q_BLOCK, LANE), lambda b, i: (b * nq + i, 0)),
                  pl.BlockSpec((1, nch, LANE), lambda b, i: (b, 0, 0)),
                  pl.BlockSpec((1, nch, LANE), lambda b, i: (b, 0, 0)),
                  pl.BlockSpec((S, 6 * LANE), lambda b, i: (b, 0)),
                  pl.BlockSpec(ov.shape, lambda b, i: (0, 0)),
                  pl.BlockSpec(e3.shape, lambda b, i: (0, 0, 0))],
        out_specs=pl.BlockSpec((Q_BLOCK, 2 * LANE), lambda b, i: (b * nq + i, 0)),
        out_shape=jax.ShapeDtypeStruct((B_ * S, N_HEADS * HEAD_DIM), F32),
        compiler_params=_cparams(2),
        name="nsa_attn_prompt",
    )(qa, gates, kc, vc, kvb, ov, e3)


def nsa_prompt_pallas(proj, B_, S, phi, phi_b, qk_g):
    qa, kvb, rawb, rows_t, win_t, _, gates = nsa_prep(proj, qk_g, B_, S)
    nch = S // CMP_STRIDE
    kc, vc = nsa_compress_pallas(rawb.reshape(B_, nch, CMP_STRIDE * 2 * LANE), compress_weights(phi), phi_b, qk_g[1])
    o = nsa_attn_prompt(qa, gates, kc, vc, kvb, B_, S)
    rows = rows_t.reshape(B_, 4, N_KV, HEAD_DIM, S).transpose(0, 4, 1, 2, 3)
    wk = min(WINDOW, S)
    win_new = win_t[:, :, S - wk:].reshape(B_, 2, N_KV, HEAD_DIM, wk).transpose(0, 4, 1, 2, 3)
    return o.reshape(B_, S, N_HEADS * HEAD_DIM), rows, win_new


N_PAGES = PAST_LEN // PAGE_SIZE
N_CHUNK_S = PAST_LEN // CMP_STRIDE
N_SEL_S = -(-(PAST_LEN + DEC_SEQ) // SEL_BLOCK)
CUR_S = PAST_LEN // SEL_BLOCK
QROWS = 8


def compress_weights_paged(phi):
    R = CMP_BLOCK // CMP_STRIDE
    wr = phi.reshape(2, R, CMP_STRIDE, HEAD_DIM, HEAD_DIM)
    w = jnp.einsum('crjde,hy->cjhdrye', wr, jnp.eye(2, dtype=phi.dtype))
    return w.reshape(2, CMP_STRIDE * LANE, R * LANE).astype(MXU_DTYPE)


def _softmax_with_extra(s, s_new):
    m = jnp.maximum(jnp.max(s, axis=-1, keepdims=True), s_new)
    e = jnp.exp(s - m)
    e_new = jnp.exp(s_new - m)
    return e, e_new, jnp.sum(e, axis=-1, keepdims=True) + e_new


CHUNKS_PER_PAGE = PAGE_SIZE // CMP_STRIDE
SWEEP_PAGES = 64


def feature_major_pages(cache_nsa):
    d, n = cache_nsa.shape[:2]
    return cache_nsa.transpose(0, 1, 3, 4, 5, 2).reshape(d * n, 4, N_KV * HEAD_DIM, PAGE_SIZE)


def _cache_compress_body(c_ref, w_ref, o_ref, sk, sv):
    n_pages = c_ref.shape[0]

    def to_row_major(p, carry):
        r0 = pl.multiple_of(p * PAGE_SIZE, PAGE_SIZE)
        sk[pl.ds(r0, PAGE_SIZE), :] = c_ref[p, 0].T
        sv[pl.ds(r0, PAGE_SIZE), :] = c_ref[p, 1].T
        return carry

    lax.fori_loop(0, n_pages, to_row_major, 0, unroll=4)
    n = n_pages * CHUNKS_PER_PAGE
    for c, src in enumerate((sk, sv)):
        x = jnp.concatenate([src[pl.ds(j, n, stride=CMP_STRIDE), :] for j in range(CMP_STRIDE)], axis=1)
        ab = _mm(x, w_ref[0, c])
        o_ref[:, c * LANE:(c + 1) * LANE] = ab[:, 0:LANE]
        o_ref[:, (2 + c) * LANE:(3 + c) * LANE] = ab[:, LANE:2 * LANE]


def cache_compress(cache_fm, nsa_phi):
    n_total = cache_fm.shape[0]
    assert (n_total // DEPTH) % SWEEP_PAGES == 0
    tiles = n_total // DEPTH // SWEEP_PAGES
    wc = jnp.stack([compress_weights_paged(nsa_phi[l]) for l in range(DEPTH)])
    rows = SWEEP_PAGES * PAGE_SIZE
    return pl.pallas_call(
        _cache_compress_body,
        grid=(DEPTH * tiles,),
        in_specs=[pl.BlockSpec((SWEEP_PAGES, 2, LANE, PAGE_SIZE), lambda i: (i, 0, 0, 0)),
                  pl.BlockSpec((1,) + wc.shape[1:], lambda i: (i // tiles, 0, 0, 0))],
        out_specs=pl.BlockSpec((SWEEP_PAGES * CHUNKS_PER_PAGE, 4 * LANE), lambda i: (i, 0)),
        out_shape=jax.ShapeDtypeStruct((n_total * CHUNKS_PER_PAGE, 4 * LANE), F32),
        scratch_shapes=[pltpu.VMEM((rows, LANE), F32), pltpu.VMEM((rows, LANE), F32)],
        compiler_params=_cparams(),
        name="cache_compress",
    )(cache_fm, wc)


SAMPLE_GROUP = 2


def _nsa_sample_body(pt_ref, qa_ref, newb_ref, wnew_ref, gates_ref, *rest):
    n_pg = SAMPLE_GROUP * N_PAGES
    pages, abs_ = rest[:n_pg], rest[n_pg:2 * n_pg]
    y_ref, wout_ref = rest[-2:]
    gens = [_nsa_sample_one(u, qa_ref, newb_ref, wnew_ref, gates_ref, pages[u * N_PAGES:(u + 1) * N_PAGES],
                            abs_[u * N_PAGES:(u + 1) * N_PAGES], *rest[2 * n_pg:-2]) for u in range(SAMPLE_GROUP)]
    outs = [None] * SAMPLE_GROUP
    while any(o is None for o in outs):
        for u, gen in enumerate(gens):
            try:
                next(gen)
            except StopIteration as stop:
                outs[u] = stop.value
    y_ref[...] = jnp.stack([o[0] for o in outs])
    wout_ref[...] = jnp.stack([o[1] for o in outs])


def _nsa_sample_one(u, qa_ref, newb_ref, wnew_ref, gates_ref, pages, abs_, win_ref, bias_ref, gkc_ref, ov_ref, e_ref):
    qs = qa_ref[u]
    newb = newb_ref[u].astype(F32)
    lane = lax.broadcasted_iota(jnp.int32, (QROWS, LANE), 1)
    row = lax.broadcasted_iota(jnp.int32, (QROWS, LANE), 0)

    ab = jnp.concatenate([a[...] for a in abs_], axis=0)
    out = ab[:, 0:2 * LANE] + pltpu.roll(ab[:, 2 * LANE:4 * LANE], N_CHUNK_S - 1, 0) + bias_ref[...]
    kc = _seg_rmsnorm(out[:, 0:LANE], gkc_ref[...])
    vc = out[:, LANE:2 * LANE]
    yield

    s = _mm_nt(qs, kc)
    yield
    s = jnp.where(lane < N_CHUNK_S - 1, s, -jnp.inf)
    e = jnp.exp(s - jnp.max(s, axis=-1, keepdims=True))
    p_cmp = e / jnp.sum(e, axis=-1, keepdims=True)
    yield
    o_cmp = _mm(p_cmp, vc)
    imp = _mm(p_cmp, ov_ref[...])
    yield
    imp = imp +jnp.where(row % GQA == 0, pltpu.roll(imp, QROWS - 1, 0), pltpu.roll(imp, 1, 0))

    valid = lane <= CUR_S
    forced = (lane == 0) | (lane == CUR_S) | (lane == CUR_S - 1)
    score = jnp.where(valid, imp, -jnp.inf)
    score = jnp.where(forced & valid, jnp.inf, score)
    cnt = jnp.zeros((QROWS, LANE), F32)
    for i in range(N_SEL_S):
        ci = score[:, i:i + 1]
        cnt = cnt + jnp.where((ci > score) | ((ci == score) & (lane > i)), 1.0, 0.0)
    sel = jnp.where((cnt < SEL_TOPK) & (score > -jnp.inf), 1.0, 0.0)
    yield

    msel = _mm(sel, e_ref[...])
    s = jnp.concatenate([_mm(qs, pg[0, 0]) for pg in pages], axis=1)
    yield
    s = jnp.where(msel > 0.5, s, NEG)
    qf = qs.astype(F32)
    s_new = jnp.sum(qf * newb[:, 2 * LANE:3 * LANE], axis=-1, keepdims=True)
    s_new = jnp.where(sel[:, CUR_S:CUR_S + 1] > 0.5, s_new, NEG)
    e, e_new, d = _softmax_with_extra(s, s_new)
    yield
    acc_o = e_new.astype(MXU_DTYPE).astype(F32) * newb[:, 3 * LANE:4 * LANE]
    for p, pg in enumerate(pages):
        acc_o = acc_o + _mm_nt(e[:, p * PAGE_SIZE:(p + 1) * PAGE_SIZE], pg[0, 1])
    o_sel = acc_o / d
    yield

    s = _mm(qs, win_ref[u, 0])
    yield
    s_new =jnp.sum(qf * newb[:, 4 * LANE:5 * LANE], axis=-1, keepdims=True)
    e, e_new, d = _softmax_with_extra(s, s_new)
    o_win = (_mm_nt(e, win_ref[u, 1]) + e_new.astype(MXU_DTYPE).astype(F32) * newb[:, 5 * LANE:6 * LANE]) / d

    g = gates_ref[u]
    o = g[:, 0:1] * o_cmp + g[:, 1:2] * o_sel + g[:, 2:3] * o_win
    o_sw = pltpu.roll(o, HEAD_DIM, 1)
    lane1 = lax.broadcasted_iota(jnp.int32, (1, LANE), 1)
    ys = []
    for h in range(N_KV):
        a = (o if h == 0 else o_sw)[GQA * h:GQA * h + 1]
        b = (o if h == 1 else o_sw)[GQA * h + 1:GQA * h + 2]
        ys.append(jnp.where(lane1 < HEAD_DIM, a, b))
    lw = win_ref.shape[3]
    last = lax.broadcasted_iota(jnp.int32, (LANE, lw), 1) == lw - 1
    wouts = []
    for c in range(2):
        col = jnp.broadcast_to(wnew_ref[u][:, c * LANE:(c + 1) * LANE], (QROWS, LANE)).T[:, 0:1]
        wouts.append(jnp.where(last, col, pltpu.roll(win_ref[u, c], lw - 1, 1)))
    return jnp.concatenate(ys, axis=1), jnp.stack(wouts)


def _sample_constants():
    ci = np.arange(LANE)[:, None] * CMP_STRIDE
    sj = np.arange(LANE)[None, :] * SEL_BLOCK
    ov = ((ci < sj + SEL_BLOCK) & (ci + CMP_BLOCK > sj) & (np.arange(LANE)[:, None] < N_CHUNK_S - 1)
          & (np.arange(LANE)[None, :] < N_SEL_S))
    e = (np.arange(LANE)[:, None] == (np.arange(PAST_LEN)[None, :] // SEL_BLOCK))
    return jnp.asarray(ov, MXU_DTYPE), jnp.asarray(e, MXU_DTYPE)


def nsa_sample_pallas(proj, layer, cache_fm, cache_ab, page_table, win_fm, phi_b, qk_g):
    B_ = proj.shape[0]
    n_phys = cache_fm.shape[0] // DEPTH
    lw = win_fm.shape[3]
    assert page_table.shape == (B_, N_PAGES) and lw <= WINDOW and lw <= PAST_LEN and CUR_S == N_SEL_S - 1
    qa, kvb, _, rows_t, _, wnew, gates = nsa_prep(proj, qk_g, 1, B_)
    qa8 = jnp.pad(qa.astype(F32).reshape(B_, N_HEADS, LANE), ((0, 0), (0, QROWS - N_HEADS), (0, 0)))
    gates8 = jnp.pad(gates[:, :3 * N_HEADS].reshape(B_, N_HEADS, 3), ((0, 0), (0, QROWS - N_HEADS), (0, LANE - 3)))
    ov, e = _sample_constants()
    bias = jnp.concatenate([jnp.tile(phi_b[0], 2), jnp.tile(phi_b[1], 2)]).reshape(1, 2 * LANE)

    G = SAMPLE_GROUP
    assert B_ % G == 0
    seq_page = [(u, p) for u in range(G) for p in range(N_PAGES)]

    def page_spec(u, p):
        return pl.BlockSpec((1, 2, LANE, PAGE_SIZE), lambda b, pt: (layer * n_phys + pt[G * b + u, p], 1, 0, 0))

    def ab_spec(u, p):
        return pl.BlockSpec((CHUNKS_PER_PAGE, 4 * LANE), lambda b, pt: (layer * n_phys + pt[G * b + u, p], 0))

    def per_b(shape):
        return pl.BlockSpec((G,) + shape, lambda b, pt: (b, 0, 0))

    def const(a):
        return pl.BlockSpec(a.shape, lambda b, pt: (0,) * a.ndim)

    gkc = jnp.tile(qk_g[1], 2).reshape(1, LANE)
    y, wout = pl.pallas_call(
        _nsa_sample_body,
        grid_spec=pltpu.PrefetchScalarGridSpec(
            num_scalar_prefetch=1,
            grid=(B_ // G,),
            in_specs=[per_b((QROWS, LANE)), per_b((1, 6 * LANE)), per_b((1, 2 * LANE)), per_b((QROWS, LANE))]
                     + [page_spec(u, p) for u, p in seq_page] + [ab_spec(u, p) for u, p in seq_page]
                     + [pl.BlockSpec((G, 2, LANE, lw), lambda b, pt: (layer * (B_ // G) + b, 0, 0, 0)),
                        const(bias), const(gkc), const(ov), const(e)],
            out_specs=[per_b((1, 2 * LANE)), pl.BlockSpec((G, 2, LANE, lw), lambda b, pt: (b, 0, 0, 0))]),
        out_shape=[jax.ShapeDtypeStruct((B_, 1, 2 * LANE), F32),
                   jax.ShapeDtypeStruct((B_, 2, LANE, lw), F32)],
        compiler_params=_cparams(),
        name="nsa_sample",
    )(page_table, qa8, kvb.reshape(B_, 1, 6 * LANE), wnew.reshape(B_, 1, 2 * LANE), gates8,
      *([cache_fm] * (G * N_PAGES)), *([cache_ab] * (G * N_PAGES)), win_fm, bias, gkc, ov, e)
    rows = rows_t.reshape(4, N_KV, HEAD_DIM, B_).transpose(3, 0, 1, 2)[:, None]
    return (y.reshape(B_, 1, N_HEADS * HEAD_DIM), rows,
            wout.reshape(B_, 2, N_KV, HEAD_DIM, lw).transpose(0, 4, 1, 2, 3))


MIX_CHUNK = 512
HALO = 16
YM_W = POOL_W + RG_W + SC_W


def _expm1(x):
    p = jnp.full_like(x, 1.0 / 3628800.0)
    for c in (1.0 / 362880.0, 1.0 / 40320.0, 1.0 / 5040.0, 1.0 / 720.0, 1.0 / 120.0, 1.0 / 24.0, 1.0 / 6.0, 0.5, 1.0):
        p = p * x + c
    return jnp.where(jnp.abs(x) < 0.25, p * x, jnp.exp(x) - 1.0)


def _softplus(x):
    return jnp.maximum(x, 0.0) + jnp.log1p(jnp.exp(-jnp.abs(x)))


def _gelu_tanh(x):
    return 0.5 * x * (1.0 + jnp.tanh(np.sqrt(2.0 / np.pi).astype(np.float32) * (x + 0.044715 * (x * x * x))))


def _rg_coeffs(xc, wa, ba, wx, bx, lam):
    r = jax.nn.sigmoid(_mm(xc, wa) + ba)
    ig = jax.nn.sigmoid(_mm(xc, wx) + bx)
    log_a = (-RG_C * r) * _softplus(-lam)
    return jnp.exp(log_a), jnp.sqrt(-_expm1(2.0 * log_a)) * (ig * xc)


def _pool_select(s2, s4, s8, s16):
    lane = lax.broadcasted_iota(jnp.int32, s2.shape, 1)
    return jnp.where(lane < POOL_GROUP, s2, jnp.where(lane < 2 * POOL_GROUP, s4,
                                                      jnp.where(lane < 3 * POOL_GROUP, s8, s16)))


def _pool_count(pos, shape):
    lane = lax.broadcasted_iota(jnp.int32, shape, 1)
    win = jnp.left_shift(2, lane // POOL_GROUP)
    return jnp.minimum(win, pos + 1).astype(F32)


def _mixers_prompt_body(pu_ref, rx_ref, rg_ref, z_ref, bg_ref, cg_ref, pw_ref, ps_ref, cw_ref, cb_ref, wa_ref, ba_ref,
                        wx_ref, bx_ref, lam_ref, scw_ref, scb_ref, ym_ref, tails_ref, hlast_ref, halo, hcar):
    c = pl.program_id(1)
    tc = pu_ref.shape[0]

    @pl.when(c == 0)
    def _():
        halo[...] = jnp.zeros_like(halo)
        hcar[...] = jnp.zeros_like(hcar)

    pu, rx = pu_ref[...], rx_ref[...]
    u = cg_ref[...] * z_ref[...]
    ext = [jnp.concatenate([halo[i], v], axis=0) for i, v in enumerate((pu, rx, u))]

    def back(e, k):
        return pltpu.roll(e, k, 0)

    s2 = ext[0] + back(ext[0], 1)
    s4 = s2 + back(s2, 2)
    s8 = s4 + back(s4, 4)
    s16 = s8 + back(s8, 8)
    tot = _pool_select(s2, s4, s8, s16)[HALO:]
    pos = c * tc + lax.broadcasted_iota(jnp.int32, (tc, POOL_W), 0)
    d = tot / _pool_count(pos, (tc, POOL_W)) - pu
    ym_ref[:, 0:POOL_W] = _mm(d, pw_ref[...]) * ps_ref[...]

    cw = cw_ref[...]
    xc = cb_ref[...] + cw[RG_CONV - 1:RG_CONV] * rx
    for k in range(1, RG_CONV):
        xc = xc + cw[RG_CONV - 1 - k:RG_CONV - k] * back(ext[1], k)[HALO:]
    a, b = _rg_coeffs(xc, wa_ref[...], ba_ref[...], wx_ref[...], bx_ref[...], lam_ref[...])
    row = lax.broadcasted_iota(jnp.int32, (tc, RG_W), 0)
    k = 1
    while k < tc:
        a_prev = jnp.where(row < k, 1.0, pltpu.roll(a, k, 0))
        b_prev = jnp.where(row < k, 0.0, pltpu.roll(b, k, 0))
        b = a * b_prev + b
        a = a * a_prev
        k *= 2
    h = a * hcar[0:1] + b
    hcar[...] = jnp.broadcast_to(h[tc - 1:tc], hcar.shape)
    hlast_ref[0] = jnp.broadcast_to(h[tc - 1:tc], hcar.shape)
    ym_ref[:, POOL_W:POOL_W + RG_W] = h * _gelu_tanh(rg_ref[...])

    scw = scw_ref[...]
    v = scb_ref[...] + scw[SC_CONV - 1:SC_CONV] * u
    for k in range(1, SC_CONV):
        v = v + scw[SC_CONV - 1 - k:SC_CONV - k] * back(ext[2], k)[HALO:]
    ym_ref[:, POOL_W + RG_W:YM_W] = bg_ref[...] * v

    for i, val in enumerate((pu, rx, u)):
        halo[i] = val[tc - HALO:]
        tails_ref[0, i] = val[tc - HALO:]


def _block_diag(w):
    g, n, _ = w.shape
    return jnp.einsum('gij,gh->gihj', w, jnp.eye(g, dtype=w.dtype)).reshape(g * n, g * n)


def _mixer_params(lw):
    row = lambda a: a.reshape(1, -1)
    return [_block_diag(lw['pool_w']).astype(MXU_DTYPE), row(lw['pool_scale']), lw['rg_conv_w'], row(lw['rg_conv_b']),
            _block_diag(lw['rg_w_a']).astype(MXU_DTYPE), row(lw['rg_b_a']),
            _block_diag(lw['rg_w_x']).astype(MXU_DTYPE), row(lw['rg_b_x']), row(lw['rg_lambda']),
            lw['sc_conv_w'], row(lw['sc_conv_b'])]


def _proj_col_specs(rows, index):
    cols = (COL_POOL, COL_RX, COL_RGATE, COL_SC, COL_SC + SC_W, COL_SC + 2 * SC_W)
    return [pl.BlockSpec((rows, GROUP_W), functools.partial(index, col // GROUP_W)) for col in cols]


def mixers_prompt(proj, lw, B_, S):
    tc = min(MIX_CHUNK, S)
    nc = S // tc
    params = _mixer_params(lw)
    fixed = lambda a: pl.BlockSpec(a.shape, lambda b, c: (0,) * a.ndim)
    return pl.pallas_call(
        _mixers_prompt_body,
        grid=(B_, nc),
        in_specs=_proj_col_specs(tc, lambda col, b, c: (b * nc + c, col)) + [fixed(a) for a in params],
        out_specs=[pl.BlockSpec((tc, YM_W), lambda b, c: (b * nc + c, 0)),
                   pl.BlockSpec((1, 3, HALO, GROUP_W), lambda b, c: (b, 0, 0, 0)),
                   pl.BlockSpec((1, 8, RG_W), lambda b, c: (b, 0, 0))],
        out_shape=[jax.ShapeDtypeStruct((B_ * S, YM_W), F32),
                   jax.ShapeDtypeStruct((B_, 3, HALO, GROUP_W), F32),
                   jax.ShapeDtypeStruct((B_, 8, RG_W), F32)],
        scratch_shapes=[pltpu.VMEM((3, HALO, GROUP_W), F32), pltpu.VMEM((8, RG_W), F32)],
        compiler_params=_cparams(2),
        name="mixers_prompt",
    )(*([proj] * 6), *params)


def _mixers_sample_body(pos0, pu_ref, rx_ref, rg_ref, z_ref, bg_ref, cg_ref, pp_ref, rp_ref, h0_ref, sp_ref, pw_ref,
                        ps_ref, cw_ref, cb_ref, wa_ref, ba_ref, wx_ref, bx_ref, lam_ref, scw_ref, scb_ref,
                        ym_ref, pn_ref, rn_ref, hn_ref, sn_ref):
    pu, rx = pu_ref[...], rx_ref[...]
    u = cg_ref[...] * z_ref[...]
    run, sums = pu, {}
    for k in range(1, POOL_KEEP + 1):
        run = run + pp_ref[POOL_KEEP - k]
        sums[k + 1] = run
    tot = _pool_select(*(sums[w] for w in POOL_WINDOWS))
    d = tot / _pool_count(pos0, pu.shape) - pu
    ym_ref[:, 0:POOL_W] = _mm(d, pw_ref[...]) * ps_ref[...]
    for k in range(POOL_KEEP - 1):
        pn_ref[k] = pp_ref[k + 1]
    pn_ref[POOL_KEEP - 1] = pu

    cw = cw_ref[...]
    xc = cb_ref[...] + cw[RG_CONV - 1:RG_CONV] * rx
    for k in range(RG_CONV - 1):
        xc = xc + cw[k:k + 1] * rp_ref[k]
    a, b = _rg_coeffs(xc, wa_ref[...], ba_ref[...], wx_ref[...], bx_ref[...], lam_ref[...])
    h = b + a * h0_ref[...]
    hn_ref[...] = h
    ym_ref[:, POOL_W:POOL_W + RG_W] = h * _gelu_tanh(rg_ref[...])
    for k in range(RG_CONV - 2):
        rn_ref[k] = rp_ref[k + 1]
    rn_ref[RG_CONV - 2] = rx

    scw = scw_ref[...]
    v = scb_ref[...] + scw[SC_CONV - 1:SC_CONV] * u
    for k in range(SC_CONV - 1):
        v = v + scw[k:k + 1] * sp_ref[k]
    ym_ref[:, POOL_W + RG_W:YM_W] = bg_ref[...] * v
    for k in range(SC_CONV - 2):
        sn_ref[k] = sp_ref[k + 1]
    sn_ref[SC_CONV - 2] = u


def mixers_sample(proj, lw, pos0, pool_prev, rgc_prev, h0, sc_prev):
    B_ = proj.shape[0]
    params = _mixer_params(lw)
    states = [pool_prev.transpose(1, 0, 2), rgc_prev.transpose(1, 0, 2), h0, sc_prev.transpose(1, 0, 2)]
    full = lambda a: pl.BlockSpec(a.shape, lambda i: (0,) * a.ndim)
    ym, pn, rn, hn, sn = pl.pallas_call(
        functools.partial(_mixers_sample_body, pos0),
        grid=(1,),
        in_specs=_proj_col_specs(B_, lambda col, i: (0, col)) + [full(a) for a in states] + [full(a) for a in params],
        out_specs=[pl.BlockSpec((B_, YM_W), lambda i: (0, 0))] + [full(a) for a in states],
        out_shape=[jax.ShapeDtypeStruct((B_, YM_W), F32)] + [jax.ShapeDtypeStruct(a.shape, F32) for a in states],
        compiler_params=_cparams(),
        name="mixers_sample",
    )(*([proj] * 6), *states, *params)
    return ym, pn.transpose(1, 0, 2), rn.transpose(1, 0, 2), hn, sn.transpose(1, 0, 2)


ROUTE_W = LANE
GROUP_LANE0 = N_EXPERTS
MOE_TILE_PROMPT = 256
MOE_TILE_SAMPLE = 32
COMBINE_TILE = 256
FETCH_GROUPS = 8
FETCH_BUFS = 3


def _rms(x, g):
    return x * lax.rsqrt(jnp.mean(x * x, axis=-1, keepdims=True) + EPS) * g


def _mix_out_router_body(ym_ref, yn_ref, x_ref, og_ref, wo_ref, gf_ref, wr_ref, br_ref, tri_ref, x2_ref, xn_ref,
                         route_ref, cnt_ref, cnt_sc):
    og = og_ref[...]
    groups = (ym_ref[:, 0:POOL_W], ym_ref[:, POOL_W:POOL_W + RG_W], yn_ref[...], ym_ref[:, POOL_W + RG_W:YM_W])
    yn = jnp.concatenate([_rms(y, og[:, i * GROUP_W:(i + 1) * GROUP_W]) for i, y in enumerate(groups)], axis=1)
    x2 = x_ref[...] + _mm(yn, wo_ref[...])
    x2_ref[...] = x2
    xn = _rms(x2, gf_ref[...])
    xn_ref[...] = xn
    logits = _mm(xn, wr_ref[...]) + br_ref[...]
    lane = lax.broadcasted_iota(jnp.int32, logits.shape, 1)
    is_grp = (lane >= GROUP_LANE0) & (lane < GROUP_LANE0 + N_GROUPS)
    grp = jnp.where(is_grp, logits, -jnp.inf)
    gmax = jnp.max(grp, axis=-1, keepdims=True)
    gsel = jnp.min(jnp.where(grp == gmax, lane - GROUP_LANE0, N_GROUPS), axis=-1, keepdims=True)
    p_group = 1.0 / jnp.sum(jnp.where(is_grp, jnp.exp(logits - gmax), 0.0), axis=-1, keepdims=True)
    le = jnp.where((lane < N_EXPERTS) & (lane // EXP_PER_GROUP == gsel), logits, -jnp.inf)
    m1 = jnp.max(le, axis=-1, keepdims=True)
    i1 = jnp.min(jnp.where(le == m1, lane, LANE), axis=-1, keepdims=True)
    le2 = jnp.where(lane == i1, -jnp.inf, le)
    m2 = jnp.max(le2, axis=-1, keepdims=True)
    i2 = jnp.min(jnp.where(le2 == m2, lane, LANE), axis=-1, keepdims=True)
    e2 = jnp.exp(m2 - m1)
    g1 = p_group * (1.0 / (1.0 + e2))
    g2 = p_group * (e2 / (1.0 + e2))
    @pl.when(pl.program_id(0) == 0)
    def _():
        cnt_sc[...] = jnp.zeros_like(cnt_sc)

    oh = jnp.where((lane == i1) | (lane == i2), 1.0, 0.0)
    before = cnt_sc[0:1] + _mm(tri_ref[...], oh)
    r1 = jnp.sum(jnp.where(lane == i1, before, 0.0), axis=-1, keepdims=True)
    r2 = jnp.sum(jnp.where(lane == i2, before, 0.0), axis=-1, keepdims=True)
    total = cnt_sc[0:1] + jnp.sum(oh, axis=0, keepdims=True)
    cnt_sc[...] = jnp.broadcast_to(total, cnt_sc.shape)
    cnt_ref[...] = jnp.broadcast_to(total, cnt_ref.shape)
    vals = (i1.astype(F32), i2.astype(F32), g1, g2, r1, r2)
    route = jnp.zeros(logits.shape, F32)
    for k, v in enumerate(vals):
        route = jnp.where(lane == k, v, route)
    route_ref[...] = route


def mix_out_router(ym, y_nsa, x2d, lw):
    T, D = x2d.shape
    tm = min(256, T)
    wr = jnp.concatenate([lw['router_expert_w'], lw['router_group_w'],
                          jnp.zeros((D, ROUTE_W - N_EXPERTS - N_GROUPS), F32)], axis=1).astype(MXU_DTYPE)
    br = jnp.concatenate([lw['router_expert_b'], lw['router_group_b'],
                          jnp.zeros((ROUTE_W - N_EXPERTS - N_GROUPS,), F32)]).reshape(1, ROUTE_W)
    row = lambda i: (i, 0)
    fixed = lambda i: (0, 0)
    tri = jnp.asarray(np.tril(np.ones((tm, tm), np.float32), -1), MXU_DTYPE)
    return pl.pallas_call(
        _mix_out_router_body,
        grid=(T // tm,),
        in_specs=[pl.BlockSpec((tm, YM_W), row), pl.BlockSpec((tm, GROUP_W), row), pl.BlockSpec((tm, D), row),
                  pl.BlockSpec((1, MIX_W), fixed),
                  pl.BlockSpec((MIX_W, D), fixed), pl.BlockSpec((1, D), fixed), pl.BlockSpec((D, ROUTE_W), fixed),
                  pl.BlockSpec((1, ROUTE_W), fixed), pl.BlockSpec((tm, tm), fixed)],
        out_specs=[pl.BlockSpec((tm, D), row), pl.BlockSpec((tm, D), row), pl.BlockSpec((tm, ROUTE_W), row),
                   pl.BlockSpec((8, ROUTE_W), fixed)],
        out_shape=[jax.ShapeDtypeStruct((T, D), F32), jax.ShapeDtypeStruct((T, D), F32),
                   jax.ShapeDtypeStruct((T, ROUTE_W), F32), jax.ShapeDtypeStruct((8, ROUTE_W), F32)],
        scratch_shapes=[pltpu.VMEM((8, ROUTE_W), F32)],
        compiler_params=_cparams(),
        name="mix_out_router",
    )(ym, y_nsa, x2d, lw['mix_out_g'].reshape(1, MIX_W), lw['w_out'].astype(MXU_DTYPE),
      lw['norm_ffn_g'].reshape(1, D), wr, br, tri)


def moe_schedule(route, counts, tile):
    T = route.shape[0]
    M = T * TOP_E
    fe = route[:, 0:TOP_E].astype(jnp.int32).reshape(M)
    rank = route[:, 4:4 + TOP_E].astype(jnp.int32).reshape(M)
    counts = counts.astype(jnp.int32)
    padded = (counts + tile - 1) // tile * tile
    pad_end = jnp.cumsum(padded)
    dest = ((pad_end - padded)[fe] + rank).astype(jnp.int32)
    n_blk = -(-M // tile) + N_EXPERTS
    tok = jnp.arange(M, dtype=jnp.int32) // TOP_E
    buf_tok = jnp.zeros((n_blk * tile,), jnp.int32).at[dest].set(tok)
    blk_exp = jnp.minimum(jnp.sum(pad_end[None, :] <= (jnp.arange(n_blk, dtype=jnp.int32) * tile)[:, None], axis=1),
                          N_EXPERTS - 1).astype(jnp.int32)
    n_used = (pad_end[-1:] // tile).astype(jnp.int32)
    return buf_tok, blk_exp, n_used, dest


def _moe_ffn_body(tile, tok_ref, bexp_ref, nused_ref, x_hbm, wgu_ref, wdn_ref, y_ref, xg, sem, wgu_bf, wdn_bf):
    j = pl.program_id(0)
    n = nused_ref[0]

    def gather(blk, slot):
        def body(r, c):
            t = tok_ref[blk * tile + r]
            pltpu.make_async_copy(x_hbm.at[pl.ds(t, 1)], xg.at[slot, pl.ds(r, 1)], sem.at[slot]).start()
            return c
        lax.fori_loop(0, tile, body, 0, unroll=8)

    @pl.when((j == 0) & (n > 0))
    def _():
        gather(0, 0)
        gather(jnp.minimum(1, n - 1), 1)

    def wait_block(slot):
        pltpu.make_async_copy(x_hbm.at[pl.ds(0, tile)], xg.at[slot], sem.at[slot]).wait()

    @pl.when(j < n)
    def _():
        slot = j % FETCH_BUFS
        nxt = jnp.minimum(j + 2, n - 1)
        dst = (j + 2) % FETCH_BUFS

        def fetch_group(g):
            per = tile // FETCH_GROUPS
            for r in range(g * per, (g + 1) * per):
                t = tok_ref[nxt * tile + r]
                pltpu.make_async_copy(x_hbm.at[pl.ds(t, 1)], xg.at[dst, pl.ds(r, 1)], sem.at[dst]).start()

        @pl.when((j == 0) | (bexp_ref[j] != bexp_ref[jnp.maximum(j - 1, 0)]))
        def _():
            wgu_bf[...] = wgu_ref[0].astype(wgu_bf.dtype)
            wdn_bf[...] = wdn_ref[0].astype(wdn_bf.dtype)

        wait_block(slot)
        x = xg[slot]
        half = FETCH_GROUPS // 2
        cg, cd = 2 * D_EXPERT // half, y_ref.shape[1] // half
        hs = []
        for c in range(half):
            fetch_group(c)
            hs.append(_mm(x, wgu_bf[:, c * cg:(c + 1) * cg]))
        h = jnp.concatenate(hs, axis=1)
        a, b = h[:, :D_EXPERT], h[:, D_EXPERT:]
        act = a * jax.nn.sigmoid(a) * b
        for c in range(half):
            fetch_group(half + c)
            y_ref[:, c * cd:(c + 1) * cd] = _mm(act, wdn_bf[:, c * cd:(c + 1) * cd])

        @pl.when(j + 1 >= n)
        def _():
            wait_block((j + 1) % FETCH_BUFS)
            wait_block(dst)

    @pl.when(j >= n)
    def _():
        y_ref[...] = jnp.zeros_like(y_ref)


def moe_ffn_pallas(xn, buf_tok, blk_exp, n_used, w_gu, w_down, tile):
    T, D = xn.shape
    n_blk = blk_exp.shape[0]
    return pl.pallas_call(
        functools.partial(_moe_ffn_body, tile),
        grid_spec=pltpu.PrefetchScalarGridSpec(
            num_scalar_prefetch=3,
            grid=(n_blk,),
            in_specs=[pl.BlockSpec(memory_space=pl.ANY),
                      pl.BlockSpec((1, D, 2 * D_EXPERT), lambda j, tok, bexp, nu: (bexp[j], 0, 0)),
                      pl.BlockSpec((1, D_EXPERT, D), lambda j, tok, bexp, nu: (bexp[j], 0, 0))],
            out_specs=pl.BlockSpec((tile, D), lambda j, tok, bexp, nu: (j, 0)),
            scratch_shapes=[pltpu.VMEM((FETCH_BUFS, tile, D), F32), pltpu.SemaphoreType.DMA((FETCH_BUFS,)),
                            pltpu.VMEM((D, 2 * D_EXPERT), MXU_DTYPE), pltpu.VMEM((D_EXPERT, D), MXU_DTYPE)]),
        out_shape=jax.ShapeDtypeStruct((n_blk * tile, D), F32),
        compiler_params=_cparams(),
        name="moe_ffn",
    )(buf_tok, blk_exp, n_used, xn, w_gu, w_down)


def _moe_combine_body(tm, slots_ref, y_hbm, x2_ref, route_ref, o_ref, yb, sem):
    i = pl.program_id(0)
    nt = pl.num_programs(0)

    def gather(tile_i, buf):
        def body(r, c):
            for k in range(TOP_E):
                s = slots_ref[(tile_i * tm + r) * TOP_E + k]
                pltpu.make_async_copy(y_hbm.at[pl.ds(s, 1)], yb.at[buf, k, pl.ds(r, 1)], sem.at[buf]).start()
            return c
        lax.fori_loop(0, tm, body, 0, unroll=8)

    def wait_tile(b):
        for k in range(TOP_E):
            pltpu.make_async_copy(y_hbm.at[pl.ds(0, tm)], yb.at[b, k], sem.at[b]).wait()

    @pl.when(i == 0)
    def _():
        gather(0, 0)
        gather(jnp.minimum(1, nt - 1), 1)

    buf = i % FETCH_BUFS
    ahead = (i + 2) % FETCH_BUFS
    gather(jnp.minimum(i + 2, nt - 1), ahead)
    wait_tile(buf)
    r = route_ref[...]
    o_ref[...] = x2_ref[...] + (r[:, 2:3] * yb[buf, 0] + r[:, 3:4] * yb[buf, 1])

    @pl.when(i + 1 >= nt)
    def _():
        wait_tile((i + 1) % FETCH_BUFS)
        wait_tile(ahead)


def moe_combine_pallas(y, slots, x2, route):
    T, D = x2.shape
    tm = min(COMBINE_TILE, T)
    return pl.pallas_call(
        functools.partial(_moe_combine_body, tm),
        grid_spec=pltpu.PrefetchScalarGridSpec(
            num_scalar_prefetch=1,
            grid=(T // tm,),
            in_specs=[pl.BlockSpec(memory_space=pl.ANY),
                      pl.BlockSpec((tm, D), lambda i, s: (i, 0)),
                      pl.BlockSpec((tm, ROUTE_W), lambda i, s: (i, 0))],
            out_specs=pl.BlockSpec((tm, D), lambda i, s: (i, 0)),
            scratch_shapes=[pltpu.VMEM((FETCH_BUFS, TOP_E, tm, D), F32), pltpu.SemaphoreType.DMA((FETCH_BUFS,))]),
        out_shape=jax.ShapeDtypeStruct((T, D), F32),
        compiler_params=_cparams(),
        name="moe_combine",
    )(slots, y, x2, route)


def mix_out_moe(ym, y_nsa, x2d, lw, tile):
    x2, xn, route, counts = mix_out_router(ym, y_nsa, x2d, lw)
    buf_tok, blk_exp, n_used, slots = moe_schedule(route, counts[0, :N_EXPERTS], tile)
    y = moe_ffn_pallas(xn, buf_tok, blk_exp + lw['expert_base'], n_used, lw['exp_w_gu'], lw['exp_w_down'], tile)
    return moe_combine_pallas(y, slots, x2, route)


def rmsnorm(x, g):
    xf = x.astype(jnp.float32)
    y = xf * lax.rsqrt(jnp.mean(xf * xf, axis=-1, keepdims=True) + EPS)
    return (y * g.astype(jnp.float32)).astype(x.dtype)


def split_cols(a, sizes):
    outs, o = [], 0
    for s in sizes:
        outs.append(a[..., o:o + s])
        o += s
    return outs


def causal_dwconv(u, prev, w, b):
    L = u.shape[1]
    ext = jnp.concatenate([prev.astype(u.dtype), u], axis=1)
    y = lax.conv_general_dilated(ext, w[:, None, :].astype(u.dtype), window_strides=(1,), padding='VALID',
                                 dimension_numbers=('NWC', 'WIO', 'NWC'), feature_group_count=u.shape[-1])
    return y + b.astype(u.dtype), ext[:, L:]


def pool_mixer(u, prev, pos0, w, scale):
    B_, L, C = u.shape
    ext = jnp.concatenate([prev.astype(u.dtype), u], axis=1)
    ef = ext.astype(jnp.float32)
    cs = jnp.concatenate([jnp.zeros((B_, 1, C), jnp.float32), jnp.cumsum(ef, axis=1)], axis=1)
    pos = pos0 + jnp.arange(L)
    means = []
    for g, win in enumerate(POOL_WINDOWS):
        sl = slice(g * POOL_GROUP, (g + 1) * POOL_GROUP)
        tot = cs[:, POOL_KEEP + 1:POOL_KEEP + 1 + L, sl] - cs[:, POOL_KEEP + 1 - win:POOL_KEEP + 1 - win + L, sl]
        cnt = jnp.minimum(win, pos + 1).astype(jnp.float32)
        means.append(tot / cnt[None, :, None])
    d = (jnp.concatenate(means, axis=-1) - ef[:, POOL_KEEP:]).astype(u.dtype)
    y = jnp.einsum('blgc,gcd->blgd', d.reshape(B_, L, len(POOL_WINDOWS), POOL_GROUP), w).reshape(B_, L, C)
    return y * scale, ext[:, L:]


def rglru_mixer(xb, gb, conv_prev, h0, conv_w, conv_b, w_a, b_a, w_x, b_x, lam):
    B_, L, C = xb.shape
    xc, conv_new = causal_dwconv(xb, conv_prev, conv_w, conv_b)
    xh = xc.reshape(B_, L, RG_HEADS, RG_BLOCK)
    r = jax.nn.sigmoid(jnp.einsum('blhi,hij->blhj', xh, w_a).reshape(B_, L, C) + b_a)
    ig = jax.nn.sigmoid(jnp.einsum('blhi,hij->blhj', xh, w_x).reshape(B_, L, C) + b_x)
    log_a = -RG_C * r.astype(jnp.float32) * jax.nn.softplus(-lam.astype(jnp.float32))
    a = jnp.exp(log_a)
    bt = jnp.sqrt(-jnp.expm1(2.0 * log_a)) * (ig * xc).astype(jnp.float32)
    bt = bt.at[:, 0].add(a[:, 0] * h0.astype(jnp.float32))
    _, h = lax.associative_scan(lambda e1, e2: (e1[0] * e2[0], e2[0] * e1[1] + e2[1]), (a, bt), axis=1)
    y = h.astype(xb.dtype) * jax.nn.gelu(gb)
    return y, conv_new, h[:, -1].astype(xb.dtype)


def masked_softmax(s, mask):
    s = jnp.where(mask, s.astype(jnp.float32), -jnp.inf)
    m = jnp.max(s, axis=-1, keepdims=True)
    e = jnp.exp(s - jnp.where(jnp.isfinite(m), m, 0.0))
    d = jnp.sum(e, axis=-1, keepdims=True)
    return e / jnp.where(d > 0, d, 1.0)


def nsa_compress(k_raw, v_raw, phi, phi_b, g_kc):
    B_, T = k_raw.shape[:2]
    R = CMP_BLOCK // CMP_STRIDE
    nch = T // CMP_STRIDE
    ncmp = nch - (R - 1)

    def comp(a, w, bias):
        ch = a[:, :nch * CMP_STRIDE].reshape(B_, nch, CMP_STRIDE, N_KV, HEAD_DIM)
        ch = ch.transpose(0, 1, 3, 2, 4).reshape(B_, nch, N_KV, CMP_STRIDE * HEAD_DIM)
        wr = w.reshape(R, CMP_STRIDE * HEAD_DIM, HEAD_DIM)
        out = jnp.einsum('bckf,fd->bckd', ch[:, 0:ncmp], wr[0])
        for r in range(1, R):
            out = out + jnp.einsum('bckf,fd->bckd', ch[:, r:r + ncmp], wr[r])
        return out + bias

    kc = rmsnorm(comp(k_raw, phi[0], phi_b[0]), g_kc)
    vc = comp(v_raw, phi[1], phi_b[1])
    cmp_end = jnp.arange(ncmp) * CMP_STRIDE + (CMP_BLOCK - 1)
    return kc, vc, cmp_end


def sel_blocks(a):
    B_, T = a.shape[:2]
    n_sel = -(-T // SEL_BLOCK)
    a = jnp.pad(a, ((0, 0), (0, n_sel * SEL_BLOCK - T), (0, 0), (0, 0)))
    return a.reshape(B_, n_sel, SEL_BLOCK, N_KV, HEAD_DIM).transpose(0, 3, 1, 2, 4)


def nsa_attend(q, q_pos, gates, kc, vc, cmp_end, ks_blk, vs_blk, kw, vw, w_pos):
    dt = q.dtype
    B_, Q = q.shape[:2]
    t = q_pos[:, None]
    s = jnp.einsum('bqkgd,bckd->bqkgc', q, kc)
    p_cmp = masked_softmax(s, (cmp_end[None, :] <= t)[None, :, None, None, :])
    o_cmp = jnp.einsum('bqkgc,bckd->bqkgd', p_cmp.astype(dt), vc)
    n_sel = ks_blk.shape[2]
    ci = jnp.arange(kc.shape[1])[:, None] * CMP_STRIDE
    sj = jnp.arange(n_sel)[None, :] * SEL_BLOCK
    overlap = ((ci < sj + SEL_BLOCK) & (ci + CMP_BLOCK > sj)).astype(jnp.float32)
    imp = jnp.einsum('bqkgc,cs->bqks', p_cmp, overlap)
    blk = jnp.arange(n_sel)[None, :]
    cur = t // SEL_BLOCK
    valid = blk <= cur
    forced = (blk == 0) | (blk == cur) | (blk == cur - 1)
    score = jnp.where(valid[None, :, None, :], imp, -jnp.inf)
    score = jnp.where((forced & valid)[None, :, None, :], jnp.inf, score)
    top_v, top_i = lax.top_k(score, min(SEL_TOPK, n_sel))
    kk = top_i.shape[-1]
    bi = jnp.arange(B_)[:, None, None, None]
    hi = jnp.arange(N_KV)[None, None, :, None]
    ks = ks_blk[bi, hi, top_i].reshape(B_, Q, N_KV, kk * SEL_BLOCK, HEAD_DIM)
    vs = vs_blk[bi, hi, top_i].reshape(B_, Q, N_KV, kk * SEL_BLOCK, HEAD_DIM)
    spos = (top_i[..., None] * SEL_BLOCK + jnp.arange(SEL_BLOCK)).reshape(B_, Q, N_KV, kk * SEL_BLOCK)
    smask = (spos <= q_pos[None, :, None, None]) & jnp.repeat(top_v > -jnp.inf, SEL_BLOCK, axis=-1)
    s = jnp.einsum('bqkgd,bqknd->bqkgn', q, ks)
    o_sel = jnp.einsum('bqkgn,bqknd->bqkgd', masked_softmax(s, smask[:, :, :, None, :]).astype(dt), vs)
    wd = t - w_pos[None, :]
    wmask = (w_pos[None, :] >= 0) & (wd >= 0) & (wd <= WINDOW)
    s = jnp.einsum('bqkgd,bnkd->bqkgn', q, kw)
    o_win = jnp.einsum('bqkgn,bnkd->bqkgd', masked_softmax(s, wmask[None, :, None, None, :]).astype(dt), vw)
    return gates[..., 0:1] * o_cmp + gates[..., 1:2] * o_sel + gates[..., 2:3] * o_win


def nsa_prompt(q, gates, kc_raw, vc_raw, ksel, vsel, kwin, vwin, phi, phi_b, g_kc):
    B_, S = q.shape[:2]
    kc, vc, cmp_end = nsa_compress(kc_raw, vc_raw, phi, phi_b, g_kc)
    ks_blk, vs_blk = sel_blocks(ksel), sel_blocks(vsel)
    zpad = jnp.zeros((B_, WINDOW, N_KV, HEAD_DIM), kwin.dtype)
    kw_pad = jnp.concatenate([zpad, kwin], axis=1)
    vw_pad = jnp.concatenate([zpad, vwin], axis=1)
    nq = S // Q_BLOCK

    def body(args):
        qc, gc, i = args
        start = i * Q_BLOCK
        kw = lax.dynamic_slice_in_dim(kw_pad, start, WINDOW + Q_BLOCK, axis=1)
        vw = lax.dynamic_slice_in_dim(vw_pad, start, WINDOW + Q_BLOCK, axis=1)
        return nsa_attend(qc, start + jnp.arange(Q_BLOCK), gc, kc, vc, cmp_end, ks_blk, vs_blk,
                          kw, vw, start - WINDOW + jnp.arange(WINDOW + Q_BLOCK))

    qb = q.reshape(B_, nq, Q_BLOCK, N_KV, GQA, HEAD_DIM).swapaxes(0, 1)
    gb = gates.reshape(B_, nq, Q_BLOCK, N_KV, GQA, 3).swapaxes(0, 1)
    o = lax.map(body, (qb, gb, jnp.arange(nq)))
    o = o.swapaxes(0, 1).reshape(B_, S, N_HEADS * HEAD_DIM)
    rows = jnp.stack([kc_raw, vc_raw, ksel, vsel], axis=2)
    win_new = jnp.stack([kwin, vwin], axis=2)[:, S - min(WINDOW, S):]
    return o, rows, win_new


def nsa_sample(pool, page_table, win_buf, q, gates, kc_raw, vc_raw, ksel, vsel, kwin, vwin, phi, phi_b, g_kc):
    B_, L = q.shape[:2]
    past = pool[page_table]
    past = past.reshape(B_, past.shape[1] * past.shape[2], 4, N_KV, HEAD_DIM)
    P = past.shape[1]
    rows = jnp.stack([kc_raw, vc_raw, ksel, vsel], axis=2)
    full = jnp.concatenate([past.astype(rows.dtype), rows], axis=1)
    kc, vc, cmp_end = nsa_compress(full[:, :, 0], full[:, :, 1], phi, phi_b, g_kc)
    ks_blk, vs_blk = sel_blocks(full[:, :, 2]), sel_blocks(full[:, :, 3])
    Lw = win_buf.shape[1]
    new_w = jnp.stack([kwin, vwin], axis=2)
    wfull = jnp.concatenate([win_buf.astype(new_w.dtype), new_w], axis=1)
    o = nsa_attend(q, P + jnp.arange(L), gates, kc, vc, cmp_end, ks_blk, vs_blk,
                   wfull[:, :, 0], wfull[:, :, 1], P - Lw + jnp.arange(Lw + L))
    return o.reshape(B_, L, N_HEADS * HEAD_DIM), rows, wfull[:, L:]


def expert_dispatch(xt, eidx, gate, w_gu, w_down):
    T, D = xt.shape
    M = T * TOP_E
    fe = eidx.reshape(M)
    ftok = jnp.arange(M, dtype=jnp.int32) // TOP_E
    fgate = gate.reshape(M)
    order = jnp.argsort(fe)
    se, stok, sgate = fe[order], ftok[order], fgate[order]
    counts = jnp.bincount(fe, length=N_EXPERTS)
    padded = (counts + MOE_BLOCK - 1) // MOE_BLOCK * MOE_BLOCK
    pad_end = jnp.cumsum(padded)
    pad_start = pad_end - padded
    start = jnp.cumsum(counts) - counts
    dest = pad_start[se] + jnp.arange(M) - start[se]
    n_blk = -(-M // MOE_BLOCK) + N_EXPERTS
    P = n_blk * MOE_BLOCK
    buf_tok = jnp.zeros((P,), jnp.int32).at[dest].set(stok)
    buf_gate = jnp.zeros((P,), fgate.dtype).at[dest].set(sgate)
    blk_exp = jnp.minimum(jnp.searchsorted(pad_end, jnp.arange(n_blk) * MOE_BLOCK, side='right'), N_EXPERTS - 1)
    xb = xt[buf_tok].reshape(n_blk, MOE_BLOCK, D)

    def run(args):
        xi, e = args
        a, b = jnp.split(xi @ w_gu[e], 2, axis=-1)
        return (jax.nn.silu(a) * b) @ w_down[e]

    yb = lax.map(run, (xb, blk_exp)).reshape(P, D)
    return jax.ops.segment_sum(yb * buf_gate[:, None].astype(yb.dtype), buf_tok, num_segments=T)


def moe_ffn(x, wg_r, bg_r, we_r, be_r, w_gu, w_down):
    B_, L, D = x.shape
    xt = x.reshape(B_ * L, D)
    T = xt.shape[0]
    lg = (xt @ wg_r + bg_r).astype(jnp.float32)
    pg = jax.nn.softmax(lg, axis=-1)
    gsel = jnp.argmax(lg, axis=-1)
    p_group = jnp.take_along_axis(pg, gsel[:, None], axis=-1)
    le = (xt @ we_r + be_r).astype(jnp.float32).reshape(T, N_GROUPS, EXP_PER_GROUP)
    le_g = jnp.take_along_axis(le, gsel[:, None, None], axis=1)[:, 0]
    tv, ti = lax.top_k(le_g, TOP_E)
    gate = p_group * jax.nn.softmax(tv, axis=-1)
    eidx = gsel[:, None] * EXP_PER_GROUP + ti
    return expert_dispatch(xt, eidx, gate, w_gu, w_down).reshape(B_, L, D)


def layer_forward(x, pos0, lw, pool_prev, rgc_prev, rgh0, sc_prev, nsa_fn):
    B_, L, _ = x.shape
    w_perm = permute_w_in(lw['w_in']).astype(MXU_DTYPE)
    proj2d = norm_matmul(x.reshape(B_ * L, D_MODEL), lw['norm_mix_g'], w_perm)
    if pool_prev is None:
        ym, tails, hlast = mixers_prompt(proj2d, lw, B_, L)
        pool_new = tails[:, 0, HALO - POOL_KEEP:]
        rgc_new = tails[:, 1, HALO - (RG_CONV - 1):]
        sc_new = tails[:, 2, HALO - (SC_CONV - 1):]
        rgh_new = hlast[:, 0]
    else:
        ym, pool_new, rgc_new, rgh_new, sc_new = mixers_sample(proj2d, lw, pos0, pool_prev, rgc_prev, rgh0, sc_prev)
    y_nsa, nsa_rows, win_new = nsa_fn(proj2d, lw['nsa_phi'], lw['nsa_phi_b'], lw['nsa_qk_g'])
    x = mix_out_moe(ym, y_nsa.reshape(B_ * L, GROUP_W), x.reshape(B_ * L, D_MODEL), lw,
                    MOE_TILE_PROMPT if L > 1 else MOE_TILE_SAMPLE)
    return x.reshape(B_, L, D_MODEL), (nsa_rows, win_new, pool_new, rgc_new, rgh_new, sc_new)


def kernel(x_prompt, x_sample, cache_nsa, state_win_kv, state_pool, state_rg_conv, state_rg_h, state_sc_conv,
           page_table, norm_mix_g, w_in, pool_w, pool_scale, rg_conv_w, rg_conv_b, rg_w_a, rg_b_a, rg_w_x, rg_b_x,
           rg_lambda, nsa_phi, nsa_phi_b, nsa_qk_g, sc_conv_w, sc_conv_b, mix_out_g, w_out, norm_ffn_g,
           router_group_w, router_group_b, router_expert_w, router_expert_b, exp_w_gu, exp_w_down):
    past_len = page_table.shape[1] * cache_nsa.shape[2]
    xp, xs = x_prompt, x_sample
    cache3 = feature_major_pages(cache_nsa)
    win3 = state_win_kv.transpose(0, 1, 3, 4, 5, 2).reshape(DEPTH * state_win_kv.shape[1], 2, N_KV * HEAD_DIM,
                                                             state_win_kv.shape[2])
    cache_ab = cache_compress(cache3, nsa_phi)
    Bp = xp.shape[0]
    st_p, st_s = [], []
    for l in range(DEPTH):
        lw = dict(norm_mix_g=norm_mix_g[l], w_in=w_in[l], pool_w=pool_w[l], pool_scale=pool_scale[l],
                  rg_conv_w=rg_conv_w[l], rg_conv_b=rg_conv_b[l], rg_w_a=rg_w_a[l], rg_b_a=rg_b_a[l],
                  rg_w_x=rg_w_x[l], rg_b_x=rg_b_x[l], rg_lambda=rg_lambda[l], nsa_phi=nsa_phi[l],
                  nsa_phi_b=nsa_phi_b[l], nsa_qk_g=nsa_qk_g[l], sc_conv_w=sc_conv_w[l], sc_conv_b=sc_conv_b[l],
                  mix_out_g=mix_out_g[l], w_out=w_out[l], norm_ffn_g=norm_ffn_g[l],
                  router_group_w=router_group_w[l], router_group_b=router_group_b[l],
                  router_expert_w=router_expert_w[l], router_expert_b=router_expert_b[l],
                  exp_w_gu=exp_w_gu.reshape((DEPTH * N_EXPERTS,) + exp_w_gu.shape[2:]),
                  exp_w_down=exp_w_down.reshape((DEPTH * N_EXPERTS,) + exp_w_down.shape[2:]),
                  expert_base=l * N_EXPERTS)
        xp, sp = layer_forward(xp, 0, lw, None, None, None, None,
                               lambda p, phi, phi_b, g: nsa_prompt_pallas(p, Bp, xp.shape[1], phi, phi_b, g))
        xs, ss = layer_forward(xs, past_len, lw, state_pool[l], state_rg_conv[l], state_rg_h[l], state_sc_conv[l],
                               lambda p, phi, phi_b, g: nsa_sample_pallas(p, l, cache3, cache_ab, page_table, win3,
                                                                          phi_b, g))
        st_p.append(sp)
        st_s.append(ss)

    def stk(lst, i):
        return jnp.stack([s[i] for s in lst])

    return (xp, xs, stk(st_p, 0), stk(st_s, 0), stk(st_p, 1), stk(st_s, 1), stk(st_p, 2), stk(st_s, 2),
            stk(st_p, 3), stk(st_s, 3), stk(st_p, 4), stk(st_s, 4), stk(st_p, 5), stk(st_s, 5))
```

```python
import functools
import jax, jax.numpy as jnp
from jax import lax
import numpy as np
from jax.experimental import pallas as pl
from jax.experimental.pallas import tpu as pltpu

D_MODEL = 1024
BATCH = 4
SEQ = 4096
DEPTH = 2
DEC_BATCH = 128
DEC_SEQ = 1
PAST_LEN = 2048
PAGE_SIZE = 128

MIX_W = D_MODEL
GROUP_W = MIX_W // 4
POOL_W = GROUP_W
POOL_WINDOWS = (2, 4, 8, 16)
POOL_GROUP = POOL_W // len(POOL_WINDOWS)
POOL_KEEP = max(POOL_WINDOWS) - 1
RG_W = GROUP_W
RG_HEADS = 4
RG_BLOCK = RG_W // RG_HEADS
RG_CONV = 4
RG_C = 8.0
HEAD_DIM = 64
N_HEADS = GROUP_W // HEAD_DIM
N_KV = 2
GQA = N_HEADS // N_KV
CMP_BLOCK = 32
CMP_STRIDE = 16
SEL_BLOCK = 64
SEL_TOPK = 16
WINDOW = 512
Q_BLOCK = 128
SC_W = GROUP_W
SC_CONV = 3
N_GROUPS = 4
EXP_PER_GROUP = 8
N_EXPERTS = N_GROUPS * EXP_PER_GROUP
TOP_E = 2
D_EXPERT = 512
MOE_BLOCK = 128
EPS = 1e-6
SPLIT_SIZES = (POOL_W, RG_W, RG_W, N_HEADS * HEAD_DIM, 6 * N_KV * HEAD_DIM, 3 * N_HEADS, 3 * SC_W)
N_IN = sum(SPLIT_SIZES)

LANE = 128
ROW_TILE = 512
VMEM_LIMIT = 48 * 1024 * 1024
MXU_DTYPE = jnp.bfloat16
F32 = jnp.float32
NEG = -1e30

KV_W = 6 * N_KV * HEAD_DIM
COL_Q = 0
COL_KV = COL_Q + N_HEADS * HEAD_DIM
COL_POOL = COL_KV + KV_W
COL_RX = COL_POOL + POOL_W
COL_RGATE = COL_RX + RG_W
COL_SC = COL_RGATE + RG_W
COL_NG = COL_SC + 3 * SC_W
N_IN_PAD = COL_NG + LANE
SEL_TILE = 512
N_SEL_PROMPT = SEQ // SEL_BLOCK


def _cparams(n_axes=1):
    return pltpu.CompilerParams(dimension_semantics=("arbitrary",) * n_axes, vmem_limit_bytes=VMEM_LIMIT)


def _mm(a, b):
    return jnp.dot(a.astype(MXU_DTYPE), b.astype(MXU_DTYPE), preferred_element_type=F32)


def _mm_nt(a, b):
    return lax.dot_general(a.astype(MXU_DTYPE), b.astype(MXU_DTYPE), (((1,), (1,)), ((), ())),
                           preferred_element_type=F32)


def permute_w_in(w):
    pu, rx, rgate, q, kv, ng, sc = split_cols(w, SPLIT_SIZES)
    pad = jnp.zeros((w.shape[0], LANE - ng.shape[1]), w.dtype)
    return jnp.concatenate([q, kv, pu, rx, rgate, sc, ng, pad], axis=1)


def _norm_matmul_body(x_ref, g_ref, w_ref, o_ref):
    xf = x_ref[...]
    h = xf * lax.rsqrt(jnp.mean(xf * xf, axis=-1, keepdims=True) + EPS) * g_ref[...]
    o_ref[...] = _mm(h, w_ref[...])


def norm_matmul(x2d, g, w):
    T, D = x2d.shape
    N = w.shape[1]
    tm = min(ROW_TILE, T)
    return pl.pallas_call(
        _norm_matmul_body,
        grid=(T // tm,),
        in_specs=[pl.BlockSpec((tm, D), lambda i: (i, 0)),
                  pl.BlockSpec((1, D), lambda i: (0, 0)),
                  pl.BlockSpec((D, N), lambda i: (0, 0))],
        out_specs=pl.BlockSpec((tm, N), lambda i: (i, 0)),
        out_shape=jax.ShapeDtypeStruct((T, N), F32),
        compiler_params=_cparams(),
        name="norm_in_proj",
    )(x2d, g.reshape(1, D), w)


def _seg_rmsnorm(x, g):
    x2 = x * x
    left = lax.broadcasted_iota(jnp.int32, x.shape, 1) < HEAD_DIM
    s_l = jnp.sum(jnp.where(left, x2, 0.0), axis=-1, keepdims=True)
    s_r = jnp.sum(jnp.where(left, 0.0, x2), axis=-1, keepdims=True)
    ms = jnp.where(left, s_l, s_r) * (1.0 / HEAD_DIM)
    return x * lax.rsqrt(ms + EPS) * g


def _nsa_prep_body(qkv_ref, ng_ref, g_ref, perm_ref, qa_ref, kvb_ref, rawb_ref, rows_t_ref, win_t_ref, win_ref,
                   gates_ref):
    g = g_ref[...]
    for hb in range(N_KV):
        qn = _seg_rmsnorm(qkv_ref[:, COL_Q + hb * LANE:COL_Q + (hb + 1) * LANE], g[0:1]) * (HEAD_DIM ** -0.5)
        qa_ref[:, hb * 2 * LANE:(hb + 1) * 2 * LANE] = _mm(qn, perm_ref[hb]).astype(qa_ref.dtype)
    comp = [qkv_ref[:, COL_KV + c * LANE:COL_KV + (c + 1) * LANE] for c in range(6)]
    comp[2] = _seg_rmsnorm(comp[2], g[2:3])
    comp[4] = _seg_rmsnorm(comp[4], g[3:4])
    for c in range(6):
        kvb_ref[:, c * LANE:(c + 1) * LANE] = comp[c].astype(kvb_ref.dtype)
    for c in range(2):
        rawb_ref[:, c * LANE:(c + 1) * LANE] = comp[c].astype(rawb_ref.dtype)
    for c in range(4):
        rows_t_ref[0, c * LANE:(c + 1) * LANE, :] = comp[c].T
    for c in range(2):
        win_t_ref[0, c * LANE:(c + 1) * LANE, :] = comp[4 + c].T
        win_ref[:, c * LANE:(c + 1) * LANE] = comp[4 + c]
    gates_ref[...] = jax.nn.sigmoid(ng_ref[...])


def _q_place_matrices():
    p = np.zeros((N_KV, LANE, 2 * LANE), np.float32)
    for hb in range(N_KV):
        for gq in range(GQA):
            for d in range(HEAD_DIM):
                p[hb, gq * HEAD_DIM + d, gq * LANE + hb * HEAD_DIM + d] = 1.0
    return jnp.asarray(p, MXU_DTYPE)


def nsa_prep(proj, qk_g, B_, S):
    T = proj.shape[0]
    tm = min(ROW_TILE, S)
    tpb = S // tm
    qkv_w = COL_POOL
    g4 = jnp.tile(qk_g, (1, 2))
    return pl.pallas_call(
        _nsa_prep_body,
        grid=(T // tm,),
        in_specs=[pl.BlockSpec((tm, qkv_w), lambda i: (i, 0)),
                  pl.BlockSpec((tm, LANE), lambda i: (i, COL_NG // LANE)),
                  pl.BlockSpec((4, LANE), lambda i: (0, 0)),
                  pl.BlockSpec((N_KV, LANE, 2 * LANE), lambda i: (0, 0, 0))],
        out_specs=[pl.BlockSpec((tm, 4 * LANE), lambda i: (i, 0)),
                   pl.BlockSpec((tm, 6 * LANE), lambda i: (i, 0)),
                   pl.BlockSpec((tm, 2 * LANE), lambda i: (i, 0)),
                   pl.BlockSpec((1, 4 * LANE, tm), lambda i: (i // tpb, 0, i % tpb)),
                   pl.BlockSpec((1, 2 * LANE, tm), lambda i: (i // tpb, 0, i % tpb)),
                   pl.BlockSpec((tm, 2 * LANE), lambda i: (i, 0)),
                   pl.BlockSpec((tm, LANE), lambda i: (i, 0))],
        out_shape=[jax.ShapeDtypeStruct((T, 4 * LANE), MXU_DTYPE),
                   jax.ShapeDtypeStruct((T, 6 * LANE), MXU_DTYPE),
                   jax.ShapeDtypeStruct((T, 2 * LANE), MXU_DTYPE),
                   jax.ShapeDtypeStruct((B_, 4 * LANE, S), F32),
                   jax.ShapeDtypeStruct((B_, 2 * LANE, S), F32),
                   jax.ShapeDtypeStruct((T, 2 * LANE), F32),
                   jax.ShapeDtypeStruct((T, LANE), F32)],
        compiler_params=_cparams(),
        name="nsa_prep",
    )(proj, proj, g4, _q_place_matrices())


def compress_weights(phi):
    R = CMP_BLOCK // CMP_STRIDE
    wr = phi.reshape(2, R, CMP_STRIDE, HEAD_DIM, HEAD_DIM)
    eye = jnp.eye(2, dtype=phi.dtype)
    w = jnp.einsum('crjde,cx,hy->rjchdxye', wr, eye, eye)
    return w.reshape(R, CMP_STRIDE * 2 * LANE, 2 * LANE).astype(MXU_DTYPE)


def _compress_body(x_ref, w_ref, b_ref, g_ref, kc_ref, vc_ref):
    x = x_ref[0]
    nch = x.shape[0]
    a = _mm(x, w_ref[0])
    bm = _mm(x, w_ref[1])
    out = a + pltpu.roll(bm, nch - 1, 0) + b_ref[...]
    kc_ref[0] = _seg_rmsnorm(out[:, 0:LANE], g_ref[...]).astype(kc_ref.dtype)
    vc_ref[0] = out[:, LANE:2 * LANE].astype(vc_ref.dtype)


def nsa_compress_pallas(rawb3, wc, phi_b, g_kc):
    B_, nch, K = rawb3.shape
    bias = jnp.concatenate([jnp.tile(phi_b[0], 2), jnp.tile(phi_b[1], 2)]).reshape(1, 2 * LANE)
    return pl.pallas_call(
        _compress_body,
        grid=(B_,),
        in_specs=[pl.BlockSpec((1, nch, K), lambda b: (b, 0, 0)),
                  pl.BlockSpec(wc.shape, lambda b: (0, 0, 0)),
                  pl.BlockSpec((1, 2 * LANE), lambda b: (0, 0)),
                  pl.BlockSpec((1, LANE), lambda b: (0, 0))],
        out_specs=[pl.BlockSpec((1, nch, LANE), lambda b: (b, 0, 0)),
                   pl.BlockSpec((1, nch, LANE), lambda b: (b, 0, 0))],
        out_shape=[jax.ShapeDtypeStruct((B_, nch, LANE), MXU_DTYPE),
                   jax.ShapeDtypeStruct((B_, nch, LANE), MXU_DTYPE)],
        compiler_params=_cparams(),
        name="nsa_compress",
    )(rawb3, wc, bias, jnp.tile(g_kc, 2).reshape(1, LANE))


def _online_update(carry, s, v):
    m, l, acc = carry
    m_new = jnp.maximum(m, jnp.max(s, axis=-1, keepdims=True))
    alpha = jnp.exp(m - m_new)
    p = jnp.exp(s - m_new)
    l = alpha * l + jnp.sum(p, axis=-1, keepdims=True)
    acc = alpha * acc + _mm(p, v)
    return m_new, l, acc


def _select_blocks(imp, start):
    n_sel = N_SEL_PROMPT
    sc_t = imp.T[0:n_sel]
    blk = lax.broadcasted_iota(jnp.int32, sc_t.shape, 0)
    cur = (start + lax.broadcasted_iota(jnp.int32, sc_t.shape, 1)) // SEL_BLOCK
    valid = blk <= cur
    forced = (blk == 0) | (blk == cur) | (blk == cur - 1)
    score = jnp.where(valid, sc_t, -jnp.inf)
    score = jnp.where(forced & valid, jnp.inf, score)
    sub = 8
    groups = [score[g * sub:(g + 1) * sub] for g in range(n_sel // sub)]
    cnts = [jnp.zeros((sub, sc_t.shape[1]), F32) for _ in groups]
    row = lax.broadcasted_iota(jnp.int32, (sub, sc_t.shape[1]), 0)
    for i in range(n_sel):
        ri = score[i:i + 1, :]
        for g, sg in enumerate(groups):
            if (g + 1) * sub - 1 < i:
                beat = jnp.where(ri > sg, 1.0, 0.0)
            elif g * sub > i:
                beat = jnp.where(ri >= sg, 1.0, 0.0)
            else:
                beat = jnp.where(row + g * sub > i, jnp.where(ri >= sg, 1.0, 0.0), jnp.where(ri > sg, 1.0, 0.0))
            cnts[g] = cnts[g] + beat
    cnt = jnp.concatenate(cnts, axis=0)
    sel_t = jnp.where((cnt < SEL_TOPK) & (score > -jnp.inf), 1.0, 0.0)
    sel_t = jnp.concatenate([sel_t, jnp.zeros((LANE - n_sel, sc_t.shape[1]), F32)], axis=0)
    return sel_t.T


def _nsa_attn_body(qa_ref, gates_ref, kc_ref, vc_ref, kv_ref, ov_ref, e_ref, o_ref):
    i = pl.program_id(1)
    start = i * Q_BLOCK
    Q = Q_BLOCK
    R = GQA * Q
    t_row = start + lax.broadcasted_iota(jnp.int32, (R, 1), 0) % Q
    gates = gates_ref[...]
    lane_q = lax.broadcasted_iota(jnp.int32, (Q, LANE), 1)
    heads = range(N_KV)
    qs = [jnp.concatenate([qa_ref[:, (h * GQA + gq) * LANE:(h * GQA + gq + 1) * LANE] for gq in range(GQA)], axis=0)
          for h in heads]

    o_cmps, sel_bias = [], []
    kc = kc_ref[0]
    ncmp = kc.shape[0]
    cmp_end = lax.broadcasted_iota(jnp.int32, (R, ncmp), 1) * CMP_STRIDE + (CMP_BLOCK - 1)
    for h in heads:
        s = jnp.where(cmp_end <= t_row, _mm_nt(qs[h], kc), -jnp.inf)
        m = jnp.max(s, axis=-1, keepdims=True)
        e = jnp.exp(s - jnp.where(m > -jnp.inf, m, 0.0))
        d = jnp.sum(e, axis=-1, keepdims=True)
        p_cmp = e / jnp.where(d > 0, d, 1.0)
        o_cmps.append(_mm(p_cmp, vc_ref[0]))
        imp = _mm(p_cmp[0:Q], ov_ref[...]) + _mm(p_cmp[Q:R], ov_ref[...])
        sel = _select_blocks(imp, start)
        bias = jnp.concatenate([jnp.where(sel > 0.5, 0.0, NEG)] * GQA, axis=0)
        sel_bias.append(jnp.concatenate([qs[h], bias.astype(MXU_DTYPE)], axis=1))

    def sel_scores(j):
        off = pl.multiple_of(j * SEL_TILE, SEL_TILE)
        k = jnp.concatenate([kv_ref[pl.ds(off, SEL_TILE), 2 * LANE:3 * LANE], e_ref[pl.ds(off, SEL_TILE), :]], axis=1)
        v = kv_ref[pl.ds(off, SEL_TILE), 3 * LANE:4 * LANE]
        return off, v, [_mm_nt(sel_bias[h], k) for h in heads]

    def sel_step(j, carry):
        _, v, ss = sel_scores(j)
        return tuple(_online_update(carry[h], ss[h], v) for h in heads)

    init = (jnp.full((R, 1), NEG, F32), jnp.zeros((R, 1), F32), jnp.zeros((R, LANE), F32))
    n_tiles = (start + Q + SEL_TILE - 1) // SEL_TILE
    carry = lax.fori_loop(0, n_tiles - 1, sel_step, (init,) * N_KV)
    off, v, ss = sel_scores(n_tiles - 1)
    causal = off + lax.broadcasted_iota(jnp.int32, (R, SEL_TILE), 1) <= t_row
    o_sels = []
    for h in heads:
        _, l_s, acc_s = _online_update(carry[h], jnp.where(causal, ss[h], NEG), v)
        o_sels.append(acc_s / l_s)

    n_w = WINDOW // Q + 1
    offs = [pl.multiple_of(jnp.maximum(i - kk, 0) * Q, Q) for kk in range(n_w)]
    kw = jnp.concatenate([kv_ref[pl.ds(o, Q), 4 * LANE:5 * LANE] for o in offs], axis=0)
    vw = jnp.concatenate([kv_ref[pl.ds(o, Q), 5 * LANE:6 * LANE] for o in offs], axis=0)
    lane_w = lax.broadcasted_iota(jnp.int32, (1, n_w * Q), 1)
    w_pos = (i - lane_w // Q) * Q + lane_w % Q
    wd = t_row - w_pos
    wmask = (w_pos >= 0) & (wd >= 0) & (wd <= WINDOW)
    o_wins = []
    for h in heads:
        s = jnp.where(wmask, _mm_nt(qs[h], kw), NEG)
        p = jnp.exp(s - jnp.max(s, axis=-1, keepdims=True))
        o_wins.append(_mm(p, vw) / jnp.sum(p, axis=-1, keepdims=True))

    for h in heads:
        o_cmp, o_sel, o_win = o_cmps[h], o_sels[h], o_wins[h]
        outs = []
        for gq in range(GQA):
            c0 = (h * GQA + gq) * 3
            rs = slice(gq * Q, (gq + 1) * Q)
            og = (gates[:, c0:c0 + 1] * o_cmp[rs] + gates[:, c0 + 1:c0 + 2] * o_sel[rs]
                  + gates[:, c0 + 2:c0 + 3] * o_win[rs])
            outs.append(og if gq == h else pltpu.roll(og, HEAD_DIM, 1))
        o_ref[:, h * LANE:(h + 1) * LANE] = jnp.where(lane_q < HEAD_DIM, outs[0], outs[1])


def _sel_constants(S):
    ncmp_rows = S // CMP_STRIDE
    ci = np.arange(ncmp_rows)[:, None] * CMP_STRIDE
    sj = np.arange(LANE)[None, :] * SEL_BLOCK
    ov = ((ci < sj + SEL_BLOCK) & (ci + CMP_BLOCK > sj) & (np.arange(LANE)[None, :] < S // SEL_BLOCK))
    e = (np.arange(S)[:, None] // SEL_BLOCK == np.arange(LANE)[None, :])
    return jnp.asarray(ov, MXU_DTYPE), jnp.asarray(e, MXU_DTYPE)


def nsa_attn_prompt(qa, gates, kc, vc, kvb, B_, S):
    nq = S // Q_BLOCK
    nch = S // CMP_STRIDE
    ov, e3 = _sel_constants(S)
    return pl.pallas_call(
        _nsa_attn_body,
        grid=(B_, nq),
        in_specs=[pl.BlockSpec((Q_BLOCK, 4 * LANE), lambda b, i: (b * nq + i, 0)),
                  pl.BlockSpec((Q_BLOCK, LANE), lambda b, i: (b * nq + i, 0)),
                  pl.BlockSpec((1, nch, LANE), lambda b, i: (b, 0, 0)),
                  pl.BlockSpec((1, nch, LANE), lambda b, i: (b, 0, 0)),
                  pl.BlockSpec((S, 6 * LANE), lambda b, i: (b, 0)),
                  pl.BlockSpec(ov.shape, lambda b, i: (0, 0)),
                  pl.BlockSpec(e3.shape, lambda b, i: (0, 0))],
        out_specs=pl.BlockSpec((Q_BLOCK, 2 * LANE), lambda b, i: (b * nq + i, 0)),
        out_shape=jax.ShapeDtypeStruct((B_ * S, N_HEADS * HEAD_DIM), F32),
        compiler_params=_cparams(2),
        name="nsa_attn_prompt",
    )(qa, gates, kc, vc, kvb, ov, e3)


def nsa_prompt_pallas(proj, B_, S, phi, phi_b, qk_g):
    qa, kvb, rawb, rows_t, win_t, _, gates = nsa_prep(proj, qk_g, B_, S)
    nch = S // CMP_STRIDE
    kc, vc = nsa_compress_pallas(rawb.reshape(B_, nch, CMP_STRIDE * 2 * LANE), compress_weights(phi), phi_b, qk_g[1])
    o = nsa_attn_prompt(qa, gates, kc, vc, kvb, B_, S)
    rows = rows_t.reshape(B_, 4, N_KV, HEAD_DIM, S).transpose(0, 4, 1, 2, 3)
    wk = min(WINDOW, S)
    win_new = win_t[:, :, S - wk:].reshape(B_, 2, N_KV, HEAD_DIM, wk).transpose(0, 4, 1, 2, 3)
    return o.reshape(B_, S, N_HEADS * HEAD_DIM), rows, win_new


N_PAGES = PAST_LEN // PAGE_SIZE
N_CHUNK_S = PAST_LEN // CMP_STRIDE
N_SEL_S = -(-(PAST_LEN + DEC_SEQ) // SEL_BLOCK)
CUR_S = PAST_LEN // SEL_BLOCK
QROWS = 8


def compress_weights_paged(phi):
    R = CMP_BLOCK // CMP_STRIDE
    wr = phi.reshape(2, R, CMP_STRIDE, HEAD_DIM, HEAD_DIM)
    w = jnp.einsum('crjde,hy->cjhdrye', wr, jnp.eye(2, dtype=phi.dtype))
    return w.reshape(2, CMP_STRIDE * LANE, R * LANE).astype(MXU_DTYPE)


def _softmax_with_extra(s, s_new):
    m = jnp.maximum(jnp.max(s, axis=-1, keepdims=True), s_new)
    e = jnp.exp(s - m)
    e_new = jnp.exp(s_new - m)
    return e, e_new, jnp.sum(e, axis=-1, keepdims=True) + e_new


CHUNKS_PER_PAGE = PAGE_SIZE // CMP_STRIDE
SWEEP_PAGES = 64


def feature_major_pages(cache_nsa):
    d, n = cache_nsa.shape[:2]
    return cache_nsa.transpose(0, 1, 3, 4, 5, 2).reshape(d * n, 4, N_KV * HEAD_DIM, PAGE_SIZE)


def _cache_compress_body(c_ref, w_ref, o_ref, sk, sv):
    n_pages = c_ref.shape[0]

    def to_row_major(p, carry):
        r0 = pl.multiple_of(p * PAGE_SIZE, PAGE_SIZE)
        sk[pl.ds(r0, PAGE_SIZE), :] = c_ref[p, 0].T
        sv[pl.ds(r0, PAGE_SIZE), :] = c_ref[p, 1].T
        return carry

    lax.fori_loop(0, n_pages, to_row_major, 0, unroll=4)
    n = n_pages * CHUNKS_PER_PAGE
    for c, src in enumerate((sk, sv)):
        x = jnp.concatenate([src[pl.ds(j, n, stride=CMP_STRIDE), :] for j in range(CMP_STRIDE)], axis=1)
        ab = _mm(x, w_ref[0, c])
        o_ref[:, c * LANE:(c + 1) * LANE] = ab[:, 0:LANE]
        o_ref[:, (2 + c) * LANE:(3 + c) * LANE] = ab[:, LANE:2 * LANE]


def cache_compress(cache_fm, nsa_phi):
    n_total = cache_fm.shape[0]
    assert (n_total // DEPTH) % SWEEP_PAGES == 0
    tiles = n_total // DEPTH // SWEEP_PAGES
    wc = jnp.stack([compress_weights_paged(nsa_phi[l]) for l in range(DEPTH)])
    rows = SWEEP_PAGES * PAGE_SIZE
    return pl.pallas_call(
        _cache_compress_body,
        grid=(DEPTH * tiles,),
        in_specs=[pl.BlockSpec((SWEEP_PAGES, 2, LANE, PAGE_SIZE), lambda i: (i, 0, 0, 0)),
                  pl.BlockSpec((1,) + wc.shape[1:], lambda i: (i // tiles, 0, 0, 0))],
        out_specs=pl.BlockSpec((SWEEP_PAGES * CHUNKS_PER_PAGE, 4 * LANE), lambda i: (i, 0)),
        out_shape=jax.ShapeDtypeStruct((n_total * CHUNKS_PER_PAGE, 4 * LANE), F32),
        scratch_shapes=[pltpu.VMEM((rows, LANE), F32), pltpu.VMEM((rows, LANE), F32)],
        compiler_params=_cparams(),
        name="cache_compress",
    )(cache_fm, wc)


SAMPLE_GROUP = 2


def _nsa_sample_body(pt_ref, qa_ref, newb_ref, wnew_ref, gates_ref, *rest):
    n_pg = SAMPLE_GROUP * N_PAGES
    pages, abs_ = rest[:n_pg], rest[n_pg:2 * n_pg]
    y_ref, wout_ref = rest[-2:]
    gens = [_nsa_sample_one(u, qa_ref, newb_ref, wnew_ref, gates_ref, pages[u * N_PAGES:(u + 1) * N_PAGES],
                            abs_[u * N_PAGES:(u + 1) * N_PAGES], *rest[2 * n_pg:-2]) for u in range(SAMPLE_GROUP)]
    outs = [None] * SAMPLE_GROUP
    while any(o is None for o in outs):
        for u, gen in enumerate(gens):
            try:
                next(gen)
            except StopIteration as stop:
                outs[u] = stop.value
    y_ref[...] = jnp.stack([o[0] for o in outs])
    wout_ref[...] = jnp.stack([o[1] for o in outs])


def _nsa_sample_one(u, qa_ref, newb_ref, wnew_ref, gates_ref, pages, abs_, win_ref, bias_ref, gkc_ref, ov_ref, e_ref):
    qs = qa_ref[u]
    newb = newb_ref[u].astype(F32)
    lane = lax.broadcasted_iota(jnp.int32, (QROWS, LANE), 1)
    row = lax.broadcasted_iota(jnp.int32, (QROWS, LANE), 0)

    ab = jnp.concatenate([a[...] for a in abs_], axis=0)
    out = ab[:, 0:2 * LANE] + pltpu.roll(ab[:, 2 * LANE:4 * LANE], N_CHUNK_S - 1, 0) + bias_ref[...]
    kc = _seg_rmsnorm(out[:, 0:LANE], gkc_ref[...])
    vc = out[:, LANE:2 * LANE]
    yield

    s = _mm_nt(qs, kc)
    yield
    s = jnp.where(lane < N_CHUNK_S - 1, s, -jnp.inf)
    e = jnp.exp(s - jnp.max(s, axis=-1, keepdims=True))
    p_cmp = e / jnp.sum(e, axis=-1, keepdims=True)
    yield
    o_cmp = _mm(p_cmp, vc)
    imp = _mm(p_cmp, ov_ref[...])
    yield
    imp = imp +jnp.where(row % GQA == 0, pltpu.roll(imp, QROWS - 1, 0), pltpu.roll(imp, 1, 0))

    valid = lane <= CUR_S
    forced = (lane == 0) | (lane == CUR_S) | (lane == CUR_S - 1)
    score = jnp.where(valid, imp, -jnp.inf)
    score = jnp.where(forced & valid, jnp.inf, score)
    cnt = jnp.zeros((QROWS, LANE), F32)
    for i in range(N_SEL_S):
        ci = score[:, i:i + 1]
        cnt = cnt + jnp.where((ci > score) | ((ci == score) & (lane > i)), 1.0, 0.0)
    sel = jnp.where((cnt < SEL_TOPK) & (score > -jnp.inf), 1.0, 0.0)
    yield

    msel = _mm(sel, e_ref[...])
    s = jnp.concatenate([_mm(qs, pg[0, 0]) for pg in pages], axis=1)
    yield
    s = jnp.where(msel > 0.5, s, NEG)
    qf = qs.astype(F32)
    s_new = jnp.sum(qf * newb[:, 2 * LANE:3 * LANE], axis=-1, keepdims=True)
    s_new = jnp.where(sel[:, CUR_S:CUR_S + 1] > 0.5, s_new, NEG)
    e, e_new, d = _softmax_with_extra(s, s_new)
    yield
    acc_o = e_new.astype(MXU_DTYPE).astype(F32) * newb[:, 3 * LANE:4 * LANE]
    for p, pg in enumerate(pages):
        acc_o = acc_o + _mm_nt(e[:, p * PAGE_SIZE:(p + 1) * PAGE_SIZE], pg[0, 1])
    o_sel = acc_o / d
    yield

    s = _mm(qs, win_ref[u, 0])
    yield
    s_new =jnp.sum(qf * newb[:, 4 * LANE:5 * LANE], axis=-1, keepdims=True)
    e, e_new, d = _softmax_with_extra(s, s_new)
    o_win = (_mm_nt(e, win_ref[u, 1]) + e_new.astype(MXU_DTYPE).astype(F32) * newb[:, 5 * LANE:6 * LANE]) / d

    g = gates_ref[u]
    o = g[:, 0:1] * o_cmp + g[:, 1:2] * o_sel + g[:, 2:3] * o_win
    o_sw = pltpu.roll(o, HEAD_DIM, 1)
    lane1 = lax.broadcasted_iota(jnp.int32, (1, LANE), 1)
    ys = []
    for h in range(N_KV):
        a = (o if h == 0 else o_sw)[GQA * h:GQA * h + 1]
        b = (o if h == 1 else o_sw)[GQA * h + 1:GQA * h + 2]
        ys.append(jnp.where(lane1 < HEAD_DIM, a, b))
    lw = win_ref.shape[3]
    last = lax.broadcasted_iota(jnp.int32, (LANE, lw), 1) == lw - 1
    wouts = []
    for c in range(2):
        col = jnp.broadcast_to(wnew_ref[u][:, c * LANE:(c + 1) * LANE], (QROWS, LANE)).T[:, 0:1]
        wouts.append(jnp.where(last, col, pltpu.roll(win_ref[u, c], lw - 1, 1)))
    return jnp.concatenate(ys, axis=1), jnp.stack(wouts)


def _sample_constants():
    ci = np.arange(LANE)[:, None] * CMP_STRIDE
    sj = np.arange(LANE)[None, :] * SEL_BLOCK
    ov = ((ci < sj + SEL_BLOCK) & (ci + CMP_BLOCK > sj) & (np.arange(LANE)[:, None] < N_CHUNK_S - 1)
          & (np.arange(LANE)[None, :] < N_SEL_S))
    e = (np.arange(LANE)[:, None] == (np.arange(PAST_LEN)[None, :] // SEL_BLOCK))
    return jnp.asarray(ov, MXU_DTYPE), jnp.asarray(e, MXU_DTYPE)


def nsa_sample_pallas(proj, layer, cache_fm, cache_ab, page_table, win_fm, phi_b, qk_g):
    B_ = proj.shape[0]
    n_phys = cache_fm.shape[0] // DEPTH
    lw = win_fm.shape[3]
    assert page_table.shape == (B_, N_PAGES) and lw <= WINDOW and lw <= PAST_LEN and CUR_S == N_SEL_S - 1
    qa, kvb, _, rows_t, _, wnew, gates = nsa_prep(proj, qk_g, 1, B_)
    qa8 = jnp.pad(qa.astype(F32).reshape(B_, N_HEADS, LANE), ((0, 0), (0, QROWS - N_HEADS), (0, 0)))
    gates8 = jnp.pad(gates[:, :3 * N_HEADS].reshape(B_, N_HEADS, 3), ((0, 0), (0, QROWS - N_HEADS), (0, LANE - 3)))
    ov, e = _sample_constants()
    bias = jnp.concatenate([jnp.tile(phi_b[0], 2), jnp.tile(phi_b[1], 2)]).reshape(1, 2 * LANE)

    G = SAMPLE_GROUP
    assert B_ % G == 0
    seq_page = [(u, p) for u in range(G) for p in range(N_PAGES)]

    def page_spec(u, p):
        return pl.BlockSpec((1, 2, LANE, PAGE_SIZE), lambda b, pt: (layer * n_phys + pt[G * b + u, p], 1, 0, 0))

    def ab_spec(u, p):
        return pl.BlockSpec((CHUNKS_PER_PAGE, 4 * LANE), lambda b, pt: (layer * n_phys + pt[G * b + u, p], 0))

    def per_b(shape):
        return pl.BlockSpec((G,) + shape, lambda b, pt: (b, 0, 0))

    def const(a):
        return pl.BlockSpec(a.shape, lambda b, pt: (0,) * a.ndim)

    gkc = jnp.tile(qk_g[1], 2).reshape(1, LANE)
    y, wout = pl.pallas_call(
        _nsa_sample_body,
        grid_spec=pltpu.PrefetchScalarGridSpec(
            num_scalar_prefetch=1,
            grid=(B_ // G,),
            in_specs=[per_b((QROWS, LANE)), per_b((1, 6 * LANE)), per_b((1, 2 * LANE)), per_b((QROWS, LANE))]
                     + [page_spec(u, p) for u, p in seq_page] + [ab_spec(u, p) for u, p in seq_page]
                     + [pl.BlockSpec((G, 2, LANE, lw), lambda b, pt: (layer * (B_ // G) + b, 0, 0, 0)),
                        const(bias), const(gkc), const(ov), const(e)],
            out_specs=[per_b((1, 2 * LANE)), pl.BlockSpec((G, 2, LANE, lw), lambda b, pt: (b, 0, 0, 0))]),
        out_shape=[jax.ShapeDtypeStruct((B_, 1, 2 * LANE), F32),
                   jax.ShapeDtypeStruct((B_, 2, LANE, lw), F32)],
        compiler_params=_cparams(),
        name="nsa_sample",
    )(page_table, qa8, kvb.reshape(B_, 1, 6 * LANE), wnew.reshape(B_, 1, 2 * LANE), gates8,
      *([cache_fm] * (G * N_PAGES)), *([cache_ab] * (G * N_PAGES)), win_fm, bias, gkc, ov, e)
    rows = rows_t.reshape(4, N_KV, HEAD_DIM, B_).transpose(3, 0, 1, 2)[:, None]
    return (y.reshape(B_, 1, N_HEADS * HEAD_DIM), rows,
            wout.reshape(B_, 2, N_KV, HEAD_DIM, lw).transpose(0, 4, 1, 2, 3))


MIX_CHUNK = 512
HALO = 16
YM_W = POOL_W + RG_W + SC_W


def _expm1(x):
    p = jnp.full_like(x, 1.0 / 3628800.0)
    for c in (1.0 / 362880.0, 1.0 / 40320.0, 1.0 / 5040.0, 1.0 / 720.0, 1.0 / 120.0, 1.0 / 24.0, 1.0 / 6.0, 0.5, 1.0):
        p = p * x + c
    return jnp.where(jnp.abs(x) < 0.25, p * x, jnp.exp(x) - 1.0)


def _softplus(x):
    return jnp.maximum(x, 0.0) + jnp.log1p(jnp.exp(-jnp.abs(x)))


def _gelu_tanh(x):
    return 0.5 * x * (1.0 + jnp.tanh(np.sqrt(2.0 / np.pi).astype(np.float32) * (x + 0.044715 * (x * x * x))))


def _rg_coeffs(xc, wa, ba, wx, bx, lam):
    r = jax.nn.sigmoid(_mm(xc, wa) + ba)
    ig = jax.nn.sigmoid(_mm(xc, wx) + bx)
    log_a = (-RG_C * r) * _softplus(-lam)
    return jnp.exp(log_a), jnp.sqrt(-_expm1(2.0 * log_a)) * (ig * xc)


def _pool_select(s2, s4, s8, s16):
    lane = lax.broadcasted_iota(jnp.int32, s2.shape, 1)
    return jnp.where(lane < POOL_GROUP, s2, jnp.where(lane < 2 * POOL_GROUP, s4,
                                                      jnp.where(lane < 3 * POOL_GROUP, s8, s16)))


def _pool_count(pos, shape):
    lane = lax.broadcasted_iota(jnp.int32, shape, 1)
    win = jnp.left_shift(2, lane // POOL_GROUP)
    return jnp.minimum(win, pos + 1).astype(F32)


def _mixers_prompt_body(pu_ref, rx_ref, rg_ref, z_ref, bg_ref, cg_ref, pw_ref, ps_ref, cw_ref, cb_ref, wa_ref, ba_ref,
                        wx_ref, bx_ref, lam_ref, scw_ref, scb_ref, ym_ref, tails_ref, hlast_ref, halo, hcar):
    c = pl.program_id(1)
    tc = pu_ref.shape[0]

    @pl.when(c == 0)
    def _():
        halo[...] = jnp.zeros_like(halo)
        hcar[...] = jnp.zeros_like(hcar)

    pu, rx = pu_ref[...], rx_ref[...]
    u = cg_ref[...] * z_ref[...]
    ext = [jnp.concatenate([halo[i], v], axis=0) for i, v in enumerate((pu, rx, u))]

    def back(e, k):
        return pltpu.roll(e, k, 0)

    s2 = ext[0] + back(ext[0], 1)
    s4 = s2 + back(s2, 2)
    s8 = s4 + back(s4, 4)
    s16 = s8 + back(s8, 8)
    tot = _pool_select(s2, s4, s8, s16)[HALO:]
    pos = c * tc + lax.broadcasted_iota(jnp.int32, (tc, POOL_W), 0)
    d = tot / _pool_count(pos, (tc, POOL_W)) - pu
    ym_ref[:, 0:POOL_W] = _mm(d, pw_ref[...]) * ps_ref[...]

    cw = cw_ref[...]
    xc = cb_ref[...] + cw[RG_CONV - 1:RG_CONV] * rx
    for k in range(1, RG_CONV):
        xc = xc + cw[RG_CONV - 1 - k:RG_CONV - k] * back(ext[1], k)[HALO:]
    a, b = _rg_coeffs(xc, wa_ref[...], ba_ref[...], wx_ref[...], bx_ref[...], lam_ref[...])
    row = lax.broadcasted_iota(jnp.int32, (tc, RG_W), 0)
    k = 1
    while k < tc:
        a_prev = jnp.where(row < k, 1.0, pltpu.roll(a, k, 0))
        b_prev = jnp.where(row < k, 0.0, pltpu.roll(b, k, 0))
        b = a * b_prev + b
        a = a * a_prev
        k *= 2
    h = a * hcar[0:1] + b
    hcar[...] = jnp.broadcast_to(h[tc - 1:tc], hcar.shape)
    hlast_ref[0] = jnp.broadcast_to(h[tc - 1:tc], hcar.shape)
    ym_ref[:, POOL_W:POOL_W + RG_W] = h * _gelu_tanh(rg_ref[...])

    scw = scw_ref[...]
    v = scb_ref[...] + scw[SC_CONV - 1:SC_CONV] * u
    for k in range(1, SC_CONV):
        v = v + scw[SC_CONV - 1 - k:SC_CONV - k] * back(ext[2], k)[HALO:]
    ym_ref[:, POOL_W + RG_W:YM_W] = bg_ref[...] * v

    for i, val in enumerate((pu, rx, u)):
        halo[i] = val[tc - HALO:]
        tails_ref[0, i] = val[tc - HALO:]


def _block_diag(w):
    g, n, _ = w.shape
    return jnp.einsum('gij,gh->gihj', w, jnp.eye(g, dtype=w.dtype)).reshape(g * n, g * n)


def _mixer_params(lw):
    row = lambda a: a.reshape(1, -1)
    return [_block_diag(lw['pool_w']).astype(MXU_DTYPE), row(lw['pool_scale']), lw['rg_conv_w'], row(lw['rg_conv_b']),
            _block_diag(lw['rg_w_a']).astype(MXU_DTYPE), row(lw['rg_b_a']),
            _block_diag(lw['rg_w_x']).astype(MXU_DTYPE), row(lw['rg_b_x']), row(lw['rg_lambda']),
            lw['sc_conv_w'], row(lw['sc_conv_b'])]


def _proj_col_specs(rows, index):
    cols = (COL_POOL, COL_RX, COL_RGATE, COL_SC, COL_SC + SC_W, COL_SC + 2 * SC_W)
    return [pl.BlockSpec((rows, GROUP_W), functools.partial(index, col // GROUP_W)) for col in cols]


def mixers_prompt(proj, lw, B_, S):
    tc = min(MIX_CHUNK, S)
    nc = S // tc
    params = _mixer_params(lw)
    fixed = lambda a: pl.BlockSpec(a.shape, lambda b, c: (0,) * a.ndim)
    return pl.pallas_call(
        _mixers_prompt_body,
        grid=(B_, nc),
        in_specs=_proj_col_specs(tc, lambda col, b, c: (b * nc + c, col)) + [fixed(a) for a in params],
        out_specs=[pl.BlockSpec((tc, YM_W), lambda b, c: (b * nc + c, 0)),
                   pl.BlockSpec((1, 3, HALO, GROUP_W), lambda b, c: (b, 0, 0, 0)),
                   pl.BlockSpec((1, 8, RG_W), lambda b, c: (b, 0, 0))],
        out_shape=[jax.ShapeDtypeStruct((B_ * S, YM_W), F32),
                   jax.ShapeDtypeStruct((B_, 3, HALO, GROUP_W), F32),
                   jax.ShapeDtypeStruct((B_, 8, RG_W), F32)],
        scratch_shapes=[pltpu.VMEM((3, HALO, GROUP_W), F32), pltpu.VMEM((8, RG_W), F32)],
        compiler_params=_cparams(2),
        name="mixers_prompt",
    )(*([proj] * 6), *params)


def _mixers_sample_body(pos0, pu_ref, rx_ref, rg_ref, z_ref, bg_ref, cg_ref, pp_ref, rp_ref, h0_ref, sp_ref, pw_ref,
                        ps_ref, cw_ref, cb_ref, wa_ref, ba_ref, wx_ref, bx_ref, lam_ref, scw_ref, scb_ref,
                        ym_ref, pn_ref, rn_ref, hn_ref, sn_ref):
    pu, rx = pu_ref[...], rx_ref[...]
    u = cg_ref[...] * z_ref[...]
    run, sums = pu, {}
    for k in range(1, POOL_KEEP + 1):
        run = run + pp_ref[POOL_KEEP - k]
        sums[k + 1] = run
    tot = _pool_select(*(sums[w] for w in POOL_WINDOWS))
    d = tot / _pool_count(pos0, pu.shape) - pu
    ym_ref[:, 0:POOL_W] = _mm(d, pw_ref[...]) * ps_ref[...]
    for k in range(POOL_KEEP - 1):
        pn_ref[k] = pp_ref[k + 1]
    pn_ref[POOL_KEEP - 1] = pu

    cw = cw_ref[...]
    xc = cb_ref[...] + cw[RG_CONV - 1:RG_CONV] * rx
    for k in range(RG_CONV - 1):
        xc = xc + cw[k:k + 1] * rp_ref[k]
    a, b = _rg_coeffs(xc, wa_ref[...], ba_ref[...], wx_ref[...], bx_ref[...], lam_ref[...])
    h = b + a * h0_ref[...]
    hn_ref[...] = h
    ym_ref[:, POOL_W:POOL_W + RG_W] = h * _gelu_tanh(rg_ref[...])
    for k in range(RG_CONV - 2):
        rn_ref[k] = rp_ref[k + 1]
    rn_ref[RG_CONV - 2] = rx

    scw = scw_ref[...]
    v = scb_ref[...] + scw[SC_CONV - 1:SC_CONV] * u
    for k in range(SC_CONV - 1):
        v = v + scw[k:k + 1] * sp_ref[k]
    ym_ref[:, POOL_W + RG_W:YM_W] = bg_ref[...] * v
    for k in range(SC_CONV - 2):
        sn_ref[k] = sp_ref[k + 1]
    sn_ref[SC_CONV - 2] = u


def mixers_sample(proj, lw, pos0, pool_prev, rgc_prev, h0, sc_prev):
    B_ = proj.shape[0]
    params = _mixer_params(lw)
    states = [pool_prev.transpose(1, 0, 2), rgc_prev.transpose(1, 0, 2), h0, sc_prev.transpose(1, 0, 2)]
    full = lambda a: pl.BlockSpec(a.shape, lambda i: (0,) * a.ndim)
    ym, pn, rn, hn, sn = pl.pallas_call(
        functools.partial(_mixers_sample_body, pos0),
        grid=(1,),
        in_specs=_proj_col_specs(B_, lambda col, i: (0, col)) + [full(a) for a in states] + [full(a) for a in params],
        out_specs=[pl.BlockSpec((B_, YM_W), lambda i: (0, 0))] + [full(a) for a in states],
        out_shape=[jax.ShapeDtypeStruct((B_, YM_W), F32)] + [jax.ShapeDtypeStruct(a.shape, F32) for a in states],
        compiler_params=_cparams(),
        name="mixers_sample",
    )(*([proj] * 6), *states, *params)
    return ym, pn.transpose(1, 0, 2), rn.transpose(1, 0, 2), hn, sn.transpose(1, 0, 2)


ROUTE_W = LANE
GROUP_LANE0 = N_EXPERTS
MOE_TILE_PROMPT = 256
MOE_TILE_SAMPLE = 32
COMBINE_TILE = 256
FETCH_GROUPS = 8
FETCH_BUFS = 3


def _rms(x, g):
    return x * lax.rsqrt(jnp.mean(x * x, axis=-1, keepdims=True) + EPS) * g


def _mix_out_router_body(ym_ref, yn_ref, x_ref, og_ref, wo_ref, gf_ref, wr_ref, br_ref, tri_ref, x2_ref, xn_ref,
                         route_ref, cnt_ref, cnt_sc):
    og = og_ref[...]
    groups = (ym_ref[:, 0:POOL_W], ym_ref[:, POOL_W:POOL_W + RG_W], yn_ref[...], ym_ref[:, POOL_W + RG_W:YM_W])
    yn = jnp.concatenate([_rms(y, og[:, i * GROUP_W:(i + 1) * GROUP_W]) for i, y in enumerate(groups)], axis=1)
    x2 = x_ref[...] + _mm(yn, wo_ref[...])
    x2_ref[...] = x2
    xn = _rms(x2, gf_ref[...])
    xn_ref[...] = xn
    logits = _mm(xn, wr_ref[...]) + br_ref[...]
    lane = lax.broadcasted_iota(jnp.int32, logits.shape, 1)
    is_grp = (lane >= GROUP_LANE0) & (lane < GROUP_LANE0 + N_GROUPS)
    grp = jnp.where(is_grp, logits, -jnp.inf)
    gmax = jnp.max(grp, axis=-1, keepdims=True)
    gsel = jnp.min(jnp.where(grp == gmax, lane - GROUP_LANE0, N_GROUPS), axis=-1, keepdims=True)
    p_group = 1.0 / jnp.sum(jnp.where(is_grp, jnp.exp(logits - gmax), 0.0), axis=-1, keepdims=True)
    le = jnp.where((lane < N_EXPERTS) & (lane // EXP_PER_GROUP == gsel), logits, -jnp.inf)
    m1 = jnp.max(le, axis=-1, keepdims=True)
    i1 = jnp.min(jnp.where(le == m1, lane, LANE), axis=-1, keepdims=True)
    le2 = jnp.where(lane == i1, -jnp.inf, le)
    m2 = jnp.max(le2, axis=-1, keepdims=True)
    i2 = jnp.min(jnp.where(le2 == m2, lane, LANE), axis=-1, keepdims=True)
    e2 = jnp.exp(m2 - m1)
    g1 = p_group * (1.0 / (1.0 + e2))
    g2 = p_group * (e2 / (1.0 + e2))
    @pl.when(pl.program_id(0) == 0)
    def _():
        cnt_sc[...] = jnp.zeros_like(cnt_sc)

    oh = jnp.where((lane == i1) | (lane == i2), 1.0, 0.0)
    before = cnt_sc[0:1] + _mm(tri_ref[...], oh)
    r1 = jnp.sum(jnp.where(lane == i1, before, 0.0), axis=-1, keepdims=True)
    r2 = jnp.sum(jnp.where(lane == i2, before, 0.0), axis=-1, keepdims=True)
    total = cnt_sc[0:1] + jnp.sum(oh, axis=0, keepdims=True)
    cnt_sc[...] = jnp.broadcast_to(total, cnt_sc.shape)
    cnt_ref[...] = jnp.broadcast_to(total, cnt_ref.shape)
    vals = (i1.astype(F32), i2.astype(F32), g1, g2, r1, r2)
    route = jnp.zeros(logits.shape, F32)
    for k, v in enumerate(vals):
        route = jnp.where(lane == k, v, route)
    route_ref[...] = route


def mix_out_router(ym, y_nsa, x2d, lw):
    T, D = x2d.shape
    tm = min(256, T)
    wr = jnp.concatenate([lw['router_expert_w'], lw['router_group_w'],
                          jnp.zeros((D, ROUTE_W - N_EXPERTS - N_GROUPS), F32)], axis=1).astype(MXU_DTYPE)
    br = jnp.concatenate([lw['router_expert_b'], lw['router_group_b'],
                          jnp.zeros((ROUTE_W - N_EXPERTS - N_GROUPS,), F32)]).reshape(1, ROUTE_W)
    row = lambda i: (i, 0)
    fixed = lambda i: (0, 0)
    tri = jnp.asarray(np.tril(np.ones((tm, tm), np.float32), -1), MXU_DTYPE)
    return pl.pallas_call(
        _mix_out_router_body,
        grid=(T // tm,),
        in_specs=[pl.BlockSpec((tm, YM_W), row), pl.BlockSpec((tm, GROUP_W), row), pl.BlockSpec((tm, D), row),
                  pl.BlockSpec((1, MIX_W), fixed),
                  pl.BlockSpec((MIX_W, D), fixed), pl.BlockSpec((1, D), fixed), pl.BlockSpec((D, ROUTE_W), fixed),
                  pl.BlockSpec((1, ROUTE_W), fixed), pl.BlockSpec((tm, tm), fixed)],
        out_specs=[pl.BlockSpec((tm, D), row), pl.BlockSpec((tm, D), row), pl.BlockSpec((tm, ROUTE_W), row),
                   pl.BlockSpec((8, ROUTE_W), fixed)],
        out_shape=[jax.ShapeDtypeStruct((T, D), F32), jax.ShapeDtypeStruct((T, D), F32),
                   jax.ShapeDtypeStruct((T, ROUTE_W), F32), jax.ShapeDtypeStruct((8, ROUTE_W), F32)],
        scratch_shapes=[pltpu.VMEM((8, ROUTE_W), F32)],
        compiler_params=_cparams(),
        name="mix_out_router",
    )(ym, y_nsa, x2d, lw['mix_out_g'].reshape(1, MIX_W), lw['w_out'].astype(MXU_DTYPE),
      lw['norm_ffn_g'].reshape(1, D), wr, br, tri)


def moe_schedule(route, counts, tile):
    T = route.shape[0]
    M = T * TOP_E
    fe = route[:, 0:TOP_E].astype(jnp.int32).reshape(M)
    rank = route[:, 4:4 + TOP_E].astype(jnp.int32).reshape(M)
    counts = counts.astype(jnp.int32)
    padded = (counts + tile - 1) // tile * tile
    pad_end = jnp.cumsum(padded)
    dest = ((pad_end - padded)[fe] + rank).astype(jnp.int32)
    n_blk = -(-M // tile) + N_EXPERTS
    tok = jnp.arange(M, dtype=jnp.int32) // TOP_E
    buf_tok = jnp.zeros((n_blk * tile,), jnp.int32).at[dest].set(tok)
    blk_exp = jnp.minimum(jnp.sum(pad_end[None, :] <= (jnp.arange(n_blk, dtype=jnp.int32) * tile)[:, None], axis=1),
                          N_EXPERTS - 1).astype(jnp.int32)
    n_used = (pad_end[-1:] // tile).astype(jnp.int32)
    return buf_tok, blk_exp, n_used, dest


def _moe_ffn_body(tile, tok_ref, bexp_ref, nused_ref, x_hbm, wgu_ref, wdn_ref, y_ref, xg, sem, wgu_bf, wdn_bf):
    j = pl.program_id(0)
    n = nused_ref[0]

    def gather(blk, slot):
        def body(r, c):
            t = tok_ref[blk * tile + r]
            pltpu.make_async_copy(x_hbm.at[pl.ds(t, 1)], xg.at[slot, pl.ds(r, 1)], sem.at[slot]).start()
            return c
        lax.fori_loop(0, tile, body, 0, unroll=8)

    @pl.when((j == 0) & (n > 0))
    def _():
        gather(0, 0)
        gather(jnp.minimum(1, n - 1), 1)

    def wait_block(slot):
        pltpu.make_async_copy(x_hbm.at[pl.ds(0, tile)], xg.at[slot], sem.at[slot]).wait()

    @pl.when(j < n)
    def _():
        slot = j % FETCH_BUFS
        nxt = jnp.minimum(j + 2, n - 1)
        dst = (j + 2) % FETCH_BUFS

        def fetch_group(g):
            per = tile // FETCH_GROUPS
            for r in range(g * per, (g + 1) * per):
                t = tok_ref[nxt * tile + r]
                pltpu.make_async_copy(x_hbm.at[pl.ds(t, 1)], xg.at[dst, pl.ds(r, 1)], sem.at[dst]).start()

        @pl.when((j == 0) | (bexp_ref[j] != bexp_ref[jnp.maximum(j - 1, 0)]))
        def _():
            wgu_bf[...] = wgu_ref[0].astype(wgu_bf.dtype)
            wdn_bf[...] = wdn_ref[0].astype(wdn_bf.dtype)

        wait_block(slot)
        x = xg[slot]
        half = FETCH_GROUPS // 2
        cg, cd = 2 * D_EXPERT // half, y_ref.shape[1] // half
        hs = []
        for c in range(half):
            fetch_group(c)
            hs.append(_mm(x, wgu_bf[:, c * cg:(c + 1) * cg]))
        h = jnp.concatenate(hs, axis=1)
        a, b = h[:, :D_EXPERT], h[:, D_EXPERT:]
        act = a * jax.nn.sigmoid(a) * b
        for c in range(half):
            fetch_group(half + c)
            y_ref[:, c * cd:(c + 1) * cd] = _mm(act, wdn_bf[:, c * cd:(c + 1) * cd])

        @pl.when(j + 1 >= n)
        def _():
            wait_block((j + 1) % FETCH_BUFS)
            wait_block(dst)

    @pl.when(j >= n)
    def _():
        y_ref[...] = jnp.zeros_like(y_ref)


def moe_ffn_pallas(xn, buf_tok, blk_exp, n_used, w_gu, w_down, tile):
    T, D = xn.shape
    n_blk = blk_exp.shape[0]
    return pl.pallas_call(
        functools.partial(_moe_ffn_body, tile),
        grid_spec=pltpu.PrefetchScalarGridSpec(
            num_scalar_prefetch=3,
            grid=(n_blk,),
            in_specs=[pl.BlockSpec(memory_space=pl.ANY),
                      pl.BlockSpec((1, D, 2 * D_EXPERT), lambda j, tok, bexp, nu: (bexp[j], 0, 0)),
                      pl.BlockSpec((1, D_EXPERT, D), lambda j, tok, bexp, nu: (bexp[j], 0, 0))],
            out_specs=pl.BlockSpec((tile, D), lambda j, tok, bexp, nu: (j, 0)),
            scratch_shapes=[pltpu.VMEM((FETCH_BUFS, tile, D), F32), pltpu.SemaphoreType.DMA((FETCH_BUFS,)),
                            pltpu.VMEM((D, 2 * D_EXPERT), MXU_DTYPE), pltpu.VMEM((D_EXPERT, D), MXU_DTYPE)]),
        out_shape=jax.ShapeDtypeStruct((n_blk * tile, D), F32),
        compiler_params=_cparams(),
        name="moe_ffn",
    )(buf_tok, blk_exp, n_used, xn, w_gu, w_down)


def _moe_combine_body(tm, slots_ref, y_hbm, x2_ref, route_ref, o_ref, yb, sem):
    i = pl.program_id(0)
    nt = pl.num_programs(0)

    def gather(tile_i, buf):
        def body(r, c):
            for k in range(TOP_E):
                s = slots_ref[(tile_i * tm + r) * TOP_E + k]
                pltpu.make_async_copy(y_hbm.at[pl.ds(s, 1)], yb.at[buf, k, pl.ds(r, 1)], sem.at[buf]).start()
            return c
        lax.fori_loop(0, tm, body, 0, unroll=8)

    def wait_tile(b):
        for k in range(TOP_E):
            pltpu.make_async_copy(y_hbm.at[pl.ds(0, tm)], yb.at[b, k], sem.at[b]).wait()

    @pl.when(i == 0)
    def _():
        gather(0, 0)
        gather(jnp.minimum(1, nt - 1), 1)

    buf = i % FETCH_BUFS
    ahead = (i + 2) % FETCH_BUFS
    gather(jnp.minimum(i + 2, nt - 1), ahead)
    wait_tile(buf)
    r = route_ref[...]
    o_ref[...] = x2_ref[...] + (r[:, 2:3] * yb[buf, 0] + r[:, 3:4] * yb[buf, 1])

    @pl.when(i + 1 >= nt)
    def _():
        wait_tile((i + 1) % FETCH_BUFS)
        wait_tile(ahead)


def moe_combine_pallas(y, slots, x2, route):
    T, D = x2.shape
    tm = min(COMBINE_TILE, T)
    return pl.pallas_call(
        functools.partial(_moe_combine_body, tm),
        grid_spec=pltpu.PrefetchScalarGridSpec(
            num_scalar_prefetch=1,
            grid=(T // tm,),
            in_specs=[pl.BlockSpec(memory_space=pl.ANY),
                      pl.BlockSpec((tm, D), lambda i, s: (i, 0)),
                      pl.BlockSpec((tm, ROUTE_W), lambda i, s: (i, 0))],
            out_specs=pl.BlockSpec((tm, D), lambda i, s: (i, 0)),
            scratch_shapes=[pltpu.VMEM((FETCH_BUFS, TOP_E, tm, D), F32), pltpu.SemaphoreType.DMA((FETCH_BUFS,))]),
        out_shape=jax.ShapeDtypeStruct((T, D), F32),
        compiler_params=_cparams(),
        name="moe_combine",
    )(slots, y, x2, route)


def mix_out_moe(ym, y_nsa, x2d, lw, tile):
    x2, xn, route, counts = mix_out_router(ym, y_nsa, x2d, lw)
    buf_tok, blk_exp, n_used, slots = moe_schedule(route, counts[0, :N_EXPERTS], tile)
    y = moe_ffn_pallas(xn, buf_tok, blk_exp + lw['expert_base'], n_used, lw['exp_w_gu'], lw['exp_w_down'], tile)
    return moe_combine_pallas(y, slots, x2, route)


def split_cols(a, sizes):
    outs, o = [], 0
    for s in sizes:
        outs.append(a[..., o:o + s])
        o += s
    return outs


def layer_forward(x, pos0, lw, pool_prev, rgc_prev, rgh0, sc_prev, nsa_fn):
    B_, L, _ = x.shape
    w_perm = permute_w_in(lw['w_in']).astype(MXU_DTYPE)
    proj2d = norm_matmul(x.reshape(B_ * L, D_MODEL), lw['norm_mix_g'], w_perm)
    if pool_prev is None:
        ym, tails, hlast = mixers_prompt(proj2d, lw, B_, L)
        pool_new = tails[:, 0, HALO - POOL_KEEP:]
        rgc_new = tails[:, 1, HALO - (RG_CONV - 1):]
        sc_new = tails[:, 2, HALO - (SC_CONV - 1):]
        rgh_new = hlast[:, 0]
    else:
        ym, pool_new, rgc_new, rgh_new, sc_new = mixers_sample(proj2d, lw, pos0, pool_prev, rgc_prev, rgh0, sc_prev)
    y_nsa, nsa_rows, win_new = nsa_fn(proj2d, lw['nsa_phi'], lw['nsa_phi_b'], lw['nsa_qk_g'])
    x = mix_out_moe(ym, y_nsa.reshape(B_ * L, GROUP_W), x.reshape(B_ * L, D_MODEL), lw,
                    MOE_TILE_PROMPT if L > 1 else MOE_TILE_SAMPLE)
    return x.reshape(B_, L, D_MODEL), (nsa_rows, win_new, pool_new, rgc_new, rgh_new, sc_new)


def kernel(x_prompt, x_sample, cache_nsa, state_win_kv, state_pool, state_rg_conv, state_rg_h, state_sc_conv,
           page_table, norm_mix_g, w_in, pool_w, pool_scale, rg_conv_w, rg_conv_b, rg_w_a, rg_b_a, rg_w_x, rg_b_x,
           rg_lambda, nsa_phi, nsa_phi_b, nsa_qk_g, sc_conv_w, sc_conv_b, mix_out_g, w_out, norm_ffn_g,
           router_group_w, router_group_b, router_expert_w, router_expert_b, exp_w_gu, exp_w_down):
    past_len = page_table.shape[1] * cache_nsa.shape[2]
    xp, xs = x_prompt, x_sample
    cache3 = feature_major_pages(cache_nsa)
    win3 = state_win_kv.transpose(0, 1, 3, 4, 5, 2).reshape(DEPTH * state_win_kv.shape[1], 2, N_KV * HEAD_DIM,
                                                             state_win_kv.shape[2])
    cache_ab = cache_compress(cache3, nsa_phi)
    Bp = xp.shape[0]
    st_p, st_s = [], []
    for l in range(DEPTH):
        lw = dict(norm_mix_g=norm_mix_g[l], w_in=w_in[l], pool_w=pool_w[l], pool_scale=pool_scale[l],
                  rg_conv_w=rg_conv_w[l], rg_conv_b=rg_conv_b[l], rg_w_a=rg_w_a[l], rg_b_a=rg_b_a[l],
                  rg_w_x=rg_w_x[l], rg_b_x=rg_b_x[l], rg_lambda=rg_lambda[l], nsa_phi=nsa_phi[l],
                  nsa_phi_b=nsa_phi_b[l], nsa_qk_g=nsa_qk_g[l], sc_conv_w=sc_conv_w[l], sc_conv_b=sc_conv_b[l],
                  mix_out_g=mix_out_g[l], w_out=w_out[l], norm_ffn_g=norm_ffn_g[l],
                  router_group_w=router_group_w[l], router_group_b=router_group_b[l],
                  router_expert_w=router_expert_w[l], router_expert_b=router_expert_b[l],
                  exp_w_gu=exp_w_gu.reshape((DEPTH * N_EXPERTS,) + exp_w_gu.shape[2:]),
                  exp_w_down=exp_w_down.reshape((DEPTH * N_EXPERTS,) + exp_w_down.shape[2:]),
                  expert_base=l * N_EXPERTS)
        xp, sp = layer_forward(xp, 0, lw, None, None, None, None,
                               lambda p, phi, phi_b, g: nsa_prompt_pallas(p, Bp, xp.shape[1], phi, phi_b, g))
        xs, ss = layer_forward(xs, past_len, lw, state_pool[l], state_rg_conv[l], state_rg_h[l], state_sc_conv[l],
                               lambda p, phi, phi_b, g: nsa_sample_pallas(p, l, cache3, cache_ab, page_table, win3,
                                                                          phi_b, g))
        st_p.append(sp)
        st_s.append(ss)

    def stk(lst, i):
        return jnp.stack([s[i] for s in lst])

    return (xp, xs, stk(st_p, 0), stk(st_s, 0), stk(st_p, 1), stk(st_s, 1), stk(st_p, 2), stk(st_s, 2),
            stk(st_p, 3), stk(st_s, 3), stk(st_p, 4), stk(st_s, 4), stk(st_p, 5), stk(st_s, 5))
```

```python
import functools
import jax, jax.numpy as jnp
from jax import lax
import numpy as np
from jax.experimental import pallas as pl
from jax.experimental.pallas import tpu as pltpu

D_MODEL = 1024
BATCH = 4
SEQ = 4096
DEPTH = 2
DEC_BATCH = 128
DEC_SEQ = 1
PAST_LEN = 2048
PAGE_SIZE = 128

MIX_W = D_MODEL
GROUP_W = MIX_W // 4
POOL_W = GROUP_W
POOL_WINDOWS = (2, 4, 8, 16)
POOL_GROUP = POOL_W // len(POOL_WINDOWS)
POOL_KEEP = max(POOL_WINDOWS) - 1
RG_W = GROUP_W
RG_HEADS = 4
RG_BLOCK = RG_W // RG_HEADS
RG_CONV = 4
RG_C = 8.0
HEAD_DIM = 64
N_HEADS = GROUP_W // HEAD_DIM
N_KV = 2
GQA = N_HEADS // N_KV
CMP_BLOCK = 32
CMP_STRIDE = 16
SEL_BLOCK = 64
SEL_TOPK = 16
WINDOW = 512
Q_BLOCK = 128
SC_W = GROUP_W
SC_CONV = 3
N_GROUPS = 4
EXP_PER_GROUP = 8
N_EXPERTS = N_GROUPS * EXP_PER_GROUP
TOP_E = 2
D_EXPERT = 512
MOE_BLOCK = 128
EPS = 1e-6
SPLIT_SIZES = (POOL_W, RG_W, RG_W, N_HEADS * HEAD_DIM, 6 * N_KV * HEAD_DIM, 3 * N_HEADS, 3 * SC_W)
N_IN = sum(SPLIT_SIZES)

LANE = 128
ROW_TILE = 512
VMEM_LIMIT = 48 * 1024 * 1024
MXU_DTYPE = jnp.bfloat16
F32 = jnp.float32
NEG = -1e30

KV_W = 6 * N_KV * HEAD_DIM
COL_Q = 0
COL_KV = COL_Q + N_HEADS * HEAD_DIM
COL_POOL = COL_KV + KV_W
COL_RX = COL_POOL + POOL_W
COL_RGATE = COL_RX + RG_W
COL_SC = COL_RGATE + RG_W
COL_NG = COL_SC + 3 * SC_W
N_IN_PAD = COL_NG + LANE
SEL_TILE = 512
N_SEL_PROMPT = SEQ // SEL_BLOCK


def _cparams(n_axes=1):
    return pltpu.CompilerParams(dimension_semantics=("arbitrary",) * n_axes, vmem_limit_bytes=VMEM_LIMIT)


def _mm(a, b):
    return jnp.dot(a.astype(MXU_DTYPE), b.astype(MXU_DTYPE), preferred_element_type=F32)


def _mm_nt(a, b):
    return lax.dot_general(a.astype(MXU_DTYPE), b.astype(MXU_DTYPE), (((1,), (1,)), ((), ())),
                           preferred_element_type=F32)


def permute_w_in(w):
    pu, rx, rgate, q, kv, ng, sc = split_cols(w, SPLIT_SIZES)
    pad = jnp.zeros((w.shape[0], LANE - ng.shape[1]), w.dtype)
    return jnp.concatenate([q, kv, pu, rx, rgate, sc, ng, pad], axis=1)


def _norm_matmul_body(x_ref, g_ref, w_ref, o_ref):
    xf = x_ref[...]
    h = xf * lax.rsqrt(jnp.mean(xf * xf, axis=-1, keepdims=True) + EPS) * g_ref[...]
    o_ref[...] = _mm(h, w_ref[...])


def norm_matmul(x2d, g, w):
    T, D = x2d.shape
    N = w.shape[1]
    tm = min(ROW_TILE, T)
    return pl.pallas_call(
        _norm_matmul_body,
        grid=(T // tm,),
        in_specs=[pl.BlockSpec((tm, D), lambda i: (i, 0)),
                  pl.BlockSpec((1, D), lambda i: (0, 0)),
                  pl.BlockSpec((D, N), lambda i: (0, 0))],
        out_specs=pl.BlockSpec((tm, N), lambda i: (i, 0)),
        out_shape=jax.ShapeDtypeStruct((T, N), F32),
        compiler_params=_cparams(),
        name="norm_in_proj",
    )(x2d, g.reshape(1, D), w)


def _seg_rmsnorm(x, g):
    x2 = x * x
    left = lax.broadcasted_iota(jnp.int32, x.shape, 1) < HEAD_DIM
    s_l = jnp.sum(jnp.where(left, x2, 0.0), axis=-1, keepdims=True)
    s_r = jnp.sum(jnp.where(left, 0.0, x2), axis=-1, keepdims=True)
    ms = jnp.where(left, s_l, s_r) * (1.0 / HEAD_DIM)
    return x * lax.rsqrt(ms + EPS) * g


def _nsa_prep_body(qkv_ref, ng_ref, g_ref, perm_ref, qa_ref, kvb_ref, rawb_ref, rows_t_ref, win_t_ref, win_ref,
                   gates_ref):
    g = g_ref[...]
    for hb in range(N_KV):
        qn = _seg_rmsnorm(qkv_ref[:, COL_Q + hb * LANE:COL_Q + (hb + 1) * LANE], g[0:1]) * (HEAD_DIM ** -0.5)
        qa_ref[:, hb * 2 * LANE:(hb + 1) * 2 * LANE] = _mm(qn, perm_ref[hb]).astype(qa_ref.dtype)
    comp = [qkv_ref[:, COL_KV + c * LANE:COL_KV + (c + 1) * LANE] for c in range(6)]
    comp[2] = _seg_rmsnorm(comp[2], g[2:3])
    comp[4] = _seg_rmsnorm(comp[4], g[3:4])
    for c in range(6):
        kvb_ref[:, c * LANE:(c + 1) * LANE] = comp[c].astype(kvb_ref.dtype)
    for c in range(2):
        rawb_ref[:, c * LANE:(c + 1) * LANE] = comp[c].astype(rawb_ref.dtype)
    for c in range(4):
        rows_t_ref[0, c * LANE:(c + 1) * LANE, :] = comp[c].T
    for c in range(2):
        win_t_ref[0, c * LANE:(c + 1) * LANE, :] = comp[4 + c].T
        win_ref[:, c * LANE:(c + 1) * LANE] = comp[4 + c]
    gates_ref[...] = jax.nn.sigmoid(ng_ref[...])


def _q_place_matrices():
    p = np.zeros((N_KV, LANE, 2 * LANE), np.float32)
    for hb in range(N_KV):
        for gq in range(GQA):
            for d in range(HEAD_DIM):
                p[hb, gq * HEAD_DIM + d, gq * LANE + hb * HEAD_DIM + d] = 1.0
    return jnp.asarray(p, MXU_DTYPE)


def nsa_prep(proj, qk_g, B_, S):
    T = proj.shape[0]
    tm = min(ROW_TILE, S)
    tpb = S // tm
    qkv_w = COL_POOL
    g4 = jnp.tile(qk_g, (1, 2))
    return pl.pallas_call(
        _nsa_prep_body,
        grid=(T // tm,),
        in_specs=[pl.BlockSpec((tm, qkv_w), lambda i: (i, 0)),
                  pl.BlockSpec((tm, LANE), lambda i: (i, COL_NG // LANE)),
                  pl.BlockSpec((4, LANE), lambda i: (0, 0)),
                  pl.BlockSpec((N_KV, LANE, 2 * LANE), lambda i: (0, 0, 0))],
        out_specs=[pl.BlockSpec((tm, 4 * LANE), lambda i: (i, 0)),
                   pl.BlockSpec((tm, 6 * LANE), lambda i: (i, 0)),
                   pl.BlockSpec((tm, 2 * LANE), lambda i: (i, 0)),
                   pl.BlockSpec((1, 4 * LANE, tm), lambda i: (i // tpb, 0, i % tpb)),
                   pl.BlockSpec((1, 2 * LANE, tm), lambda i: (i // tpb, 0, i % tpb)),
                   pl.BlockSpec((tm, 2 * LANE), lambda i: (i, 0)),
                   pl.BlockSpec((tm, LANE), lambda i: (i, 0))],
        out_shape=[jax.ShapeDtypeStruct((T, 4 * LANE), MXU_DTYPE),
                   jax.ShapeDtypeStruct((T, 6 * LANE), MXU_DTYPE),
                   jax.ShapeDtypeStruct((T, 2 * LANE), MXU_DTYPE),
                   jax.ShapeDtypeStruct((B_, 4 * LANE, S), F32),
                   jax.ShapeDtypeStruct((B_, 2 * LANE, S), F32),
                   jax.ShapeDtypeStruct((T, 2 * LANE), F32),
                   jax.ShapeDtypeStruct((T, LANE), F32)],
        compiler_params=_cparams(),
        name="nsa_prep",
    )(proj, proj, g4, _q_place_matrices())


def compress_weights(phi):
    R = CMP_BLOCK // CMP_STRIDE
    wr = phi.reshape(2, R, CMP_STRIDE, HEAD_DIM, HEAD_DIM)
    eye = jnp.eye(2, dtype=phi.dtype)
    w = jnp.einsum('crjde,cx,hy->rjchdxye', wr, eye, eye)
    return w.reshape(R, CMP_STRIDE * 2 * LANE, 2 * LANE).astype(MXU_DTYPE)


def _compress_body(x_ref, w_ref, b_ref, g_ref, kc_ref, vc_ref):
    x = x_ref[0]
    nch = x.shape[0]
    a = _mm(x, w_ref[0])
    bm = _mm(x, w_ref[1])
    out = a + pltpu.roll(bm, nch - 1, 0) + b_ref[...]
    kc_ref[0] = _seg_rmsnorm(out[:, 0:LANE], g_ref[...]).astype(kc_ref.dtype)
    vc_ref[0] = out[:, LANE:2 * LANE].astype(vc_ref.dtype)


def nsa_compress_pallas(rawb3, wc, phi_b, g_kc):
    B_, nch, K = rawb3.shape
    bias = jnp.concatenate([jnp.tile(phi_b[0], 2), jnp.tile(phi_b[1], 2)]).reshape(1, 2 * LANE)
    return pl.pallas_call(
        _compress_body,
        grid=(B_,),
        in_specs=[pl.BlockSpec((1, nch, K), lambda b: (b, 0, 0)),
                  pl.BlockSpec(wc.shape, lambda b: (0, 0, 0)),
                  pl.BlockSpec((1, 2 * LANE), lambda b: (0, 0)),
                  pl.BlockSpec((1, LANE), lambda b: (0, 0))],
        out_specs=[pl.BlockSpec((1, nch, LANE), lambda b: (b, 0, 0)),
                   pl.BlockSpec((1, nch, LANE), lambda b: (b, 0, 0))],
        out_shape=[jax.ShapeDtypeStruct((B_, nch, LANE), MXU_DTYPE),
                   jax.ShapeDtypeStruct((B_, nch, LANE), MXU_DTYPE)],
        compiler_params=_cparams(),
        name="nsa_compress",
    )(rawb3, wc, bias, jnp.tile(g_kc, 2).reshape(1, LANE))


def _online_update(carry, s, v):
    m, l, acc = carry
    m_new = jnp.maximum(m, jnp.max(s, axis=-1, keepdims=True))
    alpha = jnp.exp(m - m_new)
    p = jnp.exp(s - m_new)
    l = alpha * l + jnp.sum(p, axis=-1, keepdims=True)
    acc = alpha * acc + _mm(p, v)
    return m_new, l, acc


def _select_blocks(imp, start):
    n_sel = N_SEL_PROMPT
    sc_t = imp.T[0:n_sel]
    blk = lax.broadcasted_iota(jnp.int32, sc_t.shape, 0)
    cur = (start + lax.broadcasted_iota(jnp.int32, sc_t.shape, 1)) // SEL_BLOCK
    valid = blk <= cur
    forced = (blk == 0) | (blk == cur) | (blk == cur - 1)
    score = jnp.where(valid, sc_t, -jnp.inf)
    score = jnp.where(forced & valid, jnp.inf, score)
    sub = 8
    groups = [score[g * sub:(g + 1) * sub] for g in range(n_sel // sub)]
    cnts = [jnp.zeros((sub, sc_t.shape[1]), F32) for _ in groups]
    row = lax.broadcasted_iota(jnp.int32, (sub, sc_t.shape[1]), 0)
    for i in range(n_sel):
        ri = score[i:i + 1, :]
        for g, sg in enumerate(groups):
            if (g + 1) * sub - 1 < i:
                beat = jnp.where(ri > sg, 1.0, 0.0)
            elif g * sub > i:
                beat = jnp.where(ri >= sg, 1.0, 0.0)
            else:
                beat = jnp.where(row + g * sub > i, jnp.where(ri >= sg, 1.0, 0.0), jnp.where(ri > sg, 1.0, 0.0))
            cnts[g] = cnts[g] + beat
    cnt = jnp.concatenate(cnts, axis=0)
    sel_t = jnp.where((cnt < SEL_TOPK) & (score > -jnp.inf), 1.0, 0.0)
    sel_t = jnp.concatenate([sel_t, jnp.zeros((LANE - n_sel, sc_t.shape[1]), F32)], axis=0)
    return sel_t.T


def _nsa_attn_body(qa_ref, gates_ref, kc_ref, vc_ref, kv_ref, ov_ref, e_ref, o_ref):
    i = pl.program_id(1)
    start = i * Q_BLOCK
    Q = Q_BLOCK
    R = GQA * Q
    t_row = start + lax.broadcasted_iota(jnp.int32, (R, 1), 0) % Q
    gates = gates_ref[...]
    lane_q = lax.broadcasted_iota(jnp.int32, (Q, LANE), 1)
    heads = range(N_KV)
    qs = [jnp.concatenate([qa_ref[:, (h * GQA + gq) * LANE:(h * GQA + gq + 1) * LANE] for gq in range(GQA)], axis=0)
          for h in heads]

    o_cmps, sel_bias = [], []
    kc = kc_ref[0]
    ncmp = kc.shape[0]
    cmp_end = lax.broadcasted_iota(jnp.int32, (R, ncmp), 1) * CMP_STRIDE + (CMP_BLOCK - 1)
    for h in heads:
        s = jnp.where(cmp_end <= t_row, _mm_nt(qs[h], kc), -jnp.inf)
        m = jnp.max(s, axis=-1, keepdims=True)
        e = jnp.exp(s - jnp.where(m > -jnp.inf, m, 0.0))
        d = jnp.sum(e, axis=-1, keepdims=True)
        p_cmp = e / jnp.where(d > 0, d, 1.0)
        o_cmps.append(_mm(p_cmp, vc_ref[0]))
        imp = _mm(p_cmp[0:Q], ov_ref[...]) + _mm(p_cmp[Q:R], ov_ref[...])
        sel = _select_blocks(imp, start)
        bias = jnp.concatenate([jnp.where(sel > 0.5, 0.0, NEG)] * GQA, axis=0)
        sel_bias.append(jnp.concatenate([qs[h], bias.astype(MXU_DTYPE)], axis=1))

    def sel_scores(j):
        off = pl.multiple_of(j * SEL_TILE, SEL_TILE)
        k = jnp.concatenate([kv_ref[pl.ds(off, SEL_TILE), 2 * LANE:3 * LANE], e_ref[pl.ds(off, SEL_TILE), :]], axis=1)
        v = kv_ref[pl.ds(off, SEL_TILE), 3 * LANE:4 * LANE]
        return off, v, [_mm_nt(sel_bias[h], k) for h in heads]

    def sel_step(j, carry):
        _, v, ss = sel_scores(j)
        return tuple(_online_update(carry[h], ss[h], v) for h in heads)

    init = (jnp.full((R, 1), NEG, F32), jnp.zeros((R, 1), F32), jnp.zeros((R, LANE), F32))
    n_tiles = (start + Q + SEL_TILE - 1) // SEL_TILE
    carry = lax.fori_loop(0, n_tiles - 1, sel_step, (init,) * N_KV)
    off, v, ss = sel_scores(n_tiles - 1)
    causal = off + lax.broadcasted_iota(jnp.int32, (R, SEL_TILE), 1) <= t_row
    o_sels = []
    for h in heads:
        _, l_s, acc_s = _online_update(carry[h], jnp.where(causal, ss[h], NEG), v)
        o_sels.append(acc_s / l_s)

    n_w = WINDOW // Q + 1
    offs = [pl.multiple_of(jnp.maximum(i - kk, 0) * Q, Q) for kk in range(n_w)]
    kw = jnp.concatenate([kv_ref[pl.ds(o, Q), 4 * LANE:5 * LANE] for o in offs], axis=0)
    vw = jnp.concatenate([kv_ref[pl.ds(o, Q), 5 * LANE:6 * LANE] for o in offs], axis=0)
    lane_w = lax.broadcasted_iota(jnp.int32, (1, n_w * Q), 1)
    w_pos = (i - lane_w // Q) * Q + lane_w % Q
    wd = t_row - w_pos
    wmask = (w_pos >= 0) & (wd >= 0) & (wd <= WINDOW)
    o_wins = []
    for h in heads:
        s = jnp.where(wmask, _mm_nt(qs[h], kw), NEG)
        p = jnp.exp(s - jnp.max(s, axis=-1, keepdims=True))
        o_wins.append(_mm(p, vw) / jnp.sum(p, axis=-1, keepdims=True))

    for h in heads:
        o_cmp, o_sel, o_win = o_cmps[h], o_sels[h], o_wins[h]
        outs = []
        for gq in range(GQA):
            c0 = (h * GQA + gq) * 3
            rs = slice(gq * Q, (gq + 1) * Q)
            og = (gates[:, c0:c0 + 1] * o_cmp[rs] + gates[:, c0 + 1:c0 + 2] * o_sel[rs]
                  + gates[:, c0 + 2:c0 + 3] * o_win[rs])
            outs.append(og if gq == h else pltpu.roll(og, HEAD_DIM, 1))
        o_ref[:, h * LANE:(h + 1) * LANE] = jnp.where(lane_q < HEAD_DIM, outs[0], outs[1])


def _sel_constants(S):
    ncmp_rows = S // CMP_STRIDE
    ci = np.arange(ncmp_rows)[:, None] * CMP_STRIDE
    sj = np.arange(LANE)[None, :] * SEL_BLOCK
    ov = ((ci < sj + SEL_BLOCK) & (ci + CMP_BLOCK > sj) & (np.arange(LANE)[None, :] < S // SEL_BLOCK))
    e = (np.arange(S)[:, None] // SEL_BLOCK == np.arange(LANE)[None, :])
    return jnp.asarray(ov, MXU_DTYPE), jnp.asarray(e, MXU_DTYPE)


def nsa_attn_prompt(qa, gates, kc, vc, kvb, B_, S):
    nq = S // Q_BLOCK
    nch = S // CMP_STRIDE
    ov, e3 = _sel_constants(S)
    return pl.pallas_call(
        _nsa_attn_body,
        grid=(B_, nq),
        in_specs=[pl.BlockSpec((Q_BLOCK, 4 * LANE), lambda b, i: (b * nq + i, 0)),
                  pl.BlockSpec((Q_BLOCK, LANE), lambda b, i: (b * nq + i, 0)),
                  pl.BlockSpec((1, nch, LANE), lambda b, i: (b, 0, 0)),
                  pl.BlockSpec((1, nch, LANE), lambda b, i: (b, 0, 0)),
                  pl.BlockSpec((S, 6 * LANE), lambda b, i: (b, 0)),
                  pl.BlockSpec(ov.shape, lambda b, i: (0, 0)),
                  pl.BlockSpec(e3.shape, lambda b, i: (0, 0))],
        out_specs=pl.BlockSpec((Q_BLOCK, 2 * LANE), lambda b, i: (b * nq + i, 0)),
        out_shape=jax.ShapeDtypeStruct((B_ * S, N_HEADS * HEAD_DIM), F32),
        compiler_params=_cparams(2),
        name="nsa_attn_prompt",
    )(qa, gates, kc, vc, kvb, ov, e3)


def nsa_prompt_pallas(proj, B_, S, phi, phi_b, qk_g):
    qa, kvb, rawb, rows_t, win_t, _, gates = nsa_prep(proj, qk_g, B_, S)
    nch = S // CMP_STRIDE
    kc, vc = nsa_compress_pallas(rawb.reshape(B_, nch, CMP_STRIDE * 2 * LANE), compress_weights(phi), phi_b, qk_g[1])
    o = nsa_attn_prompt(qa, gates, kc, vc, kvb, B_, S)
    rows = rows_t.reshape(B_, 4, N_KV, HEAD_DIM, S).transpose(0, 4, 1, 2, 3)
    wk = min(WINDOW, S)
    win_new = win_t[:, :, S - wk:].reshape(B_, 2, N_KV, HEAD_DIM, wk).transpose(0, 4, 1, 2, 3)
    return o.reshape(B_, S, N_HEADS * HEAD_DIM), rows, win_new


N_PAGES = PAST_LEN // PAGE_SIZE
N_CHUNK_S = PAST_LEN // CMP_STRIDE
N_SEL_S = -(-(PAST_LEN + DEC_SEQ) // SEL_BLOCK)
CUR_S = PAST_LEN // SEL_BLOCK
QROWS = 8


def compress_weights_paged(phi):
    R = CMP_BLOCK // CMP_STRIDE
    wr = phi.reshape(2, R, CMP_STRIDE, HEAD_DIM, HEAD_DIM)
    w = jnp.einsum('crjde,hy->cjhdrye', wr, jnp.eye(2, dtype=phi.dtype))
    return w.reshape(2, CMP_STRIDE * LANE, R * LANE).astype(MXU_DTYPE)


def _softmax_with_extra(s, s_new):
    m = jnp.maximum(jnp.max(s, axis=-1, keepdims=True), s_new)
    e = jnp.exp(s - m)
    e_new = jnp.exp(s_new - m)
    return e, e_new, jnp.sum(e, axis=-1, keepdims=True) + e_new


CHUNKS_PER_PAGE = PAGE_SIZE // CMP_STRIDE
SWEEP_PAGES = 64


def feature_major_pages(cache_nsa):
    d, n = cache_nsa.shape[:2]
    return cache_nsa.transpose(0, 1, 3, 4, 5, 2).reshape(d * n, 4, N_KV * HEAD_DIM, PAGE_SIZE)


def _cache_compress_body(c_ref, w_ref, o_ref, sk, sv):
    n_pages = c_ref.shape[0]

    def to_row_major(p, carry):
        r0 = pl.multiple_of(p * PAGE_SIZE, PAGE_SIZE)
        sk[pl.ds(r0, PAGE_SIZE), :] = c_ref[p, 0].T
        sv[pl.ds(r0, PAGE_SIZE), :] = c_ref[p, 1].T
        return carry

    lax.fori_loop(0, n_pages, to_row_major, 0, unroll=4)
    n = n_pages * CHUNKS_PER_PAGE
    for c, src in enumerate((sk, sv)):
        x = jnp.concatenate([src[pl.ds(j, n, stride=CMP_STRIDE), :] for j in range(CMP_STRIDE)], axis=1)
        ab = _mm(x, w_ref[0, c])
        o_ref[:, c * LANE:(c + 1) * LANE] = ab[:, 0:LANE]
        o_ref[:, (2 + c) * LANE:(3 + c) * LANE] = ab[:, LANE:2 * LANE]


def cache_compress(cache_fm, nsa_phi):
    n_total = cache_fm.shape[0]
    assert (n_total // DEPTH) % SWEEP_PAGES == 0
    tiles = n_total // DEPTH // SWEEP_PAGES
    wc = jnp.stack([compress_weights_paged(nsa_phi[l]) for l in range(DEPTH)])
    rows = SWEEP_PAGES * PAGE_SIZE
    return pl.pallas_call(
        _cache_compress_body,
        grid=(DEPTH * tiles,),
        in_specs=[pl.BlockSpec((SWEEP_PAGES, 2, LANE, PAGE_SIZE), lambda i: (i, 0, 0, 0)),
                  pl.BlockSpec((1,) + wc.shape[1:], lambda i: (i // tiles, 0, 0, 0))],
        out_specs=pl.BlockSpec((SWEEP_PAGES * CHUNKS_PER_PAGE, 4 * LANE), lambda i: (i, 0)),
        out_shape=jax.ShapeDtypeStruct((n_total * CHUNKS_PER_PAGE, 4 * LANE), F32),
        scratch_shapes=[pltpu.VMEM((rows, LANE), F32), pltpu.VMEM((rows, LANE), F32)],
        compiler_params=_cparams(),
        name="cache_compress",
    )(cache_fm, wc)


SAMPLE_GROUP = 2


def _nsa_sample_body(pt_ref, qa_ref, newb_ref, wnew_ref, gates_ref, *rest):
    n_pg = SAMPLE_GROUP * N_PAGES
    pages, abs_ = rest[:n_pg], rest[n_pg:2 * n_pg]
    y_ref, wout_ref = rest[-2:]
    gens = [_nsa_sample_one(u, qa_ref, newb_ref, wnew_ref, gates_ref, pages[u * N_PAGES:(u + 1) * N_PAGES],
                            abs_[u * N_PAGES:(u + 1) * N_PAGES], *rest[2 * n_pg:-2]) for u in range(SAMPLE_GROUP)]
    outs = [None] * SAMPLE_GROUP
    while any(o is None for o in outs):
        for u, gen in enumerate(gens):
            try:
                next(gen)
            except StopIteration as stop:
                outs[u] = stop.value
    y_ref[...] = jnp.stack([o[0] for o in outs])
    wout_ref[...] = jnp.stack([o[1] for o in outs])


def _nsa_sample_one(u, qa_ref, newb_ref, wnew_ref, gates_ref, pages, abs_, win_ref, bias_ref, gkc_ref, ov_ref, e_ref):
    qs = qa_ref[u]
    newb = newb_ref[u].astype(F32)
    lane = lax.broadcasted_iota(jnp.int32, (QROWS, LANE), 1)
    row = lax.broadcasted_iota(jnp.int32, (QROWS, LANE), 0)

    ab = jnp.concatenate([a[...] for a in abs_], axis=0)
    out = ab[:, 0:2 * LANE] + pltpu.roll(ab[:, 2 * LANE:4 * LANE], N_CHUNK_S - 1, 0) + bias_ref[...]
    kc = _seg_rmsnorm(out[:, 0:LANE], gkc_ref[...])
    vc = out[:, LANE:2 * LANE]
    yield

    s = _mm_nt(qs, kc)
    yield
    s = jnp.where(lane < N_CHUNK_S - 1, s, -jnp.inf)
    e = jnp.exp(s - jnp.max(s, axis=-1, keepdims=True))
    p_cmp = e / jnp.sum(e, axis=-1, keepdims=True)
    yield
    o_cmp = _mm(p_cmp, vc)
    imp = _mm(p_cmp, ov_ref[...])
    yield
    imp = imp +jnp.where(row % GQA == 0, pltpu.roll(imp, QROWS - 1, 0), pltpu.roll(imp, 1, 0))

    valid = lane <= CUR_S
    forced = (lane == 0) | (lane == CUR_S) | (lane == CUR_S - 1)
    score = jnp.where(valid, imp, -jnp.inf)
    score = jnp.where(forced & valid, jnp.inf, score)
    cnt = jnp.zeros((QROWS, LANE), F32)
    for i in range(N_SEL_S):
        ci = score[:, i:i + 1]
        cnt = cnt + jnp.where((ci > score) | ((ci == score) & (lane > i)), 1.0, 0.0)
    sel = jnp.where((cnt < SEL_TOPK) & (score > -jnp.inf), 1.0, 0.0)
    yield

    msel = _mm(sel, e_ref[...])
    s = jnp.concatenate([_mm(qs, pg[0, 0]) for pg in pages], axis=1)
    yield
    s = jnp.where(msel > 0.5, s, NEG)
    qf = qs.astype(F32)
    s_new = jnp.sum(qf * newb[:, 2 * LANE:3 * LANE], axis=-1, keepdims=True)
    s_new = jnp.where(sel[:, CUR_S:CUR_S + 1] > 0.5, s_new, NEG)
    e, e_new, d = _softmax_with_extra(s, s_new)
    yield
    acc_o = e_new.astype(MXU_DTYPE).astype(F32) * newb[:, 3 * LANE:4 * LANE]
    for p, pg in enumerate(pages):
        acc_o = acc_o + _mm_nt(e[:, p * PAGE_SIZE:(p + 1) * PAGE_SIZE], pg[0, 1])
    o_sel = acc_o / d
    yield

    s = _mm(qs, win_ref[u, 0])
    yield
    s_new =jnp.sum(qf * newb[:, 4 * LANE:5 * LANE], axis=-1, keepdims=True)
    e, e_new, d = _softmax_with_extra(s, s_new)
    o_win = (_mm_nt(e, win_ref[u, 1]) + e_new.astype(MXU_DTYPE).astype(F32) * newb[:, 5 * LANE:6 * LANE]) / d

    g = gates_ref[u]
    o = g[:, 0:1] * o_cmp + g[:, 1:2] * o_sel + g[:, 2:3] * o_win
    o_sw = pltpu.roll(o, HEAD_DIM, 1)
    lane1 = lax.broadcasted_iota(jnp.int32, (1, LANE), 1)
    ys = []
    for h in range(N_KV):
        a = (o if h == 0 else o_sw)[GQA * h:GQA * h + 1]
        b = (o if h == 1 else o_sw)[GQA * h + 1:GQA * h + 2]
        ys.append(jnp.where(lane1 < HEAD_DIM, a, b))
    lw = win_ref.shape[3]
    last = lax.broadcasted_iota(jnp.int32, (LANE, lw), 1) == lw - 1
    wouts = []
    for c in range(2):
        col = jnp.broadcast_to(wnew_ref[u][:, c * LANE:(c + 1) * LANE], (QROWS, LANE)).T[:, 0:1]
        wouts.append(jnp.where(last, col, pltpu.roll(win_ref[u, c], lw - 1, 1)))
    return jnp.concatenate(ys, axis=1), jnp.stack(wouts)


def _sample_constants():
    ci = np.arange(LANE)[:, None] * CMP_STRIDE
    sj = np.arange(LANE)[None, :] * SEL_BLOCK
    ov = ((ci < sj + SEL_BLOCK) & (ci + CMP_BLOCK > sj) & (np.arange(LANE)[:, None] < N_CHUNK_S - 1)
          & (np.arange(LANE)[None, :] < N_SEL_S))
    e = (np.arange(LANE)[:, None] == (np.arange(PAST_LEN)[None, :] // SEL_BLOCK))
    return jnp.asarray(ov, MXU_DTYPE), jnp.asarray(e, MXU_DTYPE)


def nsa_sample_pallas(proj, layer, cache_fm, cache_ab, page_table, win_fm, phi_b, qk_g):
    B_ = proj.shape[0]
    n_phys = cache_fm.shape[0] // DEPTH
    lw = win_fm.shape[3]
    assert page_table.shape == (B_, N_PAGES) and lw <= WINDOW and lw <= PAST_LEN and CUR_S == N_SEL_S - 1
    qa, kvb, _, rows_t, _, wnew, gates = nsa_prep(proj, qk_g, 1, B_)
    qa8 = jnp.pad(qa.astype(F32).reshape(B_, N_HEADS, LANE), ((0, 0), (0, QROWS - N_HEADS), (0, 0)))
    gates8 = jnp.pad(gates[:, :3 * N_HEADS].reshape(B_, N_HEADS, 3), ((0, 0), (0, QROWS - N_HEADS), (0, LANE - 3)))
    ov, e = _sample_constants()
    bias = jnp.concatenate([jnp.tile(phi_b[0], 2), jnp.tile(phi_b[1], 2)]).reshape(1, 2 * LANE)

    G = SAMPLE_GROUP
    assert B_ % G == 0
    seq_page = [(u, p) for u in range(G) for p in range(N_PAGES)]

    def page_spec(u, p):
        return pl.BlockSpec((1, 2, LANE, PAGE_SIZE), lambda b, pt: (layer * n_phys + pt[G * b + u, p], 1, 0, 0))

    def ab_spec(u, p):
        return pl.BlockSpec((CHUNKS_PER_PAGE, 4 * LANE), lambda b, pt: (layer * n_phys + pt[G * b + u, p], 0))

    def per_b(shape):
        return pl.BlockSpec((G,) + shape, lambda b, pt: (b, 0, 0))

    def const(a):
        return pl.BlockSpec(a.shape, lambda b, pt: (0,) * a.ndim)

    gkc = jnp.tile(qk_g[1], 2).reshape(1, LANE)
    y, wout = pl.pallas_call(
        _nsa_sample_body,
        grid_spec=pltpu.PrefetchScalarGridSpec(
            num_scalar_prefetch=1,
            grid=(B_ // G,),
            in_specs=[per_b((QROWS, LANE)), per_b((1, 6 * LANE)), per_b((1, 2 * LANE)), per_b((QROWS, LANE))]
                     + [page_spec(u, p) for u, p in seq_page] + [ab_spec(u, p) for u, p in seq_page]
                     + [pl.BlockSpec((G, 2, LANE, lw), lambda b, pt: (layer * (B_ // G) + b, 0, 0, 0)),
                        const(bias), const(gkc), const(ov), const(e)],
            out_specs=[per_b((1, 2 * LANE)), pl.BlockSpec((G, 2, LANE, lw), lambda b, pt: (b, 0, 0, 0))]),
        out_shape=[jax.ShapeDtypeStruct((B_, 1, 2 * LANE), F32),
                   jax.ShapeDtypeStruct((B_, 2, LANE, lw), F32)],
        compiler_params=_cparams(),
        name="nsa_sample",
    )(page_table, qa8, kvb.reshape(B_, 1, 6 * LANE), wnew.reshape(B_, 1, 2 * LANE), gates8,
      *([cache_fm] * (G * N_PAGES)), *([cache_ab] * (G * N_PAGES)), win_fm, bias, gkc, ov, e)
    rows = rows_t.reshape(4, N_KV, HEAD_DIM, B_).transpose(3, 0, 1, 2)[:, None]
    return (y.reshape(B_, 1, N_HEADS * HEAD_DIM), rows,
            wout.reshape(B_, 2, N_KV, HEAD_DIM, lw).transpose(0, 4, 1, 2, 3))


MIX_CHUNK = 512
HALO = 16
YM_W = POOL_W + RG_W + SC_W


def _expm1(x):
    p = jnp.full_like(x, 1.0 / 3628800.0)
    for c in (1.0 / 362880.0, 1.0 / 40320.0, 1.0 / 5040.0, 1.0 / 720.0, 1.0 / 120.0, 1.0 / 24.0, 1.0 / 6.0, 0.5, 1.0):
        p = p * x + c
    return jnp.where(jnp.abs(x) < 0.25, p * x, jnp.exp(x) - 1.0)


def _softplus(x):
    return jnp.maximum(x, 0.0) + jnp.log1p(jnp.exp(-jnp.abs(x)))


def _gelu_tanh(x):
    return 0.5 * x * (1.0 + jnp.tanh(np.sqrt(2.0 / np.pi).astype(np.float32) * (x + 0.044715 * (x * x * x))))


def _rg_coeffs(xc, wa, ba, wx, bx, lam):
    r = jax.nn.sigmoid(_mm(xc, wa) + ba)
    ig = jax.nn.sigmoid(_mm(xc, wx) + bx)
    log_a = (-RG_C * r) * _softplus(-lam)
    return jnp.exp(log_a), jnp.sqrt(-_expm1(2.0 * log_a)) * (ig * xc)


def _pool_select(s2, s4, s8, s16):
    lane = lax.broadcasted_iota(jnp.int32, s2.shape, 1)
    return jnp.where(lane < POOL_GROUP, s2, jnp.where(lane < 2 * POOL_GROUP, s4,
                                                      jnp.where(lane < 3 * POOL_GROUP, s8, s16)))


def _pool_count(pos, shape):
    lane = lax.broadcasted_iota(jnp.int32, shape, 1)
    win = jnp.left_shift(2, lane // POOL_GROUP)
    return jnp.minimum(win, pos + 1).astype(F32)


def _mixers_prompt_body(pu_ref, rx_ref, rg_ref, z_ref, bg_ref, cg_ref, pw_ref, ps_ref, cw_ref, cb_ref, wa_ref, ba_ref,
                        wx_ref, bx_ref, lam_ref, scw_ref, scb_ref, ym_ref, tails_ref, hlast_ref, halo, hcar):
    c = pl.program_id(1)
    tc = pu_ref.shape[0]

    @pl.when(c == 0)
    def _():
        halo[...] = jnp.zeros_like(halo)
        hcar[...] = jnp.zeros_like(hcar)

    pu, rx = pu_ref[...], rx_ref[...]
    u = cg_ref[...] * z_ref[...]
    ext = [jnp.concatenate([halo[i], v], axis=0) for i, v in enumerate((pu, rx, u))]

    def back(e, k):
        return pltpu.roll(e, k, 0)

    s2 = ext[0] + back(ext[0], 1)
    s4 = s2 + back(s2, 2)
    s8 = s4 + back(s4, 4)
    s16 = s8 + back(s8, 8)
    tot = _pool_select(s2, s4, s8, s16)[HALO:]
    pos = c * tc + lax.broadcasted_iota(jnp.int32, (tc, POOL_W), 0)
    d = tot / _pool_count(pos, (tc, POOL_W)) - pu
    ym_ref[:, 0:POOL_W] = _mm(d, pw_ref[...]) * ps_ref[...]

    cw = cw_ref[...]
    xc = cb_ref[...] + cw[RG_CONV - 1:RG_CONV] * rx
    for k in range(1, RG_CONV):
        xc = xc + cw[RG_CONV - 1 - k:RG_CONV - k] * back(ext[1], k)[HALO:]
    a, b = _rg_coeffs(xc, wa_ref[...], ba_ref[...], wx_ref[...], bx_ref[...], lam_ref[...])
    row = lax.broadcasted_iota(jnp.int32, (tc, RG_W), 0)
    k = 1
    while k < tc:
        a_prev = jnp.where(row < k, 1.0, pltpu.roll(a, k, 0))
        b_prev = jnp.where(row < k, 0.0, pltpu.roll(b, k, 0))
        b = a * b_prev + b
        a = a * a_prev
        k *= 2
    h = a * hcar[0:1] + b
    hcar[...] = jnp.broadcast_to(h[tc - 1:tc], hcar.shape)
    hlast_ref[0] = jnp.broadcast_to(h[tc - 1:tc], hcar.shape)
    ym_ref[:, POOL_W:POOL_W + RG_W] = h * _gelu_tanh(rg_ref[...])

    scw = scw_ref[...]
    v = scb_ref[...] + scw[SC_CONV - 1:SC_CONV] * u
    for k in range(1, SC_CONV):
        v = v + scw[SC_CONV - 1 - k:SC_CONV - k] * back(ext[2], k)[HALO:]
    ym_ref[:, POOL_W + RG_W:YM_W] = bg_ref[...] * v

    for i, val in enumerate((pu, rx, u)):
        halo[i] = val[tc - HALO:]
        tails_ref[0, i] = val[tc - HALO:]


def _block_diag(w):
    g, n, _ = w.shape
    return jnp.einsum('gij,gh->gihj', w, jnp.eye(g, dtype=w.dtype)).reshape(g * n, g * n)


def _mixer_params(lw):
    row = lambda a: a.reshape(1, -1)
    return [_block_diag(lw['pool_w']).astype(MXU_DTYPE), row(lw['pool_scale']), lw['rg_conv_w'], row(lw['rg_conv_b']),
            _block_diag(lw['rg_w_a']).astype(MXU_DTYPE), row(lw['rg_b_a']),
            _block_diag(lw['rg_w_x']).astype(MXU_DTYPE), row(lw['rg_b_x']), row(lw['rg_lambda']),
            lw['sc_conv_w'], row(lw['sc_conv_b'])]


def _proj_col_specs(rows, index):
    cols = (COL_POOL, COL_RX, COL_RGATE, COL_SC, COL_SC + SC_W, COL_SC + 2 * SC_W)
    return [pl.BlockSpec((rows, GROUP_W), functools.partial(index, col // GROUP_W)) for col in cols]


def mixers_prompt(proj, lw, B_, S):
    tc = min(MIX_CHUNK, S)
    nc = S // tc
    params = _mixer_params(lw)
    fixed = lambda a: pl.BlockSpec(a.shape, lambda b, c: (0,) * a.ndim)
    return pl.pallas_call(
        _mixers_prompt_body,
        grid=(B_, nc),
        in_specs=_proj_col_specs(tc, lambda col, b, c: (b * nc + c, col)) + [fixed(a) for a in params],
        out_specs=[pl.BlockSpec((tc, YM_W), lambda b, c: (b * nc + c, 0)),
                   pl.BlockSpec((1, 3, HALO, GROUP_W), lambda b, c: (b, 0, 0, 0)),
                   pl.BlockSpec((1, 8, RG_W), lambda b, c: (b, 0, 0))],
        out_shape=[jax.ShapeDtypeStruct((B_ * S, YM_W), F32),
                   jax.ShapeDtypeStruct((B_, 3, HALO, GROUP_W), F32),
                   jax.ShapeDtypeStruct((B_, 8, RG_W), F32)],
        scratch_shapes=[pltpu.VMEM((3, HALO, GROUP_W), F32), pltpu.VMEM((8, RG_W), F32)],
        compiler_params=_cparams(2),
        name="mixers_prompt",
    )(*([proj] * 6), *params)


def _mixers_sample_body(pos0, pu_ref, rx_ref, rg_ref, z_ref, bg_ref, cg_ref, pp_ref, rp_ref, h0_ref, sp_ref, pw_ref,
                        ps_ref, cw_ref, cb_ref, wa_ref, ba_ref, wx_ref, bx_ref, lam_ref, scw_ref, scb_ref,
                        ym_ref, pn_ref, rn_ref, hn_ref, sn_ref):
    pu, rx = pu_ref[...], rx_ref[...]
    u = cg_ref[...] * z_ref[...]
    run, sums = pu, {}
    for k in range(1, POOL_KEEP + 1):
        run = run + pp_ref[POOL_KEEP - k]
        sums[k + 1] = run
    tot = _pool_select(*(sums[w] for w in POOL_WINDOWS))
    d = tot / _pool_count(pos0, pu.shape) - pu
    ym_ref[:, 0:POOL_W] = _mm(d, pw_ref[...]) * ps_ref[...]
    for k in range(POOL_KEEP - 1):
        pn_ref[k] = pp_ref[k + 1]
    pn_ref[POOL_KEEP - 1] = pu

    cw = cw_ref[...]
    xc = cb_ref[...] + cw[RG_CONV - 1:RG_CONV] * rx
    for k in range(RG_CONV - 1):
        xc = xc + cw[k:k + 1] * rp_ref[k]
    a, b = _rg_coeffs(xc, wa_ref[...], ba_ref[...], wx_ref[...], bx_ref[...], lam_ref[...])
    h = b + a * h0_ref[...]
    hn_ref[...] = h
    ym_ref[:, POOL_W:POOL_W + RG_W] = h * _gelu_tanh(rg_ref[...])
    for k in range(RG_CONV - 2):
        rn_ref[k] = rp_ref[k + 1]
    rn_ref[RG_CONV - 2] = rx

    scw = scw_ref[...]
    v = scb_ref[...] + scw[SC_CONV - 1:SC_CONV] * u
    for k in range(SC_CONV - 1):
        v = v + scw[k:k + 1] * sp_ref[k]
    ym_ref[:, POOL_W + RG_W:YM_W] = bg_ref[...] * v
    for k in range(SC_CONV - 2):
        sn_ref[k] = sp_ref[k + 1]
    sn_ref[SC_CONV - 2] = u


def mixers_sample(proj, lw, pos0, pool_prev, rgc_prev, h0, sc_prev):
    B_ = proj.shape[0]
    params = _mixer_params(lw)
    states = [pool_prev.transpose(1, 0, 2), rgc_prev.transpose(1, 0, 2), h0, sc_prev.transpose(1, 0, 2)]
    full = lambda a: pl.BlockSpec(a.shape, lambda i: (0,) * a.ndim)
    ym, pn, rn, hn, sn = pl.pallas_call(
        functools.partial(_mixers_sample_body, pos0),
        grid=(1,),
        in_specs=_proj_col_specs(B_, lambda col, i: (0, col)) + [full(a) for a in states] + [full(a) for a in params],
        out_specs=[pl.BlockSpec((B_, YM_W), lambda i: (0, 0))] + [full(a) for a in states],
        out_shape=[jax.ShapeDtypeStruct((B_, YM_W), F32)] + [jax.ShapeDtypeStruct(a.shape, F32) for a in states],
        compiler_params=_cparams(),
        name="mixers_sample",
    )(*([proj] * 6), *states, *params)
    return ym, pn.transpose(1, 0, 2), rn.transpose(1, 0, 2), hn, sn.transpose(1, 0, 2)


ROUTE_W = LANE
GROUP_LANE0 = N_EXPERTS
MOE_TILE_PROMPT = 512
MOE_TILE_SAMPLE = 32
COMBINE_TILE = 256
FETCH_GROUPS = 8
FETCH_BUFS = 3


def _rms(x, g):
    return x * lax.rsqrt(jnp.mean(x * x, axis=-1, keepdims=True) + EPS) * g


def _mix_out_router_body(ym_ref, yn_ref, x_ref, og_ref, wo_ref, gf_ref, wr_ref, br_ref, tri_ref, x2_ref, xn_ref,
                         route_ref, cnt_ref, cnt_sc):
    og = og_ref[...]
    groups = (ym_ref[:, 0:POOL_W], ym_ref[:, POOL_W:POOL_W + RG_W], yn_ref[...], ym_ref[:, POOL_W + RG_W:YM_W])
    yn = jnp.concatenate([_rms(y, og[:, i * GROUP_W:(i + 1) * GROUP_W]) for i, y in enumerate(groups)], axis=1)
    x2 = x_ref[...] + _mm(yn, wo_ref[...])
    x2_ref[...] = x2
    xn = _rms(x2, gf_ref[...])
    xn_ref[...] = xn
    logits = _mm(xn, wr_ref[...]) + br_ref[...]
    lane = lax.broadcasted_iota(jnp.int32, logits.shape, 1)
    is_grp = (lane >= GROUP_LANE0) & (lane < GROUP_LANE0 + N_GROUPS)
    grp = jnp.where(is_grp, logits, -jnp.inf)
    gmax = jnp.max(grp, axis=-1, keepdims=True)
    gsel = jnp.min(jnp.where(grp == gmax, lane - GROUP_LANE0, N_GROUPS), axis=-1, keepdims=True)
    p_group = 1.0 / jnp.sum(jnp.where(is_grp, jnp.exp(logits - gmax), 0.0), axis=-1, keepdims=True)
    le = jnp.where((lane < N_EXPERTS) & (lane // EXP_PER_GROUP == gsel), logits, -jnp.inf)
    m1 = jnp.max(le, axis=-1, keepdims=True)
    i1 = jnp.min(jnp.where(le == m1, lane, LANE), axis=-1, keepdims=True)
    le2 = jnp.where(lane == i1, -jnp.inf, le)
    m2 = jnp.max(le2, axis=-1, keepdims=True)
    i2 = jnp.min(jnp.where(le2 == m2, lane, LANE), axis=-1, keepdims=True)
    e2 = jnp.exp(m2 - m1)
    g1 = p_group * (1.0 / (1.0 + e2))
    g2 = p_group * (e2 / (1.0 + e2))
    @pl.when(pl.program_id(0) == 0)
    def _():
        cnt_sc[...] = jnp.zeros_like(cnt_sc)

    oh = jnp.where((lane == i1) | (lane == i2), 1.0, 0.0)
    before = cnt_sc[0:1] + _mm(tri_ref[...], oh)
    r1 = jnp.sum(jnp.where(lane == i1, before, 0.0), axis=-1, keepdims=True)
    r2 = jnp.sum(jnp.where(lane == i2, before, 0.0), axis=-1, keepdims=True)
    total = cnt_sc[0:1] + jnp.sum(oh, axis=0, keepdims=True)
    cnt_sc[...] = jnp.broadcast_to(total, cnt_sc.shape)
    cnt_ref[...] = jnp.broadcast_to(total, cnt_ref.shape)
    vals = (i1.astype(F32), i2.astype(F32), g1, g2, r1, r2)
    route = jnp.zeros(logits.shape, F32)
    for k, v in enumerate(vals):
        route = jnp.where(lane == k, v, route)
    route_ref[...] = route


def mix_out_router(ym, y_nsa, x2d, lw):
    T, D = x2d.shape
    tm = min(256, T)
    wr = jnp.concatenate([lw['router_expert_w'], lw['router_group_w'],
                          jnp.zeros((D, ROUTE_W - N_EXPERTS - N_GROUPS), F32)], axis=1).astype(MXU_DTYPE)
    br = jnp.concatenate([lw['router_expert_b'], lw['router_group_b'],
                          jnp.zeros((ROUTE_W - N_EXPERTS - N_GROUPS,), F32)]).reshape(1, ROUTE_W)
    row = lambda i: (i, 0)
    fixed = lambda i: (0, 0)
    tri = jnp.asarray(np.tril(np.ones((tm, tm), np.float32), -1), MXU_DTYPE)
    return pl.pallas_call(
        _mix_out_router_body,
        grid=(T // tm,),
        in_specs=[pl.BlockSpec((tm, YM_W), row), pl.BlockSpec((tm, GROUP_W), row), pl.BlockSpec((tm, D), row),
                  pl.BlockSpec((1, MIX_W), fixed),
                  pl.BlockSpec((MIX_W, D), fixed), pl.BlockSpec((1, D), fixed), pl.BlockSpec((D, ROUTE_W), fixed),
                  pl.BlockSpec((1, ROUTE_W), fixed), pl.BlockSpec((tm, tm), fixed)],
        out_specs=[pl.BlockSpec((tm, D), row), pl.BlockSpec((tm, D), row), pl.BlockSpec((tm, ROUTE_W), row),
                   pl.BlockSpec((8, ROUTE_W), fixed)],
        out_shape=[jax.ShapeDtypeStruct((T, D), F32), jax.ShapeDtypeStruct((T, D), F32),
                   jax.ShapeDtypeStruct((T, ROUTE_W), F32), jax.ShapeDtypeStruct((8, ROUTE_W), F32)],
        scratch_shapes=[pltpu.VMEM((8, ROUTE_W), F32)],
        compiler_params=_cparams(),
        name="mix_out_router",
    )(ym, y_nsa, x2d, lw['mix_out_g'].reshape(1, MIX_W), lw['w_out'].astype(MXU_DTYPE),
      lw['norm_ffn_g'].reshape(1, D), wr, br, tri)


def moe_schedule(route, counts, tile):
    T = route.shape[0]
    M = T * TOP_E
    fe = route[:, 0:TOP_E].astype(jnp.int32).reshape(M)
    rank = route[:, 4:4 + TOP_E].astype(jnp.int32).reshape(M)
    counts = counts.astype(jnp.int32)
    padded = (counts + tile - 1) // tile * tile
    pad_end = jnp.cumsum(padded)
    dest = ((pad_end - padded)[fe] + rank).astype(jnp.int32)
    n_blk = -(-M // tile) + N_EXPERTS
    tok = jnp.arange(M, dtype=jnp.int32) // TOP_E
    buf_tok = jnp.zeros((n_blk * tile,), jnp.int32).at[dest].set(tok)
    blk_exp = jnp.minimum(jnp.sum(pad_end[None, :] <= (jnp.arange(n_blk, dtype=jnp.int32) * tile)[:, None], axis=1),
                          N_EXPERTS - 1).astype(jnp.int32)
    n_used = (pad_end[-1:] // tile).astype(jnp.int32)
    return buf_tok, blk_exp, n_used, dest


def _moe_ffn_body(tile, tok_ref, bexp_ref, nused_ref, x_hbm, wgu_ref, wdn_ref, y_ref, xg, sem, wgu_bf, wdn_bf):
    j = pl.program_id(0)
    n = nused_ref[0]

    def gather(blk, slot):
        def body(r, c):
            t = tok_ref[blk * tile + r]
            pltpu.make_async_copy(x_hbm.at[pl.ds(t, 1)], xg.at[slot, pl.ds(r, 1)], sem.at[slot]).start()
            return c
        lax.fori_loop(0, tile, body, 0, unroll=8)

    @pl.when((j == 0) & (n > 0))
    def _():
        gather(0, 0)
        gather(jnp.minimum(1, n - 1), 1)

    def wait_block(slot):
        pltpu.make_async_copy(x_hbm.at[pl.ds(0, tile)], xg.at[slot], sem.at[slot]).wait()

    @pl.when(j < n)
    def _():
        slot = j % FETCH_BUFS
        nxt = jnp.minimum(j + 2, n - 1)
        dst = (j + 2) % FETCH_BUFS

        def fetch_group(g):
            per = tile // FETCH_GROUPS
            for r in range(g * per, (g + 1) * per):
                t = tok_ref[nxt * tile + r]
                pltpu.make_async_copy(x_hbm.at[pl.ds(t, 1)], xg.at[dst, pl.ds(r, 1)], sem.at[dst]).start()

        @pl.when((j == 0) | (bexp_ref[j] != bexp_ref[jnp.maximum(j - 1, 0)]))
        def _():
            wgu_bf[...] = wgu_ref[0].astype(wgu_bf.dtype)
            wdn_bf[...] = wdn_ref[0].astype(wdn_bf.dtype)

        wait_block(slot)
        x = xg[slot]
        half = FETCH_GROUPS // 2
        cg, cd = 2 * D_EXPERT // half, y_ref.shape[1] // half
        hs = []
        for c in range(half):
            fetch_group(c)
            hs.append(_mm(x, wgu_bf[:, c * cg:(c + 1) * cg]))
        h = jnp.concatenate(hs, axis=1)
        a, b = h[:, :D_EXPERT], h[:, D_EXPERT:]
        act = a * jax.nn.sigmoid(a) * b
        for c in range(half):
            fetch_group(half + c)
            y_ref[:, c * cd:(c + 1) * cd] = _mm(act, wdn_bf[:, c * cd:(c + 1) * cd])

        @pl.when(j + 1 >= n)
        def _():
            wait_block((j + 1) % FETCH_BUFS)
            wait_block(dst)

    @pl.when(j >= n)
    def _():
        y_ref[...] = jnp.zeros_like(y_ref)


def moe_ffn_pallas(xn, buf_tok, blk_exp, n_used, w_gu, w_down, tile):
    T, D = xn.shape
    n_blk = blk_exp.shape[0]
    return pl.pallas_call(
        functools.partial(_moe_ffn_body, tile),
        grid_spec=pltpu.PrefetchScalarGridSpec(
            num_scalar_prefetch=3,
            grid=(n_blk,),
            in_specs=[pl.BlockSpec(memory_space=pl.ANY),
                      pl.BlockSpec((1, D, 2 * D_EXPERT), lambda j, tok, bexp, nu: (bexp[j], 0, 0)),
                      pl.BlockSpec((1, D_EXPERT, D), lambda j, tok, bexp, nu: (bexp[j], 0, 0))],
            out_specs=pl.BlockSpec((tile, D), lambda j, tok, bexp, nu: (j, 0)),
            scratch_shapes=[pltpu.VMEM((FETCH_BUFS, tile, D), F32), pltpu.SemaphoreType.DMA((FETCH_BUFS,)),
                            pltpu.VMEM((D, 2 * D_EXPERT), MXU_DTYPE), pltpu.VMEM((D_EXPERT, D), MXU_DTYPE)]),
        out_shape=jax.ShapeDtypeStruct((n_blk * tile, D), F32),
        compiler_params=_cparams(),
        name="moe_ffn",
    )(buf_tok, blk_exp, n_used, xn, w_gu, w_down)


def _moe_combine_body(tm, slots_ref, y_hbm, x2_ref, route_ref, o_ref, yb, sem):
    i = pl.program_id(0)
    nt = pl.num_programs(0)

    def gather(tile_i, buf):
        def body(r, c):
            for k in range(TOP_E):
                s = slots_ref[(tile_i * tm + r) * TOP_E + k]
                pltpu.make_async_copy(y_hbm.at[pl.ds(s, 1)], yb.at[buf, k, pl.ds(r, 1)], sem.at[buf]).start()
            return c
        lax.fori_loop(0, tm, body, 0, unroll=8)

    def wait_tile(b):
        for k in range(TOP_E):
            pltpu.make_async_copy(y_hbm.at[pl.ds(0, tm)], yb.at[b, k], sem.at[b]).wait()

    @pl.when(i == 0)
    def _():
        gather(0, 0)
        gather(jnp.minimum(1, nt - 1), 1)

    buf = i % FETCH_BUFS
    ahead = (i + 2) % FETCH_BUFS
    gather(jnp.minimum(i + 2, nt - 1), ahead)
    wait_tile(buf)
    r = route_ref[...]
    o_ref[...] = x2_ref[...] + (r[:, 2:3] * yb[buf, 0] + r[:, 3:4] * yb[buf, 1])

    @pl.when(i + 1 >= nt)
    def _():
        wait_tile((i + 1) % FETCH_BUFS)
        wait_tile(ahead)


def moe_combine_pallas(y, slots, x2, route):
    T, D = x2.shape
    tm = min(COMBINE_TILE, T)
    return pl.pallas_call(
        functools.partial(_moe_combine_body, tm),
        grid_spec=pltpu.PrefetchScalarGridSpec(
            num_scalar_prefetch=1,
            grid=(T // tm,),
            in_specs=[pl.BlockSpec(memory_space=pl.ANY),
                      pl.BlockSpec((tm, D), lambda i, s: (i, 0)),
                      pl.BlockSpec((tm, ROUTE_W), lambda i, s: (i, 0))],
            out_specs=pl.BlockSpec((tm, D), lambda i, s: (i, 0)),
            scratch_shapes=[pltpu.VMEM((FETCH_BUFS, TOP_E, tm, D), F32), pltpu.SemaphoreType.DMA((FETCH_BUFS,))]),
        out_shape=jax.ShapeDtypeStruct((T, D), F32),
        compiler_params=_cparams(),
        name="moe_combine",
    )(slots, y, x2, route)


def mix_out_moe(ym, y_nsa, x2d, lw, tile):
    x2, xn, route, counts = mix_out_router(ym, y_nsa, x2d, lw)
    buf_tok, blk_exp, n_used, slots = moe_schedule(route, counts[0, :N_EXPERTS], tile)
    y = moe_ffn_pallas(xn, buf_tok, blk_exp + lw['expert_base'], n_used, lw['exp_w_gu'], lw['exp_w_down'], tile)
    return moe_combine_pallas(y, slots, x2, route)


def split_cols(a, sizes):
    outs, o = [], 0
    for s in sizes:
        outs.append(a[..., o:o + s])
        o += s
    return outs


def layer_forward(x, pos0, lw, pool_prev, rgc_prev, rgh0, sc_prev, nsa_fn):
    B_, L, _ = x.shape
    w_perm = permute_w_in(lw['w_in']).astype(MXU_DTYPE)
    proj2d = norm_matmul(x.reshape(B_ * L, D_MODEL), lw['norm_mix_g'], w_perm)
    if pool_prev is None:
        ym, tails, hlast = mixers_prompt(proj2d, lw, B_, L)
        pool_new = tails[:, 0, HALO - POOL_KEEP:]
        rgc_new = tails[:, 1, HALO - (RG_CONV - 1):]
        sc_new = tails[:, 2, HALO - (SC_CONV - 1):]
        rgh_new = hlast[:, 0]
    else:
        ym, pool_new, rgc_new, rgh_new, sc_new = mixers_sample(proj2d, lw, pos0, pool_prev, rgc_prev, rgh0, sc_prev)
    y_nsa, nsa_rows, win_new = nsa_fn(proj2d, lw['nsa_phi'], lw['nsa_phi_b'], lw['nsa_qk_g'])
    x = mix_out_moe(ym, y_nsa.reshape(B_ * L, GROUP_W), x.reshape(B_ * L, D_MODEL), lw,
                    MOE_TILE_PROMPT if L > 1 else MOE_TILE_SAMPLE)
    return x.reshape(B_, L, D_MODEL), (nsa_rows, win_new, pool_new, rgc_new, rgh_new, sc_new)


def kernel(x_prompt, x_sample, cache_nsa, state_win_kv, state_pool, state_rg_conv, state_rg_h, state_sc_conv,
           page_table, norm_mix_g, w_in, pool_w, pool_scale, rg_conv_w, rg_conv_b, rg_w_a, rg_b_a, rg_w_x, rg_b_x,
           rg_lambda, nsa_phi, nsa_phi_b, nsa_qk_g, sc_conv_w, sc_conv_b, mix_out_g, w_out, norm_ffn_g,
           router_group_w, router_group_b, router_expert_w, router_expert_b, exp_w_gu, exp_w_down):
    past_len = page_table.shape[1] * cache_nsa.shape[2]
    xp, xs = x_prompt, x_sample
    cache3 = feature_major_pages(cache_nsa)
    win3 = state_win_kv.transpose(0, 1, 3, 4, 5, 2).reshape(DEPTH * state_win_kv.shape[1], 2, N_KV * HEAD_DIM,
                                                             state_win_kv.shape[2])
    cache_ab = cache_compress(cache3, nsa_phi)
    Bp = xp.shape[0]
    st_p, st_s = [], []
    for l in range(DEPTH):
        lw = dict(norm_mix_g=norm_mix_g[l], w_in=w_in[l], pool_w=pool_w[l], pool_scale=pool_scale[l],
                  rg_conv_w=rg_conv_w[l], rg_conv_b=rg_conv_b[l], rg_w_a=rg_w_a[l], rg_b_a=rg_b_a[l],
                  rg_w_x=rg_w_x[l], rg_b_x=rg_b_x[l], rg_lambda=rg_lambda[l], nsa_phi=nsa_phi[l],
                  nsa_phi_b=nsa_phi_b[l], nsa_qk_g=nsa_qk_g[l], sc_conv_w=sc_conv_w[l], sc_conv_b=sc_conv_b[l],
                  mix_out_g=mix_out_g[l], w_out=w_out[l], norm_ffn_g=norm_ffn_g[l],
                  router_group_w=router_group_w[l], router_group_b=router_group_b[l],
                  router_expert_w=router_expert_w[l], router_expert_b=router_expert_b[l],
                  exp_w_gu=exp_w_gu.reshape((DEPTH * N_EXPERTS,) + exp_w_gu.shape[2:]),
                  exp_w_down=exp_w_down.reshape((DEPTH * N_EXPERTS,) + exp_w_down.shape[2:]),
                  expert_base=l * N_EXPERTS)
        xp, sp = layer_forward(xp, 0, lw, None, None, None, None,
                               lambda p, phi, phi_b, g: nsa_prompt_pallas(p, Bp, xp.shape[1], phi, phi_b, g))
        xs, ss = layer_forward(xs, past_len, lw, state_pool[l], state_rg_conv[l], state_rg_h[l], state_sc_conv[l],
                               lambda p, phi, phi_b, g: nsa_sample_pallas(p, l, cache3, cache_ab, page_table, win3,
                                                                          phi_b, g))
        st_p.append(sp)
        st_s.append(ss)

    def stk(lst, i):
        return jnp.stack([s[i] for s in lst])

    return (xp, xs, stk(st_p, 0), stk(st_s, 0), stk(st_p, 1), stk(st_s, 1), stk(st_p, 2), stk(st_s, 2),
            stk(st_p, 3), stk(st_s, 3), stk(st_p, 4), stk(st_s, 4), stk(st_p, 5), stk(st_s, 5))
```

```python
import functools
import jax, jax.numpy as jnp
from jax import lax
import numpy as np
from jax.experimental import pallas as pl
from jax.experimental.pallas import tpu as pltpu

D_MODEL = 1024
BATCH = 4
SEQ = 4096
DEPTH = 2
DEC_BATCH = 128
DEC_SEQ = 1
PAST_LEN = 2048
PAGE_SIZE = 128

MIX_W = D_MODEL
GROUP_W = MIX_W // 4
POOL_W = GROUP_W
POOL_WINDOWS = (2, 4, 8, 16)
POOL_GROUP = POOL_W // len(POOL_WINDOWS)
POOL_KEEP = max(POOL_WINDOWS) - 1
RG_W = GROUP_W
RG_HEADS = 4
RG_BLOCK = RG_W // RG_HEADS
RG_CONV = 4
RG_C = 8.0
HEAD_DIM = 64
N_HEADS = GROUP_W // HEAD_DIM
N_KV = 2
GQA = N_HEADS // N_KV
CMP_BLOCK = 32
CMP_STRIDE = 16
SEL_BLOCK = 64
SEL_TOPK = 16
WINDOW = 512
Q_BLOCK = 128
SC_W = GROUP_W
SC_CONV = 3
N_GROUPS = 4
EXP_PER_GROUP = 8
N_EXPERTS = N_GROUPS * EXP_PER_GROUP
TOP_E = 2
D_EXPERT = 512
MOE_BLOCK = 128
EPS = 1e-6
SPLIT_SIZES = (POOL_W, RG_W, RG_W, N_HEADS * HEAD_DIM, 6 * N_KV * HEAD_DIM, 3 * N_HEADS, 3 * SC_W)
N_IN = sum(SPLIT_SIZES)

LANE = 128
ROW_TILE = 512
VMEM_LIMIT = 48 * 1024 * 1024
MXU_DTYPE = jnp.bfloat16
F32 = jnp.float32
NEG = -1e30

KV_W = 6 * N_KV * HEAD_DIM
COL_Q = 0
COL_KV = COL_Q + N_HEADS * HEAD_DIM
COL_POOL = COL_KV + KV_W
COL_RX = COL_POOL + POOL_W
COL_RGATE = COL_RX + RG_W
COL_SC = COL_RGATE + RG_W
COL_NG = COL_SC + 3 * SC_W
N_IN_PAD = COL_NG + LANE
SEL_TILE = 512
N_SEL_PROMPT = SEQ // SEL_BLOCK


def _cparams(n_axes=1):
    return pltpu.CompilerParams(dimension_semantics=("arbitrary",) * n_axes, vmem_limit_bytes=VMEM_LIMIT)


def _mm(a, b):
    return jnp.dot(a.astype(MXU_DTYPE), b.astype(MXU_DTYPE), preferred_element_type=F32)


def _mm_nt(a, b):
    return lax.dot_general(a.astype(MXU_DTYPE), b.astype(MXU_DTYPE), (((1,), (1,)), ((), ())),
                           preferred_element_type=F32)


def permute_w_in(w):
    pu, rx, rgate, q, kv, ng, sc = split_cols(w, SPLIT_SIZES)
    pad = jnp.zeros((w.shape[0], LANE - ng.shape[1]), w.dtype)
    return jnp.concatenate([q, kv, pu, rx, rgate, sc, ng, pad], axis=1)


def _norm_matmul_body(x_ref, g_ref, w_ref, o_ref):
    xf = x_ref[...]
    h = xf * lax.rsqrt(jnp.mean(xf * xf, axis=-1, keepdims=True) + EPS) * g_ref[...]
    o_ref[...] = _mm(h, w_ref[...])


def norm_matmul(x2d, g, w):
    T, D = x2d.shape
    N = w.shape[1]
    tm = min(ROW_TILE, T)
    return pl.pallas_call(
        _norm_matmul_body,
        grid=(T // tm,),
        in_specs=[pl.BlockSpec((tm, D), lambda i: (i, 0)),
                  pl.BlockSpec((1, D), lambda i: (0, 0)),
                  pl.BlockSpec((D, N), lambda i: (0, 0))],
        out_specs=pl.BlockSpec((tm, N), lambda i: (i, 0)),
        out_shape=jax.ShapeDtypeStruct((T, N), F32),
        compiler_params=_cparams(),
        name="norm_in_proj",
    )(x2d, g.reshape(1, D), w)


def _seg_rmsnorm(x, g):
    x2 = x * x
    left = lax.broadcasted_iota(jnp.int32, x.shape, 1) < HEAD_DIM
    s_l = jnp.sum(jnp.where(left, x2, 0.0), axis=-1, keepdims=True)
    s_r = jnp.sum(jnp.where(left, 0.0, x2), axis=-1, keepdims=True)
    ms = jnp.where(left, s_l, s_r) * (1.0 / HEAD_DIM)
    return x * lax.rsqrt(ms + EPS) * g


def _nsa_prep_body(qkv_ref, ng_ref, g_ref, perm_ref, qa_ref, kvb_ref, rawb_ref, rows_t_ref, win_t_ref, win_ref,
                   gates_ref):
    g = g_ref[...]
    for hb in range(N_KV):
        qn = _seg_rmsnorm(qkv_ref[:, COL_Q + hb * LANE:COL_Q + (hb + 1) * LANE], g[0:1]) * (HEAD_DIM ** -0.5)
        qa_ref[:, hb * 2 * LANE:(hb + 1) * 2 * LANE] = _mm(qn, perm_ref[hb]).astype(qa_ref.dtype)
    comp = [qkv_ref[:, COL_KV + c * LANE:COL_KV + (c + 1) * LANE] for c in range(6)]
    comp[2] = _seg_rmsnorm(comp[2], g[2:3])
    comp[4] = _seg_rmsnorm(comp[4], g[3:4])
    for c in range(6):
        kvb_ref[:, c * LANE:(c + 1) * LANE] = comp[c].astype(kvb_ref.dtype)
    for c in range(2):
        rawb_ref[:, c * LANE:(c + 1) * LANE] = comp[c].astype(rawb_ref.dtype)
    for c in range(4):
        rows_t_ref[0, c * LANE:(c + 1) * LANE, :] = comp[c].T
    for c in range(2):
        win_t_ref[0, c * LANE:(c + 1) * LANE, :] = comp[4 + c].T
        win_ref[:, c * LANE:(c + 1) * LANE] = comp[4 + c]
    gates_ref[...] = jax.nn.sigmoid(ng_ref[...])


def _q_place_matrices():
    p = np.zeros((N_KV, LANE, 2 * LANE), np.float32)
    for hb in range(N_KV):
        for gq in range(GQA):
            for d in range(HEAD_DIM):
                p[hb, gq * HEAD_DIM + d, gq * LANE + hb * HEAD_DIM + d] = 1.0
    return jnp.asarray(p, MXU_DTYPE)


def nsa_prep(proj, qk_g, B_, S):
    T = proj.shape[0]
    tm = min(ROW_TILE, S)
    tpb = S // tm
    qkv_w = COL_POOL
    g4 = jnp.tile(qk_g, (1, 2))
    return pl.pallas_call(
        _nsa_prep_body,
        grid=(T // tm,),
        in_specs=[pl.BlockSpec((tm, qkv_w), lambda i: (i, 0)),
                  pl.BlockSpec((tm, LANE), lambda i: (i, COL_NG // LANE)),
                  pl.BlockSpec((4, LANE), lambda i: (0, 0)),
                  pl.BlockSpec((N_KV, LANE, 2 * LANE), lambda i: (0, 0, 0))],
        out_specs=[pl.BlockSpec((tm, 4 * LANE), lambda i: (i, 0)),
                   pl.BlockSpec((tm, 6 * LANE), lambda i: (i, 0)),
                   pl.BlockSpec((tm, 2 * LANE), lambda i: (i, 0)),
                   pl.BlockSpec((1, 4 * LANE, tm), lambda i: (i // tpb, 0, i % tpb)),
                   pl.BlockSpec((1, 2 * LANE, tm), lambda i: (i // tpb, 0, i % tpb)),
                   pl.BlockSpec((tm, 2 * LANE), lambda i: (i, 0)),
                   pl.BlockSpec((tm, LANE), lambda i: (i, 0))],
        out_shape=[jax.ShapeDtypeStruct((T, 4 * LANE), MXU_DTYPE),
                   jax.ShapeDtypeStruct((T, 6 * LANE), MXU_DTYPE),
                   jax.ShapeDtypeStruct((T, 2 * LANE), MXU_DTYPE),
                   jax.ShapeDtypeStruct((B_, 4 * LANE, S), F32),
                   jax.ShapeDtypeStruct((B_, 2 * LANE, S), F32),
                   jax.ShapeDtypeStruct((T, 2 * LANE), F32),
                   jax.ShapeDtypeStruct((T, LANE), F32)],
        compiler_params=_cparams(),
        name="nsa_prep",
    )(proj, proj, g4, _q_place_matrices())


def compress_weights(phi):
    R = CMP_BLOCK // CMP_STRIDE
    wr = phi.reshape(2, R, CMP_STRIDE, HEAD_DIM, HEAD_DIM)
    eye = jnp.eye(2, dtype=phi.dtype)
    w = jnp.einsum('crjde,cx,hy->rjchdxye', wr, eye, eye)
    return w.reshape(R, CMP_STRIDE * 2 * LANE, 2 * LANE).astype(MXU_DTYPE)


def _compress_body(x_ref, w_ref, b_ref, g_ref, kc_ref, vc_ref):
    x = x_ref[0]
    nch = x.shape[0]
    a = _mm(x, w_ref[0])
    bm = _mm(x, w_ref[1])
    out = a + pltpu.roll(bm, nch - 1, 0) + b_ref[...]
    kc_ref[0] = _seg_rmsnorm(out[:, 0:LANE], g_ref[...]).astype(kc_ref.dtype)
    vc_ref[0] = out[:, LANE:2 * LANE].astype(vc_ref.dtype)


def nsa_compress_pallas(rawb3, wc, phi_b, g_kc):
    B_, nch, K = rawb3.shape
    bias = jnp.concatenate([jnp.tile(phi_b[0], 2), jnp.tile(phi_b[1], 2)]).reshape(1, 2 * LANE)
    return pl.pallas_call(
        _compress_body,
        grid=(B_,),
        in_specs=[pl.BlockSpec((1, nch, K), lambda b: (b, 0, 0)),
                  pl.BlockSpec(wc.shape, lambda b: (0, 0, 0)),
                  pl.BlockSpec((1, 2 * LANE), lambda b: (0, 0)),
                  pl.BlockSpec((1, LANE), lambda b: (0, 0))],
        out_specs=[pl.BlockSpec((1, nch, LANE), lambda b: (b, 0, 0)),
                   pl.BlockSpec((1, nch, LANE), lambda b: (b, 0, 0))],
        out_shape=[jax.ShapeDtypeStruct((B_, nch, LANE), MXU_DTYPE),
                   jax.ShapeDtypeStruct((B_, nch, LANE), MXU_DTYPE)],
        compiler_params=_cparams(),
        name="nsa_compress",
    )(rawb3, wc, bias, jnp.tile(g_kc, 2).reshape(1, LANE))


def _online_update(carry, s, v):
    m, l, acc = carry
    m_new = jnp.maximum(m, jnp.max(s, axis=-1, keepdims=True))
    alpha = jnp.exp(m - m_new)
    p = jnp.exp(s - m_new)
    l = alpha * l + jnp.sum(p, axis=-1, keepdims=True)
    acc = alpha * acc + _mm(p, v)
    return m_new, l, acc


def _select_blocks(imp, start):
    n_sel = N_SEL_PROMPT
    sc_t = imp.T[0:n_sel]
    blk = lax.broadcasted_iota(jnp.int32, sc_t.shape, 0)
    cur = (start + lax.broadcasted_iota(jnp.int32, sc_t.shape, 1)) // SEL_BLOCK
    valid = blk <= cur
    forced = (blk == 0) | (blk == cur) | (blk == cur - 1)
    score = jnp.where(valid, sc_t, -jnp.inf)
    score = jnp.where(forced & valid, jnp.inf, score)
    sub = 8
    groups = [score[g * sub:(g + 1) * sub] for g in range(n_sel // sub)]
    cnts = [jnp.zeros((sub, sc_t.shape[1]), F32) for _ in groups]
    row = lax.broadcasted_iota(jnp.int32, (sub, sc_t.shape[1]), 0)
    for i in range(n_sel):
        ri = score[i:i + 1, :]
        for g, sg in enumerate(groups):
            if (g + 1) * sub - 1 < i:
                beat = jnp.where(ri > sg, 1.0, 0.0)
            elif g * sub > i:
                beat = jnp.where(ri >= sg, 1.0, 0.0)
            else:
                beat = jnp.where(row + g * sub > i, jnp.where(ri >= sg, 1.0, 0.0), jnp.where(ri > sg, 1.0, 0.0))
            cnts[g] = cnts[g] + beat
    cnt = jnp.concatenate(cnts, axis=0)
    sel_t = jnp.where((cnt < SEL_TOPK) & (score > -jnp.inf), 1.0, 0.0)
    sel_t = jnp.concatenate([sel_t, jnp.zeros((LANE - n_sel, sc_t.shape[1]), F32)], axis=0)
    return sel_t.T


def _nsa_attn_body(qa_ref, gates_ref, kc_ref, vc_ref, kv_ref, ov_ref, e_ref, o_ref):
    i = pl.program_id(1)
    start = i * Q_BLOCK
    Q = Q_BLOCK
    R = GQA * Q
    t_row = start + lax.broadcasted_iota(jnp.int32, (R, 1), 0) % Q
    gates = gates_ref[...]
    lane_q = lax.broadcasted_iota(jnp.int32, (Q, LANE), 1)
    heads = range(N_KV)
    qs = [jnp.concatenate([qa_ref[:, (h * GQA + gq) * LANE:(h * GQA + gq + 1) * LANE] for gq in range(GQA)], axis=0)
          for h in heads]

    o_cmps, sel_bias = [], []
    kc = kc_ref[0]
    ncmp = kc.shape[0]
    cmp_end = lax.broadcasted_iota(jnp.int32, (R, ncmp), 1) * CMP_STRIDE + (CMP_BLOCK - 1)
    for h in heads:
        s = jnp.where(cmp_end <= t_row, _mm_nt(qs[h], kc), -jnp.inf)
        m = jnp.max(s, axis=-1, keepdims=True)
        e = jnp.exp(s - jnp.where(m > -jnp.inf, m, 0.0))
        d = jnp.sum(e, axis=-1, keepdims=True)
        p_cmp = e / jnp.where(d > 0, d, 1.0)
        o_cmps.append(_mm(p_cmp, vc_ref[0]))
        imp = _mm(p_cmp[0:Q], ov_ref[...]) + _mm(p_cmp[Q:R], ov_ref[...])
        sel = _select_blocks(imp, start)
        bias = jnp.concatenate([jnp.where(sel > 0.5, 0.0, NEG)] * GQA, axis=0)
        sel_bias.append(jnp.concatenate([qs[h], bias.astype(MXU_DTYPE)], axis=1))

    def sel_scores(j):
        off = pl.multiple_of(j * SEL_TILE, SEL_TILE)
        k = jnp.concatenate([kv_ref[pl.ds(off, SEL_TILE), 2 * LANE:3 * LANE], e_ref[pl.ds(off, SEL_TILE), :]], axis=1)
        v = kv_ref[pl.ds(off, SEL_TILE), 3 * LANE:4 * LANE]
        return off, v, [_mm_nt(sel_bias[h], k) for h in heads]

    def sel_step(j, carry):
        _, v, ss = sel_scores(j)
        return tuple(_online_update(carry[h], ss[h], v) for h in heads)

    init = (jnp.full((R, 1), NEG, F32), jnp.zeros((R, 1), F32), jnp.zeros((R, LANE), F32))
    n_tiles = (start + Q + SEL_TILE - 1) // SEL_TILE
    carry = lax.fori_loop(0, n_tiles - 1, sel_step, (init,) * N_KV)
    off, v, ss = sel_scores(n_tiles - 1)
    causal = off + lax.broadcasted_iota(jnp.int32, (R, SEL_TILE), 1) <= t_row
    o_sels = []
    for h in heads:
        _, l_s, acc_s = _online_update(carry[h], jnp.where(causal, ss[h], NEG), v)
        o_sels.append(acc_s / l_s)

    n_w = WINDOW // Q + 1
    offs = [pl.multiple_of(jnp.maximum(i - kk, 0) * Q, Q) for kk in range(n_w)]
    kw = jnp.concatenate([kv_ref[pl.ds(o, Q), 4 * LANE:5 * LANE] for o in offs], axis=0)
    vw = jnp.concatenate([kv_ref[pl.ds(o, Q), 5 * LANE:6 * LANE] for o in offs], axis=0)
    lane_w = lax.broadcasted_iota(jnp.int32, (1, n_w * Q), 1)
    w_pos = (i - lane_w // Q) * Q + lane_w % Q
    wd = t_row - w_pos
    wmask = (w_pos >= 0) & (wd >= 0) & (wd <= WINDOW)
    o_wins = []
    for h in heads:
        s = jnp.where(wmask, _mm_nt(qs[h], kw), NEG)
        p = jnp.exp(s - jnp.max(s, axis=-1, keepdims=True))
        o_wins.append(_mm(p, vw) / jnp.sum(p, axis=-1, keepdims=True))

    for h in heads:
        o_cmp, o_sel, o_win = o_cmps[h], o_sels[h], o_wins[h]
        outs = []
        for gq in range(GQA):
            c0 = (h * GQA + gq) * 3
            rs = slice(gq * Q, (gq + 1) * Q)
            og = (gates[:, c0:c0 + 1] * o_cmp[rs] + gates[:, c0 + 1:c0 + 2] * o_sel[rs]
                  + gates[:, c0 + 2:c0 + 3] * o_win[rs])
            outs.append(og if gq == h else pltpu.roll(og, HEAD_DIM, 1))
        o_ref[:, h * LANE:(h + 1) * LANE] = jnp.where(lane_q < HEAD_DIM, outs[0], outs[1])


def _sel_constants(S):
    ncmp_rows = S // CMP_STRIDE
    ci = np.arange(ncmp_rows)[:, None] * CMP_STRIDE
    sj = np.arange(LANE)[None, :] * SEL_BLOCK
    ov = ((ci < sj + SEL_BLOCK) & (ci + CMP_BLOCK > sj) & (np.arange(LANE)[None, :] < S // SEL_BLOCK))
    e = (np.arange(S)[:, None] // SEL_BLOCK == np.arange(LANE)[None, :])
    return jnp.asarray(ov, MXU_DTYPE), jnp.asarray(e, MXU_DTYPE)


def nsa_attn_prompt(qa, gates, kc, vc, kvb, B_, S):
    nq = S // Q_BLOCK
    nch = S // CMP_STRIDE
    ov, e3 = _sel_constants(S)
    return pl.pallas_call(
        _nsa_attn_body,
        grid=(B_, nq),
        in_specs=[pl.BlockSpec((Q_BLOCK, 4 * LANE), lambda b, i: (b * nq + i, 0)),
                  pl.BlockSpec((Q_BLOCK, LANE), lambda b, i: (b * nq + i, 0)),
                  pl.BlockSpec((1, nch, LANE), lambda b, i: (b, 0, 0)),
                  pl.BlockSpec((1, nch, LANE), lambda b, i: (b, 0, 0)),
                  pl.BlockSpec((S, 6 * LANE), lambda b, i: (b, 0)),
                  pl.BlockSpec(ov.shape, lambda b, i: (0, 0)),
                  pl.BlockSpec(e3.shape, lambda b, i: (0, 0))],
        out_specs=pl.BlockSpec((Q_BLOCK, 2 * LANE), lambda b, i: (b * nq + i, 0)),
        out_shape=jax.ShapeDtypeStruct((B_ * S, N_HEADS * HEAD_DIM), F32),
        compiler_params=_cparams(2),
        name="nsa_attn_prompt",
    )(qa, gates, kc, vc, kvb, ov, e3)


def nsa_prompt_pallas(proj, B_, S, phi, phi_b, qk_g):
    qa, kvb, rawb, rows_t, win_t, _, gates = nsa_prep(proj, qk_g, B_, S)
    nch = S // CMP_STRIDE
    kc, vc = nsa_compress_pallas(rawb.reshape(B_, nch, CMP_STRIDE * 2 * LANE), compress_weights(phi), phi_b, qk_g[1])
    o = nsa_attn_prompt(qa, gates, kc, vc, kvb, B_, S)
    rows = rows_t.reshape(B_, 4, N_KV, HEAD_DIM, S).transpose(0, 4, 1, 2, 3)
    wk = min(WINDOW, S)
    win_new = win_t[:, :, S - wk:].reshape(B_, 2, N_KV, HEAD_DIM, wk).transpose(0, 4, 1, 2, 3)
    return o.reshape(B_, S, N_HEADS * HEAD_DIM), rows, win_new


N_PAGES = PAST_LEN // PAGE_SIZE
N_CHUNK_S = PAST_LEN // CMP_STRIDE
N_SEL_S = -(-(PAST_LEN + DEC_SEQ) // SEL_BLOCK)
CUR_S = PAST_LEN // SEL_BLOCK
QROWS = 8


def compress_weights_paged(phi):
    R = CMP_BLOCK // CMP_STRIDE
    wr = phi.reshape(2, R, CMP_STRIDE, HEAD_DIM, HEAD_DIM)
    w = jnp.einsum('crjde,hy->cjhdrye', wr, jnp.eye(2, dtype=phi.dtype))
    return w.reshape(2, CMP_STRIDE * LANE, R * LANE).astype(MXU_DTYPE)


def _softmax_with_extra(s, s_new):
    m = jnp.maximum(jnp.max(s, axis=-1, keepdims=True), s_new)
    e = jnp.exp(s - m)
    e_new = jnp.exp(s_new - m)
    return e, e_new, jnp.sum(e, axis=-1, keepdims=True) + e_new


CHUNKS_PER_PAGE = PAGE_SIZE // CMP_STRIDE
SWEEP_PAGES = 64


def feature_major_pages(cache_nsa):
    d, n = cache_nsa.shape[:2]
    return cache_nsa.transpose(0, 1, 3, 4, 5, 2).reshape(d * n, 4, N_KV * HEAD_DIM, PAGE_SIZE)


def _cache_compress_body(c_ref, w_ref, o_ref, sk, sv):
    n_pages = c_ref.shape[0]

    def to_row_major(p, carry):
        r0 = pl.multiple_of(p * PAGE_SIZE, PAGE_SIZE)
        sk[pl.ds(r0, PAGE_SIZE), :] = c_ref[p, 0].T
        sv[pl.ds(r0, PAGE_SIZE), :] = c_ref[p, 1].T
        return carry

    lax.fori_loop(0, n_pages, to_row_major, 0, unroll=4)
    n = n_pages * CHUNKS_PER_PAGE
    for c, src in enumerate((sk, sv)):
        x = jnp.concatenate([src[pl.ds(j, n, stride=CMP_STRIDE), :] for j in range(CMP_STRIDE)], axis=1)
        ab = _mm(x, w_ref[0, c])
        o_ref[:, c * LANE:(c + 1) * LANE] = ab[:, 0:LANE]
        o_ref[:, (2 + c) * LANE:(3 + c) * LANE] = ab[:, LANE:2 * LANE]


def cache_compress(cache_fm, nsa_phi):
    n_total = cache_fm.shape[0]
    assert (n_total // DEPTH) % SWEEP_PAGES == 0
    tiles = n_total // DEPTH // SWEEP_PAGES
    wc = jnp.stack([compress_weights_paged(nsa_phi[l]) for l in range(DEPTH)])
    rows = SWEEP_PAGES * PAGE_SIZE
    return pl.pallas_call(
        _cache_compress_body,
        grid=(DEPTH * tiles,),
        in_specs=[pl.BlockSpec((SWEEP_PAGES, 2, LANE, PAGE_SIZE), lambda i: (i, 0, 0, 0)),
                  pl.BlockSpec((1,) + wc.shape[1:], lambda i: (i // tiles, 0, 0, 0))],
        out_specs=pl.BlockSpec((SWEEP_PAGES * CHUNKS_PER_PAGE, 4 * LANE), lambda i: (i, 0)),
        out_shape=jax.ShapeDtypeStruct((n_total * CHUNKS_PER_PAGE, 4 * LANE), F32),
        scratch_shapes=[pltpu.VMEM((rows, LANE), F32), pltpu.VMEM((rows, LANE), F32)],
        compiler_params=_cparams(),
        name="cache_compress",
    )(cache_fm, wc)


SAMPLE_GROUP = 2


def _nsa_sample_body(pt_ref, qa_ref, newb_ref, wnew_ref, gates_ref, *rest):
    n_pg = SAMPLE_GROUP * N_PAGES
    pages, abs_ = rest[:n_pg], rest[n_pg:2 * n_pg]
    y_ref, wout_ref = rest[-2:]
    gens = [_nsa_sample_one(u, qa_ref, newb_ref, wnew_ref, gates_ref, pages[u * N_PAGES:(u + 1) * N_PAGES],
                            abs_[u * N_PAGES:(u + 1) * N_PAGES], *rest[2 * n_pg:-2]) for u in range(SAMPLE_GROUP)]
    outs = [None] * SAMPLE_GROUP
    while any(o is None for o in outs):
        for u, gen in enumerate(gens):
            try:
                next(gen)
            except StopIteration as stop:
                outs[u] = stop.value
    y_ref[...] = jnp.stack([o[0] for o in outs])
    wout_ref[...] = jnp.stack([o[1] for o in outs])


def _nsa_sample_one(u, qa_ref, newb_ref, wnew_ref, gates_ref, pages, abs_, win_ref, bias_ref, gkc_ref, ov_ref, e_ref):
    qs = qa_ref[u]
    newb = newb_ref[u].astype(F32)
    lane = lax.broadcasted_iota(jnp.int32, (QROWS, LANE), 1)
    row = lax.broadcasted_iota(jnp.int32, (QROWS, LANE), 0)

    ab = jnp.concatenate([a[...] for a in abs_], axis=0)
    out = ab[:, 0:2 * LANE] + pltpu.roll(ab[:, 2 * LANE:4 * LANE], N_CHUNK_S - 1, 0) + bias_ref[...]
    kc = _seg_rmsnorm(out[:, 0:LANE], gkc_ref[...])
    vc = out[:, LANE:2 * LANE]
    yield

    s = _mm_nt(qs, kc)
    yield
    s = jnp.where(lane < N_CHUNK_S - 1, s, -jnp.inf)
    e = jnp.exp(s - jnp.max(s, axis=-1, keepdims=True))
    p_cmp = e / jnp.sum(e, axis=-1, keepdims=True)
    yield
    o_cmp = _mm(p_cmp, vc)
    imp = _mm(p_cmp, ov_ref[...])
    yield
    imp = imp +jnp.where(row % GQA == 0, pltpu.roll(imp, QROWS - 1, 0), pltpu.roll(imp, 1, 0))

    valid = lane <= CUR_S
    forced = (lane == 0) | (lane == CUR_S) | (lane == CUR_S - 1)
    score = jnp.where(valid, imp, -jnp.inf)
    score = jnp.where(forced & valid, jnp.inf, score)
    cnt = jnp.zeros((QROWS, LANE), F32)
    for i in range(N_SEL_S):
        ci = score[:, i:i + 1]
        cnt = cnt + jnp.where((ci > score) | ((ci == score) & (lane > i)), 1.0, 0.0)
    sel = jnp.where((cnt < SEL_TOPK) & (score > -jnp.inf), 1.0, 0.0)
    yield

    msel = _mm(sel, e_ref[...])
    s = jnp.concatenate([_mm(qs, pg[0, 0]) for pg in pages], axis=1)
    yield
    s = jnp.where(msel > 0.5, s, NEG)
    qf = qs.astype(F32)
    s_new = jnp.sum(qf * newb[:, 2 * LANE:3 * LANE], axis=-1, keepdims=True)
    s_new = jnp.where(sel[:, CUR_S:CUR_S + 1] > 0.5, s_new, NEG)
    e, e_new, d = _softmax_with_extra(s, s_new)
    yield
    acc_o = e_new.astype(MXU_DTYPE).astype(F32) * newb[:, 3 * LANE:4 * LANE]
    for p, pg in enumerate(pages):
        acc_o = acc_o + _mm_nt(e[:, p * PAGE_SIZE:(p + 1) * PAGE_SIZE], pg[0, 1])
    o_sel = acc_o / d
    yield

    s = _mm(qs, win_ref[u, 0])
    yield
    s_new =jnp.sum(qf * newb[:, 4 * LANE:5 * LANE], axis=-1, keepdims=True)
    e, e_new, d = _softmax_with_extra(s, s_new)
    o_win = (_mm_nt(e, win_ref[u, 1]) + e_new.astype(MXU_DTYPE).astype(F32) * newb[:, 5 * LANE:6 * LANE]) / d

    g = gates_ref[u]
    o = g[:, 0:1] * o_cmp + g[:, 1:2] * o_sel + g[:, 2:3] * o_win
    o_sw = pltpu.roll(o, HEAD_DIM, 1)
    lane1 = lax.broadcasted_iota(jnp.int32, (1, LANE), 1)
    ys = []
    for h in range(N_KV):
        a = (o if h == 0 else o_sw)[GQA * h:GQA * h + 1]
        b = (o if h == 1 else o_sw)[GQA * h + 1:GQA * h + 2]
        ys.append(jnp.where(lane1 < HEAD_DIM, a, b))
    lw = win_ref.shape[3]
    last = lax.broadcasted_iota(jnp.int32, (LANE, lw), 1) == lw - 1
    wouts = []
    for c in range(2):
        col = jnp.broadcast_to(wnew_ref[u][:, c * LANE:(c + 1) * LANE], (QROWS, LANE)).T[:, 0:1]
        wouts.append(jnp.where(last, col, pltpu.roll(win_ref[u, c], lw - 1, 1)))
    return jnp.concatenate(ys, axis=1), jnp.stack(wouts)


def _sample_constants():
    ci = np.arange(LANE)[:, None] * CMP_STRIDE
    sj = np.arange(LANE)[None, :] * SEL_BLOCK
    ov = ((ci < sj + SEL_BLOCK) & (ci + CMP_BLOCK > sj) & (np.arange(LANE)[:, None] < N_CHUNK_S - 1)
          & (np.arange(LANE)[None, :] < N_SEL_S))
    e = (np.arange(LANE)[:, None] == (np.arange(PAST_LEN)[None, :] // SEL_BLOCK))
    return jnp.asarray(ov, MXU_DTYPE), jnp.asarray(e, MXU_DTYPE)


def nsa_sample_pallas(proj, layer, cache_fm, cache_ab, page_table, win_fm, phi_b, qk_g):
    B_ = proj.shape[0]
    n_phys = cache_fm.shape[0] // DEPTH
    lw = win_fm.shape[3]
    assert page_table.shape == (B_, N_PAGES) and lw <= WINDOW and lw <= PAST_LEN and CUR_S == N_SEL_S - 1
    qa, kvb, _, rows_t, _, wnew, gates = nsa_prep(proj, qk_g, 1, B_)
    qa8 = jnp.pad(qa.astype(F32).reshape(B_, N_HEADS, LANE), ((0, 0), (0, QROWS - N_HEADS), (0, 0)))
    gates8 = jnp.pad(gates[:, :3 * N_HEADS].reshape(B_, N_HEADS, 3), ((0, 0), (0, QROWS - N_HEADS), (0, LANE - 3)))
    ov, e = _sample_constants()
    bias = jnp.concatenate([jnp.tile(phi_b[0], 2), jnp.tile(phi_b[1], 2)]).reshape(1, 2 * LANE)

    G = SAMPLE_GROUP
    assert B_ % G == 0
    seq_page = [(u, p) for u in range(G) for p in range(N_PAGES)]

    def page_spec(u, p):
        return pl.BlockSpec((1, 2, LANE, PAGE_SIZE), lambda b, pt: (layer * n_phys + pt[G * b + u, p], 1, 0, 0))

    def ab_spec(u, p):
        return pl.BlockSpec((CHUNKS_PER_PAGE, 4 * LANE), lambda b, pt: (layer * n_phys + pt[G * b + u, p], 0))

    def per_b(shape):
        return pl.BlockSpec((G,) + shape, lambda b, pt: (b, 0, 0))

    def const(a):
        return pl.BlockSpec(a.shape, lambda b, pt: (0,) * a.ndim)

    gkc = jnp.tile(qk_g[1], 2).reshape(1, LANE)
    y, wout = pl.pallas_call(
        _nsa_sample_body,
        grid_spec=pltpu.PrefetchScalarGridSpec(
            num_scalar_prefetch=1,
            grid=(B_ // G,),
            in_specs=[per_b((QROWS, LANE)), per_b((1, 6 * LANE)), per_b((1, 2 * LANE)), per_b((QROWS, LANE))]
                     + [page_spec(u, p) for u, p in seq_page] + [ab_spec(u, p) for u, p in seq_page]
                     + [pl.BlockSpec((G, 2, LANE, lw), lambda b, pt: (layer * (B_ // G) + b, 0, 0, 0)),
                        const(bias), const(gkc), const(ov), const(e)],
            out_specs=[per_b((1, 2 * LANE)), pl.BlockSpec((G, 2, LANE, lw), lambda b, pt: (b, 0, 0, 0))]),
        out_shape=[jax.ShapeDtypeStruct((B_, 1, 2 * LANE), F32),
                   jax.ShapeDtypeStruct((B_, 2, LANE, lw), F32)],
        compiler_params=_cparams(),
        name="nsa_sample",
    )(page_table, qa8, kvb.reshape(B_, 1, 6 * LANE), wnew.reshape(B_, 1, 2 * LANE), gates8,
      *([cache_fm] * (G * N_PAGES)), *([cache_ab] * (G * N_PAGES)), win_fm, bias, gkc, ov, e)
    rows = rows_t.reshape(4, N_KV, HEAD_DIM, B_).transpose(3, 0, 1, 2)[:, None]
    return (y.reshape(B_, 1, N_HEADS * HEAD_DIM), rows,
            wout.reshape(B_, 2, N_KV, HEAD_DIM, lw).transpose(0, 4, 1, 2, 3))


MIX_CHUNK = 512
HALO = 16
YM_W = POOL_W + RG_W + SC_W


def _expm1(x):
    p = jnp.full_like(x, 1.0 / 3628800.0)
    for c in (1.0 / 362880.0, 1.0 / 40320.0, 1.0 / 5040.0, 1.0 / 720.0, 1.0 / 120.0, 1.0 / 24.0, 1.0 / 6.0, 0.5, 1.0):
        p = p * x + c
    return jnp.where(jnp.abs(x) < 0.25, p * x, jnp.exp(x) - 1.0)


def _softplus(x):
    return jnp.maximum(x, 0.0) + jnp.log1p(jnp.exp(-jnp.abs(x)))


def _gelu_tanh(x):
    return 0.5 * x * (1.0 + jnp.tanh(np.sqrt(2.0 / np.pi).astype(np.float32) * (x + 0.044715 * (x * x * x))))


def _rg_coeffs(xc, wa, ba, wx, bx, lam):
    r = jax.nn.sigmoid(_mm(xc, wa) + ba)
    ig = jax.nn.sigmoid(_mm(xc, wx) + bx)
    log_a = (-RG_C * r) * _softplus(-lam)
    return jnp.exp(log_a), jnp.sqrt(-_expm1(2.0 * log_a)) * (ig * xc)


def _pool_select(s2, s4, s8, s16):
    lane = lax.broadcasted_iota(jnp.int32, s2.shape, 1)
    return jnp.where(lane < POOL_GROUP, s2, jnp.where(lane < 2 * POOL_GROUP, s4,
                                                      jnp.where(lane < 3 * POOL_GROUP, s8, s16)))


def _pool_count(pos, shape):
    lane = lax.broadcasted_iota(jnp.int32, shape, 1)
    win = jnp.left_shift(2, lane // POOL_GROUP)
    return jnp.minimum(win, pos + 1).astype(F32)


def _mixers_prompt_body(pu_ref, rx_ref, rg_ref, z_ref, bg_ref, cg_ref, pw_ref, ps_ref, cw_ref, cb_ref, wa_ref, ba_ref,
                        wx_ref, bx_ref, lam_ref, scw_ref, scb_ref, ym_ref, tails_ref, hlast_ref, halo, hcar):
    c = pl.program_id(1)
    tc = pu_ref.shape[0]

    @pl.when(c == 0)
    def _():
        halo[...] = jnp.zeros_like(halo)
        hcar[...] = jnp.zeros_like(hcar)

    pu, rx = pu_ref[...], rx_ref[...]
    u = cg_ref[...] * z_ref[...]
    ext = [jnp.concatenate([halo[i], v], axis=0) for i, v in enumerate((pu, rx, u))]

    def back(e, k):
        return pltpu.roll(e, k, 0)

    s2 = ext[0] + back(ext[0], 1)
    s4 = s2 + back(s2, 2)
    s8 = s4 + back(s4, 4)
    s16 = s8 + back(s8, 8)
    tot = _pool_select(s2, s4, s8, s16)[HALO:]
    pos = c * tc + lax.broadcasted_iota(jnp.int32, (tc, POOL_W), 0)
    d = tot / _pool_count(pos, (tc, POOL_W)) - pu
    ym_ref[:, 0:POOL_W] = _mm(d, pw_ref[...]) * ps_ref[...]

    cw = cw_ref[...]
    xc = cb_ref[...] + cw[RG_CONV - 1:RG_CONV] * rx
    for k in range(1, RG_CONV):
        xc = xc + cw[RG_CONV - 1 - k:RG_CONV - k] * back(ext[1], k)[HALO:]
    a, b = _rg_coeffs(xc, wa_ref[...], ba_ref[...], wx_ref[...], bx_ref[...], lam_ref[...])
    row = lax.broadcasted_iota(jnp.int32, (tc, RG_W), 0)
    k = 1
    while k < tc:
        a_prev = jnp.where(row < k, 1.0, pltpu.roll(a, k, 0))
        b_prev = jnp.where(row < k, 0.0, pltpu.roll(b, k, 0))
        b = a * b_prev + b
        a = a * a_prev
        k *= 2
    h = a * hcar[0:1] + b
    hcar[...] = jnp.broadcast_to(h[tc - 1:tc], hcar.shape)
    hlast_ref[0] = jnp.broadcast_to(h[tc - 1:tc], hcar.shape)
    ym_ref[:, POOL_W:POOL_W + RG_W] = h * _gelu_tanh(rg_ref[...])

    scw = scw_ref[...]
    v = scb_ref[...] + scw[SC_CONV - 1:SC_CONV] * u
    for k in range(1, SC_CONV):
        v = v + scw[SC_CONV - 1 - k:SC_CONV - k] * back(ext[2], k)[HALO:]
    ym_ref[:, POOL_W + RG_W:YM_W] = bg_ref[...] * v

    for i, val in enumerate((pu, rx, u)):
        halo[i] = val[tc - HALO:]
        tails_ref[0, i] = val[tc - HALO:]


def _block_diag(w):
    g, n, _ = w.shape
    return jnp.einsum('gij,gh->gihj', w, jnp.eye(g, dtype=w.dtype)).reshape(g * n, g * n)


def _mixer_params(lw):
    row = lambda a: a.reshape(1, -1)
    return [_block_diag(lw['pool_w']).astype(MXU_DTYPE), row(lw['pool_scale']), lw['rg_conv_w'], row(lw['rg_conv_b']),
            _block_diag(lw['rg_w_a']).astype(MXU_DTYPE), row(lw['rg_b_a']),
            _block_diag(lw['rg_w_x']).astype(MXU_DTYPE), row(lw['rg_b_x']), row(lw['rg_lambda']),
            lw['sc_conv_w'], row(lw['sc_conv_b'])]


def _proj_col_specs(rows, index):
    cols = (COL_POOL, COL_RX, COL_RGATE, COL_SC, COL_SC + SC_W, COL_SC + 2 * SC_W)
    return [pl.BlockSpec((rows, GROUP_W), functools.partial(index, col // GROUP_W)) for col in cols]


def mixers_prompt(proj, lw, B_, S):
    tc = min(MIX_CHUNK, S)
    nc = S // tc
    params = _mixer_params(lw)
    fixed = lambda a: pl.BlockSpec(a.shape, lambda b, c: (0,) * a.ndim)
    return pl.pallas_call(
        _mixers_prompt_body,
        grid=(B_, nc),
        in_specs=_proj_col_specs(tc, lambda col, b, c: (b * nc + c, col)) + [fixed(a) for a in params],
        out_specs=[pl.BlockSpec((tc, YM_W), lambda b, c: (b * nc + c, 0)),
                   pl.BlockSpec((1, 3, HALO, GROUP_W), lambda b, c: (b, 0, 0, 0)),
                   pl.BlockSpec((1, 8, RG_W), lambda b, c: (b, 0, 0))],
        out_shape=[jax.ShapeDtypeStruct((B_ * S, YM_W), F32),
                   jax.ShapeDtypeStruct((B_, 3, HALO, GROUP_W), F32),
                   jax.ShapeDtypeStruct((B_, 8, RG_W), F32)],
        scratch_shapes=[pltpu.VMEM((3, HALO, GROUP_W), F32), pltpu.VMEM((8, RG_W), F32)],
        compiler_params=_cparams(2),
        name="mixers_prompt",
    )(*([proj] * 6), *params)


def _mixers_sample_body(pos0, pu_ref, rx_ref, rg_ref, z_ref, bg_ref, cg_ref, pp_ref, rp_ref, h0_ref, sp_ref, pw_ref,
                        ps_ref, cw_ref, cb_ref, wa_ref, ba_ref, wx_ref, bx_ref, lam_ref, scw_ref, scb_ref,
                        ym_ref, pn_ref, rn_ref, hn_ref, sn_ref):
    pu, rx = pu_ref[...], rx_ref[...]
    u = cg_ref[...] * z_ref[...]
    run, sums = pu, {}
    for k in range(1, POOL_KEEP + 1):
        run = run + pp_ref[POOL_KEEP - k]
        sums[k + 1] = run
    tot = _pool_select(*(sums[w] for w in POOL_WINDOWS))
    d = tot / _pool_count(pos0, pu.shape) - pu
    ym_ref[:, 0:POOL_W] = _mm(d, pw_ref[...]) * ps_ref[...]
    for k in range(POOL_KEEP - 1):
        pn_ref[k] = pp_ref[k + 1]
    pn_ref[POOL_KEEP - 1] = pu

    cw = cw_ref[...]
    xc = cb_ref[...] + cw[RG_CONV - 1:RG_CONV] * rx
    for k in range(RG_CONV - 1):
        xc = xc + cw[k:k + 1] * rp_ref[k]
    a, b = _rg_coeffs(xc, wa_ref[...], ba_ref[...], wx_ref[...], bx_ref[...], lam_ref[...])
    h = b + a * h0_ref[...]
    hn_ref[...] = h
    ym_ref[:, POOL_W:POOL_W + RG_W] = h * _gelu_tanh(rg_ref[...])
    for k in range(RG_CONV - 2):
        rn_ref[k] = rp_ref[k + 1]
    rn_ref[RG_CONV - 2] = rx

    scw = scw_ref[...]
    v = scb_ref[...] + scw[SC_CONV - 1:SC_CONV] * u
    for k in range(SC_CONV - 1):
        v = v + scw[k:k + 1] * sp_ref[k]
    ym_ref[:, POOL_W + RG_W:YM_W] = bg_ref[...] * v
    for k in range(SC_CONV - 2):
        sn_ref[k] = sp_ref[k + 1]
    sn_ref[SC_CONV - 2] = u


def mixers_sample(proj, lw, pos0, pool_prev, rgc_prev, h0, sc_prev):
    B_ = proj.shape[0]
    params = _mixer_params(lw)
    states = [pool_prev.transpose(1, 0, 2), rgc_prev.transpose(1, 0, 2), h0, sc_prev.transpose(1, 0, 2)]
    full = lambda a: pl.BlockSpec(a.shape, lambda i: (0,) * a.ndim)
    ym, pn, rn, hn, sn = pl.pallas_call(
        functools.partial(_mixers_sample_body, pos0),
        grid=(1,),
        in_specs=_proj_col_specs(B_, lambda col, i: (0, col)) + [full(a) for a in states] + [full(a) for a in params],
        out_specs=[pl.BlockSpec((B_, YM_W), lambda i: (0, 0))] + [full(a) for a in states],
        out_shape=[jax.ShapeDtypeStruct((B_, YM_W), F32)] + [jax.ShapeDtypeStruct(a.shape, F32) for a in states],
        compiler_params=_cparams(),
        name="mixers_sample",
    )(*([proj] * 6), *states, *params)
    return ym, pn.transpose(1, 0, 2), rn.transpose(1, 0, 2), hn, sn.transpose(1, 0, 2)


ROUTE_W = LANE
GROUP_LANE0 = N_EXPERTS
MOE_TILE_PROMPT = 256
MOE_TILE_SAMPLE = 32
COMBINE_TILE = 256
FETCH_GROUPS = 8
FETCH_BUFS = 3


def _rms(x, g):
    return x * lax.rsqrt(jnp.mean(x * x, axis=-1, keepdims=True) + EPS) * g


def _mix_out_router_body(ym_ref, yn_ref, x_ref, og_ref, wo_ref, gf_ref, wr_ref, br_ref, tri_ref, x2_ref, xn_ref,
                         route_ref, cnt_ref, cnt_sc):
    og = og_ref[...]
    groups = (ym_ref[:, 0:POOL_W], ym_ref[:, POOL_W:POOL_W + RG_W], yn_ref[...], ym_ref[:, POOL_W + RG_W:YM_W])
    yn = jnp.concatenate([_rms(y, og[:, i * GROUP_W:(i + 1) * GROUP_W]) for i, y in enumerate(groups)], axis=1)
    x2 = x_ref[...] + _mm(yn, wo_ref[...])
    x2_ref[...] = x2
    xn = _rms(x2, gf_ref[...])
    bits = lax.bitcast_convert_type(xn.astype(jnp.bfloat16).astype(F32), jnp.uint32)
    half = xn.shape[1] // 2
    xn_ref[...] = (bits[:, half:] & jnp.uint32(0xFFFF0000)) | (bits[:, :half] >> 16)
    logits = _mm(xn, wr_ref[...]) + br_ref[...]
    lane = lax.broadcasted_iota(jnp.int32, logits.shape, 1)
    is_grp = (lane >= GROUP_LANE0) & (lane < GROUP_LANE0 + N_GROUPS)
    grp = jnp.where(is_grp, logits, -jnp.inf)
    gmax = jnp.max(grp, axis=-1, keepdims=True)
    gsel = jnp.min(jnp.where(grp == gmax, lane - GROUP_LANE0, N_GROUPS), axis=-1, keepdims=True)
    p_group = 1.0 / jnp.sum(jnp.where(is_grp, jnp.exp(logits - gmax), 0.0), axis=-1, keepdims=True)
    le = jnp.where((lane < N_EXPERTS) & (lane // EXP_PER_GROUP == gsel), logits, -jnp.inf)
    m1 = jnp.max(le, axis=-1, keepdims=True)
    i1 = jnp.min(jnp.where(le == m1, lane, LANE), axis=-1, keepdims=True)
    le2 = jnp.where(lane == i1, -jnp.inf, le)
    m2 = jnp.max(le2, axis=-1, keepdims=True)
    i2 = jnp.min(jnp.where(le2 == m2, lane, LANE), axis=-1, keepdims=True)
    e2 = jnp.exp(m2 - m1)
    g1 = p_group * (1.0 / (1.0 + e2))
    g2 = p_group * (e2 / (1.0 + e2))
    @pl.when(pl.program_id(0) == 0)
    def _():
        cnt_sc[...] = jnp.zeros_like(cnt_sc)

    oh = jnp.where((lane == i1) | (lane == i2), 1.0, 0.0)
    before = cnt_sc[0:1] + _mm(tri_ref[...], oh)
    r1 = jnp.sum(jnp.where(lane == i1, before, 0.0), axis=-1, keepdims=True)
    r2 = jnp.sum(jnp.where(lane == i2, before, 0.0), axis=-1, keepdims=True)
    total = cnt_sc[0:1] + jnp.sum(oh, axis=0, keepdims=True)
    cnt_sc[...] = jnp.broadcast_to(total, cnt_sc.shape)
    cnt_ref[...] = jnp.broadcast_to(total, cnt_ref.shape)
    vals = (i1.astype(F32), i2.astype(F32), g1, g2, r1, r2)
    route = jnp.zeros(logits.shape, F32)
    for k, v in enumerate(vals):
        route = jnp.where(lane == k, v, route)
    route_ref[...] = route


def mix_out_router(ym, y_nsa, x2d, lw):
    T, D = x2d.shape
    tm = min(256, T)
    wr = jnp.concatenate([lw['router_expert_w'], lw['router_group_w'],
                          jnp.zeros((D, ROUTE_W - N_EXPERTS - N_GROUPS), F32)], axis=1).astype(MXU_DTYPE)
    br = jnp.concatenate([lw['router_expert_b'], lw['router_group_b'],
                          jnp.zeros((ROUTE_W - N_EXPERTS - N_GROUPS,), F32)]).reshape(1, ROUTE_W)
    row = lambda i: (i, 0)
    fixed = lambda i: (0, 0)
    tri = jnp.asarray(np.tril(np.ones((tm, tm), np.float32), -1), MXU_DTYPE)
    return pl.pallas_call(
        _mix_out_router_body,
        grid=(T // tm,),
        in_specs=[pl.BlockSpec((tm, YM_W), row), pl.BlockSpec((tm, GROUP_W), row), pl.BlockSpec((tm, D), row),
                  pl.BlockSpec((1, MIX_W), fixed),
                  pl.BlockSpec((MIX_W, D), fixed), pl.BlockSpec((1, D), fixed), pl.BlockSpec((D, ROUTE_W), fixed),
                  pl.BlockSpec((1, ROUTE_W), fixed), pl.BlockSpec((tm, tm), fixed)],
        out_specs=[pl.BlockSpec((tm, D), row), pl.BlockSpec((tm, D // 2), row), pl.BlockSpec((tm, ROUTE_W), row),
                   pl.BlockSpec((8, ROUTE_W), fixed)],
        out_shape=[jax.ShapeDtypeStruct((T, D), F32), jax.ShapeDtypeStruct((T, D // 2), jnp.uint32),
                   jax.ShapeDtypeStruct((T, ROUTE_W), F32), jax.ShapeDtypeStruct((8, ROUTE_W), F32)],
        scratch_shapes=[pltpu.VMEM((8, ROUTE_W), F32)],
        compiler_params=_cparams(),
        name="mix_out_router",
    )(ym, y_nsa, x2d, lw['mix_out_g'].reshape(1, MIX_W), lw['w_out'].astype(MXU_DTYPE),
      lw['norm_ffn_g'].reshape(1, D), wr, br, tri)


def moe_schedule(route, counts, tile):
    T = route.shape[0]
    M = T * TOP_E
    fe = route[:, 0:TOP_E].astype(jnp.int32).reshape(M)
    rank = route[:, 4:4 + TOP_E].astype(jnp.int32).reshape(M)
    counts = counts.astype(jnp.int32)
    padded = (counts + tile - 1) // tile * tile
    pad_end = jnp.cumsum(padded)
    dest = ((pad_end - padded)[fe] + rank).astype(jnp.int32)
    n_blk = -(-M // tile) + N_EXPERTS
    tok = jnp.arange(M, dtype=jnp.int32) // TOP_E
    buf_tok = jnp.zeros((n_blk * tile,), jnp.int32).at[dest].set(tok)
    blk_exp = jnp.minimum(jnp.sum(pad_end[None, :] <= (jnp.arange(n_blk, dtype=jnp.int32) * tile)[:, None], axis=1),
                          N_EXPERTS - 1).astype(jnp.int32)
    n_used = (pad_end[-1:] // tile).astype(jnp.int32)
    return buf_tok, blk_exp, n_used, dest


def _moe_ffn_body(tile, tok_ref, bexp_ref, nused_ref, x_hbm, wgu_ref, wdn_ref, y_ref, xg, sem, wgu_bf, wdn_bf):
    j = pl.program_id(0)
    n = nused_ref[0]

    def gather(blk, slot):
        def body(r, c):
            t = tok_ref[blk * tile + r]
            pltpu.make_async_copy(x_hbm.at[pl.ds(t, 1)], xg.at[slot, pl.ds(r, 1)], sem.at[slot]).start()
            return c
        lax.fori_loop(0, tile, body, 0, unroll=8)

    @pl.when((j == 0) & (n > 0))
    def _():
        gather(0, 0)
        gather(jnp.minimum(1, n - 1), 1)

    def wait_block(slot):
        pltpu.make_async_copy(x_hbm.at[pl.ds(0, tile)], xg.at[slot], sem.at[slot]).wait()

    @pl.when(j < n)
    def _():
        slot = j % FETCH_BUFS
        nxt = jnp.minimum(j + 2, n - 1)
        dst = (j + 2) % FETCH_BUFS

        def fetch_group(g):
            per = tile // FETCH_GROUPS
            for r in range(g * per, (g + 1) * per):
                t = tok_ref[nxt * tile + r]
                pltpu.make_async_copy(x_hbm.at[pl.ds(t, 1)], xg.at[dst, pl.ds(r, 1)], sem.at[dst]).start()

        @pl.when((j == 0) | (bexp_ref[j] != bexp_ref[jnp.maximum(j - 1, 0)]))
        def _():
            wgu_bf[...] = wgu_ref[0].astype(wgu_bf.dtype)
            wdn_bf[...] = wdn_ref[0].astype(wdn_bf.dtype)

        wait_block(slot)
        u = xg[slot]
        dk = u.shape[1]
        x_lo = lax.bitcast_convert_type(u << 16, F32)
        x_hi = lax.bitcast_convert_type(u & jnp.uint32(0xFFFF0000), F32)
        half = FETCH_GROUPS // 2
        cg, cd = 2 * D_EXPERT // half, y_ref.shape[1] // half
        hs = []
        for c in range(half):
            fetch_group(c)
            hs.append(_mm(x_lo, wgu_bf[0:dk, c * cg:(c + 1) * cg]) + _mm(x_hi, wgu_bf[dk:2 * dk, c * cg:(c + 1) * cg]))
        h = jnp.concatenate(hs, axis=1)
        a, b = h[:, :D_EXPERT], h[:, D_EXPERT:]
        act = a * jax.nn.sigmoid(a) * b
        for c in range(half):
            fetch_group(half + c)
            y_ref[:, c * cd:(c + 1) * cd] = _mm(act, wdn_bf[:, c * cd:(c + 1) * cd])

        @pl.when(j + 1 >= n)
        def _():
            wait_block((j + 1) % FETCH_BUFS)
            wait_block(dst)

    @pl.when(j >= n)
    def _():
        y_ref[...] = jnp.zeros_like(y_ref)


def moe_ffn_pallas(xn, buf_tok, blk_exp, n_used, w_gu, w_down, tile):
    T, dk = xn.shape
    D = 2 * dk
    n_blk = blk_exp.shape[0]
    return pl.pallas_call(
        functools.partial(_moe_ffn_body, tile),
        grid_spec=pltpu.PrefetchScalarGridSpec(
            num_scalar_prefetch=3,
            grid=(n_blk,),
            in_specs=[pl.BlockSpec(memory_space=pl.ANY),
                      pl.BlockSpec((1, D, 2 * D_EXPERT), lambda j, tok, bexp, nu: (bexp[j], 0, 0)),
                      pl.BlockSpec((1, D_EXPERT, D), lambda j, tok, bexp, nu: (bexp[j], 0, 0))],
            out_specs=pl.BlockSpec((tile, D), lambda j, tok, bexp, nu: (j, 0)),
            scratch_shapes=[pltpu.VMEM((FETCH_BUFS, tile, dk), jnp.uint32), pltpu.SemaphoreType.DMA((FETCH_BUFS,)),
                            pltpu.VMEM((D, 2 * D_EXPERT), MXU_DTYPE), pltpu.VMEM((D_EXPERT, D), MXU_DTYPE)]),
        out_shape=jax.ShapeDtypeStruct((n_blk * tile, D), F32),
        compiler_params=_cparams(),
        name="moe_ffn",
    )(buf_tok, blk_exp, n_used, xn, w_gu, w_down)


def _moe_combine_body(tm, slots_ref, y_hbm, x2_ref, route_ref, o_ref, yb, sem):
    i = pl.program_id(0)
    nt = pl.num_programs(0)

    def gather(tile_i, buf):
        def body(r, c):
            for k in range(TOP_E):
                s = slots_ref[(tile_i * tm + r) * TOP_E + k]
                pltpu.make_async_copy(y_hbm.at[pl.ds(s, 1)], yb.at[buf, k, pl.ds(r, 1)], sem.at[buf]).start()
            return c
        lax.fori_loop(0, tm, body, 0, unroll=8)

    def wait_tile(b):
        for k in range(TOP_E):
            pltpu.make_async_copy(y_hbm.at[pl.ds(0, tm)], yb.at[b, k], sem.at[b]).wait()

    @pl.when(i == 0)
    def _():
        gather(0, 0)
        gather(jnp.minimum(1, nt - 1), 1)

    buf = i % FETCH_BUFS
    ahead = (i + 2) % FETCH_BUFS
    gather(jnp.minimum(i + 2, nt - 1), ahead)
    wait_tile(buf)
    r = route_ref[...]
    o_ref[...] = x2_ref[...] + (r[:, 2:3] * yb[buf, 0] + r[:, 3:4] * yb[buf, 1])

    @pl.when(i + 1 >= nt)
    def _():
        wait_tile((i + 1) % FETCH_BUFS)
        wait_tile(ahead)


def moe_combine_pallas(y, slots, x2, route):
    T, D = x2.shape
    tm = min(COMBINE_TILE, T)
    return pl.pallas_call(
        functools.partial(_moe_combine_body, tm),
        grid_spec=pltpu.PrefetchScalarGridSpec(
            num_scalar_prefetch=1,
            grid=(T // tm,),
            in_specs=[pl.BlockSpec(memory_space=pl.ANY),
                      pl.BlockSpec((tm, D), lambda i, s: (i, 0)),
                      pl.BlockSpec((tm, ROUTE_W), lambda i, s: (i, 0))],
            out_specs=pl.BlockSpec((tm, D), lambda i, s: (i, 0)),
            scratch_shapes=[pltpu.VMEM((FETCH_BUFS, TOP_E, tm, D), F32), pltpu.SemaphoreType.DMA((FETCH_BUFS,))]),
        out_shape=jax.ShapeDtypeStruct((T, D), F32),
        compiler_params=_cparams(),
        name="moe_combine",
    )(slots, y, x2, route)


def mix_out_moe(ym, y_nsa, x2d, lw, tile):
    x2, xn, route, counts = mix_out_router(ym, y_nsa, x2d, lw)
    buf_tok, blk_exp, n_used, slots = moe_schedule(route, counts[0, :N_EXPERTS], tile)
    y = moe_ffn_pallas(xn, buf_tok, blk_exp + lw['expert_base'], n_used, lw['exp_w_gu'], lw['exp_w_down'], tile)
    return moe_combine_pallas(y, slots, x2, route)


def split_cols(a, sizes):
    outs, o = [], 0
    for s in sizes:
        outs.append(a[..., o:o + s])
        o += s
    return outs


def layer_forward(x, pos0, lw, pool_prev, rgc_prev, rgh0, sc_prev, nsa_fn):
    B_, L, _ = x.shape
    w_perm = permute_w_in(lw['w_in']).astype(MXU_DTYPE)
    proj2d = norm_matmul(x.reshape(B_ * L, D_MODEL), lw['norm_mix_g'], w_perm)
    if pool_prev is None:
        ym, tails, hlast = mixers_prompt(proj2d, lw, B_, L)
        pool_new = tails[:, 0, HALO - POOL_KEEP:]
        rgc_new = tails[:, 1, HALO - (RG_CONV - 1):]
        sc_new = tails[:, 2, HALO - (SC_CONV - 1):]
        rgh_new = hlast[:, 0]
    else:
        ym, pool_new, rgc_new, rgh_new, sc_new = mixers_sample(proj2d, lw, pos0, pool_prev, rgc_prev, rgh0, sc_prev)
    y_nsa, nsa_rows, win_new = nsa_fn(proj2d, lw['nsa_phi'], lw['nsa_phi_b'], lw['nsa_qk_g'])
    x = mix_out_moe(ym, y_nsa.reshape(B_ * L, GROUP_W), x.reshape(B_ * L, D_MODEL), lw,
                    MOE_TILE_PROMPT if L > 1 else MOE_TILE_SAMPLE)
    return x.reshape(B_, L, D_MODEL), (nsa_rows, win_new, pool_new, rgc_new, rgh_new, sc_new)


def kernel(x_prompt, x_sample, cache_nsa, state_win_kv, state_pool, state_rg_conv, state_rg_h, state_sc_conv,
           page_table, norm_mix_g, w_in, pool_w, pool_scale, rg_conv_w, rg_conv_b, rg_w_a, rg_b_a, rg_w_x, rg_b_x,
           rg_lambda, nsa_phi, nsa_phi_b, nsa_qk_g, sc_conv_w, sc_conv_b, mix_out_g, w_out, norm_ffn_g,
           router_group_w, router_group_b, router_expert_w, router_expert_b, exp_w_gu, exp_w_down):
    past_len = page_table.shape[1] * cache_nsa.shape[2]
    xp, xs = x_prompt, x_sample
    cache3 = feature_major_pages(cache_nsa)
    win3 = state_win_kv.transpose(0, 1, 3, 4, 5, 2).reshape(DEPTH * state_win_kv.shape[1], 2, N_KV * HEAD_DIM,
                                                             state_win_kv.shape[2])
    cache_ab = cache_compress(cache3, nsa_phi)
    Bp = xp.shape[0]
    st_p, st_s = [], []
    for l in range(DEPTH):
        lw = dict(norm_mix_g=norm_mix_g[l], w_in=w_in[l], pool_w=pool_w[l], pool_scale=pool_scale[l],
                  rg_conv_w=rg_conv_w[l], rg_conv_b=rg_conv_b[l], rg_w_a=rg_w_a[l], rg_b_a=rg_b_a[l],
                  rg_w_x=rg_w_x[l], rg_b_x=rg_b_x[l], rg_lambda=rg_lambda[l], nsa_phi=nsa_phi[l],
                  nsa_phi_b=nsa_phi_b[l], nsa_qk_g=nsa_qk_g[l], sc_conv_w=sc_conv_w[l], sc_conv_b=sc_conv_b[l],
                  mix_out_g=mix_out_g[l], w_out=w_out[l], norm_ffn_g=norm_ffn_g[l],
                  router_group_w=router_group_w[l], router_group_b=router_group_b[l],
                  router_expert_w=router_expert_w[l], router_expert_b=router_expert_b[l],
                  exp_w_gu=exp_w_gu.reshape((DEPTH * N_EXPERTS,) + exp_w_gu.shape[2:]),
                  exp_w_down=exp_w_down.reshape((DEPTH * N_EXPERTS,) + exp_w_down.shape[2:]),
                  expert_base=l * N_EXPERTS)
        xp, sp = layer_forward(xp, 0, lw, None, None, None, None,
                               lambda p, phi, phi_b, g: nsa_prompt_pallas(p, Bp, xp.shape[1], phi, phi_b, g))
        xs, ss = layer_forward(xs, past_len, lw, state_pool[l], state_rg_conv[l], state_rg_h[l], state_sc_conv[l],
                               lambda p, phi, phi_b, g: nsa_sample_pallas(p, l, cache3, cache_ab, page_table, win3,
                                                                          phi_b, g))
        st_p.append(sp)
        st_s.append(ss)

    def stk(lst, i):
        return jnp.stack([s[i] for s in lst])

    return (xp, xs, stk(st_p, 0), stk(st_s, 0), stk(st_p, 1), stk(st_s, 1), stk(st_p, 2), stk(st_s, 2),
            stk(st_p, 3), stk(st_s, 3), stk(st_p, 4), stk(st_s, 4), stk(st_p, 5), stk(st_s, 5))
```

```python
import functools
import jax, jax.numpy as jnp
from jax import lax
import numpy as np
from jax.experimental import pallas as pl
from jax.experimental.pallas import tpu as pltpu

D_MODEL = 1024
BATCH = 4
SEQ = 4096
DEPTH = 2
DEC_BATCH = 128
DEC_SEQ = 1
PAST_LEN = 2048
PAGE_SIZE = 128

MIX_W = D_MODEL
GROUP_W = MIX_W // 4
POOL_W = GROUP_W
POOL_WINDOWS = (2, 4, 8, 16)
POOL_GROUP = POOL_W // len(POOL_WINDOWS)
POOL_KEEP = max(POOL_WINDOWS) - 1
RG_W = GROUP_W
RG_HEADS = 4
RG_BLOCK = RG_W // RG_HEADS
RG_CONV = 4
RG_C = 8.0
HEAD_DIM = 64
N_HEADS = GROUP_W // HEAD_DIM
N_KV = 2
GQA = N_HEADS // N_KV
CMP_BLOCK = 32
CMP_STRIDE = 16
SEL_BLOCK = 64
SEL_TOPK = 16
WINDOW = 512
Q_BLOCK = 128
SC_W = GROUP_W
SC_CONV = 3
N_GROUPS = 4
EXP_PER_GROUP = 8
N_EXPERTS = N_GROUPS * EXP_PER_GROUP
TOP_E = 2
D_EXPERT = 512
MOE_BLOCK = 128
EPS = 1e-6
SPLIT_SIZES = (POOL_W, RG_W, RG_W, N_HEADS * HEAD_DIM, 6 * N_KV * HEAD_DIM, 3 * N_HEADS, 3 * SC_W)
N_IN = sum(SPLIT_SIZES)

LANE = 128
ROW_TILE = 512
VMEM_LIMIT = 48 * 1024 * 1024
MXU_DTYPE = jnp.bfloat16
F32 = jnp.float32
NEG = -1e30

KV_W = 6 * N_KV * HEAD_DIM
COL_Q = 0
COL_KV = COL_Q + N_HEADS * HEAD_DIM
COL_POOL = COL_KV + KV_W
COL_RX = COL_POOL + POOL_W
COL_RGATE = COL_RX + RG_W
COL_SC = COL_RGATE + RG_W
COL_NG = COL_SC + 3 * SC_W
N_IN_PAD = COL_NG + LANE
SEL_TILE = 512
N_SEL_PROMPT = SEQ // SEL_BLOCK


def _cparams(n_axes=1):
    return pltpu.CompilerParams(dimension_semantics=("arbitrary",) * n_axes, vmem_limit_bytes=VMEM_LIMIT)


def _mm(a, b):
    return jnp.dot(a.astype(MXU_DTYPE), b.astype(MXU_DTYPE), preferred_element_type=F32)


def _mm_nt(a, b):
    return lax.dot_general(a.astype(MXU_DTYPE), b.astype(MXU_DTYPE), (((1,), (1,)), ((), ())),
                           preferred_element_type=F32)


def permute_w_in(w):
    pu, rx, rgate, q, kv, ng, sc = split_cols(w, SPLIT_SIZES)
    pad = jnp.zeros((w.shape[0], LANE - ng.shape[1]), w.dtype)
    return jnp.concatenate([q, kv, pu, rx, rgate, sc, ng, pad], axis=1)


def _norm_matmul_body(x_ref, g_ref, w_ref, o_ref):
    xf = x_ref[...]
    h = xf * lax.rsqrt(jnp.mean(xf * xf, axis=-1, keepdims=True) + EPS) * g_ref[...]
    o_ref[...] = _mm(h, w_ref[...])


def norm_matmul(x2d, g, w):
    T, D = x2d.shape
    N = w.shape[1]
    tm = min(ROW_TILE, T)
    return pl.pallas_call(
        _norm_matmul_body,
        grid=(T // tm,),
        in_specs=[pl.BlockSpec((tm, D), lambda i: (i, 0)),
                  pl.BlockSpec((1, D), lambda i: (0, 0)),
                  pl.BlockSpec((D, N), lambda i: (0, 0))],
        out_specs=pl.BlockSpec((tm, N), lambda i: (i, 0)),
        out_shape=jax.ShapeDtypeStruct((T, N), F32),
        compiler_params=_cparams(),
        name="norm_in_proj",
    )(x2d, g.reshape(1, D), w)


def _seg_rmsnorm(x, g):
    x2 = x * x
    left = lax.broadcasted_iota(jnp.int32, x.shape, 1) < HEAD_DIM
    s_l = jnp.sum(jnp.where(left, x2, 0.0), axis=-1, keepdims=True)
    s_r = jnp.sum(jnp.where(left, 0.0, x2), axis=-1, keepdims=True)
    ms = jnp.where(left, s_l, s_r) * (1.0 / HEAD_DIM)
    return x * lax.rsqrt(ms + EPS) * g


def _nsa_prep_body(qkv_ref, ng_ref, g_ref, perm_ref, qa_ref, kvb_ref, rawb_ref, rows_t_ref, win_t_ref, win_ref,
                   gates_ref):
    g = g_ref[...]
    for hb in range(N_KV):
        qn = _seg_rmsnorm(qkv_ref[:, COL_Q + hb * LANE:COL_Q + (hb + 1) * LANE], g[0:1]) * (HEAD_DIM ** -0.5)
        qa_ref[:, hb * 2 * LANE:(hb + 1) * 2 * LANE] = _mm(qn, perm_ref[hb]).astype(qa_ref.dtype)
    comp = [qkv_ref[:, COL_KV + c * LANE:COL_KV + (c + 1) * LANE] for c in range(6)]
    comp[2] = _seg_rmsnorm(comp[2], g[2:3])
    comp[4] = _seg_rmsnorm(comp[4], g[3:4])
    for c in range(6):
        kvb_ref[:, c * LANE:(c + 1) * LANE] = comp[c].astype(kvb_ref.dtype)
    for c in range(2):
        rawb_ref[:, c * LANE:(c + 1) * LANE] = comp[c].astype(rawb_ref.dtype)
    for c in range(4):
        rows_t_ref[0, c * LANE:(c + 1) * LANE, :] = comp[c].T
    for c in range(2):
        win_t_ref[0, c * LANE:(c + 1) * LANE, :] = comp[4 + c].T
        win_ref[:, c * LANE:(c + 1) * LANE] = comp[4 + c]
    gates_ref[...] = jax.nn.sigmoid(ng_ref[...])


def _q_place_matrices():
    p = np.zeros((N_KV, LANE, 2 * LANE), np.float32)
    for hb in range(N_KV):
        for gq in range(GQA):
            for d in range(HEAD_DIM):
                p[hb, gq * HEAD_DIM + d, gq * LANE + hb * HEAD_DIM + d] = 1.0
    return jnp.asarray(p, MXU_DTYPE)


def nsa_prep(proj, qk_g, B_, S):
    T = proj.shape[0]
    tm = min(ROW_TILE, S)
    tpb = S // tm
    qkv_w = COL_POOL
    g4 = jnp.tile(qk_g, (1, 2))
    return pl.pallas_call(
        _nsa_prep_body,
        grid=(T // tm,),
        in_specs=[pl.BlockSpec((tm, qkv_w), lambda i: (i, 0)),
                  pl.BlockSpec((tm, LANE), lambda i: (i, COL_NG // LANE)),
                  pl.BlockSpec((4, LANE), lambda i: (0, 0)),
                  pl.BlockSpec((N_KV, LANE, 2 * LANE), lambda i: (0, 0, 0))],
        out_specs=[pl.BlockSpec((tm, 4 * LANE), lambda i: (i, 0)),
                   pl.BlockSpec((tm, 6 * LANE), lambda i: (i, 0)),
                   pl.BlockSpec((tm, 2 * LANE), lambda i: (i, 0)),
                   pl.BlockSpec((1, 4 * LANE, tm), lambda i: (i // tpb, 0, i % tpb)),
                   pl.BlockSpec((1, 2 * LANE, tm), lambda i: (i // tpb, 0, i % tpb)),
                   pl.BlockSpec((tm, 2 * LANE), lambda i: (i, 0)),
                   pl.BlockSpec((tm, LANE), lambda i: (i, 0))],
        out_shape=[jax.ShapeDtypeStruct((T, 4 * LANE), MXU_DTYPE),
                   jax.ShapeDtypeStruct((T, 6 * LANE), MXU_DTYPE),
                   jax.ShapeDtypeStruct((T, 2 * LANE), MXU_DTYPE),
                   jax.ShapeDtypeStruct((B_, 4 * LANE, S), F32),
                   jax.ShapeDtypeStruct((B_, 2 * LANE, S), F32),
                   jax.ShapeDtypeStruct((T, 2 * LANE), F32),
                   jax.ShapeDtypeStruct((T, LANE), F32)],
        compiler_params=_cparams(),
        name="nsa_prep",
    )(proj, proj, g4, _q_place_matrices())


def compress_weights(phi):
    R = CMP_BLOCK // CMP_STRIDE
    wr = phi.reshape(2, R, CMP_STRIDE, HEAD_DIM, HEAD_DIM)
    eye = jnp.eye(2, dtype=phi.dtype)
    w = jnp.einsum('crjde,cx,hy->rjchdxye', wr, eye, eye)
    return w.reshape(R, CMP_STRIDE * 2 * LANE, 2 * LANE).astype(MXU_DTYPE)


def _compress_body(x_ref, w_ref, b_ref, g_ref, kc_ref, vc_ref):
    x = x_ref[0]
    nch = x.shape[0]
    a = _mm(x, w_ref[0])
    bm = _mm(x, w_ref[1])
    out = a + pltpu.roll(bm, nch - 1, 0) + b_ref[...]
    kc_ref[0] = _seg_rmsnorm(out[:, 0:LANE], g_ref[...]).astype(kc_ref.dtype)
    vc_ref[0] = out[:, LANE:2 * LANE].astype(vc_ref.dtype)


def nsa_compress_pallas(rawb3, wc, phi_b, g_kc):
    B_, nch, K = rawb3.shape
    bias = jnp.concatenate([jnp.tile(phi_b[0], 2), jnp.tile(phi_b[1], 2)]).reshape(1, 2 * LANE)
    return pl.pallas_call(
        _compress_body,
        grid=(B_,),
        in_specs=[pl.BlockSpec((1, nch, K), lambda b: (b, 0, 0)),
                  pl.BlockSpec(wc.shape, lambda b: (0, 0, 0)),
                  pl.BlockSpec((1, 2 * LANE), lambda b: (0, 0)),
                  pl.BlockSpec((1, LANE), lambda b: (0, 0))],
        out_specs=[pl.BlockSpec((1, nch, LANE), lambda b: (b, 0, 0)),
                   pl.BlockSpec((1, nch, LANE), lambda b: (b, 0, 0))],
        out_shape=[jax.ShapeDtypeStruct((B_, nch, LANE), MXU_DTYPE),
                   jax.ShapeDtypeStruct((B_, nch, LANE), MXU_DTYPE)],
        compiler_params=_cparams(),
        name="nsa_compress",
    )(rawb3, wc, bias, jnp.tile(g_kc, 2).reshape(1, LANE))


def _online_update(carry, s, v):
    m, l, acc = carry
    m_new = jnp.maximum(m, jnp.max(s, axis=-1, keepdims=True))
    alpha = jnp.exp(m - m_new)
    p = jnp.exp(s - m_new)
    l = alpha * l + jnp.sum(p, axis=-1, keepdims=True)
    acc = alpha * acc + _mm(p, v)
    return m_new, l, acc


def _select_blocks(imp, start):
    n_sel = N_SEL_PROMPT
    sc_t = imp.T[0:n_sel]
    blk = lax.broadcasted_iota(jnp.int32, sc_t.shape, 0)
    cur = (start + lax.broadcasted_iota(jnp.int32, sc_t.shape, 1)) // SEL_BLOCK
    valid = blk <= cur
    forced = (blk == 0) | (blk == cur) | (blk == cur - 1)
    score = jnp.where(valid, sc_t, -jnp.inf)
    score = jnp.where(forced & valid, jnp.inf, score)
    sub = 8
    groups = [score[g * sub:(g + 1) * sub] for g in range(n_sel // sub)]
    cnts = [jnp.zeros((sub, sc_t.shape[1]), F32) for _ in groups]
    row = lax.broadcasted_iota(jnp.int32, (sub, sc_t.shape[1]), 0)
    for i in range(n_sel):
        ri = score[i:i + 1, :]
        for g, sg in enumerate(groups):
            if (g + 1) * sub - 1 < i:
                beat = jnp.where(ri > sg, 1.0, 0.0)
            elif g * sub > i:
                beat = jnp.where(ri >= sg, 1.0, 0.0)
            else:
                beat = jnp.where(row + g * sub > i, jnp.where(ri >= sg, 1.0, 0.0), jnp.where(ri > sg, 1.0, 0.0))
            cnts[g] = cnts[g] + beat
    cnt = jnp.concatenate(cnts, axis=0)
    sel_t = jnp.where((cnt < SEL_TOPK) & (score > -jnp.inf), 1.0, 0.0)
    sel_t = jnp.concatenate([sel_t, jnp.zeros((LANE - n_sel, sc_t.shape[1]), F32)], axis=0)
    return sel_t.T


def _nsa_attn_body(qa_ref, gates_ref, kc_ref, vc_ref, kv_ref, ov_ref, e_ref, o_ref):
    i = pl.program_id(1)
    start = i * Q_BLOCK
    Q = Q_BLOCK
    R = GQA * Q
    t_row = start + lax.broadcasted_iota(jnp.int32, (R, 1), 0) % Q
    gates = gates_ref[...]
    lane_q = lax.broadcasted_iota(jnp.int32, (Q, LANE), 1)
    heads = range(N_KV)
    qs = [jnp.concatenate([qa_ref[:, (h * GQA + gq) * LANE:(h * GQA + gq + 1) * LANE] for gq in range(GQA)], axis=0)
          for h in heads]

    o_cmps, sel_bias = [], []
    kc = kc_ref[0]
    ncmp = kc.shape[0]
    cmp_end = lax.broadcasted_iota(jnp.int32, (R, ncmp), 1) * CMP_STRIDE + (CMP_BLOCK - 1)
    for h in heads:
        s = jnp.where(cmp_end <= t_row, _mm_nt(qs[h], kc), -jnp.inf)
        m = jnp.max(s, axis=-1, keepdims=True)
        e = jnp.exp(s - jnp.where(m > -jnp.inf, m, 0.0))
        d = jnp.sum(e, axis=-1, keepdims=True)
        p_cmp = e / jnp.where(d > 0, d, 1.0)
        o_cmps.append(_mm(p_cmp, vc_ref[0]))
        imp = _mm(p_cmp[0:Q], ov_ref[...]) + _mm(p_cmp[Q:R], ov_ref[...])
        sel = _select_blocks(imp, start)
        bias = jnp.concatenate([jnp.where(sel > 0.5, 0.0, NEG)] * GQA, axis=0)
        sel_bias.append(jnp.concatenate([qs[h], bias.astype(MXU_DTYPE)], axis=1))

    def sel_scores(j):
        off = pl.multiple_of(j * SEL_TILE, SEL_TILE)
        k = jnp.concatenate([kv_ref[pl.ds(off, SEL_TILE), 2 * LANE:3 * LANE], e_ref[pl.ds(off, SEL_TILE), :]], axis=1)
        v = kv_ref[pl.ds(off, SEL_TILE), 3 * LANE:4 * LANE]
        return off, v, [_mm_nt(sel_bias[h], k) for h in heads]

    def sel_step(j, carry):
        _, v, ss = sel_scores(j)
        return tuple(_online_update(carry[h], ss[h], v) for h in heads)

    init = (jnp.full((R, 1), NEG, F32), jnp.zeros((R, 1), F32), jnp.zeros((R, LANE), F32))
    n_tiles = (start + Q + SEL_TILE - 1) // SEL_TILE
    carry = lax.fori_loop(0, n_tiles - 1, sel_step, (init,) * N_KV)
    off, v, ss = sel_scores(n_tiles - 1)
    causal = off + lax.broadcasted_iota(jnp.int32, (R, SEL_TILE), 1) <= t_row
    o_sels = []
    for h in heads:
        _, l_s, acc_s = _online_update(carry[h], jnp.where(causal, ss[h], NEG), v)
        o_sels.append(acc_s / l_s)

    n_w = WINDOW // Q + 1
    offs = [pl.multiple_of(jnp.maximum(i - kk, 0) * Q, Q) for kk in range(n_w)]
    kw = jnp.concatenate([kv_ref[pl.ds(o, Q), 4 * LANE:5 * LANE] for o in offs], axis=0)
    vw = jnp.concatenate([kv_ref[pl.ds(o, Q), 5 * LANE:6 * LANE] for o in offs], axis=0)
    lane_w = lax.broadcasted_iota(jnp.int32, (1, n_w * Q), 1)
    w_pos = (i - lane_w // Q) * Q + lane_w % Q
    wd = t_row - w_pos
    wmask = (w_pos >= 0) & (wd >= 0) & (wd <= WINDOW)
    o_wins = []
    for h in heads:
        s = jnp.where(wmask, _mm_nt(qs[h], kw), NEG)
        p = jnp.exp(s - jnp.max(s, axis=-1, keepdims=True))
        o_wins.append(_mm(p, vw) / jnp.sum(p, axis=-1, keepdims=True))

    for h in heads:
        o_cmp, o_sel, o_win = o_cmps[h], o_sels[h], o_wins[h]
        outs = []
        for gq in range(GQA):
            c0 = (h * GQA + gq) * 3
            rs = slice(gq * Q, (gq + 1) * Q)
            og = (gates[:, c0:c0 + 1] * o_cmp[rs] + gates[:, c0 + 1:c0 + 2] * o_sel[rs]
                  + gates[:, c0 + 2:c0 + 3] * o_win[rs])
            outs.append(og if gq == h else pltpu.roll(og, HEAD_DIM, 1))
        o_ref[:, h * LANE:(h + 1) * LANE] = jnp.where(lane_q < HEAD_DIM, outs[0], outs[1])


def _sel_constants(S):
    ncmp_rows = S // CMP_STRIDE
    ci = np.arange(ncmp_rows)[:, None] * CMP_STRIDE
    sj = np.arange(LANE)[None, :] * SEL_BLOCK
    ov = ((ci < sj + SEL_BLOCK) & (ci + CMP_BLOCK > sj) & (np.arange(LANE)[None, :] < S // SEL_BLOCK))
    e = (np.arange(S)[:, None] // SEL_BLOCK == np.arange(LANE)[None, :])
    return jnp.asarray(ov, MXU_DTYPE), jnp.asarray(e, MXU_DTYPE)


def nsa_attn_prompt(qa, gates, kc, vc, kvb, B_, S):
    nq = S // Q_BLOCK
    nch = S // CMP_STRIDE
    ov, e3 = _sel_constants(S)
    return pl.pallas_call(
        _nsa_attn_body,
        grid=(B_, nq),
        in_specs=[pl.BlockSpec((Q_BLOCK, 4 * LANE), lambda b, i: (b * nq + i, 0)),
                  pl.BlockSpec((Q_BLOCK, LANE), lambda b, i: (b * nq + i, 0)),
                  pl.BlockSpec((1, nch, LANE), lambda b, i: (b, 0, 0)),
                  pl.BlockSpec((1, nch, LANE), lambda b, i: (b, 0, 0)),
                  pl.BlockSpec((S, 6 * LANE), lambda b, i: (b, 0)),
                  pl.BlockSpec(ov.shape, lambda b, i: (0, 0)),
                  pl.BlockSpec(e3.shape, lambda b, i: (0, 0))],
        out_specs=pl.BlockSpec((Q_BLOCK, 2 * LANE), lambda b, i: (b * nq + i, 0)),
        out_shape=jax.ShapeDtypeStruct((B_ * S, N_HEADS * HEAD_DIM), F32),
        compiler_params=_cparams(2),
        name="nsa_attn_prompt",
    )(qa, gates, kc, vc, kvb, ov, e3)


def nsa_prompt_pallas(proj, B_, S, phi, phi_b, qk_g):
    qa, kvb, rawb, rows_t, win_t, _, gates = nsa_prep(proj, qk_g, B_, S)
    nch = S // CMP_STRIDE
    kc, vc = nsa_compress_pallas(rawb.reshape(B_, nch, CMP_STRIDE * 2 * LANE), compress_weights(phi), phi_b, qk_g[1])
    o = nsa_attn_prompt(qa, gates, kc, vc, kvb, B_, S)
    rows = rows_t.reshape(B_, 4, N_KV, HEAD_DIM, S).transpose(0, 4, 1, 2, 3)
    wk = min(WINDOW, S)
    win_new = win_t[:, :, S - wk:].reshape(B_, 2, N_KV, HEAD_DIM, wk).transpose(0, 4, 1, 2, 3)
    return o.reshape(B_, S, N_HEADS * HEAD_DIM), rows, win_new


N_PAGES = PAST_LEN // PAGE_SIZE
N_CHUNK_S = PAST_LEN // CMP_STRIDE
N_SEL_S = -(-(PAST_LEN + DEC_SEQ) // SEL_BLOCK)
CUR_S = PAST_LEN // SEL_BLOCK
QROWS = 8


def compress_weights_paged(phi):
    R = CMP_BLOCK // CMP_STRIDE
    wr = phi.reshape(2, R, CMP_STRIDE, HEAD_DIM, HEAD_DIM)
    w = jnp.einsum('crjde,hy->cjhdrye', wr, jnp.eye(2, dtype=phi.dtype))
    return w.reshape(2, CMP_STRIDE * LANE, R * LANE).astype(MXU_DTYPE)


def _softmax_with_extra(s, s_new):
    m = jnp.maximum(jnp.max(s, axis=-1, keepdims=True), s_new)
    e = jnp.exp(s - m)
    e_new = jnp.exp(s_new - m)
    return e, e_new, jnp.sum(e, axis=-1, keepdims=True) + e_new


CHUNKS_PER_PAGE = PAGE_SIZE // CMP_STRIDE
SWEEP_PAGES = 64


def feature_major_pages(cache_nsa):
    d, n = cache_nsa.shape[:2]
    return cache_nsa.transpose(0, 1, 3, 4, 5, 2).reshape(d * n, 4, N_KV * HEAD_DIM, PAGE_SIZE)


def _cache_compress_body(c_ref, w_ref, o_ref, sk, sv):
    n_pages = c_ref.shape[0]

    def to_row_major(p, carry):
        r0 = pl.multiple_of(p * PAGE_SIZE, PAGE_SIZE)
        sk[pl.ds(r0, PAGE_SIZE), :] = c_ref[p, 0].T
        sv[pl.ds(r0, PAGE_SIZE), :] = c_ref[p, 1].T
        return carry

    lax.fori_loop(0, n_pages, to_row_major, 0, unroll=4)
    n = n_pages * CHUNKS_PER_PAGE
    for c, src in enumerate((sk, sv)):
        x = jnp.concatenate([src[pl.ds(j, n, stride=CMP_STRIDE), :] for j in range(CMP_STRIDE)], axis=1)
        ab = _mm(x, w_ref[0, c])
        o_ref[:, c * LANE:(c + 1) * LANE] = ab[:, 0:LANE]
        o_ref[:, (2 + c) * LANE:(3 + c) * LANE] = ab[:, LANE:2 * LANE]


def cache_compress(cache_fm, nsa_phi):
    n_total = cache_fm.shape[0]
    assert (n_total // DEPTH) % SWEEP_PAGES == 0
    tiles = n_total // DEPTH // SWEEP_PAGES
    wc = jnp.stack([compress_weights_paged(nsa_phi[l]) for l in range(DEPTH)])
    rows = SWEEP_PAGES * PAGE_SIZE
    return pl.pallas_call(
        _cache_compress_body,
        grid=(DEPTH * tiles,),
        in_specs=[pl.BlockSpec((SWEEP_PAGES, 2, LANE, PAGE_SIZE), lambda i: (i, 0, 0, 0)),
                  pl.BlockSpec((1,) + wc.shape[1:], lambda i: (i // tiles, 0, 0, 0))],
        out_specs=pl.BlockSpec((SWEEP_PAGES * CHUNKS_PER_PAGE, 4 * LANE), lambda i: (i, 0)),
        out_shape=jax.ShapeDtypeStruct((n_total * CHUNKS_PER_PAGE, 4 * LANE), F32),
        scratch_shapes=[pltpu.VMEM((rows, LANE), F32), pltpu.VMEM((rows, LANE), F32)],
        compiler_params=_cparams(),
        name="cache_compress",
    )(cache_fm, wc)


SAMPLE_GROUP = 2


def _nsa_sample_body(pt_ref, qa_ref, newb_ref, wnew_ref, gates_ref, *rest):
    n_pg = SAMPLE_GROUP * N_PAGES
    pages, abs_ = rest[:n_pg], rest[n_pg:2 * n_pg]
    y_ref, wout_ref = rest[-2:]
    gens = [_nsa_sample_one(u, qa_ref, newb_ref, wnew_ref, gates_ref, pages[u * N_PAGES:(u + 1) * N_PAGES],
                            abs_[u * N_PAGES:(u + 1) * N_PAGES], *rest[2 * n_pg:-2]) for u in range(SAMPLE_GROUP)]
    outs = [None] * SAMPLE_GROUP
    while any(o is None for o in outs):
        for u, gen in enumerate(gens):
            try:
                next(gen)
            except StopIteration as stop:
                outs[u] = stop.value
    y_ref[...] = jnp.stack([o[0] for o in outs])
    wout_ref[...] = jnp.stack([o[1] for o in outs])


def _nsa_sample_one(u, qa_ref, newb_ref, wnew_ref, gates_ref, pages, abs_, win_ref, bias_ref, gkc_ref, ov_ref, e_ref):
    qs = qa_ref[u]
    newb = newb_ref[u].astype(F32)
    lane = lax.broadcasted_iota(jnp.int32, (QROWS, LANE), 1)
    row = lax.broadcasted_iota(jnp.int32, (QROWS, LANE), 0)

    ab = jnp.concatenate([a[...] for a in abs_], axis=0)
    out = ab[:, 0:2 * LANE] + pltpu.roll(ab[:, 2 * LANE:4 * LANE], N_CHUNK_S - 1, 0) + bias_ref[...]
    kc = _seg_rmsnorm(out[:, 0:LANE], gkc_ref[...])
    vc = out[:, LANE:2 * LANE]
    yield

    s = _mm_nt(qs, kc)
    yield
    s = jnp.where(lane < N_CHUNK_S - 1, s, -jnp.inf)
    e = jnp.exp(s - jnp.max(s, axis=-1, keepdims=True))
    p_cmp = e / jnp.sum(e, axis=-1, keepdims=True)
    yield
    o_cmp = _mm(p_cmp, vc)
    imp = _mm(p_cmp, ov_ref[...])
    yield
    imp = imp +jnp.where(row % GQA == 0, pltpu.roll(imp, QROWS - 1, 0), pltpu.roll(imp, 1, 0))

    valid = lane <= CUR_S
    forced = (lane == 0) | (lane == CUR_S) | (lane == CUR_S - 1)
    score = jnp.where(valid, imp, -jnp.inf)
    score = jnp.where(forced & valid, jnp.inf, score)
    cnt = jnp.zeros((QROWS, LANE), F32)
    for i in range(N_SEL_S):
        ci = score[:, i:i + 1]
        cnt = cnt + jnp.where((ci > score) | ((ci == score) & (lane > i)), 1.0, 0.0)
    sel = jnp.where((cnt < SEL_TOPK) & (score > -jnp.inf), 1.0, 0.0)
    yield

    msel = _mm(sel, e_ref[...])
    s = jnp.concatenate([_mm(qs, pg[0, 0]) for pg in pages], axis=1)
    yield
    s = jnp.where(msel > 0.5, s, NEG)
    qf = qs.astype(F32)
    s_new = jnp.sum(qf * newb[:, 2 * LANE:3 * LANE], axis=-1, keepdims=True)
    s_new = jnp.where(sel[:, CUR_S:CUR_S + 1] > 0.5, s_new, NEG)
    e, e_new, d = _softmax_with_extra(s, s_new)
    yield
    acc_o = e_new.astype(MXU_DTYPE).astype(F32) * newb[:, 3 * LANE:4 * LANE]
    for p, pg in enumerate(pages):
        acc_o = acc_o + _mm_nt(e[:, p * PAGE_SIZE:(p + 1) * PAGE_SIZE], pg[0, 1])
    o_sel = acc_o / d
    yield

    s = _mm(qs, win_ref[u, 0])
    yield
    s_new =jnp.sum(qf * newb[:, 4 * LANE:5 * LANE], axis=-1, keepdims=True)
    e, e_new, d = _softmax_with_extra(s, s_new)
    o_win = (_mm_nt(e, win_ref[u, 1]) + e_new.astype(MXU_DTYPE).astype(F32) * newb[:, 5 * LANE:6 * LANE]) / d

    g = gates_ref[u]
    o = g[:, 0:1] * o_cmp + g[:, 1:2] * o_sel + g[:, 2:3] * o_win
    o_sw = pltpu.roll(o, HEAD_DIM, 1)
    lane1 = lax.broadcasted_iota(jnp.int32, (1, LANE), 1)
    ys = []
    for h in range(N_KV):
        a = (o if h == 0 else o_sw)[GQA * h:GQA * h + 1]
        b = (o if h == 1 else o_sw)[GQA * h + 1:GQA * h + 2]
        ys.append(jnp.where(lane1 < HEAD_DIM, a, b))
    lw = win_ref.shape[3]
    last = lax.broadcasted_iota(jnp.int32, (LANE, lw), 1) == lw - 1
    wouts = []
    for c in range(2):
        col = jnp.broadcast_to(wnew_ref[u][:, c * LANE:(c + 1) * LANE], (QROWS, LANE)).T[:, 0:1]
        wouts.append(jnp.where(last, col, pltpu.roll(win_ref[u, c], lw - 1, 1)))
    return jnp.concatenate(ys, axis=1), jnp.stack(wouts)


def _sample_constants():
    ci = np.arange(LANE)[:, None] * CMP_STRIDE
    sj = np.arange(LANE)[None, :] * SEL_BLOCK
    ov = ((ci < sj + SEL_BLOCK) & (ci + CMP_BLOCK > sj) & (np.arange(LANE)[:, None] < N_CHUNK_S - 1)
          & (np.arange(LANE)[None, :] < N_SEL_S))
    e = (np.arange(LANE)[:, None] == (np.arange(PAST_LEN)[None, :] // SEL_BLOCK))
    return jnp.asarray(ov, MXU_DTYPE), jnp.asarray(e, MXU_DTYPE)


def nsa_sample_pallas(proj, layer, cache_fm, cache_ab, page_table, win_fm, phi_b, qk_g):
    B_ = proj.shape[0]
    n_phys = cache_fm.shape[0] // DEPTH
    lw = win_fm.shape[3]
    assert page_table.shape == (B_, N_PAGES) and lw <= WINDOW and lw <= PAST_LEN and CUR_S == N_SEL_S - 1
    qa, kvb, _, rows_t, _, wnew, gates = nsa_prep(proj, qk_g, 1, B_)
    qa8 = jnp.pad(qa.astype(F32).reshape(B_, N_HEADS, LANE), ((0, 0), (0, QROWS - N_HEADS), (0, 0)))
    gates8 = jnp.pad(gates[:, :3 * N_HEADS].reshape(B_, N_HEADS, 3), ((0, 0), (0, QROWS - N_HEADS), (0, LANE - 3)))
    ov, e = _sample_constants()
    bias = jnp.concatenate([jnp.tile(phi_b[0], 2), jnp.tile(phi_b[1], 2)]).reshape(1, 2 * LANE)

    G = SAMPLE_GROUP
    assert B_ % G == 0
    seq_page = [(u, p) for u in range(G) for p in range(N_PAGES)]

    def page_spec(u, p):
        return pl.BlockSpec((1, 2, LANE, PAGE_SIZE), lambda b, pt: (layer * n_phys + pt[G * b + u, p], 1, 0, 0))

    def ab_spec(u, p):
        return pl.BlockSpec((CHUNKS_PER_PAGE, 4 * LANE), lambda b, pt: (layer * n_phys + pt[G * b + u, p], 0))

    def per_b(shape):
        return pl.BlockSpec((G,) + shape, lambda b, pt: (b, 0, 0))

    def const(a):
        return pl.BlockSpec(a.shape, lambda b, pt: (0,) * a.ndim)

    gkc = jnp.tile(qk_g[1], 2).reshape(1, LANE)
    y, wout = pl.pallas_call(
        _nsa_sample_body,
        grid_spec=pltpu.PrefetchScalarGridSpec(
            num_scalar_prefetch=1,
            grid=(B_ // G,),
            in_specs=[per_b((QROWS, LANE)), per_b((1, 6 * LANE)), per_b((1, 2 * LANE)), per_b((QROWS, LANE))]
                     + [page_spec(u, p) for u, p in seq_page] + [ab_spec(u, p) for u, p in seq_page]
                     + [pl.BlockSpec((G, 2, LANE, lw), lambda b, pt: (layer * (B_ // G) + b, 0, 0, 0)),
                        const(bias), const(gkc), const(ov), const(e)],
            out_specs=[per_b((1, 2 * LANE)), pl.BlockSpec((G, 2, LANE, lw), lambda b, pt: (b, 0, 0, 0))]),
        out_shape=[jax.ShapeDtypeStruct((B_, 1, 2 * LANE), F32),
                   jax.ShapeDtypeStruct((B_, 2, LANE, lw), F32)],
        compiler_params=_cparams(),
        name="nsa_sample",
    )(page_table, qa8, kvb.reshape(B_, 1, 6 * LANE), wnew.reshape(B_, 1, 2 * LANE), gates8,
      *([cache_fm] * (G * N_PAGES)), *([cache_ab] * (G * N_PAGES)), win_fm, bias, gkc, ov, e)
    rows = rows_t.reshape(4, N_KV, HEAD_DIM, B_).transpose(3, 0, 1, 2)[:, None]
    return (y.reshape(B_, 1, N_HEADS * HEAD_DIM), rows,
            wout.reshape(B_, 2, N_KV, HEAD_DIM, lw).transpose(0, 4, 1, 2, 3))


MIX_CHUNK = 512
HALO = 16
YM_W = POOL_W + RG_W + SC_W


def _expm1(x):
    p = jnp.full_like(x, 1.0 / 3628800.0)
    for c in (1.0 / 362880.0, 1.0 / 40320.0, 1.0 / 5040.0, 1.0 / 720.0, 1.0 / 120.0, 1.0 / 24.0, 1.0 / 6.0, 0.5, 1.0):
        p = p * x + c
    return jnp.where(jnp.abs(x) < 0.25, p * x, jnp.exp(x) - 1.0)


def _softplus(x):
    return jnp.maximum(x, 0.0) + jnp.log1p(jnp.exp(-jnp.abs(x)))


def _gelu_tanh(x):
    return 0.5 * x * (1.0 + jnp.tanh(np.sqrt(2.0 / np.pi).astype(np.float32) * (x + 0.044715 * (x * x * x))))


def _rg_coeffs(xc, wa, ba, wx, bx, lam):
    r = jax.nn.sigmoid(_mm(xc, wa) + ba)
    ig = jax.nn.sigmoid(_mm(xc, wx) + bx)
    log_a = (-RG_C * r) * _softplus(-lam)
    return jnp.exp(log_a), jnp.sqrt(-_expm1(2.0 * log_a)) * (ig * xc)


def _pool_select(s2, s4, s8, s16):
    lane = lax.broadcasted_iota(jnp.int32, s2.shape, 1)
    return jnp.where(lane < POOL_GROUP, s2, jnp.where(lane < 2 * POOL_GROUP, s4,
                                                      jnp.where(lane < 3 * POOL_GROUP, s8, s16)))


def _pool_count(pos, shape):
    lane = lax.broadcasted_iota(jnp.int32, shape, 1)
    win = jnp.left_shift(2, lane // POOL_GROUP)
    return jnp.minimum(win, pos + 1).astype(F32)


def _mixers_prompt_body(pu_ref, rx_ref, rg_ref, z_ref, bg_ref, cg_ref, pw_ref, ps_ref, cw_ref, cb_ref, wa_ref, ba_ref,
                        wx_ref, bx_ref, lam_ref, scw_ref, scb_ref, ym_ref, tails_ref, hlast_ref, halo, hcar):
    c = pl.program_id(1)
    tc = pu_ref.shape[0]

    @pl.when(c == 0)
    def _():
        halo[...] = jnp.zeros_like(halo)
        hcar[...] = jnp.zeros_like(hcar)

    pu, rx = pu_ref[...], rx_ref[...]
    u = cg_ref[...] * z_ref[...]
    ext = [jnp.concatenate([halo[i], v], axis=0) for i, v in enumerate((pu, rx, u))]

    def back(e, k):
        return pltpu.roll(e, k, 0)

    s2 = ext[0] + back(ext[0], 1)
    s4 = s2 + back(s2, 2)
    s8 = s4 + back(s4, 4)
    s16 = s8 + back(s8, 8)
    tot = _pool_select(s2, s4, s8, s16)[HALO:]
    pos = c * tc + lax.broadcasted_iota(jnp.int32, (tc, POOL_W), 0)
    d = tot / _pool_count(pos, (tc, POOL_W)) - pu
    ym_ref[:, 0:POOL_W] = _mm(d, pw_ref[...]) * ps_ref[...]

    cw = cw_ref[...]
    xc = cb_ref[...] + cw[RG_CONV - 1:RG_CONV] * rx
    for k in range(1, RG_CONV):
        xc = xc + cw[RG_CONV - 1 - k:RG_CONV - k] * back(ext[1], k)[HALO:]
    a, b = _rg_coeffs(xc, wa_ref[...], ba_ref[...], wx_ref[...], bx_ref[...], lam_ref[...])
    row = lax.broadcasted_iota(jnp.int32, (tc, RG_W), 0)
    k = 1
    while k < tc:
        a_prev = jnp.where(row < k, 1.0, pltpu.roll(a, k, 0))
        b_prev = jnp.where(row < k, 0.0, pltpu.roll(b, k, 0))
        b = a * b_prev + b
        a = a * a_prev
        k *= 2
    h = a * hcar[0:1] + b
    hcar[...] = jnp.broadcast_to(h[tc - 1:tc], hcar.shape)
    hlast_ref[0] = jnp.broadcast_to(h[tc - 1:tc], hcar.shape)
    ym_ref[:, POOL_W:POOL_W + RG_W] = h * _gelu_tanh(rg_ref[...])

    scw = scw_ref[...]
    v = scb_ref[...] + scw[SC_CONV - 1:SC_CONV] * u
    for k in range(1, SC_CONV):
        v = v + scw[SC_CONV - 1 - k:SC_CONV - k] * back(ext[2], k)[HALO:]
    ym_ref[:, POOL_W + RG_W:YM_W] = bg_ref[...] * v

    for i, val in enumerate((pu, rx, u)):
        halo[i] = val[tc - HALO:]
        tails_ref[0, i] = val[tc - HALO:]


def _block_diag(w):
    g, n, _ = w.shape
    return jnp.einsum('gij,gh->gihj', w, jnp.eye(g, dtype=w.dtype)).reshape(g * n, g * n)


def _mixer_params(lw):
    row = lambda a: a.reshape(1, -1)
    return [_block_diag(lw['pool_w']).astype(MXU_DTYPE), row(lw['pool_scale']), lw['rg_conv_w'], row(lw['rg_conv_b']),
            _block_diag(lw['rg_w_a']).astype(MXU_DTYPE), row(lw['rg_b_a']),
            _block_diag(lw['rg_w_x']).astype(MXU_DTYPE), row(lw['rg_b_x']), row(lw['rg_lambda']),
            lw['sc_conv_w'], row(lw['sc_conv_b'])]


def _proj_col_specs(rows, index):
    cols = (COL_POOL, COL_RX, COL_RGATE, COL_SC, COL_SC + SC_W, COL_SC + 2 * SC_W)
    return [pl.BlockSpec((rows, GROUP_W), functools.partial(index, col // GROUP_W)) for col in cols]


def mixers_prompt(proj, lw, B_, S):
    tc = min(MIX_CHUNK, S)
    nc = S // tc
    params = _mixer_params(lw)
    fixed = lambda a: pl.BlockSpec(a.shape, lambda b, c: (0,) * a.ndim)
    return pl.pallas_call(
        _mixers_prompt_body,
        grid=(B_, nc),
        in_specs=_proj_col_specs(tc, lambda col, b, c: (b * nc + c, col)) + [fixed(a) for a in params],
        out_specs=[pl.BlockSpec((tc, YM_W), lambda b, c: (b * nc + c, 0)),
                   pl.BlockSpec((1, 3, HALO, GROUP_W), lambda b, c: (b, 0, 0, 0)),
                   pl.BlockSpec((1, 8, RG_W), lambda b, c: (b, 0, 0))],
        out_shape=[jax.ShapeDtypeStruct((B_ * S, YM_W), F32),
                   jax.ShapeDtypeStruct((B_, 3, HALO, GROUP_W), F32),
                   jax.ShapeDtypeStruct((B_, 8, RG_W), F32)],
        scratch_shapes=[pltpu.VMEM((3, HALO, GROUP_W), F32), pltpu.VMEM((8, RG_W), F32)],
        compiler_params=_cparams(2),
        name="mixers_prompt",
    )(*([proj] * 6), *params)


def _mixers_sample_body(pos0, pu_ref, rx_ref, rg_ref, z_ref, bg_ref, cg_ref, pp_ref, rp_ref, h0_ref, sp_ref, pw_ref,
                        ps_ref, cw_ref, cb_ref, wa_ref, ba_ref, wx_ref, bx_ref, lam_ref, scw_ref, scb_ref,
                        ym_ref, pn_ref, rn_ref, hn_ref, sn_ref):
    pu, rx = pu_ref[...], rx_ref[...]
    u = cg_ref[...] * z_ref[...]
    run, sums = pu, {}
    for k in range(1, POOL_KEEP + 1):
        run = run + pp_ref[POOL_KEEP - k]
        sums[k + 1] = run
    tot = _pool_select(*(sums[w] for w in POOL_WINDOWS))
    d = tot / _pool_count(pos0, pu.shape) - pu
    ym_ref[:, 0:POOL_W] = _mm(d, pw_ref[...]) * ps_ref[...]
    for k in range(POOL_KEEP - 1):
        pn_ref[k] = pp_ref[k + 1]
    pn_ref[POOL_KEEP - 1] = pu

    cw = cw_ref[...]
    xc = cb_ref[...] + cw[RG_CONV - 1:RG_CONV] * rx
    for k in range(RG_CONV - 1):
        xc = xc + cw[k:k + 1] * rp_ref[k]
    a, b = _rg_coeffs(xc, wa_ref[...], ba_ref[...], wx_ref[...], bx_ref[...], lam_ref[...])
    h = b + a * h0_ref[...]
    hn_ref[...] = h
    ym_ref[:, POOL_W:POOL_W + RG_W] = h * _gelu_tanh(rg_ref[...])
    for k in range(RG_CONV - 2):
        rn_ref[k] = rp_ref[k + 1]
    rn_ref[RG_CONV - 2] = rx

    scw = scw_ref[...]
    v = scb_ref[...] + scw[SC_CONV - 1:SC_CONV] * u
    for k in range(SC_CONV - 1):
        v = v + scw[k:k + 1] * sp_ref[k]
    ym_ref[:, POOL_W + RG_W:YM_W] = bg_ref[...] * v
    for k in range(SC_CONV - 2):
        sn_ref[k] = sp_ref[k + 1]
    sn_ref[SC_CONV - 2] = u


def mixers_sample(proj, lw, pos0, pool_prev, rgc_prev, h0, sc_prev):
    B_ = proj.shape[0]
    params = _mixer_params(lw)
    states = [pool_prev.transpose(1, 0, 2), rgc_prev.transpose(1, 0, 2), h0, sc_prev.transpose(1, 0, 2)]
    full = lambda a: pl.BlockSpec(a.shape, lambda i: (0,) * a.ndim)
    ym, pn, rn, hn, sn = pl.pallas_call(
        functools.partial(_mixers_sample_body, pos0),
        grid=(1,),
        in_specs=_proj_col_specs(B_, lambda col, i: (0, col)) + [full(a) for a in states] + [full(a) for a in params],
        out_specs=[pl.BlockSpec((B_, YM_W), lambda i: (0, 0))] + [full(a) for a in states],
        out_shape=[jax.ShapeDtypeStruct((B_, YM_W), F32)] + [jax.ShapeDtypeStruct(a.shape, F32) for a in states],
        compiler_params=_cparams(),
        name="mixers_sample",
    )(*([proj] * 6), *states, *params)
    return ym, pn.transpose(1, 0, 2), rn.transpose(1, 0, 2), hn, sn.transpose(1, 0, 2)


ROUTE_W = LANE
GROUP_LANE0 = N_EXPERTS
MOE_TILE_PROMPT = 256
MOE_TILE_SAMPLE = 32
COMBINE_TILE = 256
FETCH_GROUPS = 8
FETCH_BUFS = 3


def _rms(x, g):
    return x * lax.rsqrt(jnp.mean(x * x, axis=-1, keepdims=True) + EPS) * g


def _mix_out_router_body(ym_ref, yn_ref, x_ref, og_ref, wo_ref, gf_ref, wr_ref, br_ref, tri_ref, x2_ref, xn_ref,
                         route_ref, cnt_ref, cnt_sc):
    og = og_ref[...]
    groups = (ym_ref[:, 0:POOL_W], ym_ref[:, POOL_W:POOL_W + RG_W], yn_ref[...], ym_ref[:, POOL_W + RG_W:YM_W])
    yn = jnp.concatenate([_rms(y, og[:, i * GROUP_W:(i + 1) * GROUP_W]) for i, y in enumerate(groups)], axis=1)
    x2 = x_ref[...] + _mm(yn, wo_ref[...])
    x2_ref[...] = x2
    xn = _rms(x2, gf_ref[...])
    bits = lax.bitcast_convert_type(xn.astype(jnp.bfloat16).astype(F32), jnp.uint32)
    half = xn.shape[1] // 2
    xn_ref[...] = (bits[:, half:] & jnp.uint32(0xFFFF0000)) | (bits[:, :half] >> 16)
    logits = _mm(xn, wr_ref[...]) + br_ref[...]
    lane = lax.broadcasted_iota(jnp.int32, logits.shape, 1)
    is_grp = (lane >= GROUP_LANE0) & (lane < GROUP_LANE0 + N_GROUPS)
    grp = jnp.where(is_grp, logits, -jnp.inf)
    gmax = jnp.max(grp, axis=-1, keepdims=True)
    gsel = jnp.min(jnp.where(grp == gmax, lane - GROUP_LANE0, N_GROUPS), axis=-1, keepdims=True)
    p_group = 1.0 / jnp.sum(jnp.where(is_grp, jnp.exp(logits - gmax), 0.0), axis=-1, keepdims=True)
    le = jnp.where((lane < N_EXPERTS) & (lane // EXP_PER_GROUP == gsel), logits, -jnp.inf)
    m1 = jnp.max(le, axis=-1, keepdims=True)
    i1 = jnp.min(jnp.where(le == m1, lane, LANE), axis=-1, keepdims=True)
    le2 = jnp.where(lane == i1, -jnp.inf, le)
    m2 = jnp.max(le2, axis=-1, keepdims=True)
    i2 = jnp.min(jnp.where(le2 == m2, lane, LANE), axis=-1, keepdims=True)
    e2 = jnp.exp(m2 - m1)
    g1 = p_group * (1.0 / (1.0 + e2))
    g2 = p_group * (e2 / (1.0 + e2))
    @pl.when(pl.program_id(0) == 0)
    def _():
        cnt_sc[...] = jnp.zeros_like(cnt_sc)

    oh = jnp.where((lane == i1) | (lane == i2), 1.0, 0.0)
    before = cnt_sc[0:1] + _mm(tri_ref[...], oh)
    r1 = jnp.sum(jnp.where(lane == i1, before, 0.0), axis=-1, keepdims=True)
    r2 = jnp.sum(jnp.where(lane == i2, before, 0.0), axis=-1, keepdims=True)
    total = cnt_sc[0:1] + jnp.sum(oh, axis=0, keepdims=True)
    cnt_sc[...] = jnp.broadcast_to(total, cnt_sc.shape)
    cnt_ref[...] = jnp.broadcast_to(total, cnt_ref.shape)
    vals = (i1.astype(F32), i2.astype(F32), g1, g2, r1, r2)
    route = jnp.zeros(logits.shape, F32)
    for k, v in enumerate(vals):
        route = jnp.where(lane == k, v, route)
    route_ref[...] = route


def mix_out_router(ym, y_nsa, x2d, lw):
    T, D = x2d.shape
    tm = min(256, T)
    wr = jnp.concatenate([lw['router_expert_w'], lw['router_group_w'],
                          jnp.zeros((D, ROUTE_W - N_EXPERTS - N_GROUPS), F32)], axis=1).astype(MXU_DTYPE)
    br = jnp.concatenate([lw['router_expert_b'], lw['router_group_b'],
                          jnp.zeros((ROUTE_W - N_EXPERTS - N_GROUPS,), F32)]).reshape(1, ROUTE_W)
    row = lambda i: (i, 0)
    fixed = lambda i: (0, 0)
    tri = jnp.asarray(np.tril(np.ones((tm, tm), np.float32), -1), MXU_DTYPE)
    return pl.pallas_call(
        _mix_out_router_body,
        grid=(T // tm,),
        in_specs=[pl.BlockSpec((tm, YM_W), row), pl.BlockSpec((tm, GROUP_W), row), pl.BlockSpec((tm, D), row),
                  pl.BlockSpec((1, MIX_W), fixed),
                  pl.BlockSpec((MIX_W, D), fixed), pl.BlockSpec((1, D), fixed), pl.BlockSpec((D, ROUTE_W), fixed),
                  pl.BlockSpec((1, ROUTE_W), fixed), pl.BlockSpec((tm, tm), fixed)],
        out_specs=[pl.BlockSpec((tm, D), row), pl.BlockSpec((tm, D // 2), row), pl.BlockSpec((tm, ROUTE_W), row),
                   pl.BlockSpec((8, ROUTE_W), fixed)],
        out_shape=[jax.ShapeDtypeStruct((T, D), F32), jax.ShapeDtypeStruct((T, D // 2), jnp.uint32),
                   jax.ShapeDtypeStruct((T, ROUTE_W), F32), jax.ShapeDtypeStruct((8, ROUTE_W), F32)],
        scratch_shapes=[pltpu.VMEM((8, ROUTE_W), F32)],
        compiler_params=_cparams(),
        name="mix_out_router",
    )(ym, y_nsa, x2d, lw['mix_out_g'].reshape(1, MIX_W), lw['w_out'].astype(MXU_DTYPE),
      lw['norm_ffn_g'].reshape(1, D), wr, br, tri)


def moe_schedule(route, counts, tile):
    T = route.shape[0]
    M = T * TOP_E
    fe = route[:, 0:TOP_E].astype(jnp.int32).reshape(M)
    rank = route[:, 4:4 + TOP_E].astype(jnp.int32).reshape(M)
    counts = counts.astype(jnp.int32)
    padded = (counts + tile - 1) // tile * tile
    pad_end = jnp.cumsum(padded)
    dest = ((pad_end - padded)[fe] + rank).astype(jnp.int32)
    n_blk = -(-M // tile) + N_EXPERTS
    tok = jnp.arange(M, dtype=jnp.int32) // TOP_E
    buf_tok = jnp.zeros((n_blk * tile,), jnp.int32).at[dest].set(tok, unique_indices=True, mode='promise_in_bounds')
    blk_exp = jnp.minimum(jnp.sum(pad_end[None, :] <= (jnp.arange(n_blk, dtype=jnp.int32) * tile)[:, None], axis=1),
                          N_EXPERTS - 1).astype(jnp.int32)
    n_used = (pad_end[-1:] // tile).astype(jnp.int32)
    return buf_tok, blk_exp, n_used, dest


def _moe_ffn_body(tile, tok_ref, bexp_ref, nused_ref, x_hbm, wgu_ref, wdn_ref, y_ref, xg, sem, wgu_bf, wdn_bf):
    j = pl.program_id(0)
    n = nused_ref[0]

    def gather(blk, slot):
        def body(r, c):
            t = tok_ref[blk * tile + r]
            pltpu.make_async_copy(x_hbm.at[pl.ds(t, 1)], xg.at[slot, pl.ds(r, 1)], sem.at[slot]).start()
            return c
        lax.fori_loop(0, tile, body, 0, unroll=8)

    @pl.when((j == 0) & (n > 0))
    def _():
        gather(0, 0)
        gather(jnp.minimum(1, n - 1), 1)

    def wait_block(slot):
        pltpu.make_async_copy(x_hbm.at[pl.ds(0, tile)], xg.at[slot], sem.at[slot]).wait()

    @pl.when(j < n)
    def _():
        slot = j % FETCH_BUFS
        nxt = jnp.minimum(j + 2, n - 1)
        dst = (j + 2) % FETCH_BUFS

        def fetch_group(g):
            per = tile // FETCH_GROUPS
            for r in range(g * per, (g + 1) * per):
                t = tok_ref[nxt * tile + r]
                pltpu.make_async_copy(x_hbm.at[pl.ds(t, 1)], xg.at[dst, pl.ds(r, 1)], sem.at[dst]).start()

        @pl.when((j == 0) | (bexp_ref[j] != bexp_ref[jnp.maximum(j - 1, 0)]))
        def _():
            wgu_bf[...] = wgu_ref[0].astype(wgu_bf.dtype)
            wdn_bf[...] = wdn_ref[0].astype(wdn_bf.dtype)

        wait_block(slot)
        u = xg[slot]
        dk = u.shape[1]
        x_lo = lax.bitcast_convert_type(u << 16, F32)
        x_hi = lax.bitcast_convert_type(u & jnp.uint32(0xFFFF0000), F32)
        half = FETCH_GROUPS // 2
        cg, cd = 2 * D_EXPERT // half, y_ref.shape[1] // half
        hs = []
        for c in range(half):
            fetch_group(c)
            hs.append(_mm(x_lo, wgu_bf[0:dk, c * cg:(c + 1) * cg]) + _mm(x_hi, wgu_bf[dk:2 * dk, c * cg:(c + 1) * cg]))
        h = jnp.concatenate(hs, axis=1)
        a, b = h[:, :D_EXPERT], h[:, D_EXPERT:]
        act = a * jax.nn.sigmoid(a) * b
        for c in range(half):
            fetch_group(half + c)
            y_ref[:, c * cd:(c + 1) * cd] = _mm(act, wdn_bf[:, c * cd:(c + 1) * cd])

        @pl.when(j + 1 >= n)
        def _():
            wait_block((j + 1) % FETCH_BUFS)
            wait_block(dst)

    @pl.when(j >= n)
    def _():
        y_ref[...] = jnp.zeros_like(y_ref)


def moe_ffn_pallas(xn, buf_tok, blk_exp, n_used, w_gu, w_down, tile):
    T, dk = xn.shape
    D = 2 * dk
    n_blk = blk_exp.shape[0]
    return pl.pallas_call(
        functools.partial(_moe_ffn_body, tile),
        grid_spec=pltpu.PrefetchScalarGridSpec(
            num_scalar_prefetch=3,
            grid=(n_blk,),
            in_specs=[pl.BlockSpec(memory_space=pl.ANY),
                      pl.BlockSpec((1, D, 2 * D_EXPERT), lambda j, tok, bexp, nu: (bexp[j], 0, 0)),
                      pl.BlockSpec((1, D_EXPERT, D), lambda j, tok, bexp, nu: (bexp[j], 0, 0))],
            out_specs=pl.BlockSpec((tile, D), lambda j, tok, bexp, nu: (j, 0)),
            scratch_shapes=[pltpu.VMEM((FETCH_BUFS, tile, dk), jnp.uint32), pltpu.SemaphoreType.DMA((FETCH_BUFS,)),
                            pltpu.VMEM((D, 2 * D_EXPERT), MXU_DTYPE), pltpu.VMEM((D_EXPERT, D), MXU_DTYPE)]),
        out_shape=jax.ShapeDtypeStruct((n_blk * tile, D), F32),
        compiler_params=_cparams(),
        name="moe_ffn",
    )(buf_tok, blk_exp, n_used, xn, w_gu, w_down)


def _moe_combine_body(tm, slots_ref, y_hbm, x2_ref, route_ref, o_ref, yb, sem):
    i = pl.program_id(0)
    nt = pl.num_programs(0)

    def gather(tile_i, buf):
        def body(r, c):
            for k in range(TOP_E):
                s = slots_ref[(tile_i * tm + r) * TOP_E + k]
                pltpu.make_async_copy(y_hbm.at[pl.ds(s, 1)], yb.at[buf, k, pl.ds(r, 1)], sem.at[buf]).start()
            return c
        lax.fori_loop(0, tm, body, 0, unroll=8)

    def wait_tile(b):
        for k in range(TOP_E):
            pltpu.make_async_copy(y_hbm.at[pl.ds(0, tm)], yb.at[b, k], sem.at[b]).wait()

    @pl.when(i == 0)
    def _():
        gather(0, 0)
        gather(jnp.minimum(1, nt - 1), 1)

    buf = i % FETCH_BUFS
    ahead = (i + 2) % FETCH_BUFS
    gather(jnp.minimum(i + 2, nt - 1), ahead)
    wait_tile(buf)
    r = route_ref[...]
    o_ref[...] = x2_ref[...] + (r[:, 2:3] * yb[buf, 0] + r[:, 3:4] * yb[buf, 1])

    @pl.when(i + 1 >= nt)
    def _():
        wait_tile((i + 1) % FETCH_BUFS)
        wait_tile(ahead)


def moe_combine_pallas(y, slots, x2, route):
    T, D = x2.shape
    tm = min(COMBINE_TILE, T)
    return pl.pallas_call(
        functools.partial(_moe_combine_body, tm),
        grid_spec=pltpu.PrefetchScalarGridSpec(
            num_scalar_prefetch=1,
            grid=(T // tm,),
            in_specs=[pl.BlockSpec(memory_space=pl.ANY),
                      pl.BlockSpec((tm, D), lambda i, s: (i, 0)),
                      pl.BlockSpec((tm, ROUTE_W), lambda i, s: (i, 0))],
            out_specs=pl.BlockSpec((tm, D), lambda i, s: (i, 0)),
            scratch_shapes=[pltpu.VMEM((FETCH_BUFS, TOP_E, tm, D), F32), pltpu.SemaphoreType.DMA((FETCH_BUFS,))]),
        out_shape=jax.ShapeDtypeStruct((T, D), F32),
        compiler_params=_cparams(),
        name="moe_combine",
    )(slots, y, x2, route)


def mix_out_moe(ym, y_nsa, x2d, lw, tile):
    x2, xn, route, counts = mix_out_router(ym, y_nsa, x2d, lw)
    buf_tok, blk_exp, n_used, slots = moe_schedule(route, counts[0, :N_EXPERTS], tile)
    y = moe_ffn_pallas(xn, buf_tok, blk_exp + lw['expert_base'], n_used, lw['exp_w_gu'], lw['exp_w_down'], tile)
    return moe_combine_pallas(y, slots, x2, route)


def split_cols(a, sizes):
    outs, o = [], 0
    for s in sizes:
        outs.append(a[..., o:o + s])
        o += s
    return outs


def layer_forward(x, pos0, lw, pool_prev, rgc_prev, rgh0, sc_prev, nsa_fn):
    B_, L, _ = x.shape
    w_perm = permute_w_in(lw['w_in']).astype(MXU_DTYPE)
    proj2d = norm_matmul(x.reshape(B_ * L, D_MODEL), lw['norm_mix_g'], w_perm)
    if pool_prev is None:
        ym, tails, hlast = mixers_prompt(proj2d, lw, B_, L)
        pool_new = tails[:, 0, HALO - POOL_KEEP:]
        rgc_new = tails[:, 1, HALO - (RG_CONV - 1):]
        sc_new = tails[:, 2, HALO - (SC_CONV - 1):]
        rgh_new = hlast[:, 0]
    else:
        ym, pool_new, rgc_new, rgh_new, sc_new = mixers_sample(proj2d, lw, pos0, pool_prev, rgc_prev, rgh0, sc_prev)
    y_nsa, nsa_rows, win_new = nsa_fn(proj2d, lw['nsa_phi'], lw['nsa_phi_b'], lw['nsa_qk_g'])
    x = mix_out_moe(ym, y_nsa.reshape(B_ * L, GROUP_W), x.reshape(B_ * L, D_MODEL), lw,
                    MOE_TILE_PROMPT if L > 1 else MOE_TILE_SAMPLE)
    return x.reshape(B_, L, D_MODEL), (nsa_rows, win_new, pool_new, rgc_new, rgh_new, sc_new)


def kernel(x_prompt, x_sample, cache_nsa, state_win_kv, state_pool, state_rg_conv, state_rg_h, state_sc_conv,
           page_table, norm_mix_g, w_in, pool_w, pool_scale, rg_conv_w, rg_conv_b, rg_w_a, rg_b_a, rg_w_x, rg_b_x,
           rg_lambda, nsa_phi, nsa_phi_b, nsa_qk_g, sc_conv_w, sc_conv_b, mix_out_g, w_out, norm_ffn_g,
           router_group_w, router_group_b, router_expert_w, router_expert_b, exp_w_gu, exp_w_down):
    past_len = page_table.shape[1] * cache_nsa.shape[2]
    xp, xs = x_prompt, x_sample
    cache3 = feature_major_pages(cache_nsa)
    win3 = state_win_kv.transpose(0, 1, 3, 4, 5, 2).reshape(DEPTH * state_win_kv.shape[1], 2, N_KV * HEAD_DIM,
                                                             state_win_kv.shape[2])
    cache_ab = cache_compress(cache3, nsa_phi)
    Bp = xp.shape[0]
    st_p, st_s = [], []
    for l in range(DEPTH):
        lw = dict(norm_mix_g=norm_mix_g[l], w_in=w_in[l], pool_w=pool_w[l], pool_scale=pool_scale[l],
                  rg_conv_w=rg_conv_w[l], rg_conv_b=rg_conv_b[l], rg_w_a=rg_w_a[l], rg_b_a=rg_b_a[l],
                  rg_w_x=rg_w_x[l], rg_b_x=rg_b_x[l], rg_lambda=rg_lambda[l], nsa_phi=nsa_phi[l],
                  nsa_phi_b=nsa_phi_b[l], nsa_qk_g=nsa_qk_g[l], sc_conv_w=sc_conv_w[l], sc_conv_b=sc_conv_b[l],
                  mix_out_g=mix_out_g[l], w_out=w_out[l], norm_ffn_g=norm_ffn_g[l],
                  router_group_w=router_group_w[l], router_group_b=router_group_b[l],
                  router_expert_w=router_expert_w[l], router_expert_b=router_expert_b[l],
                  exp_w_gu=exp_w_gu.reshape((DEPTH * N_EXPERTS,) + exp_w_gu.shape[2:]),
                  exp_w_down=exp_w_down.reshape((DEPTH * N_EXPERTS,) + exp_w_down.shape[2:]),
                  expert_base=l * N_EXPERTS)
        xp, sp = layer_forward(xp, 0, lw, None, None, None, None,
                               lambda p, phi, phi_b, g: nsa_prompt_pallas(p, Bp, xp.shape[1], phi, phi_b, g))
        xs, ss = layer_forward(xs, past_len, lw, state_pool[l], state_rg_conv[l], state_rg_h[l], state_sc_conv[l],
                               lambda p, phi, phi_b, g: nsa_sample_pallas(p, l, cache3, cache_ab, page_table, win3,
                                                                          phi_b, g))
        st_p.append(sp)
        st_s.append(ss)

    def stk(lst, i):
        return jnp.stack([s[i] for s in lst])

    return (xp, xs, stk(st_p, 0), stk(st_s, 0), stk(st_p, 1), stk(st_s, 1), stk(st_p, 2), stk(st_s, 2),
            stk(st_p, 3), stk(st_s, 3), stk(st_p, 4), stk(st_s, 4), stk(st_p, 5), stk(st_s, 5))
```

```python
import functools
import jax, jax.numpy as jnp
from jax import lax
import numpy as np
from jax.experimental import pallas as pl
from jax.experimental.pallas import tpu as pltpu

D_MODEL = 1024
BATCH = 4
SEQ = 4096
DEPTH = 2
DEC_BATCH = 128
DEC_SEQ = 1
PAST_LEN = 2048
PAGE_SIZE = 128

MIX_W = D_MODEL
GROUP_W = MIX_W // 4
POOL_W = GROUP_W
POOL_WINDOWS = (2, 4, 8, 16)
POOL_GROUP = POOL_W // len(POOL_WINDOWS)
POOL_KEEP = max(POOL_WINDOWS) - 1
RG_W = GROUP_W
RG_HEADS = 4
RG_BLOCK = RG_W // RG_HEADS
RG_CONV = 4
RG_C = 8.0
HEAD_DIM = 64
N_HEADS = GROUP_W // HEAD_DIM
N_KV = 2
GQA = N_HEADS // N_KV
CMP_BLOCK = 32
CMP_STRIDE = 16
SEL_BLOCK = 64
SEL_TOPK = 16
WINDOW = 512
Q_BLOCK = 128
SC_W = GROUP_W
SC_CONV = 3
N_GROUPS = 4
EXP_PER_GROUP = 8
N_EXPERTS = N_GROUPS * EXP_PER_GROUP
TOP_E = 2
D_EXPERT = 512
MOE_BLOCK = 128
EPS = 1e-6
SPLIT_SIZES = (POOL_W, RG_W, RG_W, N_HEADS * HEAD_DIM, 6 * N_KV * HEAD_DIM, 3 * N_HEADS, 3 * SC_W)
N_IN = sum(SPLIT_SIZES)

LANE = 128
ROW_TILE = 512
VMEM_LIMIT = 48 * 1024 * 1024
MXU_DTYPE = jnp.bfloat16
F32 = jnp.float32
NEG = -1e30

KV_W = 6 * N_KV * HEAD_DIM
COL_Q = 0
COL_KV = COL_Q + N_HEADS * HEAD_DIM
COL_POOL = COL_KV + KV_W
COL_RX = COL_POOL + POOL_W
COL_RGATE = COL_RX + RG_W
COL_SC = COL_RGATE + RG_W
COL_NG = COL_SC + 3 * SC_W
N_IN_PAD = COL_NG + LANE
SEL_TILE = 1024
N_SEL_PROMPT = SEQ // SEL_BLOCK


def _cparams(n_axes=1):
    return pltpu.CompilerParams(dimension_semantics=("arbitrary",) * n_axes, vmem_limit_bytes=VMEM_LIMIT)


def _mm(a, b):
    return jnp.dot(a.astype(MXU_DTYPE), b.astype(MXU_DTYPE), preferred_element_type=F32)


def _mm_nt(a, b):
    return lax.dot_general(a.astype(MXU_DTYPE), b.astype(MXU_DTYPE), (((1,), (1,)), ((), ())),
                           preferred_element_type=F32)


def permute_w_in(w):
    pu, rx, rgate, q, kv, ng, sc = split_cols(w, SPLIT_SIZES)
    pad = jnp.zeros((w.shape[0], LANE - ng.shape[1]), w.dtype)
    return jnp.concatenate([q, kv, pu, rx, rgate, sc, ng, pad], axis=1)


def _norm_matmul_body(x_ref, g_ref, w_ref, o_ref):
    xf = x_ref[...]
    h = xf * lax.rsqrt(jnp.mean(xf * xf, axis=-1, keepdims=True) + EPS) * g_ref[...]
    o_ref[...] = _mm(h, w_ref[...])


def norm_matmul(x2d, g, w):
    T, D = x2d.shape
    N = w.shape[1]
    tm = min(ROW_TILE, T)
    return pl.pallas_call(
        _norm_matmul_body,
        grid=(T // tm,),
        in_specs=[pl.BlockSpec((tm, D), lambda i: (i, 0)),
                  pl.BlockSpec((1, D), lambda i: (0, 0)),
                  pl.BlockSpec((D, N), lambda i: (0, 0))],
        out_specs=pl.BlockSpec((tm, N), lambda i: (i, 0)),
        out_shape=jax.ShapeDtypeStruct((T, N), F32),
        compiler_params=_cparams(),
        name="norm_in_proj",
    )(x2d, g.reshape(1, D), w)


def _seg_rmsnorm(x, g):
    x2 = x * x
    left = lax.broadcasted_iota(jnp.int32, x.shape, 1) < HEAD_DIM
    s_l = jnp.sum(jnp.where(left, x2, 0.0), axis=-1, keepdims=True)
    s_r = jnp.sum(jnp.where(left, 0.0, x2), axis=-1, keepdims=True)
    ms = jnp.where(left, s_l, s_r) * (1.0 / HEAD_DIM)
    return x * lax.rsqrt(ms + EPS) * g


def _nsa_prep_body(qkv_ref, ng_ref, g_ref, perm_ref, qa_ref, kvb_ref, rawb_ref, rows_t_ref, win_t_ref, win_ref,
                   gates_ref):
    g = g_ref[...]
    for hb in range(N_KV):
        qn = _seg_rmsnorm(qkv_ref[:, COL_Q + hb * LANE:COL_Q + (hb + 1) * LANE], g[0:1]) * (HEAD_DIM ** -0.5)
        qa_ref[:, hb * 2 * LANE:(hb + 1) * 2 * LANE] = _mm(qn, perm_ref[hb]).astype(qa_ref.dtype)
    comp = [qkv_ref[:, COL_KV + c * LANE:COL_KV + (c + 1) * LANE] for c in range(6)]
    comp[2] = _seg_rmsnorm(comp[2], g[2:3])
    comp[4] = _seg_rmsnorm(comp[4], g[3:4])
    for c in range(6):
        kvb_ref[:, c * LANE:(c + 1) * LANE] = comp[c].astype(kvb_ref.dtype)
    for c in range(2):
        rawb_ref[:, c * LANE:(c + 1) * LANE] = comp[c].astype(rawb_ref.dtype)
    for c in range(4):
        rows_t_ref[0, c * LANE:(c + 1) * LANE, :] = comp[c].T
    for c in range(2):
        win_t_ref[0, c * LANE:(c + 1) * LANE, :] = comp[4 + c].T
        win_ref[:, c * LANE:(c + 1) * LANE] = comp[4 + c]
    gates_ref[...] = jax.nn.sigmoid(ng_ref[...])


def _q_place_matrices():
    p = np.zeros((N_KV, LANE, 2 * LANE), np.float32)
    for hb in range(N_KV):
        for gq in range(GQA):
            for d in range(HEAD_DIM):
                p[hb, gq * HEAD_DIM + d, gq * LANE + hb * HEAD_DIM + d] = 1.0
    return jnp.asarray(p, MXU_DTYPE)


def nsa_prep(proj, qk_g, B_, S):
    T = proj.shape[0]
    tm = min(ROW_TILE, S)
    tpb = S // tm
    qkv_w = COL_POOL
    g4 = jnp.tile(qk_g, (1, 2))
    return pl.pallas_call(
        _nsa_prep_body,
        grid=(T // tm,),
        in_specs=[pl.BlockSpec((tm, qkv_w), lambda i: (i, 0)),
                  pl.BlockSpec((tm, LANE), lambda i: (i, COL_NG // LANE)),
                  pl.BlockSpec((4, LANE), lambda i: (0, 0)),
                  pl.BlockSpec((N_KV, LANE, 2 * LANE), lambda i: (0, 0, 0))],
        out_specs=[pl.BlockSpec((tm, 4 * LANE), lambda i: (i, 0)),
                   pl.BlockSpec((tm, 6 * LANE), lambda i: (i, 0)),
                   pl.BlockSpec((tm, 2 * LANE), lambda i: (i, 0)),
                   pl.BlockSpec((1, 4 * LANE, tm), lambda i: (i // tpb, 0, i % tpb)),
                   pl.BlockSpec((1, 2 * LANE, tm), lambda i: (i // tpb, 0, i % tpb)),
                   pl.BlockSpec((tm, 2 * LANE), lambda i: (i, 0)),
                   pl.BlockSpec((tm, LANE), lambda i: (i, 0))],
        out_shape=[jax.ShapeDtypeStruct((T, 4 * LANE), MXU_DTYPE),
                   jax.ShapeDtypeStruct((T, 6 * LANE), MXU_DTYPE),
                   jax.ShapeDtypeStruct((T, 2 * LANE), MXU_DTYPE),
                   jax.ShapeDtypeStruct((B_, 4 * LANE, S), F32),
                   jax.ShapeDtypeStruct((B_, 2 * LANE, S), F32),
                   jax.ShapeDtypeStruct((T, 2 * LANE), F32),
                   jax.ShapeDtypeStruct((T, LANE), F32)],
        compiler_params=_cparams(),
        name="nsa_prep",
    )(proj, proj, g4, _q_place_matrices())


def compress_weights(phi):
    R = CMP_BLOCK // CMP_STRIDE
    wr = phi.reshape(2, R, CMP_STRIDE, HEAD_DIM, HEAD_DIM)
    eye = jnp.eye(2, dtype=phi.dtype)
    w = jnp.einsum('crjde,cx,hy->rjchdxye', wr, eye, eye)
    return w.reshape(R, CMP_STRIDE * 2 * LANE, 2 * LANE).astype(MXU_DTYPE)


def _compress_body(x_ref, w_ref, b_ref, g_ref, kc_ref, vc_ref):
    x = x_ref[0]
    nch = x.shape[0]
    a = _mm(x, w_ref[0])
    bm = _mm(x, w_ref[1])
    out = a + pltpu.roll(bm, nch - 1, 0) + b_ref[...]
    kc_ref[0] = _seg_rmsnorm(out[:, 0:LANE], g_ref[...]).astype(kc_ref.dtype)
    vc_ref[0] = out[:, LANE:2 * LANE].astype(vc_ref.dtype)


def nsa_compress_pallas(rawb3, wc, phi_b, g_kc):
    B_, nch, K = rawb3.shape
    bias = jnp.concatenate([jnp.tile(phi_b[0], 2), jnp.tile(phi_b[1], 2)]).reshape(1, 2 * LANE)
    return pl.pallas_call(
        _compress_body,
        grid=(B_,),
        in_specs=[pl.BlockSpec((1, nch, K), lambda b: (b, 0, 0)),
                  pl.BlockSpec(wc.shape, lambda b: (0, 0, 0)),
                  pl.BlockSpec((1, 2 * LANE), lambda b: (0, 0)),
                  pl.BlockSpec((1, LANE), lambda b: (0, 0))],
        out_specs=[pl.BlockSpec((1, nch, LANE), lambda b: (b, 0, 0)),
                   pl.BlockSpec((1, nch, LANE), lambda b: (b, 0, 0))],
        out_shape=[jax.ShapeDtypeStruct((B_, nch, LANE), MXU_DTYPE),
                   jax.ShapeDtypeStruct((B_, nch, LANE), MXU_DTYPE)],
        compiler_params=_cparams(),
        name="nsa_compress",
    )(rawb3, wc, bias, jnp.tile(g_kc, 2).reshape(1, LANE))


def _online_update(carry, s, v):
    m, l, acc = carry
    m_new = jnp.maximum(m, jnp.max(s, axis=-1, keepdims=True))
    alpha = jnp.exp(m - m_new)
    p = jnp.exp(s - m_new)
    l = alpha * l + jnp.sum(p, axis=-1, keepdims=True)
    acc = alpha * acc + _mm(p, v)
    return m_new, l, acc


def _select_blocks(imp, start):
    n_sel = N_SEL_PROMPT
    sc_t = imp.T[0:n_sel]
    blk = lax.broadcasted_iota(jnp.int32, sc_t.shape, 0)
    cur = (start + lax.broadcasted_iota(jnp.int32, sc_t.shape, 1)) // SEL_BLOCK
    valid = blk <= cur
    forced = (blk == 0) | (blk == cur) | (blk == cur - 1)
    score = jnp.where(valid, sc_t, -jnp.inf)
    score = jnp.where(forced & valid, jnp.inf, score)
    sub = 8
    groups = [score[g * sub:(g + 1) * sub] for g in range(n_sel // sub)]
    cnts = [jnp.zeros((sub, sc_t.shape[1]), F32) for _ in groups]
    row = lax.broadcasted_iota(jnp.int32, (sub, sc_t.shape[1]), 0)
    for i in range(n_sel):
        ri = score[i:i + 1, :]
        for g, sg in enumerate(groups):
            if (g + 1) * sub - 1 < i:
                beat = jnp.where(ri > sg, 1.0, 0.0)
            elif g * sub > i:
                beat = jnp.where(ri >= sg, 1.0, 0.0)
            else:
                beat = jnp.where(row + g * sub > i, jnp.where(ri >= sg, 1.0, 0.0), jnp.where(ri > sg, 1.0, 0.0))
            cnts[g] = cnts[g] + beat
    cnt = jnp.concatenate(cnts, axis=0)
    sel_t = jnp.where((cnt < SEL_TOPK) & (score > -jnp.inf), 1.0, 0.0)
    sel_t = jnp.concatenate([sel_t, jnp.zeros((LANE - n_sel, sc_t.shape[1]), F32)], axis=0)
    return sel_t.T


def _nsa_attn_body(qa_ref, gates_ref, kc_ref, vc_ref, kv_ref, ov_ref, e_ref, o_ref):
    i = pl.program_id(1)
    start = i * Q_BLOCK
    Q = Q_BLOCK
    R = GQA * Q
    t_row = start + lax.broadcasted_iota(jnp.int32, (R, 1), 0) % Q
    gates = gates_ref[...]
    lane_q = lax.broadcasted_iota(jnp.int32, (Q, LANE), 1)
    heads = range(N_KV)
    qs = [jnp.concatenate([qa_ref[:, (h * GQA + gq) * LANE:(h * GQA + gq + 1) * LANE] for gq in range(GQA)], axis=0)
          for h in heads]

    o_cmps, sel_bias = [], []
    kc = kc_ref[0]
    ncmp = kc.shape[0]
    cmp_end = lax.broadcasted_iota(jnp.int32, (R, ncmp), 1) * CMP_STRIDE + (CMP_BLOCK - 1)
    for h in heads:
        s = jnp.where(cmp_end <= t_row, _mm_nt(qs[h], kc), -jnp.inf)
        m = jnp.max(s, axis=-1, keepdims=True)
        e = jnp.exp(s - jnp.where(m > -jnp.inf, m, 0.0))
        d = jnp.sum(e, axis=-1, keepdims=True)
        p_cmp = e / jnp.where(d > 0, d, 1.0)
        o_cmps.append(_mm(p_cmp, vc_ref[0]))
        imp = _mm(p_cmp[0:Q], ov_ref[...]) + _mm(p_cmp[Q:R], ov_ref[...])
        sel = _select_blocks(imp, start)
        bias = jnp.concatenate([jnp.where(sel > 0.5, 0.0, NEG)] * GQA, axis=0)
        sel_bias.append(jnp.concatenate([qs[h], bias.astype(MXU_DTYPE)], axis=1))

    def sel_scores(j):
        off = pl.multiple_of(j * SEL_TILE, SEL_TILE)
        k = jnp.concatenate([kv_ref[pl.ds(off, SEL_TILE), 2 * LANE:3 * LANE], e_ref[pl.ds(off, SEL_TILE), :]], axis=1)
        v = kv_ref[pl.ds(off, SEL_TILE), 3 * LANE:4 * LANE]
        return off, v, [_mm_nt(sel_bias[h], k) for h in heads]

    def sel_step(j, carry):
        _, v, ss = sel_scores(j)
        return tuple(_online_update(carry[h], ss[h], v) for h in heads)

    init = (jnp.full((R, 1), NEG, F32), jnp.zeros((R, 1), F32), jnp.zeros((R, LANE), F32))
    n_tiles = (start + Q + SEL_TILE - 1) // SEL_TILE
    carry = lax.fori_loop(0, n_tiles - 1, sel_step, (init,) * N_KV)
    off, v, ss = sel_scores(n_tiles - 1)
    causal = off + lax.broadcasted_iota(jnp.int32, (R, SEL_TILE), 1) <= t_row
    o_sels = []
    for h in heads:
        _, l_s, acc_s = _online_update(carry[h], jnp.where(causal, ss[h], NEG), v)
        o_sels.append(acc_s / l_s)

    n_w = WINDOW // Q + 1
    offs = [pl.multiple_of(jnp.maximum(i - kk, 0) * Q, Q) for kk in range(n_w)]
    kw = jnp.concatenate([kv_ref[pl.ds(o, Q), 4 * LANE:5 * LANE] for o in offs], axis=0)
    vw = jnp.concatenate([kv_ref[pl.ds(o, Q), 5 * LANE:6 * LANE] for o in offs], axis=0)
    lane_w = lax.broadcasted_iota(jnp.int32, (1, n_w * Q), 1)
    w_pos = (i - lane_w // Q) * Q + lane_w % Q
    wd = t_row - w_pos
    wmask = (w_pos >= 0) & (wd >= 0) & (wd <= WINDOW)
    o_wins = []
    for h in heads:
        s = jnp.where(wmask, _mm_nt(qs[h], kw), NEG)
        p = jnp.exp(s - jnp.max(s, axis=-1, keepdims=True))
        o_wins.append(_mm(p, vw) / jnp.sum(p, axis=-1, keepdims=True))

    for h in heads:
        o_cmp, o_sel, o_win = o_cmps[h], o_sels[h], o_wins[h]
        outs = []
        for gq in range(GQA):
            c0 = (h * GQA + gq) * 3
            rs = slice(gq * Q, (gq + 1) * Q)
            og = (gates[:, c0:c0 + 1] * o_cmp[rs] + gates[:, c0 + 1:c0 + 2] * o_sel[rs]
                  + gates[:, c0 + 2:c0 + 3] * o_win[rs])
            outs.append(og if gq == h else pltpu.roll(og, HEAD_DIM, 1))
        o_ref[:, h * LANE:(h + 1) * LANE] = jnp.where(lane_q < HEAD_DIM, outs[0], outs[1])


def _sel_constants(S):
    ncmp_rows = S // CMP_STRIDE
    ci = np.arange(ncmp_rows)[:, None] * CMP_STRIDE
    sj = np.arange(LANE)[None, :] * SEL_BLOCK
    ov = ((ci < sj + SEL_BLOCK) & (ci + CMP_BLOCK > sj) & (np.arange(LANE)[None, :] < S // SEL_BLOCK))
    e = (np.arange(S)[:, None] // SEL_BLOCK == np.arange(LANE)[None, :])
    return jnp.asarray(ov, MXU_DTYPE), jnp.asarray(e, MXU_DTYPE)


def nsa_attn_prompt(qa, gates, kc, vc, kvb, B_, S):
    nq = S // Q_BLOCK
    nch = S // CMP_STRIDE
    ov, e3 = _sel_constants(S)
    return pl.pallas_call(
        _nsa_attn_body,
        grid=(B_, nq),
        in_specs=[pl.BlockSpec((Q_BLOCK, 4 * LANE), lambda b, i: (b * nq + i, 0)),
                  pl.BlockSpec((Q_BLOCK, LANE), lambda b, i: (b * nq + i, 0)),
                  pl.BlockSpec((1, nch, LANE), lambda b, i: (b, 0, 0)),
                  pl.BlockSpec((1, nch, LANE), lambda b, i: (b, 0, 0)),
                  pl.BlockSpec((S, 6 * LANE), lambda b, i: (b, 0)),
                  pl.BlockSpec(ov.shape, lambda b, i: (0, 0)),
                  pl.BlockSpec(e3.shape, lambda b, i: (0, 0))],
        out_specs=pl.BlockSpec((Q_BLOCK, 2 * LANE), lambda b, i: (b * nq + i, 0)),
        out_shape=jax.ShapeDtypeStruct((B_ * S, N_HEADS * HEAD_DIM), F32),
        compiler_params=_cparams(2),
        name="nsa_attn_prompt",
    )(qa, gates, kc, vc, kvb, ov, e3)


def nsa_prompt_pallas(proj, B_, S, phi, phi_b, qk_g):
    qa, kvb, rawb, rows_t, win_t, _, gates = nsa_prep(proj, qk_g, B_, S)
    nch = S // CMP_STRIDE
    kc, vc = nsa_compress_pallas(rawb.reshape(B_, nch, CMP_STRIDE * 2 * LANE), compress_weights(phi), phi_b, qk_g[1])
    o = nsa_attn_prompt(qa, gates, kc, vc, kvb, B_, S)
    rows = rows_t.reshape(B_, 4, N_KV, HEAD_DIM, S).transpose(0, 4, 1, 2, 3)
    wk = min(WINDOW, S)
    win_new = win_t[:, :, S - wk:].reshape(B_, 2, N_KV, HEAD_DIM, wk).transpose(0, 4, 1, 2, 3)
    return o.reshape(B_, S, N_HEADS * HEAD_DIM), rows, win_new


N_PAGES = PAST_LEN // PAGE_SIZE
N_CHUNK_S = PAST_LEN // CMP_STRIDE
N_SEL_S = -(-(PAST_LEN + DEC_SEQ) // SEL_BLOCK)
CUR_S = PAST_LEN // SEL_BLOCK
QROWS = 8


def compress_weights_paged(phi):
    R = CMP_BLOCK // CMP_STRIDE
    wr = phi.reshape(2, R, CMP_STRIDE, HEAD_DIM, HEAD_DIM)
    w = jnp.einsum('crjde,hy->cjhdrye', wr, jnp.eye(2, dtype=phi.dtype))
    return w.reshape(2, CMP_STRIDE * LANE, R * LANE).astype(MXU_DTYPE)


def _softmax_with_extra(s, s_new):
    m = jnp.maximum(jnp.max(s, axis=-1, keepdims=True), s_new)
    e = jnp.exp(s - m)
    e_new = jnp.exp(s_new - m)
    return e, e_new, jnp.sum(e, axis=-1, keepdims=True) + e_new


CHUNKS_PER_PAGE = PAGE_SIZE // CMP_STRIDE
SWEEP_PAGES = 64


def feature_major_pages(cache_nsa):
    d, n = cache_nsa.shape[:2]
    return cache_nsa.transpose(0, 1, 3, 4, 5, 2).reshape(d * n, 4, N_KV * HEAD_DIM, PAGE_SIZE)


def _cache_compress_body(c_ref, w_ref, o_ref, sk, sv):
    n_pages = c_ref.shape[0]

    def to_row_major(p, carry):
        r0 = pl.multiple_of(p * PAGE_SIZE, PAGE_SIZE)
        sk[pl.ds(r0, PAGE_SIZE), :] = c_ref[p, 0].T
        sv[pl.ds(r0, PAGE_SIZE), :] = c_ref[p, 1].T
        return carry

    lax.fori_loop(0, n_pages, to_row_major, 0, unroll=4)
    n = n_pages * CHUNKS_PER_PAGE
    for c, src in enumerate((sk, sv)):
        x = jnp.concatenate([src[pl.ds(j, n, stride=CMP_STRIDE), :] for j in range(CMP_STRIDE)], axis=1)
        ab = _mm(x, w_ref[0, c])
        o_ref[:, c * LANE:(c + 1) * LANE] = ab[:, 0:LANE]
        o_ref[:, (2 + c) * LANE:(3 + c) * LANE] = ab[:, LANE:2 * LANE]


def cache_compress(cache_fm, nsa_phi):
    n_total = cache_fm.shape[0]
    assert (n_total // DEPTH) % SWEEP_PAGES == 0
    tiles = n_total // DEPTH // SWEEP_PAGES
    wc = jnp.stack([compress_weights_paged(nsa_phi[l]) for l in range(DEPTH)])
    rows = SWEEP_PAGES * PAGE_SIZE
    return pl.pallas_call(
        _cache_compress_body,
        grid=(DEPTH * tiles,),
        in_specs=[pl.BlockSpec((SWEEP_PAGES, 2, LANE, PAGE_SIZE), lambda i: (i, 0, 0, 0)),
                  pl.BlockSpec((1,) + wc.shape[1:], lambda i: (i // tiles, 0, 0, 0))],
        out_specs=pl.BlockSpec((SWEEP_PAGES * CHUNKS_PER_PAGE, 4 * LANE), lambda i: (i, 0)),
        out_shape=jax.ShapeDtypeStruct((n_total * CHUNKS_PER_PAGE, 4 * LANE), F32),
        scratch_shapes=[pltpu.VMEM((rows, LANE), F32), pltpu.VMEM((rows, LANE), F32)],
        compiler_params=_cparams(),
        name="cache_compress",
    )(cache_fm, wc)


SAMPLE_GROUP = 2


def _nsa_sample_body(pt_ref, qa_ref, newb_ref, wnew_ref, gates_ref, *rest):
    n_pg = SAMPLE_GROUP * N_PAGES
    pages, abs_ = rest[:n_pg], rest[n_pg:2 * n_pg]
    y_ref, wout_ref = rest[-2:]
    gens = [_nsa_sample_one(u, qa_ref, newb_ref, wnew_ref, gates_ref, pages[u * N_PAGES:(u + 1) * N_PAGES],
                            abs_[u * N_PAGES:(u + 1) * N_PAGES], *rest[2 * n_pg:-2]) for u in range(SAMPLE_GROUP)]
    outs = [None] * SAMPLE_GROUP
    while any(o is None for o in outs):
        for u, gen in enumerate(gens):
            try:
                next(gen)
            except StopIteration as stop:
                outs[u] = stop.value
    y_ref[...] = jnp.stack([o[0] for o in outs])
    wout_ref[...] = jnp.stack([o[1] for o in outs])


def _nsa_sample_one(u, qa_ref, newb_ref, wnew_ref, gates_ref, pages, abs_, win_ref, bias_ref, gkc_ref, ov_ref, e_ref):
    qs = qa_ref[u]
    newb = newb_ref[u].astype(F32)
    lane = lax.broadcasted_iota(jnp.int32, (QROWS, LANE), 1)
    row = lax.broadcasted_iota(jnp.int32, (QROWS, LANE), 0)

    ab = jnp.concatenate([a[...] for a in abs_], axis=0)
    out = ab[:, 0:2 * LANE] + pltpu.roll(ab[:, 2 * LANE:4 * LANE], N_CHUNK_S - 1, 0) + bias_ref[...]
    kc = _seg_rmsnorm(out[:, 0:LANE], gkc_ref[...])
    vc = out[:, LANE:2 * LANE]
    yield

    s = _mm_nt(qs, kc)
    yield
    s = jnp.where(lane < N_CHUNK_S - 1, s, -jnp.inf)
    e = jnp.exp(s - jnp.max(s, axis=-1, keepdims=True))
    p_cmp = e / jnp.sum(e, axis=-1, keepdims=True)
    yield
    o_cmp = _mm(p_cmp, vc)
    imp = _mm(p_cmp, ov_ref[...])
    yield
    imp = imp +jnp.where(row % GQA == 0, pltpu.roll(imp, QROWS - 1, 0), pltpu.roll(imp, 1, 0))

    valid = lane <= CUR_S
    forced = (lane == 0) | (lane == CUR_S) | (lane == CUR_S - 1)
    score = jnp.where(valid, imp, -jnp.inf)
    score = jnp.where(forced & valid, jnp.inf, score)
    cnt = jnp.zeros((QROWS, LANE), F32)
    for i in range(N_SEL_S):
        ci = score[:, i:i + 1]
        cnt = cnt + jnp.where((ci > score) | ((ci == score) & (lane > i)), 1.0, 0.0)
    sel = jnp.where((cnt < SEL_TOPK) & (score > -jnp.inf), 1.0, 0.0)
    yield

    msel = _mm(sel, e_ref[...])
    s = jnp.concatenate([_mm(qs, pg[0, 0]) for pg in pages], axis=1)
    yield
    s = jnp.where(msel > 0.5, s, NEG)
    qf = qs.astype(F32)
    s_new = jnp.sum(qf * newb[:, 2 * LANE:3 * LANE], axis=-1, keepdims=True)
    s_new = jnp.where(sel[:, CUR_S:CUR_S + 1] > 0.5, s_new, NEG)
    e, e_new, d = _softmax_with_extra(s, s_new)
    yield
    acc_o = e_new.astype(MXU_DTYPE).astype(F32) * newb[:, 3 * LANE:4 * LANE]
    for p, pg in enumerate(pages):
        acc_o = acc_o + _mm_nt(e[:, p * PAGE_SIZE:(p + 1) * PAGE_SIZE], pg[0, 1])
    o_sel = acc_o / d
    yield

    s = _mm(qs, win_ref[u, 0])
    yield
    s_new =jnp.sum(qf * newb[:, 4 * LANE:5 * LANE], axis=-1, keepdims=True)
    e, e_new, d = _softmax_with_extra(s, s_new)
    o_win = (_mm_nt(e, win_ref[u, 1]) + e_new.astype(MXU_DTYPE).astype(F32) * newb[:, 5 * LANE:6 * LANE]) / d

    g = gates_ref[u]
    o = g[:, 0:1] * o_cmp + g[:, 1:2] * o_sel + g[:, 2:3] * o_win
    o_sw = pltpu.roll(o, HEAD_DIM, 1)
    lane1 = lax.broadcasted_iota(jnp.int32, (1, LANE), 1)
    ys = []
    for h in range(N_KV):
        a = (o if h == 0 else o_sw)[GQA * h:GQA * h + 1]
        b = (o if h == 1 else o_sw)[GQA * h + 1:GQA * h + 2]
        ys.append(jnp.where(lane1 < HEAD_DIM, a, b))
    lw = win_ref.shape[3]
    last = lax.broadcasted_iota(jnp.int32, (LANE, lw), 1) == lw - 1
    wouts = []
    for c in range(2):
        col = jnp.broadcast_to(wnew_ref[u][:, c * LANE:(c + 1) * LANE], (QROWS, LANE)).T[:, 0:1]
        wouts.append(jnp.where(last, col, pltpu.roll(win_ref[u, c], lw - 1, 1)))
    return jnp.concatenate(ys, axis=1), jnp.stack(wouts)


def _sample_constants():
    ci = np.arange(LANE)[:, None] * CMP_STRIDE
    sj = np.arange(LANE)[None, :] * SEL_BLOCK
    ov = ((ci < sj + SEL_BLOCK) & (ci + CMP_BLOCK > sj) & (np.arange(LANE)[:, None] < N_CHUNK_S - 1)
          & (np.arange(LANE)[None, :] < N_SEL_S))
    e = (np.arange(LANE)[:, None] == (np.arange(PAST_LEN)[None, :] // SEL_BLOCK))
    return jnp.asarray(ov, MXU_DTYPE), jnp.asarray(e, MXU_DTYPE)


def nsa_sample_pallas(proj, layer, cache_fm, cache_ab, page_table, win_fm, phi_b, qk_g):
    B_ = proj.shape[0]
    n_phys = cache_fm.shape[0] // DEPTH
    lw = win_fm.shape[3]
    assert page_table.shape == (B_, N_PAGES) and lw <= WINDOW and lw <= PAST_LEN and CUR_S == N_SEL_S - 1
    qa, kvb, _, rows_t, _, wnew, gates = nsa_prep(proj, qk_g, 1, B_)
    qa8 = jnp.pad(qa.astype(F32).reshape(B_, N_HEADS, LANE), ((0, 0), (0, QROWS - N_HEADS), (0, 0)))
    gates8 = jnp.pad(gates[:, :3 * N_HEADS].reshape(B_, N_HEADS, 3), ((0, 0), (0, QROWS - N_HEADS), (0, LANE - 3)))
    ov, e = _sample_constants()
    bias = jnp.concatenate([jnp.tile(phi_b[0], 2), jnp.tile(phi_b[1], 2)]).reshape(1, 2 * LANE)

    G = SAMPLE_GROUP
    assert B_ % G == 0
    seq_page = [(u, p) for u in range(G) for p in range(N_PAGES)]

    def page_spec(u, p):
        return pl.BlockSpec((1, 2, LANE, PAGE_SIZE), lambda b, pt: (layer * n_phys + pt[G * b + u, p], 1, 0, 0))

    def ab_spec(u, p):
        return pl.BlockSpec((CHUNKS_PER_PAGE, 4 * LANE), lambda b, pt: (layer * n_phys + pt[G * b + u, p], 0))

    def per_b(shape):
        return pl.BlockSpec((G,) + shape, lambda b, pt: (b, 0, 0))

    def const(a):
        return pl.BlockSpec(a.shape, lambda b, pt: (0,) * a.ndim)

    gkc = jnp.tile(qk_g[1], 2).reshape(1, LANE)
    y, wout = pl.pallas_call(
        _nsa_sample_body,
        grid_spec=pltpu.PrefetchScalarGridSpec(
            num_scalar_prefetch=1,
            grid=(B_ // G,),
            in_specs=[per_b((QROWS, LANE)), per_b((1, 6 * LANE)), per_b((1, 2 * LANE)), per_b((QROWS, LANE))]
                     + [page_spec(u, p) for u, p in seq_page] + [ab_spec(u, p) for u, p in seq_page]
                     + [pl.BlockSpec((G, 2, LANE, lw), lambda b, pt: (layer * (B_ // G) + b, 0, 0, 0)),
                        const(bias), const(gkc), const(ov), const(e)],
            out_specs=[per_b((1, 2 * LANE)), pl.BlockSpec((G, 2, LANE, lw), lambda b, pt: (b, 0, 0, 0))]),
        out_shape=[jax.ShapeDtypeStruct((B_, 1, 2 * LANE), F32),
                   jax.ShapeDtypeStruct((B_, 2, LANE, lw), F32)],
        compiler_params=_cparams(),
        name="nsa_sample",
    )(page_table, qa8, kvb.reshape(B_, 1, 6 * LANE), wnew.reshape(B_, 1, 2 * LANE), gates8,
      *([cache_fm] * (G * N_PAGES)), *([cache_ab] * (G * N_PAGES)), win_fm, bias, gkc, ov, e)
    rows = rows_t.reshape(4, N_KV, HEAD_DIM, B_).transpose(3, 0, 1, 2)[:, None]
    return (y.reshape(B_, 1, N_HEADS * HEAD_DIM), rows,
            wout.reshape(B_, 2, N_KV, HEAD_DIM, lw).transpose(0, 4, 1, 2, 3))


MIX_CHUNK = 512
HALO = 16
YM_W = POOL_W + RG_W + SC_W


def _expm1(x):
    p = jnp.full_like(x, 1.0 / 3628800.0)
    for c in (1.0 / 362880.0, 1.0 / 40320.0, 1.0 / 5040.0, 1.0 / 720.0, 1.0 / 120.0, 1.0 / 24.0, 1.0 / 6.0, 0.5, 1.0):
        p = p * x + c
    return jnp.where(jnp.abs(x) < 0.25, p * x, jnp.exp(x) - 1.0)


def _softplus(x):
    return jnp.maximum(x, 0.0) + jnp.log1p(jnp.exp(-jnp.abs(x)))


def _gelu_tanh(x):
    return 0.5 * x * (1.0 + jnp.tanh(np.sqrt(2.0 / np.pi).astype(np.float32) * (x + 0.044715 * (x * x * x))))


def _rg_coeffs(xc, wa, ba, wx, bx, lam):
    r = jax.nn.sigmoid(_mm(xc, wa) + ba)
    ig = jax.nn.sigmoid(_mm(xc, wx) + bx)
    log_a = (-RG_C * r) * _softplus(-lam)
    return jnp.exp(log_a), jnp.sqrt(-_expm1(2.0 * log_a)) * (ig * xc)


def _pool_select(s2, s4, s8, s16):
    lane = lax.broadcasted_iota(jnp.int32, s2.shape, 1)
    return jnp.where(lane < POOL_GROUP, s2, jnp.where(lane < 2 * POOL_GROUP, s4,
                                                      jnp.where(lane < 3 * POOL_GROUP, s8, s16)))


def _pool_count(pos, shape):
    lane = lax.broadcasted_iota(jnp.int32, shape, 1)
    win = jnp.left_shift(2, lane // POOL_GROUP)
    return jnp.minimum(win, pos + 1).astype(F32)


def _mixers_prompt_body(pu_ref, rx_ref, rg_ref, z_ref, bg_ref, cg_ref, pw_ref, ps_ref, cw_ref, cb_ref, wa_ref, ba_ref,
                        wx_ref, bx_ref, lam_ref, scw_ref, scb_ref, ym_ref, tails_ref, hlast_ref, halo, hcar):
    c = pl.program_id(1)
    tc = pu_ref.shape[0]

    @pl.when(c == 0)
    def _():
        halo[...] = jnp.zeros_like(halo)
        hcar[...] = jnp.zeros_like(hcar)

    pu, rx = pu_ref[...], rx_ref[...]
    u = cg_ref[...] * z_ref[...]
    ext = [jnp.concatenate([halo[i], v], axis=0) for i, v in enumerate((pu, rx, u))]

    def back(e, k):
        return pltpu.roll(e, k, 0)

    s2 = ext[0] + back(ext[0], 1)
    s4 = s2 + back(s2, 2)
    s8 = s4 + back(s4, 4)
    s16 = s8 + back(s8, 8)
    tot = _pool_select(s2, s4, s8, s16)[HALO:]
    pos = c * tc + lax.broadcasted_iota(jnp.int32, (tc, POOL_W), 0)
    d = tot / _pool_count(pos, (tc, POOL_W)) - pu
    ym_ref[:, 0:POOL_W] = _mm(d, pw_ref[...]) * ps_ref[...]

    cw = cw_ref[...]
    xc = cb_ref[...] + cw[RG_CONV - 1:RG_CONV] * rx
    for k in range(1, RG_CONV):
        xc = xc + cw[RG_CONV - 1 - k:RG_CONV - k] * back(ext[1], k)[HALO:]
    a, b = _rg_coeffs(xc, wa_ref[...], ba_ref[...], wx_ref[...], bx_ref[...], lam_ref[...])
    row = lax.broadcasted_iota(jnp.int32, (tc, RG_W), 0)
    k = 1
    while k < tc:
        a_prev = jnp.where(row < k, 1.0, pltpu.roll(a, k, 0))
        b_prev = jnp.where(row < k, 0.0, pltpu.roll(b, k, 0))
        b = a * b_prev + b
        a = a * a_prev
        k *= 2
    h = a * hcar[0:1] + b
    hcar[...] = jnp.broadcast_to(h[tc - 1:tc], hcar.shape)
    hlast_ref[0] = jnp.broadcast_to(h[tc - 1:tc], hcar.shape)
    ym_ref[:, POOL_W:POOL_W + RG_W] = h * _gelu_tanh(rg_ref[...])

    scw = scw_ref[...]
    v = scb_ref[...] + scw[SC_CONV - 1:SC_CONV] * u
    for k in range(1, SC_CONV):
        v = v + scw[SC_CONV - 1 - k:SC_CONV - k] * back(ext[2], k)[HALO:]
    ym_ref[:, POOL_W + RG_W:YM_W] = bg_ref[...] * v

    for i, val in enumerate((pu, rx, u)):
        halo[i] = val[tc - HALO:]
        tails_ref[0, i] = val[tc - HALO:]


def _block_diag(w):
    g, n, _ = w.shape
    return jnp.einsum('gij,gh->gihj', w, jnp.eye(g, dtype=w.dtype)).reshape(g * n, g * n)


def _mixer_params(lw):
    row = lambda a: a.reshape(1, -1)
    return [_block_diag(lw['pool_w']).astype(MXU_DTYPE), row(lw['pool_scale']), lw['rg_conv_w'], row(lw['rg_conv_b']),
            _block_diag(lw['rg_w_a']).astype(MXU_DTYPE), row(lw['rg_b_a']),
            _block_diag(lw['rg_w_x']).astype(MXU_DTYPE), row(lw['rg_b_x']), row(lw['rg_lambda']),
            lw['sc_conv_w'], row(lw['sc_conv_b'])]


def _proj_col_specs(rows, index):
    cols = (COL_POOL, COL_RX, COL_RGATE, COL_SC, COL_SC + SC_W, COL_SC + 2 * SC_W)
    return [pl.BlockSpec((rows, GROUP_W), functools.partial(index, col // GROUP_W)) for col in cols]


def mixers_prompt(proj, lw, B_, S):
    tc = min(MIX_CHUNK, S)
    nc = S // tc
    params = _mixer_params(lw)
    fixed = lambda a: pl.BlockSpec(a.shape, lambda b, c: (0,) * a.ndim)
    return pl.pallas_call(
        _mixers_prompt_body,
        grid=(B_, nc),
        in_specs=_proj_col_specs(tc, lambda col, b, c: (b * nc + c, col)) + [fixed(a) for a in params],
        out_specs=[pl.BlockSpec((tc, YM_W), lambda b, c: (b * nc + c, 0)),
                   pl.BlockSpec((1, 3, HALO, GROUP_W), lambda b, c: (b, 0, 0, 0)),
                   pl.BlockSpec((1, 8, RG_W), lambda b, c: (b, 0, 0))],
        out_shape=[jax.ShapeDtypeStruct((B_ * S, YM_W), F32),
                   jax.ShapeDtypeStruct((B_, 3, HALO, GROUP_W), F32),
                   jax.ShapeDtypeStruct((B_, 8, RG_W), F32)],
        scratch_shapes=[pltpu.VMEM((3, HALO, GROUP_W), F32), pltpu.VMEM((8, RG_W), F32)],
        compiler_params=_cparams(2),
        name="mixers_prompt",
    )(*([proj] * 6), *params)


def _mixers_sample_body(pos0, pu_ref, rx_ref, rg_ref, z_ref, bg_ref, cg_ref, pp_ref, rp_ref, h0_ref, sp_ref, pw_ref,
                        ps_ref, cw_ref, cb_ref, wa_ref, ba_ref, wx_ref, bx_ref, lam_ref, scw_ref, scb_ref,
                        ym_ref, pn_ref, rn_ref, hn_ref, sn_ref):
    pu, rx = pu_ref[...], rx_ref[...]
    u = cg_ref[...] * z_ref[...]
    run, sums = pu, {}
    for k in range(1, POOL_KEEP + 1):
        run = run + pp_ref[POOL_KEEP - k]
        sums[k + 1] = run
    tot = _pool_select(*(sums[w] for w in POOL_WINDOWS))
    d = tot / _pool_count(pos0, pu.shape) - pu
    ym_ref[:, 0:POOL_W] = _mm(d, pw_ref[...]) * ps_ref[...]
    for k in range(POOL_KEEP - 1):
        pn_ref[k] = pp_ref[k + 1]
    pn_ref[POOL_KEEP - 1] = pu

    cw = cw_ref[...]
    xc = cb_ref[...] + cw[RG_CONV - 1:RG_CONV] * rx
    for k in range(RG_CONV - 1):
        xc = xc + cw[k:k + 1] * rp_ref[k]
    a, b = _rg_coeffs(xc, wa_ref[...], ba_ref[...], wx_ref[...], bx_ref[...], lam_ref[...])
    h = b + a * h0_ref[...]
    hn_ref[...] = h
    ym_ref[:, POOL_W:POOL_W + RG_W] = h * _gelu_tanh(rg_ref[...])
    for k in range(RG_CONV - 2):
        rn_ref[k] = rp_ref[k + 1]
    rn_ref[RG_CONV - 2] = rx

    scw = scw_ref[...]
    v = scb_ref[...] + scw[SC_CONV - 1:SC_CONV] * u
    for k in range(SC_CONV - 1):
        v = v + scw[k:k + 1] * sp_ref[k]
    ym_ref[:, POOL_W + RG_W:YM_W] = bg_ref[...] * v
    for k in range(SC_CONV - 2):
        sn_ref[k] = sp_ref[k + 1]
    sn_ref[SC_CONV - 2] = u


def mixers_sample(proj, lw, pos0, pool_prev, rgc_prev, h0, sc_prev):
    B_ = proj.shape[0]
    params = _mixer_params(lw)
    states = [pool_prev.transpose(1, 0, 2), rgc_prev.transpose(1, 0, 2), h0, sc_prev.transpose(1, 0, 2)]
    full = lambda a: pl.BlockSpec(a.shape, lambda i: (0,) * a.ndim)
    ym, pn, rn, hn, sn = pl.pallas_call(
        functools.partial(_mixers_sample_body, pos0),
        grid=(1,),
        in_specs=_proj_col_specs(B_, lambda col, i: (0, col)) + [full(a) for a in states] + [full(a) for a in params],
        out_specs=[pl.BlockSpec((B_, YM_W), lambda i: (0, 0))] + [full(a) for a in states],
        out_shape=[jax.ShapeDtypeStruct((B_, YM_W), F32)] + [jax.ShapeDtypeStruct(a.shape, F32) for a in states],
        compiler_params=_cparams(),
        name="mixers_sample",
    )(*([proj] * 6), *states, *params)
    return ym, pn.transpose(1, 0, 2), rn.transpose(1, 0, 2), hn, sn.transpose(1, 0, 2)


ROUTE_W = LANE
GROUP_LANE0 = N_EXPERTS
MOE_TILE_PROMPT = 256
MOE_TILE_SAMPLE = 32
COMBINE_TILE = 256
FETCH_GROUPS = 8
FETCH_BUFS = 3


def _rms(x, g):
    return x * lax.rsqrt(jnp.mean(x * x, axis=-1, keepdims=True) + EPS) * g


def _mix_out_router_body(ym_ref, yn_ref, x_ref, og_ref, wo_ref, gf_ref, wr_ref, br_ref, tri_ref, x2_ref, xn_ref,
                         route_ref, cnt_ref, cnt_sc):
    og = og_ref[...]
    groups = (ym_ref[:, 0:POOL_W], ym_ref[:, POOL_W:POOL_W + RG_W], yn_ref[...], ym_ref[:, POOL_W + RG_W:YM_W])
    yn = jnp.concatenate([_rms(y, og[:, i * GROUP_W:(i + 1) * GROUP_W]) for i, y in enumerate(groups)], axis=1)
    x2 = x_ref[...] + _mm(yn, wo_ref[...])
    x2_ref[...] = x2
    xn = _rms(x2, gf_ref[...])
    bits = lax.bitcast_convert_type(xn.astype(jnp.bfloat16).astype(F32), jnp.uint32)
    half = xn.shape[1] // 2
    xn_ref[...] = (bits[:, half:] & jnp.uint32(0xFFFF0000)) | (bits[:, :half] >> 16)
    logits = _mm(xn, wr_ref[...]) + br_ref[...]
    lane = lax.broadcasted_iota(jnp.int32, logits.shape, 1)
    is_grp = (lane >= GROUP_LANE0) & (lane < GROUP_LANE0 + N_GROUPS)
    grp = jnp.where(is_grp, logits, -jnp.inf)
    gmax = jnp.max(grp, axis=-1, keepdims=True)
    gsel = jnp.min(jnp.where(grp == gmax, lane - GROUP_LANE0, N_GROUPS), axis=-1, keepdims=True)
    p_group = 1.0 / jnp.sum(jnp.where(is_grp, jnp.exp(logits - gmax), 0.0), axis=-1, keepdims=True)
    le = jnp.where((lane < N_EXPERTS) & (lane // EXP_PER_GROUP == gsel), logits, -jnp.inf)
    m1 = jnp.max(le, axis=-1, keepdims=True)
    i1 = jnp.min(jnp.where(le == m1, lane, LANE), axis=-1, keepdims=True)
    le2 = jnp.where(lane == i1, -jnp.inf, le)
    m2 = jnp.max(le2, axis=-1, keepdims=True)
    i2 = jnp.min(jnp.where(le2 == m2, lane, LANE), axis=-1, keepdims=True)
    e2 = jnp.exp(m2 - m1)
    g1 = p_group * (1.0 / (1.0 + e2))
    g2 = p_group * (e2 / (1.0 + e2))
    @pl.when(pl.program_id(0) == 0)
    def _():
        cnt_sc[...] = jnp.zeros_like(cnt_sc)

    oh = jnp.where((lane == i1) | (lane == i2), 1.0, 0.0)
    before = cnt_sc[0:1] + _mm(tri_ref[...], oh)
    r1 = jnp.sum(jnp.where(lane == i1, before, 0.0), axis=-1, keepdims=True)
    r2 = jnp.sum(jnp.where(lane == i2, before, 0.0), axis=-1, keepdims=True)
    total = cnt_sc[0:1] + jnp.sum(oh, axis=0, keepdims=True)
    cnt_sc[...] = jnp.broadcast_to(total, cnt_sc.shape)
    cnt_ref[...] = jnp.broadcast_to(total, cnt_ref.shape)
    vals = (i1.astype(F32), i2.astype(F32), g1, g2, r1, r2)
    route = jnp.zeros(logits.shape, F32)
    for k, v in enumerate(vals):
        route = jnp.where(lane == k, v, route)
    route_ref[...] = route


def mix_out_router(ym, y_nsa, x2d, lw):
    T, D = x2d.shape
    tm = min(256, T)
    wr = jnp.concatenate([lw['router_expert_w'], lw['router_group_w'],
                          jnp.zeros((D, ROUTE_W - N_EXPERTS - N_GROUPS), F32)], axis=1).astype(MXU_DTYPE)
    br = jnp.concatenate([lw['router_expert_b'], lw['router_group_b'],
                          jnp.zeros((ROUTE_W - N_EXPERTS - N_GROUPS,), F32)]).reshape(1, ROUTE_W)
    row = lambda i: (i, 0)
    fixed = lambda i: (0, 0)
    tri = jnp.asarray(np.tril(np.ones((tm, tm), np.float32), -1), MXU_DTYPE)
    return pl.pallas_call(
        _mix_out_router_body,
        grid=(T // tm,),
        in_specs=[pl.BlockSpec((tm, YM_W), row), pl.BlockSpec((tm, GROUP_W), row), pl.BlockSpec((tm, D), row),
                  pl.BlockSpec((1, MIX_W), fixed),
                  pl.BlockSpec((MIX_W, D), fixed), pl.BlockSpec((1, D), fixed), pl.BlockSpec((D, ROUTE_W), fixed),
                  pl.BlockSpec((1, ROUTE_W), fixed), pl.BlockSpec((tm, tm), fixed)],
        out_specs=[pl.BlockSpec((tm, D), row), pl.BlockSpec((tm, D // 2), row), pl.BlockSpec((tm, ROUTE_W), row),
                   pl.BlockSpec((8, ROUTE_W), fixed)],
        out_shape=[jax.ShapeDtypeStruct((T, D), F32), jax.ShapeDtypeStruct((T, D // 2), jnp.uint32),
                   jax.ShapeDtypeStruct((T, ROUTE_W), F32), jax.ShapeDtypeStruct((8, ROUTE_W), F32)],
        scratch_shapes=[pltpu.VMEM((8, ROUTE_W), F32)],
        compiler_params=_cparams(),
        name="mix_out_router",
    )(ym, y_nsa, x2d, lw['mix_out_g'].reshape(1, MIX_W), lw['w_out'].astype(MXU_DTYPE),
      lw['norm_ffn_g'].reshape(1, D), wr, br, tri)


def moe_schedule(route, counts, tile):
    T = route.shape[0]
    M = T * TOP_E
    fe = route[:, 0:TOP_E].astype(jnp.int32).reshape(M)
    rank = route[:, 4:4 + TOP_E].astype(jnp.int32).reshape(M)
    counts = counts.astype(jnp.int32)
    padded = (counts + tile - 1) // tile * tile
    pad_end = jnp.cumsum(padded)
    dest = ((pad_end - padded)[fe] + rank).astype(jnp.int32)
    n_blk = -(-M // tile) + N_EXPERTS
    tok = jnp.arange(M, dtype=jnp.int32) // TOP_E
    buf_tok = jnp.zeros((n_blk * tile,), jnp.int32).at[dest].set(tok)
    blk_exp = jnp.minimum(jnp.sum(pad_end[None, :] <= (jnp.arange(n_blk, dtype=jnp.int32) * tile)[:, None], axis=1),
                          N_EXPERTS - 1).astype(jnp.int32)
    n_used = (pad_end[-1:] // tile).astype(jnp.int32)
    return buf_tok, blk_exp, n_used, dest


def _moe_ffn_body(tile, tok_ref, bexp_ref, nused_ref, x_hbm, wgu_ref, wdn_ref, y_ref, xg, sem, wgu_bf, wdn_bf):
    j = pl.program_id(0)
    n = nused_ref[0]

    def gather(blk, slot):
        def body(r, c):
            t = tok_ref[blk * tile + r]
            pltpu.make_async_copy(x_hbm.at[pl.ds(t, 1)], xg.at[slot, pl.ds(r, 1)], sem.at[slot]).start()
            return c
        lax.fori_loop(0, tile, body, 0, unroll=8)

    @pl.when((j == 0) & (n > 0))
    def _():
        gather(0, 0)
        gather(jnp.minimum(1, n - 1), 1)

    def wait_block(slot):
        pltpu.make_async_copy(x_hbm.at[pl.ds(0, tile)], xg.at[slot], sem.at[slot]).wait()

    @pl.when(j < n)
    def _():
        slot = j % FETCH_BUFS
        nxt = jnp.minimum(j + 2, n - 1)
        dst = (j + 2) % FETCH_BUFS

        def fetch_group(g):
            per = tile // FETCH_GROUPS
            for r in range(g * per, (g + 1) * per):
                t = tok_ref[nxt * tile + r]
                pltpu.make_async_copy(x_hbm.at[pl.ds(t, 1)], xg.at[dst, pl.ds(r, 1)], sem.at[dst]).start()

        @pl.when((j == 0) | (bexp_ref[j] != bexp_ref[jnp.maximum(j - 1, 0)]))
        def _():
            wgu_bf[...] = wgu_ref[0].astype(wgu_bf.dtype)
            wdn_bf[...] = wdn_ref[0].astype(wdn_bf.dtype)

        wait_block(slot)
        u = xg[slot]
        dk = u.shape[1]
        x_lo = lax.bitcast_convert_type(u << 16, F32)
        x_hi = lax.bitcast_convert_type(u & jnp.uint32(0xFFFF0000), F32)
        half = FETCH_GROUPS // 2
        cg, cd = 2 * D_EXPERT // half, y_ref.shape[1] // half
        hs = []
        for c in range(half):
            fetch_group(c)
            hs.append(_mm(x_lo, wgu_bf[0:dk, c * cg:(c + 1) * cg]) + _mm(x_hi, wgu_bf[dk:2 * dk, c * cg:(c + 1) * cg]))
        h = jnp.concatenate(hs, axis=1)
        a, b = h[:, :D_EXPERT], h[:, D_EXPERT:]
        act = a * jax.nn.sigmoid(a) * b
        for c in range(half):
            fetch_group(half + c)
            y_ref[:, c * cd:(c + 1) * cd] = _mm(act, wdn_bf[:, c * cd:(c + 1) * cd])

        @pl.when(j + 1 >= n)
        def _():
            wait_block((j + 1) % FETCH_BUFS)
            wait_block(dst)

    @pl.when(j >= n)
    def _():
        y_ref[...] = jnp.zeros_like(y_ref)


def moe_ffn_pallas(xn, buf_tok, blk_exp, n_used, w_gu, w_down, tile):
    T, dk = xn.shape
    D = 2 * dk
    n_blk = blk_exp.shape[0]
    return pl.pallas_call(
        functools.partial(_moe_ffn_body, tile),
        grid_spec=pltpu.PrefetchScalarGridSpec(
            num_scalar_prefetch=3,
            grid=(n_blk,),
            in_specs=[pl.BlockSpec(memory_space=pl.ANY),
                      pl.BlockSpec((1, D, 2 * D_EXPERT), lambda j, tok, bexp, nu: (bexp[j], 0, 0)),
                      pl.BlockSpec((1, D_EXPERT, D), lambda j, tok, bexp, nu: (bexp[j], 0, 0))],
            out_specs=pl.BlockSpec((tile, D), lambda j, tok, bexp, nu: (j, 0)),
            scratch_shapes=[pltpu.VMEM((FETCH_BUFS, tile, dk), jnp.uint32), pltpu.SemaphoreType.DMA((FETCH_BUFS,)),
                            pltpu.VMEM((D, 2 * D_EXPERT), MXU_DTYPE), pltpu.VMEM((D_EXPERT, D), MXU_DTYPE)]),
        out_shape=jax.ShapeDtypeStruct((n_blk * tile, D), F32),
        compiler_params=_cparams(),
        name="moe_ffn",
    )(buf_tok, blk_exp, n_used, xn, w_gu, w_down)


def _moe_combine_body(tm, slots_ref, y_hbm, x2_ref, route_ref, o_ref, yb, sem):
    i = pl.program_id(0)
    nt = pl.num_programs(0)

    def gather(tile_i, buf):
        def body(r, c):
            for k in range(TOP_E):
                s = slots_ref[(tile_i * tm + r) * TOP_E + k]
                pltpu.make_async_copy(y_hbm.at[pl.ds(s, 1)], yb.at[buf, k, pl.ds(r, 1)], sem.at[buf]).start()
            return c
        lax.fori_loop(0, tm, body, 0, unroll=8)

    def wait_tile(b):
        for k in range(TOP_E):
            pltpu.make_async_copy(y_hbm.at[pl.ds(0, tm)], yb.at[b, k], sem.at[b]).wait()

    @pl.when(i == 0)
    def _():
        gather(0, 0)
        gather(jnp.minimum(1, nt - 1), 1)

    buf = i % FETCH_BUFS
    ahead = (i + 2) % FETCH_BUFS
    gather(jnp.minimum(i + 2, nt - 1), ahead)
    wait_tile(buf)
    r = route_ref[...]
    o_ref[...] = x2_ref[...] + (r[:, 2:3] * yb[buf, 0] + r[:, 3:4] * yb[buf, 1])

    @pl.when(i + 1 >= nt)
    def _():
        wait_tile((i + 1) % FETCH_BUFS)
        wait_tile(ahead)


def moe_combine_pallas(y, slots, x2, route):
    T, D = x2.shape
    tm = min(COMBINE_TILE, T)
    return pl.pallas_call(
        functools.partial(_moe_combine_body, tm),
        grid_spec=pltpu.PrefetchScalarGridSpec(
            num_scalar_prefetch=1,
            grid=(T // tm,),
            in_specs=[pl.BlockSpec(memory_space=pl.ANY),
                      pl.BlockSpec((tm, D), lambda i, s: (i, 0)),
                      pl.BlockSpec((tm, ROUTE_W), lambda i, s: (i, 0))],
            out_specs=pl.BlockSpec((tm, D), lambda i, s: (i, 0)),
            scratch_shapes=[pltpu.VMEM((FETCH_BUFS, TOP_E, tm, D), F32), pltpu.SemaphoreType.DMA((FETCH_BUFS,))]),
        out_shape=jax.ShapeDtypeStruct((T, D), F32),
        compiler_params=_cparams(),
        name="moe_combine",
    )(slots, y, x2, route)


def mix_out_moe(ym, y_nsa, x2d, lw, tile):
    x2, xn, route, counts = mix_out_router(ym, y_nsa, x2d, lw)
    buf_tok, blk_exp, n_used, slots = moe_schedule(route, counts[0, :N_EXPERTS], tile)
    y = moe_ffn_pallas(xn, buf_tok, blk_exp + lw['expert_base'], n_used, lw['exp_w_gu'], lw['exp_w_down'], tile)
    return moe_combine_pallas(y, slots, x2, route)


def split_cols(a, sizes):
    outs, o = [], 0
    for s in sizes:
        outs.append(a[..., o:o + s])
        o += s
    return outs


def layer_forward(x, pos0, lw, pool_prev, rgc_prev, rgh0, sc_prev, nsa_fn):
    B_, L, _ = x.shape
    w_perm = permute_w_in(lw['w_in']).astype(MXU_DTYPE)
    proj2d = norm_matmul(x.reshape(B_ * L, D_MODEL), lw['norm_mix_g'], w_perm)
    if pool_prev is None:
        ym, tails, hlast = mixers_prompt(proj2d, lw, B_, L)
        pool_new = tails[:, 0, HALO - POOL_KEEP:]
        rgc_new = tails[:, 1, HALO - (RG_CONV - 1):]
        sc_new = tails[:, 2, HALO - (SC_CONV - 1):]
        rgh_new = hlast[:, 0]
    else:
        ym, pool_new, rgc_new, rgh_new, sc_new = mixers_sample(proj2d, lw, pos0, pool_prev, rgc_prev, rgh0, sc_prev)
    y_nsa, nsa_rows, win_new = nsa_fn(proj2d, lw['nsa_phi'], lw['nsa_phi_b'], lw['nsa_qk_g'])
    x = mix_out_moe(ym, y_nsa.reshape(B_ * L, GROUP_W), x.reshape(B_ * L, D_MODEL), lw,
                    MOE_TILE_PROMPT if L > 1 else MOE_TILE_SAMPLE)
    return x.reshape(B_, L, D_MODEL), (nsa_rows, win_new, pool_new, rgc_new, rgh_new, sc_new)


def kernel(x_prompt, x_sample, cache_nsa, state_win_kv, state_pool, state_rg_conv, state_rg_h, state_sc_conv,
           page_table, norm_mix_g, w_in, pool_w, pool_scale, rg_conv_w, rg_conv_b, rg_w_a, rg_b_a, rg_w_x, rg_b_x,
           rg_lambda, nsa_phi, nsa_phi_b, nsa_qk_g, sc_conv_w, sc_conv_b, mix_out_g, w_out, norm_ffn_g,
           router_group_w, router_group_b, router_expert_w, router_expert_b, exp_w_gu, exp_w_down):
    past_len = page_table.shape[1] * cache_nsa.shape[2]
    xp, xs = x_prompt, x_sample
    cache3 = feature_major_pages(cache_nsa)
    win3 = state_win_kv.transpose(0, 1, 3, 4, 5, 2).reshape(DEPTH * state_win_kv.shape[1], 2, N_KV * HEAD_DIM,
                                                             state_win_kv.shape[2])
    cache_ab = cache_compress(cache3, nsa_phi)
    Bp = xp.shape[0]
    st_p, st_s = [], []
    for l in range(DEPTH):
        lw = dict(norm_mix_g=norm_mix_g[l], w_in=w_in[l], pool_w=pool_w[l], pool_scale=pool_scale[l],
                  rg_conv_w=rg_conv_w[l], rg_conv_b=rg_conv_b[l], rg_w_a=rg_w_a[l], rg_b_a=rg_b_a[l],
                  rg_w_x=rg_w_x[l], rg_b_x=rg_b_x[l], rg_lambda=rg_lambda[l], nsa_phi=nsa_phi[l],
                  nsa_phi_b=nsa_phi_b[l], nsa_qk_g=nsa_qk_g[l], sc_conv_w=sc_conv_w[l], sc_conv_b=sc_conv_b[l],
                  mix_out_g=mix_out_g[l], w_out=w_out[l], norm_ffn_g=norm_ffn_g[l],
                  router_group_w=router_group_w[l], router_group_b=router_group_b[l],
                  router_expert_w=router_expert_w[l], router_expert_b=router_expert_b[l],
                  exp_w_gu=exp_w_gu.reshape((DEPTH * N_EXPERTS,) + exp_w_gu.shape[2:]),
                  exp_w_down=exp_w_down.reshape((DEPTH * N_EXPERTS,) + exp_w_down.shape[2:]),
                  expert_base=l * N_EXPERTS)
        xp, sp = layer_forward(xp, 0, lw, None, None, None, None,
                               lambda p, phi, phi_b, g: nsa_prompt_pallas(p, Bp, xp.shape[1], phi, phi_b, g))
        xs, ss = layer_forward(xs, past_len, lw, state_pool[l], state_rg_conv[l], state_rg_h[l], state_sc_conv[l],
                               lambda p, phi, phi_b, g: nsa_sample_pallas(p, l, cache3, cache_ab, page_table, win3,
                                                                          phi_b, g))
        st_p.append(sp)
        st_s.append(ss)

    def stk(lst, i):
        return jnp.stack([s[i] for s in lst])

    return (xp, xs, stk(st_p, 0), stk(st_s, 0), stk(st_p, 1), stk(st_s, 1), stk(st_p, 2), stk(st_s, 2),
            stk(st_p, 3), stk(st_s, 3), stk(st_p, 4), stk(st_s, 4), stk(st_p, 5), stk(st_s, 5))
```
